```python
import math
import jax
import jax.numpy as jnp
from jax import lax
import numpy as np

D_MODEL = 1024
BATCH = 8
SEQ = 4096
DEPTH = 4

CHUNK = 64
Q_BLOCK = 128
EPS = 1e-6
MEM_LEN = 256

SSD_D_INNER = 1024
SSD_HEAD_DIM = 64
SSD_N_HEADS = SSD_D_INNER // SSD_HEAD_DIM
SSD_N_GROUPS = 4
SSD_HEADS_PER_GROUP = SSD_N_HEADS // SSD_N_GROUPS
SSD_D_STATE = 128
SSD_CONV = 4
SSD_XBC = SSD_D_INNER + 2 * SSD_N_GROUPS * SSD_D_STATE

CONV_D = 1024
CONV_K = 31

MLA_HEADS = 8
MLA_Q_RANK = 384
MLA_KV_RANK = 256
MLA_NOPE = 128
MLA_ROPE = 64
MLA_V = 128
MLA_QK = MLA_NOPE + MLA_ROPE
ROPE_THETA = 10000.0

N_BRANCH = 3
IN_SIZES = (SSD_D_INNER, SSD_XBC, SSD_N_HEADS, 2 * CONV_D, MLA_Q_RANK, MLA_KV_RANK + MLA_ROPE, N_BRANCH * D_MODEL)
IN_WIDTH = sum(IN_SIZES)

X_HEADS = 4
X_HEAD_DIM = D_MODEL // X_HEADS

FFN_HIDDEN = -(-(8 * D_MODEL) // (3 * 256)) * 256

kernel_name = 'hybrid_ssd_conformer_mla_block'


def split_cols(x, sizes):
    parts, start = [], 0
    for n in sizes:
        parts.append(x[..., start:start + n])
        start += n
    return parts


def rms_norm(x, g):
    xf = x.astype(jnp.float32)
    y = xf * lax.rsqrt(jnp.mean(xf * xf, axis=-1, keepdims=True) + EPS)
    return (y * g.astype(jnp.float32)).astype(x.dtype)


def layer_norm(x, g, b):
    xf = x.astype(jnp.float32)
    mu = jnp.mean(xf, axis=-1, keepdims=True)
    xc = xf - mu
    var = jnp.mean(xc * xc, axis=-1, keepdims=True)
    y = xc * lax.rsqrt(var + EPS) * g.astype(jnp.float32) + b.astype(jnp.float32)
    return y.astype(x.dtype)


def causal_depthwise_conv(x, w, b):
    k, c = w.shape
    y = lax.conv_general_dilated(x, w[:, None, :].astype(x.dtype), window_strides=(1,), padding=[(k - 1, 0)],
                                 dimension_numbers=('NWC', 'WIO', 'NWC'), feature_group_count=c)
    return y + b


def rope_tables(positions, dim):
    inv = ROPE_THETA ** (-jnp.arange(0, dim, 2, dtype=jnp.float32) / dim)
    ang = positions.astype(jnp.float32)[..., None] * inv
    return jnp.cos(ang), jnp.sin(ang)


def apply_rope(x, cos, sin):
    x1, x2 = jnp.split(x.astype(jnp.float32), 2, axis=-1)
    return jnp.concatenate([x1 * cos - x2 * sin, x2 * cos + x1 * sin], axis=-1).astype(x.dtype)


def segsum(a):
    l = a.shape[-1]
    cs = jnp.cumsum(a, axis=-1)
    seg = cs[..., :, None] - cs[..., None, :]
    mask = jnp.tril(jnp.ones((l, l), dtype=bool))
    return jnp.where(mask, seg, -jnp.inf)


def ssd_chunked_scan(xh, dt, A, Bg, Cg):
    b, s, G, R, P = xh.shape
    N = Bg.shape[-1]
    nc = s // CHUNK
    X = (xh * dt[..., None]).reshape(b, nc, CHUNK, G, R, P)
    a = (dt * A).reshape(b, nc, CHUNK, G, R).transpose(0, 3, 4, 1, 2)
    Bc = Bg.reshape(b, nc, CHUNK, G, N)
    Cc = Cg.reshape(b, nc, CHUNK, G, N)
    a_cs = jnp.cumsum(a, axis=-1)
    decay = jnp.exp(segsum(a))
    cb = jnp.einsum('bclgn,bcsgn->bgcls', Cc, Bc)
    y_diag = jnp.einsum('bgrcls,bcsgrp->bclgrp', cb[:, :, None] * decay, X)
    decay_states = jnp.exp(a_cs[..., -1:] - a_cs)
    states = jnp.einsum('bclgn,bgrcl,bclgrp->bcgrpn', Bc, decay_states, X)
    chunk_decay = jnp.exp(a_cs[..., -1])

    def step(h, inp):
        st, dec = inp
        return h * dec[..., None, None] + st, h

    h0 = jnp.zeros((b, G, R, P, N), dtype=X.dtype)
    _, prev = lax.scan(step, h0, (states.transpose(1, 0, 2, 3, 4, 5), chunk_decay.transpose(3, 0, 1, 2)))
    y_off = jnp.einsum('bclgn,cbgrpn,bgrcl->bclgrp', Cc, prev, jnp.exp(a_cs))
    return (y_diag + y_off).reshape(b, s, G, R, P)


def ssd_mixer(z, xbc, dt_raw, conv_w, conv_b, dt_bias, a_log, d_skip, norm_g, w_proj):
    b, s, _ = z.shape
    G, R, P, N = SSD_N_GROUPS, SSD_HEADS_PER_GROUP, SSD_HEAD_DIM, SSD_D_STATE
    xbc = jax.nn.silu(causal_depthwise_conv(xbc, conv_w, conv_b))
    xs, bm, cm = split_cols(xbc, (SSD_D_INNER, G * N, G * N))
    xh = xs.astype(jnp.float32).reshape(b, s, G, R, P)
    Bg = bm.astype(jnp.float32).reshape(b, s, G, N)
    Cg = cm.astype(jnp.float32).reshape(b, s, G, N)
    dt = jax.nn.softplus(dt_raw.astype(jnp.float32) + dt_bias.astype(jnp.float32)).reshape(b, s, G, R)
    A = -jnp.exp(a_log.astype(jnp.float32)).reshape(G, R)
    y = ssd_chunked_scan(xh, dt, A, Bg, Cg) + d_skip.astype(jnp.float32).reshape(G, R)[..., None] * xh
    y = y.reshape(b, s, SSD_D_INNER) * jax.nn.silu(z.astype(jnp.float32))
    yg = y.reshape(b, s, G, SSD_D_INNER // G)
    yg = yg * lax.rsqrt(jnp.mean(yg * yg, axis=-1, keepdims=True) + EPS)
    y = yg.reshape(b, s, SSD_D_INNER) * norm_g.astype(jnp.float32)
    return y.astype(z.dtype) @ w_proj


def conformer_conv_module(glu_in, dw_w, dw_b, ln_g, ln_b, w_pw):
    a, g = jnp.split(glu_in, 2, axis=-1)
    v = a * jax.nn.sigmoid(g)
    v = causal_depthwise_conv(v, dw_w, dw_b)
    v = jax.nn.silu(layer_norm(v, ln_g, ln_b))
    return v @ w_pw


def block_causal_attention(q, k, v):
    b, s, h, dq = q.shape
    dv = v.shape[-1]
    nb = s // Q_BLOCK
    scale = dq ** -0.5
    kt = k.transpose(0, 2, 1, 3)
    vt = v.transpose(0, 2, 1, 3)
    k_chunk = jnp.arange(s) // CHUNK
    qb = q.reshape(b, nb, Q_BLOCK, h, dq).transpose(1, 0, 3, 2, 4)

    def one_block(args):
        qblk, i = args
        q_chunk = (i * Q_BLOCK + jnp.arange(Q_BLOCK)) // CHUNK
        sc = jnp.einsum('bhqd,bhkd->bhqk', qblk, kt, preferred_element_type=jnp.float32) * scale
        sc = jnp.where(k_chunk[None, :] <= q_chunk[:, None], sc, -jnp.inf)
        p = jax.nn.softmax(sc, axis=-1)
        return jnp.einsum('bhqk,bhkd->bhqd', p.astype(v.dtype), vt)

    o = lax.map(one_block, (qb, jnp.arange(nb)))
    return o.transpose(1, 0, 3, 2, 4).reshape(b, s, h, dv)


def mla_mixer(q_lat, kv_lat, cos, sin, q_a_g, w_q_b, kv_a_g, w_kv_b, q_norm_g, k_norm_g, w_o):
    b, s, _ = q_lat.shape
    q = (rms_norm(q_lat, q_a_g) @ w_q_b).reshape(b, s, MLA_HEADS, MLA_QK)
    c_kv, k_rope = kv_lat[..., :MLA_KV_RANK], kv_lat[..., MLA_KV_RANK:]
    kv = (rms_norm(c_kv, kv_a_g) @ w_kv_b).reshape(b, s, MLA_HEADS, MLA_NOPE + MLA_V)
    k_nope, v = kv[..., :MLA_NOPE], kv[..., MLA_NOPE:]
    q_nope = rms_norm(q[..., :MLA_NOPE], q_norm_g[:MLA_NOPE])
    q_rope = apply_rope(rms_norm(q[..., MLA_NOPE:], q_norm_g[MLA_NOPE:]), cos[:, :, None], sin[:, :, None])
    k_nope = rms_norm(k_nope, k_norm_g[:MLA_NOPE])
    k_rope = apply_rope(rms_norm(k_rope, k_norm_g[MLA_NOPE:]), cos, sin)
    qf = jnp.concatenate([q_nope, q_rope], axis=-1)
    kf = jnp.concatenate([k_nope, jnp.broadcast_to(k_rope[:, :, None], (b, s, MLA_HEADS, MLA_ROPE))], axis=-1)
    o = block_causal_attention(qf, kf, v)
    return o.reshape(b, s, MLA_HEADS * MLA_V) @ w_o


def memory_cross_attention(h, mem_n, w_q, w_kv, q_norm_g, k_norm_g, w_o):
    b, s, _ = h.shape
    m = mem_n.shape[1]
    q = rms_norm((h @ w_q).reshape(b, s, X_HEADS, X_HEAD_DIM), q_norm_g)
    kv = (mem_n @ w_kv).reshape(b, m, 2, X_HEADS, X_HEAD_DIM)
    k = rms_norm(kv[:, :, 0], k_norm_g)
    v = kv[:, :, 1]
    sc = jnp.einsum('bqhd,bkhd->bhqk', q, k, preferred_element_type=jnp.float32) * (X_HEAD_DIM ** -0.5)
    p = jax.nn.softmax(sc, axis=-1)
    o = jnp.einsum('bhqk,bkhd->bqhd', p.astype(v.dtype), v)
    return o.reshape(b, s, D_MODEL) @ w_o


def swiglu_ffn(h, w_in, w_out):
    gate, up = jnp.split(h @ w_in, 2, axis=-1)
    return (jax.nn.silu(gate) * up) @ w_out


def _fwd_setup_inputs(seed: int = 0) -> dict:
    key = jax.random.key(seed)
    ks = iter(jax.random.split(key, 64))
    f32 = jnp.float32
    L = DEPTH

    def normal(shape, std):
        return jax.random.normal(next(ks), shape, f32) * std

    def dense(shape, fan_in, scale=1.0):
        return normal(shape, scale * fan_in ** -0.5)

    def gain(shape):
        return 1.0 + normal(shape, 0.02)

    def small(shape):
        return normal(shape, 0.02)

    out_scale = 0.5
    x = normal((BATCH, SEQ, D_MODEL), 1.0)
    mem = normal((BATCH, MEM_LEN, D_MODEL), 1.0)
    start = jax.random.randint(next(ks), (BATCH, 1), 0, 100000, dtype=jnp.int32)
    positions = start + jnp.arange(SEQ, dtype=jnp.int32)[None, :]
    dt0 = jnp.exp(jax.random.uniform(next(ks), (L, SSD_N_HEADS), f32, math.log(1e-3), math.log(1e-1)))
    ssd_dt_bias = dt0 + jnp.log(-jnp.expm1(-dt0))
    ssd_a_log = jnp.log(jax.random.uniform(next(ks), (L, SSD_N_HEADS), f32, 1.0, 16.0))
    return {
        'x': x,
        'mem': mem,
        'positions': positions,
        'mix_norm_g': gain((L, D_MODEL)),
        'w_in': dense((L, D_MODEL, IN_WIDTH), D_MODEL),
        'ssd_conv_w': dense((L, SSD_CONV, SSD_XBC), SSD_CONV),
        'ssd_conv_b': small((L, SSD_XBC)),
        'ssd_dt_bias': ssd_dt_bias,
        'ssd_a_log': ssd_a_log,
        'ssd_d': gain((L, SSD_N_HEADS)),
        'ssd_norm_g': gain((L, SSD_D_INNER)),
        'ssd_w_out': dense((L, SSD_D_INNER, D_MODEL), SSD_D_INNER),
        'conv_dw_w': dense((L, CONV_K, CONV_D), CONV_K),
        'conv_dw_b': small((L, CONV_D)),
        'conv_ln_g': gain((L, CONV_D)),
        'conv_ln_b': small((L, CONV_D)),
        'conv_w_out': dense((L, CONV_D, D_MODEL), CONV_D),
        'mla_q_a_g': gain((L, MLA_Q_RANK)),
        'mla_w_q_b': dense((L, MLA_Q_RANK, MLA_HEADS * MLA_QK), MLA_Q_RANK),
        'mla_kv_a_g': gain((L, MLA_KV_RANK)),
        'mla_w_kv_b': dense((L, MLA_KV_RANK, MLA_HEADS * (MLA_NOPE + MLA_V)), MLA_KV_RANK),
        'mla_q_norm_g': gain((L, MLA_QK)),
        'mla_k_norm_g': gain((L, MLA_QK)),
        'mla_w_o': dense((L, MLA_HEADS * MLA_V, D_MODEL), MLA_HEADS * MLA_V),
        'gate_b': small((L, N_BRANCH, D_MODEL)),
        'w_out': dense((L, D_MODEL, D_MODEL), D_MODEL, out_scale),
        'xattn_norm_g': gain((L, D_MODEL)),
        'mem_norm_g': gain((L, D_MODEL)),
        'xattn_w_q': dense((L, D_MODEL, D_MODEL), D_MODEL),
        'xattn_w_kv': dense((L, D_MODEL, 2 * D_MODEL), D_MODEL),
        'xattn_q_norm_g': gain((L, X_HEAD_DIM)),
        'xattn_k_norm_g': gain((L, X_HEAD_DIM)),
        'xattn_w_o': dense((L, D_MODEL, D_MODEL), D_MODEL, out_scale),
        'ffn_norm_g': gain((L, D_MODEL)),
        'ffn_w_in': dense((L, D_MODEL, 2 * FFN_HIDDEN), D_MODEL),
        'ffn_w_out': dense((L, FFN_HIDDEN, D_MODEL), FFN_HIDDEN, out_scale),
    }


def _fwd_reference(x, mem, positions, mix_norm_g, w_in, ssd_conv_w, ssd_conv_b, ssd_dt_bias, ssd_a_log, ssd_d,
              ssd_norm_g, ssd_w_out, conv_dw_w, conv_dw_b, conv_ln_g, conv_ln_b, conv_w_out, mla_q_a_g,
              mla_w_q_b, mla_kv_a_g, mla_w_kv_b, mla_q_norm_g, mla_k_norm_g, mla_w_o, gate_b, w_out,
              xattn_norm_g, mem_norm_g, xattn_w_q, xattn_w_kv, xattn_q_norm_g, xattn_k_norm_g, xattn_w_o,
              ffn_norm_g, ffn_w_in, ffn_w_out):
    b, s, _ = x.shape
    cos, sin = rope_tables(positions, MLA_ROPE)
    for l in range(DEPTH):
        u = rms_norm(x, mix_norm_g[l])
        z, xbc, dt_raw, glu_in, q_lat, kv_lat, gate_logits = split_cols(u @ w_in[l], IN_SIZES)
        y_ssd = ssd_mixer(z, xbc, dt_raw, ssd_conv_w[l], ssd_conv_b[l], ssd_dt_bias[l], ssd_a_log[l],
                          ssd_d[l], ssd_norm_g[l], ssd_w_out[l])
        y_conv = conformer_conv_module(glu_in, conv_dw_w[l], conv_dw_b[l], conv_ln_g[l], conv_ln_b[l],
                                       conv_w_out[l])
        y_mla = mla_mixer(q_lat, kv_lat, cos, sin, mla_q_a_g[l], mla_w_q_b[l], mla_kv_a_g[l], mla_w_kv_b[l],
                          mla_q_norm_g[l], mla_k_norm_g[l], mla_w_o[l])
        gates = jax.nn.sigmoid((gate_logits + gate_b[l].reshape(-1)).astype(jnp.float32))
        gates = gates.astype(x.dtype).reshape(b, s, N_BRANCH, D_MODEL)
        merged = gates[:, :, 0] * y_ssd + gates[:, :, 1] * y_conv + gates[:, :, 2] * y_mla
        x = x + merged @ w_out[l]
        x = x + memory_cross_attention(rms_norm(x, xattn_norm_g[l]), rms_norm(mem, mem_norm_g[l]),
                                       xattn_w_q[l], xattn_w_kv[l], xattn_q_norm_g[l], xattn_k_norm_g[l],
                                       xattn_w_o[l])
        x = x + swiglu_ffn(rms_norm(x, ffn_norm_g[l]), ffn_w_in[l], ffn_w_out[l])
    return x


import jax as _jax
import jax.numpy as _jnp

TWIN_FORMAT = 'train_step'
FWD_PARAMS = ['x', 'mem', 'positions', 'mix_norm_g', 'w_in', 'ssd_conv_w', 'ssd_conv_b', 'ssd_dt_bias', 'ssd_a_log', 'ssd_d', 'ssd_norm_g', 'ssd_w_out', 'conv_dw_w', 'conv_dw_b', 'conv_ln_g', 'conv_ln_b', 'conv_w_out', 'mla_q_a_g', 'mla_w_q_b', 'mla_kv_a_g', 'mla_w_kv_b', 'mla_q_norm_g', 'mla_k_norm_g', 'mla_w_o', 'gate_b', 'w_out', 'xattn_norm_g', 'mem_norm_g', 'xattn_w_q', 'xattn_w_kv', 'xattn_q_norm_g', 'xattn_k_norm_g', 'xattn_w_o', 'ffn_norm_g', 'ffn_w_in', 'ffn_w_out']
TWIN_WEIGHTS = ['mix_norm_g', 'w_in', 'ssd_conv_w', 'ssd_conv_b', 'ssd_dt_bias', 'ssd_a_log', 'ssd_d', 'ssd_norm_g', 'ssd_w_out', 'conv_dw_w', 'conv_dw_b', 'conv_ln_g', 'conv_ln_b', 'conv_w_out', 'mla_q_a_g', 'mla_w_q_b', 'mla_kv_a_g', 'mla_w_kv_b', 'mla_q_norm_g', 'mla_k_norm_g', 'mla_w_o', 'gate_b', 'w_out', 'xattn_norm_g', 'mem_norm_g', 'xattn_w_q', 'xattn_w_kv', 'xattn_q_norm_g', 'xattn_k_norm_g', 'xattn_w_o', 'ffn_norm_g', 'ffn_w_in', 'ffn_w_out']
TWIN_DIFF_INPUT = 'x'
TWIN_INPUTS = ['x', 'mem', 'positions', 'mix_norm_g', 'w_in', 'ssd_conv_w', 'ssd_conv_b', 'ssd_dt_bias', 'ssd_a_log', 'ssd_d', 'ssd_norm_g', 'ssd_w_out', 'conv_dw_w', 'conv_dw_b', 'conv_ln_g', 'conv_ln_b', 'conv_w_out', 'mla_q_a_g', 'mla_w_q_b', 'mla_kv_a_g', 'mla_w_kv_b', 'mla_q_norm_g', 'mla_k_norm_g', 'mla_w_o', 'gate_b', 'w_out', 'xattn_norm_g', 'mem_norm_g', 'xattn_w_q', 'xattn_w_kv', 'xattn_q_norm_g', 'xattn_k_norm_g', 'xattn_w_o', 'ffn_norm_g', 'ffn_w_in', 'ffn_w_out', 'loss_target', 'm_mix_norm_g', 'm_w_in', 'm_ssd_conv_w', 'm_ssd_conv_b', 'm_ssd_dt_bias', 'm_ssd_a_log', 'm_ssd_d', 'm_ssd_norm_g', 'm_ssd_w_out', 'm_conv_dw_w', 'm_conv_dw_b', 'm_conv_ln_g', 'm_conv_ln_b', 'm_conv_w_out', 'm_mla_q_a_g', 'm_mla_w_q_b', 'm_mla_kv_a_g', 'm_mla_w_kv_b', 'm_mla_q_norm_g', 'm_mla_k_norm_g', 'm_mla_w_o', 'm_gate_b', 'm_w_out', 'm_xattn_norm_g', 'm_mem_norm_g', 'm_xattn_w_q', 'm_xattn_w_kv', 'm_xattn_q_norm_g', 'm_xattn_k_norm_g', 'm_xattn_w_o', 'm_ffn_norm_g', 'm_ffn_w_in', 'm_ffn_w_out', 'v_mix_norm_g', 'v_w_in', 'v_ssd_conv_w', 'v_ssd_conv_b', 'v_ssd_dt_bias', 'v_ssd_a_log', 'v_ssd_d', 'v_ssd_norm_g', 'v_ssd_w_out', 'v_conv_dw_w', 'v_conv_dw_b', 'v_conv_ln_g', 'v_conv_ln_b', 'v_conv_w_out', 'v_mla_q_a_g', 'v_mla_w_q_b', 'v_mla_kv_a_g', 'v_mla_w_kv_b', 'v_mla_q_norm_g', 'v_mla_k_norm_g', 'v_mla_w_o', 'v_gate_b', 'v_w_out', 'v_xattn_norm_g', 'v_mem_norm_g', 'v_xattn_w_q', 'v_xattn_w_kv', 'v_xattn_q_norm_g', 'v_xattn_k_norm_g', 'v_xattn_w_o', 'v_ffn_norm_g', 'v_ffn_w_in', 'v_ffn_w_out']
TWIN_OUTPUTS = ['loss', 'grad_x', 'grad_mix_norm_g', 'grad_w_in', 'grad_ssd_conv_w', 'grad_ssd_conv_b', 'grad_ssd_dt_bias', 'grad_ssd_a_log', 'grad_ssd_d', 'grad_ssd_norm_g', 'grad_ssd_w_out', 'grad_conv_dw_w', 'grad_conv_dw_b', 'grad_conv_ln_g', 'grad_conv_ln_b', 'grad_conv_w_out', 'grad_mla_q_a_g', 'grad_mla_w_q_b', 'grad_mla_kv_a_g', 'grad_mla_w_kv_b', 'grad_mla_q_norm_g', 'grad_mla_k_norm_g', 'grad_mla_w_o', 'grad_gate_b', 'grad_w_out', 'grad_xattn_norm_g', 'grad_mem_norm_g', 'grad_xattn_w_q', 'grad_xattn_w_kv', 'grad_xattn_q_norm_g', 'grad_xattn_k_norm_g', 'grad_xattn_w_o', 'grad_ffn_norm_g', 'grad_ffn_w_in', 'grad_ffn_w_out', 'delta_mix_norm_g', 'delta_w_in', 'delta_ssd_conv_w', 'delta_ssd_conv_b', 'delta_ssd_dt_bias', 'delta_ssd_a_log', 'delta_ssd_d', 'delta_ssd_norm_g', 'delta_ssd_w_out', 'delta_conv_dw_w', 'delta_conv_dw_b', 'delta_conv_ln_g', 'delta_conv_ln_b', 'delta_conv_w_out', 'delta_mla_q_a_g', 'delta_mla_w_q_b', 'delta_mla_kv_a_g', 'delta_mla_w_kv_b', 'delta_mla_q_norm_g', 'delta_mla_k_norm_g', 'delta_mla_w_o', 'delta_gate_b', 'delta_w_out', 'delta_xattn_norm_g', 'delta_mem_norm_g', 'delta_xattn_w_q', 'delta_xattn_w_kv', 'delta_xattn_q_norm_g', 'delta_xattn_k_norm_g', 'delta_xattn_w_o', 'delta_ffn_norm_g', 'delta_ffn_w_in', 'delta_ffn_w_out', 'new_m_mix_norm_g', 'new_m_w_in', 'new_m_ssd_conv_w', 'new_m_ssd_conv_b', 'new_m_ssd_dt_bias', 'new_m_ssd_a_log', 'new_m_ssd_d', 'new_m_ssd_norm_g', 'new_m_ssd_w_out', 'new_m_conv_dw_w', 'new_m_conv_dw_b', 'new_m_conv_ln_g', 'new_m_conv_ln_b', 'new_m_conv_w_out', 'new_m_mla_q_a_g', 'new_m_mla_w_q_b', 'new_m_mla_kv_a_g', 'new_m_mla_w_kv_b', 'new_m_mla_q_norm_g', 'new_m_mla_k_norm_g', 'new_m_mla_w_o', 'new_m_gate_b', 'new_m_w_out', 'new_m_xattn_norm_g', 'new_m_mem_norm_g', 'new_m_xattn_w_q', 'new_m_xattn_w_kv', 'new_m_xattn_q_norm_g', 'new_m_xattn_k_norm_g', 'new_m_xattn_w_o', 'new_m_ffn_norm_g', 'new_m_ffn_w_in', 'new_m_ffn_w_out', 'new_v_mix_norm_g', 'new_v_w_in', 'new_v_ssd_conv_w', 'new_v_ssd_conv_b', 'new_v_ssd_dt_bias', 'new_v_ssd_a_log', 'new_v_ssd_d', 'new_v_ssd_norm_g', 'new_v_ssd_w_out', 'new_v_conv_dw_w', 'new_v_conv_dw_b', 'new_v_conv_ln_g', 'new_v_conv_ln_b', 'new_v_conv_w_out', 'new_v_mla_q_a_g', 'new_v_mla_w_q_b', 'new_v_mla_kv_a_g', 'new_v_mla_w_kv_b', 'new_v_mla_q_norm_g', 'new_v_mla_k_norm_g', 'new_v_mla_w_o', 'new_v_gate_b', 'new_v_w_out', 'new_v_xattn_norm_g', 'new_v_mem_norm_g', 'new_v_xattn_w_q', 'new_v_xattn_w_kv', 'new_v_xattn_q_norm_g', 'new_v_xattn_k_norm_g', 'new_v_xattn_w_o', 'new_v_ffn_norm_g', 'new_v_ffn_w_in', 'new_v_ffn_w_out']
TWIN_LEAF_KINDS = {'loss': 'loss', 'grad_x': 'grad_x', 'grad_mix_norm_g': 'grad_w', 'grad_w_in': 'grad_w', 'grad_ssd_conv_w': 'grad_w', 'grad_ssd_conv_b': 'grad_w', 'grad_ssd_dt_bias': 'grad_w', 'grad_ssd_a_log': 'grad_w', 'grad_ssd_d': 'grad_w', 'grad_ssd_norm_g': 'grad_w', 'grad_ssd_w_out': 'grad_w', 'grad_conv_dw_w': 'grad_w', 'grad_conv_dw_b': 'grad_w', 'grad_conv_ln_g': 'grad_w', 'grad_conv_ln_b': 'grad_w', 'grad_conv_w_out': 'grad_w', 'grad_mla_q_a_g': 'grad_w', 'grad_mla_w_q_b': 'grad_w', 'grad_mla_kv_a_g': 'grad_w', 'grad_mla_w_kv_b': 'grad_w', 'grad_mla_q_norm_g': 'grad_w', 'grad_mla_k_norm_g': 'grad_w', 'grad_mla_w_o': 'grad_w', 'grad_gate_b': 'grad_w', 'grad_w_out': 'grad_w', 'grad_xattn_norm_g': 'grad_w', 'grad_mem_norm_g': 'grad_w', 'grad_xattn_w_q': 'grad_w', 'grad_xattn_w_kv': 'grad_w', 'grad_xattn_q_norm_g': 'grad_w', 'grad_xattn_k_norm_g': 'grad_w', 'grad_xattn_w_o': 'grad_w', 'grad_ffn_norm_g': 'grad_w', 'grad_ffn_w_in': 'grad_w', 'grad_ffn_w_out': 'grad_w', 'delta_mix_norm_g': 'delta_w', 'delta_w_in': 'delta_w', 'delta_ssd_conv_w': 'delta_w', 'delta_ssd_conv_b': 'delta_w', 'delta_ssd_dt_bias': 'delta_w', 'delta_ssd_a_log': 'delta_w', 'delta_ssd_d': 'delta_w', 'delta_ssd_norm_g': 'delta_w', 'delta_ssd_w_out': 'delta_w', 'delta_conv_dw_w': 'delta_w', 'delta_conv_dw_b': 'delta_w', 'delta_conv_ln_g': 'delta_w', 'delta_conv_ln_b': 'delta_w', 'delta_conv_w_out': 'delta_w', 'delta_mla_q_a_g': 'delta_w', 'delta_mla_w_q_b': 'delta_w', 'delta_mla_kv_a_g': 'delta_w', 'delta_mla_w_kv_b': 'delta_w', 'delta_mla_q_norm_g': 'delta_w', 'delta_mla_k_norm_g': 'delta_w', 'delta_mla_w_o': 'delta_w', 'delta_gate_b': 'delta_w', 'delta_w_out': 'delta_w', 'delta_xattn_norm_g': 'delta_w', 'delta_mem_norm_g': 'delta_w', 'delta_xattn_w_q': 'delta_w', 'delta_xattn_w_kv': 'delta_w', 'delta_xattn_q_norm_g': 'delta_w', 'delta_xattn_k_norm_g': 'delta_w', 'delta_xattn_w_o': 'delta_w', 'delta_ffn_norm_g': 'delta_w', 'delta_ffn_w_in': 'delta_w', 'delta_ffn_w_out': 'delta_w', 'new_m_mix_norm_g': 'new_m', 'new_m_w_in': 'new_m', 'new_m_ssd_conv_w': 'new_m', 'new_m_ssd_conv_b': 'new_m', 'new_m_ssd_dt_bias': 'new_m', 'new_m_ssd_a_log': 'new_m', 'new_m_ssd_d': 'new_m', 'new_m_ssd_norm_g': 'new_m', 'new_m_ssd_w_out': 'new_m', 'new_m_conv_dw_w': 'new_m', 'new_m_conv_dw_b': 'new_m', 'new_m_conv_ln_g': 'new_m', 'new_m_conv_ln_b': 'new_m', 'new_m_conv_w_out': 'new_m', 'new_m_mla_q_a_g': 'new_m', 'new_m_mla_w_q_b': 'new_m', 'new_m_mla_kv_a_g': 'new_m', 'new_m_mla_w_kv_b': 'new_m', 'new_m_mla_q_norm_g': 'new_m', 'new_m_mla_k_norm_g': 'new_m', 'new_m_mla_w_o': 'new_m', 'new_m_gate_b': 'new_m', 'new_m_w_out': 'new_m', 'new_m_xattn_norm_g': 'new_m', 'new_m_mem_norm_g': 'new_m', 'new_m_xattn_w_q': 'new_m', 'new_m_xattn_w_kv': 'new_m', 'new_m_xattn_q_norm_g': 'new_m', 'new_m_xattn_k_norm_g': 'new_m', 'new_m_xattn_w_o': 'new_m', 'new_m_ffn_norm_g': 'new_m', 'new_m_ffn_w_in': 'new_m', 'new_m_ffn_w_out': 'new_m', 'new_v_mix_norm_g': 'new_v', 'new_v_w_in': 'new_v', 'new_v_ssd_conv_w': 'new_v', 'new_v_ssd_conv_b': 'new_v', 'new_v_ssd_dt_bias': 'new_v', 'new_v_ssd_a_log': 'new_v', 'new_v_ssd_d': 'new_v', 'new_v_ssd_norm_g': 'new_v', 'new_v_ssd_w_out': 'new_v', 'new_v_conv_dw_w': 'new_v', 'new_v_conv_dw_b': 'new_v', 'new_v_conv_ln_g': 'new_v', 'new_v_conv_ln_b': 'new_v', 'new_v_conv_w_out': 'new_v', 'new_v_mla_q_a_g': 'new_v', 'new_v_mla_w_q_b': 'new_v', 'new_v_mla_kv_a_g': 'new_v', 'new_v_mla_w_kv_b': 'new_v', 'new_v_mla_q_norm_g': 'new_v', 'new_v_mla_k_norm_g': 'new_v', 'new_v_mla_w_o': 'new_v', 'new_v_gate_b': 'new_v', 'new_v_w_out': 'new_v', 'new_v_xattn_norm_g': 'new_v', 'new_v_mem_norm_g': 'new_v', 'new_v_xattn_w_q': 'new_v', 'new_v_xattn_w_kv': 'new_v', 'new_v_xattn_q_norm_g': 'new_v', 'new_v_xattn_k_norm_g': 'new_v', 'new_v_xattn_w_o': 'new_v', 'new_v_ffn_norm_g': 'new_v', 'new_v_ffn_w_in': 'new_v', 'new_v_ffn_w_out': 'new_v'}


def _forward(args):
    return _fwd_reference(*[args[k] for k in FWD_PARAMS])


def _output_shape():
    out = _jax.eval_shape(lambda: _forward(_fwd_setup_inputs(0)))
    return out.shape, out.dtype

N_MICROBATCH = 1
ADAM_LR = 0.001
ADAM_B1 = 0.9
ADAM_B2 = 0.999
ADAM_EPS = 1e-08
ADAM_WD = 0.01
ADAM_STEP = 10
PER_EXAMPLE_BATCH_AXIS = {'x': 0, 'mem': 0, 'positions': 0, 'loss_target': 0}
SHARED_INPUTS = []
_WEIGHT_DTYPES = {'mix_norm_g': _jnp.float32, 'w_in': _jnp.float32, 'ssd_conv_w': _jnp.float32, 'ssd_conv_b': _jnp.float32, 'ssd_dt_bias': _jnp.float32, 'ssd_a_log': _jnp.float32, 'ssd_d': _jnp.float32, 'ssd_norm_g': _jnp.float32, 'ssd_w_out': _jnp.float32, 'conv_dw_w': _jnp.float32, 'conv_dw_b': _jnp.float32, 'conv_ln_g': _jnp.float32, 'conv_ln_b': _jnp.float32, 'conv_w_out': _jnp.float32, 'mla_q_a_g': _jnp.float32, 'mla_w_q_b': _jnp.float32, 'mla_kv_a_g': _jnp.float32, 'mla_w_kv_b': _jnp.float32, 'mla_q_norm_g': _jnp.float32, 'mla_k_norm_g': _jnp.float32, 'mla_w_o': _jnp.float32, 'gate_b': _jnp.float32, 'w_out': _jnp.float32, 'xattn_norm_g': _jnp.float32, 'mem_norm_g': _jnp.float32, 'xattn_w_q': _jnp.float32, 'xattn_w_kv': _jnp.float32, 'xattn_q_norm_g': _jnp.float32, 'xattn_k_norm_g': _jnp.float32, 'xattn_w_o': _jnp.float32, 'ffn_norm_g': _jnp.float32, 'ffn_w_in': _jnp.float32, 'ffn_w_out': _jnp.float32}
MOMENT_SCALE = {'mix_norm_g': 3.214351e-01, 'w_in': 7.675080e-02, 'ssd_conv_w': 1.404021e-01, 'ssd_conv_b': 4.627545e-01, 'ssd_dt_bias': 2.670302e-01, 'ssd_a_log': 1.005464e+00, 'ssd_d': 1.285920e+00, 'ssd_norm_g': 3.176414e+00, 'ssd_w_out': 2.747400e-01, 'conv_dw_w': 9.827498e-02, 'conv_dw_b': 1.361454e+00, 'conv_ln_g': 1.168003e+00, 'conv_ln_b': 1.042264e+00, 'conv_w_out': 2.773940e-01, 'mla_q_a_g': 2.686392e-02, 'mla_w_q_b': 1.348197e-02, 'mla_kv_a_g': 2.074032e-01, 'mla_w_kv_b': 6.124684e-02, 'mla_q_norm_g': 5.814932e-02, 'mla_k_norm_g': 5.828812e-02, 'mla_w_o': 9.040143e-02, 'gate_b': 5.011596e-01, 'w_out': 7.112320e-01, 'xattn_norm_g': 2.694323e-02, 'mem_norm_g': 1.034238e-01, 'xattn_w_q': 2.696378e-02, 'xattn_w_kv': 5.297537e-02, 'xattn_q_norm_g': 2.839841e-01, 'xattn_k_norm_g': 2.833526e-01, 'xattn_w_o': 1.462312e-01, 'ffn_norm_g': 6.081327e+00, 'ffn_w_in': 8.877163e-02, 'ffn_w_out': 2.874490e-01}


def _to_microbatches(a, axis):
    t = _jnp.moveaxis(a, axis, 0)
    t = t.reshape((N_MICROBATCH, t.shape[0] // N_MICROBATCH) + t.shape[1:])
    return _jnp.moveaxis(t, 1, axis + 1)


def setup_inputs(seed: int = 0) -> dict:
    inp = _fwd_setup_inputs(seed)
    key = _jax.random.fold_in(_jax.random.key(seed), 7919)
    shape, _ = _output_shape()
    out = dict(inp)
    out["loss_target"] = _jax.random.normal(_jax.random.fold_in(key, 0), shape, _jnp.float32)
    for i, name in enumerate(TWIN_WEIGHTS):
        w = inp[name].astype(_jnp.float32)
        if MOMENT_SCALE is None:
            s = _jnp.sqrt(_jnp.mean(_jnp.square(w)) + 1e-30)
        else:
            s = MOMENT_SCALE[name]
        km, kv = _jax.random.split(_jax.random.fold_in(key, i + 1))
        out[name] = w
        out["m_" + name] = s * _jax.random.normal(km, w.shape, _jnp.float32)
        out["v_" + name] = (s * s) * _jax.random.uniform(kv, w.shape, _jnp.float32, 0.5, 1.5)
    if N_MICROBATCH > 1:
        for name, axis in PER_EXAMPLE_BATCH_AXIS.items():
            out[name] = _to_microbatches(out[name], axis)
    return {'x': out['x'], 'mem': out['mem'], 'positions': out['positions'], 'mix_norm_g': out['mix_norm_g'], 'w_in': out['w_in'], 'ssd_conv_w': out['ssd_conv_w'], 'ssd_conv_b': out['ssd_conv_b'], 'ssd_dt_bias': out['ssd_dt_bias'], 'ssd_a_log': out['ssd_a_log'], 'ssd_d': out['ssd_d'], 'ssd_norm_g': out['ssd_norm_g'], 'ssd_w_out': out['ssd_w_out'], 'conv_dw_w': out['conv_dw_w'], 'conv_dw_b': out['conv_dw_b'], 'conv_ln_g': out['conv_ln_g'], 'conv_ln_b': out['conv_ln_b'], 'conv_w_out': out['conv_w_out'], 'mla_q_a_g': out['mla_q_a_g'], 'mla_w_q_b': out['mla_w_q_b'], 'mla_kv_a_g': out['mla_kv_a_g'], 'mla_w_kv_b': out['mla_w_kv_b'], 'mla_q_norm_g': out['mla_q_norm_g'], 'mla_k_norm_g': out['mla_k_norm_g'], 'mla_w_o': out['mla_w_o'], 'gate_b': out['gate_b'], 'w_out': out['w_out'], 'xattn_norm_g': out['xattn_norm_g'], 'mem_norm_g': out['mem_norm_g'], 'xattn_w_q': out['xattn_w_q'], 'xattn_w_kv': out['xattn_w_kv'], 'xattn_q_norm_g': out['xattn_q_norm_g'], 'xattn_k_norm_g': out['xattn_k_norm_g'], 'xattn_w_o': out['xattn_w_o'], 'ffn_norm_g': out['ffn_norm_g'], 'ffn_w_in': out['ffn_w_in'], 'ffn_w_out': out['ffn_w_out'], 'loss_target': out['loss_target'], 'm_mix_norm_g': out['m_mix_norm_g'], 'm_w_in': out['m_w_in'], 'm_ssd_conv_w': out['m_ssd_conv_w'], 'm_ssd_conv_b': out['m_ssd_conv_b'], 'm_ssd_dt_bias': out['m_ssd_dt_bias'], 'm_ssd_a_log': out['m_ssd_a_log'], 'm_ssd_d': out['m_ssd_d'], 'm_ssd_norm_g': out['m_ssd_norm_g'], 'm_ssd_w_out': out['m_ssd_w_out'], 'm_conv_dw_w': out['m_conv_dw_w'], 'm_conv_dw_b': out['m_conv_dw_b'], 'm_conv_ln_g': out['m_conv_ln_g'], 'm_conv_ln_b': out['m_conv_ln_b'], 'm_conv_w_out': out['m_conv_w_out'], 'm_mla_q_a_g': out['m_mla_q_a_g'], 'm_mla_w_q_b': out['m_mla_w_q_b'], 'm_mla_kv_a_g': out['m_mla_kv_a_g'], 'm_mla_w_kv_b': out['m_mla_w_kv_b'], 'm_mla_q_norm_g': out['m_mla_q_norm_g'], 'm_mla_k_norm_g': out['m_mla_k_norm_g'], 'm_mla_w_o': out['m_mla_w_o'], 'm_gate_b': out['m_gate_b'], 'm_w_out': out['m_w_out'], 'm_xattn_norm_g': out['m_xattn_norm_g'], 'm_mem_norm_g': out['m_mem_norm_g'], 'm_xattn_w_q': out['m_xattn_w_q'], 'm_xattn_w_kv': out['m_xattn_w_kv'], 'm_xattn_q_norm_g': out['m_xattn_q_norm_g'], 'm_xattn_k_norm_g': out['m_xattn_k_norm_g'], 'm_xattn_w_o': out['m_xattn_w_o'], 'm_ffn_norm_g': out['m_ffn_norm_g'], 'm_ffn_w_in': out['m_ffn_w_in'], 'm_ffn_w_out': out['m_ffn_w_out'], 'v_mix_norm_g': out['v_mix_norm_g'], 'v_w_in': out['v_w_in'], 'v_ssd_conv_w': out['v_ssd_conv_w'], 'v_ssd_conv_b': out['v_ssd_conv_b'], 'v_ssd_dt_bias': out['v_ssd_dt_bias'], 'v_ssd_a_log': out['v_ssd_a_log'], 'v_ssd_d': out['v_ssd_d'], 'v_ssd_norm_g': out['v_ssd_norm_g'], 'v_ssd_w_out': out['v_ssd_w_out'], 'v_conv_dw_w': out['v_conv_dw_w'], 'v_conv_dw_b': out['v_conv_dw_b'], 'v_conv_ln_g': out['v_conv_ln_g'], 'v_conv_ln_b': out['v_conv_ln_b'], 'v_conv_w_out': out['v_conv_w_out'], 'v_mla_q_a_g': out['v_mla_q_a_g'], 'v_mla_w_q_b': out['v_mla_w_q_b'], 'v_mla_kv_a_g': out['v_mla_kv_a_g'], 'v_mla_w_kv_b': out['v_mla_w_kv_b'], 'v_mla_q_norm_g': out['v_mla_q_norm_g'], 'v_mla_k_norm_g': out['v_mla_k_norm_g'], 'v_mla_w_o': out['v_mla_w_o'], 'v_gate_b': out['v_gate_b'], 'v_w_out': out['v_w_out'], 'v_xattn_norm_g': out['v_xattn_norm_g'], 'v_mem_norm_g': out['v_mem_norm_g'], 'v_xattn_w_q': out['v_xattn_w_q'], 'v_xattn_w_kv': out['v_xattn_w_kv'], 'v_xattn_q_norm_g': out['v_xattn_q_norm_g'], 'v_xattn_k_norm_g': out['v_xattn_k_norm_g'], 'v_xattn_w_o': out['v_xattn_w_o'], 'v_ffn_norm_g': out['v_ffn_norm_g'], 'v_ffn_w_in': out['v_ffn_w_in'], 'v_ffn_w_out': out['v_ffn_w_out']}


def _loss(weights, diff, rest, loss_target):
    with _jax.named_scope("forward"):
        args = {**rest, TWIN_DIFF_INPUT: diff, **{k: w.astype(_WEIGHT_DTYPES[k]) for k, w in weights.items()}}
        y = _forward(args)
    with _jax.named_scope("loss_head"):
        err = _jnp.square(y.astype(_jnp.float32) - loss_target)
        return 0.5 * _jnp.sum(_jnp.mean(err, axis=-1)) if err.ndim else 0.5 * err


def _adamw(w, g, m, v):
    m = ADAM_B1 * m + (1.0 - ADAM_B1) * g
    v = ADAM_B2 * v + (1.0 - ADAM_B2) * _jnp.square(g)
    m_hat = m / (1.0 - ADAM_B1 ** ADAM_STEP)
    v_hat = v / (1.0 - ADAM_B2 ** ADAM_STEP)
    delta = -ADAM_LR * (m_hat / (_jnp.sqrt(v_hat) + ADAM_EPS) + ADAM_WD * w)
    return delta, m, v


def reference(x, mem, positions, mix_norm_g, w_in, ssd_conv_w, ssd_conv_b, ssd_dt_bias, ssd_a_log, ssd_d, ssd_norm_g, ssd_w_out, conv_dw_w, conv_dw_b, conv_ln_g, conv_ln_b, conv_w_out, mla_q_a_g, mla_w_q_b, mla_kv_a_g, mla_w_kv_b, mla_q_norm_g, mla_k_norm_g, mla_w_o, gate_b, w_out, xattn_norm_g, mem_norm_g, xattn_w_q, xattn_w_kv, xattn_q_norm_g, xattn_k_norm_g, xattn_w_o, ffn_norm_g, ffn_w_in, ffn_w_out, loss_target, m_mix_norm_g, m_w_in, m_ssd_conv_w, m_ssd_conv_b, m_ssd_dt_bias, m_ssd_a_log, m_ssd_d, m_ssd_norm_g, m_ssd_w_out, m_conv_dw_w, m_conv_dw_b, m_conv_ln_g, m_conv_ln_b, m_conv_w_out, m_mla_q_a_g, m_mla_w_q_b, m_mla_kv_a_g, m_mla_w_kv_b, m_mla_q_norm_g, m_mla_k_norm_g, m_mla_w_o, m_gate_b, m_w_out, m_xattn_norm_g, m_mem_norm_g, m_xattn_w_q, m_xattn_w_kv, m_xattn_q_norm_g, m_xattn_k_norm_g, m_xattn_w_o, m_ffn_norm_g, m_ffn_w_in, m_ffn_w_out, v_mix_norm_g, v_w_in, v_ssd_conv_w, v_ssd_conv_b, v_ssd_dt_bias, v_ssd_a_log, v_ssd_d, v_ssd_norm_g, v_ssd_w_out, v_conv_dw_w, v_conv_dw_b, v_conv_ln_g, v_conv_ln_b, v_conv_w_out, v_mla_q_a_g, v_mla_w_q_b, v_mla_kv_a_g, v_mla_w_kv_b, v_mla_q_norm_g, v_mla_k_norm_g, v_mla_w_o, v_gate_b, v_w_out, v_xattn_norm_g, v_mem_norm_g, v_xattn_w_q, v_xattn_w_kv, v_xattn_q_norm_g, v_xattn_k_norm_g, v_xattn_w_o, v_ffn_norm_g, v_ffn_w_in, v_ffn_w_out):
    given = dict(x=x, mem=mem, positions=positions, mix_norm_g=mix_norm_g, w_in=w_in, ssd_conv_w=ssd_conv_w, ssd_conv_b=ssd_conv_b, ssd_dt_bias=ssd_dt_bias, ssd_a_log=ssd_a_log, ssd_d=ssd_d, ssd_norm_g=ssd_norm_g, ssd_w_out=ssd_w_out, conv_dw_w=conv_dw_w, conv_dw_b=conv_dw_b, conv_ln_g=conv_ln_g, conv_ln_b=conv_ln_b, conv_w_out=conv_w_out, mla_q_a_g=mla_q_a_g, mla_w_q_b=mla_w_q_b, mla_kv_a_g=mla_kv_a_g, mla_w_kv_b=mla_w_kv_b, mla_q_norm_g=mla_q_norm_g, mla_k_norm_g=mla_k_norm_g, mla_w_o=mla_w_o, gate_b=gate_b, w_out=w_out, xattn_norm_g=xattn_norm_g, mem_norm_g=mem_norm_g, xattn_w_q=xattn_w_q, xattn_w_kv=xattn_w_kv, xattn_q_norm_g=xattn_q_norm_g, xattn_k_norm_g=xattn_k_norm_g, xattn_w_o=xattn_w_o, ffn_norm_g=ffn_norm_g, ffn_w_in=ffn_w_in, ffn_w_out=ffn_w_out, loss_target=loss_target, m_mix_norm_g=m_mix_norm_g, m_w_in=m_w_in, m_ssd_conv_w=m_ssd_conv_w, m_ssd_conv_b=m_ssd_conv_b, m_ssd_dt_bias=m_ssd_dt_bias, m_ssd_a_log=m_ssd_a_log, m_ssd_d=m_ssd_d, m_ssd_norm_g=m_ssd_norm_g, m_ssd_w_out=m_ssd_w_out, m_conv_dw_w=m_conv_dw_w, m_conv_dw_b=m_conv_dw_b, m_conv_ln_g=m_conv_ln_g, m_conv_ln_b=m_conv_ln_b, m_conv_w_out=m_conv_w_out, m_mla_q_a_g=m_mla_q_a_g, m_mla_w_q_b=m_mla_w_q_b, m_mla_kv_a_g=m_mla_kv_a_g, m_mla_w_kv_b=m_mla_w_kv_b, m_mla_q_norm_g=m_mla_q_norm_g, m_mla_k_norm_g=m_mla_k_norm_g, m_mla_w_o=m_mla_w_o, m_gate_b=m_gate_b, m_w_out=m_w_out, m_xattn_norm_g=m_xattn_norm_g, m_mem_norm_g=m_mem_norm_g, m_xattn_w_q=m_xattn_w_q, m_xattn_w_kv=m_xattn_w_kv, m_xattn_q_norm_g=m_xattn_q_norm_g, m_xattn_k_norm_g=m_xattn_k_norm_g, m_xattn_w_o=m_xattn_w_o, m_ffn_norm_g=m_ffn_norm_g, m_ffn_w_in=m_ffn_w_in, m_ffn_w_out=m_ffn_w_out, v_mix_norm_g=v_mix_norm_g, v_w_in=v_w_in, v_ssd_conv_w=v_ssd_conv_w, v_ssd_conv_b=v_ssd_conv_b, v_ssd_dt_bias=v_ssd_dt_bias, v_ssd_a_log=v_ssd_a_log, v_ssd_d=v_ssd_d, v_ssd_norm_g=v_ssd_norm_g, v_ssd_w_out=v_ssd_w_out, v_conv_dw_w=v_conv_dw_w, v_conv_dw_b=v_conv_dw_b, v_conv_ln_g=v_conv_ln_g, v_conv_ln_b=v_conv_ln_b, v_conv_w_out=v_conv_w_out, v_mla_q_a_g=v_mla_q_a_g, v_mla_w_q_b=v_mla_w_q_b, v_mla_kv_a_g=v_mla_kv_a_g, v_mla_w_kv_b=v_mla_w_kv_b, v_mla_q_norm_g=v_mla_q_norm_g, v_mla_k_norm_g=v_mla_k_norm_g, v_mla_w_o=v_mla_w_o, v_gate_b=v_gate_b, v_w_out=v_w_out, v_xattn_norm_g=v_xattn_norm_g, v_mem_norm_g=v_mem_norm_g, v_xattn_w_q=v_xattn_w_q, v_xattn_w_kv=v_xattn_w_kv, v_xattn_q_norm_g=v_xattn_q_norm_g, v_xattn_k_norm_g=v_xattn_k_norm_g, v_xattn_w_o=v_xattn_w_o, v_ffn_norm_g=v_ffn_norm_g, v_ffn_w_in=v_ffn_w_in, v_ffn_w_out=v_ffn_w_out)
    weights = {n: given[n] for n in TWIN_WEIGHTS}
    shared = {n: given[n] for n in SHARED_INPUTS}
    per_example = {n: given[n] for n in ['x', 'mem', 'positions']}
    grad_fn = _jax.value_and_grad(_loss, argnums=(0, 1))

    def one_microbatch(ex, loss_target):
        ex = dict(ex)
        diff = ex.pop(TWIN_DIFF_INPUT)
        return grad_fn(weights, diff, {**shared, **ex}, loss_target)

    if N_MICROBATCH == 1:
        loss, (grad_w, grad_x) = one_microbatch(per_example, given["loss_target"])
    else:
        def body(carry, xs):
            loss_sum, grad_sum = carry
            l_k, (gw_k, gx_k) = one_microbatch(xs[0], xs[1])
            with _jax.named_scope("update"):
                return (loss_sum + l_k, _jax.tree.map(_jnp.add, grad_sum, gw_k)), gx_k

        init = (_jnp.zeros((), _jnp.float32), _jax.tree.map(_jnp.zeros_like, weights))
        (loss, grad_w), grad_x = _jax.lax.scan(body, init, (per_example, given["loss_target"]))
    with _jax.named_scope("update"):
        delta_w, new_m, new_v = {}, {}, {}
        for n in TWIN_WEIGHTS:
            delta_w[n], new_m[n], new_v[n] = _adamw(weights[n], grad_w[n], given["m_" + n], given["v_" + n])
    return (loss, grad_x, *[grad_w[n] for n in TWIN_WEIGHTS], *[delta_w[n] for n in TWIN_WEIGHTS],
            *[new_m[n] for n in TWIN_WEIGHTS], *[new_v[n] for n in TWIN_WEIGHTS])
```

```python
import functools
import math

import jax
import jax.numpy as jnp
import numpy as np
from jax import lax
from jax.experimental import pallas as pl
from jax.experimental.pallas import tpu as pltpu

F32 = jnp.float32
BF16 = jnp.bfloat16
HI = lax.Precision.HIGHEST
MESH = pl.DeviceIdType.MESH

EPS = 1e-6
CHUNK = 64
SSD_HEADS = 16
SSD_GROUPS = 4
SSD_P = 64
SSD_N = 128
MLA_HEADS = 8
MLA_NOPE = 128
MLA_ROPE = 64
MLA_V = 128
MLA_HP = 256
X_HEADS = 4
ROPE_THETA = 10000.0
ADAM_LR, ADAM_B1, ADAM_B2, ADAM_EPS, ADAM_WD, ADAM_STEP = 0.001, 0.9, 0.999, 1e-08, 0.01, 10
LANE = 128
NEG = -1e30
VMEM_MB = 1024 * 1024


def _pick(n, cap, mult=128):
    if n <= cap:
        return n
    d = (cap // mult) * mult
    while d >= mult:
        if n % d == 0:
            return d
        d -= mult
    return n


def _cparams(sem, mb=40):
    return pltpu.CompilerParams(dimension_semantics=sem, vmem_limit_bytes=mb * VMEM_MB)


def _sigmoid(x):
    return 1.0 / (1.0 + jnp.exp(-x))


def _silu(x):
    return x * _sigmoid(x)


def _dsilu(x):
    s = _sigmoid(x)
    return s * (1.0 + x * (1.0 - s))


def _softplus(x):
    return jnp.maximum(x, 0.0) + jnp.log(1.0 + jnp.exp(-jnp.abs(x)))


def _matmul(a, b, *, ta=False, tb=False, out_dtype=F32, add=None, name):
    if ta:
        kdim, m = a.shape
    else:
        m, kdim = a.shape
    if tb:
        n, kb = b.shape
    else:
        kb, n = b.shape
    assert kb == kdim, (a.shape, b.shape, ta, tb)
    tm = _pick(m, 512, 128 if ta else 8)
    tn = _pick(n, 1024 if n <= 1024 else 512, 128)
    tk = _pick(kdim, 2048, 128)
    nk = kdim // tk
    dims = (((0 if ta else 1,), (1 if tb else 0,)), ((), ()))

    has_add = add is not None

    def body(a_ref, b_ref, *rest):
        if has_add:
            add_ref, o_ref = rest[0], rest[1]
            acc = rest[2:]
        else:
            o_ref = rest[0]
            acc = rest[1:]
        part = lax.dot_general(a_ref[...].astype(BF16), b_ref[...].astype(BF16), dims,
                               preferred_element_type=F32)

        def finish(total):
            if has_add:
                total = total + add_ref[...]
            o_ref[...] = total.astype(o_ref.dtype)

        if nk == 1:
            finish(part)
        else:
            acc_ref, = acc
            k = pl.program_id(2)

            @pl.when(k == 0)
            def _():
                acc_ref[...] = part

            @pl.when(k > 0)
            def _():
                acc_ref[...] += part

            @pl.when(k == nk - 1)
            def _():
                finish(acc_ref[...])

    a_spec = pl.BlockSpec((tk, tm), lambda i, j, k: (k, i)) if ta else pl.BlockSpec((tm, tk), lambda i, j, k: (i, k))
    b_spec = pl.BlockSpec((tn, tk), lambda i, j, k: (j, k)) if tb else pl.BlockSpec((tk, tn), lambda i, j, k: (k, j))
    o_spec = pl.BlockSpec((tm, tn), lambda i, j, k: (i, j))
    return pl.pallas_call(
        body, name=name, grid=(m // tm, n // tn, nk),
        in_specs=[a_spec, b_spec] + ([o_spec] if has_add else []),
        out_specs=o_spec,
        out_shape=jax.ShapeDtypeStruct((m, n), out_dtype),
        scratch_shapes=[] if nk == 1 else [pltpu.VMEM((tm, tn), F32)],
        compiler_params=_cparams(("parallel", "parallel", "arbitrary"), 48),
    )(*([a, b] + ([add] if has_add else [])))


def _rms_fwd(x, g, *, out_dtype, name):
    r, w = x.shape
    tr = _pick(r, 512, 8)

    def body(x_ref, g_ref, o_ref):
        xv = x_ref[...]
        rstd = lax.rsqrt(jnp.mean(xv * xv, axis=-1, keepdims=True) + EPS)
        o_ref[...] = (xv * rstd * g_ref[...]).astype(o_ref.dtype)

    return pl.pallas_call(
        body, name=name, grid=(r // tr,),
        in_specs=[pl.BlockSpec((tr, w), lambda i: (i, 0)), pl.BlockSpec((1, w), lambda i: (0, 0))],
        out_specs=pl.BlockSpec((tr, w), lambda i: (i, 0)),
        out_shape=jax.ShapeDtypeStruct((r, w), out_dtype),
        compiler_params=_cparams(("parallel",)),
    )(x, g)


def _rms_bwd(x, g, dy, *, dx_dtype, name, add=None):
    r, w = x.shape
    tr = _pick(r, 512, 8)
    has_add = add is not None

    def body(x_ref, g_ref, dy_ref, *rest):
        if has_add:
            add_ref, dx_ref, dg_ref = rest
        else:
            dx_ref, dg_ref = rest
        xv = x_ref[...]
        dyv = dy_ref[...].astype(F32)
        rstd = lax.rsqrt(jnp.mean(xv * xv, axis=-1, keepdims=True) + EPS)
        xh = xv * rstd
        dyg = dyv * g_ref[...]
        dx = rstd * (dyg - xh * jnp.mean(dyg * xh, axis=-1, keepdims=True))
        if has_add:
            dx = dx + add_ref[...]
        dx_ref[...] = dx.astype(dx_ref.dtype)
        part = jnp.sum(dyv * xh, axis=0, keepdims=True)

        @pl.when(pl.program_id(0) == 0)
        def _():
            dg_ref[...] = part

        @pl.when(pl.program_id(0) > 0)
        def _():
            dg_ref[...] += part

    row = pl.BlockSpec((tr, w), lambda i: (i, 0))
    vec = pl.BlockSpec((1, w), lambda i: (0, 0))
    ins = [x, g, dy] + ([add] if has_add else [])
    return pl.pallas_call(
        body, name=name, grid=(r // tr,),
        in_specs=[row, vec, row] + ([row] if has_add else []),
        out_specs=[row, vec],
        out_shape=[jax.ShapeDtypeStruct((r, w), dx_dtype), jax.ShapeDtypeStruct((1, w), F32)],
        compiler_params=_cparams(("arbitrary",)),
    )(*ins)


def _ln_silu_fwd(x, g, b, *, name):
    r, w = x.shape
    tr = _pick(r, 512, 8)

    def body(x_ref, g_ref, b_ref, o_ref):
        xv = x_ref[...]
        mu = jnp.mean(xv, axis=-1, keepdims=True)
        xc = xv - mu
        rstd = lax.rsqrt(jnp.mean(xc * xc, axis=-1, keepdims=True) + EPS)
        o_ref[...] = _silu(xc * rstd * g_ref[...] + b_ref[...]).astype(o_ref.dtype)

    row = pl.BlockSpec((tr, w), lambda i: (i, 0))
    vec = pl.BlockSpec((1, w), lambda i: (0, 0))
    return pl.pallas_call(
        body, name=name, grid=(r // tr,), in_specs=[row, vec, vec], out_specs=row,
        out_shape=jax.ShapeDtypeStruct((r, w), BF16), compiler_params=_cparams(("parallel",)),
    )(x, g, b)


def _ln_silu_bwd(x, g, b, dy, *, name):
    r, w = x.shape
    tr = _pick(r, 512, 8)

    def body(x_ref, g_ref, b_ref, dy_ref, dx_ref, dg_ref, db_ref):
        xv = x_ref[...]
        mu = jnp.mean(xv, axis=-1, keepdims=True)
        xc = xv - mu
        rstd = lax.rsqrt(jnp.mean(xc * xc, axis=-1, keepdims=True) + EPS)
        xh = xc * rstd
        pre = xh * g_ref[...] + b_ref[...]
        dpre = dy_ref[...].astype(F32) * _dsilu(pre)
        dxh = dpre * g_ref[...]
        dx_ref[...] = rstd * (dxh - jnp.mean(dxh, axis=-1, keepdims=True)
                              - xh * jnp.mean(dxh * xh, axis=-1, keepdims=True))
        pg = jnp.sum(dpre * xh, axis=0, keepdims=True)
        pb = jnp.sum(dpre, axis=0, keepdims=True)

        @pl.when(pl.program_id(0) == 0)
        def _():
            dg_ref[...] = pg
            db_ref[...] = pb

        @pl.when(pl.program_id(0) > 0)
        def _():
            dg_ref[...] += pg
            db_ref[...] += pb

    row = pl.BlockSpec((tr, w), lambda i: (i, 0))
    vec = pl.BlockSpec((1, w), lambda i: (0, 0))
    return pl.pallas_call(
        body, name=name, grid=(r // tr,), in_specs=[row, vec, vec, row], out_specs=[row, vec, vec],
        out_shape=[jax.ShapeDtypeStruct((r, w), F32), jax.ShapeDtypeStruct((1, w), F32),
                   jax.ShapeDtypeStruct((1, w), F32)],
        compiler_params=_cparams(("arbitrary",)),
    )(x, g, b, dy)


CONV_PAD = 32
CONV_T = 256


def _conv_fwd(src, w, b, *, glu, act, name):
    s = src.shape[0]
    k, c = w.shape
    tc = LANE
    ncb = c // tc
    tt = _pick(s, CONV_T, 8)
    assert k - 1 <= CONV_PAD

    def body(*refs):
        if glu:
            a_ref, g_ref, w_ref, b_ref = refs[:4]
            outs = refs[4:-1]
        else:
            a_ref, w_ref, b_ref = refs[:3]
            outs = refs[3:-1]
        xp = refs[-1]
        xp[0:CONV_PAD, :] = jnp.zeros((CONV_PAD, tc), F32)
        if glu:
            xp[CONV_PAD:CONV_PAD + s, :] = a_ref[...] * _sigmoid(g_ref[...])
        else:
            xp[CONV_PAD:CONV_PAD + s, :] = a_ref[...]
        wv = w_ref[...]
        bv = b_ref[...]
        for t0 in range(0, s, tt):
            acc = jnp.zeros((tt, tc), F32) + bv
            for kk in range(k):
                off = CONV_PAD + t0 - (k - 1) + kk
                acc = acc + wv[kk:kk + 1, :] * xp[off:off + tt, :]
            outs[0][t0:t0 + tt, :] = acc
            if act:
                outs[1][t0:t0 + tt, :] = _silu(acc)

    col = pl.BlockSpec((s, tc), lambda j: (0, j))
    in_specs = [col, pl.BlockSpec((s, tc), lambda j: (0, j + ncb))] if glu else [col]
    in_specs += [pl.BlockSpec((k, tc), lambda j: (0, j)), pl.BlockSpec((1, tc), lambda j: (0, j))]
    n_out = 2 if act else 1
    res = pl.pallas_call(
        body, name=name, grid=(ncb,), in_specs=in_specs,
        out_specs=[col] * n_out,
        out_shape=[jax.ShapeDtypeStruct((s, c), F32)] * n_out,
        scratch_shapes=[pltpu.VMEM((CONV_PAD + s, tc), F32)],
        compiler_params=_cparams(("parallel",), 48),
    )(*([src, src] if glu else [src]), w, b)
    return res


def _conv_bwd(src, w, b, dy, pre, *, glu, act, name):
    s = src.shape[0]
    k, c = w.shape
    tc = LANE
    ncb = c // tc
    tt = _pick(s, CONV_T, 8)

    def body(*refs):
        i = 0
        a_ref = refs[i]; i += 1
        if glu:
            g_ref = refs[i]; i += 1
        w_ref = refs[i]; i += 1
        dy_ref = refs[i]; i += 1
        if act:
            pre_ref = refs[i]; i += 1
        da_ref = refs[i]; i += 1
        if glu:
            dg_ref = refs[i]; i += 1
        dw_ref = refs[i]; db_ref = refs[i + 1]
        xp, dp = refs[-2], refs[-1]
        xp[0:CONV_PAD, :] = jnp.zeros((CONV_PAD, tc), F32)
        if glu:
            xp[CONV_PAD:CONV_PAD + s, :] = a_ref[...] * _sigmoid(g_ref[...])
        else:
            xp[CONV_PAD:CONV_PAD + s, :] = a_ref[...]
        dp[s:s + CONV_PAD, :] = jnp.zeros((CONV_PAD, tc), F32)
        if act:
            dp[0:s, :] = dy_ref[...].astype(F32) * _dsilu(pre_ref[...])
        else:
            dp[0:s, :] = dy_ref[...].astype(F32)
        wv = w_ref[...]
        dws = [jnp.zeros((1, tc), F32) for _ in range(k)]
        dbs = jnp.zeros((1, tc), F32)
        for t0 in range(0, s, tt):
            acc = jnp.zeros((tt, tc), F32)
            dcur = dp[t0:t0 + tt, :]
            dbs = dbs + jnp.sum(dcur, axis=0, keepdims=True)
            for kk in range(k):
                acc = acc + wv[kk:kk + 1, :] * dp[t0 + (k - 1) - kk:t0 + (k - 1) - kk + tt, :]
                off = CONV_PAD + t0 - (k - 1) + kk
                dws[kk] = dws[kk] + jnp.sum(dcur * xp[off:off + tt, :], axis=0, keepdims=True)
            if glu:
                av = a_ref[t0:t0 + tt, :]
                sg = _sigmoid(g_ref[t0:t0 + tt, :])
                da_ref[t0:t0 + tt, :] = (acc * sg).astype(da_ref.dtype)
                dg_ref[t0:t0 + tt, :] = (acc * av * sg * (1.0 - sg)).astype(dg_ref.dtype)
            else:
                da_ref[t0:t0 + tt, :] = acc.astype(da_ref.dtype)
        for kk in range(k):
            dw_ref[kk:kk + 1, :] = dws[kk]
        db_ref[...] = dbs

    col = pl.BlockSpec((s, tc), lambda j: (0, j))
    in_specs = [col] + ([pl.BlockSpec((s, tc), lambda j: (0, j + ncb))] if glu else [])
    in_specs += [pl.BlockSpec((k, tc), lambda j: (0, j)), col] + ([col] if act else [])
    ins = ([src, src] if glu else [src]) + [w, dy] + ([pre] if act else [])
    out_specs = [col] + ([col] if glu else []) + [pl.BlockSpec((k, tc), lambda j: (0, j)),
                                                  pl.BlockSpec((1, tc), lambda j: (0, j))]
    out_shape = [jax.ShapeDtypeStruct((s, c), BF16)] * (2 if glu else 1) + [
        jax.ShapeDtypeStruct((k, c), F32), jax.ShapeDtypeStruct((1, c), F32)]
    return pl.pallas_call(
        body, name=name, grid=(ncb,), in_specs=in_specs, out_specs=out_specs, out_shape=out_shape,
        scratch_shapes=[pltpu.VMEM((CONV_PAD + s, tc), F32), pltpu.VMEM((s + CONV_PAD, tc), F32)],
        compiler_params=_cparams(("parallel",), 56),
    )(*ins)


def _ssd_consts():
    e = np.zeros((LANE, SSD_HEADS * SSD_P), np.float32)
    for h in range(SSD_HEADS):
        e[h, h * SSD_P:(h + 1) * SSD_P] = 1.0
    ltri = np.tril(np.ones((CHUNK, CHUNK), np.float32))
    return jnp.asarray(e), jnp.asarray(e.T.copy()), jnp.asarray(ltri), jnp.asarray(ltri.T.copy())


def _ssd_chunk_terms(dtr_ref, dtrt_ref, bias_ref, biast_ref, alog_ref, alogt_ref, e_ref, ltri_ref, utri_ref):
    a_neg = -jnp.exp(alog_ref[...])
    dt = _softplus(dtr_ref[...] + bias_ref[...])
    a = dt * a_neg
    s = jnp.dot(ltri_ref[...], a, precision=HI, preferred_element_type=F32)
    dtt = _softplus(dtrt_ref[...] + biast_ref[...])
    st = jnp.dot(dtt * (-jnp.exp(alogt_ref[...])), utri_ref[...], precision=HI, preferred_element_type=F32)
    ev = e_ref[...]
    s_x = jnp.dot(s, ev, precision=HI, preferred_element_type=F32)
    dt_x = jnp.dot(dt, ev, precision=HI, preferred_element_type=F32)
    return a_neg, dt, s, st, s_x, dt_x


def _ssd_decay(s, st, h, tril):
    seg = s[:, h:h + 1] - st[h:h + 1, :]
    return jnp.exp(jnp.where(tril, seg, NEG))


def _head_masks():
    lane = lax.broadcasted_iota(jnp.int32, (1, SSD_P * 4), 1)
    return [((lane >= r * SSD_P) & (lane < (r + 1) * SSD_P)).astype(F32) for r in range(4)]


def _ssd_scan_fwd(xbc, dtr, dtrt, bias, biast, alog, alogt, dskip, *, name):
    s_len = xbc.shape[0]
    nc = s_len // CHUNK
    e, et, ltri, utri = _ssd_consts()
    gw = SSD_P * 4

    def body(xs_ref, b_ref, c_ref, dtr_ref, dtrt_ref, bias_ref, biast_ref, alog_ref, alogt_ref, d_ref,
             e_ref, ltri_ref, utri_ref, y_ref, prev_ref, state):
        @pl.when(pl.program_id(0) == 0)
        def _():
            state[...] = jnp.zeros_like(state)

        a_neg, dt, s, st, s_x, dt_x = _ssd_chunk_terms(dtr_ref, dtrt_ref, bias_ref, biast_ref, alog_ref,
                                                       alogt_ref, e_ref, ltri_ref, utri_ref)
        s_last = s_x[CHUNK - 1:CHUNK, :]
        es_x = jnp.exp(s_x)
        w_x = jnp.exp(s_last - s_x)
        cd_x = jnp.exp(s_last)
        d_x = jnp.dot(jnp.broadcast_to(d_ref[...], (8, LANE)), e_ref[...], precision=HI,
                      preferred_element_type=F32)[0:1, :]
        xs = xs_ref[...]
        xv = xs * dt_x
        row = lax.broadcasted_iota(jnp.int32, (CHUNK, CHUNK), 0)
        colm = lax.broadcasted_iota(jnp.int32, (CHUNK, CHUNK), 1)
        tril = colm <= row
        masks = _head_masks()
        for g in range(SSD_GROUPS):
            gs = slice(g * gw, (g + 1) * gw)
            bg = b_ref[:, g * SSD_N:(g + 1) * SSD_N].astype(BF16)
            cg = c_ref[:, g * SSD_N:(g + 1) * SSD_N].astype(BF16)
            xg = xv[:, gs]
            hg = state[g]
            prev_ref[g] = hg
            cb = lax.dot_general(cg, bg, (((1,), (1,)), ((), ())), preferred_element_type=F32)
            yg = jnp.dot(cg, hg.astype(BF16), preferred_element_type=F32) * es_x[:, gs]
            for r in range(4):
                m = (cb * _ssd_decay(s, st, g * 4 + r, tril)).astype(BF16)
                yg = yg + jnp.dot(m, (xg * masks[r]).astype(BF16), preferred_element_type=F32)
            y_ref[:, gs] = yg + d_x[:, gs] * xs[:, gs]
            upd = lax.dot_general(bg, (xg * w_x[:, gs]).astype(BF16), (((0,), (0,)), ((), ())),
                                  preferred_element_type=F32)
            state[g] = hg * cd_x[:, gs] + upd

    nh = LANE
    chunk_row = lambda w, cb: pl.BlockSpec((CHUNK, w), lambda i, cb=cb: (i, cb))
    full = lambda a: pl.BlockSpec(a.shape, lambda i: (0,) * a.ndim)
    in_specs = [chunk_row(1024, 0), chunk_row(512, 2), chunk_row(512, 3), chunk_row(nh, 0),
                pl.BlockSpec((None, nh, CHUNK), lambda i: (i, 0, 0)),
                full(bias), full(biast), full(alog), full(alogt), full(dskip), full(e), full(ltri), full(utri)]
    return pl.pallas_call(
        body, name=name, grid=(nc,), in_specs=in_specs,
        out_specs=[pl.BlockSpec((CHUNK, 1024), lambda i: (i, 0)),
                   pl.BlockSpec((None, SSD_GROUPS, SSD_N, gw), lambda i: (i, 0, 0, 0))],
        out_shape=[jax.ShapeDtypeStruct((s_len, 1024), F32),
                   jax.ShapeDtypeStruct((nc, SSD_GROUPS, SSD_N, gw), F32)],
        scratch_shapes=[pltpu.VMEM((SSD_GROUPS, SSD_N, gw), F32)],
        compiler_params=_cparams(("arbitrary",)),
    )(xbc, xbc, xbc, dtr, dtrt, bias, biast, alog, alogt, dskip, e, ltri, utri)


def _ssd_scan_bwd(xbc, dtr, dtrt, bias, biast, alog, alogt, dskip, prev, dy, *, name):
    s_len = xbc.shape[0]
    nc = s_len // CHUNK
    e, et, ltri, utri = _ssd_consts()
    gw = SSD_P * 4

    def body(xs_ref, b_ref, c_ref, dtr_ref, dtrt_ref, bias_ref, biast_ref, alog_ref, alogt_ref, d_ref,
             e_ref, et_ref, ltri_ref, utri_ref, prev_ref, dy_ref,
             dxs_ref, db_ref, dc_ref, ddtr_ref, dalog_ref, dbias_ref, dd_ref, dstate):
        step = pl.program_id(0)

        @pl.when(step == 0)
        def _():
            dstate[...] = jnp.zeros_like(dstate)

        a_neg, dt, s, st, s_x, dt_x = _ssd_chunk_terms(dtr_ref, dtrt_ref, bias_ref, biast_ref, alog_ref,
                                                       alogt_ref, e_ref, ltri_ref, utri_ref)
        s_last = s_x[CHUNK - 1:CHUNK, :]
        es_x = jnp.exp(s_x)
        w_x = jnp.exp(s_last - s_x)
        cd_x = jnp.exp(s_last)
        d_x = jnp.dot(jnp.broadcast_to(d_ref[...], (8, LANE)), e_ref[...], precision=HI,
                      preferred_element_type=F32)[0:1, :]
        xs = xs_ref[...]
        xv = xs * dt_x
        dyv = dy_ref[...]
        row = lax.broadcasted_iota(jnp.int32, (CHUNK, CHUNK), 0)
        colm = lax.broadcasted_iota(jnp.int32, (CHUNK, CHUNK), 1)
        tril = colm <= row
        masks = _head_masks()
        is_last = lax.broadcasted_iota(jnp.int32, (CHUNK, 1), 0) == CHUNK - 1
        nt = (((1,), (1,)), ((), ()))
        tn = (((0,), (0,)), ((), ()))
        ds_parts, ddt_parts = [], []
        head_lane = lax.broadcasted_iota(jnp.int32, (CHUNK, LANE), 1)
        ones = jnp.ones((CHUNK, LANE), F32)
        ds_diag = jnp.zeros((CHUNK, LANE), F32)
        for g in range(SSD_GROUPS):
            gs = slice(g * gw, (g + 1) * gw)
            bg = b_ref[:, g * SSD_N:(g + 1) * SSD_N].astype(BF16)
            cg = c_ref[:, g * SSD_N:(g + 1) * SSD_N].astype(BF16)
            xg = xv[:, gs]
            xgb = xg.astype(BF16)
            hg = prev_ref[g]
            hgb = hg.astype(BF16)
            dsg = dstate[g]
            dsgb = dsg.astype(BF16)
            dyg = dyv[:, gs]
            dye = (dyg * es_x[:, gs]).astype(BF16)
            xw = (xg * w_x[:, gs]).astype(BF16)
            cb = lax.dot_general(cg, bg, nt, preferred_element_type=F32)
            dcg = lax.dot_general(dye, hgb, nt, preferred_element_type=F32)
            dh = lax.dot_general(cg, dye, tn, preferred_element_type=F32)
            bds = jnp.dot(bg, dsgb, preferred_element_type=F32)
            yoff = es_x[:, gs] * jnp.dot(cg, hgb, preferred_element_type=F32)
            dx_state = w_x[:, gs] * bds
            dbg = lax.dot_general(xw, dsgb, nt, preferred_element_type=F32)
            dxd = jnp.zeros((CHUNK, gw), F32)
            dcb = jnp.zeros((CHUNK, CHUNK), F32)
            for r in range(4):
                dec = _ssd_decay(s, st, g * 4 + r, tril)
                mf = cb * dec
                m = mf.astype(BF16)
                dym = (dyg * masks[r]).astype(BF16)
                dm = lax.dot_general(dym, xgb, nt, preferred_element_type=F32)
                dxd = dxd + lax.dot_general(m, dym, tn, preferred_element_type=F32)
                dcb = dcb + dm * dec
                qm = dm * mf
                rc = (jnp.sum(qm, axis=1, keepdims=True)
                      - lax.dot_general(qm, ones, tn, precision=HI, preferred_element_type=F32))
                ds_diag = ds_diag + jnp.where(head_lane == g * 4 + r, rc, 0.0)
            dcbb = dcb.astype(BF16)
            dcg = dcg + jnp.dot(dcbb, bg, preferred_element_type=F32)
            dbg = dbg + lax.dot_general(dcbb, cg, tn, preferred_element_type=F32)
            dxg = dxd + dx_state
            extra = (jnp.sum(xg * dx_state, axis=0, keepdims=True)
                     + cd_x[:, gs] * jnp.sum(dsg * hg, axis=0, keepdims=True))
            ds_parts.append(dyg * yoff - xg * dx_state + jnp.where(is_last, extra, 0.0))
            ddt_parts.append(dxg * xs[:, gs])
            dxs_ref[:, gs] = dxg * dt_x[:, gs] + d_x[:, gs] * dyg
            db_ref[:, g * SSD_N:(g + 1) * SSD_N] = dbg
            dc_ref[:, g * SSD_N:(g + 1) * SSD_N] = dcg
            dstate[g] = cd_x[:, gs] * dsg + dh
        etv = et_ref[...]
        ds = ds_diag + jnp.dot(jnp.concatenate(ds_parts, axis=1), etv, precision=HI, preferred_element_type=F32)
        da = jnp.dot(utri_ref[...], ds, precision=HI, preferred_element_type=F32)
        ddt = da * a_neg + jnp.dot(jnp.concatenate(ddt_parts, axis=1), etv, precision=HI,
                                   preferred_element_type=F32)
        ddtr = ddt * _sigmoid(dtr_ref[...] + bias_ref[...])
        ddtr_ref[...] = ddtr.astype(ddtr_ref.dtype)
        p_alog = jnp.sum(da * dt, axis=0, keepdims=True) * a_neg
        p_bias = jnp.sum(ddtr, axis=0, keepdims=True)
        p_d = jnp.dot(jnp.broadcast_to(jnp.sum(dyv * xs, axis=0, keepdims=True), (8, SSD_HEADS * SSD_P)), etv,
                      precision=HI, preferred_element_type=F32)[0:1, :]

        @pl.when(step == 0)
        def _():
            dalog_ref[...] = p_alog
            dbias_ref[...] = p_bias
            dd_ref[...] = p_d

        @pl.when(step > 0)
        def _():
            dalog_ref[...] += p_alog
            dbias_ref[...] += p_bias
            dd_ref[...] += p_d

    nh = LANE
    rev = lambda i: nc - 1 - i
    chunk_row = lambda w, cb: pl.BlockSpec((CHUNK, w), lambda i, cb=cb: (rev(i), cb))
    full = lambda a: pl.BlockSpec(a.shape, lambda i: (0,) * a.ndim)
    vec = pl.BlockSpec((1, nh), lambda i: (0, 0))
    in_specs = [chunk_row(1024, 0), chunk_row(512, 2), chunk_row(512, 3), chunk_row(nh, 0),
                pl.BlockSpec((None, nh, CHUNK), lambda i: (rev(i), 0, 0)),
                full(bias), full(biast), full(alog), full(alogt), full(dskip), full(e), full(et), full(ltri),
                full(utri),
                pl.BlockSpec((None, SSD_GROUPS, SSD_N, gw), lambda i: (rev(i), 0, 0, 0)), chunk_row(1024, 0)]
    return pl.pallas_call(
        body, name=name, grid=(nc,), in_specs=in_specs,
        out_specs=[chunk_row(1024, 0), chunk_row(512, 0), chunk_row(512, 0), chunk_row(nh, 0), vec, vec, vec],
        out_shape=[jax.ShapeDtypeStruct((s_len, 1024), F32), jax.ShapeDtypeStruct((s_len, 512), F32),
                   jax.ShapeDtypeStruct((s_len, 512), F32), jax.ShapeDtypeStruct((s_len, nh), BF16),
                   jax.ShapeDtypeStruct((1, nh), F32), jax.ShapeDtypeStruct((1, nh), F32),
                   jax.ShapeDtypeStruct((1, nh), F32)],
        scratch_shapes=[pltpu.VMEM((SSD_GROUPS, SSD_N, gw), F32)],
        compiler_params=_cparams(("arbitrary",)),
    )(xbc, xbc, xbc, dtr, dtrt, bias, biast, alog, alogt, dskip, e, et, ltri, utri, prev, dy)


def _ssd_gate_fwd(y, z, g, *, name):
    r, w = y.shape
    tr = _pick(r, 512, 8)
    gw = w // SSD_GROUPS

    def body(y_ref, z_ref, g_ref, o_ref):
        for k in range(SSD_GROUPS):
            cs = slice(k * gw, (k + 1) * gw)
            t = y_ref[:, cs] * _silu(z_ref[:, cs])
            rstd = lax.rsqrt(jnp.mean(t * t, axis=-1, keepdims=True) + EPS)
            o_ref[:, cs] = (t * rstd * g_ref[:, cs]).astype(o_ref.dtype)

    row = pl.BlockSpec((tr, w), lambda i: (i, 0))
    vec = pl.BlockSpec((1, w), lambda i: (0, 0))
    return pl.pallas_call(
        body, name=name, grid=(r // tr,), in_specs=[row, row, vec], out_specs=row,
        out_shape=jax.ShapeDtypeStruct((r, w), BF16), compiler_params=_cparams(("parallel",)),
    )(y, z, g)


def _ssd_gate_bwd(y, z, g, do, *, name):
    r, w = y.shape
    tr = _pick(r, 512, 8)
    gw = w // SSD_GROUPS

    def body(y_ref, z_ref, g_ref, do_ref, dy_ref, dz_ref, dg_ref):
        parts = []
        for k in range(SSD_GROUPS):
            cs = slice(k * gw, (k + 1) * gw)
            yv = y_ref[:, cs]
            zv = z_ref[:, cs]
            sz = _silu(zv)
            t = yv * sz
            rstd = lax.rsqrt(jnp.mean(t * t, axis=-1, keepdims=True) + EPS)
            th = t * rstd
            dov = do_ref[:, cs].astype(F32)
            dog = dov * g_ref[:, cs]
            dt = rstd * (dog - th * jnp.mean(dog * th, axis=-1, keepdims=True))
            dy_ref[:, cs] = dt * sz
            dz_ref[:, cs] = (dt * yv * _dsilu(zv)).astype(dz_ref.dtype)
            parts.append(jnp.sum(dov * th, axis=0, keepdims=True))
        pg = jnp.concatenate(parts, axis=1)

        @pl.when(pl.program_id(0) == 0)
        def _():
            dg_ref[...] = pg

        @pl.when(pl.program_id(0) > 0)
        def _():
            dg_ref[...] += pg

    row = pl.BlockSpec((tr, w), lambda i: (i, 0))
    vec = pl.BlockSpec((1, w), lambda i: (0, 0))
    return pl.pallas_call(
        body, name=name, grid=(r // tr,), in_specs=[row, row, vec, row], out_specs=[row, row, vec],
        out_shape=[jax.ShapeDtypeStruct((r, w), F32), jax.ShapeDtypeStruct((r, w), BF16),
                   jax.ShapeDtypeStruct((1, w), F32)],
        compiler_params=_cparams(("arbitrary",)),
    )(y, z, g, do)


def _rope_swap(x):
    lane = lax.broadcasted_iota(jnp.int32, x.shape, 1)
    lo = pltpu.roll(x, 96, 1)
    hi = pltpu.roll(x, 32, 1)
    return jnp.where(lane < 32, lo, jnp.where(lane < 64, hi, 0.0))


def _norm_part(v, g, n):
    rstd = lax.rsqrt(jnp.sum(v * v, axis=-1, keepdims=True) * (1.0 / n) + EPS)
    xh = v * rstd
    return xh * g, xh, rstd


def _norm_part_bwd(dout, g, xh, rstd, n):
    dg = dout * g
    return rstd * (dg - xh * (jnp.sum(dg * xh, axis=-1, keepdims=True) * (1.0 / n)))


def _mla_prep_fwd(q, kv, krr, cs, sn, gq, gk, *, name):
    s = q.shape[0]
    tr = _pick(s, 256, 8)
    hp = MLA_HP

    def body(q_ref, kv_ref, krr_ref, cs_ref, sn_ref, gq_ref, gk_ref, qf_ref, kf_ref, v_ref):
        csv, snv = cs_ref[...], sn_ref[...]
        gqn, gqr = gq_ref[:, 0:128], gq_ref[:, 128:256]
        gkn, gkr = gk_ref[:, 0:128], gk_ref[:, 128:256]
        kr, _, _ = _norm_part(krr_ref[...], gkr, MLA_ROPE)
        kr = (kr * csv + _rope_swap(kr) * snv).astype(BF16)
        for h in range(MLA_HEADS):
            qn, _, _ = _norm_part(q_ref[:, h * hp:h * hp + 128], gqn, MLA_NOPE)
            qr, _, _ = _norm_part(q_ref[:, h * hp + 128:(h + 1) * hp], gqr, MLA_ROPE)
            qr = qr * csv + _rope_swap(qr) * snv
            qf_ref[h, :, 0:128] = qn.astype(BF16)
            qf_ref[h, :, 128:256] = qr.astype(BF16)
            kn, _, _ = _norm_part(kv_ref[:, h * hp:h * hp + 128], gkn, MLA_NOPE)
            kf_ref[h, :, 0:128] = kn.astype(BF16)
            kf_ref[h, :, 128:256] = kr
            v_ref[h] = kv_ref[:, h * hp + 128:(h + 1) * hp].astype(BF16)

    row = lambda w: pl.BlockSpec((tr, w), lambda i: (i, 0))
    vec = pl.BlockSpec((1, hp), lambda i: (0, 0))
    hrow = lambda w: pl.BlockSpec((MLA_HEADS, tr, w), lambda i: (0, i, 0))
    return pl.pallas_call(
        body, name=name, grid=(s // tr,),
        in_specs=[row(MLA_HEADS * hp), row(MLA_HEADS * hp), row(128), row(128), row(128), vec, vec],
        out_specs=[hrow(hp), hrow(hp), hrow(128)],
        out_shape=[jax.ShapeDtypeStruct((MLA_HEADS, s, hp), BF16), jax.ShapeDtypeStruct((MLA_HEADS, s, hp), BF16),
                   jax.ShapeDtypeStruct((MLA_HEADS, s, 128), BF16)],
        compiler_params=_cparams(("parallel",)),
    )(q, kv, krr, cs, sn, gq, gk)


def _mla_prep_bwd(q, kv, krr, cs, sn, gq, gk, dqf, dkf, dv, *, name):
    s = q.shape[0]
    tr = _pick(s, 256, 8)
    hp = MLA_HP

    def body(q_ref, kv_ref, krr_ref, cs_ref, sn_ref, gq_ref, gk_ref, dqf_ref, dkf_ref, dv_ref,
             dq_ref, dkv_ref, dkrr_ref, dgq_ref, dgk_ref):
        csv, snv = cs_ref[...], sn_ref[...]
        gqn, gqr = gq_ref[:, 0:128], gq_ref[:, 128:256]
        gkn, gkr = gk_ref[:, 0:128], gk_ref[:, 128:256]
        _, krh, krs = _norm_part(krr_ref[...], gkr, MLA_ROPE)
        dkr_sum = jnp.zeros((tr, 128), F32)
        pgqn = jnp.zeros((1, 128), F32)
        pgqr = jnp.zeros((1, 128), F32)
        pgkn = jnp.zeros((1, 128), F32)
        for h in range(MLA_HEADS):
            _, qnh, qns = _norm_part(q_ref[:, h * hp:h * hp + 128], gqn, MLA_NOPE)
            _, qrh, qrs = _norm_part(q_ref[:, h * hp + 128:(h + 1) * hp], gqr, MLA_ROPE)
            dqn = dqf_ref[h, :, 0:128]
            drr = dqf_ref[h, :, 128:256]
            dqr = drr * csv + _rope_swap(drr * snv)
            dq_ref[:, h * hp:h * hp + 128] = _norm_part_bwd(dqn, gqn, qnh, qns, MLA_NOPE).astype(dq_ref.dtype)
            dq_ref[:, h * hp + 128:(h + 1) * hp] = _norm_part_bwd(dqr, gqr, qrh, qrs, MLA_ROPE).astype(dq_ref.dtype)
            pgqn = pgqn + jnp.sum(dqn * qnh, axis=0, keepdims=True)
            pgqr = pgqr + jnp.sum(dqr * qrh, axis=0, keepdims=True)
            _, knh, kns = _norm_part(kv_ref[:, h * hp:h * hp + 128], gkn, MLA_NOPE)
            dkn = dkf_ref[h, :, 0:128]
            dkv_ref[:, h * hp:h * hp + 128] = _norm_part_bwd(dkn, gkn, knh, kns, MLA_NOPE).astype(dkv_ref.dtype)
            dkv_ref[:, h * hp + 128:(h + 1) * hp] = dv_ref[h].astype(dkv_ref.dtype)
            pgkn = pgkn + jnp.sum(dkn * knh, axis=0, keepdims=True)
            dkr_sum = dkr_sum + dkf_ref[h, :, 128:256]
        dkr = dkr_sum * csv + _rope_swap(dkr_sum * snv)
        dkrr_ref[...] = _norm_part_bwd(dkr, gkr, krh, krs, MLA_ROPE).astype(dkrr_ref.dtype)
        pgkr = jnp.sum(dkr * krh, axis=0, keepdims=True)
        pq = jnp.concatenate([pgqn, pgqr], axis=1)
        pk = jnp.concatenate([pgkn, pgkr], axis=1)

        @pl.when(pl.program_id(0) == 0)
        def _():
            dgq_ref[...] = pq
            dgk_ref[...] = pk

        @pl.when(pl.program_id(0) > 0)
        def _():
            dgq_ref[...] += pq
            dgk_ref[...] += pk

    row = lambda w: pl.BlockSpec((tr, w), lambda i: (i, 0))
    vec = pl.BlockSpec((1, hp), lambda i: (0, 0))
    hrow = lambda w: pl.BlockSpec((MLA_HEADS, tr, w), lambda i: (0, i, 0))
    return pl.pallas_call(
        body, name=name, grid=(s // tr,),
        in_specs=[row(MLA_HEADS * hp), row(MLA_HEADS * hp), row(128), row(128), row(128), vec, vec,
                  hrow(hp), hrow(hp), hrow(128)],
        out_specs=[row(MLA_HEADS * hp), row(MLA_HEADS * hp), row(128), vec, vec],
        out_shape=[jax.ShapeDtypeStruct((s, MLA_HEADS * hp), BF16), jax.ShapeDtypeStruct((s, MLA_HEADS * hp), BF16),
                   jax.ShapeDtypeStruct((s, 128), BF16), jax.ShapeDtypeStruct((1, hp), F32),
                   jax.ShapeDtypeStruct((1, hp), F32)],
        compiler_params=_cparams(("arbitrary",), 48),
    )(q, kv, krr, cs, sn, gq, gk, dqf, dkf, dv)


ATT_T = 512


def _chunk_mask(t):
    r = lax.shift_right_logical(lax.broadcasted_iota(jnp.int32, (t, t), 0), 6)
    c = lax.shift_right_logical(lax.broadcasted_iota(jnp.int32, (t, t), 1), 6)
    return c <= r


def _mla_attn_fwd(qf, kf, v, *, name):
    nh, s, hp = qf.shape
    t = _pick(s, ATT_T, CHUNK)
    scale = (MLA_NOPE + MLA_ROPE) ** -0.5
    nt = (((1,), (1,)), ((), ()))

    def body(q_ref, k_ref, v_ref, o_ref, lse_ref):
        i = pl.program_id(1)
        q = q_ref[...]

        def block(j, carry, masked):
            m, l, acc = carry
            start = pl.multiple_of(j * t, t)
            k = k_ref[pl.ds(start, t), :]
            sc = lax.dot_general(q, k, nt, preferred_element_type=F32) * scale
            if masked:
                sc = jnp.where(_chunk_mask(t), sc, NEG)
            m_new = jnp.maximum(m, jnp.max(sc, axis=-1, keepdims=True))
            alpha = jnp.exp(m - m_new)
            p = jnp.exp(sc - m_new)
            l = alpha * l + jnp.sum(p, axis=-1, keepdims=True)
            acc = alpha * acc + jnp.dot(p.astype(BF16), v_ref[pl.ds(start, t), :], preferred_element_type=F32)
            return m_new, l, acc

        init = (jnp.full((t, 1), NEG, F32), jnp.zeros((t, 1), F32), jnp.zeros((t, MLA_V), F32))
        carry = lax.fori_loop(0, i, lambda j, c: block(j, c, False), init)
        m, l, acc = block(i, carry, True)
        o_ref[...] = acc / l
        lse_ref[...] = m + jnp.log(l)

    return pl.pallas_call(
        body, name=name, grid=(nh, s // t),
        in_specs=[pl.BlockSpec((None, t, hp), lambda h, i: (h, i, 0)),
                  pl.BlockSpec((None, s, hp), lambda h, i: (h, 0, 0)),
                  pl.BlockSpec((None, s, MLA_V), lambda h, i: (h, 0, 0))],
        out_specs=[pl.BlockSpec((t, MLA_V), lambda h, i: (i, h)),
                   pl.BlockSpec((None, t, 1), lambda h, i: (h, i, 0))],
        out_shape=[jax.ShapeDtypeStruct((s, nh * MLA_V), F32), jax.ShapeDtypeStruct((nh, s, 1), F32)],
        compiler_params=_cparams(("parallel", "arbitrary"), 48),
    )(qf, kf, v)


def _mla_attn_bwd(qf, kf, v, o, lse, do, *, name):
    nh, s, hp = qf.shape
    t = _pick(s, ATT_T, CHUNK)
    nb = s // t
    scale = (MLA_NOPE + MLA_ROPE) ** -0.5
    nt = (((1,), (1,)), ((), ()))
    tn = (((0,), (0,)), ((), ()))

    def body(q_ref, k_ref, v_ref, o_ref, lse_ref, do_ref, dq_ref, dk_ref, dv_ref, delta):
        j = pl.program_id(1)

        @pl.when(j == 0)
        def _():
            dq_ref[...] = jnp.zeros_like(dq_ref)
            delta[...] = jnp.sum(do_ref[...] * o_ref[...], axis=-1, keepdims=True)

        k = k_ref[...]
        vv = v_ref[...]

        def block(i, carry, masked):
            dk, dv = carry
            start = pl.multiple_of(i * t, t)
            q = q_ref[pl.ds(start, t), :]
            dob = do_ref[pl.ds(start, t), :].astype(BF16)
            sc = lax.dot_general(q, k, nt, preferred_element_type=F32) * scale
            if masked:
                sc = jnp.where(_chunk_mask(t), sc, NEG)
            p = jnp.exp(sc - lse_ref[pl.ds(start, t), :])
            dp = lax.dot_general(dob, vv, nt, preferred_element_type=F32)
            ds = (p * (dp - delta[pl.ds(start, t), :]) * scale).astype(BF16)
            dv = dv + lax.dot_general(p.astype(BF16), dob, tn, preferred_element_type=F32)
            dk = dk + lax.dot_general(ds, q, tn, preferred_element_type=F32)
            dq_ref[pl.ds(start, t), :] += jnp.dot(ds, k, preferred_element_type=F32)
            return dk, dv

        init = (jnp.zeros((t, hp), F32), jnp.zeros((t, MLA_V), F32))
        carry = block(j, init, True)
        dk, dv = lax.fori_loop(j + 1, nb, lambda i, c: block(i, c, False), carry)
        dk_ref[...] = dk
        dv_ref[...] = dv

    whole = lambda w: pl.BlockSpec((None, s, w), lambda h, j: (h, 0, 0))
    blk = lambda w: pl.BlockSpec((None, t, w), lambda h, j: (h, j, 0))
    colh = pl.BlockSpec((s, MLA_V), lambda h, j: (0, h))
    return pl.pallas_call(
        body, name=name, grid=(nh, nb),
        in_specs=[whole(hp), blk(hp), blk(MLA_V), colh, whole(1), colh],
        out_specs=[whole(hp), blk(hp), blk(MLA_V)],
        out_shape=[jax.ShapeDtypeStruct((nh, s, hp), F32), jax.ShapeDtypeStruct((nh, s, hp), F32),
                   jax.ShapeDtypeStruct((nh, s, MLA_V), F32)],
        scratch_shapes=[pltpu.VMEM((s, 1), F32)],
        compiler_params=_cparams(("parallel", "arbitrary"), 56),
    )(qf, kf, v, o, lse, do)


def _merge_fwd(gl, gb, ys, yc, ym, *, name):
    s, d = ys.shape
    tr = _pick(s, 256, 8)

    def body(gl_ref, gb_ref, ys_ref, yc_ref, ym_ref, o_ref):
        acc = jnp.zeros((tr, d), F32)
        for k, y_ref in enumerate((ys_ref, yc_ref, ym_ref)):
            gt = _sigmoid(gl_ref[:, k * d:(k + 1) * d] + gb_ref[:, k * d:(k + 1) * d])
            acc = acc + gt * y_ref[...]
        o_ref[...] = acc.astype(o_ref.dtype)

    row = lambda w: pl.BlockSpec((tr, w), lambda i: (i, 0))
    return pl.pallas_call(
        body, name=name, grid=(s // tr,),
        in_specs=[row(3 * d), pl.BlockSpec((1, 3 * d), lambda i: (0, 0)), row(d), row(d), row(d)],
        out_specs=row(d), out_shape=jax.ShapeDtypeStruct((s, d), BF16),
        compiler_params=_cparams(("parallel",)),
    )(gl, gb, ys, yc, ym)


def _merge_bwd(gl, gb, ys, yc, ym, dm, *, name):
    s, d = ys.shape
    tr = _pick(s, 256, 8)

    def body(gl_ref, gb_ref, ys_ref, yc_ref, ym_ref, dm_ref, dgl_ref, dgb_ref, dys_ref, dyc_ref, dym_ref):
        dmv = dm_ref[...]
        parts = []
        for k, (y_ref, dy_ref) in enumerate(((ys_ref, dys_ref), (yc_ref, dyc_ref), (ym_ref, dym_ref))):
            gt = _sigmoid(gl_ref[:, k * d:(k + 1) * d] + gb_ref[:, k * d:(k + 1) * d])
            dy_ref[...] = (gt * dmv).astype(dy_ref.dtype)
            dl = dmv * y_ref[...] * gt * (1.0 - gt)
            dgl_ref[:, k * d:(k + 1) * d] = dl.astype(dgl_ref.dtype)
            parts.append(jnp.sum(dl, axis=0, keepdims=True))
        pb = jnp.concatenate(parts, axis=1)

        @pl.when(pl.program_id(0) == 0)
        def _():
            dgb_ref[...] = pb

        @pl.when(pl.program_id(0) > 0)
        def _():
            dgb_ref[...] += pb

    row = lambda w: pl.BlockSpec((tr, w), lambda i: (i, 0))
    vec = pl.BlockSpec((1, 3 * d), lambda i: (0, 0))
    return pl.pallas_call(
        body, name=name, grid=(s // tr,),
        in_specs=[row(3 * d), vec, row(d), row(d), row(d), row(d)],
        out_specs=[row(3 * d), vec, row(d), row(d), row(d)],
        out_shape=[jax.ShapeDtypeStruct((s, 3 * d), BF16), jax.ShapeDtypeStruct((1, 3 * d), F32)]
        + [jax.ShapeDtypeStruct((s, d), BF16)] * 3,
        compiler_params=_cparams(("arbitrary",)),
    )(gl, gb, ys, yc, ym, dm)


def _xattn_fwd(q, k, v, gq, *, name):
    s, d = q.shape
    dh = d // X_HEADS
    tr = _pick(s, 512, 8)
    scale = dh ** -0.5
    nt = (((1,), (1,)), ((), ()))

    def body(q_ref, k_ref, v_ref, gq_ref, o_ref):
        for h in range(X_HEADS):
            cs = slice(h * dh, (h + 1) * dh)
            qn, _, _ = _norm_part(q_ref[:, cs], gq_ref[...], dh)
            sc = lax.dot_general(qn.astype(BF16), k_ref[:, cs], nt, preferred_element_type=F32) * scale
            p = jnp.exp(sc - jnp.max(sc, axis=-1, keepdims=True))
            p = p / jnp.sum(p, axis=-1, keepdims=True)
            o_ref[:, cs] = jnp.dot(p.astype(BF16), v_ref[:, cs], preferred_element_type=F32)

    row = pl.BlockSpec((tr, d), lambda i: (i, 0))
    mem = pl.BlockSpec(k.shape, lambda i: (0, 0))
    return pl.pallas_call(
        body, name=name, grid=(s // tr,),
        in_specs=[row, mem, mem, pl.BlockSpec((1, dh), lambda i: (0, 0))], out_specs=row,
        out_shape=jax.ShapeDtypeStruct((s, d), F32), compiler_params=_cparams(("parallel",)),
    )(q, k, v, gq)


def _xattn_bwd(q, k, v, gq, do, *, name):
    s, d = q.shape
    dh = d // X_HEADS
    tr = _pick(s, 512, 8)
    scale = dh ** -0.5
    nt = (((1,), (1,)), ((), ()))
    tn = (((0,), (0,)), ((), ()))

    def body(q_ref, k_ref, v_ref, gq_ref, do_ref, dq_ref, dk_ref, dv_ref, dgq_ref):
        first = pl.program_id(0) == 0
        pg = jnp.zeros((1, dh), F32)
        for h in range(X_HEADS):
            cs = slice(h * dh, (h + 1) * dh)
            qn, qh, qs = _norm_part(q_ref[:, cs], gq_ref[...], dh)
            qnb = qn.astype(BF16)
            kh = k_ref[:, cs]
            sc = lax.dot_general(qnb, kh, nt, preferred_element_type=F32) * scale
            p = jnp.exp(sc - jnp.max(sc, axis=-1, keepdims=True))
            p = p / jnp.sum(p, axis=-1, keepdims=True)
            dob = do_ref[:, cs].astype(BF16)
            dp = lax.dot_general(dob, v_ref[:, cs], nt, preferred_element_type=F32)
            ds = (p * (dp - jnp.sum(dp * p, axis=-1, keepdims=True)) * scale).astype(BF16)
            dqn = jnp.dot(ds, kh, preferred_element_type=F32)
            dq_ref[:, cs] = _norm_part_bwd(dqn, gq_ref[...], qh, qs, dh).astype(dq_ref.dtype)
            pg = pg + jnp.sum(dqn * qh, axis=0, keepdims=True)
            pv = lax.dot_general(p.astype(BF16), dob, tn, preferred_element_type=F32)
            pk = lax.dot_general(ds, qnb, tn, preferred_element_type=F32)

            @pl.when(first)
            def _():
                dv_ref[:, cs] = pv
                dk_ref[:, cs] = pk

            @pl.when(jnp.logical_not(first))
            def _():
                dv_ref[:, cs] += pv
                dk_ref[:, cs] += pk

        @pl.when(first)
        def _():
            dgq_ref[...] = pg

        @pl.when(jnp.logical_not(first))
        def _():
            dgq_ref[...] += pg

    row = pl.BlockSpec((tr, d), lambda i: (i, 0))
    mem = pl.BlockSpec(k.shape, lambda i: (0, 0))
    vec = pl.BlockSpec((1, dh), lambda i: (0, 0))
    return pl.pallas_call(
        body, name=name, grid=(s // tr,),
        in_specs=[row, mem, mem, vec, row], out_specs=[row, mem, mem, vec],
        out_shape=[jax.ShapeDtypeStruct((s, d), BF16), jax.ShapeDtypeStruct(k.shape, F32),
                   jax.ShapeDtypeStruct(k.shape, F32), jax.ShapeDtypeStruct((1, dh), F32)],
        compiler_params=_cparams(("arbitrary",)),
    )(q, k, v, gq, do)


def _swiglu_fwd(h1, *, name):
    s, w2 = h1.shape
    w = w2 // 2
    tr = _pick(s, 256, 8)
    tc = _pick(w, 1408, 128)
    ncb = w // tc

    def body(g_ref, u_ref, o_ref):
        o_ref[...] = (_silu(g_ref[...]) * u_ref[...]).astype(o_ref.dtype)

    return pl.pallas_call(
        body, name=name, grid=(s // tr, ncb),
        in_specs=[pl.BlockSpec((tr, tc), lambda i, j: (i, j)), pl.BlockSpec((tr, tc), lambda i, j: (i, j + ncb))],
        out_specs=pl.BlockSpec((tr, tc), lambda i, j: (i, j)),
        out_shape=jax.ShapeDtypeStruct((s, w), BF16), compiler_params=_cparams(("parallel", "parallel")),
    )(h1, h1)


def _swiglu_bwd(h1, dact, *, name):
    s, w2 = h1.shape
    w = w2 // 2
    tr = _pick(s, 256, 8)
    tc = _pick(w, 1408, 128)
    ncb = w // tc

    def body(g_ref, u_ref, d_ref, dg_ref, du_ref):
        gv = g_ref[...]
        dv = d_ref[...]
        dg_ref[...] = (dv * u_ref[...] * _dsilu(gv)).astype(dg_ref.dtype)
        du_ref[...] = (dv * _silu(gv)).astype(du_ref.dtype)

    blk = pl.BlockSpec((tr, tc), lambda i, j: (i, j))
    dg, du = pl.pallas_call(
        body, name=name, grid=(s // tr, ncb),
        in_specs=[blk, pl.BlockSpec((tr, tc), lambda i, j: (i, j + ncb)), blk],
        out_specs=[blk, blk],
        out_shape=[jax.ShapeDtypeStruct((s, w), BF16)] * 2, compiler_params=_cparams(("parallel", "parallel")),
    )(h1, h1, dact)
    return jnp.concatenate([dg, du], axis=1)


def _add(a, b, *, name):
    r, w = a.shape
    tr = _pick(r, 512, 8)

    def body(a_ref, b_ref, o_ref):
        o_ref[...] = a_ref[...] + b_ref[...].astype(F32)

    row = pl.BlockSpec((tr, w), lambda i: (i, 0))
    return pl.pallas_call(
        body, name=name, grid=(r // tr,), in_specs=[row, row], out_specs=row,
        out_shape=jax.ShapeDtypeStruct((r, w), F32), compiler_params=_cparams(("parallel",)),
    )(a, b)


def _loss(y, target, *, name):
    r, w = y.shape
    tr = _pick(r, 512, 8)

    def body(y_ref, t_ref, dy_ref, l_ref):
        err = y_ref[...] - t_ref[...]
        dy_ref[...] = err * (1.0 / w)
        part = jnp.zeros((8, LANE), F32) + 0.5 * jnp.sum(jnp.mean(err * err, axis=-1, keepdims=True))

        @pl.when(pl.program_id(0) == 0)
        def _():
            l_ref[...] = part

        @pl.when(pl.program_id(0) > 0)
        def _():
            l_ref[...] += part

    row = pl.BlockSpec((tr, w), lambda i: (i, 0))
    dy, l = pl.pallas_call(
        body, name=name, grid=(r // tr,), in_specs=[row, row],
        out_specs=[row, pl.BlockSpec((8, LANE), lambda i: (0, 0))],
        out_shape=[jax.ShapeDtypeStruct((r, w), F32), jax.ShapeDtypeStruct((8, LANE), F32)],
        compiler_params=_cparams(("arbitrary",)),
    )(y, target)
    return dy, l[0, 0]


def _adamw(w, g, m, v, *, name):
    r, c = w.shape
    tr = _pick(r, 256, 8)
    c1 = 1.0 - ADAM_B1 ** ADAM_STEP
    c2 = 1.0 - ADAM_B2 ** ADAM_STEP

    def body(w_ref, g_ref, m_ref, v_ref, d_ref, nm_ref, nv_ref):
        gv = g_ref[...]
        nm = ADAM_B1 * m_ref[...] + (1.0 - ADAM_B1) * gv
        nv = ADAM_B2 * v_ref[...] + (1.0 - ADAM_B2) * (gv * gv)
        nm_ref[...] = nm
        nv_ref[...] = nv
        d_ref[...] = -ADAM_LR * ((nm / c1) / (jnp.sqrt(nv / c2) + ADAM_EPS) + ADAM_WD * w_ref[...])

    row = pl.BlockSpec((tr, c), lambda i: (i, 0))
    return pl.pallas_call(
        body, name=name, grid=(r // tr,), in_specs=[row] * 4, out_specs=[row] * 3,
        out_shape=[jax.ShapeDtypeStruct((r, c), F32)] * 3, compiler_params=_cparams(("parallel",)),
    )(w, g, m, v)


IN_SPLIT = dict(z=(0, 1024), xbc=(1024, 3072), dt=(3072, 3200), glu=(3200, 5248), ql=(5248, 5632),
                ckv=(5632, 5888), kr=(5888, 6016), gate=(6016, 9088))


IN_WIDTH_PAD = 9216


def _w_in_pad(w):
    zeros = lambda n: jnp.zeros(w.shape[:-1] + (n,), w.dtype)
    return jnp.concatenate([w[..., :3088], zeros(112), w[..., 3088:5840], zeros(64), w[..., 5840:],
                            zeros(IN_WIDTH_PAD - 9088)], axis=-1)


def _w_in_unpad(g):
    return jnp.concatenate([g[..., :3088], g[..., 3200:5952], g[..., 6016:9088]], axis=-1)


def _qb_pad(w):
    lead = w.shape[:-1]
    w = w.reshape(lead + (MLA_HEADS, MLA_NOPE + MLA_ROPE))
    w = jnp.concatenate([w, jnp.zeros(lead + (MLA_HEADS, MLA_HP - MLA_NOPE - MLA_ROPE), w.dtype)], axis=-1)
    return w.reshape(lead + (MLA_HEADS * MLA_HP,))


def _qb_unpad(g):
    lead = g.shape[:-1]
    g = g.reshape(lead + (MLA_HEADS, MLA_HP))[..., :MLA_NOPE + MLA_ROPE]
    return g.reshape(lead + (MLA_HEADS * (MLA_NOPE + MLA_ROPE),))


def _pad_lanes(v, n):
    return jnp.concatenate([v, jnp.zeros((n - v.shape[0],), v.dtype)]).reshape(1, n)


def _layer_params(full, rep, l):
    p = {}
    w_in = full['w_in'][l]
    for k, (a, b) in IN_SPLIT.items():
        p['w_' + k] = w_in[:, a:b]
    p['w_in'] = w_in
    for k in ('mla_w_q_b', 'mla_w_kv_b', 'xattn_w_kv', 'ffn_w_in', 'ssd_w_out', 'conv_w_out', 'mla_w_o', 'w_out',
              'xattn_w_q', 'xattn_w_o', 'ffn_w_out', 'ssd_conv_w', 'conv_dw_w'):
        p[k] = full[k][l]
    p['gate_b'] = full['gate_b'][l].reshape(1, -1)
    row = lambda name: rep[name][l].reshape(1, -1)
    for k in ('mix_norm_g', 'ssd_conv_b', 'ssd_norm_g', 'conv_dw_b', 'conv_ln_g', 'conv_ln_b', 'mla_q_a_g',
              'mla_kv_a_g', 'xattn_norm_g', 'mem_norm_g', 'xattn_q_norm_g', 'xattn_k_norm_g', 'ffn_norm_g'):
        p[k] = row(k)
    for k in ('ssd_dt_bias', 'ssd_a_log', 'ssd_d'):
        p[k] = _pad_lanes(rep[k][l], LANE)
        p[k + '_t'] = p[k].reshape(LANE, 1)
    p['gq'] = _pad_lanes(rep['mla_q_norm_g'][l], MLA_HP)
    p['gk'] = _pad_lanes(rep['mla_k_norm_g'][l], MLA_HP)
    return p


def _layer_fwd(x, mem, cs, sn, p, l):
    n = lambda s: f"l{l}_{s}"
    s_len, d = x.shape
    nc = s_len // CHUNK
    sv = {'x': x}
    u = _rms_fwd(x, p['mix_norm_g'], out_dtype=BF16, name=n("mix_norm"))
    z = _matmul(u, p['w_z'], name=n("in_z"))
    xbc = _matmul(u, p['w_xbc'], name=n("in_xbc"))
    dtr = _matmul(u, p['w_dt'], name=n("in_dt"))
    glu = _matmul(u, p['w_glu'], name=n("in_glu"))
    ql = _matmul(u, p['w_ql'], name=n("in_ql"))
    ckv = _matmul(u, p['w_ckv'], name=n("in_ckv"))
    krr = _matmul(u, p['w_kr'], name=n("in_kr"))
    gl = _matmul(u, p['w_gate'], name=n("in_gate"))
    pre_s, act_s = _conv_fwd(xbc, p['ssd_conv_w'], p['ssd_conv_b'], glu=False, act=True, name=n("ssd_conv"))
    dtrt = dtr.reshape(nc, CHUNK, LANE).transpose(0, 2, 1)
    y_scan, prev = _ssd_scan_fwd(act_s, dtr, dtrt, p['ssd_dt_bias'], p['ssd_dt_bias_t'], p['ssd_a_log'],
                                 p['ssd_a_log_t'], p['ssd_d'], name=n("ssd_scan"))
    yn = _ssd_gate_fwd(y_scan, z, p['ssd_norm_g'], name=n("ssd_gate"))
    y_ssd = _matmul(yn, p['ssd_w_out'], name=n("ssd_out"))
    pre_c, = _conv_fwd(glu, p['conv_dw_w'], p['conv_dw_b'], glu=True, act=False, name=n("dw_conv"))
    vc = _ln_silu_fwd(pre_c, p['conv_ln_g'], p['conv_ln_b'], name=n("conv_ln"))
    y_conv = _matmul(vc, p['conv_w_out'], name=n("conv_out"))
    qln = _rms_fwd(ql, p['mla_q_a_g'], out_dtype=BF16, name=n("q_a_norm"))
    q = _matmul(qln, p['mla_w_q_b'], name=n("q_b"))
    ckvn = _rms_fwd(ckv, p['mla_kv_a_g'], out_dtype=BF16, name=n("kv_a_norm"))
    kv = _matmul(ckvn, p['mla_w_kv_b'], name=n("kv_b"))
    qf, kf, v = _mla_prep_fwd(q, kv, krr, cs, sn, p['gq'], p['gk'], name=n("mla_prep"))
    o, lse = _mla_attn_fwd(qf, kf, v, name=n("mla_attn"))
    y_mla = _matmul(o, p['mla_w_o'], name=n("mla_out"))
    merged = _merge_fwd(gl, p['gate_b'], y_ssd, y_conv, y_mla, name=n("merge"))
    x1 = _matmul(merged, p['w_out'], add=x, name=n("mix_out"))
    hx = _rms_fwd(x1, p['xattn_norm_g'], out_dtype=BF16, name=n("xattn_norm"))
    qx = _matmul(hx, p['xattn_w_q'], name=n("xattn_q"))
    memn = _rms_fwd(mem, p['mem_norm_g'], out_dtype=BF16, name=n("mem_norm"))
    kvx = _matmul(memn, p['xattn_w_kv'], name=n("xattn_kv"))
    m_len = mem.shape[0]
    dh = d // X_HEADS
    kraw = kvx[:, :d].reshape(m_len * X_HEADS, dh)
    kx = _rms_fwd(kraw, p['xattn_k_norm_g'], out_dtype=BF16, name=n("xattn_k_norm")).reshape(m_len, d)
    vx = kvx[:, d:].astype(BF16)
    ox = _xattn_fwd(qx, kx, vx, p['xattn_q_norm_g'], name=n("xattn_core"))
    x2 = _matmul(ox, p['xattn_w_o'], add=x1, name=n("xattn_out"))
    hf = _rms_fwd(x2, p['ffn_norm_g'], out_dtype=BF16, name=n("ffn_norm"))
    h1 = _matmul(hf, p['ffn_w_in'], name=n("ffn_in"))
    act = _swiglu_fwd(h1, name=n("swiglu"))
    x3 = _matmul(act, p['ffn_w_out'], add=x2, name=n("ffn_out"))
    sv.update(u=u, z=z, xbc=xbc, dtr=dtr, dtrt=dtrt, glu=glu, ql=ql, ckv=ckv, krr=krr, gl=gl, pre_s=pre_s,
              act_s=act_s, y_scan=y_scan, prev=prev, yn=yn, y_ssd=y_ssd, pre_c=pre_c, vc=vc, y_conv=y_conv,
              qln=qln, q=q, ckvn=ckvn, kv=kv, qf=qf, kf=kf, v=v, o=o, lse=lse, y_mla=y_mla, merged=merged,
              x1=x1, hx=hx, qx=qx, memn=memn, kraw=kraw, kx=kx, vx=vx, ox=ox, x2=x2, hf=hf, h1=h1, act=act)
    return x3, sv


def _layer_bwd(dx3, mem, cs, sn, p, sv, l):
    n = lambda s: f"l{l}_b_{s}"
    g = {}
    d = dx3.shape[1]
    dact = _matmul(dx3, p['ffn_w_out'], tb=True, name=n("ffn_out_dx"))
    g['ffn_w_out'] = _matmul(sv['act'], dx3, ta=True, name=n("ffn_out_dw"))
    dh1 = _swiglu_bwd(sv['h1'], dact, name=n("swiglu"))
    g['ffn_w_in'] = _matmul(sv['hf'], dh1, ta=True, name=n("ffn_in_dw"))
    dhf = _matmul(dh1, p['ffn_w_in'], tb=True, name=n("ffn_in_dx"))
    dx2, g['ffn_norm_g'] = _rms_bwd(sv['x2'], p['ffn_norm_g'], dhf, dx_dtype=F32, add=dx3, name=n("ffn_norm"))
    dox = _matmul(dx2, p['xattn_w_o'], tb=True, name=n("xattn_out_dx"))
    g['xattn_w_o'] = _matmul(sv['ox'], dx2, ta=True, name=n("xattn_out_dw"))
    dqx, dkx, dvx, g['xattn_q_norm_g'] = _xattn_bwd(sv['qx'], sv['kx'], sv['vx'], p['xattn_q_norm_g'], dox,
                                                    name=n("xattn_core"))
    g['xattn_w_q'] = _matmul(sv['hx'], dqx, ta=True, name=n("xattn_q_dw"))
    dhx = _matmul(dqx, p['xattn_w_q'], tb=True, name=n("xattn_q_dx"))
    dx1, g['xattn_norm_g'] = _rms_bwd(sv['x1'], p['xattn_norm_g'], dhx, dx_dtype=F32, add=dx2, name=n("xattn_norm"))
    m_len = mem.shape[0]
    dh = d // X_HEADS
    dkraw, g['xattn_k_norm_g'] = _rms_bwd(sv['kraw'], p['xattn_k_norm_g'], dkx.reshape(m_len * X_HEADS, dh),
                                          dx_dtype=BF16, name=n("xattn_k_norm"))
    dkvx = jnp.concatenate([dkraw.reshape(m_len, d), dvx.astype(BF16)], axis=1)
    g['xattn_w_kv'] = _matmul(sv['memn'], dkvx, ta=True, name=n("xattn_kv_dw"))
    dmemn = _matmul(dkvx, p['xattn_w_kv'], tb=True, name=n("xattn_kv_dx"))
    _, g['mem_norm_g'] = _rms_bwd(mem, p['mem_norm_g'], dmemn, dx_dtype=BF16, name=n("mem_norm"))
    dmerged = _matmul(dx1, p['w_out'], tb=True, name=n("mix_out_dx"))
    g['w_out'] = _matmul(sv['merged'], dx1, ta=True, name=n("mix_out_dw"))
    dgl, g['gate_b'], dys, dyc, dym = _merge_bwd(sv['gl'], p['gate_b'], sv['y_ssd'], sv['y_conv'], sv['y_mla'],
                                                 dmerged, name=n("merge"))
    do = _matmul(dym, p['mla_w_o'], tb=True, name=n("mla_out_dx"))
    g['mla_w_o'] = _matmul(sv['o'], dym, ta=True, name=n("mla_out_dw"))
    dqf, dkf, dv = _mla_attn_bwd(sv['qf'], sv['kf'], sv['v'], sv['o'], sv['lse'], do, name=n("mla_attn"))
    dq, dkv, dkrr, g['gq'], g['gk'] = _mla_prep_bwd(sv['q'], sv['kv'], sv['krr'], cs, sn, p['gq'], p['gk'],
                                                    dqf, dkf, dv, name=n("mla_prep"))
    g['mla_w_q_b'] = _matmul(sv['qln'], dq, ta=True, name=n("q_b_dw"))
    dqln = _matmul(dq, p['mla_w_q_b'], tb=True, name=n("q_b_dx"))
    dql, g['mla_q_a_g'] = _rms_bwd(sv['ql'], p['mla_q_a_g'], dqln, dx_dtype=BF16, name=n("q_a_norm"))
    g['mla_w_kv_b'] = _matmul(sv['ckvn'], dkv, ta=True, name=n("kv_b_dw"))
    dckvn = _matmul(dkv, p['mla_w_kv_b'], tb=True, name=n("kv_b_dx"))
    dckv, g['mla_kv_a_g'] = _rms_bwd(sv['ckv'], p['mla_kv_a_g'], dckvn, dx_dtype=BF16, name=n("kv_a_norm"))
    dvc = _matmul(dyc, p['conv_w_out'], tb=True, name=n("conv_out_dx"))
    g['conv_w_out'] = _matmul(sv['vc'], dyc, ta=True, name=n("conv_out_dw"))
    dpre_c, g['conv_ln_g'], g['conv_ln_b'] = _ln_silu_bwd(sv['pre_c'], p['conv_ln_g'], p['conv_ln_b'], dvc,
                                                          name=n("conv_ln"))
    da, dg, g['conv_dw_w'], g['conv_dw_b'] = _conv_bwd(sv['glu'], p['conv_dw_w'], p['conv_dw_b'], dpre_c, None,
                                                       glu=True, act=False, name=n("dw_conv"))
    dyn = _matmul(dys, p['ssd_w_out'], tb=True, name=n("ssd_out_dx"))
    g['ssd_w_out'] = _matmul(sv['yn'], dys, ta=True, name=n("ssd_out_dw"))
    dy_scan, dz, g['ssd_norm_g'] = _ssd_gate_bwd(sv['y_scan'], sv['z'], p['ssd_norm_g'], dyn, name=n("ssd_gate"))
    dxs, db, dc, ddtr, g['ssd_a_log'], g['ssd_dt_bias'], g['ssd_d'] = _ssd_scan_bwd(
        sv['act_s'], sv['dtr'], sv['dtrt'], p['ssd_dt_bias'], p['ssd_dt_bias_t'], p['ssd_a_log'], p['ssd_a_log_t'],
        p['ssd_d'], sv['prev'], dy_scan, name=n("ssd_scan"))
    dact_s = jnp.concatenate([dxs, db, dc], axis=1)
    dxbc, g['ssd_conv_w'], g['ssd_conv_b'] = _conv_bwd(sv['xbc'], p['ssd_conv_w'], p['ssd_conv_b'], dact_s,
                                                       sv['pre_s'], glu=False, act=True, name=n("ssd_conv"))
    tail = jnp.zeros((dz.shape[0], IN_WIDTH_PAD - IN_SPLIT['gate'][1]), BF16)
    dproj = jnp.concatenate([dz, dxbc, ddtr, da, dg, dql, dckv, dkrr, dgl, tail], axis=1)
    g['w_in'] = _matmul(sv['u'], dproj, ta=True, name=n("in_dw"))
    du = _matmul(dproj, p['w_in'], tb=True, name=n("in_dx"))
    dx, g['mix_norm_g'] = _rms_bwd(sv['x'], p['mix_norm_g'], du, dx_dtype=F32, add=dx1, name=n("mix_norm"))
    return dx, g


REP_NAMES = ('mix_norm_g', 'ssd_conv_b', 'ssd_dt_bias', 'ssd_a_log', 'ssd_d', 'ssd_norm_g', 'conv_dw_b', 'conv_ln_g',
             'conv_ln_b', 'mla_q_a_g', 'mla_kv_a_g', 'mla_q_norm_g', 'mla_k_norm_g', 'xattn_norm_g', 'mem_norm_g',
             'xattn_q_norm_g', 'xattn_k_norm_g', 'ffn_norm_g')
BIG = (('w_in', 1, 1024, 8912), ('mla_w_q_b', 1, 384, 1536), ('mla_w_kv_b', 1, 256, 2048),
       ('xattn_w_kv', 1, 1024, 2048), ('ffn_w_in', 1, 1024, 5632), ('ssd_w_out', 0, 1024, 1024),
       ('conv_w_out', 0, 1024, 1024), ('mla_w_o', 0, 1024, 1024), ('w_out', 0, 1024, 1024),
       ('xattn_w_q', 0, 1024, 1024), ('xattn_w_o', 0, 1024, 1024), ('ffn_w_out', 0, 2816, 1024))
SMALL = (('ssd_conv_w', 1, 4, 2048), ('conv_dw_w', 1, 31, 1024), ('gate_b', 1, 3, 1024))


def _rope_tables(positions):
    half = MLA_ROPE // 2
    inv = ROPE_THETA ** (-jnp.arange(0, MLA_ROPE, 2, dtype=F32) / MLA_ROPE)
    ang = positions.astype(F32)[:, None] * inv
    cos, sin = jnp.cos(ang), jnp.sin(ang)
    z = jnp.zeros((positions.shape[0], LANE - 2 * half), F32)
    return jnp.concatenate([cos, cos, z], axis=1), jnp.concatenate([-sin, sin, z], axis=1)


def _local_step(x, mem, positions, target, full, rep):
    depth = rep['mix_norm_g'].shape[0]
    cs, sn = _rope_tables(positions)
    params, saved = [], []
    h = x
    for l in range(depth):
        p = _layer_params(full, rep, l)
        h, sv = _layer_fwd(h, mem, cs, sn, p, l)
        params.append(p)
        saved.append(sv)
    dh, loss = _loss(h, target, name="loss")
    layer_grads = [None] * depth
    for l in reversed(range(depth)):
        dh, layer_grads[l] = _layer_bwd(dh, mem, cs, sn, params[l], saved[l], l)
    stack = lambda k: jnp.stack([layer_grads[l][k] for l in range(depth)])
    gfull = {k: stack(k) for k, _, _, _ in BIG + SMALL}
    gfull['w_in'] = _w_in_unpad(gfull['w_in'])
    gfull['mla_w_q_b'] = _qb_unpad(gfull['mla_w_q_b'])
    gfull['gate_b'] = gfull['gate_b'].reshape(depth, 3, -1)
    grep = {}
    for k in REP_NAMES:
        if k == 'mla_q_norm_g':
            grep[k] = stack('gq')[:, 0, :MLA_NOPE + MLA_ROPE]
        elif k == 'mla_k_norm_g':
            grep[k] = stack('gk')[:, 0, :MLA_NOPE + MLA_ROPE]
        elif k in ('ssd_dt_bias', 'ssd_a_log', 'ssd_d'):
            grep[k] = stack(k)[:, 0, :SSD_HEADS]
        else:
            grep[k] = stack(k)[:, 0, :]
    return loss, dh, gfull, grep


N_CHIPS = 4
HBM_SPEC = pl.BlockSpec(memory_space=pl.ANY)
ROW, COL, STK, REP = "row", "col", "stk", "rep"


def _kind(axis, cs):
    if axis == 0:
        return ROW
    return COL if cs % LANE == 0 else STK


def _mesh_pos():
    return lax.axis_index("x"), lax.axis_index("y"), lax.axis_index("c")


def _chip_view(ref, kind, j, a, b, layers=None):
    lsel = slice(None) if layers is None else pl.ds(layers[0], layers[1])
    if kind == STK:
        return ref.at[j, lsel]
    if kind == ROW:
        return ref.at[lsel, pl.ds(pl.multiple_of(j * a, 8), a), :]
    if kind == COL:
        return ref.at[lsel, :, pl.ds(pl.multiple_of(j * b, LANE), b)]
    return ref.at[lsel]


def _all_gather(shards, kinds, *, name):
    n = len(shards)
    depth = shards[0].shape[0]
    lh = depth // 2

    def out_shape(w, kind):
        _, a, b = w.shape
        full = {ROW: (depth, N_CHIPS * a, b), COL: (depth, a, N_CHIPS * b), STK: (N_CHIPS, depth, a, b)}[kind]
        return jax.ShapeDtypeStruct(full, w.dtype)

    def body(*refs):
        w_refs, out_refs = refs[:n], refs[n:2 * n]
        send_sems, recv_sems, local_sems = refs[2 * n:]
        x, y, cc = _mesh_pos()
        me = (x, y, cc)
        sibling = (x, y, 1 - cc)
        chips = [(1 - x, y), (x, 1 - y), (1 - x, 1 - y)]
        slot = lambda chip: 2 * chip[0] + chip[1]

        def part(i, chip, hc):
            _, a, b = w_refs[i].shape
            return _chip_view(out_refs[i], kinds[i], slot(chip), a, b, (hc * lh, lh))

        def copy(i, k, src, dst, to):
            return pltpu.make_async_remote_copy(src_ref=src, dst_ref=dst, send_sem=send_sems.at[i, k],
                                                recv_sem=recv_sems.at[i, k], device_id=to, device_id_type=MESH)

        local, first, passed = [], [], []
        for i in range(n):
            _, a, b = w_refs[i].shape
            cp = pltpu.make_async_copy(w_refs[i], _chip_view(out_refs[i], kinds[i], slot((x, y)), a, b),
                                       local_sems.at[i])
            cp.start()
            local.append(cp)
            my_half = w_refs[i].at[pl.ds(cc * lh, lh)]
            for k, chip in enumerate(chips):
                cp = copy(i, k, my_half, part(i, (x, y), cc), (*chip, cc))
                cp.start()
                first.append(cp)
        for k, chip in enumerate(chips):
            for i in range(n):
                copy(i, k, part(i, chip, cc), part(i, chip, cc), me).wait_recv()
                cp = copy(i, 3 + k, part(i, chip, cc), part(i, chip, cc), sibling)
                cp.start()
                passed.append(cp)
        for k, chip in enumerate(chips):
            for i in range(n):
                copy(i, 3 + k, part(i, chip, 1 - cc), part(i, chip, 1 - cc), me).wait_recv()
        for cp in first + passed:
            cp.wait_send()
        for cp in local:
            cp.wait()

    return pl.pallas_call(
        body, name=name, in_specs=[HBM_SPEC] * n, out_specs=[HBM_SPEC] * n,
        out_shape=[out_shape(w, kd) for w, kd in zip(shards, kinds)],
        scratch_shapes=[pltpu.SemaphoreType.DMA((n, 6)), pltpu.SemaphoreType.DMA((n, 6)),
                        pltpu.SemaphoreType.DMA((n,))],
    )(*shards)


def _layers_half(ref, kind, start, lh):
    return ref.at[:, pl.ds(start, lh)] if kind == STK else ref.at[pl.ds(start, lh)]


def _rs_pair(gs, kinds, *, name):
    n = len(gs)

    def half_shape(g, kind):
        s = list(g.shape)
        s[1 if kind == STK else 0] //= 2
        return jax.ShapeDtypeStruct(tuple(s), g.dtype)

    def body(*refs):
        g_refs, buf_refs = refs[:n], refs[n:2 * n]
        send_sems, recv_sems = refs[2 * n:]
        x, y, cc = _mesh_pos()
        cps = []
        for i in range(n):
            lh = buf_refs[i].shape[1 if kinds[i] == STK else 0]
            cp = pltpu.make_async_remote_copy(src_ref=_layers_half(g_refs[i], kinds[i], (1 - cc) * lh, lh),
                                              dst_ref=buf_refs[i], send_sem=send_sems.at[i], recv_sem=recv_sems.at[i],
                                              device_id=(x, y, 1 - cc), device_id_type=MESH)
            cp.start()
            cps.append(cp)
        for cp in cps:
            cp.wait()

    return pl.pallas_call(
        body, name=name, in_specs=[HBM_SPEC] * n, out_specs=[HBM_SPEC] * n,
        out_shape=[half_shape(g, kd) for g, kd in zip(gs, kinds)],
        scratch_shapes=[pltpu.SemaphoreType.DMA((n,)), pltpu.SemaphoreType.DMA((n,))],
    )(*gs)


def _row_tile(rows, cols):
    return _pick(rows, max(8, (512 * 1024 // cols) // 8 * 8), 8)


def _rs_pair_add(g, buf, kind, cc, *, name):
    cols = g.shape[-1]
    pre = g.shape[0] if kind == STK else 1
    rows = buf.size // (pre * cols)
    tr = _row_tile(rows, cols)

    def body(cc_ref, g_ref, b_ref, o_ref):
        o_ref[...] = g_ref[...] + b_ref[...]

    out = pl.pallas_call(
        body, name=name,
        grid_spec=pltpu.PrefetchScalarGridSpec(
            num_scalar_prefetch=1, grid=(pre, rows // tr),
            in_specs=[pl.BlockSpec((None, None, tr, cols), lambda s, i, cc_ref: (s, cc_ref[0], i, 0)),
                      pl.BlockSpec((None, tr, cols), lambda s, i, cc_ref: (s, i, 0))],
            out_specs=pl.BlockSpec((None, tr, cols), lambda s, i, cc_ref: (s, i, 0))),
        out_shape=jax.ShapeDtypeStruct((pre, rows, cols), g.dtype),
        compiler_params=_cparams(("parallel", "parallel")),
    )(cc.reshape(1).astype(jnp.int32), g.reshape(pre, 2, rows, cols), buf.reshape(pre, rows, cols))
    return out.reshape(buf.shape)


def _rs_cross(ps, kinds, shard_shapes, *, name):
    n = len(ps)

    def body(*refs):
        p_refs, out_refs = refs[:n], refs[n:2 * n]
        send_sems, recv_sems, local_sems = refs[2 * n:]
        x, y, cc = _mesh_pos()
        chips = [(1 - x, y), (x, 1 - y), (1 - x, 1 - y)]
        slot = lambda chip: 2 * chip[0] + chip[1]
        local, sends = [], []
        for i in range(n):
            a, b = shard_shapes[i]
            cp = pltpu.make_async_copy(_chip_view(p_refs[i], kinds[i], slot((x, y)), a, b),
                                       out_refs[i].at[slot((x, y))], local_sems.at[i])
            cp.start()
            local.append(cp)
            for k, chip in enumerate(chips):
                cp = pltpu.make_async_remote_copy(src_ref=_chip_view(p_refs[i], kinds[i], slot(chip), a, b),
                                                  dst_ref=out_refs[i].at[slot((x, y))], send_sem=send_sems.at[i, k],
                                                  recv_sem=recv_sems.at[i, k], device_id=(*chip, cc),
                                                  device_id_type=MESH)
                cp.start()
                sends.append(cp)
        for i in range(n):
            for k, chip in enumerate(chips):
                landed = out_refs[i].at[slot(chip)]
                pltpu.make_async_remote_copy(src_ref=landed, dst_ref=landed, send_sem=send_sems.at[i, k],
                                             recv_sem=recv_sems.at[i, k], device_id=(*chip, cc),
                                             device_id_type=MESH).wait_recv()
        for cp in sends:
            cp.wait_send()
        for cp in local:
            cp.wait()

    def out_shape(p, kind, ab):
        lh = p.shape[1 if kind == STK else 0]
        return jax.ShapeDtypeStruct((N_CHIPS, lh) + tuple(ab), p.dtype)

    return pl.pallas_call(
        body, name=name, in_specs=[HBM_SPEC] * n, out_specs=[HBM_SPEC] * n,
        out_shape=[out_shape(p, kd, ab) for p, kd, ab in zip(ps, kinds, shard_shapes)],
        scratch_shapes=[pltpu.SemaphoreType.DMA((n, 3)), pltpu.SemaphoreType.DMA((n, 3)),
                        pltpu.SemaphoreType.DMA((n,))],
    )(*ps)


def _rs_sum4(b, *, name):
    cols = b.shape[-1]
    rows = b.size // (N_CHIPS * cols)
    tr = _row_tile(rows, cols)

    def body(b_ref, o_ref):
        o_ref[...] = ((b_ref[0] + b_ref[1]) + b_ref[2]) + b_ref[3]

    out = pl.pallas_call(
        body, name=name, grid=(rows // tr,),
        in_specs=[pl.BlockSpec((N_CHIPS, tr, cols), lambda i: (0, i, 0))],
        out_specs=pl.BlockSpec((tr, cols), lambda i: (i, 0)),
        out_shape=jax.ShapeDtypeStruct((rows, cols), b.dtype), compiler_params=_cparams(("parallel",)),
    )(b.reshape(N_CHIPS, rows, cols))
    return out.reshape(b.shape[1:])


def _rs_share(fs, *, name):
    n = len(fs)

    def body(*refs):
        f_refs, out_refs = refs[:n], refs[n:2 * n]
        send_sems, recv_sems, local_sems = refs[2 * n:]
        x, y, cc = _mesh_pos()
        cps, local = [], []
        for i in range(n):
            lh = f_refs[i].shape[0]
            mine = out_refs[i].at[pl.ds(cc * lh, lh)]
            cp = pltpu.make_async_copy(f_refs[i], mine, local_sems.at[i])
            cp.start()
            local.append(cp)
            cp = pltpu.make_async_remote_copy(src_ref=f_refs[i], dst_ref=mine, send_sem=send_sems.at[i],
                                              recv_sem=recv_sems.at[i], device_id=(x, y, 1 - cc),
                                              device_id_type=MESH)
            cp.start()
            cps.append(cp)
        for i in range(n):
            lh = f_refs[i].shape[0]
            theirs = out_refs[i].at[pl.ds((1 - cc) * lh, lh)]
            pltpu.make_async_remote_copy(src_ref=theirs, dst_ref=theirs, send_sem=send_sems.at[i],
                                         recv_sem=recv_sems.at[i], device_id=(x, y, 1 - cc),
                                         device_id_type=MESH).wait_recv()
        for cp in cps:
            cp.wait_send()
        for cp in local:
            cp.wait()

    return pl.pallas_call(
        body, name=name, in_specs=[HBM_SPEC] * n, out_specs=[HBM_SPEC] * n,
        out_shape=[jax.ShapeDtypeStruct((2 * f.shape[0],) + f.shape[1:], f.dtype) for f in fs],
        scratch_shapes=[pltpu.SemaphoreType.DMA((n,)), pltpu.SemaphoreType.DMA((n,)), pltpu.SemaphoreType.DMA((n,))],
    )(*fs)


def _reduce_scatter(gs, kinds, shard_shapes):
    cc = lax.axis_index("c")
    bufs = _rs_pair(gs, kinds, name="rs_pair")
    ps = [_rs_pair_add(g, buf, kd, cc, name=f"rs_pair_add_{i}") for i, (g, buf, kd) in enumerate(zip(gs, bufs, kinds))]
    bs = _rs_cross(ps, kinds, shard_shapes, name="rs_cross")
    fs = [_rs_sum4(b, name=f"rs_sum4_{i}") for i, b in enumerate(bs)]
    return _rs_share(fs, name="rs_share")


def _shard_shape(axis, r, c):
    return (r // N_CHIPS, c) if axis == 0 else (r, c // N_CHIPS)


def _unstack(stacked):
    ns, depth, r, cs = stacked.shape
    return stacked.transpose(1, 2, 0, 3).reshape(depth, r, ns * cs)


def _stack(fullw):
    depth, r, c = fullw.shape
    return fullw.reshape(depth, r, N_CHIPS, c // N_CHIPS).transpose(2, 0, 1, 3)


REP_SIZES = dict(mix_norm_g=1024, ssd_conv_b=2048, ssd_dt_bias=16, ssd_a_log=16, ssd_d=16, ssd_norm_g=1024,
                 conv_dw_b=1024, conv_ln_g=1024, conv_ln_b=1024, mla_q_a_g=384, mla_kv_a_g=256, mla_q_norm_g=192,
                 mla_k_norm_g=192, xattn_norm_g=1024, mem_norm_g=1024, xattn_q_norm_g=256, xattn_k_norm_g=256,
                 ffn_norm_g=1024)
REP_WIDTH = -(-sum(REP_SIZES.values()) // LANE) * LANE


def _pack_rep(d):
    flat = jnp.concatenate([d[k] for k in REP_NAMES], axis=1)
    return jnp.pad(flat, ((0, 0), (0, REP_WIDTH - flat.shape[1])))[:, None, :]


def _unpack_rep(packed):
    out, off = {}, 0
    for k in REP_NAMES:
        out[k] = packed[:, 0, off:off + REP_SIZES[k]]
        off += REP_SIZES[k]
    return out


WEIGHT_NAMES = ('mix_norm_g', 'w_in', 'ssd_conv_w', 'ssd_conv_b', 'ssd_dt_bias', 'ssd_a_log', 'ssd_d', 'ssd_norm_g',
                'ssd_w_out', 'conv_dw_w', 'conv_dw_b', 'conv_ln_g', 'conv_ln_b', 'conv_w_out', 'mla_q_a_g',
                'mla_w_q_b', 'mla_kv_a_g', 'mla_w_kv_b', 'mla_q_norm_g', 'mla_k_norm_g', 'mla_w_o', 'gate_b', 'w_out',
                'xattn_norm_g', 'mem_norm_g', 'xattn_w_q', 'xattn_w_kv', 'xattn_q_norm_g', 'xattn_k_norm_g',
                'xattn_w_o', 'ffn_norm_g', 'ffn_w_in', 'ffn_w_out')


def kernel(x, mem, positions, *rest):
    nw = len(WEIGHT_NAMES)
    weights = dict(zip(WEIGHT_NAMES, rest[:nw]))
    target = rest[nw]
    mom_m = dict(zip(WEIGHT_NAMES, rest[nw + 1:2 * nw + 1]))
    mom_v = dict(zip(WEIGHT_NAMES, rest[2 * nw + 1:3 * nw + 1]))
    depth = weights['mix_norm_g'].shape[0]

    sharded = BIG + SMALL
    kinds = [_kind(axis, _shard_shape(axis, r, c)[1]) for _, axis, r, c in sharded]
    shard_shapes = [_shard_shape(axis, r, c) for _, axis, r, c in sharded]
    shards = [weights[k].astype(BF16) for k, _, _, _ in BIG] + [weights[k] for k, _, _, _ in SMALL]
    gathered = _all_gather(shards, kinds, name="ag_weights")
    full = {k: (_unstack(w) if kd == STK else w) for (k, _, _, _), kd, w in zip(sharded, kinds, gathered)}
    full['w_in'] = _w_in_pad(full['w_in'])
    full['mla_w_q_b'] = _qb_pad(full['mla_w_q_b'])
    rep = {k: weights[k] for k in REP_NAMES}

    loss, dx, gfull, grep = _local_step(x[0], mem[0], positions[0], target[0], full, rep)
    loss = lax.psum(loss, ("x", "y", "c"))

    gs = [(_stack(gfull[k]) if kd == STK else gfull[k]) for (k, _, _, _), kd in zip(sharded, kinds)]
    summed = _reduce_scatter(gs + [_pack_rep(grep)], kinds + [REP], shard_shapes + [(1, REP_WIDTH)])
    grads = {k: g for (k, _, _, _), g in zip(sharded, summed[:-1])}
    rep_sum = summed[-1]
    grads.update(_unpack_rep(rep_sum))

    delta, new_m, new_v = {}, {}, {}
    for k, _, _, _ in sharded:
        w = weights[k]
        two_d = (w.shape[0] * w.shape[1], w.shape[2])
        d_, m_, v_ = _adamw(w.reshape(two_d), grads[k].reshape(two_d), mom_m[k].reshape(two_d),
                            mom_v[k].reshape(two_d), name="adamw_" + k)
        delta[k], new_m[k], new_v[k] = d_.reshape(w.shape), m_.reshape(w.shape), v_.reshape(w.shape)
    pack2 = lambda d: _pack_rep(d)[:, 0, :]
    d_, m_, v_ = _adamw(pack2(rep), rep_sum[:, 0, :], pack2({k: mom_m[k] for k in REP_NAMES}),
                        pack2({k: mom_v[k] for k in REP_NAMES}), name="adamw_rep")
    delta.update(_unpack_rep(d_[:, None, :]))
    new_m.update(_unpack_rep(m_[:, None, :]))
    new_v.update(_unpack_rep(v_[:, None, :]))

    return (loss, dx[None], *[grads[k] for k in WEIGHT_NAMES], *[delta[k] for k in WEIGHT_NAMES],
            *[new_m[k] for k in WEIGHT_NAMES], *[new_v[k] for k in WEIGHT_NAMES])
```

```python
import functools
import math

import jax
import jax.numpy as jnp
import numpy as np
from jax import lax
from jax.experimental import pallas as pl
from jax.experimental.pallas import tpu as pltpu

F32 = jnp.float32
BF16 = jnp.bfloat16
HI = lax.Precision.HIGHEST
MESH = pl.DeviceIdType.MESH

EPS = 1e-6
CHUNK = 64
SSD_HEADS = 16
SSD_GROUPS = 4
SSD_P = 64
SSD_N = 128
MLA_HEADS = 8
MLA_NOPE = 128
MLA_ROPE = 64
MLA_V = 128
MLA_HP = 256
X_HEADS = 4
ROPE_THETA = 10000.0
ADAM_LR, ADAM_B1, ADAM_B2, ADAM_EPS, ADAM_WD, ADAM_STEP = 0.001, 0.9, 0.999, 1e-08, 0.01, 10
LANE = 128
NEG = -1e30
VMEM_MB = 1024 * 1024


def _pick(n, cap, mult=128):
    if n <= cap:
        return n
    d = (cap // mult) * mult
    while d >= mult:
        if n % d == 0:
            return d
        d -= mult
    return n


def _cparams(sem, mb=40):
    return pltpu.CompilerParams(dimension_semantics=sem, vmem_limit_bytes=mb * VMEM_MB)


def _sigmoid(x):
    return 1.0 / (1.0 + jnp.exp(-x))


def _silu(x):
    return x * _sigmoid(x)


def _dsilu(x):
    s = _sigmoid(x)
    return s * (1.0 + x * (1.0 - s))


def _softplus(x):
    return jnp.maximum(x, 0.0) + jnp.log(1.0 + jnp.exp(-jnp.abs(x)))


def _matmul(a, b, *, ta=False, tb=False, out_dtype=F32, add=None, into=None, name):
    if ta:
        kdim, m = a.shape
    else:
        m, kdim = a.shape
    if tb:
        n, kb = b.shape
    else:
        kb, n = b.shape
    assert kb == kdim, (a.shape, b.shape, ta, tb)
    tm = _pick(m, 512, 128 if ta else 8)
    tn = _pick(n, 1024 if n <= 1024 else 512, 128)
    tk = _pick(kdim, 2048, 128)
    nk = kdim // tk
    dims = (((0 if ta else 1,), (1 if tb else 0,)), ((), ()))

    has_add = add is not None
    has_stack = into is not None and into[0] is not None

    def body(a_ref, b_ref, *rest):
        add_ref = rest[0] if has_add else None
        o_ref = rest[has_add + has_stack]
        acc = rest[has_add + has_stack + 1:]
        part = lax.dot_general(a_ref[...].astype(BF16), b_ref[...].astype(BF16), dims,
                               preferred_element_type=F32)

        def finish(total):
            if has_add:
                total = total + add_ref[...]
            o_ref[...] = total.astype(o_ref.dtype)

        if nk == 1:
            finish(part)
        else:
            acc_ref, = acc
            k = pl.program_id(2)

            @pl.when(k == 0)
            def _():
                acc_ref[...] = part

            @pl.when(k > 0)
            def _():
                acc_ref[...] += part

            @pl.when(k == nk - 1)
            def _():
                finish(acc_ref[...])

    a_spec = pl.BlockSpec((tk, tm), lambda i, j, k: (k, i)) if ta else pl.BlockSpec((tm, tk), lambda i, j, k: (i, k))
    b_spec = pl.BlockSpec((tn, tk), lambda i, j, k: (j, k)) if tb else pl.BlockSpec((tk, tn), lambda i, j, k: (k, j))
    o_spec = pl.BlockSpec((tm, tn), lambda i, j, k: (i, j))
    operands = [a, b] + ([add] if has_add else [])
    in_specs = [a_spec, b_spec] + ([o_spec] if has_add else [])
    if into is None:
        out_spec, out_shape, aliases = o_spec, jax.ShapeDtypeStruct((m, n), out_dtype), {}
    else:
        stack, layer, depth = into
        out_spec = pl.BlockSpec((None, tm, tn), lambda i, j, k: (layer, i, j))
        out_shape = jax.ShapeDtypeStruct((depth, m, n), out_dtype)
        aliases = {}
        if stack is not None:
            aliases = {len(operands): 0}
            operands.append(stack)
            in_specs.append(HBM_SPEC)
    return pl.pallas_call(
        body, name=name, grid=(m // tm, n // tn, nk),
        in_specs=in_specs, out_specs=out_spec, out_shape=out_shape, input_output_aliases=aliases,
        scratch_shapes=[] if nk == 1 else [pltpu.VMEM((tm, tn), F32)],
        compiler_params=_cparams(("parallel", "parallel", "arbitrary"), 48),
    )(*operands)


def _rms_fwd(x, g, *, out_dtype, name):
    r, w = x.shape
    tr = _pick(r, 512, 8)

    def body(x_ref, g_ref, o_ref):
        xv = x_ref[...]
        rstd = lax.rsqrt(jnp.mean(xv * xv, axis=-1, keepdims=True) + EPS)
        o_ref[...] = (xv * rstd * g_ref[...]).astype(o_ref.dtype)

    return pl.pallas_call(
        body, name=name, grid=(r // tr,),
        in_specs=[pl.BlockSpec((tr, w), lambda i: (i, 0)), pl.BlockSpec((1, w), lambda i: (0, 0))],
        out_specs=pl.BlockSpec((tr, w), lambda i: (i, 0)),
        out_shape=jax.ShapeDtypeStruct((r, w), out_dtype),
        compiler_params=_cparams(("parallel",)),
    )(x, g)


def _rms_bwd(x, g, dy, *, dx_dtype, name, add=None):
    r, w = x.shape
    tr = _pick(r, 512, 8)
    has_add = add is not None

    def body(x_ref, g_ref, dy_ref, *rest):
        if has_add:
            add_ref, dx_ref, dg_ref = rest
        else:
            dx_ref, dg_ref = rest
        xv = x_ref[...]
        dyv = dy_ref[...].astype(F32)
        rstd = lax.rsqrt(jnp.mean(xv * xv, axis=-1, keepdims=True) + EPS)
        xh = xv * rstd
        dyg = dyv * g_ref[...]
        dx = rstd * (dyg - xh * jnp.mean(dyg * xh, axis=-1, keepdims=True))
        if has_add:
            dx = dx + add_ref[...]
        dx_ref[...] = dx.astype(dx_ref.dtype)
        part = jnp.sum(dyv * xh, axis=0, keepdims=True)

        @pl.when(pl.program_id(0) == 0)
        def _():
            dg_ref[...] = part

        @pl.when(pl.program_id(0) > 0)
        def _():
            dg_ref[...] += part

    row = pl.BlockSpec((tr, w), lambda i: (i, 0))
    vec = pl.BlockSpec((1, w), lambda i: (0, 0))
    ins = [x, g, dy] + ([add] if has_add else [])
    return pl.pallas_call(
        body, name=name, grid=(r // tr,),
        in_specs=[row, vec, row] + ([row] if has_add else []),
        out_specs=[row, vec],
        out_shape=[jax.ShapeDtypeStruct((r, w), dx_dtype), jax.ShapeDtypeStruct((1, w), F32)],
        compiler_params=_cparams(("arbitrary",)),
    )(*ins)


def _ln_silu_fwd(x, g, b, *, name):
    r, w = x.shape
    tr = _pick(r, 512, 8)

    def body(x_ref, g_ref, b_ref, o_ref):
        xv = x_ref[...]
        mu = jnp.mean(xv, axis=-1, keepdims=True)
        xc = xv - mu
        rstd = lax.rsqrt(jnp.mean(xc * xc, axis=-1, keepdims=True) + EPS)
        o_ref[...] = _silu(xc * rstd * g_ref[...] + b_ref[...]).astype(o_ref.dtype)

    row = pl.BlockSpec((tr, w), lambda i: (i, 0))
    vec = pl.BlockSpec((1, w), lambda i: (0, 0))
    return pl.pallas_call(
        body, name=name, grid=(r // tr,), in_specs=[row, vec, vec], out_specs=row,
        out_shape=jax.ShapeDtypeStruct((r, w), BF16), compiler_params=_cparams(("parallel",)),
    )(x, g, b)


def _ln_silu_bwd(x, g, b, dy, *, name):
    r, w = x.shape
    tr = _pick(r, 512, 8)

    def body(x_ref, g_ref, b_ref, dy_ref, dx_ref, dg_ref, db_ref):
        xv = x_ref[...]
        mu = jnp.mean(xv, axis=-1, keepdims=True)
        xc = xv - mu
        rstd = lax.rsqrt(jnp.mean(xc * xc, axis=-1, keepdims=True) + EPS)
        xh = xc * rstd
        pre = xh * g_ref[...] + b_ref[...]
        dpre = dy_ref[...].astype(F32) * _dsilu(pre)
        dxh = dpre * g_ref[...]
        dx_ref[...] = rstd * (dxh - jnp.mean(dxh, axis=-1, keepdims=True)
                              - xh * jnp.mean(dxh * xh, axis=-1, keepdims=True))
        pg = jnp.sum(dpre * xh, axis=0, keepdims=True)
        pb = jnp.sum(dpre, axis=0, keepdims=True)

        @pl.when(pl.program_id(0) == 0)
        def _():
            dg_ref[...] = pg
            db_ref[...] = pb

        @pl.when(pl.program_id(0) > 0)
        def _():
            dg_ref[...] += pg
            db_ref[...] += pb

    row = pl.BlockSpec((tr, w), lambda i: (i, 0))
    vec = pl.BlockSpec((1, w), lambda i: (0, 0))
    return pl.pallas_call(
        body, name=name, grid=(r // tr,), in_specs=[row, vec, vec, row], out_specs=[row, vec, vec],
        out_shape=[jax.ShapeDtypeStruct((r, w), F32), jax.ShapeDtypeStruct((1, w), F32),
                   jax.ShapeDtypeStruct((1, w), F32)],
        compiler_params=_cparams(("arbitrary",)),
    )(x, g, b, dy)


CONV_PAD = 32
CONV_T = 256


def _conv_fwd(src, w, b, *, glu, act, name):
    s = src.shape[0]
    k, c = w.shape
    tc = LANE
    ncb = c // tc
    tt = _pick(s, CONV_T, 8)
    assert k - 1 <= CONV_PAD

    def body(*refs):
        if glu:
            a_ref, g_ref, w_ref, b_ref = refs[:4]
            outs = refs[4:-1]
        else:
            a_ref, w_ref, b_ref = refs[:3]
            outs = refs[3:-1]
        xp = refs[-1]
        xp[0:CONV_PAD, :] = jnp.zeros((CONV_PAD, tc), F32)
        if glu:
            xp[CONV_PAD:CONV_PAD + s, :] = a_ref[...] * _sigmoid(g_ref[...])
        else:
            xp[CONV_PAD:CONV_PAD + s, :] = a_ref[...]
        wv = w_ref[...]
        bv = b_ref[...]
        for t0 in range(0, s, tt):
            acc = jnp.zeros((tt, tc), F32) + bv
            for kk in range(k):
                off = CONV_PAD + t0 - (k - 1) + kk
                acc = acc + wv[kk:kk + 1, :] * xp[off:off + tt, :]
            outs[0][t0:t0 + tt, :] = acc
            if act:
                outs[1][t0:t0 + tt, :] = _silu(acc)

    col = pl.BlockSpec((s, tc), lambda j: (0, j))
    in_specs = [col, pl.BlockSpec((s, tc), lambda j: (0, j + ncb))] if glu else [col]
    in_specs += [pl.BlockSpec((k, tc), lambda j: (0, j)), pl.BlockSpec((1, tc), lambda j: (0, j))]
    n_out = 2 if act else 1
    res = pl.pallas_call(
        body, name=name, grid=(ncb,), in_specs=in_specs,
        out_specs=[col] * n_out,
        out_shape=[jax.ShapeDtypeStruct((s, c), F32)] * n_out,
        scratch_shapes=[pltpu.VMEM((CONV_PAD + s, tc), F32)],
        compiler_params=_cparams(("parallel",), 48),
    )(*([src, src] if glu else [src]), w, b)
    return res


def _conv_bwd(src, w, b, dy, pre, *, glu, act, name):
    s = src.shape[0]
    k, c = w.shape
    tc = LANE
    ncb = c // tc
    tt = _pick(s, CONV_T, 8)

    def body(*refs):
        i = 0
        a_ref = refs[i]; i += 1
        if glu:
            g_ref = refs[i]; i += 1
        w_ref = refs[i]; i += 1
        dy_ref = refs[i]; i += 1
        if act:
            pre_ref = refs[i]; i += 1
        da_ref = refs[i]; i += 1
        if glu:
            dg_ref = refs[i]; i += 1
        dw_ref = refs[i]; db_ref = refs[i + 1]
        xp, dp = refs[-2], refs[-1]
        xp[0:CONV_PAD, :] = jnp.zeros((CONV_PAD, tc), F32)
        if glu:
            xp[CONV_PAD:CONV_PAD + s, :] = a_ref[...] * _sigmoid(g_ref[...])
        else:
            xp[CONV_PAD:CONV_PAD + s, :] = a_ref[...]
        dp[s:s + CONV_PAD, :] = jnp.zeros((CONV_PAD, tc), F32)
        if act:
            dp[0:s, :] = dy_ref[...].astype(F32) * _dsilu(pre_ref[...])
        else:
            dp[0:s, :] = dy_ref[...].astype(F32)
        wv = w_ref[...]
        dws = [jnp.zeros((1, tc), F32) for _ in range(k)]
        dbs = jnp.zeros((1, tc), F32)
        for t0 in range(0, s, tt):
            acc = jnp.zeros((tt, tc), F32)
            dcur = dp[t0:t0 + tt, :]
            dbs = dbs + jnp.sum(dcur, axis=0, keepdims=True)
            for kk in range(k):
                acc = acc + wv[kk:kk + 1, :] * dp[t0 + (k - 1) - kk:t0 + (k - 1) - kk + tt, :]
                off = CONV_PAD + t0 - (k - 1) + kk
                dws[kk] = dws[kk] + jnp.sum(dcur * xp[off:off + tt, :], axis=0, keepdims=True)
            if glu:
                av = a_ref[t0:t0 + tt, :]
                sg = _sigmoid(g_ref[t0:t0 + tt, :])
                da_ref[t0:t0 + tt, :] = (acc * sg).astype(da_ref.dtype)
                dg_ref[t0:t0 + tt, :] = (acc * av * sg * (1.0 - sg)).astype(dg_ref.dtype)
            else:
                da_ref[t0:t0 + tt, :] = acc.astype(da_ref.dtype)
        for kk in range(k):
            dw_ref[kk:kk + 1, :] = dws[kk]
        db_ref[...] = dbs

    col = pl.BlockSpec((s, tc), lambda j: (0, j))
    in_specs = [col] + ([pl.BlockSpec((s, tc), lambda j: (0, j + ncb))] if glu else [])
    in_specs += [pl.BlockSpec((k, tc), lambda j: (0, j)), col] + ([col] if act else [])
    ins = ([src, src] if glu else [src]) + [w, dy] + ([pre] if act else [])
    out_specs = [col] + ([col] if glu else []) + [pl.BlockSpec((k, tc), lambda j: (0, j)),
                                                  pl.BlockSpec((1, tc), lambda j: (0, j))]
    out_shape = [jax.ShapeDtypeStruct((s, c), BF16)] * (2 if glu else 1) + [
        jax.ShapeDtypeStruct((k, c), F32), jax.ShapeDtypeStruct((1, c), F32)]
    return pl.pallas_call(
        body, name=name, grid=(ncb,), in_specs=in_specs, out_specs=out_specs, out_shape=out_shape,
        scratch_shapes=[pltpu.VMEM((CONV_PAD + s, tc), F32), pltpu.VMEM((s + CONV_PAD, tc), F32)],
        compiler_params=_cparams(("parallel",), 56),
    )(*ins)


def _ssd_consts():
    e = np.zeros((LANE, SSD_HEADS * SSD_P), np.float32)
    for h in range(SSD_HEADS):
        e[h, h * SSD_P:(h + 1) * SSD_P] = 1.0
    ltri = np.tril(np.ones((CHUNK, CHUNK), np.float32))
    return jnp.asarray(e), jnp.asarray(e.T.copy()), jnp.asarray(ltri), jnp.asarray(ltri.T.copy())


def _ssd_chunk_terms(dtr_ref, dtrt_ref, bias_ref, biast_ref, alog_ref, alogt_ref, e_ref, ltri_ref, utri_ref):
    a_neg = -jnp.exp(alog_ref[...])
    dt = _softplus(dtr_ref[...] + bias_ref[...])
    a = dt * a_neg
    s = jnp.dot(ltri_ref[...], a, precision=HI, preferred_element_type=F32)
    dtt = _softplus(dtrt_ref[...] + biast_ref[...])
    st = jnp.dot(dtt * (-jnp.exp(alogt_ref[...])), utri_ref[...], precision=HI, preferred_element_type=F32)
    ev = e_ref[...]
    s_x = jnp.dot(s, ev, precision=HI, preferred_element_type=F32)
    dt_x = jnp.dot(dt, ev, precision=HI, preferred_element_type=F32)
    return a_neg, dt, s, st, s_x, dt_x


def _ssd_decay(s, st, h, tril):
    seg = s[:, h:h + 1] - st[h:h + 1, :]
    return jnp.exp(jnp.where(tril, seg, NEG))


def _head_masks():
    lane = lax.broadcasted_iota(jnp.int32, (1, SSD_P * 4), 1)
    return [((lane >= r * SSD_P) & (lane < (r + 1) * SSD_P)).astype(F32) for r in range(4)]


def _ssd_scan_fwd(xbc, dtr, dtrt, bias, biast, alog, alogt, dskip, *, name):
    s_len = xbc.shape[0]
    nc = s_len // CHUNK
    e, et, ltri, utri = _ssd_consts()
    gw = SSD_P * 4

    def body(xs_ref, b_ref, c_ref, dtr_ref, dtrt_ref, bias_ref, biast_ref, alog_ref, alogt_ref, d_ref,
             e_ref, ltri_ref, utri_ref, y_ref, prev_ref, state):
        @pl.when(pl.program_id(0) == 0)
        def _():
            state[...] = jnp.zeros_like(state)

        a_neg, dt, s, st, s_x, dt_x = _ssd_chunk_terms(dtr_ref, dtrt_ref, bias_ref, biast_ref, alog_ref,
                                                       alogt_ref, e_ref, ltri_ref, utri_ref)
        s_last = s_x[CHUNK - 1:CHUNK, :]
        es_x = jnp.exp(s_x)
        w_x = jnp.exp(s_last - s_x)
        cd_x = jnp.exp(s_last)
        d_x = jnp.dot(jnp.broadcast_to(d_ref[...], (8, LANE)), e_ref[...], precision=HI,
                      preferred_element_type=F32)[0:1, :]
        xs = xs_ref[...]
        xv = xs * dt_x
        row = lax.broadcasted_iota(jnp.int32, (CHUNK, CHUNK), 0)
        colm = lax.broadcasted_iota(jnp.int32, (CHUNK, CHUNK), 1)
        tril = colm <= row
        masks = _head_masks()
        for g in range(SSD_GROUPS):
            gs = slice(g * gw, (g + 1) * gw)
            bg = b_ref[:, g * SSD_N:(g + 1) * SSD_N].astype(BF16)
            cg = c_ref[:, g * SSD_N:(g + 1) * SSD_N].astype(BF16)
            xg = xv[:, gs]
            hg = state[g]
            prev_ref[g] = hg
            cb = lax.dot_general(cg, bg, (((1,), (1,)), ((), ())), preferred_element_type=F32)
            yg = jnp.dot(cg, hg.astype(BF16), preferred_element_type=F32) * es_x[:, gs]
            for r in range(4):
                m = (cb * _ssd_decay(s, st, g * 4 + r, tril)).astype(BF16)
                yg = yg + jnp.dot(m, (xg * masks[r]).astype(BF16), preferred_element_type=F32)
            y_ref[:, gs] = yg + d_x[:, gs] * xs[:, gs]
            upd = lax.dot_general(bg, (xg * w_x[:, gs]).astype(BF16), (((0,), (0,)), ((), ())),
                                  preferred_element_type=F32)
            state[g] = hg * cd_x[:, gs] + upd

    nh = LANE
    chunk_row = lambda w, cb: pl.BlockSpec((CHUNK, w), lambda i, cb=cb: (i, cb))
    full = lambda a: pl.BlockSpec(a.shape, lambda i: (0,) * a.ndim)
    in_specs = [chunk_row(1024, 0), chunk_row(512, 2), chunk_row(512, 3), chunk_row(nh, 0),
                pl.BlockSpec((None, nh, CHUNK), lambda i: (i, 0, 0)),
                full(bias), full(biast), full(alog), full(alogt), full(dskip), full(e), full(ltri), full(utri)]
    return pl.pallas_call(
        body, name=name, grid=(nc,), in_specs=in_specs,
        out_specs=[pl.BlockSpec((CHUNK, 1024), lambda i: (i, 0)),
                   pl.BlockSpec((None, SSD_GROUPS, SSD_N, gw), lambda i: (i, 0, 0, 0))],
        out_shape=[jax.ShapeDtypeStruct((s_len, 1024), F32),
                   jax.ShapeDtypeStruct((nc, SSD_GROUPS, SSD_N, gw), F32)],
        scratch_shapes=[pltpu.VMEM((SSD_GROUPS, SSD_N, gw), F32)],
        compiler_params=_cparams(("arbitrary",)),
    )(xbc, xbc, xbc, dtr, dtrt, bias, biast, alog, alogt, dskip, e, ltri, utri)


def _ssd_scan_bwd(xbc, dtr, dtrt, bias, biast, alog, alogt, dskip, prev, dy, *, name):
    s_len = xbc.shape[0]
    nc = s_len // CHUNK
    e, et, ltri, utri = _ssd_consts()
    gw = SSD_P * 4

    def body(xs_ref, b_ref, c_ref, dtr_ref, dtrt_ref, bias_ref, biast_ref, alog_ref, alogt_ref, d_ref,
             e_ref, et_ref, ltri_ref, utri_ref, prev_ref, dy_ref,
             dxs_ref, db_ref, dc_ref, ddtr_ref, dalog_ref, dbias_ref, dd_ref, dstate):
        step = pl.program_id(0)

        @pl.when(step == 0)
        def _():
            dstate[...] = jnp.zeros_like(dstate)

        a_neg, dt, s, st, s_x, dt_x = _ssd_chunk_terms(dtr_ref, dtrt_ref, bias_ref, biast_ref, alog_ref,
                                                       alogt_ref, e_ref, ltri_ref, utri_ref)
        s_last = s_x[CHUNK - 1:CHUNK, :]
        es_x = jnp.exp(s_x)
        w_x = jnp.exp(s_last - s_x)
        cd_x = jnp.exp(s_last)
        d_x = jnp.dot(jnp.broadcast_to(d_ref[...], (8, LANE)), e_ref[...], precision=HI,
                      preferred_element_type=F32)[0:1, :]
        xs = xs_ref[...]
        xv = xs * dt_x
        dyv = dy_ref[...]
        row = lax.broadcasted_iota(jnp.int32, (CHUNK, CHUNK), 0)
        colm = lax.broadcasted_iota(jnp.int32, (CHUNK, CHUNK), 1)
        tril = colm <= row
        masks = _head_masks()
        is_last = lax.broadcasted_iota(jnp.int32, (CHUNK, 1), 0) == CHUNK - 1
        nt = (((1,), (1,)), ((), ()))
        tn = (((0,), (0,)), ((), ()))
        ds_parts, ddt_parts = [], []
        head_lane = lax.broadcasted_iota(jnp.int32, (CHUNK, LANE), 1)
        ones = jnp.ones((CHUNK, LANE), F32)
        ds_diag = jnp.zeros((CHUNK, LANE), F32)
        for g in range(SSD_GROUPS):
            gs = slice(g * gw, (g + 1) * gw)
            bg = b_ref[:, g * SSD_N:(g + 1) * SSD_N].astype(BF16)
            cg = c_ref[:, g * SSD_N:(g + 1) * SSD_N].astype(BF16)
            xg = xv[:, gs]
            xgb = xg.astype(BF16)
            hg = prev_ref[g]
            hgb = hg.astype(BF16)
            dsg = dstate[g]
            dsgb = dsg.astype(BF16)
            dyg = dyv[:, gs]
            dye = (dyg * es_x[:, gs]).astype(BF16)
            xw = (xg * w_x[:, gs]).astype(BF16)
            cb = lax.dot_general(cg, bg, nt, preferred_element_type=F32)
            dcg = lax.dot_general(dye, hgb, nt, preferred_element_type=F32)
            dh = lax.dot_general(cg, dye, tn, preferred_element_type=F32)
            bds = jnp.dot(bg, dsgb, preferred_element_type=F32)
            yoff = es_x[:, gs] * jnp.dot(cg, hgb, preferred_element_type=F32)
            dx_state = w_x[:, gs] * bds
            dbg = lax.dot_general(xw, dsgb, nt, preferred_element_type=F32)
            dxd = jnp.zeros((CHUNK, gw), F32)
            dcb = jnp.zeros((CHUNK, CHUNK), F32)
            for r in range(4):
                dec = _ssd_decay(s, st, g * 4 + r, tril)
                mf = cb * dec
                m = mf.astype(BF16)
                dym = (dyg * masks[r]).astype(BF16)
                dm = lax.dot_general(dym, xgb, nt, preferred_element_type=F32)
                dxd = dxd + lax.dot_general(m, dym, tn, preferred_element_type=F32)
                dcb = dcb + dm * dec
                qm = dm * mf
                rc = (jnp.sum(qm, axis=1, keepdims=True)
                      - lax.dot_general(qm, ones, tn, precision=HI, preferred_element_type=F32))
                ds_diag = ds_diag + jnp.where(head_lane == g * 4 + r, rc, 0.0)
            dcbb = dcb.astype(BF16)
            dcg = dcg + jnp.dot(dcbb, bg, preferred_element_type=F32)
            dbg = dbg + lax.dot_general(dcbb, cg, tn, preferred_element_type=F32)
            dxg = dxd + dx_state
            extra = (jnp.sum(xg * dx_state, axis=0, keepdims=True)
                     + cd_x[:, gs] * jnp.sum(dsg * hg, axis=0, keepdims=True))
            ds_parts.append(dyg * yoff - xg * dx_state + jnp.where(is_last, extra, 0.0))
            ddt_parts.append(dxg * xs[:, gs])
            dxs_ref[:, gs] = dxg * dt_x[:, gs] + d_x[:, gs] * dyg
            db_ref[:, g * SSD_N:(g + 1) * SSD_N] = dbg
            dc_ref[:, g * SSD_N:(g + 1) * SSD_N] = dcg
            dstate[g] = cd_x[:, gs] * dsg + dh
        etv = et_ref[...]
        ds = ds_diag + jnp.dot(jnp.concatenate(ds_parts, axis=1), etv, precision=HI, preferred_element_type=F32)
        da = jnp.dot(utri_ref[...], ds, precision=HI, preferred_element_type=F32)
        ddt = da * a_neg + jnp.dot(jnp.concatenate(ddt_parts, axis=1), etv, precision=HI,
                                   preferred_element_type=F32)
        ddtr = ddt * _sigmoid(dtr_ref[...] + bias_ref[...])
        ddtr_ref[...] = ddtr.astype(ddtr_ref.dtype)
        p_alog = jnp.sum(da * dt, axis=0, keepdims=True) * a_neg
        p_bias = jnp.sum(ddtr, axis=0, keepdims=True)
        p_d = jnp.dot(jnp.broadcast_to(jnp.sum(dyv * xs, axis=0, keepdims=True), (8, SSD_HEADS * SSD_P)), etv,
                      precision=HI, preferred_element_type=F32)[0:1, :]

        @pl.when(step == 0)
        def _():
            dalog_ref[...] = p_alog
            dbias_ref[...] = p_bias
            dd_ref[...] = p_d

        @pl.when(step > 0)
        def _():
            dalog_ref[...] += p_alog
            dbias_ref[...] += p_bias
            dd_ref[...] += p_d

    nh = LANE
    rev = lambda i: nc - 1 - i
    chunk_row = lambda w, cb: pl.BlockSpec((CHUNK, w), lambda i, cb=cb: (rev(i), cb))
    full = lambda a: pl.BlockSpec(a.shape, lambda i: (0,) * a.ndim)
    vec = pl.BlockSpec((1, nh), lambda i: (0, 0))
    in_specs = [chunk_row(1024, 0), chunk_row(512, 2), chunk_row(512, 3), chunk_row(nh, 0),
                pl.BlockSpec((None, nh, CHUNK), lambda i: (rev(i), 0, 0)),
                full(bias), full(biast), full(alog), full(alogt), full(dskip), full(e), full(et), full(ltri),
                full(utri),
                pl.BlockSpec((None, SSD_GROUPS, SSD_N, gw), lambda i: (rev(i), 0, 0, 0)), chunk_row(1024, 0)]
    return pl.pallas_call(
        body, name=name, grid=(nc,), in_specs=in_specs,
        out_specs=[chunk_row(1024, 0), chunk_row(512, 0), chunk_row(512, 0), chunk_row(nh, 0), vec, vec, vec],
        out_shape=[jax.ShapeDtypeStruct((s_len, 1024), F32), jax.ShapeDtypeStruct((s_len, 512), F32),
                   jax.ShapeDtypeStruct((s_len, 512), F32), jax.ShapeDtypeStruct((s_len, nh), BF16),
                   jax.ShapeDtypeStruct((1, nh), F32), jax.ShapeDtypeStruct((1, nh), F32),
                   jax.ShapeDtypeStruct((1, nh), F32)],
        scratch_shapes=[pltpu.VMEM((SSD_GROUPS, SSD_N, gw), F32)],
        compiler_params=_cparams(("arbitrary",)),
    )(xbc, xbc, xbc, dtr, dtrt, bias, biast, alog, alogt, dskip, e, et, ltri, utri, prev, dy)


def _ssd_gate_fwd(y, z, g, *, name):
    r, w = y.shape
    tr = _pick(r, 512, 8)
    gw = w // SSD_GROUPS

    def body(y_ref, z_ref, g_ref, o_ref):
        for k in range(SSD_GROUPS):
            cs = slice(k * gw, (k + 1) * gw)
            t = y_ref[:, cs] * _silu(z_ref[:, cs])
            rstd = lax.rsqrt(jnp.mean(t * t, axis=-1, keepdims=True) + EPS)
            o_ref[:, cs] = (t * rstd * g_ref[:, cs]).astype(o_ref.dtype)

    row = pl.BlockSpec((tr, w), lambda i: (i, 0))
    vec = pl.BlockSpec((1, w), lambda i: (0, 0))
    return pl.pallas_call(
        body, name=name, grid=(r // tr,), in_specs=[row, row, vec], out_specs=row,
        out_shape=jax.ShapeDtypeStruct((r, w), BF16), compiler_params=_cparams(("parallel",)),
    )(y, z, g)


def _ssd_gate_bwd(y, z, g, do, *, name):
    r, w = y.shape
    tr = _pick(r, 512, 8)
    gw = w // SSD_GROUPS

    def body(y_ref, z_ref, g_ref, do_ref, dy_ref, dz_ref, dg_ref):
        parts = []
        for k in range(SSD_GROUPS):
            cs = slice(k * gw, (k + 1) * gw)
            yv = y_ref[:, cs]
            zv = z_ref[:, cs]
            sz = _silu(zv)
            t = yv * sz
            rstd = lax.rsqrt(jnp.mean(t * t, axis=-1, keepdims=True) + EPS)
            th = t * rstd
            dov = do_ref[:, cs].astype(F32)
            dog = dov * g_ref[:, cs]
            dt = rstd * (dog - th * jnp.mean(dog * th, axis=-1, keepdims=True))
            dy_ref[:, cs] = dt * sz
            dz_ref[:, cs] = (dt * yv * _dsilu(zv)).astype(dz_ref.dtype)
            parts.append(jnp.sum(dov * th, axis=0, keepdims=True))
        pg = jnp.concatenate(parts, axis=1)

        @pl.when(pl.program_id(0) == 0)
        def _():
            dg_ref[...] = pg

        @pl.when(pl.program_id(0) > 0)
        def _():
            dg_ref[...] += pg

    row = pl.BlockSpec((tr, w), lambda i: (i, 0))
    vec = pl.BlockSpec((1, w), lambda i: (0, 0))
    return pl.pallas_call(
        body, name=name, grid=(r // tr,), in_specs=[row, row, vec, row], out_specs=[row, row, vec],
        out_shape=[jax.ShapeDtypeStruct((r, w), F32), jax.ShapeDtypeStruct((r, w), BF16),
                   jax.ShapeDtypeStruct((1, w), F32)],
        compiler_params=_cparams(("arbitrary",)),
    )(y, z, g, do)


def _rope_swap(x):
    lane = lax.broadcasted_iota(jnp.int32, x.shape, 1)
    lo = pltpu.roll(x, 96, 1)
    hi = pltpu.roll(x, 32, 1)
    return jnp.where(lane < 32, lo, jnp.where(lane < 64, hi, 0.0))


def _norm_part(v, g, n):
    rstd = lax.rsqrt(jnp.sum(v * v, axis=-1, keepdims=True) * (1.0 / n) + EPS)
    xh = v * rstd
    return xh * g, xh, rstd


def _norm_part_bwd(dout, g, xh, rstd, n):
    dg = dout * g
    return rstd * (dg - xh * (jnp.sum(dg * xh, axis=-1, keepdims=True) * (1.0 / n)))


def _mla_prep_fwd(q, kv, krr, cs, sn, gq, gk, *, name):
    s = q.shape[0]
    tr = _pick(s, 256, 8)
    hp = MLA_HP

    def body(q_ref, kv_ref, krr_ref, cs_ref, sn_ref, gq_ref, gk_ref, qf_ref, kf_ref, v_ref):
        csv, snv = cs_ref[...], sn_ref[...]
        gqn, gqr = gq_ref[:, 0:128], gq_ref[:, 128:256]
        gkn, gkr = gk_ref[:, 0:128], gk_ref[:, 128:256]
        kr, _, _ = _norm_part(krr_ref[...], gkr, MLA_ROPE)
        kr = (kr * csv + _rope_swap(kr) * snv).astype(BF16)
        for h in range(MLA_HEADS):
            qn, _, _ = _norm_part(q_ref[:, h * hp:h * hp + 128], gqn, MLA_NOPE)
            qr, _, _ = _norm_part(q_ref[:, h * hp + 128:(h + 1) * hp], gqr, MLA_ROPE)
            qr = qr * csv + _rope_swap(qr) * snv
            qf_ref[h, :, 0:128] = qn.astype(BF16)
            qf_ref[h, :, 128:256] = qr.astype(BF16)
            kn, _, _ = _norm_part(kv_ref[:, h * hp:h * hp + 128], gkn, MLA_NOPE)
            kf_ref[h, :, 0:128] = kn.astype(BF16)
            kf_ref[h, :, 128:256] = kr
            v_ref[h] = kv_ref[:, h * hp + 128:(h + 1) * hp].astype(BF16)

    row = lambda w: pl.BlockSpec((tr, w), lambda i: (i, 0))
    vec = pl.BlockSpec((1, hp), lambda i: (0, 0))
    hrow = lambda w: pl.BlockSpec((MLA_HEADS, tr, w), lambda i: (0, i, 0))
    return pl.pallas_call(
        body, name=name, grid=(s // tr,),
        in_specs=[row(MLA_HEADS * hp), row(MLA_HEADS * hp), row(128), row(128), row(128), vec, vec],
        out_specs=[hrow(hp), hrow(hp), hrow(128)],
        out_shape=[jax.ShapeDtypeStruct((MLA_HEADS, s, hp), BF16), jax.ShapeDtypeStruct((MLA_HEADS, s, hp), BF16),
                   jax.ShapeDtypeStruct((MLA_HEADS, s, 128), BF16)],
        compiler_params=_cparams(("parallel",)),
    )(q, kv, krr, cs, sn, gq, gk)


def _mla_prep_bwd(q, kv, krr, cs, sn, gq, gk, dqf, dkf, dv, *, name):
    s = q.shape[0]
    tr = _pick(s, 256, 8)
    hp = MLA_HP

    def body(q_ref, kv_ref, krr_ref, cs_ref, sn_ref, gq_ref, gk_ref, dqf_ref, dkf_ref, dv_ref,
             dq_ref, dkv_ref, dkrr_ref, dgq_ref, dgk_ref):
        csv, snv = cs_ref[...], sn_ref[...]
        gqn, gqr = gq_ref[:, 0:128], gq_ref[:, 128:256]
        gkn, gkr = gk_ref[:, 0:128], gk_ref[:, 128:256]
        _, krh, krs = _norm_part(krr_ref[...], gkr, MLA_ROPE)
        dkr_sum = jnp.zeros((tr, 128), F32)
        pgqn = jnp.zeros((1, 128), F32)
        pgqr = jnp.zeros((1, 128), F32)
        pgkn = jnp.zeros((1, 128), F32)
        for h in range(MLA_HEADS):
            _, qnh, qns = _norm_part(q_ref[:, h * hp:h * hp + 128], gqn, MLA_NOPE)
            _, qrh, qrs = _norm_part(q_ref[:, h * hp + 128:(h + 1) * hp], gqr, MLA_ROPE)
            dqn = dqf_ref[h, :, 0:128]
            drr = dqf_ref[h, :, 128:256]
            dqr = drr * csv + _rope_swap(drr * snv)
            dq_ref[:, h * hp:h * hp + 128] = _norm_part_bwd(dqn, gqn, qnh, qns, MLA_NOPE).astype(dq_ref.dtype)
            dq_ref[:, h * hp + 128:(h + 1) * hp] = _norm_part_bwd(dqr, gqr, qrh, qrs, MLA_ROPE).astype(dq_ref.dtype)
            pgqn = pgqn + jnp.sum(dqn * qnh, axis=0, keepdims=True)
            pgqr = pgqr + jnp.sum(dqr * qrh, axis=0, keepdims=True)
            _, knh, kns = _norm_part(kv_ref[:, h * hp:h * hp + 128], gkn, MLA_NOPE)
            dkn = dkf_ref[h, :, 0:128]
            dkv_ref[:, h * hp:h * hp + 128] = _norm_part_bwd(dkn, gkn, knh, kns, MLA_NOPE).astype(dkv_ref.dtype)
            dkv_ref[:, h * hp + 128:(h + 1) * hp] = dv_ref[h].astype(dkv_ref.dtype)
            pgkn = pgkn + jnp.sum(dkn * knh, axis=0, keepdims=True)
            dkr_sum = dkr_sum + dkf_ref[h, :, 128:256]
        dkr = dkr_sum * csv + _rope_swap(dkr_sum * snv)
        dkrr_ref[...] = _norm_part_bwd(dkr, gkr, krh, krs, MLA_ROPE).astype(dkrr_ref.dtype)
        pgkr = jnp.sum(dkr * krh, axis=0, keepdims=True)
        pq = jnp.concatenate([pgqn, pgqr], axis=1)
        pk = jnp.concatenate([pgkn, pgkr], axis=1)

        @pl.when(pl.program_id(0) == 0)
        def _():
            dgq_ref[...] = pq
            dgk_ref[...] = pk

        @pl.when(pl.program_id(0) > 0)
        def _():
            dgq_ref[...] += pq
            dgk_ref[...] += pk

    row = lambda w: pl.BlockSpec((tr, w), lambda i: (i, 0))
    vec = pl.BlockSpec((1, hp), lambda i: (0, 0))
    hrow = lambda w: pl.BlockSpec((MLA_HEADS, tr, w), lambda i: (0, i, 0))
    return pl.pallas_call(
        body, name=name, grid=(s // tr,),
        in_specs=[row(MLA_HEADS * hp), row(MLA_HEADS * hp), row(128), row(128), row(128), vec, vec,
                  hrow(hp), hrow(hp), hrow(128)],
        out_specs=[row(MLA_HEADS * hp), row(MLA_HEADS * hp), row(128), vec, vec],
        out_shape=[jax.ShapeDtypeStruct((s, MLA_HEADS * hp), BF16), jax.ShapeDtypeStruct((s, MLA_HEADS * hp), BF16),
                   jax.ShapeDtypeStruct((s, 128), BF16), jax.ShapeDtypeStruct((1, hp), F32),
                   jax.ShapeDtypeStruct((1, hp), F32)],
        compiler_params=_cparams(("arbitrary",), 48),
    )(q, kv, krr, cs, sn, gq, gk, dqf, dkf, dv)


ATT_T = 512


def _chunk_mask(t):
    r = lax.shift_right_logical(lax.broadcasted_iota(jnp.int32, (t, t), 0), 6)
    c = lax.shift_right_logical(lax.broadcasted_iota(jnp.int32, (t, t), 1), 6)
    return c <= r


def _mla_attn_fwd(qf, kf, v, *, name):
    nh, s, hp = qf.shape
    t = _pick(s, ATT_T, CHUNK)
    scale = (MLA_NOPE + MLA_ROPE) ** -0.5
    nt = (((1,), (1,)), ((), ()))

    def body(q_ref, k_ref, v_ref, o_ref, lse_ref):
        i = pl.program_id(1)
        q = q_ref[...]

        def block(j, carry, masked):
            m, l, acc = carry
            start = pl.multiple_of(j * t, t)
            k = k_ref[pl.ds(start, t), :]
            sc = lax.dot_general(q, k, nt, preferred_element_type=F32) * scale
            if masked:
                sc = jnp.where(_chunk_mask(t), sc, NEG)
            m_new = jnp.maximum(m, jnp.max(sc, axis=-1, keepdims=True))
            alpha = jnp.exp(m - m_new)
            p = jnp.exp(sc - m_new)
            l = alpha * l + jnp.sum(p, axis=-1, keepdims=True)
            acc = alpha * acc + jnp.dot(p.astype(BF16), v_ref[pl.ds(start, t), :], preferred_element_type=F32)
            return m_new, l, acc

        init = (jnp.full((t, 1), NEG, F32), jnp.zeros((t, 1), F32), jnp.zeros((t, MLA_V), F32))
        carry = lax.fori_loop(0, i, lambda j, c: block(j, c, False), init)
        m, l, acc = block(i, carry, True)
        o_ref[...] = acc / l
        lse_ref[...] = m + jnp.log(l)

    return pl.pallas_call(
        body, name=name, grid=(nh, s // t),
        in_specs=[pl.BlockSpec((None, t, hp), lambda h, i: (h, i, 0)),
                  pl.BlockSpec((None, s, hp), lambda h, i: (h, 0, 0)),
                  pl.BlockSpec((None, s, MLA_V), lambda h, i: (h, 0, 0))],
        out_specs=[pl.BlockSpec((t, MLA_V), lambda h, i: (i, h)),
                   pl.BlockSpec((None, t, 1), lambda h, i: (h, i, 0))],
        out_shape=[jax.ShapeDtypeStruct((s, nh * MLA_V), F32), jax.ShapeDtypeStruct((nh, s, 1), F32)],
        compiler_params=_cparams(("parallel", "arbitrary"), 48),
    )(qf, kf, v)


def _mla_attn_bwd(qf, kf, v, o, lse, do, *, name):
    nh, s, hp = qf.shape
    t = _pick(s, ATT_T, CHUNK)
    nb = s // t
    scale = (MLA_NOPE + MLA_ROPE) ** -0.5
    nt = (((1,), (1,)), ((), ()))
    tn = (((0,), (0,)), ((), ()))

    def body(q_ref, k_ref, v_ref, o_ref, lse_ref, do_ref, dq_ref, dk_ref, dv_ref, delta):
        j = pl.program_id(1)

        @pl.when(j == 0)
        def _():
            dq_ref[...] = jnp.zeros_like(dq_ref)
            delta[...] = jnp.sum(do_ref[...] * o_ref[...], axis=-1, keepdims=True)

        k = k_ref[...]
        vv = v_ref[...]

        def block(i, carry, masked):
            dk, dv = carry
            start = pl.multiple_of(i * t, t)
            q = q_ref[pl.ds(start, t), :]
            dob = do_ref[pl.ds(start, t), :].astype(BF16)
            sc = lax.dot_general(q, k, nt, preferred_element_type=F32) * scale
            if masked:
                sc = jnp.where(_chunk_mask(t), sc, NEG)
            p = jnp.exp(sc - lse_ref[pl.ds(start, t), :])
            dp = lax.dot_general(dob, vv, nt, preferred_element_type=F32)
            ds = (p * (dp - delta[pl.ds(start, t), :]) * scale).astype(BF16)
            dv = dv + lax.dot_general(p.astype(BF16), dob, tn, preferred_element_type=F32)
            dk = dk + lax.dot_general(ds, q, tn, preferred_element_type=F32)
            dq_ref[pl.ds(start, t), :] += jnp.dot(ds, k, preferred_element_type=F32)
            return dk, dv

        init = (jnp.zeros((t, hp), F32), jnp.zeros((t, MLA_V), F32))
        carry = block(j, init, True)
        dk, dv = lax.fori_loop(j + 1, nb, lambda i, c: block(i, c, False), carry)
        dk_ref[...] = dk
        dv_ref[...] = dv

    whole = lambda w: pl.BlockSpec((None, s, w), lambda h, j: (h, 0, 0))
    blk = lambda w: pl.BlockSpec((None, t, w), lambda h, j: (h, j, 0))
    colh = pl.BlockSpec((s, MLA_V), lambda h, j: (0, h))
    return pl.pallas_call(
        body, name=name, grid=(nh, nb),
        in_specs=[whole(hp), blk(hp), blk(MLA_V), colh, whole(1), colh],
        out_specs=[whole(hp), blk(hp), blk(MLA_V)],
        out_shape=[jax.ShapeDtypeStruct((nh, s, hp), F32), jax.ShapeDtypeStruct((nh, s, hp), F32),
                   jax.ShapeDtypeStruct((nh, s, MLA_V), F32)],
        scratch_shapes=[pltpu.VMEM((s, 1), F32)],
        compiler_params=_cparams(("parallel", "arbitrary"), 56),
    )(qf, kf, v, o, lse, do)


def _merge_fwd(gl, gb, ys, yc, ym, *, name):
    s, d = ys.shape
    tr = _pick(s, 256, 8)

    def body(gl_ref, gb_ref, ys_ref, yc_ref, ym_ref, o_ref):
        acc = jnp.zeros((tr, d), F32)
        for k, y_ref in enumerate((ys_ref, yc_ref, ym_ref)):
            gt = _sigmoid(gl_ref[:, k * d:(k + 1) * d] + gb_ref[:, k * d:(k + 1) * d])
            acc = acc + gt * y_ref[...]
        o_ref[...] = acc.astype(o_ref.dtype)

    row = lambda w: pl.BlockSpec((tr, w), lambda i: (i, 0))
    return pl.pallas_call(
        body, name=name, grid=(s // tr,),
        in_specs=[row(3 * d), pl.BlockSpec((1, 3 * d), lambda i: (0, 0)), row(d), row(d), row(d)],
        out_specs=row(d), out_shape=jax.ShapeDtypeStruct((s, d), BF16),
        compiler_params=_cparams(("parallel",)),
    )(gl, gb, ys, yc, ym)


def _merge_bwd(gl, gb, ys, yc, ym, dm, *, name):
    s, d = ys.shape
    tr = _pick(s, 256, 8)

    def body(gl_ref, gb_ref, ys_ref, yc_ref, ym_ref, dm_ref, dgl_ref, dgb_ref, dys_ref, dyc_ref, dym_ref):
        dmv = dm_ref[...]
        parts = []
        for k, (y_ref, dy_ref) in enumerate(((ys_ref, dys_ref), (yc_ref, dyc_ref), (ym_ref, dym_ref))):
            gt = _sigmoid(gl_ref[:, k * d:(k + 1) * d] + gb_ref[:, k * d:(k + 1) * d])
            dy_ref[...] = (gt * dmv).astype(dy_ref.dtype)
            dl = dmv * y_ref[...] * gt * (1.0 - gt)
            dgl_ref[:, k * d:(k + 1) * d] = dl.astype(dgl_ref.dtype)
            parts.append(jnp.sum(dl, axis=0, keepdims=True))
        pb = jnp.concatenate(parts, axis=1)

        @pl.when(pl.program_id(0) == 0)
        def _():
            dgb_ref[...] = pb

        @pl.when(pl.program_id(0) > 0)
        def _():
            dgb_ref[...] += pb

    row = lambda w: pl.BlockSpec((tr, w), lambda i: (i, 0))
    vec = pl.BlockSpec((1, 3 * d), lambda i: (0, 0))
    return pl.pallas_call(
        body, name=name, grid=(s // tr,),
        in_specs=[row(3 * d), vec, row(d), row(d), row(d), row(d)],
        out_specs=[row(3 * d), vec, row(d), row(d), row(d)],
        out_shape=[jax.ShapeDtypeStruct((s, 3 * d), BF16), jax.ShapeDtypeStruct((1, 3 * d), F32)]
        + [jax.ShapeDtypeStruct((s, d), BF16)] * 3,
        compiler_params=_cparams(("arbitrary",)),
    )(gl, gb, ys, yc, ym, dm)


def _xattn_fwd(q, k, v, gq, *, name):
    s, d = q.shape
    dh = d // X_HEADS
    tr = _pick(s, 512, 8)
    scale = dh ** -0.5
    nt = (((1,), (1,)), ((), ()))

    def body(q_ref, k_ref, v_ref, gq_ref, o_ref):
        for h in range(X_HEADS):
            cs = slice(h * dh, (h + 1) * dh)
            qn, _, _ = _norm_part(q_ref[:, cs], gq_ref[...], dh)
            sc = lax.dot_general(qn.astype(BF16), k_ref[:, cs], nt, preferred_element_type=F32) * scale
            p = jnp.exp(sc - jnp.max(sc, axis=-1, keepdims=True))
            p = p / jnp.sum(p, axis=-1, keepdims=True)
            o_ref[:, cs] = jnp.dot(p.astype(BF16), v_ref[:, cs], preferred_element_type=F32)

    row = pl.BlockSpec((tr, d), lambda i: (i, 0))
    mem = pl.BlockSpec(k.shape, lambda i: (0, 0))
    return pl.pallas_call(
        body, name=name, grid=(s // tr,),
        in_specs=[row, mem, mem, pl.BlockSpec((1, dh), lambda i: (0, 0))], out_specs=row,
        out_shape=jax.ShapeDtypeStruct((s, d), F32), compiler_params=_cparams(("parallel",)),
    )(q, k, v, gq)


def _xattn_bwd(q, k, v, gq, do, *, name):
    s, d = q.shape
    dh = d // X_HEADS
    tr = _pick(s, 512, 8)
    scale = dh ** -0.5
    nt = (((1,), (1,)), ((), ()))
    tn = (((0,), (0,)), ((), ()))

    def body(q_ref, k_ref, v_ref, gq_ref, do_ref, dq_ref, dk_ref, dv_ref, dgq_ref):
        first = pl.program_id(0) == 0
        pg = jnp.zeros((1, dh), F32)
        for h in range(X_HEADS):
            cs = slice(h * dh, (h + 1) * dh)
            qn, qh, qs = _norm_part(q_ref[:, cs], gq_ref[...], dh)
            qnb = qn.astype(BF16)
            kh = k_ref[:, cs]
            sc = lax.dot_general(qnb, kh, nt, preferred_element_type=F32) * scale
            p = jnp.exp(sc - jnp.max(sc, axis=-1, keepdims=True))
            p = p / jnp.sum(p, axis=-1, keepdims=True)
            dob = do_ref[:, cs].astype(BF16)
            dp = lax.dot_general(dob, v_ref[:, cs], nt, preferred_element_type=F32)
            ds = (p * (dp - jnp.sum(dp * p, axis=-1, keepdims=True)) * scale).astype(BF16)
            dqn = jnp.dot(ds, kh, preferred_element_type=F32)
            dq_ref[:, cs] = _norm_part_bwd(dqn, gq_ref[...], qh, qs, dh).astype(dq_ref.dtype)
            pg = pg + jnp.sum(dqn * qh, axis=0, keepdims=True)
            pv = lax.dot_general(p.astype(BF16), dob, tn, preferred_element_type=F32)
            pk = lax.dot_general(ds, qnb, tn, preferred_element_type=F32)

            @pl.when(first)
            def _():
                dv_ref[:, cs] = pv
                dk_ref[:, cs] = pk

            @pl.when(jnp.logical_not(first))
            def _():
                dv_ref[:, cs] += pv
                dk_ref[:, cs] += pk

        @pl.when(first)
        def _():
            dgq_ref[...] = pg

        @pl.when(jnp.logical_not(first))
        def _():
            dgq_ref[...] += pg

    row = pl.BlockSpec((tr, d), lambda i: (i, 0))
    mem = pl.BlockSpec(k.shape, lambda i: (0, 0))
    vec = pl.BlockSpec((1, dh), lambda i: (0, 0))
    return pl.pallas_call(
        body, name=name, grid=(s // tr,),
        in_specs=[row, mem, mem, vec, row], out_specs=[row, mem, mem, vec],
        out_shape=[jax.ShapeDtypeStruct((s, d), BF16), jax.ShapeDtypeStruct(k.shape, F32),
                   jax.ShapeDtypeStruct(k.shape, F32), jax.ShapeDtypeStruct((1, dh), F32)],
        compiler_params=_cparams(("arbitrary",)),
    )(q, k, v, gq, do)


def _swiglu_fwd(h1, *, name):
    s, w2 = h1.shape
    w = w2 // 2
    tr = _pick(s, 256, 8)
    tc = _pick(w, 1408, 128)
    ncb = w // tc

    def body(g_ref, u_ref, o_ref):
        o_ref[...] = (_silu(g_ref[...]) * u_ref[...]).astype(o_ref.dtype)

    return pl.pallas_call(
        body, name=name, grid=(s // tr, ncb),
        in_specs=[pl.BlockSpec((tr, tc), lambda i, j: (i, j)), pl.BlockSpec((tr, tc), lambda i, j: (i, j + ncb))],
        out_specs=pl.BlockSpec((tr, tc), lambda i, j: (i, j)),
        out_shape=jax.ShapeDtypeStruct((s, w), BF16), compiler_params=_cparams(("parallel", "parallel")),
    )(h1, h1)


def _swiglu_bwd(h1, dact, *, name):
    s, w2 = h1.shape
    w = w2 // 2
    tr = _pick(s, 256, 8)
    tc = _pick(w, 1408, 128)
    ncb = w // tc

    def body(g_ref, u_ref, d_ref, dg_ref, du_ref):
        gv = g_ref[...]
        dv = d_ref[...]
        dg_ref[...] = (dv * u_ref[...] * _dsilu(gv)).astype(dg_ref.dtype)
        du_ref[...] = (dv * _silu(gv)).astype(du_ref.dtype)

    blk = pl.BlockSpec((tr, tc), lambda i, j: (i, j))
    dg, du = pl.pallas_call(
        body, name=name, grid=(s // tr, ncb),
        in_specs=[blk, pl.BlockSpec((tr, tc), lambda i, j: (i, j + ncb)), blk],
        out_specs=[blk, blk],
        out_shape=[jax.ShapeDtypeStruct((s, w), BF16)] * 2, compiler_params=_cparams(("parallel", "parallel")),
    )(h1, h1, dact)
    return jnp.concatenate([dg, du], axis=1)


def _add(a, b, *, name):
    r, w = a.shape
    tr = _pick(r, 512, 8)

    def body(a_ref, b_ref, o_ref):
        o_ref[...] = a_ref[...] + b_ref[...].astype(F32)

    row = pl.BlockSpec((tr, w), lambda i: (i, 0))
    return pl.pallas_call(
        body, name=name, grid=(r // tr,), in_specs=[row, row], out_specs=row,
        out_shape=jax.ShapeDtypeStruct((r, w), F32), compiler_params=_cparams(("parallel",)),
    )(a, b)


def _loss(y, target, *, name):
    r, w = y.shape
    tr = _pick(r, 512, 8)

    def body(y_ref, t_ref, dy_ref, l_ref):
        err = y_ref[...] - t_ref[...]
        dy_ref[...] = err * (1.0 / w)
        part = jnp.zeros((8, LANE), F32) + 0.5 * jnp.sum(jnp.mean(err * err, axis=-1, keepdims=True))

        @pl.when(pl.program_id(0) == 0)
        def _():
            l_ref[...] = part

        @pl.when(pl.program_id(0) > 0)
        def _():
            l_ref[...] += part

    row = pl.BlockSpec((tr, w), lambda i: (i, 0))
    dy, l = pl.pallas_call(
        body, name=name, grid=(r // tr,), in_specs=[row, row],
        out_specs=[row, pl.BlockSpec((8, LANE), lambda i: (0, 0))],
        out_shape=[jax.ShapeDtypeStruct((r, w), F32), jax.ShapeDtypeStruct((8, LANE), F32)],
        compiler_params=_cparams(("arbitrary",)),
    )(y, target)
    return dy, l[0, 0]


def _adamw(w, g, m, v, *, name):
    r, c = w.shape
    tr = _pick(r, 256, 8)
    c1 = 1.0 - ADAM_B1 ** ADAM_STEP
    c2 = 1.0 - ADAM_B2 ** ADAM_STEP

    def body(w_ref, g_ref, m_ref, v_ref, d_ref, nm_ref, nv_ref):
        gv = g_ref[...]
        nm = ADAM_B1 * m_ref[...] + (1.0 - ADAM_B1) * gv
        nv = ADAM_B2 * v_ref[...] + (1.0 - ADAM_B2) * (gv * gv)
        nm_ref[...] = nm
        nv_ref[...] = nv
        d_ref[...] = -ADAM_LR * ((nm / c1) / (jnp.sqrt(nv / c2) + ADAM_EPS) + ADAM_WD * w_ref[...])

    row = pl.BlockSpec((tr, c), lambda i: (i, 0))
    return pl.pallas_call(
        body, name=name, grid=(r // tr,), in_specs=[row] * 4, out_specs=[row] * 3,
        out_shape=[jax.ShapeDtypeStruct((r, c), F32)] * 3, compiler_params=_cparams(("parallel",)),
    )(w, g, m, v)


IN_SPLIT = dict(z=(0, 1024), xbc=(1024, 3072), dt=(3072, 3200), glu=(3200, 5248), ql=(5248, 5632),
                ckv=(5632, 5888), kr=(5888, 6016), gate=(6016, 9088))


IN_WIDTH_PAD = 9216


def _w_in_pad(w):
    zeros = lambda n: jnp.zeros(w.shape[:-1] + (n,), w.dtype)
    return jnp.concatenate([w[..., :3088], zeros(112), w[..., 3088:5840], zeros(64), w[..., 5840:],
                            zeros(IN_WIDTH_PAD - 9088)], axis=-1)


def _w_in_unpad(g):
    return jnp.concatenate([g[..., :3088], g[..., 3200:5952], g[..., 6016:9088]], axis=-1)


def _qb_pad(w):
    lead = w.shape[:-1]
    w = w.reshape(lead + (MLA_HEADS, MLA_NOPE + MLA_ROPE))
    w = jnp.concatenate([w, jnp.zeros(lead + (MLA_HEADS, MLA_HP - MLA_NOPE - MLA_ROPE), w.dtype)], axis=-1)
    return w.reshape(lead + (MLA_HEADS * MLA_HP,))


def _qb_unpad(g):
    lead = g.shape[:-1]
    g = g.reshape(lead + (MLA_HEADS, MLA_HP))[..., :MLA_NOPE + MLA_ROPE]
    return g.reshape(lead + (MLA_HEADS * (MLA_NOPE + MLA_ROPE),))


def _pad_lanes(v, n):
    return jnp.concatenate([v, jnp.zeros((n - v.shape[0],), v.dtype)]).reshape(1, n)


def _layer_params(full, rep, l):
    p = {}
    w_in = full['w_in'][l]
    for k, (a, b) in IN_SPLIT.items():
        p['w_' + k] = w_in[:, a:b]
    p['w_in'] = w_in
    for k in ('mla_w_q_b', 'mla_w_kv_b', 'xattn_w_kv', 'ffn_w_in', 'ssd_w_out', 'conv_w_out', 'mla_w_o', 'w_out',
              'xattn_w_q', 'xattn_w_o', 'ffn_w_out', 'ssd_conv_w', 'conv_dw_w'):
        p[k] = full[k][l]
    p['gate_b'] = full['gate_b'][l].reshape(1, -1)
    row = lambda name: rep[name][l].reshape(1, -1)
    for k in ('mix_norm_g', 'ssd_conv_b', 'ssd_norm_g', 'conv_dw_b', 'conv_ln_g', 'conv_ln_b', 'mla_q_a_g',
              'mla_kv_a_g', 'xattn_norm_g', 'mem_norm_g', 'xattn_q_norm_g', 'xattn_k_norm_g', 'ffn_norm_g'):
        p[k] = row(k)
    for k in ('ssd_dt_bias', 'ssd_a_log', 'ssd_d'):
        p[k] = _pad_lanes(rep[k][l], LANE)
        p[k + '_t'] = p[k].reshape(LANE, 1)
    p['gq'] = _pad_lanes(rep['mla_q_norm_g'][l], MLA_HP)
    p['gk'] = _pad_lanes(rep['mla_k_norm_g'][l], MLA_HP)
    return p


def _layer_fwd(x, mem, cs, sn, p, l):
    n = lambda s: f"l{l}_{s}"
    s_len, d = x.shape
    nc = s_len // CHUNK
    sv = {'x': x}
    u = _rms_fwd(x, p['mix_norm_g'], out_dtype=BF16, name=n("mix_norm"))
    z = _matmul(u, p['w_z'], name=n("in_z"))
    xbc = _matmul(u, p['w_xbc'], name=n("in_xbc"))
    dtr = _matmul(u, p['w_dt'], name=n("in_dt"))
    glu = _matmul(u, p['w_glu'], name=n("in_glu"))
    ql = _matmul(u, p['w_ql'], name=n("in_ql"))
    ckv = _matmul(u, p['w_ckv'], name=n("in_ckv"))
    krr = _matmul(u, p['w_kr'], name=n("in_kr"))
    gl = _matmul(u, p['w_gate'], name=n("in_gate"))
    pre_s, act_s = _conv_fwd(xbc, p['ssd_conv_w'], p['ssd_conv_b'], glu=False, act=True, name=n("ssd_conv"))
    dtrt = dtr.reshape(nc, CHUNK, LANE).transpose(0, 2, 1)
    y_scan, prev = _ssd_scan_fwd(act_s, dtr, dtrt, p['ssd_dt_bias'], p['ssd_dt_bias_t'], p['ssd_a_log'],
                                 p['ssd_a_log_t'], p['ssd_d'], name=n("ssd_scan"))
    yn = _ssd_gate_fwd(y_scan, z, p['ssd_norm_g'], name=n("ssd_gate"))
    y_ssd = _matmul(yn, p['ssd_w_out'], name=n("ssd_out"))
    pre_c, = _conv_fwd(glu, p['conv_dw_w'], p['conv_dw_b'], glu=True, act=False, name=n("dw_conv"))
    vc = _ln_silu_fwd(pre_c, p['conv_ln_g'], p['conv_ln_b'], name=n("conv_ln"))
    y_conv = _matmul(vc, p['conv_w_out'], name=n("conv_out"))
    qln = _rms_fwd(ql, p['mla_q_a_g'], out_dtype=BF16, name=n("q_a_norm"))
    q = _matmul(qln, p['mla_w_q_b'], name=n("q_b"))
    ckvn = _rms_fwd(ckv, p['mla_kv_a_g'], out_dtype=BF16, name=n("kv_a_norm"))
    kv = _matmul(ckvn, p['mla_w_kv_b'], name=n("kv_b"))
    qf, kf, v = _mla_prep_fwd(q, kv, krr, cs, sn, p['gq'], p['gk'], name=n("mla_prep"))
    o, lse = _mla_attn_fwd(qf, kf, v, name=n("mla_attn"))
    y_mla = _matmul(o, p['mla_w_o'], name=n("mla_out"))
    merged = _merge_fwd(gl, p['gate_b'], y_ssd, y_conv, y_mla, name=n("merge"))
    x1 = _matmul(merged, p['w_out'], add=x, name=n("mix_out"))
    hx = _rms_fwd(x1, p['xattn_norm_g'], out_dtype=BF16, name=n("xattn_norm"))
    qx = _matmul(hx, p['xattn_w_q'], name=n("xattn_q"))
    memn = _rms_fwd(mem, p['mem_norm_g'], out_dtype=BF16, name=n("mem_norm"))
    kvx = _matmul(memn, p['xattn_w_kv'], name=n("xattn_kv"))
    m_len = mem.shape[0]
    dh = d // X_HEADS
    kraw = kvx[:, :d].reshape(m_len * X_HEADS, dh)
    kx = _rms_fwd(kraw, p['xattn_k_norm_g'], out_dtype=BF16, name=n("xattn_k_norm")).reshape(m_len, d)
    vx = kvx[:, d:].astype(BF16)
    ox = _xattn_fwd(qx, kx, vx, p['xattn_q_norm_g'], name=n("xattn_core"))
    x2 = _matmul(ox, p['xattn_w_o'], add=x1, name=n("xattn_out"))
    hf = _rms_fwd(x2, p['ffn_norm_g'], out_dtype=BF16, name=n("ffn_norm"))
    h1 = _matmul(hf, p['ffn_w_in'], name=n("ffn_in"))
    act = _swiglu_fwd(h1, name=n("swiglu"))
    x3 = _matmul(act, p['ffn_w_out'], add=x2, name=n("ffn_out"))
    sv.update(u=u, z=z, xbc=xbc, dtr=dtr, dtrt=dtrt, glu=glu, ql=ql, ckv=ckv, krr=krr, gl=gl, pre_s=pre_s,
              act_s=act_s, y_scan=y_scan, prev=prev, yn=yn, y_ssd=y_ssd, pre_c=pre_c, vc=vc, y_conv=y_conv,
              qln=qln, q=q, ckvn=ckvn, kv=kv, qf=qf, kf=kf, v=v, o=o, lse=lse, y_mla=y_mla, merged=merged,
              x1=x1, hx=hx, qx=qx, memn=memn, kraw=kraw, kx=kx, vx=vx, ox=ox, x2=x2, hf=hf, h1=h1, act=act)
    return x3, sv


DW_KEY = dict(ffn_out_dw='ffn_w_out', ffn_in_dw='ffn_w_in', xattn_out_dw='xattn_w_o', xattn_q_dw='xattn_w_q',
              xattn_kv_dw='xattn_w_kv', mix_out_dw='w_out', mla_out_dw='mla_w_o', q_b_dw='mla_w_q_b',
              kv_b_dw='mla_w_kv_b', conv_out_dw='conv_w_out', ssd_out_dw='ssd_w_out', in_dw='w_in')


def _layer_bwd(dx3, mem, cs, sn, p, sv, l, stacks, depth):
    n = lambda s: f"l{l}_b_{s}"
    dw = lambda s: dict(name=n(s), into=(stacks.get(DW_KEY[s]), l, depth))
    g = {}
    d = dx3.shape[1]
    dact = _matmul(dx3, p['ffn_w_out'], tb=True, name=n("ffn_out_dx"))
    g['ffn_w_out'] = _matmul(sv['act'], dx3, ta=True, **dw("ffn_out_dw"))
    dh1 = _swiglu_bwd(sv['h1'], dact, name=n("swiglu"))
    g['ffn_w_in'] = _matmul(sv['hf'], dh1, ta=True, **dw("ffn_in_dw"))
    dhf = _matmul(dh1, p['ffn_w_in'], tb=True, name=n("ffn_in_dx"))
    dx2, g['ffn_norm_g'] = _rms_bwd(sv['x2'], p['ffn_norm_g'], dhf, dx_dtype=F32, add=dx3, name=n("ffn_norm"))
    dox = _matmul(dx2, p['xattn_w_o'], tb=True, name=n("xattn_out_dx"))
    g['xattn_w_o'] = _matmul(sv['ox'], dx2, ta=True, **dw("xattn_out_dw"))
    dqx, dkx, dvx, g['xattn_q_norm_g'] = _xattn_bwd(sv['qx'], sv['kx'], sv['vx'], p['xattn_q_norm_g'], dox,
                                                    name=n("xattn_core"))
    g['xattn_w_q'] = _matmul(sv['hx'], dqx, ta=True, **dw("xattn_q_dw"))
    dhx = _matmul(dqx, p['xattn_w_q'], tb=True, name=n("xattn_q_dx"))
    dx1, g['xattn_norm_g'] = _rms_bwd(sv['x1'], p['xattn_norm_g'], dhx, dx_dtype=F32, add=dx2, name=n("xattn_norm"))
    m_len = mem.shape[0]
    dh = d // X_HEADS
    dkraw, g['xattn_k_norm_g'] = _rms_bwd(sv['kraw'], p['xattn_k_norm_g'], dkx.reshape(m_len * X_HEADS, dh),
                                          dx_dtype=BF16, name=n("xattn_k_norm"))
    dkvx = jnp.concatenate([dkraw.reshape(m_len, d), dvx.astype(BF16)], axis=1)
    g['xattn_w_kv'] = _matmul(sv['memn'], dkvx, ta=True, **dw("xattn_kv_dw"))
    dmemn = _matmul(dkvx, p['xattn_w_kv'], tb=True, name=n("xattn_kv_dx"))
    _, g['mem_norm_g'] = _rms_bwd(mem, p['mem_norm_g'], dmemn, dx_dtype=BF16, name=n("mem_norm"))
    dmerged = _matmul(dx1, p['w_out'], tb=True, name=n("mix_out_dx"))
    g['w_out'] = _matmul(sv['merged'], dx1, ta=True, **dw("mix_out_dw"))
    dgl, g['gate_b'], dys, dyc, dym = _merge_bwd(sv['gl'], p['gate_b'], sv['y_ssd'], sv['y_conv'], sv['y_mla'],
                                                 dmerged, name=n("merge"))
    do = _matmul(dym, p['mla_w_o'], tb=True, name=n("mla_out_dx"))
    g['mla_w_o'] = _matmul(sv['o'], dym, ta=True, **dw("mla_out_dw"))
    dqf, dkf, dv = _mla_attn_bwd(sv['qf'], sv['kf'], sv['v'], sv['o'], sv['lse'], do, name=n("mla_attn"))
    dq, dkv, dkrr, g['gq'], g['gk'] = _mla_prep_bwd(sv['q'], sv['kv'], sv['krr'], cs, sn, p['gq'], p['gk'],
                                                    dqf, dkf, dv, name=n("mla_prep"))
    g['mla_w_q_b'] = _matmul(sv['qln'], dq, ta=True, **dw("q_b_dw"))
    dqln = _matmul(dq, p['mla_w_q_b'], tb=True, name=n("q_b_dx"))
    dql, g['mla_q_a_g'] = _rms_bwd(sv['ql'], p['mla_q_a_g'], dqln, dx_dtype=BF16, name=n("q_a_norm"))
    g['mla_w_kv_b'] = _matmul(sv['ckvn'], dkv, ta=True, **dw("kv_b_dw"))
    dckvn = _matmul(dkv, p['mla_w_kv_b'], tb=True, name=n("kv_b_dx"))
    dckv, g['mla_kv_a_g'] = _rms_bwd(sv['ckv'], p['mla_kv_a_g'], dckvn, dx_dtype=BF16, name=n("kv_a_norm"))
    dvc = _matmul(dyc, p['conv_w_out'], tb=True, name=n("conv_out_dx"))
    g['conv_w_out'] = _matmul(sv['vc'], dyc, ta=True, **dw("conv_out_dw"))
    dpre_c, g['conv_ln_g'], g['conv_ln_b'] = _ln_silu_bwd(sv['pre_c'], p['conv_ln_g'], p['conv_ln_b'], dvc,
                                                          name=n("conv_ln"))
    da, dg, g['conv_dw_w'], g['conv_dw_b'] = _conv_bwd(sv['glu'], p['conv_dw_w'], p['conv_dw_b'], dpre_c, None,
                                                       glu=True, act=False, name=n("dw_conv"))
    dyn = _matmul(dys, p['ssd_w_out'], tb=True, name=n("ssd_out_dx"))
    g['ssd_w_out'] = _matmul(sv['yn'], dys, ta=True, **dw("ssd_out_dw"))
    dy_scan, dz, g['ssd_norm_g'] = _ssd_gate_bwd(sv['y_scan'], sv['z'], p['ssd_norm_g'], dyn, name=n("ssd_gate"))
    dxs, db, dc, ddtr, g['ssd_a_log'], g['ssd_dt_bias'], g['ssd_d'] = _ssd_scan_bwd(
        sv['act_s'], sv['dtr'], sv['dtrt'], p['ssd_dt_bias'], p['ssd_dt_bias_t'], p['ssd_a_log'], p['ssd_a_log_t'],
        p['ssd_d'], sv['prev'], dy_scan, name=n("ssd_scan"))
    dact_s = jnp.concatenate([dxs, db, dc], axis=1)
    dxbc, g['ssd_conv_w'], g['ssd_conv_b'] = _conv_bwd(sv['xbc'], p['ssd_conv_w'], p['ssd_conv_b'], dact_s,
                                                       sv['pre_s'], glu=False, act=True, name=n("ssd_conv"))
    tail = jnp.zeros((dz.shape[0], IN_WIDTH_PAD - IN_SPLIT['gate'][1]), BF16)
    dproj = jnp.concatenate([dz, dxbc, ddtr, da, dg, dql, dckv, dkrr, dgl, tail], axis=1)
    g['w_in'] = _matmul(sv['u'], dproj, ta=True, **dw("in_dw"))
    du = _matmul(dproj, p['w_in'], tb=True, name=n("in_dx"))
    dx, g['mix_norm_g'] = _rms_bwd(sv['x'], p['mix_norm_g'], du, dx_dtype=F32, add=dx1, name=n("mix_norm"))
    return dx, g


REP_NAMES = ('mix_norm_g', 'ssd_conv_b', 'ssd_dt_bias', 'ssd_a_log', 'ssd_d', 'ssd_norm_g', 'conv_dw_b', 'conv_ln_g',
             'conv_ln_b', 'mla_q_a_g', 'mla_kv_a_g', 'mla_q_norm_g', 'mla_k_norm_g', 'xattn_norm_g', 'mem_norm_g',
             'xattn_q_norm_g', 'xattn_k_norm_g', 'ffn_norm_g')
BIG = (('w_in', 1, 1024, 8912), ('mla_w_q_b', 1, 384, 1536), ('mla_w_kv_b', 1, 256, 2048),
       ('xattn_w_kv', 1, 1024, 2048), ('ffn_w_in', 1, 1024, 5632), ('ssd_w_out', 0, 1024, 1024),
       ('conv_w_out', 0, 1024, 1024), ('mla_w_o', 0, 1024, 1024), ('w_out', 0, 1024, 1024),
       ('xattn_w_q', 0, 1024, 1024), ('xattn_w_o', 0, 1024, 1024), ('ffn_w_out', 0, 2816, 1024))
SMALL = (('ssd_conv_w', 1, 4, 2048), ('conv_dw_w', 1, 31, 1024), ('gate_b', 1, 3, 1024))


def _rope_tables(positions):
    half = MLA_ROPE // 2
    inv = ROPE_THETA ** (-jnp.arange(0, MLA_ROPE, 2, dtype=F32) / MLA_ROPE)
    ang = positions.astype(F32)[:, None] * inv
    cos, sin = jnp.cos(ang), jnp.sin(ang)
    z = jnp.zeros((positions.shape[0], LANE - 2 * half), F32)
    return jnp.concatenate([cos, cos, z], axis=1), jnp.concatenate([-sin, sin, z], axis=1)


def _local_step(x, mem, positions, target, full, rep):
    depth = rep['mix_norm_g'].shape[0]
    cs, sn = _rope_tables(positions)
    params, saved = [], []
    h = x
    for l in range(depth):
        p = _layer_params(full, rep, l)
        h, sv = _layer_fwd(h, mem, cs, sn, p, l)
        params.append(p)
        saved.append(sv)
    dh, loss = _loss(h, target, name="loss")
    layer_grads = [None] * depth
    stacks = {}
    for l in reversed(range(depth)):
        dh, layer_grads[l] = _layer_bwd(dh, mem, cs, sn, params[l], saved[l], l, stacks, depth)
        stacks = {k: layer_grads[l][k] for k in DW_KEY.values()}
    stack = lambda k: jnp.stack([layer_grads[l][k] for l in range(depth)])
    gfull = {k: stack(k) for k, _, _, _ in SMALL}
    gfull.update(stacks)
    gfull['w_in'] = _w_in_unpad(gfull['w_in'])
    gfull['mla_w_q_b'] = _qb_unpad(gfull['mla_w_q_b'])
    gfull['gate_b'] = gfull['gate_b'].reshape(depth, 3, -1)
    grep = {}
    for k in REP_NAMES:
        if k == 'mla_q_norm_g':
            grep[k] = stack('gq')[:, 0, :MLA_NOPE + MLA_ROPE]
        elif k == 'mla_k_norm_g':
            grep[k] = stack('gk')[:, 0, :MLA_NOPE + MLA_ROPE]
        elif k in ('ssd_dt_bias', 'ssd_a_log', 'ssd_d'):
            grep[k] = stack(k)[:, 0, :SSD_HEADS]
        else:
            grep[k] = stack(k)[:, 0, :]
    return loss, dh, gfull, grep


N_CHIPS = 4
HBM_SPEC = pl.BlockSpec(memory_space=pl.ANY)
ROW, COL, STK, REP = "row", "col", "stk", "rep"


def _kind(axis, cs):
    if axis == 0:
        return ROW
    return COL if cs % LANE == 0 else STK


def _mesh_pos():
    return lax.axis_index("x"), lax.axis_index("y"), lax.axis_index("c")


def _chip_view(ref, kind, j, a, b, layers=None):
    lsel = slice(None) if layers is None else pl.ds(layers[0], layers[1])
    if kind == STK:
        return ref.at[j, lsel]
    if kind == ROW:
        return ref.at[lsel, pl.ds(pl.multiple_of(j * a, 8), a), :]
    if kind == COL:
        return ref.at[lsel, :, pl.ds(pl.multiple_of(j * b, LANE), b)]
    return ref.at[lsel]


def _all_gather(shards, kinds, *, name):
    n = len(shards)
    depth = shards[0].shape[0]
    lh = depth // 2

    def out_shape(w, kind):
        _, a, b = w.shape
        full = {ROW: (depth, N_CHIPS * a, b), COL: (depth, a, N_CHIPS * b), STK: (N_CHIPS, depth, a, b)}[kind]
        return jax.ShapeDtypeStruct(full, w.dtype)

    def body(*refs):
        w_refs, out_refs = refs[:n], refs[n:2 * n]
        send_sems, recv_sems = refs[2 * n:]
        x, y, cc = _mesh_pos()
        me = (x, y, cc)
        sibling = (x, y, 1 - cc)
        chips = [(1 - x, y), (x, 1 - y), (1 - x, 1 - y)]
        slot = lambda chip: 2 * chip[0] + chip[1]

        def part(i, chip, hc):
            _, a, b = w_refs[i].shape
            return _chip_view(out_refs[i], kinds[i], slot(chip), a, b, (hc * lh, lh))

        def own(i):
            _, a, b = w_refs[i].shape
            return _chip_view(out_refs[i], kinds[i], slot((x, y)), a, b)

        def copy(i, k, src, dst, to):
            return pltpu.make_async_remote_copy(src_ref=src, dst_ref=dst, send_sem=send_sems.at[i, k],
                                                recv_sem=recv_sems.at[i, k], device_id=to, device_id_type=MESH)

        sends = []
        for i in range(n):
            my_half = w_refs[i].at[pl.ds(cc * lh, lh)]
            for k, chip in enumerate(chips):
                sends.append(copy(i, k, my_half, part(i, (x, y), cc), (*chip, cc)))
            sends.append(copy(i, 6, w_refs[i], own(i), sibling))
        for cp in sends:
            cp.start()
        for k, chip in enumerate(chips):
            for i in range(n):
                copy(i, k, part(i, chip, cc), part(i, chip, cc), me).wait_recv()
                cp = copy(i, 3 + k, part(i, chip, cc), part(i, chip, cc), sibling)
                cp.start()
                sends.append(cp)
        for i in range(n):
            for k, chip in enumerate(chips):
                copy(i, 3 + k, part(i, chip, 1 - cc), part(i, chip, 1 - cc), me).wait_recv()
            copy(i, 6, own(i), own(i), me).wait_recv()
        for cp in sends:
            cp.wait_send()

    return pl.pallas_call(
        body, name=name, in_specs=[HBM_SPEC] * n, out_specs=[HBM_SPEC] * n,
        out_shape=[out_shape(w, kd) for w, kd in zip(shards, kinds)],
        scratch_shapes=[pltpu.SemaphoreType.DMA((n, 7)), pltpu.SemaphoreType.DMA((n, 7))],
    )(*shards)


def _layers_half(ref, kind, start, lh):
    return ref.at[:, pl.ds(start, lh)] if kind == STK else ref.at[pl.ds(start, lh)]


def _rs_pair(gs, kinds, *, name):
    n = len(gs)

    def half_shape(g, kind):
        s = list(g.shape)
        s[1 if kind == STK else 0] //= 2
        return jax.ShapeDtypeStruct(tuple(s), g.dtype)

    def body(*refs):
        g_refs, buf_refs = refs[:n], refs[n:2 * n]
        send_sems, recv_sems = refs[2 * n:]
        x, y, cc = _mesh_pos()
        cps = []
        for i in range(n):
            lh = buf_refs[i].shape[1 if kinds[i] == STK else 0]
            cp = pltpu.make_async_remote_copy(src_ref=_layers_half(g_refs[i], kinds[i], (1 - cc) * lh, lh),
                                              dst_ref=buf_refs[i], send_sem=send_sems.at[i], recv_sem=recv_sems.at[i],
                                              device_id=(x, y, 1 - cc), device_id_type=MESH)
            cp.start()
            cps.append(cp)
        for cp in cps:
            cp.wait()

    return pl.pallas_call(
        body, name=name, in_specs=[HBM_SPEC] * n, out_specs=[HBM_SPEC] * n,
        out_shape=[half_shape(g, kd) for g, kd in zip(gs, kinds)],
        scratch_shapes=[pltpu.SemaphoreType.DMA((n,)), pltpu.SemaphoreType.DMA((n,))],
    )(*gs)


def _row_tile(rows, cols):
    return _pick(rows, max(8, (512 * 1024 // cols) // 8 * 8), 8)


def _rs_pair_add(g, buf, kind, cc, *, name):
    cols = g.shape[-1]
    pre = g.shape[0] if kind == STK else 1
    rows = buf.size // (pre * cols)
    tr = _row_tile(rows, cols)

    def body(cc_ref, g_ref, b_ref, o_ref):
        o_ref[...] = g_ref[...] + b_ref[...]

    out = pl.pallas_call(
        body, name=name,
        grid_spec=pltpu.PrefetchScalarGridSpec(
            num_scalar_prefetch=1, grid=(pre, rows // tr),
            in_specs=[pl.BlockSpec((None, None, tr, cols), lambda s, i, cc_ref: (s, cc_ref[0], i, 0)),
                      pl.BlockSpec((None, tr, cols), lambda s, i, cc_ref: (s, i, 0))],
            out_specs=pl.BlockSpec((None, tr, cols), lambda s, i, cc_ref: (s, i, 0))),
        out_shape=jax.ShapeDtypeStruct((pre, rows, cols), g.dtype),
        compiler_params=_cparams(("parallel", "parallel")),
    )(cc.reshape(1).astype(jnp.int32), g.reshape(pre, 2, rows, cols), buf.reshape(pre, rows, cols))
    return out.reshape(buf.shape)


def _rs_cross(ps, kinds, shard_shapes, *, name):
    n = len(ps)

    def body(*refs):
        p_refs, out_refs = refs[:n], refs[n:2 * n]
        send_sems, recv_sems, local_sems = refs[2 * n:]
        x, y, cc = _mesh_pos()
        chips = [(1 - x, y), (x, 1 - y), (1 - x, 1 - y)]
        slot = lambda chip: 2 * chip[0] + chip[1]
        local, sends = [], []
        for i in range(n):
            a, b = shard_shapes[i]
            if kinds[i] == REP:
                cp = pltpu.make_async_copy(p_refs[i], out_refs[i].at[slot((x, y))], local_sems.at[i])
                cp.start()
                local.append(cp)
            for k, chip in enumerate(chips):
                cp = pltpu.make_async_remote_copy(src_ref=_chip_view(p_refs[i], kinds[i], slot(chip), a, b),
                                                  dst_ref=out_refs[i].at[slot((x, y))], send_sem=send_sems.at[i, k],
                                                  recv_sem=recv_sems.at[i, k], device_id=(*chip, cc),
                                                  device_id_type=MESH)
                cp.start()
                sends.append(cp)
        for i in range(n):
            for k, chip in enumerate(chips):
                landed = out_refs[i].at[slot(chip)]
                pltpu.make_async_remote_copy(src_ref=landed, dst_ref=landed, send_sem=send_sems.at[i, k],
                                             recv_sem=recv_sems.at[i, k], device_id=(*chip, cc),
                                             device_id_type=MESH).wait_recv()
        for cp in sends:
            cp.wait_send()
        for cp in local:
            cp.wait()

    def out_shape(p, kind, ab):
        lh = p.shape[1 if kind == STK else 0]
        return jax.ShapeDtypeStruct((N_CHIPS, lh) + tuple(ab), p.dtype)

    return pl.pallas_call(
        body, name=name, in_specs=[HBM_SPEC] * n, out_specs=[HBM_SPEC] * n,
        out_shape=[out_shape(p, kd, ab) for p, kd, ab in zip(ps, kinds, shard_shapes)],
        scratch_shapes=[pltpu.SemaphoreType.DMA((n, 3)), pltpu.SemaphoreType.DMA((n, 3)),
                        pltpu.SemaphoreType.DMA((n,))],
    )(*ps)


def _rs_sum(p, landed, kind, ab, where, *, name):
    a, b = ab
    lh = landed.shape[1]
    tr = _row_tile(a, b)
    blk = lambda f: pl.BlockSpec((None, None, tr, b), f)
    if kind == ROW:
        p_spec = pl.BlockSpec((None, tr, b), lambda l, i, w: (l, w[0] * (a // tr) + i, 0))
    elif kind == COL:
        p_spec = pl.BlockSpec((None, tr, b), lambda l, i, w: (l, i, w[0]))
    elif kind == STK:
        p_spec = blk(lambda l, i, w: (w[0], l, i, 0))
    else:
        p_spec = blk(lambda l, i, w: (0, l, i, 0))
        p = landed
    others = [blk(lambda l, i, w, k=k: ((w[k] if kind != REP else k), l, i, 0)) for k in (1, 2, 3)]

    def body(w_ref, p_ref, b1_ref, b2_ref, b3_ref, o_ref):
        o_ref[...] = ((p_ref[...] + b1_ref[...]) + b2_ref[...]) + b3_ref[...]

    return pl.pallas_call(
        body, name=name,
        grid_spec=pltpu.PrefetchScalarGridSpec(
            num_scalar_prefetch=1, grid=(lh, a // tr), in_specs=[p_spec] + others,
            out_specs=pl.BlockSpec((None, tr, b), lambda l, i, w: (w[4] * lh + l, i, 0))),
        out_shape=jax.ShapeDtypeStruct((2 * lh, a, b), landed.dtype),
        compiler_params=_cparams(("parallel", "parallel")),
    )(where, p, landed, landed, landed)


def _rs_share(fs, *, name):
    n = len(fs)

    def body(*refs):
        out_refs = refs[n:2 * n]
        send_sems, recv_sems = refs[2 * n:]
        x, y, cc = _mesh_pos()
        cps = []
        for i in range(n):
            lh = out_refs[i].shape[0] // 2
            mine = out_refs[i].at[pl.ds(cc * lh, lh)]
            cp = pltpu.make_async_remote_copy(src_ref=mine, dst_ref=mine, send_sem=send_sems.at[i],
                                              recv_sem=recv_sems.at[i], device_id=(x, y, 1 - cc),
                                              device_id_type=MESH)
            cp.start()
            cps.append(cp)
        for i in range(n):
            lh = out_refs[i].shape[0] // 2
            theirs = out_refs[i].at[pl.ds((1 - cc) * lh, lh)]
            pltpu.make_async_remote_copy(src_ref=theirs, dst_ref=theirs, send_sem=send_sems.at[i],
                                         recv_sem=recv_sems.at[i], device_id=(x, y, 1 - cc),
                                         device_id_type=MESH).wait_recv()
        for cp in cps:
            cp.wait_send()

    return pl.pallas_call(
        body, name=name, in_specs=[HBM_SPEC] * n, out_specs=[HBM_SPEC] * n,
        out_shape=[jax.ShapeDtypeStruct(f.shape, f.dtype) for f in fs],
        input_output_aliases={i: i for i in range(n)},
        scratch_shapes=[pltpu.SemaphoreType.DMA((n,)), pltpu.SemaphoreType.DMA((n,))],
    )(*fs)


def _reduce_scatter(gs, kinds, shard_shapes):
    x, y, cc = _mesh_pos()
    where = jnp.stack([2 * x + y, 2 * (1 - x) + y, 2 * x + (1 - y), 2 * (1 - x) + (1 - y), cc]).astype(jnp.int32)
    bufs = _rs_pair(gs, kinds, name="rs_pair")
    ps = [_rs_pair_add(g, buf, kd, cc, name=f"rs_pair_add_{i}") for i, (g, buf, kd) in enumerate(zip(gs, bufs, kinds))]
    landed = _rs_cross(ps, kinds, shard_shapes, name="rs_cross")
    fs = [_rs_sum(p, b, kd, ab, where, name=f"rs_sum_{i}")
          for i, (p, b, kd, ab) in enumerate(zip(ps, landed, kinds, shard_shapes))]
    return _rs_share(fs, name="rs_share")


def _shard_shape(axis, r, c):
    return (r // N_CHIPS, c) if axis == 0 else (r, c // N_CHIPS)


def _unstack(stacked):
    ns, depth, r, cs = stacked.shape
    return stacked.transpose(1, 2, 0, 3).reshape(depth, r, ns * cs)


def _stack(fullw):
    depth, r, c = fullw.shape
    return fullw.reshape(depth, r, N_CHIPS, c // N_CHIPS).transpose(2, 0, 1, 3)


REP_SIZES = dict(mix_norm_g=1024, ssd_conv_b=2048, ssd_dt_bias=16, ssd_a_log=16, ssd_d=16, ssd_norm_g=1024,
                 conv_dw_b=1024, conv_ln_g=1024, conv_ln_b=1024, mla_q_a_g=384, mla_kv_a_g=256, mla_q_norm_g=192,
                 mla_k_norm_g=192, xattn_norm_g=1024, mem_norm_g=1024, xattn_q_norm_g=256, xattn_k_norm_g=256,
                 ffn_norm_g=1024)
REP_WIDTH = -(-sum(REP_SIZES.values()) // LANE) * LANE


def _pack_rep(d):
    flat = jnp.concatenate([d[k] for k in REP_NAMES], axis=1)
    return jnp.pad(flat, ((0, 0), (0, REP_WIDTH - flat.shape[1])))[:, None, :]


def _unpack_rep(packed):
    out, off = {}, 0
    for k in REP_NAMES:
        out[k] = packed[:, 0, off:off + REP_SIZES[k]]
        off += REP_SIZES[k]
    return out


WEIGHT_NAMES = ('mix_norm_g', 'w_in', 'ssd_conv_w', 'ssd_conv_b', 'ssd_dt_bias', 'ssd_a_log', 'ssd_d', 'ssd_norm_g',
                'ssd_w_out', 'conv_dw_w', 'conv_dw_b', 'conv_ln_g', 'conv_ln_b', 'conv_w_out', 'mla_q_a_g',
                'mla_w_q_b', 'mla_kv_a_g', 'mla_w_kv_b', 'mla_q_norm_g', 'mla_k_norm_g', 'mla_w_o', 'gate_b', 'w_out',
                'xattn_norm_g', 'mem_norm_g', 'xattn_w_q', 'xattn_w_kv', 'xattn_q_norm_g', 'xattn_k_norm_g',
                'xattn_w_o', 'ffn_norm_g', 'ffn_w_in', 'ffn_w_out')


def kernel(x, mem, positions, *rest):
    nw = len(WEIGHT_NAMES)
    weights = dict(zip(WEIGHT_NAMES, rest[:nw]))
    target = rest[nw]
    mom_m = dict(zip(WEIGHT_NAMES, rest[nw + 1:2 * nw + 1]))
    mom_v = dict(zip(WEIGHT_NAMES, rest[2 * nw + 1:3 * nw + 1]))
    depth = weights['mix_norm_g'].shape[0]

    sharded = BIG + SMALL
    kinds = [_kind(axis, _shard_shape(axis, r, c)[1]) for _, axis, r, c in sharded]
    shard_shapes = [_shard_shape(axis, r, c) for _, axis, r, c in sharded]
    shards = [weights[k].astype(BF16) for k, _, _, _ in BIG] + [weights[k] for k, _, _, _ in SMALL]
    gathered = _all_gather(shards, kinds, name="ag_weights")
    full = {k: (_unstack(w) if kd == STK else w) for (k, _, _, _), kd, w in zip(sharded, kinds, gathered)}
    full['w_in'] = _w_in_pad(full['w_in'])
    full['mla_w_q_b'] = _qb_pad(full['mla_w_q_b'])
    rep = {k: weights[k] for k in REP_NAMES}

    loss, dx, gfull, grep = _local_step(x[0], mem[0], positions[0], target[0], full, rep)
    loss = lax.psum(loss, ("x", "y", "c"))

    gs = [(_stack(gfull[k]) if kd == STK else gfull[k]) for (k, _, _, _), kd in zip(sharded, kinds)]
    summed = _reduce_scatter(gs + [_pack_rep(grep)], kinds + [REP], shard_shapes + [(1, REP_WIDTH)])
    grads = {k: g for (k, _, _, _), g in zip(sharded, summed[:-1])}
    rep_sum = summed[-1]
    grads.update(_unpack_rep(rep_sum))

    delta, new_m, new_v = {}, {}, {}
    for k, _, _, _ in sharded:
        w = weights[k]
        two_d = (w.shape[0] * w.shape[1], w.shape[2])
        d_, m_, v_ = _adamw(w.reshape(two_d), grads[k].reshape(two_d), mom_m[k].reshape(two_d),
                            mom_v[k].reshape(two_d), name="adamw_" + k)
        delta[k], new_m[k], new_v[k] = d_.reshape(w.shape), m_.reshape(w.shape), v_.reshape(w.shape)
    pack2 = lambda d: _pack_rep(d)[:, 0, :]
    d_, m_, v_ = _adamw(pack2(rep), rep_sum[:, 0, :], pack2({k: mom_m[k] for k in REP_NAMES}),
                        pack2({k: mom_v[k] for k in REP_NAMES}), name="adamw_rep")
    delta.update(_unpack_rep(d_[:, None, :]))
    new_m.update(_unpack_rep(m_[:, None, :]))
    new_v.update(_unpack_rep(v_[:, None, :]))

    return (loss, dx[None], *[grads[k] for k in WEIGHT_NAMES], *[delta[k] for k in WEIGHT_NAMES],
            *[new_m[k] for k in WEIGHT_NAMES], *[new_v[k] for k in WEIGHT_NAMES])
```

```python
import functools
import math

import jax
import jax.numpy as jnp
import numpy as np
from jax import lax
from jax.experimental import pallas as pl
from jax.experimental.pallas import tpu as pltpu

F32 = jnp.float32
BF16 = jnp.bfloat16
HI = lax.Precision.HIGHEST
MESH = pl.DeviceIdType.MESH

EPS = 1e-6
CHUNK = 64
SSD_HEADS = 16
SSD_GROUPS = 4
SSD_P = 64
SSD_N = 128
MLA_HEADS = 8
MLA_NOPE = 128
MLA_ROPE = 64
MLA_V = 128
MLA_HP = 256
X_HEADS = 4
ROPE_THETA = 10000.0
ADAM_LR, ADAM_B1, ADAM_B2, ADAM_EPS, ADAM_WD, ADAM_STEP = 0.001, 0.9, 0.999, 1e-08, 0.01, 10
LANE = 128
NEG = -1e30
VMEM_MB = 1024 * 1024


def _pick(n, cap, mult=128):
    if n <= cap:
        return n
    d = (cap // mult) * mult
    while d >= mult:
        if n % d == 0:
            return d
        d -= mult
    return n


def _cparams(sem, mb=40):
    return pltpu.CompilerParams(dimension_semantics=sem, vmem_limit_bytes=mb * VMEM_MB)


def _sigmoid(x):
    return 1.0 / (1.0 + jnp.exp(-x))


def _silu(x):
    return x * _sigmoid(x)


def _dsilu(x):
    s = _sigmoid(x)
    return s * (1.0 + x * (1.0 - s))


def _softplus(x):
    return jnp.maximum(x, 0.0) + jnp.log(1.0 + jnp.exp(-jnp.abs(x)))


def _matmul(a, b, *, ta=False, tb=False, out_dtype=F32, add=None, into=None, name):
    if ta:
        kdim, m = a.shape
    else:
        m, kdim = a.shape
    if tb:
        n, kb = b.shape
    else:
        kb, n = b.shape
    assert kb == kdim, (a.shape, b.shape, ta, tb)
    tm = _pick(m, 512, 128) if ta else _pick(m, 1024, 8)
    tn = _pick(n, 1024 if n <= 1024 else 512, 128)
    tk = _pick(kdim, 2048, 128)
    nk = kdim // tk
    dims = (((0 if ta else 1,), (1 if tb else 0,)), ((), ()))

    has_add = add is not None
    has_stack = into is not None and into[0] is not None

    def body(a_ref, b_ref, *rest):
        add_ref = rest[0] if has_add else None
        o_ref = rest[has_add + has_stack]
        acc = rest[has_add + has_stack + 1:]
        part = lax.dot_general(a_ref[...].astype(BF16), b_ref[...].astype(BF16), dims,
                               preferred_element_type=F32)

        def finish(total):
            if has_add:
                total = total + add_ref[...]
            o_ref[...] = total.astype(o_ref.dtype)

        if nk == 1:
            finish(part)
        else:
            acc_ref, = acc
            k = pl.program_id(2)

            @pl.when(k == 0)
            def _():
                acc_ref[...] = part

            @pl.when(k > 0)
            def _():
                acc_ref[...] += part

            @pl.when(k == nk - 1)
            def _():
                finish(acc_ref[...])

    a_spec = pl.BlockSpec((tk, tm), lambda i, j, k: (k, i)) if ta else pl.BlockSpec((tm, tk), lambda i, j, k: (i, k))
    b_spec = pl.BlockSpec((tn, tk), lambda i, j, k: (j, k)) if tb else pl.BlockSpec((tk, tn), lambda i, j, k: (k, j))
    o_spec = pl.BlockSpec((tm, tn), lambda i, j, k: (i, j))
    operands = [a, b] + ([add] if has_add else [])
    in_specs = [a_spec, b_spec] + ([o_spec] if has_add else [])
    if into is None:
        out_spec, out_shape, aliases = o_spec, jax.ShapeDtypeStruct((m, n), out_dtype), {}
    else:
        stack, layer, depth = into
        out_spec = pl.BlockSpec((None, tm, tn), lambda i, j, k: (layer, i, j))
        out_shape = jax.ShapeDtypeStruct((depth, m, n), out_dtype)
        aliases = {}
        if stack is not None:
            aliases = {len(operands): 0}
            operands.append(stack)
            in_specs.append(HBM_SPEC)
    return pl.pallas_call(
        body, name=name, grid=(m // tm, n // tn, nk),
        in_specs=in_specs, out_specs=out_spec, out_shape=out_shape, input_output_aliases=aliases,
        scratch_shapes=[] if nk == 1 else [pltpu.VMEM((tm, tn), F32)],
        compiler_params=_cparams(("parallel", "parallel", "arbitrary"), 48),
    )(*operands)


def _rms_fwd(x, g, *, out_dtype, name):
    r, w = x.shape
    tr = _pick(r, 512, 8)

    def body(x_ref, g_ref, o_ref):
        xv = x_ref[...]
        rstd = lax.rsqrt(jnp.mean(xv * xv, axis=-1, keepdims=True) + EPS)
        o_ref[...] = (xv * rstd * g_ref[...]).astype(o_ref.dtype)

    return pl.pallas_call(
        body, name=name, grid=(r // tr,),
        in_specs=[pl.BlockSpec((tr, w), lambda i: (i, 0)), pl.BlockSpec((1, w), lambda i: (0, 0))],
        out_specs=pl.BlockSpec((tr, w), lambda i: (i, 0)),
        out_shape=jax.ShapeDtypeStruct((r, w), out_dtype),
        compiler_params=_cparams(("parallel",)),
    )(x, g)


def _rms_bwd(x, g, dy, *, dx_dtype, name, add=None):
    r, w = x.shape
    tr = _pick(r, 512, 8)
    has_add = add is not None

    def body(x_ref, g_ref, dy_ref, *rest):
        if has_add:
            add_ref, dx_ref, dg_ref = rest
        else:
            dx_ref, dg_ref = rest
        xv = x_ref[...]
        dyv = dy_ref[...].astype(F32)
        rstd = lax.rsqrt(jnp.mean(xv * xv, axis=-1, keepdims=True) + EPS)
        xh = xv * rstd
        dyg = dyv * g_ref[...]
        dx = rstd * (dyg - xh * jnp.mean(dyg * xh, axis=-1, keepdims=True))
        if has_add:
            dx = dx + add_ref[...]
        dx_ref[...] = dx.astype(dx_ref.dtype)
        part = jnp.sum(dyv * xh, axis=0, keepdims=True)

        @pl.when(pl.program_id(0) == 0)
        def _():
            dg_ref[...] = part

        @pl.when(pl.program_id(0) > 0)
        def _():
            dg_ref[...] += part

    row = pl.BlockSpec((tr, w), lambda i: (i, 0))
    vec = pl.BlockSpec((1, w), lambda i: (0, 0))
    ins = [x, g, dy] + ([add] if has_add else [])
    return pl.pallas_call(
        body, name=name, grid=(r // tr,),
        in_specs=[row, vec, row] + ([row] if has_add else []),
        out_specs=[row, vec],
        out_shape=[jax.ShapeDtypeStruct((r, w), dx_dtype), jax.ShapeDtypeStruct((1, w), F32)],
        compiler_params=_cparams(("arbitrary",)),
    )(*ins)


def _ln_silu_fwd(x, g, b, *, name):
    r, w = x.shape
    tr = _pick(r, 512, 8)

    def body(x_ref, g_ref, b_ref, o_ref):
        xv = x_ref[...]
        mu = jnp.mean(xv, axis=-1, keepdims=True)
        xc = xv - mu
        rstd = lax.rsqrt(jnp.mean(xc * xc, axis=-1, keepdims=True) + EPS)
        o_ref[...] = _silu(xc * rstd * g_ref[...] + b_ref[...]).astype(o_ref.dtype)

    row = pl.BlockSpec((tr, w), lambda i: (i, 0))
    vec = pl.BlockSpec((1, w), lambda i: (0, 0))
    return pl.pallas_call(
        body, name=name, grid=(r // tr,), in_specs=[row, vec, vec], out_specs=row,
        out_shape=jax.ShapeDtypeStruct((r, w), BF16), compiler_params=_cparams(("parallel",)),
    )(x, g, b)


def _ln_silu_bwd(x, g, b, dy, *, name):
    r, w = x.shape
    tr = _pick(r, 512, 8)

    def body(x_ref, g_ref, b_ref, dy_ref, dx_ref, dg_ref, db_ref):
        xv = x_ref[...]
        mu = jnp.mean(xv, axis=-1, keepdims=True)
        xc = xv - mu
        rstd = lax.rsqrt(jnp.mean(xc * xc, axis=-1, keepdims=True) + EPS)
        xh = xc * rstd
        pre = xh * g_ref[...] + b_ref[...]
        dpre = dy_ref[...].astype(F32) * _dsilu(pre)
        dxh = dpre * g_ref[...]
        dx_ref[...] = rstd * (dxh - jnp.mean(dxh, axis=-1, keepdims=True)
                              - xh * jnp.mean(dxh * xh, axis=-1, keepdims=True))
        pg = jnp.sum(dpre * xh, axis=0, keepdims=True)
        pb = jnp.sum(dpre, axis=0, keepdims=True)

        @pl.when(pl.program_id(0) == 0)
        def _():
            dg_ref[...] = pg
            db_ref[...] = pb

        @pl.when(pl.program_id(0) > 0)
        def _():
            dg_ref[...] += pg
            db_ref[...] += pb

    row = pl.BlockSpec((tr, w), lambda i: (i, 0))
    vec = pl.BlockSpec((1, w), lambda i: (0, 0))
    return pl.pallas_call(
        body, name=name, grid=(r // tr,), in_specs=[row, vec, vec, row], out_specs=[row, vec, vec],
        out_shape=[jax.ShapeDtypeStruct((r, w), F32), jax.ShapeDtypeStruct((1, w), F32),
                   jax.ShapeDtypeStruct((1, w), F32)],
        compiler_params=_cparams(("arbitrary",)),
    )(x, g, b, dy)


CONV_PAD = 32
CONV_T = 256


def _conv_fwd(src, w, b, *, glu, act, name):
    s = src.shape[0]
    k, c = w.shape
    tc = LANE
    ncb = c // tc
    tt = _pick(s, CONV_T, 8)
    assert k - 1 <= CONV_PAD

    def body(*refs):
        if glu:
            a_ref, g_ref, w_ref, b_ref = refs[:4]
            outs = refs[4:-1]
        else:
            a_ref, w_ref, b_ref = refs[:3]
            outs = refs[3:-1]
        xp = refs[-1]
        xp[0:CONV_PAD, :] = jnp.zeros((CONV_PAD, tc), F32)
        if glu:
            xp[CONV_PAD:CONV_PAD + s, :] = a_ref[...] * _sigmoid(g_ref[...])
        else:
            xp[CONV_PAD:CONV_PAD + s, :] = a_ref[...]
        wv = w_ref[...]
        bv = b_ref[...]
        for t0 in range(0, s, tt):
            acc = jnp.zeros((tt, tc), F32) + bv
            for kk in range(k):
                off = CONV_PAD + t0 - (k - 1) + kk
                acc = acc + wv[kk:kk + 1, :] * xp[off:off + tt, :]
            outs[0][t0:t0 + tt, :] = acc
            if act:
                outs[1][t0:t0 + tt, :] = _silu(acc)

    col = pl.BlockSpec((s, tc), lambda j: (0, j))
    in_specs = [col, pl.BlockSpec((s, tc), lambda j: (0, j + ncb))] if glu else [col]
    in_specs += [pl.BlockSpec((k, tc), lambda j: (0, j)), pl.BlockSpec((1, tc), lambda j: (0, j))]
    n_out = 2 if act else 1
    res = pl.pallas_call(
        body, name=name, grid=(ncb,), in_specs=in_specs,
        out_specs=[col] * n_out,
        out_shape=[jax.ShapeDtypeStruct((s, c), F32)] * n_out,
        scratch_shapes=[pltpu.VMEM((CONV_PAD + s, tc), F32)],
        compiler_params=_cparams(("parallel",), 48),
    )(*([src, src] if glu else [src]), w, b)
    return res


def _conv_bwd(src, w, b, dy, pre, *, glu, act, name):
    s = src.shape[0]
    k, c = w.shape
    tc = LANE
    ncb = c // tc
    tt = _pick(s, CONV_T, 8)

    def body(*refs):
        i = 0
        a_ref = refs[i]; i += 1
        if glu:
            g_ref = refs[i]; i += 1
        w_ref = refs[i]; i += 1
        dy_ref = refs[i]; i += 1
        if act:
            pre_ref = refs[i]; i += 1
        da_ref = refs[i]; i += 1
        if glu:
            dg_ref = refs[i]; i += 1
        dw_ref = refs[i]; db_ref = refs[i + 1]
        xp, dp = refs[-2], refs[-1]
        xp[0:CONV_PAD, :] = jnp.zeros((CONV_PAD, tc), F32)
        if glu:
            xp[CONV_PAD:CONV_PAD + s, :] = a_ref[...] * _sigmoid(g_ref[...])
        else:
            xp[CONV_PAD:CONV_PAD + s, :] = a_ref[...]
        dp[s:s + CONV_PAD, :] = jnp.zeros((CONV_PAD, tc), F32)
        if act:
            dp[0:s, :] = dy_ref[...].astype(F32) * _dsilu(pre_ref[...])
        else:
            dp[0:s, :] = dy_ref[...].astype(F32)
        wv = w_ref[...]
        dws = [jnp.zeros((1, tc), F32) for _ in range(k)]
        dbs = jnp.zeros((1, tc), F32)
        for t0 in range(0, s, tt):
            acc = jnp.zeros((tt, tc), F32)
            dcur = dp[t0:t0 + tt, :]
            dbs = dbs + jnp.sum(dcur, axis=0, keepdims=True)
            for kk in range(k):
                acc = acc + wv[kk:kk + 1, :] * dp[t0 + (k - 1) - kk:t0 + (k - 1) - kk + tt, :]
                off = CONV_PAD + t0 - (k - 1) + kk
                dws[kk] = dws[kk] + jnp.sum(dcur * xp[off:off + tt, :], axis=0, keepdims=True)
            if glu:
                av = a_ref[t0:t0 + tt, :]
                sg = _sigmoid(g_ref[t0:t0 + tt, :])
                da_ref[t0:t0 + tt, :] = (acc * sg).astype(da_ref.dtype)
                dg_ref[t0:t0 + tt, :] = (acc * av * sg * (1.0 - sg)).astype(dg_ref.dtype)
            else:
                da_ref[t0:t0 + tt, :] = acc.astype(da_ref.dtype)
        for kk in range(k):
            dw_ref[kk:kk + 1, :] = dws[kk]
        db_ref[...] = dbs

    col = pl.BlockSpec((s, tc), lambda j: (0, j))
    in_specs = [col] + ([pl.BlockSpec((s, tc), lambda j: (0, j + ncb))] if glu else [])
    in_specs += [pl.BlockSpec((k, tc), lambda j: (0, j)), col] + ([col] if act else [])
    ins = ([src, src] if glu else [src]) + [w, dy] + ([pre] if act else [])
    out_specs = [col] + ([col] if glu else []) + [pl.BlockSpec((k, tc), lambda j: (0, j)),
                                                  pl.BlockSpec((1, tc), lambda j: (0, j))]
    out_shape = [jax.ShapeDtypeStruct((s, c), BF16)] * (2 if glu else 1) + [
        jax.ShapeDtypeStruct((k, c), F32), jax.ShapeDtypeStruct((1, c), F32)]
    return pl.pallas_call(
        body, name=name, grid=(ncb,), in_specs=in_specs, out_specs=out_specs, out_shape=out_shape,
        scratch_shapes=[pltpu.VMEM((CONV_PAD + s, tc), F32), pltpu.VMEM((s + CONV_PAD, tc), F32)],
        compiler_params=_cparams(("parallel",), 56),
    )(*ins)


def _ssd_consts():
    e = np.zeros((LANE, SSD_HEADS * SSD_P), np.float32)
    for h in range(SSD_HEADS):
        e[h, h * SSD_P:(h + 1) * SSD_P] = 1.0
    ltri = np.tril(np.ones((CHUNK, CHUNK), np.float32))
    return jnp.asarray(e), jnp.asarray(e.T.copy()), jnp.asarray(ltri), jnp.asarray(ltri.T.copy())


def _ssd_chunk_terms(dtr_ref, dtrt_ref, bias_ref, biast_ref, alog_ref, alogt_ref, e_ref, ltri_ref, utri_ref):
    a_neg = -jnp.exp(alog_ref[...])
    dt = _softplus(dtr_ref[...] + bias_ref[...])
    a = dt * a_neg
    s = jnp.dot(ltri_ref[...], a, precision=HI, preferred_element_type=F32)
    dtt = _softplus(dtrt_ref[...] + biast_ref[...])
    st = jnp.dot(dtt * (-jnp.exp(alogt_ref[...])), utri_ref[...], precision=HI, preferred_element_type=F32)
    ev = e_ref[...]
    s_x = jnp.dot(s, ev, precision=HI, preferred_element_type=F32)
    dt_x = jnp.dot(dt, ev, precision=HI, preferred_element_type=F32)
    return a_neg, dt, s, st, s_x, dt_x


def _ssd_decay(s, st, h, tril):
    seg = s[:, h:h + 1] - st[h:h + 1, :]
    return jnp.exp(jnp.where(tril, seg, NEG))


def _head_masks():
    lane = lax.broadcasted_iota(jnp.int32, (1, SSD_P * 4), 1)
    return [((lane >= r * SSD_P) & (lane < (r + 1) * SSD_P)).astype(F32) for r in range(4)]


def _ssd_scan_fwd(xbc, dtr, dtrt, bias, biast, alog, alogt, dskip, *, name):
    s_len = xbc.shape[0]
    nc = s_len // CHUNK
    e, et, ltri, utri = _ssd_consts()
    gw = SSD_P * 4

    def body(xs_ref, b_ref, c_ref, dtr_ref, dtrt_ref, bias_ref, biast_ref, alog_ref, alogt_ref, d_ref,
             e_ref, ltri_ref, utri_ref, y_ref, prev_ref, state):
        @pl.when(pl.program_id(0) == 0)
        def _():
            state[...] = jnp.zeros_like(state)

        a_neg, dt, s, st, s_x, dt_x = _ssd_chunk_terms(dtr_ref, dtrt_ref, bias_ref, biast_ref, alog_ref,
                                                       alogt_ref, e_ref, ltri_ref, utri_ref)
        s_last = s_x[CHUNK - 1:CHUNK, :]
        es_x = jnp.exp(s_x)
        w_x = jnp.exp(s_last - s_x)
        cd_x = jnp.exp(s_last)
        d_x = jnp.dot(jnp.broadcast_to(d_ref[...], (8, LANE)), e_ref[...], precision=HI,
                      preferred_element_type=F32)[0:1, :]
        xs = xs_ref[...]
        xv = xs * dt_x
        row = lax.broadcasted_iota(jnp.int32, (CHUNK, CHUNK), 0)
        colm = lax.broadcasted_iota(jnp.int32, (CHUNK, CHUNK), 1)
        tril = colm <= row
        masks = _head_masks()
        for g in range(SSD_GROUPS):
            gs = slice(g * gw, (g + 1) * gw)
            bg = b_ref[:, g * SSD_N:(g + 1) * SSD_N].astype(BF16)
            cg = c_ref[:, g * SSD_N:(g + 1) * SSD_N].astype(BF16)
            xg = xv[:, gs]
            hg = state[g]
            prev_ref[g] = hg
            cb = lax.dot_general(cg, bg, (((1,), (1,)), ((), ())), preferred_element_type=F32)
            yg = jnp.dot(cg, hg.astype(BF16), preferred_element_type=F32) * es_x[:, gs]
            for r in range(4):
                m = (cb * _ssd_decay(s, st, g * 4 + r, tril)).astype(BF16)
                yg = yg + jnp.dot(m, (xg * masks[r]).astype(BF16), preferred_element_type=F32)
            y_ref[:, gs] = yg + d_x[:, gs] * xs[:, gs]
            upd = lax.dot_general(bg, (xg * w_x[:, gs]).astype(BF16), (((0,), (0,)), ((), ())),
                                  preferred_element_type=F32)
            state[g] = hg * cd_x[:, gs] + upd

    nh = LANE
    chunk_row = lambda w, cb: pl.BlockSpec((CHUNK, w), lambda i, cb=cb: (i, cb))
    full = lambda a: pl.BlockSpec(a.shape, lambda i: (0,) * a.ndim)
    in_specs = [chunk_row(1024, 0), chunk_row(512, 2), chunk_row(512, 3), chunk_row(nh, 0),
                pl.BlockSpec((None, nh, CHUNK), lambda i: (i, 0, 0)),
                full(bias), full(biast), full(alog), full(alogt), full(dskip), full(e), full(ltri), full(utri)]
    return pl.pallas_call(
        body, name=name, grid=(nc,), in_specs=in_specs,
        out_specs=[pl.BlockSpec((CHUNK, 1024), lambda i: (i, 0)),
                   pl.BlockSpec((None, SSD_GROUPS, SSD_N, gw), lambda i: (i, 0, 0, 0))],
        out_shape=[jax.ShapeDtypeStruct((s_len, 1024), F32),
                   jax.ShapeDtypeStruct((nc, SSD_GROUPS, SSD_N, gw), F32)],
        scratch_shapes=[pltpu.VMEM((SSD_GROUPS, SSD_N, gw), F32)],
        compiler_params=_cparams(("arbitrary",)),
    )(xbc, xbc, xbc, dtr, dtrt, bias, biast, alog, alogt, dskip, e, ltri, utri)


def _ssd_scan_bwd(xbc, dtr, dtrt, bias, biast, alog, alogt, dskip, prev, dy, *, name):
    s_len = xbc.shape[0]
    nc = s_len // CHUNK
    e, et, ltri, utri = _ssd_consts()
    gw = SSD_P * 4

    def body(xs_ref, b_ref, c_ref, dtr_ref, dtrt_ref, bias_ref, biast_ref, alog_ref, alogt_ref, d_ref,
             e_ref, et_ref, ltri_ref, utri_ref, prev_ref, dy_ref,
             dxs_ref, db_ref, dc_ref, ddtr_ref, dalog_ref, dbias_ref, dd_ref, dstate):
        step = pl.program_id(0)

        @pl.when(step == 0)
        def _():
            dstate[...] = jnp.zeros_like(dstate)

        a_neg, dt, s, st, s_x, dt_x = _ssd_chunk_terms(dtr_ref, dtrt_ref, bias_ref, biast_ref, alog_ref,
                                                       alogt_ref, e_ref, ltri_ref, utri_ref)
        s_last = s_x[CHUNK - 1:CHUNK, :]
        es_x = jnp.exp(s_x)
        w_x = jnp.exp(s_last - s_x)
        cd_x = jnp.exp(s_last)
        d_x = jnp.dot(jnp.broadcast_to(d_ref[...], (8, LANE)), e_ref[...], precision=HI,
                      preferred_element_type=F32)[0:1, :]
        xs = xs_ref[...]
        xv = xs * dt_x
        dyv = dy_ref[...]
        row = lax.broadcasted_iota(jnp.int32, (CHUNK, CHUNK), 0)
        colm = lax.broadcasted_iota(jnp.int32, (CHUNK, CHUNK), 1)
        tril = colm <= row
        masks = _head_masks()
        is_last = lax.broadcasted_iota(jnp.int32, (CHUNK, 1), 0) == CHUNK - 1
        nt = (((1,), (1,)), ((), ()))
        tn = (((0,), (0,)), ((), ()))
        ds_parts, ddt_parts = [], []
        head_lane = lax.broadcasted_iota(jnp.int32, (CHUNK, LANE), 1)
        ones = jnp.ones((CHUNK, LANE), F32)
        ds_diag = jnp.zeros((CHUNK, LANE), F32)
        for g in range(SSD_GROUPS):
            gs = slice(g * gw, (g + 1) * gw)
            bg = b_ref[:, g * SSD_N:(g + 1) * SSD_N].astype(BF16)
            cg = c_ref[:, g * SSD_N:(g + 1) * SSD_N].astype(BF16)
            xg = xv[:, gs]
            xgb = xg.astype(BF16)
            hg = prev_ref[g]
            hgb = hg.astype(BF16)
            dsg = dstate[g]
            dsgb = dsg.astype(BF16)
            dyg = dyv[:, gs]
            dye = (dyg * es_x[:, gs]).astype(BF16)
            xw = (xg * w_x[:, gs]).astype(BF16)
            cb = lax.dot_general(cg, bg, nt, preferred_element_type=F32)
            dcg = lax.dot_general(dye, hgb, nt, preferred_element_type=F32)
            dh = lax.dot_general(cg, dye, tn, preferred_element_type=F32)
            bds = jnp.dot(bg, dsgb, preferred_element_type=F32)
            yoff = es_x[:, gs] * jnp.dot(cg, hgb, preferred_element_type=F32)
            dx_state = w_x[:, gs] * bds
            dbg = lax.dot_general(xw, dsgb, nt, preferred_element_type=F32)
            dxd = jnp.zeros((CHUNK, gw), F32)
            dcb = jnp.zeros((CHUNK, CHUNK), F32)
            for r in range(4):
                dec = _ssd_decay(s, st, g * 4 + r, tril)
                mf = cb * dec
                m = mf.astype(BF16)
                dym = (dyg * masks[r]).astype(BF16)
                dm = lax.dot_general(dym, xgb, nt, preferred_element_type=F32)
                dxd = dxd + lax.dot_general(m, dym, tn, preferred_element_type=F32)
                dcb = dcb + dm * dec
                qm = dm * mf
                rc = (jnp.sum(qm, axis=1, keepdims=True)
                      - lax.dot_general(qm, ones, tn, precision=HI, preferred_element_type=F32))
                ds_diag = ds_diag + jnp.where(head_lane == g * 4 + r, rc, 0.0)
            dcbb = dcb.astype(BF16)
            dcg = dcg + jnp.dot(dcbb, bg, preferred_element_type=F32)
            dbg = dbg + lax.dot_general(dcbb, cg, tn, preferred_element_type=F32)
            dxg = dxd + dx_state
            extra = (jnp.sum(xg * dx_state, axis=0, keepdims=True)
                     + cd_x[:, gs] * jnp.sum(dsg * hg, axis=0, keepdims=True))
            ds_parts.append(dyg * yoff - xg * dx_state + jnp.where(is_last, extra, 0.0))
            ddt_parts.append(dxg * xs[:, gs])
            dxs_ref[:, gs] = dxg * dt_x[:, gs] + d_x[:, gs] * dyg
            db_ref[:, g * SSD_N:(g + 1) * SSD_N] = dbg
            dc_ref[:, g * SSD_N:(g + 1) * SSD_N] = dcg
            dstate[g] = cd_x[:, gs] * dsg + dh
        etv = et_ref[...]
        ds = ds_diag + jnp.dot(jnp.concatenate(ds_parts, axis=1), etv, precision=HI, preferred_element_type=F32)
        da = jnp.dot(utri_ref[...], ds, precision=HI, preferred_element_type=F32)
        ddt = da * a_neg + jnp.dot(jnp.concatenate(ddt_parts, axis=1), etv, precision=HI,
                                   preferred_element_type=F32)
        ddtr = ddt * _sigmoid(dtr_ref[...] + bias_ref[...])
        ddtr_ref[...] = ddtr.astype(ddtr_ref.dtype)
        p_alog = jnp.sum(da * dt, axis=0, keepdims=True) * a_neg
        p_bias = jnp.sum(ddtr, axis=0, keepdims=True)
        p_d = jnp.dot(jnp.broadcast_to(jnp.sum(dyv * xs, axis=0, keepdims=True), (8, SSD_HEADS * SSD_P)), etv,
                      precision=HI, preferred_element_type=F32)[0:1, :]

        @pl.when(step == 0)
        def _():
            dalog_ref[...] = p_alog
            dbias_ref[...] = p_bias
            dd_ref[...] = p_d

        @pl.when(step > 0)
        def _():
            dalog_ref[...] += p_alog
            dbias_ref[...] += p_bias
            dd_ref[...] += p_d

    nh = LANE
    rev = lambda i: nc - 1 - i
    chunk_row = lambda w, cb: pl.BlockSpec((CHUNK, w), lambda i, cb=cb: (rev(i), cb))
    full = lambda a: pl.BlockSpec(a.shape, lambda i: (0,) * a.ndim)
    vec = pl.BlockSpec((1, nh), lambda i: (0, 0))
    in_specs = [chunk_row(1024, 0), chunk_row(512, 2), chunk_row(512, 3), chunk_row(nh, 0),
                pl.BlockSpec((None, nh, CHUNK), lambda i: (rev(i), 0, 0)),
                full(bias), full(biast), full(alog), full(alogt), full(dskip), full(e), full(et), full(ltri),
                full(utri),
                pl.BlockSpec((None, SSD_GROUPS, SSD_N, gw), lambda i: (rev(i), 0, 0, 0)), chunk_row(1024, 0)]
    return pl.pallas_call(
        body, name=name, grid=(nc,), in_specs=in_specs,
        out_specs=[chunk_row(1024, 0), chunk_row(512, 0), chunk_row(512, 0), chunk_row(nh, 0), vec, vec, vec],
        out_shape=[jax.ShapeDtypeStruct((s_len, 1024), F32), jax.ShapeDtypeStruct((s_len, 512), F32),
                   jax.ShapeDtypeStruct((s_len, 512), F32), jax.ShapeDtypeStruct((s_len, nh), BF16),
                   jax.ShapeDtypeStruct((1, nh), F32), jax.ShapeDtypeStruct((1, nh), F32),
                   jax.ShapeDtypeStruct((1, nh), F32)],
        scratch_shapes=[pltpu.VMEM((SSD_GROUPS, SSD_N, gw), F32)],
        compiler_params=_cparams(("arbitrary",)),
    )(xbc, xbc, xbc, dtr, dtrt, bias, biast, alog, alogt, dskip, e, et, ltri, utri, prev, dy)


def _ssd_gate_fwd(y, z, g, *, name):
    r, w = y.shape
    tr = _pick(r, 512, 8)
    gw = w // SSD_GROUPS

    def body(y_ref, z_ref, g_ref, o_ref):
        for k in range(SSD_GROUPS):
            cs = slice(k * gw, (k + 1) * gw)
            t = y_ref[:, cs] * _silu(z_ref[:, cs])
            rstd = lax.rsqrt(jnp.mean(t * t, axis=-1, keepdims=True) + EPS)
            o_ref[:, cs] = (t * rstd * g_ref[:, cs]).astype(o_ref.dtype)

    row = pl.BlockSpec((tr, w), lambda i: (i, 0))
    vec = pl.BlockSpec((1, w), lambda i: (0, 0))
    return pl.pallas_call(
        body, name=name, grid=(r // tr,), in_specs=[row, row, vec], out_specs=row,
        out_shape=jax.ShapeDtypeStruct((r, w), BF16), compiler_params=_cparams(("parallel",)),
    )(y, z, g)


def _ssd_gate_bwd(y, z, g, do, *, name):
    r, w = y.shape
    tr = _pick(r, 512, 8)
    gw = w // SSD_GROUPS

    def body(y_ref, z_ref, g_ref, do_ref, dy_ref, dz_ref, dg_ref):
        parts = []
        for k in range(SSD_GROUPS):
            cs = slice(k * gw, (k + 1) * gw)
            yv = y_ref[:, cs]
            zv = z_ref[:, cs]
            sz = _silu(zv)
            t = yv * sz
            rstd = lax.rsqrt(jnp.mean(t * t, axis=-1, keepdims=True) + EPS)
            th = t * rstd
            dov = do_ref[:, cs].astype(F32)
            dog = dov * g_ref[:, cs]
            dt = rstd * (dog - th * jnp.mean(dog * th, axis=-1, keepdims=True))
            dy_ref[:, cs] = dt * sz
            dz_ref[:, cs] = (dt * yv * _dsilu(zv)).astype(dz_ref.dtype)
            parts.append(jnp.sum(dov * th, axis=0, keepdims=True))
        pg = jnp.concatenate(parts, axis=1)

        @pl.when(pl.program_id(0) == 0)
        def _():
            dg_ref[...] = pg

        @pl.when(pl.program_id(0) > 0)
        def _():
            dg_ref[...] += pg

    row = pl.BlockSpec((tr, w), lambda i: (i, 0))
    vec = pl.BlockSpec((1, w), lambda i: (0, 0))
    return pl.pallas_call(
        body, name=name, grid=(r // tr,), in_specs=[row, row, vec, row], out_specs=[row, row, vec],
        out_shape=[jax.ShapeDtypeStruct((r, w), F32), jax.ShapeDtypeStruct((r, w), BF16),
                   jax.ShapeDtypeStruct((1, w), F32)],
        compiler_params=_cparams(("arbitrary",)),
    )(y, z, g, do)


def _rope_swap(x):
    lane = lax.broadcasted_iota(jnp.int32, x.shape, 1)
    lo = pltpu.roll(x, 96, 1)
    hi = pltpu.roll(x, 32, 1)
    return jnp.where(lane < 32, lo, jnp.where(lane < 64, hi, 0.0))


def _norm_part(v, g, n):
    rstd = lax.rsqrt(jnp.sum(v * v, axis=-1, keepdims=True) * (1.0 / n) + EPS)
    xh = v * rstd
    return xh * g, xh, rstd


def _norm_part_bwd(dout, g, xh, rstd, n):
    dg = dout * g
    return rstd * (dg - xh * (jnp.sum(dg * xh, axis=-1, keepdims=True) * (1.0 / n)))


def _mla_prep_fwd(q, kv, krr, cs, sn, gq, gk, *, name):
    s = q.shape[0]
    tr = _pick(s, 256, 8)
    hp = MLA_HP

    def body(q_ref, kv_ref, krr_ref, cs_ref, sn_ref, gq_ref, gk_ref, qf_ref, kf_ref, v_ref):
        csv, snv = cs_ref[...], sn_ref[...]
        gqn, gqr = gq_ref[:, 0:128], gq_ref[:, 128:256]
        gkn, gkr = gk_ref[:, 0:128], gk_ref[:, 128:256]
        kr, _, _ = _norm_part(krr_ref[...], gkr, MLA_ROPE)
        kr = (kr * csv + _rope_swap(kr) * snv).astype(BF16)
        for h in range(MLA_HEADS):
            qn, _, _ = _norm_part(q_ref[:, h * hp:h * hp + 128], gqn, MLA_NOPE)
            qr, _, _ = _norm_part(q_ref[:, h * hp + 128:(h + 1) * hp], gqr, MLA_ROPE)
            qr = qr * csv + _rope_swap(qr) * snv
            qf_ref[h, :, 0:128] = qn.astype(BF16)
            qf_ref[h, :, 128:256] = qr.astype(BF16)
            kn, _, _ = _norm_part(kv_ref[:, h * hp:h * hp + 128], gkn, MLA_NOPE)
            kf_ref[h, :, 0:128] = kn.astype(BF16)
            kf_ref[h, :, 128:256] = kr
            v_ref[h] = kv_ref[:, h * hp + 128:(h + 1) * hp].astype(BF16)

    row = lambda w: pl.BlockSpec((tr, w), lambda i: (i, 0))
    vec = pl.BlockSpec((1, hp), lambda i: (0, 0))
    hrow = lambda w: pl.BlockSpec((MLA_HEADS, tr, w), lambda i: (0, i, 0))
    return pl.pallas_call(
        body, name=name, grid=(s // tr,),
        in_specs=[row(MLA_HEADS * hp), row(MLA_HEADS * hp), row(128), row(128), row(128), vec, vec],
        out_specs=[hrow(hp), hrow(hp), hrow(128)],
        out_shape=[jax.ShapeDtypeStruct((MLA_HEADS, s, hp), BF16), jax.ShapeDtypeStruct((MLA_HEADS, s, hp), BF16),
                   jax.ShapeDtypeStruct((MLA_HEADS, s, 128), BF16)],
        compiler_params=_cparams(("parallel",)),
    )(q, kv, krr, cs, sn, gq, gk)


def _mla_prep_bwd(q, kv, krr, cs, sn, gq, gk, dqf, dkf, dv, *, name):
    s = q.shape[0]
    tr = _pick(s, 256, 8)
    hp = MLA_HP

    def body(q_ref, kv_ref, krr_ref, cs_ref, sn_ref, gq_ref, gk_ref, dqf_ref, dkf_ref, dv_ref,
             dq_ref, dkv_ref, dkrr_ref, dgq_ref, dgk_ref):
        csv, snv = cs_ref[...], sn_ref[...]
        gqn, gqr = gq_ref[:, 0:128], gq_ref[:, 128:256]
        gkn, gkr = gk_ref[:, 0:128], gk_ref[:, 128:256]
        _, krh, krs = _norm_part(krr_ref[...], gkr, MLA_ROPE)
        dkr_sum = jnp.zeros((tr, 128), F32)
        pgqn = jnp.zeros((1, 128), F32)
        pgqr = jnp.zeros((1, 128), F32)
        pgkn = jnp.zeros((1, 128), F32)
        for h in range(MLA_HEADS):
            _, qnh, qns = _norm_part(q_ref[:, h * hp:h * hp + 128], gqn, MLA_NOPE)
            _, qrh, qrs = _norm_part(q_ref[:, h * hp + 128:(h + 1) * hp], gqr, MLA_ROPE)
            dqn = dqf_ref[h, :, 0:128]
            drr = dqf_ref[h, :, 128:256]
            dqr = drr * csv + _rope_swap(drr * snv)
            dq_ref[:, h * hp:h * hp + 128] = _norm_part_bwd(dqn, gqn, qnh, qns, MLA_NOPE).astype(dq_ref.dtype)
            dq_ref[:, h * hp + 128:(h + 1) * hp] = _norm_part_bwd(dqr, gqr, qrh, qrs, MLA_ROPE).astype(dq_ref.dtype)
            pgqn = pgqn + jnp.sum(dqn * qnh, axis=0, keepdims=True)
            pgqr = pgqr + jnp.sum(dqr * qrh, axis=0, keepdims=True)
            _, knh, kns = _norm_part(kv_ref[:, h * hp:h * hp + 128], gkn, MLA_NOPE)
            dkn = dkf_ref[h, :, 0:128]
            dkv_ref[:, h * hp:h * hp + 128] = _norm_part_bwd(dkn, gkn, knh, kns, MLA_NOPE).astype(dkv_ref.dtype)
            dkv_ref[:, h * hp + 128:(h + 1) * hp] = dv_ref[h].astype(dkv_ref.dtype)
            pgkn = pgkn + jnp.sum(dkn * knh, axis=0, keepdims=True)
            dkr_sum = dkr_sum + dkf_ref[h, :, 128:256]
        dkr = dkr_sum * csv + _rope_swap(dkr_sum * snv)
        dkrr_ref[...] = _norm_part_bwd(dkr, gkr, krh, krs, MLA_ROPE).astype(dkrr_ref.dtype)
        pgkr = jnp.sum(dkr * krh, axis=0, keepdims=True)
        pq = jnp.concatenate([pgqn, pgqr], axis=1)
        pk = jnp.concatenate([pgkn, pgkr], axis=1)

        @pl.when(pl.program_id(0) == 0)
        def _():
            dgq_ref[...] = pq
            dgk_ref[...] = pk

        @pl.when(pl.program_id(0) > 0)
        def _():
            dgq_ref[...] += pq
            dgk_ref[...] += pk

    row = lambda w: pl.BlockSpec((tr, w), lambda i: (i, 0))
    vec = pl.BlockSpec((1, hp), lambda i: (0, 0))
    hrow = lambda w: pl.BlockSpec((MLA_HEADS, tr, w), lambda i: (0, i, 0))
    return pl.pallas_call(
        body, name=name, grid=(s // tr,),
        in_specs=[row(MLA_HEADS * hp), row(MLA_HEADS * hp), row(128), row(128), row(128), vec, vec,
                  hrow(hp), hrow(hp), hrow(128)],
        out_specs=[row(MLA_HEADS * hp), row(MLA_HEADS * hp), row(128), vec, vec],
        out_shape=[jax.ShapeDtypeStruct((s, MLA_HEADS * hp), BF16), jax.ShapeDtypeStruct((s, MLA_HEADS * hp), BF16),
                   jax.ShapeDtypeStruct((s, 128), BF16), jax.ShapeDtypeStruct((1, hp), F32),
                   jax.ShapeDtypeStruct((1, hp), F32)],
        compiler_params=_cparams(("arbitrary",), 48),
    )(q, kv, krr, cs, sn, gq, gk, dqf, dkf, dv)


ATT_T = 512


def _chunk_mask(t):
    r = lax.shift_right_logical(lax.broadcasted_iota(jnp.int32, (t, t), 0), 6)
    c = lax.shift_right_logical(lax.broadcasted_iota(jnp.int32, (t, t), 1), 6)
    return c <= r


def _mla_attn_fwd(qf, kf, v, *, name):
    nh, s, hp = qf.shape
    t = _pick(s, ATT_T, CHUNK)
    scale = (MLA_NOPE + MLA_ROPE) ** -0.5
    nt = (((1,), (1,)), ((), ()))

    def body(q_ref, k_ref, v_ref, o_ref, lse_ref):
        i = pl.program_id(1)
        q = q_ref[...]

        def block(j, carry, masked):
            m, l, acc = carry
            start = pl.multiple_of(j * t, t)
            k = k_ref[pl.ds(start, t), :]
            sc = lax.dot_general(q, k, nt, preferred_element_type=F32) * scale
            if masked:
                sc = jnp.where(_chunk_mask(t), sc, NEG)
            m_new = jnp.maximum(m, jnp.max(sc, axis=-1, keepdims=True))
            alpha = jnp.exp(m - m_new)
            p = jnp.exp(sc - m_new)
            l = alpha * l + jnp.sum(p, axis=-1, keepdims=True)
            acc = alpha * acc + jnp.dot(p.astype(BF16), v_ref[pl.ds(start, t), :], preferred_element_type=F32)
            return m_new, l, acc

        init = (jnp.full((t, 1), NEG, F32), jnp.zeros((t, 1), F32), jnp.zeros((t, MLA_V), F32))
        carry = lax.fori_loop(0, i, lambda j, c: block(j, c, False), init)
        m, l, acc = block(i, carry, True)
        o_ref[...] = acc / l
        lse_ref[...] = m + jnp.log(l)

    return pl.pallas_call(
        body, name=name, grid=(nh, s // t),
        in_specs=[pl.BlockSpec((None, t, hp), lambda h, i: (h, i, 0)),
                  pl.BlockSpec((None, s, hp), lambda h, i: (h, 0, 0)),
                  pl.BlockSpec((None, s, MLA_V), lambda h, i: (h, 0, 0))],
        out_specs=[pl.BlockSpec((t, MLA_V), lambda h, i: (i, h)),
                   pl.BlockSpec((None, t, 1), lambda h, i: (h, i, 0))],
        out_shape=[jax.ShapeDtypeStruct((s, nh * MLA_V), F32), jax.ShapeDtypeStruct((nh, s, 1), F32)],
        compiler_params=_cparams(("parallel", "arbitrary"), 48),
    )(qf, kf, v)


def _mla_attn_bwd(qf, kf, v, o, lse, do, *, name):
    nh, s, hp = qf.shape
    t = _pick(s, ATT_T, CHUNK)
    nb = s // t
    scale = (MLA_NOPE + MLA_ROPE) ** -0.5
    nt = (((1,), (1,)), ((), ()))
    tn = (((0,), (0,)), ((), ()))

    def body(q_ref, k_ref, v_ref, o_ref, lse_ref, do_ref, dq_ref, dk_ref, dv_ref, delta):
        j = pl.program_id(1)

        @pl.when(j == 0)
        def _():
            dq_ref[...] = jnp.zeros_like(dq_ref)
            delta[...] = jnp.sum(do_ref[...] * o_ref[...], axis=-1, keepdims=True)

        k = k_ref[...]
        vv = v_ref[...]

        def block(i, carry, masked):
            dk, dv = carry
            start = pl.multiple_of(i * t, t)
            q = q_ref[pl.ds(start, t), :]
            dob = do_ref[pl.ds(start, t), :].astype(BF16)
            sc = lax.dot_general(q, k, nt, preferred_element_type=F32) * scale
            if masked:
                sc = jnp.where(_chunk_mask(t), sc, NEG)
            p = jnp.exp(sc - lse_ref[pl.ds(start, t), :])
            dp = lax.dot_general(dob, vv, nt, preferred_element_type=F32)
            ds = (p * (dp - delta[pl.ds(start, t), :]) * scale).astype(BF16)
            dv = dv + lax.dot_general(p.astype(BF16), dob, tn, preferred_element_type=F32)
            dk = dk + lax.dot_general(ds, q, tn, preferred_element_type=F32)
            dq_ref[pl.ds(start, t), :] += jnp.dot(ds, k, preferred_element_type=F32)
            return dk, dv

        init = (jnp.zeros((t, hp), F32), jnp.zeros((t, MLA_V), F32))
        carry = block(j, init, True)
        dk, dv = lax.fori_loop(j + 1, nb, lambda i, c: block(i, c, False), carry)
        dk_ref[...] = dk
        dv_ref[...] = dv

    whole = lambda w: pl.BlockSpec((None, s, w), lambda h, j: (h, 0, 0))
    blk = lambda w: pl.BlockSpec((None, t, w), lambda h, j: (h, j, 0))
    colh = pl.BlockSpec((s, MLA_V), lambda h, j: (0, h))
    return pl.pallas_call(
        body, name=name, grid=(nh, nb),
        in_specs=[whole(hp), blk(hp), blk(MLA_V), colh, whole(1), colh],
        out_specs=[whole(hp), blk(hp), blk(MLA_V)],
        out_shape=[jax.ShapeDtypeStruct((nh, s, hp), F32), jax.ShapeDtypeStruct((nh, s, hp), F32),
                   jax.ShapeDtypeStruct((nh, s, MLA_V), F32)],
        scratch_shapes=[pltpu.VMEM((s, 1), F32)],
        compiler_params=_cparams(("parallel", "arbitrary"), 56),
    )(qf, kf, v, o, lse, do)


def _merge_fwd(gl, gb, ys, yc, ym, *, name):
    s, d = ys.shape
    tr = _pick(s, 256, 8)

    def body(gl_ref, gb_ref, ys_ref, yc_ref, ym_ref, o_ref):
        acc = jnp.zeros((tr, d), F32)
        for k, y_ref in enumerate((ys_ref, yc_ref, ym_ref)):
            gt = _sigmoid(gl_ref[:, k * d:(k + 1) * d] + gb_ref[:, k * d:(k + 1) * d])
            acc = acc + gt * y_ref[...]
        o_ref[...] = acc.astype(o_ref.dtype)

    row = lambda w: pl.BlockSpec((tr, w), lambda i: (i, 0))
    return pl.pallas_call(
        body, name=name, grid=(s // tr,),
        in_specs=[row(3 * d), pl.BlockSpec((1, 3 * d), lambda i: (0, 0)), row(d), row(d), row(d)],
        out_specs=row(d), out_shape=jax.ShapeDtypeStruct((s, d), BF16),
        compiler_params=_cparams(("parallel",)),
    )(gl, gb, ys, yc, ym)


def _merge_bwd(gl, gb, ys, yc, ym, dm, *, name):
    s, d = ys.shape
    tr = _pick(s, 256, 8)

    def body(gl_ref, gb_ref, ys_ref, yc_ref, ym_ref, dm_ref, dgl_ref, dgb_ref, dys_ref, dyc_ref, dym_ref):
        dmv = dm_ref[...]
        parts = []
        for k, (y_ref, dy_ref) in enumerate(((ys_ref, dys_ref), (yc_ref, dyc_ref), (ym_ref, dym_ref))):
            gt = _sigmoid(gl_ref[:, k * d:(k + 1) * d] + gb_ref[:, k * d:(k + 1) * d])
            dy_ref[...] = (gt * dmv).astype(dy_ref.dtype)
            dl = dmv * y_ref[...] * gt * (1.0 - gt)
            dgl_ref[:, k * d:(k + 1) * d] = dl.astype(dgl_ref.dtype)
            parts.append(jnp.sum(dl, axis=0, keepdims=True))
        pb = jnp.concatenate(parts, axis=1)

        @pl.when(pl.program_id(0) == 0)
        def _():
            dgb_ref[...] = pb

        @pl.when(pl.program_id(0) > 0)
        def _():
            dgb_ref[...] += pb

    row = lambda w: pl.BlockSpec((tr, w), lambda i: (i, 0))
    vec = pl.BlockSpec((1, 3 * d), lambda i: (0, 0))
    return pl.pallas_call(
        body, name=name, grid=(s // tr,),
        in_specs=[row(3 * d), vec, row(d), row(d), row(d), row(d)],
        out_specs=[row(3 * d), vec, row(d), row(d), row(d)],
        out_shape=[jax.ShapeDtypeStruct((s, 3 * d), BF16), jax.ShapeDtypeStruct((1, 3 * d), F32)]
        + [jax.ShapeDtypeStruct((s, d), BF16)] * 3,
        compiler_params=_cparams(("arbitrary",)),
    )(gl, gb, ys, yc, ym, dm)


def _xattn_fwd(q, k, v, gq, *, name):
    s, d = q.shape
    dh = d // X_HEADS
    tr = _pick(s, 512, 8)
    scale = dh ** -0.5
    nt = (((1,), (1,)), ((), ()))

    def body(q_ref, k_ref, v_ref, gq_ref, o_ref):
        for h in range(X_HEADS):
            cs = slice(h * dh, (h + 1) * dh)
            qn, _, _ = _norm_part(q_ref[:, cs], gq_ref[...], dh)
            sc = lax.dot_general(qn.astype(BF16), k_ref[:, cs], nt, preferred_element_type=F32) * scale
            p = jnp.exp(sc - jnp.max(sc, axis=-1, keepdims=True))
            p = p / jnp.sum(p, axis=-1, keepdims=True)
            o_ref[:, cs] = jnp.dot(p.astype(BF16), v_ref[:, cs], preferred_element_type=F32)

    row = pl.BlockSpec((tr, d), lambda i: (i, 0))
    mem = pl.BlockSpec(k.shape, lambda i: (0, 0))
    return pl.pallas_call(
        body, name=name, grid=(s // tr,),
        in_specs=[row, mem, mem, pl.BlockSpec((1, dh), lambda i: (0, 0))], out_specs=row,
        out_shape=jax.ShapeDtypeStruct((s, d), F32), compiler_params=_cparams(("parallel",)),
    )(q, k, v, gq)


def _xattn_bwd(q, k, v, gq, do, *, name):
    s, d = q.shape
    dh = d // X_HEADS
    tr = _pick(s, 512, 8)
    scale = dh ** -0.5
    nt = (((1,), (1,)), ((), ()))
    tn = (((0,), (0,)), ((), ()))

    def body(q_ref, k_ref, v_ref, gq_ref, do_ref, dq_ref, dk_ref, dv_ref, dgq_ref):
        first = pl.program_id(0) == 0
        pg = jnp.zeros((1, dh), F32)
        for h in range(X_HEADS):
            cs = slice(h * dh, (h + 1) * dh)
            qn, qh, qs = _norm_part(q_ref[:, cs], gq_ref[...], dh)
            qnb = qn.astype(BF16)
            kh = k_ref[:, cs]
            sc = lax.dot_general(qnb, kh, nt, preferred_element_type=F32) * scale
            p = jnp.exp(sc - jnp.max(sc, axis=-1, keepdims=True))
            p = p / jnp.sum(p, axis=-1, keepdims=True)
            dob = do_ref[:, cs].astype(BF16)
            dp = lax.dot_general(dob, v_ref[:, cs], nt, preferred_element_type=F32)
            ds = (p * (dp - jnp.sum(dp * p, axis=-1, keepdims=True)) * scale).astype(BF16)
            dqn = jnp.dot(ds, kh, preferred_element_type=F32)
            dq_ref[:, cs] = _norm_part_bwd(dqn, gq_ref[...], qh, qs, dh).astype(dq_ref.dtype)
            pg = pg + jnp.sum(dqn * qh, axis=0, keepdims=True)
            pv = lax.dot_general(p.astype(BF16), dob, tn, preferred_element_type=F32)
            pk = lax.dot_general(ds, qnb, tn, preferred_element_type=F32)

            @pl.when(first)
            def _():
                dv_ref[:, cs] = pv
                dk_ref[:, cs] = pk

            @pl.when(jnp.logical_not(first))
            def _():
                dv_ref[:, cs] += pv
                dk_ref[:, cs] += pk

        @pl.when(first)
        def _():
            dgq_ref[...] = pg

        @pl.when(jnp.logical_not(first))
        def _():
            dgq_ref[...] += pg

    row = pl.BlockSpec((tr, d), lambda i: (i, 0))
    mem = pl.BlockSpec(k.shape, lambda i: (0, 0))
    vec = pl.BlockSpec((1, dh), lambda i: (0, 0))
    return pl.pallas_call(
        body, name=name, grid=(s // tr,),
        in_specs=[row, mem, mem, vec, row], out_specs=[row, mem, mem, vec],
        out_shape=[jax.ShapeDtypeStruct((s, d), BF16), jax.ShapeDtypeStruct(k.shape, F32),
                   jax.ShapeDtypeStruct(k.shape, F32), jax.ShapeDtypeStruct((1, dh), F32)],
        compiler_params=_cparams(("arbitrary",)),
    )(q, k, v, gq, do)


def _swiglu_fwd(h1, *, name):
    s, w2 = h1.shape
    w = w2 // 2
    tr = _pick(s, 256, 8)
    tc = _pick(w, 1408, 128)
    ncb = w // tc

    def body(g_ref, u_ref, o_ref):
        o_ref[...] = (_silu(g_ref[...]) * u_ref[...]).astype(o_ref.dtype)

    return pl.pallas_call(
        body, name=name, grid=(s // tr, ncb),
        in_specs=[pl.BlockSpec((tr, tc), lambda i, j: (i, j)), pl.BlockSpec((tr, tc), lambda i, j: (i, j + ncb))],
        out_specs=pl.BlockSpec((tr, tc), lambda i, j: (i, j)),
        out_shape=jax.ShapeDtypeStruct((s, w), BF16), compiler_params=_cparams(("parallel", "parallel")),
    )(h1, h1)


def _swiglu_bwd(h1, dact, *, name):
    s, w2 = h1.shape
    w = w2 // 2
    tr = _pick(s, 256, 8)
    tc = _pick(w, 1408, 128)
    ncb = w // tc

    def body(g_ref, u_ref, d_ref, dg_ref, du_ref):
        gv = g_ref[...]
        dv = d_ref[...]
        dg_ref[...] = (dv * u_ref[...] * _dsilu(gv)).astype(dg_ref.dtype)
        du_ref[...] = (dv * _silu(gv)).astype(du_ref.dtype)

    blk = pl.BlockSpec((tr, tc), lambda i, j: (i, j))
    dg, du = pl.pallas_call(
        body, name=name, grid=(s // tr, ncb),
        in_specs=[blk, pl.BlockSpec((tr, tc), lambda i, j: (i, j + ncb)), blk],
        out_specs=[blk, blk],
        out_shape=[jax.ShapeDtypeStruct((s, w), BF16)] * 2, compiler_params=_cparams(("parallel", "parallel")),
    )(h1, h1, dact)
    return jnp.concatenate([dg, du], axis=1)


def _add(a, b, *, name):
    r, w = a.shape
    tr = _pick(r, 512, 8)

    def body(a_ref, b_ref, o_ref):
        o_ref[...] = a_ref[...] + b_ref[...].astype(F32)

    row = pl.BlockSpec((tr, w), lambda i: (i, 0))
    return pl.pallas_call(
        body, name=name, grid=(r // tr,), in_specs=[row, row], out_specs=row,
        out_shape=jax.ShapeDtypeStruct((r, w), F32), compiler_params=_cparams(("parallel",)),
    )(a, b)


def _loss(y, target, *, name):
    r, w = y.shape
    tr = _pick(r, 512, 8)

    def body(y_ref, t_ref, dy_ref, l_ref):
        err = y_ref[...] - t_ref[...]
        dy_ref[...] = err * (1.0 / w)
        part = jnp.zeros((8, LANE), F32) + 0.5 * jnp.sum(jnp.mean(err * err, axis=-1, keepdims=True))

        @pl.when(pl.program_id(0) == 0)
        def _():
            l_ref[...] = part

        @pl.when(pl.program_id(0) > 0)
        def _():
            l_ref[...] += part

    row = pl.BlockSpec((tr, w), lambda i: (i, 0))
    dy, l = pl.pallas_call(
        body, name=name, grid=(r // tr,), in_specs=[row, row],
        out_specs=[row, pl.BlockSpec((8, LANE), lambda i: (0, 0))],
        out_shape=[jax.ShapeDtypeStruct((r, w), F32), jax.ShapeDtypeStruct((8, LANE), F32)],
        compiler_params=_cparams(("arbitrary",)),
    )(y, target)
    return dy, l[0, 0]


def _adamw(w, g, m, v, *, name):
    r, c = w.shape
    tr = _pick(r, 256, 8)
    c1 = 1.0 - ADAM_B1 ** ADAM_STEP
    c2 = 1.0 - ADAM_B2 ** ADAM_STEP

    def body(w_ref, g_ref, m_ref, v_ref, d_ref, nm_ref, nv_ref):
        gv = g_ref[...]
        nm = ADAM_B1 * m_ref[...] + (1.0 - ADAM_B1) * gv
        nv = ADAM_B2 * v_ref[...] + (1.0 - ADAM_B2) * (gv * gv)
        nm_ref[...] = nm
        nv_ref[...] = nv
        d_ref[...] = -ADAM_LR * ((nm / c1) / (jnp.sqrt(nv / c2) + ADAM_EPS) + ADAM_WD * w_ref[...])

    row = pl.BlockSpec((tr, c), lambda i: (i, 0))
    return pl.pallas_call(
        body, name=name, grid=(r // tr,), in_specs=[row] * 4, out_specs=[row] * 3,
        out_shape=[jax.ShapeDtypeStruct((r, c), F32)] * 3, compiler_params=_cparams(("parallel",)),
    )(w, g, m, v)


IN_SPLIT = dict(z=(0, 1024), xbc=(1024, 3072), dt=(3072, 3200), glu=(3200, 5248), ql=(5248, 5632),
                ckv=(5632, 5888), kr=(5888, 6016), gate=(6016, 9088))


IN_WIDTH_PAD = 9216


def _w_in_pad(w):
    zeros = lambda n: jnp.zeros(w.shape[:-1] + (n,), w.dtype)
    return jnp.concatenate([w[..., :3088], zeros(112), w[..., 3088:5840], zeros(64), w[..., 5840:],
                            zeros(IN_WIDTH_PAD - 9088)], axis=-1)


def _w_in_unpad(g):
    return jnp.concatenate([g[..., :3088], g[..., 3200:5952], g[..., 6016:9088]], axis=-1)


def _qb_pad(w):
    lead = w.shape[:-1]
    w = w.reshape(lead + (MLA_HEADS, MLA_NOPE + MLA_ROPE))
    w = jnp.concatenate([w, jnp.zeros(lead + (MLA_HEADS, MLA_HP - MLA_NOPE - MLA_ROPE), w.dtype)], axis=-1)
    return w.reshape(lead + (MLA_HEADS * MLA_HP,))


def _qb_unpad(g):
    lead = g.shape[:-1]
    g = g.reshape(lead + (MLA_HEADS, MLA_HP))[..., :MLA_NOPE + MLA_ROPE]
    return g.reshape(lead + (MLA_HEADS * (MLA_NOPE + MLA_ROPE),))


def _pad_lanes(v, n):
    return jnp.concatenate([v, jnp.zeros((n - v.shape[0],), v.dtype)]).reshape(1, n)


def _layer_params(full, rep, l):
    p = {}
    w_in = full['w_in'][l]
    for k, (a, b) in IN_SPLIT.items():
        p['w_' + k] = w_in[:, a:b]
    p['w_in'] = w_in
    for k in ('mla_w_q_b', 'mla_w_kv_b', 'xattn_w_kv', 'ffn_w_in', 'ssd_w_out', 'conv_w_out', 'mla_w_o', 'w_out',
              'xattn_w_q', 'xattn_w_o', 'ffn_w_out', 'ssd_conv_w', 'conv_dw_w'):
        p[k] = full[k][l]
    p['gate_b'] = full['gate_b'][l].reshape(1, -1)
    row = lambda name: rep[name][l].reshape(1, -1)
    for k in ('mix_norm_g', 'ssd_conv_b', 'ssd_norm_g', 'conv_dw_b', 'conv_ln_g', 'conv_ln_b', 'mla_q_a_g',
              'mla_kv_a_g', 'xattn_norm_g', 'mem_norm_g', 'xattn_q_norm_g', 'xattn_k_norm_g', 'ffn_norm_g'):
        p[k] = row(k)
    for k in ('ssd_dt_bias', 'ssd_a_log', 'ssd_d'):
        p[k] = _pad_lanes(rep[k][l], LANE)
        p[k + '_t'] = p[k].reshape(LANE, 1)
    p['gq'] = _pad_lanes(rep['mla_q_norm_g'][l], MLA_HP)
    p['gk'] = _pad_lanes(rep['mla_k_norm_g'][l], MLA_HP)
    return p


def _layer_fwd(x, mem, cs, sn, p, l):
    n = lambda s: f"l{l}_{s}"
    s_len, d = x.shape
    nc = s_len // CHUNK
    sv = {'x': x}
    u = _rms_fwd(x, p['mix_norm_g'], out_dtype=BF16, name=n("mix_norm"))
    z = _matmul(u, p['w_z'], name=n("in_z"))
    xbc = _matmul(u, p['w_xbc'], name=n("in_xbc"))
    dtr = _matmul(u, p['w_dt'], name=n("in_dt"))
    glu = _matmul(u, p['w_glu'], name=n("in_glu"))
    ql = _matmul(u, p['w_ql'], name=n("in_ql"))
    ckv = _matmul(u, p['w_ckv'], name=n("in_ckv"))
    krr = _matmul(u, p['w_kr'], name=n("in_kr"))
    gl = _matmul(u, p['w_gate'], name=n("in_gate"))
    pre_s, act_s = _conv_fwd(xbc, p['ssd_conv_w'], p['ssd_conv_b'], glu=False, act=True, name=n("ssd_conv"))
    dtrt = dtr.reshape(nc, CHUNK, LANE).transpose(0, 2, 1)
    y_scan, prev = _ssd_scan_fwd(act_s, dtr, dtrt, p['ssd_dt_bias'], p['ssd_dt_bias_t'], p['ssd_a_log'],
                                 p['ssd_a_log_t'], p['ssd_d'], name=n("ssd_scan"))
    yn = _ssd_gate_fwd(y_scan, z, p['ssd_norm_g'], name=n("ssd_gate"))
    y_ssd = _matmul(yn, p['ssd_w_out'], name=n("ssd_out"))
    pre_c, = _conv_fwd(glu, p['conv_dw_w'], p['conv_dw_b'], glu=True, act=False, name=n("dw_conv"))
    vc = _ln_silu_fwd(pre_c, p['conv_ln_g'], p['conv_ln_b'], name=n("conv_ln"))
    y_conv = _matmul(vc, p['conv_w_out'], name=n("conv_out"))
    qln = _rms_fwd(ql, p['mla_q_a_g'], out_dtype=BF16, name=n("q_a_norm"))
    q = _matmul(qln, p['mla_w_q_b'], name=n("q_b"))
    ckvn = _rms_fwd(ckv, p['mla_kv_a_g'], out_dtype=BF16, name=n("kv_a_norm"))
    kv = _matmul(ckvn, p['mla_w_kv_b'], name=n("kv_b"))
    qf, kf, v = _mla_prep_fwd(q, kv, krr, cs, sn, p['gq'], p['gk'], name=n("mla_prep"))
    o, lse = _mla_attn_fwd(qf, kf, v, name=n("mla_attn"))
    y_mla = _matmul(o, p['mla_w_o'], name=n("mla_out"))
    merged = _merge_fwd(gl, p['gate_b'], y_ssd, y_conv, y_mla, name=n("merge"))
    x1 = _matmul(merged, p['w_out'], add=x, name=n("mix_out"))
    hx = _rms_fwd(x1, p['xattn_norm_g'], out_dtype=BF16, name=n("xattn_norm"))
    qx = _matmul(hx, p['xattn_w_q'], name=n("xattn_q"))
    memn = _rms_fwd(mem, p['mem_norm_g'], out_dtype=BF16, name=n("mem_norm"))
    kvx = _matmul(memn, p['xattn_w_kv'], name=n("xattn_kv"))
    m_len = mem.shape[0]
    dh = d // X_HEADS
    kraw = kvx[:, :d].reshape(m_len * X_HEADS, dh)
    kx = _rms_fwd(kraw, p['xattn_k_norm_g'], out_dtype=BF16, name=n("xattn_k_norm")).reshape(m_len, d)
    vx = kvx[:, d:].astype(BF16)
    ox = _xattn_fwd(qx, kx, vx, p['xattn_q_norm_g'], name=n("xattn_core"))
    x2 = _matmul(ox, p['xattn_w_o'], add=x1, name=n("xattn_out"))
    hf = _rms_fwd(x2, p['ffn_norm_g'], out_dtype=BF16, name=n("ffn_norm"))
    h1 = _matmul(hf, p['ffn_w_in'], name=n("ffn_in"))
    act = _swiglu_fwd(h1, name=n("swiglu"))
    x3 = _matmul(act, p['ffn_w_out'], add=x2, name=n("ffn_out"))
    sv.update(u=u, z=z, xbc=xbc, dtr=dtr, dtrt=dtrt, glu=glu, ql=ql, ckv=ckv, krr=krr, gl=gl, pre_s=pre_s,
              act_s=act_s, y_scan=y_scan, prev=prev, yn=yn, y_ssd=y_ssd, pre_c=pre_c, vc=vc, y_conv=y_conv,
              qln=qln, q=q, ckvn=ckvn, kv=kv, qf=qf, kf=kf, v=v, o=o, lse=lse, y_mla=y_mla, merged=merged,
              x1=x1, hx=hx, qx=qx, memn=memn, kraw=kraw, kx=kx, vx=vx, ox=ox, x2=x2, hf=hf, h1=h1, act=act)
    return x3, sv


DW_KEY = dict(ffn_out_dw='ffn_w_out', ffn_in_dw='ffn_w_in', xattn_out_dw='xattn_w_o', xattn_q_dw='xattn_w_q',
              xattn_kv_dw='xattn_w_kv', mix_out_dw='w_out', mla_out_dw='mla_w_o', q_b_dw='mla_w_q_b',
              kv_b_dw='mla_w_kv_b', conv_out_dw='conv_w_out', ssd_out_dw='ssd_w_out', in_dw='w_in')


def _layer_bwd(dx3, mem, cs, sn, p, sv, l, stacks, depth):
    n = lambda s: f"l{l}_b_{s}"
    dw = lambda s: dict(name=n(s), into=(stacks.get(DW_KEY[s]), l, depth))
    g = {}
    d = dx3.shape[1]
    dact = _matmul(dx3, p['ffn_w_out'], tb=True, name=n("ffn_out_dx"))
    g['ffn_w_out'] = _matmul(sv['act'], dx3, ta=True, **dw("ffn_out_dw"))
    dh1 = _swiglu_bwd(sv['h1'], dact, name=n("swiglu"))
    g['ffn_w_in'] = _matmul(sv['hf'], dh1, ta=True, **dw("ffn_in_dw"))
    dhf = _matmul(dh1, p['ffn_w_in'], tb=True, name=n("ffn_in_dx"))
    dx2, g['ffn_norm_g'] = _rms_bwd(sv['x2'], p['ffn_norm_g'], dhf, dx_dtype=F32, add=dx3, name=n("ffn_norm"))
    dox = _matmul(dx2, p['xattn_w_o'], tb=True, name=n("xattn_out_dx"))
    g['xattn_w_o'] = _matmul(sv['ox'], dx2, ta=True, **dw("xattn_out_dw"))
    dqx, dkx, dvx, g['xattn_q_norm_g'] = _xattn_bwd(sv['qx'], sv['kx'], sv['vx'], p['xattn_q_norm_g'], dox,
                                                    name=n("xattn_core"))
    g['xattn_w_q'] = _matmul(sv['hx'], dqx, ta=True, **dw("xattn_q_dw"))
    dhx = _matmul(dqx, p['xattn_w_q'], tb=True, name=n("xattn_q_dx"))
    dx1, g['xattn_norm_g'] = _rms_bwd(sv['x1'], p['xattn_norm_g'], dhx, dx_dtype=F32, add=dx2, name=n("xattn_norm"))
    m_len = mem.shape[0]
    dh = d // X_HEADS
    dkraw, g['xattn_k_norm_g'] = _rms_bwd(sv['kraw'], p['xattn_k_norm_g'], dkx.reshape(m_len * X_HEADS, dh),
                                          dx_dtype=BF16, name=n("xattn_k_norm"))
    dkvx = jnp.concatenate([dkraw.reshape(m_len, d), dvx.astype(BF16)], axis=1)
    g['xattn_w_kv'] = _matmul(sv['memn'], dkvx, ta=True, **dw("xattn_kv_dw"))
    dmemn = _matmul(dkvx, p['xattn_w_kv'], tb=True, name=n("xattn_kv_dx"))
    _, g['mem_norm_g'] = _rms_bwd(mem, p['mem_norm_g'], dmemn, dx_dtype=BF16, name=n("mem_norm"))
    dmerged = _matmul(dx1, p['w_out'], tb=True, name=n("mix_out_dx"))
    g['w_out'] = _matmul(sv['merged'], dx1, ta=True, **dw("mix_out_dw"))
    dgl, g['gate_b'], dys, dyc, dym = _merge_bwd(sv['gl'], p['gate_b'], sv['y_ssd'], sv['y_conv'], sv['y_mla'],
                                                 dmerged, name=n("merge"))
    do = _matmul(dym, p['mla_w_o'], tb=True, name=n("mla_out_dx"))
    g['mla_w_o'] = _matmul(sv['o'], dym, ta=True, **dw("mla_out_dw"))
    dqf, dkf, dv = _mla_attn_bwd(sv['qf'], sv['kf'], sv['v'], sv['o'], sv['lse'], do, name=n("mla_attn"))
    dq, dkv, dkrr, g['gq'], g['gk'] = _mla_prep_bwd(sv['q'], sv['kv'], sv['krr'], cs, sn, p['gq'], p['gk'],
                                                    dqf, dkf, dv, name=n("mla_prep"))
    g['mla_w_q_b'] = _matmul(sv['qln'], dq, ta=True, **dw("q_b_dw"))
    dqln = _matmul(dq, p['mla_w_q_b'], tb=True, name=n("q_b_dx"))
    dql, g['mla_q_a_g'] = _rms_bwd(sv['ql'], p['mla_q_a_g'], dqln, dx_dtype=BF16, name=n("q_a_norm"))
    g['mla_w_kv_b'] = _matmul(sv['ckvn'], dkv, ta=True, **dw("kv_b_dw"))
    dckvn = _matmul(dkv, p['mla_w_kv_b'], tb=True, name=n("kv_b_dx"))
    dckv, g['mla_kv_a_g'] = _rms_bwd(sv['ckv'], p['mla_kv_a_g'], dckvn, dx_dtype=BF16, name=n("kv_a_norm"))
    dvc = _matmul(dyc, p['conv_w_out'], tb=True, name=n("conv_out_dx"))
    g['conv_w_out'] = _matmul(sv['vc'], dyc, ta=True, **dw("conv_out_dw"))
    dpre_c, g['conv_ln_g'], g['conv_ln_b'] = _ln_silu_bwd(sv['pre_c'], p['conv_ln_g'], p['conv_ln_b'], dvc,
                                                          name=n("conv_ln"))
    da, dg, g['conv_dw_w'], g['conv_dw_b'] = _conv_bwd(sv['glu'], p['conv_dw_w'], p['conv_dw_b'], dpre_c, None,
                                                       glu=True, act=False, name=n("dw_conv"))
    dyn = _matmul(dys, p['ssd_w_out'], tb=True, name=n("ssd_out_dx"))
    g['ssd_w_out'] = _matmul(sv['yn'], dys, ta=True, **dw("ssd_out_dw"))
    dy_scan, dz, g['ssd_norm_g'] = _ssd_gate_bwd(sv['y_scan'], sv['z'], p['ssd_norm_g'], dyn, name=n("ssd_gate"))
    dxs, db, dc, ddtr, g['ssd_a_log'], g['ssd_dt_bias'], g['ssd_d'] = _ssd_scan_bwd(
        sv['act_s'], sv['dtr'], sv['dtrt'], p['ssd_dt_bias'], p['ssd_dt_bias_t'], p['ssd_a_log'], p['ssd_a_log_t'],
        p['ssd_d'], sv['prev'], dy_scan, name=n("ssd_scan"))
    dact_s = jnp.concatenate([dxs, db, dc], axis=1)
    dxbc, g['ssd_conv_w'], g['ssd_conv_b'] = _conv_bwd(sv['xbc'], p['ssd_conv_w'], p['ssd_conv_b'], dact_s,
                                                       sv['pre_s'], glu=False, act=True, name=n("ssd_conv"))
    tail = jnp.zeros((dz.shape[0], IN_WIDTH_PAD - IN_SPLIT['gate'][1]), BF16)
    dproj = jnp.concatenate([dz, dxbc, ddtr, da, dg, dql, dckv, dkrr, dgl, tail], axis=1)
    g['w_in'] = _matmul(sv['u'], dproj, ta=True, **dw("in_dw"))
    du = _matmul(dproj, p['w_in'], tb=True, name=n("in_dx"))
    dx, g['mix_norm_g'] = _rms_bwd(sv['x'], p['mix_norm_g'], du, dx_dtype=F32, add=dx1, name=n("mix_norm"))
    return dx, g


REP_NAMES = ('mix_norm_g', 'ssd_conv_b', 'ssd_dt_bias', 'ssd_a_log', 'ssd_d', 'ssd_norm_g', 'conv_dw_b', 'conv_ln_g',
             'conv_ln_b', 'mla_q_a_g', 'mla_kv_a_g', 'mla_q_norm_g', 'mla_k_norm_g', 'xattn_norm_g', 'mem_norm_g',
             'xattn_q_norm_g', 'xattn_k_norm_g', 'ffn_norm_g')
BIG = (('w_in', 1, 1024, 8912), ('mla_w_q_b', 1, 384, 1536), ('mla_w_kv_b', 1, 256, 2048),
       ('xattn_w_kv', 1, 1024, 2048), ('ffn_w_in', 1, 1024, 5632), ('ssd_w_out', 0, 1024, 1024),
       ('conv_w_out', 0, 1024, 1024), ('mla_w_o', 0, 1024, 1024), ('w_out', 0, 1024, 1024),
       ('xattn_w_q', 0, 1024, 1024), ('xattn_w_o', 0, 1024, 1024), ('ffn_w_out', 0, 2816, 1024))
SMALL = (('ssd_conv_w', 1, 4, 2048), ('conv_dw_w', 1, 31, 1024), ('gate_b', 1, 3, 1024))


def _rope_tables(positions):
    half = MLA_ROPE // 2
    inv = ROPE_THETA ** (-jnp.arange(0, MLA_ROPE, 2, dtype=F32) / MLA_ROPE)
    ang = positions.astype(F32)[:, None] * inv
    cos, sin = jnp.cos(ang), jnp.sin(ang)
    z = jnp.zeros((positions.shape[0], LANE - 2 * half), F32)
    return jnp.concatenate([cos, cos, z], axis=1), jnp.concatenate([-sin, sin, z], axis=1)


def _local_step(x, mem, positions, target, full, rep):
    depth = rep['mix_norm_g'].shape[0]
    cs, sn = _rope_tables(positions)
    params, saved = [], []
    h = x
    for l in range(depth):
        p = _layer_params(full, rep, l)
        h, sv = _layer_fwd(h, mem, cs, sn, p, l)
        params.append(p)
        saved.append(sv)
    dh, loss = _loss(h, target, name="loss")
    layer_grads = [None] * depth
    stacks = {}
    for l in reversed(range(depth)):
        dh, layer_grads[l] = _layer_bwd(dh, mem, cs, sn, params[l], saved[l], l, stacks, depth)
        stacks = {k: layer_grads[l][k] for k in DW_KEY.values()}
    stack = lambda k: jnp.stack([layer_grads[l][k] for l in range(depth)])
    gfull = {k: stack(k) for k, _, _, _ in SMALL}
    gfull.update(stacks)
    gfull['w_in'] = _w_in_unpad(gfull['w_in'])
    gfull['mla_w_q_b'] = _qb_unpad(gfull['mla_w_q_b'])
    gfull['gate_b'] = gfull['gate_b'].reshape(depth, 3, -1)
    grep = {}
    for k in REP_NAMES:
        if k == 'mla_q_norm_g':
            grep[k] = stack('gq')[:, 0, :MLA_NOPE + MLA_ROPE]
        elif k == 'mla_k_norm_g':
            grep[k] = stack('gk')[:, 0, :MLA_NOPE + MLA_ROPE]
        elif k in ('ssd_dt_bias', 'ssd_a_log', 'ssd_d'):
            grep[k] = stack(k)[:, 0, :SSD_HEADS]
        else:
            grep[k] = stack(k)[:, 0, :]
    return loss, dh, gfull, grep


N_CHIPS = 4
HBM_SPEC = pl.BlockSpec(memory_space=pl.ANY)
ROW, COL, STK, REP = "row", "col", "stk", "rep"


def _kind(axis, cs):
    if axis == 0:
        return ROW
    return COL if cs % LANE == 0 else STK


def _mesh_pos():
    return lax.axis_index("x"), lax.axis_index("y"), lax.axis_index("c")


def _chip_view(ref, kind, j, a, b, layers=None):
    lsel = slice(None) if layers is None else pl.ds(layers[0], layers[1])
    if kind == STK:
        return ref.at[j, lsel]
    if kind == ROW:
        return ref.at[lsel, pl.ds(pl.multiple_of(j * a, 8), a), :]
    if kind == COL:
        return ref.at[lsel, :, pl.ds(pl.multiple_of(j * b, LANE), b)]
    return ref.at[lsel]


def _all_gather(shards, kinds, *, name):
    n = len(shards)
    depth = shards[0].shape[0]
    lh = depth // 2

    def out_shape(w, kind):
        _, a, b = w.shape
        full = {ROW: (depth, N_CHIPS * a, b), COL: (depth, a, N_CHIPS * b), STK: (N_CHIPS, depth, a, b)}[kind]
        return jax.ShapeDtypeStruct(full, w.dtype)

    def body(*refs):
        w_refs, out_refs = refs[:n], refs[n:2 * n]
        send_sems, recv_sems = refs[2 * n:]
        x, y, cc = _mesh_pos()
        me = (x, y, cc)
        sibling = (x, y, 1 - cc)
        chips = [(1 - x, y), (x, 1 - y), (1 - x, 1 - y)]
        slot = lambda chip: 2 * chip[0] + chip[1]

        def part(i, chip, hc):
            _, a, b = w_refs[i].shape
            return _chip_view(out_refs[i], kinds[i], slot(chip), a, b, (hc * lh, lh))

        def own(i):
            _, a, b = w_refs[i].shape
            return _chip_view(out_refs[i], kinds[i], slot((x, y)), a, b)

        def copy(i, k, src, dst, to):
            return pltpu.make_async_remote_copy(src_ref=src, dst_ref=dst, send_sem=send_sems.at[i, k],
                                                recv_sem=recv_sems.at[i, k], device_id=to, device_id_type=MESH)

        sends = []
        for i in range(n):
            my_half = w_refs[i].at[pl.ds(cc * lh, lh)]
            for k, chip in enumerate(chips):
                sends.append(copy(i, k, my_half, part(i, (x, y), cc), (*chip, cc)))
            sends.append(copy(i, 6, w_refs[i], own(i), sibling))
        for cp in sends:
            cp.start()
        for k, chip in enumerate(chips):
            for i in range(n):
                copy(i, k, part(i, chip, cc), part(i, chip, cc), me).wait_recv()
                cp = copy(i, 3 + k, part(i, chip, cc), part(i, chip, cc), sibling)
                cp.start()
                sends.append(cp)
        for i in range(n):
            for k, chip in enumerate(chips):
                copy(i, 3 + k, part(i, chip, 1 - cc), part(i, chip, 1 - cc), me).wait_recv()
            copy(i, 6, own(i), own(i), me).wait_recv()
        for cp in sends:
            cp.wait_send()

    return pl.pallas_call(
        body, name=name, in_specs=[HBM_SPEC] * n, out_specs=[HBM_SPEC] * n,
        out_shape=[out_shape(w, kd) for w, kd in zip(shards, kinds)],
        scratch_shapes=[pltpu.SemaphoreType.DMA((n, 7)), pltpu.SemaphoreType.DMA((n, 7))],
    )(*shards)


def _layers_half(ref, kind, start, lh):
    return ref.at[:, pl.ds(start, lh)] if kind == STK else ref.at[pl.ds(start, lh)]


def _rs_pair(gs, kinds, *, name):
    n = len(gs)

    def half_shape(g, kind):
        s = list(g.shape)
        s[1 if kind == STK else 0] //= 2
        return jax.ShapeDtypeStruct(tuple(s), g.dtype)

    def body(*refs):
        g_refs, buf_refs = refs[:n], refs[n:2 * n]
        send_sems, recv_sems = refs[2 * n:]
        x, y, cc = _mesh_pos()
        cps = []
        for i in range(n):
            lh = buf_refs[i].shape[1 if kinds[i] == STK else 0]
            cp = pltpu.make_async_remote_copy(src_ref=_layers_half(g_refs[i], kinds[i], (1 - cc) * lh, lh),
                                              dst_ref=buf_refs[i], send_sem=send_sems.at[i], recv_sem=recv_sems.at[i],
                                              device_id=(x, y, 1 - cc), device_id_type=MESH)
            cp.start()
            cps.append(cp)
        for cp in cps:
            cp.wait()

    return pl.pallas_call(
        body, name=name, in_specs=[HBM_SPEC] * n, out_specs=[HBM_SPEC] * n,
        out_shape=[half_shape(g, kd) for g, kd in zip(gs, kinds)],
        scratch_shapes=[pltpu.SemaphoreType.DMA((n,)), pltpu.SemaphoreType.DMA((n,))],
    )(*gs)


def _row_tile(rows, cols):
    return _pick(rows, max(8, (512 * 1024 // cols) // 8 * 8), 8)


def _rs_pair_add(g, buf, kind, cc, out_dtype, *, name):
    cols = g.shape[-1]
    pre = g.shape[0] if kind == STK else 1
    rows = buf.size // (pre * cols)
    tr = _row_tile(rows, cols)

    def body(cc_ref, g_ref, b_ref, o_ref):
        o_ref[...] = (g_ref[...] + b_ref[...]).astype(o_ref.dtype)

    out = pl.pallas_call(
        body, name=name,
        grid_spec=pltpu.PrefetchScalarGridSpec(
            num_scalar_prefetch=1, grid=(pre, rows // tr),
            in_specs=[pl.BlockSpec((None, None, tr, cols), lambda s, i, cc_ref: (s, cc_ref[0], i, 0)),
                      pl.BlockSpec((None, tr, cols), lambda s, i, cc_ref: (s, i, 0))],
            out_specs=pl.BlockSpec((None, tr, cols), lambda s, i, cc_ref: (s, i, 0))),
        out_shape=jax.ShapeDtypeStruct((pre, rows, cols), out_dtype),
        compiler_params=_cparams(("parallel", "parallel")),
    )(cc.reshape(1).astype(jnp.int32), g.reshape(pre, 2, rows, cols), buf.reshape(pre, rows, cols))
    return out.reshape(buf.shape)


def _rs_cross(ps, kinds, shard_shapes, *, name):
    n = len(ps)

    def body(*refs):
        p_refs, out_refs = refs[:n], refs[n:2 * n]
        send_sems, recv_sems, local_sems = refs[2 * n:]
        x, y, cc = _mesh_pos()
        chips = [(1 - x, y), (x, 1 - y), (1 - x, 1 - y)]
        slot = lambda chip: 2 * chip[0] + chip[1]
        local, sends = [], []
        for i in range(n):
            a, b = shard_shapes[i]
            if kinds[i] == REP:
                cp = pltpu.make_async_copy(p_refs[i], out_refs[i].at[slot((x, y))], local_sems.at[i])
                cp.start()
                local.append(cp)
            for k, chip in enumerate(chips):
                cp = pltpu.make_async_remote_copy(src_ref=_chip_view(p_refs[i], kinds[i], slot(chip), a, b),
                                                  dst_ref=out_refs[i].at[slot((x, y))], send_sem=send_sems.at[i, k],
                                                  recv_sem=recv_sems.at[i, k], device_id=(*chip, cc),
                                                  device_id_type=MESH)
                cp.start()
                sends.append(cp)
        for i in range(n):
            for k, chip in enumerate(chips):
                landed = out_refs[i].at[slot(chip)]
                pltpu.make_async_remote_copy(src_ref=landed, dst_ref=landed, send_sem=send_sems.at[i, k],
                                             recv_sem=recv_sems.at[i, k], device_id=(*chip, cc),
                                             device_id_type=MESH).wait_recv()
        for cp in sends:
            cp.wait_send()
        for cp in local:
            cp.wait()

    def out_shape(p, kind, ab):
        lh = p.shape[1 if kind == STK else 0]
        return jax.ShapeDtypeStruct((N_CHIPS, lh) + tuple(ab), p.dtype)

    return pl.pallas_call(
        body, name=name, in_specs=[HBM_SPEC] * n, out_specs=[HBM_SPEC] * n,
        out_shape=[out_shape(p, kd, ab) for p, kd, ab in zip(ps, kinds, shard_shapes)],
        scratch_shapes=[pltpu.SemaphoreType.DMA((n, 3)), pltpu.SemaphoreType.DMA((n, 3)),
                        pltpu.SemaphoreType.DMA((n,))],
    )(*ps)


def _rs_sum(p, landed, kind, ab, place, cc, *, name):
    a, b = ab
    lh = landed.shape[1]
    tr = _row_tile(a, b)
    blk = lambda f: pl.BlockSpec((None, None, tr, b), f)
    if kind == ROW:
        p_spec = pl.BlockSpec((None, tr, b), lambda l, i, w, c: (l, w[0] * (a // tr) + i, 0))
    elif kind == COL:
        p_spec = pl.BlockSpec((None, tr, b), lambda l, i, w, c: (l, i, w[0]))
    elif kind == STK:
        p_spec = blk(lambda l, i, w, c: (w[0], l, i, 0))
    else:
        p_spec = blk(lambda l, i, w, c: (0, l, i, 0))
        p = landed
    if kind == REP:
        others = [blk(lambda l, i, w, c, k=k: (k, l, i, 0)) for k in (1, 2, 3)]
    else:
        others = [blk(lambda l, i, w, c: (lax.rem(w[0] + 2, 4), l, i, 0)),
                  blk(lambda l, i, w, c: (w[0] + 1 - 2 * lax.rem(w[0], 2), l, i, 0)),
                  blk(lambda l, i, w, c: (3 - w[0], l, i, 0))]

    def body(w_ref, c_ref, p_ref, b1_ref, b2_ref, b3_ref, o_ref):
        f = lambda r: r[...].astype(F32)
        o_ref[...] = ((f(p_ref) + f(b1_ref)) + f(b2_ref)) + f(b3_ref)

    return pl.pallas_call(
        body, name=name,
        grid_spec=pltpu.PrefetchScalarGridSpec(
            num_scalar_prefetch=2, grid=(lh, a // tr), in_specs=[p_spec] + others,
            out_specs=pl.BlockSpec((None, tr, b), lambda l, i, w, c: (c[0] * lh + l, i, 0))),
        out_shape=jax.ShapeDtypeStruct((2 * lh, a, b), F32),
        compiler_params=_cparams(("parallel", "parallel")),
    )(place, cc, p, landed, landed, landed)


def _rs_share(fs, *, name):
    n = len(fs)

    def body(*refs):
        out_refs = refs[n:2 * n]
        send_sems, recv_sems = refs[2 * n:]
        x, y, cc = _mesh_pos()
        cps = []
        for i in range(n):
            lh = out_refs[i].shape[0] // 2
            mine = out_refs[i].at[pl.ds(cc * lh, lh)]
            cp = pltpu.make_async_remote_copy(src_ref=mine, dst_ref=mine, send_sem=send_sems.at[i],
                                              recv_sem=recv_sems.at[i], device_id=(x, y, 1 - cc),
                                              device_id_type=MESH)
            cp.start()
            cps.append(cp)
        for i in range(n):
            lh = out_refs[i].shape[0] // 2
            theirs = out_refs[i].at[pl.ds((1 - cc) * lh, lh)]
            pltpu.make_async_remote_copy(src_ref=theirs, dst_ref=theirs, send_sem=send_sems.at[i],
                                         recv_sem=recv_sems.at[i], device_id=(x, y, 1 - cc),
                                         device_id_type=MESH).wait_recv()
        for cp in cps:
            cp.wait_send()

    return pl.pallas_call(
        body, name=name, in_specs=[HBM_SPEC] * n, out_specs=[HBM_SPEC] * n,
        out_shape=[jax.ShapeDtypeStruct(f.shape, f.dtype) for f in fs],
        input_output_aliases={i: i for i in range(n)},
        scratch_shapes=[pltpu.SemaphoreType.DMA((n,)), pltpu.SemaphoreType.DMA((n,))],
    )(*fs)


def _reduce_scatter(gs, kinds, shard_shapes, wire_dtypes):
    x, y, cc = _mesh_pos()
    place = (2 * x + y).reshape(1).astype(jnp.int32)
    cc1 = cc.reshape(1).astype(jnp.int32)
    bufs = _rs_pair(gs, kinds, name="rs_pair")
    ps = [_rs_pair_add(g, buf, kd, cc, wd, name=f"rs_pair_add_{i}")
          for i, (g, buf, kd, wd) in enumerate(zip(gs, bufs, kinds, wire_dtypes))]
    landed = _rs_cross(ps, kinds, shard_shapes, name="rs_cross")
    fs = [_rs_sum(p, b, kd, ab, place, cc1, name=f"rs_sum_{i}")
          for i, (p, b, kd, ab) in enumerate(zip(ps, landed, kinds, shard_shapes))]
    return _rs_share(fs, name="rs_share")


def _shard_shape(axis, r, c):
    return (r // N_CHIPS, c) if axis == 0 else (r, c // N_CHIPS)


def _unstack(stacked):
    ns, depth, r, cs = stacked.shape
    return stacked.transpose(1, 2, 0, 3).reshape(depth, r, ns * cs)


def _stack(fullw):
    depth, r, c = fullw.shape
    return fullw.reshape(depth, r, N_CHIPS, c // N_CHIPS).transpose(2, 0, 1, 3)


REP_SIZES = dict(mix_norm_g=1024, ssd_conv_b=2048, ssd_dt_bias=16, ssd_a_log=16, ssd_d=16, ssd_norm_g=1024,
                 conv_dw_b=1024, conv_ln_g=1024, conv_ln_b=1024, mla_q_a_g=384, mla_kv_a_g=256, mla_q_norm_g=192,
                 mla_k_norm_g=192, xattn_norm_g=1024, mem_norm_g=1024, xattn_q_norm_g=256, xattn_k_norm_g=256,
                 ffn_norm_g=1024)
REP_WIDTH = -(-sum(REP_SIZES.values()) // LANE) * LANE


def _pack_rep(d):
    flat = jnp.concatenate([d[k] for k in REP_NAMES], axis=1)
    return jnp.pad(flat, ((0, 0), (0, REP_WIDTH - flat.shape[1])))[:, None, :]


def _unpack_rep(packed):
    out, off = {}, 0
    for k in REP_NAMES:
        out[k] = packed[:, 0, off:off + REP_SIZES[k]]
        off += REP_SIZES[k]
    return out


WEIGHT_NAMES = ('mix_norm_g', 'w_in', 'ssd_conv_w', 'ssd_conv_b', 'ssd_dt_bias', 'ssd_a_log', 'ssd_d', 'ssd_norm_g',
                'ssd_w_out', 'conv_dw_w', 'conv_dw_b', 'conv_ln_g', 'conv_ln_b', 'conv_w_out', 'mla_q_a_g',
                'mla_w_q_b', 'mla_kv_a_g', 'mla_w_kv_b', 'mla_q_norm_g', 'mla_k_norm_g', 'mla_w_o', 'gate_b', 'w_out',
                'xattn_norm_g', 'mem_norm_g', 'xattn_w_q', 'xattn_w_kv', 'xattn_q_norm_g', 'xattn_k_norm_g',
                'xattn_w_o', 'ffn_norm_g', 'ffn_w_in', 'ffn_w_out')


def kernel(x, mem, positions, *rest):
    nw = len(WEIGHT_NAMES)
    weights = dict(zip(WEIGHT_NAMES, rest[:nw]))
    target = rest[nw]
    mom_m = dict(zip(WEIGHT_NAMES, rest[nw + 1:2 * nw + 1]))
    mom_v = dict(zip(WEIGHT_NAMES, rest[2 * nw + 1:3 * nw + 1]))
    depth = weights['mix_norm_g'].shape[0]

    sharded = BIG + SMALL
    kinds = [_kind(axis, _shard_shape(axis, r, c)[1]) for _, axis, r, c in sharded]
    shard_shapes = [_shard_shape(axis, r, c) for _, axis, r, c in sharded]
    shards = [weights[k].astype(BF16) for k, _, _, _ in BIG] + [weights[k] for k, _, _, _ in SMALL]
    gathered = _all_gather(shards, kinds, name="ag_weights")
    full = {k: (_unstack(w) if kd == STK else w) for (k, _, _, _), kd, w in zip(sharded, kinds, gathered)}
    full['w_in'] = _w_in_pad(full['w_in'])
    full['mla_w_q_b'] = _qb_pad(full['mla_w_q_b'])
    rep = {k: weights[k] for k in REP_NAMES}

    loss, dx, gfull, grep = _local_step(x[0], mem[0], positions[0], target[0], full, rep)
    loss = lax.psum(loss, ("x", "y", "c"))

    gs = [(_stack(gfull[k]) if kd == STK else gfull[k]) for (k, _, _, _), kd in zip(sharded, kinds)]
    wire = [BF16] * len(BIG) + [F32] * (len(SMALL) + 1)
    summed = _reduce_scatter(gs + [_pack_rep(grep)], kinds + [REP], shard_shapes + [(1, REP_WIDTH)], wire)
    grads = {k: g for (k, _, _, _), g in zip(sharded, summed[:-1])}
    rep_sum = summed[-1]
    grads.update(_unpack_rep(rep_sum))

    delta, new_m, new_v = {}, {}, {}
    for k, _, _, _ in sharded:
        w = weights[k]
        two_d = (w.shape[0] * w.shape[1], w.shape[2])
        d_, m_, v_ = _adamw(w.reshape(two_d), grads[k].reshape(two_d), mom_m[k].reshape(two_d),
                            mom_v[k].reshape(two_d), name="adamw_" + k)
        delta[k], new_m[k], new_v[k] = d_.reshape(w.shape), m_.reshape(w.shape), v_.reshape(w.shape)
    pack2 = lambda d: _pack_rep(d)[:, 0, :]
    d_, m_, v_ = _adamw(pack2(rep), rep_sum[:, 0, :], pack2({k: mom_m[k] for k in REP_NAMES}),
                        pack2({k: mom_v[k] for k in REP_NAMES}), name="adamw_rep")
    delta.update(_unpack_rep(d_[:, None, :]))
    new_m.update(_unpack_rep(m_[:, None, :]))
    new_v.update(_unpack_rep(v_[:, None, :]))

    return (loss, dx[None], *[grads[k] for k in WEIGHT_NAMES], *[delta[k] for k in WEIGHT_NAMES],
            *[new_m[k] for k in WEIGHT_NAMES], *[new_v[k] for k in WEIGHT_NAMES])
```

```python
import functools
import math

import jax
import jax.numpy as jnp
import numpy as np
from jax import lax
from jax.experimental import pallas as pl
from jax.experimental.pallas import tpu as pltpu

F32 = jnp.float32
BF16 = jnp.bfloat16
MESH = pl.DeviceIdType.MESH

EPS = 1e-6
CHUNK = 64
SSD_HEADS = 16
SSD_GROUPS = 4
SSD_P = 64
SSD_N = 128
MLA_HEADS = 8
MLA_NOPE = 128
MLA_ROPE = 64
MLA_V = 128
MLA_HP = 256
X_HEADS = 4
ROPE_THETA = 10000.0
ADAM_LR, ADAM_B1, ADAM_B2, ADAM_EPS, ADAM_WD, ADAM_STEP = 0.001, 0.9, 0.999, 1e-08, 0.01, 10
LANE = 128
NEG = -1e30
VMEM_MB = 1024 * 1024


def _pick(n, cap, mult=128):
    if n <= cap:
        return n
    d = (cap // mult) * mult
    while d >= mult:
        if n % d == 0:
            return d
        d -= mult
    return n


def _cparams(sem, mb=40):
    return pltpu.CompilerParams(dimension_semantics=sem, vmem_limit_bytes=mb * VMEM_MB)


def _sigmoid(x):
    return 1.0 / (1.0 + jnp.exp(-x))


def _silu(x):
    return x * _sigmoid(x)


def _dsilu(x):
    s = _sigmoid(x)
    return s * (1.0 + x * (1.0 - s))


def _softplus(x):
    return jnp.maximum(x, 0.0) + jnp.log(1.0 + jnp.exp(-jnp.abs(x)))


def _matmul(a, b, *, ta=False, tb=False, out_dtype=F32, add=None, into=None, name):
    if ta:
        kdim, m = a.shape
    else:
        m, kdim = a.shape
    if tb:
        n, kb = b.shape
    else:
        kb, n = b.shape
    assert kb == kdim, (a.shape, b.shape, ta, tb)
    tm = _pick(m, 512, 128) if ta else _pick(m, 1024, 8)
    tn = _pick(n, 1024 if n <= 1024 else 512, 128)
    tk = _pick(kdim, 2048, 128)
    nk = kdim // tk
    dims = (((0 if ta else 1,), (1 if tb else 0,)), ((), ()))

    has_add = add is not None
    has_stack = into is not None and into[0] is not None

    def body(a_ref, b_ref, *rest):
        add_ref = rest[0] if has_add else None
        o_ref = rest[has_add + has_stack]
        acc = rest[has_add + has_stack + 1:]
        part = lax.dot_general(a_ref[...].astype(BF16), b_ref[...].astype(BF16), dims,
                               preferred_element_type=F32)

        def finish(total):
            if has_add:
                total = total + add_ref[...]
            o_ref[...] = total.astype(o_ref.dtype)

        if nk == 1:
            finish(part)
        else:
            acc_ref, = acc
            k = pl.program_id(2)

            @pl.when(k == 0)
            def _():
                acc_ref[...] = part

            @pl.when(k > 0)
            def _():
                acc_ref[...] += part

            @pl.when(k == nk - 1)
            def _():
                finish(acc_ref[...])

    a_spec = pl.BlockSpec((tk, tm), lambda i, j, k: (k, i)) if ta else pl.BlockSpec((tm, tk), lambda i, j, k: (i, k))
    b_spec = pl.BlockSpec((tn, tk), lambda i, j, k: (j, k)) if tb else pl.BlockSpec((tk, tn), lambda i, j, k: (k, j))
    o_spec = pl.BlockSpec((tm, tn), lambda i, j, k: (i, j))
    operands = [a, b] + ([add] if has_add else [])
    in_specs = [a_spec, b_spec] + ([o_spec] if has_add else [])
    if into is None:
        out_spec, out_shape, aliases = o_spec, jax.ShapeDtypeStruct((m, n), out_dtype), {}
    else:
        stack, layer, depth = into
        out_spec = pl.BlockSpec((None, tm, tn), lambda i, j, k: (layer, i, j))
        out_shape = jax.ShapeDtypeStruct((depth, m, n), out_dtype)
        aliases = {}
        if stack is not None:
            aliases = {len(operands): 0}
            operands.append(stack)
            in_specs.append(HBM_SPEC)
    return pl.pallas_call(
        body, name=name, grid=(m // tm, n // tn, nk),
        in_specs=in_specs, out_specs=out_spec, out_shape=out_shape, input_output_aliases=aliases,
        scratch_shapes=[] if nk == 1 else [pltpu.VMEM((tm, tn), F32)],
        compiler_params=_cparams(("parallel", "parallel", "arbitrary"), 48),
    )(*operands)


def _rms_fwd(x, g, *, out_dtype, name):
    r, w = x.shape
    tr = _pick(r, 512, 8)

    def body(x_ref, g_ref, o_ref):
        xv = x_ref[...]
        rstd = lax.rsqrt(jnp.mean(xv * xv, axis=-1, keepdims=True) + EPS)
        o_ref[...] = (xv * rstd * g_ref[...]).astype(o_ref.dtype)

    return pl.pallas_call(
        body, name=name, grid=(r // tr,),
        in_specs=[pl.BlockSpec((tr, w), lambda i: (i, 0)), pl.BlockSpec((1, w), lambda i: (0, 0))],
        out_specs=pl.BlockSpec((tr, w), lambda i: (i, 0)),
        out_shape=jax.ShapeDtypeStruct((r, w), out_dtype),
        compiler_params=_cparams(("parallel",)),
    )(x, g)


def _rms_bwd(x, g, dy, *, dx_dtype, name, add=None):
    r, w = x.shape
    tr = _pick(r, 512, 8)
    has_add = add is not None

    def body(x_ref, g_ref, dy_ref, *rest):
        if has_add:
            add_ref, dx_ref, dg_ref = rest
        else:
            dx_ref, dg_ref = rest
        xv = x_ref[...]
        dyv = dy_ref[...].astype(F32)
        rstd = lax.rsqrt(jnp.mean(xv * xv, axis=-1, keepdims=True) + EPS)
        xh = xv * rstd
        dyg = dyv * g_ref[...]
        dx = rstd * (dyg - xh * jnp.mean(dyg * xh, axis=-1, keepdims=True))
        if has_add:
            dx = dx + add_ref[...]
        dx_ref[...] = dx.astype(dx_ref.dtype)
        part = jnp.sum(dyv * xh, axis=0, keepdims=True)

        @pl.when(pl.program_id(0) == 0)
        def _():
            dg_ref[...] = part

        @pl.when(pl.program_id(0) > 0)
        def _():
            dg_ref[...] += part

    row = pl.BlockSpec((tr, w), lambda i: (i, 0))
    vec = pl.BlockSpec((1, w), lambda i: (0, 0))
    ins = [x, g, dy] + ([add] if has_add else [])
    return pl.pallas_call(
        body, name=name, grid=(r // tr,),
        in_specs=[row, vec, row] + ([row] if has_add else []),
        out_specs=[row, vec],
        out_shape=[jax.ShapeDtypeStruct((r, w), dx_dtype), jax.ShapeDtypeStruct((1, w), F32)],
        compiler_params=_cparams(("arbitrary",)),
    )(*ins)


def _ln_silu_fwd(x, g, b, *, name):
    r, w = x.shape
    tr = _pick(r, 512, 8)

    def body(x_ref, g_ref, b_ref, o_ref):
        xv = x_ref[...]
        mu = jnp.mean(xv, axis=-1, keepdims=True)
        xc = xv - mu
        rstd = lax.rsqrt(jnp.mean(xc * xc, axis=-1, keepdims=True) + EPS)
        o_ref[...] = _silu(xc * rstd * g_ref[...] + b_ref[...]).astype(o_ref.dtype)

    row = pl.BlockSpec((tr, w), lambda i: (i, 0))
    vec = pl.BlockSpec((1, w), lambda i: (0, 0))
    return pl.pallas_call(
        body, name=name, grid=(r // tr,), in_specs=[row, vec, vec], out_specs=row,
        out_shape=jax.ShapeDtypeStruct((r, w), BF16), compiler_params=_cparams(("parallel",)),
    )(x, g, b)


def _ln_silu_bwd(x, g, b, dy, *, name):
    r, w = x.shape
    tr = _pick(r, 512, 8)

    def body(x_ref, g_ref, b_ref, dy_ref, dx_ref, dg_ref, db_ref):
        xv = x_ref[...]
        mu = jnp.mean(xv, axis=-1, keepdims=True)
        xc = xv - mu
        rstd = lax.rsqrt(jnp.mean(xc * xc, axis=-1, keepdims=True) + EPS)
        xh = xc * rstd
        pre = xh * g_ref[...] + b_ref[...]
        dpre = dy_ref[...].astype(F32) * _dsilu(pre)
        dxh = dpre * g_ref[...]
        dx_ref[...] = rstd * (dxh - jnp.mean(dxh, axis=-1, keepdims=True)
                              - xh * jnp.mean(dxh * xh, axis=-1, keepdims=True))
        pg = jnp.sum(dpre * xh, axis=0, keepdims=True)
        pb = jnp.sum(dpre, axis=0, keepdims=True)

        @pl.when(pl.program_id(0) == 0)
        def _():
            dg_ref[...] = pg
            db_ref[...] = pb

        @pl.when(pl.program_id(0) > 0)
        def _():
            dg_ref[...] += pg
            db_ref[...] += pb

    row = pl.BlockSpec((tr, w), lambda i: (i, 0))
    vec = pl.BlockSpec((1, w), lambda i: (0, 0))
    return pl.pallas_call(
        body, name=name, grid=(r // tr,), in_specs=[row, vec, vec, row], out_specs=[row, vec, vec],
        out_shape=[jax.ShapeDtypeStruct((r, w), F32), jax.ShapeDtypeStruct((1, w), F32),
                   jax.ShapeDtypeStruct((1, w), F32)],
        compiler_params=_cparams(("arbitrary",)),
    )(x, g, b, dy)


CONV_PAD = 32
CONV_T = 256


def _conv_fwd(src, w, b, *, glu, act, name):
    s = src.shape[0]
    k, c = w.shape
    tc = LANE
    ncb = c // tc
    tt = _pick(s, CONV_T, 8)
    assert k - 1 <= CONV_PAD

    def body(*refs):
        if glu:
            a_ref, g_ref, w_ref, b_ref = refs[:4]
            outs = refs[4:-1]
        else:
            a_ref, w_ref, b_ref = refs[:3]
            outs = refs[3:-1]
        xp = refs[-1]
        xp[0:CONV_PAD, :] = jnp.zeros((CONV_PAD, tc), F32)
        if glu:
            xp[CONV_PAD:CONV_PAD + s, :] = a_ref[...] * _sigmoid(g_ref[...])
        else:
            xp[CONV_PAD:CONV_PAD + s, :] = a_ref[...]
        wv = w_ref[...]
        bv = b_ref[...]
        for t0 in range(0, s, tt):
            acc = jnp.zeros((tt, tc), F32) + bv
            for kk in range(k):
                off = CONV_PAD + t0 - (k - 1) + kk
                acc = acc + wv[kk:kk + 1, :] * xp[off:off + tt, :]
            outs[0][t0:t0 + tt, :] = acc
            if act:
                outs[1][t0:t0 + tt, :] = _silu(acc)

    col = pl.BlockSpec((s, tc), lambda j: (0, j))
    in_specs = [col, pl.BlockSpec((s, tc), lambda j: (0, j + ncb))] if glu else [col]
    in_specs += [pl.BlockSpec((k, tc), lambda j: (0, j)), pl.BlockSpec((1, tc), lambda j: (0, j))]
    n_out = 2 if act else 1
    res = pl.pallas_call(
        body, name=name, grid=(ncb,), in_specs=in_specs,
        out_specs=[col] * n_out,
        out_shape=[jax.ShapeDtypeStruct((s, c), F32)] * n_out,
        scratch_shapes=[pltpu.VMEM((CONV_PAD + s, tc), F32)],
        compiler_params=_cparams(("parallel",), 48),
    )(*([src, src] if glu else [src]), w, b)
    return res


def _conv_bwd(src, w, b, dy, pre, *, glu, act, name):
    s = src.shape[0]
    k, c = w.shape
    tc = LANE
    ncb = c // tc
    tt = _pick(s, CONV_T, 8)

    def body(*refs):
        i = 0
        a_ref = refs[i]; i += 1
        if glu:
            g_ref = refs[i]; i += 1
        w_ref = refs[i]; i += 1
        dy_ref = refs[i]; i += 1
        if act:
            pre_ref = refs[i]; i += 1
        da_ref = refs[i]; i += 1
        if glu:
            dg_ref = refs[i]; i += 1
        dw_ref = refs[i]; db_ref = refs[i + 1]
        xp, dp = refs[-2], refs[-1]
        xp[0:CONV_PAD, :] = jnp.zeros((CONV_PAD, tc), F32)
        if glu:
            xp[CONV_PAD:CONV_PAD + s, :] = a_ref[...] * _sigmoid(g_ref[...])
        else:
            xp[CONV_PAD:CONV_PAD + s, :] = a_ref[...]
        dp[s:s + CONV_PAD, :] = jnp.zeros((CONV_PAD, tc), F32)
        if act:
            dp[0:s, :] = dy_ref[...].astype(F32) * _dsilu(pre_ref[...])
        else:
            dp[0:s, :] = dy_ref[...].astype(F32)
        wv = w_ref[...]
        dws = [jnp.zeros((1, tc), F32) for _ in range(k)]
        dbs = jnp.zeros((1, tc), F32)
        for t0 in range(0, s, tt):
            acc = jnp.zeros((tt, tc), F32)
            dcur = dp[t0:t0 + tt, :]
            dbs = dbs + jnp.sum(dcur, axis=0, keepdims=True)
            for kk in range(k):
                acc = acc + wv[kk:kk + 1, :] * dp[t0 + (k - 1) - kk:t0 + (k - 1) - kk + tt, :]
                off = CONV_PAD + t0 - (k - 1) + kk
                dws[kk] = dws[kk] + jnp.sum(dcur * xp[off:off + tt, :], axis=0, keepdims=True)
            if glu:
                av = a_ref[t0:t0 + tt, :]
                sg = _sigmoid(g_ref[t0:t0 + tt, :])
                da_ref[t0:t0 + tt, :] = (acc * sg).astype(da_ref.dtype)
                dg_ref[t0:t0 + tt, :] = (acc * av * sg * (1.0 - sg)).astype(dg_ref.dtype)
            else:
                da_ref[t0:t0 + tt, :] = acc.astype(da_ref.dtype)
        for kk in range(k):
            dw_ref[kk:kk + 1, :] = dws[kk]
        db_ref[...] = dbs

    col = pl.BlockSpec((s, tc), lambda j: (0, j))
    in_specs = [col] + ([pl.BlockSpec((s, tc), lambda j: (0, j + ncb))] if glu else [])
    in_specs += [pl.BlockSpec((k, tc), lambda j: (0, j)), col] + ([col] if act else [])
    ins = ([src, src] if glu else [src]) + [w, dy] + ([pre] if act else [])
    out_specs = [col] + ([col] if glu else []) + [pl.BlockSpec((k, tc), lambda j: (0, j)),
                                                  pl.BlockSpec((1, tc), lambda j: (0, j))]
    out_shape = [jax.ShapeDtypeStruct((s, c), BF16)] * (2 if glu else 1) + [
        jax.ShapeDtypeStruct((k, c), F32), jax.ShapeDtypeStruct((1, c), F32)]
    return pl.pallas_call(
        body, name=name, grid=(ncb,), in_specs=in_specs, out_specs=out_specs, out_shape=out_shape,
        scratch_shapes=[pltpu.VMEM((CONV_PAD + s, tc), F32), pltpu.VMEM((s + CONV_PAD, tc), F32)],
        compiler_params=_cparams(("parallel",), 56),
    )(*ins)


def _ssd_consts():
    e = np.zeros((LANE, SSD_HEADS * SSD_P), np.float32)
    for h in range(SSD_HEADS):
        e[h, h * SSD_P:(h + 1) * SSD_P] = 1.0
    ltri = np.tril(np.ones((CHUNK, CHUNK), np.float32))
    return jnp.asarray(e), jnp.asarray(e.T.copy()), jnp.asarray(ltri), jnp.asarray(ltri.T.copy())


def _split3(x):
    hi = x.astype(BF16)
    r = x - hi.astype(F32)
    mid = r.astype(BF16)
    lo = (r - mid.astype(F32)).astype(BF16)
    return hi, mid, lo


def _dot_sel(x, sel, x_left=True):
    sb = sel.astype(BF16)
    out = None
    for part in _split3(x):
        t = jnp.dot(part, sb, preferred_element_type=F32) if x_left else jnp.dot(sb, part, preferred_element_type=F32)
        out = t if out is None else out + t
    return out


def _ssd_chunk_terms(dtr_ref, dtrt_ref, bias_ref, biast_ref, alog_ref, alogt_ref, e_ref, ltri_ref, utri_ref):
    a_neg = -jnp.exp(alog_ref[...])
    dt = _softplus(dtr_ref[...] + bias_ref[...])
    a = dt * a_neg
    s = _dot_sel(a, ltri_ref[...], x_left=False)
    dtt = _softplus(dtrt_ref[...] + biast_ref[...])
    st = _dot_sel(dtt * (-jnp.exp(alogt_ref[...])), utri_ref[...])
    ev = e_ref[...]
    s_x = _dot_sel(s, ev)
    dt_x = _dot_sel(dt, ev)
    return a_neg, dt, s, st, s_x, dt_x


def _ssd_decay(s, st, h, tril):
    seg = s[:, h:h + 1] - st[h:h + 1, :]
    return jnp.exp(jnp.where(tril, seg, NEG))


def _ssd_decay_t(s, st, h, triu):
    seg = st[h:h + 1, :] - s[:, h:h + 1]
    return jnp.exp(jnp.where(triu, seg, NEG))


def _head_masks():
    lane = lax.broadcasted_iota(jnp.int32, (1, SSD_P * 4), 1)
    return [((lane >= r * SSD_P) & (lane < (r + 1) * SSD_P)).astype(F32) for r in range(4)]


def _ssd_scan_fwd(xbc, dtr, dtrt, bias, biast, alog, alogt, dskip, *, name):
    s_len = xbc.shape[0]
    nc = s_len // CHUNK
    e, et, ltri, utri = _ssd_consts()
    gw = SSD_P * 4

    def body(xs_ref, b_ref, c_ref, dtr_ref, dtrt_ref, bias_ref, biast_ref, alog_ref, alogt_ref, d_ref,
             e_ref, ltri_ref, utri_ref, y_ref, prev_ref, state):
        @pl.when(pl.program_id(0) == 0)
        def _():
            state[...] = jnp.zeros_like(state)

        a_neg, dt, s, st, s_x, dt_x = _ssd_chunk_terms(dtr_ref, dtrt_ref, bias_ref, biast_ref, alog_ref,
                                                       alogt_ref, e_ref, ltri_ref, utri_ref)
        s_last = s_x[CHUNK - 1:CHUNK, :]
        es_x = jnp.exp(s_x)
        w_x = jnp.exp(s_last - s_x)
        cd_x = jnp.exp(s_last)
        d_x = _dot_sel(jnp.broadcast_to(d_ref[...], (8, LANE)), e_ref[...])[0:1, :]
        xs = xs_ref[...]
        xv = xs * dt_x
        row = lax.broadcasted_iota(jnp.int32, (CHUNK, CHUNK), 0)
        colm = lax.broadcasted_iota(jnp.int32, (CHUNK, CHUNK), 1)
        tril = colm <= row
        masks = _head_masks()
        for g in range(SSD_GROUPS):
            gs = slice(g * gw, (g + 1) * gw)
            bg = b_ref[:, g * SSD_N:(g + 1) * SSD_N].astype(BF16)
            cg = c_ref[:, g * SSD_N:(g + 1) * SSD_N].astype(BF16)
            xg = xv[:, gs]
            hg = state[g]
            prev_ref[g] = hg
            cb = lax.dot_general(cg, bg, (((1,), (1,)), ((), ())), preferred_element_type=F32)
            yg = jnp.dot(cg, hg.astype(BF16), preferred_element_type=F32) * es_x[:, gs]
            for r in range(4):
                m = (cb * _ssd_decay(s, st, g * 4 + r, tril)).astype(BF16)
                yg = yg + jnp.dot(m, (xg * masks[r]).astype(BF16), preferred_element_type=F32)
            y_ref[:, gs] = yg + d_x[:, gs] * xs[:, gs]
            upd = lax.dot_general(bg, (xg * w_x[:, gs]).astype(BF16), (((0,), (0,)), ((), ())),
                                  preferred_element_type=F32)
            state[g] = hg * cd_x[:, gs] + upd

    nh = LANE
    chunk_row = lambda w, cb: pl.BlockSpec((CHUNK, w), lambda i, cb=cb: (i, cb))
    full = lambda a: pl.BlockSpec(a.shape, lambda i: (0,) * a.ndim)
    in_specs = [chunk_row(1024, 0), chunk_row(512, 2), chunk_row(512, 3), chunk_row(nh, 0),
                pl.BlockSpec((None, nh, CHUNK), lambda i: (i, 0, 0)),
                full(bias), full(biast), full(alog), full(alogt), full(dskip), full(e), full(ltri), full(utri)]
    return pl.pallas_call(
        body, name=name, grid=(nc,), in_specs=in_specs,
        out_specs=[pl.BlockSpec((CHUNK, 1024), lambda i: (i, 0)),
                   pl.BlockSpec((None, SSD_GROUPS, SSD_N, gw), lambda i: (i, 0, 0, 0))],
        out_shape=[jax.ShapeDtypeStruct((s_len, 1024), F32),
                   jax.ShapeDtypeStruct((nc, SSD_GROUPS, SSD_N, gw), F32)],
        scratch_shapes=[pltpu.VMEM((SSD_GROUPS, SSD_N, gw), F32)],
        compiler_params=_cparams(("arbitrary",)),
    )(xbc, xbc, xbc, dtr, dtrt, bias, biast, alog, alogt, dskip, e, ltri, utri)


def _ssd_scan_bwd(xbc, dtr, dtrt, bias, biast, alog, alogt, dskip, prev, dy, *, name):
    s_len = xbc.shape[0]
    nc = s_len // CHUNK
    e, et, ltri, utri = _ssd_consts()
    gw = SSD_P * 4

    def body(xs_ref, b_ref, c_ref, dtr_ref, dtrt_ref, bias_ref, biast_ref, alog_ref, alogt_ref, d_ref,
             e_ref, et_ref, ltri_ref, utri_ref, prev_ref, dy_ref,
             dxs_ref, db_ref, dc_ref, ddtr_ref, dalog_ref, dbias_ref, dd_ref, dstate):
        step = pl.program_id(0)

        @pl.when(step == 0)
        def _():
            dstate[...] = jnp.zeros_like(dstate)

        a_neg, dt, s, st, s_x, dt_x = _ssd_chunk_terms(dtr_ref, dtrt_ref, bias_ref, biast_ref, alog_ref,
                                                       alogt_ref, e_ref, ltri_ref, utri_ref)
        s_last = s_x[CHUNK - 1:CHUNK, :]
        es_x = jnp.exp(s_x)
        w_x = jnp.exp(s_last - s_x)
        cd_x = jnp.exp(s_last)
        d_x = _dot_sel(jnp.broadcast_to(d_ref[...], (8, LANE)), e_ref[...])[0:1, :]
        xs = xs_ref[...]
        xv = xs * dt_x
        dyv = dy_ref[...]
        row = lax.broadcasted_iota(jnp.int32, (CHUNK, CHUNK), 0)
        colm = lax.broadcasted_iota(jnp.int32, (CHUNK, CHUNK), 1)
        tril = colm <= row
        masks = _head_masks()
        is_last = lax.broadcasted_iota(jnp.int32, (CHUNK, 1), 0) == CHUNK - 1
        nt = (((1,), (1,)), ((), ()))
        tn = (((0,), (0,)), ((), ()))
        ds_parts, ddt_parts = [], []
        head_lane = lax.broadcasted_iota(jnp.int32, (CHUNK, LANE), 1)
        triu = colm >= row
        ds_diag = jnp.zeros((CHUNK, LANE), F32)
        for g in range(SSD_GROUPS):
            gs = slice(g * gw, (g + 1) * gw)
            bg = b_ref[:, g * SSD_N:(g + 1) * SSD_N].astype(BF16)
            cg = c_ref[:, g * SSD_N:(g + 1) * SSD_N].astype(BF16)
            xg = xv[:, gs]
            xgb = xg.astype(BF16)
            hg = prev_ref[g]
            hgb = hg.astype(BF16)
            dsg = dstate[g]
            dsgb = dsg.astype(BF16)
            dyg = dyv[:, gs]
            dye = (dyg * es_x[:, gs]).astype(BF16)
            xw = (xg * w_x[:, gs]).astype(BF16)
            cb = lax.dot_general(cg, bg, nt, preferred_element_type=F32)
            cbt = lax.dot_general(bg, cg, nt, preferred_element_type=F32)
            dcg = lax.dot_general(dye, hgb, nt, preferred_element_type=F32)
            dh = lax.dot_general(cg, dye, tn, preferred_element_type=F32)
            bds = jnp.dot(bg, dsgb, preferred_element_type=F32)
            yoff = es_x[:, gs] * jnp.dot(cg, hgb, preferred_element_type=F32)
            dx_state = w_x[:, gs] * bds
            dbg = lax.dot_general(xw, dsgb, nt, preferred_element_type=F32)
            dxd = jnp.zeros((CHUNK, gw), F32)
            dcb = jnp.zeros((CHUNK, CHUNK), F32)
            for r in range(4):
                dec = _ssd_decay(s, st, g * 4 + r, tril)
                mf = cb * dec
                m = mf.astype(BF16)
                dym = (dyg * masks[r]).astype(BF16)
                dm = lax.dot_general(dym, xgb, nt, preferred_element_type=F32)
                dxd = dxd + lax.dot_general(m, dym, tn, preferred_element_type=F32)
                dcb = dcb + dm * dec
                dmt = lax.dot_general(xgb, dym, nt, preferred_element_type=F32)
                rc = (jnp.sum(dm * mf, axis=1, keepdims=True)
                      - jnp.sum(dmt * cbt * _ssd_decay_t(s, st, g * 4 + r, triu), axis=1, keepdims=True))
                ds_diag = ds_diag + jnp.where(head_lane == g * 4 + r, rc, 0.0)
            dcbb = dcb.astype(BF16)
            dcg = dcg + jnp.dot(dcbb, bg, preferred_element_type=F32)
            dbg = dbg + lax.dot_general(dcbb, cg, tn, preferred_element_type=F32)
            dxg = dxd + dx_state
            extra = (jnp.sum(xg * dx_state, axis=0, keepdims=True)
                     + cd_x[:, gs] * jnp.sum(dsg * hg, axis=0, keepdims=True))
            ds_parts.append(dyg * yoff - xg * dx_state + jnp.where(is_last, extra, 0.0))
            ddt_parts.append(dxg * xs[:, gs])
            dxs_ref[:, gs] = dxg * dt_x[:, gs] + d_x[:, gs] * dyg
            db_ref[:, g * SSD_N:(g + 1) * SSD_N] = dbg
            dc_ref[:, g * SSD_N:(g + 1) * SSD_N] = dcg
            dstate[g] = cd_x[:, gs] * dsg + dh
        etv = et_ref[...]
        ds = ds_diag + _dot_sel(jnp.concatenate(ds_parts, axis=1), etv)
        da = _dot_sel(ds, utri_ref[...], x_left=False)
        ddt = da * a_neg + _dot_sel(jnp.concatenate(ddt_parts, axis=1), etv)
        ddtr = ddt * _sigmoid(dtr_ref[...] + bias_ref[...])
        ddtr_ref[...] = ddtr.astype(ddtr_ref.dtype)
        p_alog = jnp.sum(da * dt, axis=0, keepdims=True) * a_neg
        p_bias = jnp.sum(ddtr, axis=0, keepdims=True)
        p_d = _dot_sel(jnp.broadcast_to(jnp.sum(dyv * xs, axis=0, keepdims=True), (8, SSD_HEADS * SSD_P)),
                       etv)[0:1, :]

        @pl.when(step == 0)
        def _():
            dalog_ref[...] = p_alog
            dbias_ref[...] = p_bias
            dd_ref[...] = p_d

        @pl.when(step > 0)
        def _():
            dalog_ref[...] += p_alog
            dbias_ref[...] += p_bias
            dd_ref[...] += p_d

    nh = LANE
    rev = lambda i: nc - 1 - i
    chunk_row = lambda w, cb: pl.BlockSpec((CHUNK, w), lambda i, cb=cb: (rev(i), cb))
    full = lambda a: pl.BlockSpec(a.shape, lambda i: (0,) * a.ndim)
    vec = pl.BlockSpec((1, nh), lambda i: (0, 0))
    in_specs = [chunk_row(1024, 0), chunk_row(512, 2), chunk_row(512, 3), chunk_row(nh, 0),
                pl.BlockSpec((None, nh, CHUNK), lambda i: (rev(i), 0, 0)),
                full(bias), full(biast), full(alog), full(alogt), full(dskip), full(e), full(et), full(ltri),
                full(utri),
                pl.BlockSpec((None, SSD_GROUPS, SSD_N, gw), lambda i: (rev(i), 0, 0, 0)), chunk_row(1024, 0)]
    return pl.pallas_call(
        body, name=name, grid=(nc,), in_specs=in_specs,
        out_specs=[chunk_row(1024, 0), chunk_row(512, 0), chunk_row(512, 0), chunk_row(nh, 0), vec, vec, vec],
        out_shape=[jax.ShapeDtypeStruct((s_len, 1024), F32), jax.ShapeDtypeStruct((s_len, 512), F32),
                   jax.ShapeDtypeStruct((s_len, 512), F32), jax.ShapeDtypeStruct((s_len, nh), BF16),
                   jax.ShapeDtypeStruct((1, nh), F32), jax.ShapeDtypeStruct((1, nh), F32),
                   jax.ShapeDtypeStruct((1, nh), F32)],
        scratch_shapes=[pltpu.VMEM((SSD_GROUPS, SSD_N, gw), F32)],
        compiler_params=_cparams(("arbitrary",)),
    )(xbc, xbc, xbc, dtr, dtrt, bias, biast, alog, alogt, dskip, e, et, ltri, utri, prev, dy)


def _ssd_gate_fwd(y, z, g, *, name):
    r, w = y.shape
    tr = _pick(r, 512, 8)
    gw = w // SSD_GROUPS

    def body(y_ref, z_ref, g_ref, o_ref):
        for k in range(SSD_GROUPS):
            cs = slice(k * gw, (k + 1) * gw)
            t = y_ref[:, cs] * _silu(z_ref[:, cs])
            rstd = lax.rsqrt(jnp.mean(t * t, axis=-1, keepdims=True) + EPS)
            o_ref[:, cs] = (t * rstd * g_ref[:, cs]).astype(o_ref.dtype)

    row = pl.BlockSpec((tr, w), lambda i: (i, 0))
    vec = pl.BlockSpec((1, w), lambda i: (0, 0))
    return pl.pallas_call(
        body, name=name, grid=(r // tr,), in_specs=[row, row, vec], out_specs=row,
        out_shape=jax.ShapeDtypeStruct((r, w), BF16), compiler_params=_cparams(("parallel",)),
    )(y, z, g)


def _ssd_gate_bwd(y, z, g, do, *, name):
    r, w = y.shape
    tr = _pick(r, 512, 8)
    gw = w // SSD_GROUPS

    def body(y_ref, z_ref, g_ref, do_ref, dy_ref, dz_ref, dg_ref):
        parts = []
        for k in range(SSD_GROUPS):
            cs = slice(k * gw, (k + 1) * gw)
            yv = y_ref[:, cs]
            zv = z_ref[:, cs]
            sz = _silu(zv)
            t = yv * sz
            rstd = lax.rsqrt(jnp.mean(t * t, axis=-1, keepdims=True) + EPS)
            th = t * rstd
            dov = do_ref[:, cs].astype(F32)
            dog = dov * g_ref[:, cs]
            dt = rstd * (dog - th * jnp.mean(dog * th, axis=-1, keepdims=True))
            dy_ref[:, cs] = dt * sz
            dz_ref[:, cs] = (dt * yv * _dsilu(zv)).astype(dz_ref.dtype)
            parts.append(jnp.sum(dov * th, axis=0, keepdims=True))
        pg = jnp.concatenate(parts, axis=1)

        @pl.when(pl.program_id(0) == 0)
        def _():
            dg_ref[...] = pg

        @pl.when(pl.program_id(0) > 0)
        def _():
            dg_ref[...] += pg

    row = pl.BlockSpec((tr, w), lambda i: (i, 0))
    vec = pl.BlockSpec((1, w), lambda i: (0, 0))
    return pl.pallas_call(
        body, name=name, grid=(r // tr,), in_specs=[row, row, vec, row], out_specs=[row, row, vec],
        out_shape=[jax.ShapeDtypeStruct((r, w), F32), jax.ShapeDtypeStruct((r, w), BF16),
                   jax.ShapeDtypeStruct((1, w), F32)],
        compiler_params=_cparams(("arbitrary",)),
    )(y, z, g, do)


def _rope_swap(x):
    lane = lax.broadcasted_iota(jnp.int32, x.shape, 1)
    lo = pltpu.roll(x, 96, 1)
    hi = pltpu.roll(x, 32, 1)
    return jnp.where(lane < 32, lo, jnp.where(lane < 64, hi, 0.0))


def _norm_part(v, g, n):
    rstd = lax.rsqrt(jnp.sum(v * v, axis=-1, keepdims=True) * (1.0 / n) + EPS)
    xh = v * rstd
    return xh * g, xh, rstd


def _norm_part_bwd(dout, g, xh, rstd, n):
    dg = dout * g
    return rstd * (dg - xh * (jnp.sum(dg * xh, axis=-1, keepdims=True) * (1.0 / n)))


def _mla_prep_fwd(q, kv, krr, cs, sn, gq, gk, *, name):
    s = q.shape[0]
    tr = _pick(s, 256, 8)
    hp = MLA_HP

    def body(q_ref, kv_ref, krr_ref, cs_ref, sn_ref, gq_ref, gk_ref, qf_ref, kf_ref, v_ref):
        csv, snv = cs_ref[...], sn_ref[...]
        gqn, gqr = gq_ref[:, 0:128], gq_ref[:, 128:256]
        gkn, gkr = gk_ref[:, 0:128], gk_ref[:, 128:256]
        kr, _, _ = _norm_part(krr_ref[...], gkr, MLA_ROPE)
        kr = (kr * csv + _rope_swap(kr) * snv).astype(BF16)
        for h in range(MLA_HEADS):
            qn, _, _ = _norm_part(q_ref[:, h * hp:h * hp + 128], gqn, MLA_NOPE)
            qr, _, _ = _norm_part(q_ref[:, h * hp + 128:(h + 1) * hp], gqr, MLA_ROPE)
            qr = qr * csv + _rope_swap(qr) * snv
            qf_ref[h, :, 0:128] = qn.astype(BF16)
            qf_ref[h, :, 128:256] = qr.astype(BF16)
            kn, _, _ = _norm_part(kv_ref[:, h * hp:h * hp + 128], gkn, MLA_NOPE)
            kf_ref[h, :, 0:128] = kn.astype(BF16)
            kf_ref[h, :, 128:256] = kr
            v_ref[h] = kv_ref[:, h * hp + 128:(h + 1) * hp].astype(BF16)

    row = lambda w: pl.BlockSpec((tr, w), lambda i: (i, 0))
    vec = pl.BlockSpec((1, hp), lambda i: (0, 0))
    hrow = lambda w: pl.BlockSpec((MLA_HEADS, tr, w), lambda i: (0, i, 0))
    return pl.pallas_call(
        body, name=name, grid=(s // tr,),
        in_specs=[row(MLA_HEADS * hp), row(MLA_HEADS * hp), row(128), row(128), row(128), vec, vec],
        out_specs=[hrow(hp), hrow(hp), hrow(128)],
        out_shape=[jax.ShapeDtypeStruct((MLA_HEADS, s, hp), BF16), jax.ShapeDtypeStruct((MLA_HEADS, s, hp), BF16),
                   jax.ShapeDtypeStruct((MLA_HEADS, s, 128), BF16)],
        compiler_params=_cparams(("parallel",)),
    )(q, kv, krr, cs, sn, gq, gk)


def _mla_prep_bwd(q, kv, krr, cs, sn, gq, gk, dqf, dkf, dv, *, name):
    s = q.shape[0]
    tr = _pick(s, 256, 8)
    hp = MLA_HP

    def body(q_ref, kv_ref, krr_ref, cs_ref, sn_ref, gq_ref, gk_ref, dqf_ref, dkf_ref, dv_ref,
             dq_ref, dkv_ref, dkrr_ref, dgq_ref, dgk_ref):
        csv, snv = cs_ref[...], sn_ref[...]
        gqn, gqr = gq_ref[:, 0:128], gq_ref[:, 128:256]
        gkn, gkr = gk_ref[:, 0:128], gk_ref[:, 128:256]
        _, krh, krs = _norm_part(krr_ref[...], gkr, MLA_ROPE)
        dkr_sum = jnp.zeros((tr, 128), F32)
        pgqn = jnp.zeros((1, 128), F32)
        pgqr = jnp.zeros((1, 128), F32)
        pgkn = jnp.zeros((1, 128), F32)
        for h in range(MLA_HEADS):
            _, qnh, qns = _norm_part(q_ref[:, h * hp:h * hp + 128], gqn, MLA_NOPE)
            _, qrh, qrs = _norm_part(q_ref[:, h * hp + 128:(h + 1) * hp], gqr, MLA_ROPE)
            dqn = dqf_ref[h, :, 0:128]
            drr = dqf_ref[h, :, 128:256]
            dqr = drr * csv + _rope_swap(drr * snv)
            dq_ref[:, h * hp:h * hp + 128] = _norm_part_bwd(dqn, gqn, qnh, qns, MLA_NOPE).astype(dq_ref.dtype)
            dq_ref[:, h * hp + 128:(h + 1) * hp] = _norm_part_bwd(dqr, gqr, qrh, qrs, MLA_ROPE).astype(dq_ref.dtype)
            pgqn = pgqn + jnp.sum(dqn * qnh, axis=0, keepdims=True)
            pgqr = pgqr + jnp.sum(dqr * qrh, axis=0, keepdims=True)
            _, knh, kns = _norm_part(kv_ref[:, h * hp:h * hp + 128], gkn, MLA_NOPE)
            dkn = dkf_ref[h, :, 0:128]
            dkv_ref[:, h * hp:h * hp + 128] = _norm_part_bwd(dkn, gkn, knh, kns, MLA_NOPE).astype(dkv_ref.dtype)
            dkv_ref[:, h * hp + 128:(h + 1) * hp] = dv_ref[h].astype(dkv_ref.dtype)
            pgkn = pgkn + jnp.sum(dkn * knh, axis=0, keepdims=True)
            dkr_sum = dkr_sum + dkf_ref[h, :, 128:256]
        dkr = dkr_sum * csv + _rope_swap(dkr_sum * snv)
        dkrr_ref[...] = _norm_part_bwd(dkr, gkr, krh, krs, MLA_ROPE).astype(dkrr_ref.dtype)
        pgkr = jnp.sum(dkr * krh, axis=0, keepdims=True)
        pq = jnp.concatenate([pgqn, pgqr], axis=1)
        pk = jnp.concatenate([pgkn, pgkr], axis=1)

        @pl.when(pl.program_id(0) == 0)
        def _():
            dgq_ref[...] = pq
            dgk_ref[...] = pk

        @pl.when(pl.program_id(0) > 0)
        def _():
            dgq_ref[...] += pq
            dgk_ref[...] += pk

    row = lambda w: pl.BlockSpec((tr, w), lambda i: (i, 0))
    vec = pl.BlockSpec((1, hp), lambda i: (0, 0))
    hrow = lambda w: pl.BlockSpec((MLA_HEADS, tr, w), lambda i: (0, i, 0))
    return pl.pallas_call(
        body, name=name, grid=(s // tr,),
        in_specs=[row(MLA_HEADS * hp), row(MLA_HEADS * hp), row(128), row(128), row(128), vec, vec,
                  hrow(hp), hrow(hp), hrow(128)],
        out_specs=[row(MLA_HEADS * hp), row(MLA_HEADS * hp), row(128), vec, vec],
        out_shape=[jax.ShapeDtypeStruct((s, MLA_HEADS * hp), BF16), jax.ShapeDtypeStruct((s, MLA_HEADS * hp), BF16),
                   jax.ShapeDtypeStruct((s, 128), BF16), jax.ShapeDtypeStruct((1, hp), F32),
                   jax.ShapeDtypeStruct((1, hp), F32)],
        compiler_params=_cparams(("arbitrary",), 48),
    )(q, kv, krr, cs, sn, gq, gk, dqf, dkf, dv)


ATT_T = 512


def _chunk_mask(t):
    r = lax.shift_right_logical(lax.broadcasted_iota(jnp.int32, (t, t), 0), 6)
    c = lax.shift_right_logical(lax.broadcasted_iota(jnp.int32, (t, t), 1), 6)
    return c <= r


def _mla_attn_fwd(qf, kf, v, *, name):
    nh, s, hp = qf.shape
    t = _pick(s, ATT_T, CHUNK)
    scale = (MLA_NOPE + MLA_ROPE) ** -0.5
    nt = (((1,), (1,)), ((), ()))

    def body(q_ref, k_ref, v_ref, o_ref, lse_ref):
        i = pl.program_id(1)
        q = q_ref[...]

        def block(j, carry, masked):
            m, l, acc = carry
            start = pl.multiple_of(j * t, t)
            k = k_ref[pl.ds(start, t), :]
            sc = lax.dot_general(q, k, nt, preferred_element_type=F32) * scale
            if masked:
                sc = jnp.where(_chunk_mask(t), sc, NEG)
            m_new = jnp.maximum(m, jnp.max(sc, axis=-1, keepdims=True))
            alpha = jnp.exp(m - m_new)
            p = jnp.exp(sc - m_new)
            l = alpha * l + jnp.sum(p, axis=-1, keepdims=True)
            acc = alpha * acc + jnp.dot(p.astype(BF16), v_ref[pl.ds(start, t), :], preferred_element_type=F32)
            return m_new, l, acc

        init = (jnp.full((t, 1), NEG, F32), jnp.zeros((t, 1), F32), jnp.zeros((t, MLA_V), F32))
        carry = lax.fori_loop(0, i, lambda j, c: block(j, c, False), init)
        m, l, acc = block(i, carry, True)
        o_ref[...] = acc / l
        lse_ref[...] = m + jnp.log(l)

    return pl.pallas_call(
        body, name=name, grid=(nh, s // t),
        in_specs=[pl.BlockSpec((None, t, hp), lambda h, i: (h, i, 0)),
                  pl.BlockSpec((None, s, hp), lambda h, i: (h, 0, 0)),
                  pl.BlockSpec((None, s, MLA_V), lambda h, i: (h, 0, 0))],
        out_specs=[pl.BlockSpec((t, MLA_V), lambda h, i: (i, h)),
                   pl.BlockSpec((None, t, 1), lambda h, i: (h, i, 0))],
        out_shape=[jax.ShapeDtypeStruct((s, nh * MLA_V), F32), jax.ShapeDtypeStruct((nh, s, 1), F32)],
        compiler_params=_cparams(("parallel", "arbitrary"), 48),
    )(qf, kf, v)


def _mla_attn_bwd(qf, kf, v, o, lse, do, *, name):
    nh, s, hp = qf.shape
    t = _pick(s, ATT_T, CHUNK)
    nb = s // t
    scale = (MLA_NOPE + MLA_ROPE) ** -0.5
    nt = (((1,), (1,)), ((), ()))
    tn = (((0,), (0,)), ((), ()))

    def body(q_ref, k_ref, v_ref, o_ref, lse_ref, do_ref, dq_ref, dk_ref, dv_ref, delta):
        j = pl.program_id(1)

        @pl.when(j == 0)
        def _():
            dq_ref[...] = jnp.zeros_like(dq_ref)
            delta[...] = jnp.sum(do_ref[...] * o_ref[...], axis=-1, keepdims=True)

        k = k_ref[...]
        vv = v_ref[...]

        def block(i, carry, masked):
            dk, dv = carry
            start = pl.multiple_of(i * t, t)
            q = q_ref[pl.ds(start, t), :]
            dob = do_ref[pl.ds(start, t), :].astype(BF16)
            sc = lax.dot_general(q, k, nt, preferred_element_type=F32) * scale
            if masked:
                sc = jnp.where(_chunk_mask(t), sc, NEG)
            p = jnp.exp(sc - lse_ref[pl.ds(start, t), :])
            dp = lax.dot_general(dob, vv, nt, preferred_element_type=F32)
            ds = (p * (dp - delta[pl.ds(start, t), :]) * scale).astype(BF16)
            dv = dv + lax.dot_general(p.astype(BF16), dob, tn, preferred_element_type=F32)
            dk = dk + lax.dot_general(ds, q, tn, preferred_element_type=F32)
            dq_ref[pl.ds(start, t), :] += jnp.dot(ds, k, preferred_element_type=F32)
            return dk, dv

        init = (jnp.zeros((t, hp), F32), jnp.zeros((t, MLA_V), F32))
        carry = block(j, init, True)
        dk, dv = lax.fori_loop(j + 1, nb, lambda i, c: block(i, c, False), carry)
        dk_ref[...] = dk
        dv_ref[...] = dv

    whole = lambda w: pl.BlockSpec((None, s, w), lambda h, j: (h, 0, 0))
    blk = lambda w: pl.BlockSpec((None, t, w), lambda h, j: (h, j, 0))
    colh = pl.BlockSpec((s, MLA_V), lambda h, j: (0, h))
    return pl.pallas_call(
        body, name=name, grid=(nh, nb),
        in_specs=[whole(hp), blk(hp), blk(MLA_V), colh, whole(1), colh],
        out_specs=[whole(hp), blk(hp), blk(MLA_V)],
        out_shape=[jax.ShapeDtypeStruct((nh, s, hp), F32), jax.ShapeDtypeStruct((nh, s, hp), F32),
                   jax.ShapeDtypeStruct((nh, s, MLA_V), F32)],
        scratch_shapes=[pltpu.VMEM((s, 1), F32)],
        compiler_params=_cparams(("parallel", "arbitrary"), 56),
    )(qf, kf, v, o, lse, do)


def _merge_fwd(gl, gb, ys, yc, ym, *, name):
    s, d = ys.shape
    tr = _pick(s, 256, 8)

    def body(gl_ref, gb_ref, ys_ref, yc_ref, ym_ref, o_ref):
        acc = jnp.zeros((tr, d), F32)
        for k, y_ref in enumerate((ys_ref, yc_ref, ym_ref)):
            gt = _sigmoid(gl_ref[:, k * d:(k + 1) * d] + gb_ref[:, k * d:(k + 1) * d])
            acc = acc + gt * y_ref[...]
        o_ref[...] = acc.astype(o_ref.dtype)

    row = lambda w: pl.BlockSpec((tr, w), lambda i: (i, 0))
    return pl.pallas_call(
        body, name=name, grid=(s // tr,),
        in_specs=[row(3 * d), pl.BlockSpec((1, 3 * d), lambda i: (0, 0)), row(d), row(d), row(d)],
        out_specs=row(d), out_shape=jax.ShapeDtypeStruct((s, d), BF16),
        compiler_params=_cparams(("parallel",)),
    )(gl, gb, ys, yc, ym)


def _merge_bwd(gl, gb, ys, yc, ym, dm, *, name):
    s, d = ys.shape
    tr = _pick(s, 256, 8)

    def body(gl_ref, gb_ref, ys_ref, yc_ref, ym_ref, dm_ref, dgl_ref, dgb_ref, dys_ref, dyc_ref, dym_ref):
        dmv = dm_ref[...]
        parts = []
        for k, (y_ref, dy_ref) in enumerate(((ys_ref, dys_ref), (yc_ref, dyc_ref), (ym_ref, dym_ref))):
            gt = _sigmoid(gl_ref[:, k * d:(k + 1) * d] + gb_ref[:, k * d:(k + 1) * d])
            dy_ref[...] = (gt * dmv).astype(dy_ref.dtype)
            dl = dmv * y_ref[...] * gt * (1.0 - gt)
            dgl_ref[:, k * d:(k + 1) * d] = dl.astype(dgl_ref.dtype)
            parts.append(jnp.sum(dl, axis=0, keepdims=True))
        pb = jnp.concatenate(parts, axis=1)

        @pl.when(pl.program_id(0) == 0)
        def _():
            dgb_ref[...] = pb

        @pl.when(pl.program_id(0) > 0)
        def _():
            dgb_ref[...] += pb

    row = lambda w: pl.BlockSpec((tr, w), lambda i: (i, 0))
    vec = pl.BlockSpec((1, 3 * d), lambda i: (0, 0))
    return pl.pallas_call(
        body, name=name, grid=(s // tr,),
        in_specs=[row(3 * d), vec, row(d), row(d), row(d), row(d)],
        out_specs=[row(3 * d), vec, row(d), row(d), row(d)],
        out_shape=[jax.ShapeDtypeStruct((s, 3 * d), BF16), jax.ShapeDtypeStruct((1, 3 * d), F32)]
        + [jax.ShapeDtypeStruct((s, d), BF16)] * 3,
        compiler_params=_cparams(("arbitrary",)),
    )(gl, gb, ys, yc, ym, dm)


def _xattn_fwd(q, k, v, gq, *, name):
    s, d = q.shape
    dh = d // X_HEADS
    tr = _pick(s, 512, 8)
    scale = dh ** -0.5
    nt = (((1,), (1,)), ((), ()))

    def body(q_ref, k_ref, v_ref, gq_ref, o_ref):
        for h in range(X_HEADS):
            cs = slice(h * dh, (h + 1) * dh)
            qn, _, _ = _norm_part(q_ref[:, cs], gq_ref[...], dh)
            sc = lax.dot_general(qn.astype(BF16), k_ref[:, cs], nt, preferred_element_type=F32) * scale
            p = jnp.exp(sc - jnp.max(sc, axis=-1, keepdims=True))
            p = p / jnp.sum(p, axis=-1, keepdims=True)
            o_ref[:, cs] = jnp.dot(p.astype(BF16), v_ref[:, cs], preferred_element_type=F32)

    row = pl.BlockSpec((tr, d), lambda i: (i, 0))
    mem = pl.BlockSpec(k.shape, lambda i: (0, 0))
    return pl.pallas_call(
        body, name=name, grid=(s // tr,),
        in_specs=[row, mem, mem, pl.BlockSpec((1, dh), lambda i: (0, 0))], out_specs=row,
        out_shape=jax.ShapeDtypeStruct((s, d), F32), compiler_params=_cparams(("parallel",)),
    )(q, k, v, gq)


def _xattn_bwd(q, k, v, gq, do, *, name):
    s, d = q.shape
    dh = d // X_HEADS
    tr = _pick(s, 512, 8)
    scale = dh ** -0.5
    nt = (((1,), (1,)), ((), ()))
    tn = (((0,), (0,)), ((), ()))

    def body(q_ref, k_ref, v_ref, gq_ref, do_ref, dq_ref, dk_ref, dv_ref, dgq_ref):
        first = pl.program_id(0) == 0
        pg = jnp.zeros((1, dh), F32)
        for h in range(X_HEADS):
            cs = slice(h * dh, (h + 1) * dh)
            qn, qh, qs = _norm_part(q_ref[:, cs], gq_ref[...], dh)
            qnb = qn.astype(BF16)
            kh = k_ref[:, cs]
            sc = lax.dot_general(qnb, kh, nt, preferred_element_type=F32) * scale
            p = jnp.exp(sc - jnp.max(sc, axis=-1, keepdims=True))
            p = p / jnp.sum(p, axis=-1, keepdims=True)
            dob = do_ref[:, cs].astype(BF16)
            dp = lax.dot_general(dob, v_ref[:, cs], nt, preferred_element_type=F32)
            ds = (p * (dp - jnp.sum(dp * p, axis=-1, keepdims=True)) * scale).astype(BF16)
            dqn = jnp.dot(ds, kh, preferred_element_type=F32)
            dq_ref[:, cs] = _norm_part_bwd(dqn, gq_ref[...], qh, qs, dh).astype(dq_ref.dtype)
            pg = pg + jnp.sum(dqn * qh, axis=0, keepdims=True)
            pv = lax.dot_general(p.astype(BF16), dob, tn, preferred_element_type=F32)
            pk = lax.dot_general(ds, qnb, tn, preferred_element_type=F32)

            @pl.when(first)
            def _():
                dv_ref[:, cs] = pv
                dk_ref[:, cs] = pk

            @pl.when(jnp.logical_not(first))
            def _():
                dv_ref[:, cs] += pv
                dk_ref[:, cs] += pk

        @pl.when(first)
        def _():
            dgq_ref[...] = pg

        @pl.when(jnp.logical_not(first))
        def _():
            dgq_ref[...] += pg

    row = pl.BlockSpec((tr, d), lambda i: (i, 0))
    mem = pl.BlockSpec(k.shape, lambda i: (0, 0))
    vec = pl.BlockSpec((1, dh), lambda i: (0, 0))
    return pl.pallas_call(
        body, name=name, grid=(s // tr,),
        in_specs=[row, mem, mem, vec, row], out_specs=[row, mem, mem, vec],
        out_shape=[jax.ShapeDtypeStruct((s, d), BF16), jax.ShapeDtypeStruct(k.shape, F32),
                   jax.ShapeDtypeStruct(k.shape, F32), jax.ShapeDtypeStruct((1, dh), F32)],
        compiler_params=_cparams(("arbitrary",)),
    )(q, k, v, gq, do)


def _swiglu_fwd(h1, *, name):
    s, w2 = h1.shape
    w = w2 // 2
    tr = _pick(s, 256, 8)
    tc = _pick(w, 1408, 128)
    ncb = w // tc

    def body(g_ref, u_ref, o_ref):
        o_ref[...] = (_silu(g_ref[...]) * u_ref[...]).astype(o_ref.dtype)

    return pl.pallas_call(
        body, name=name, grid=(s // tr, ncb),
        in_specs=[pl.BlockSpec((tr, tc), lambda i, j: (i, j)), pl.BlockSpec((tr, tc), lambda i, j: (i, j + ncb))],
        out_specs=pl.BlockSpec((tr, tc), lambda i, j: (i, j)),
        out_shape=jax.ShapeDtypeStruct((s, w), BF16), compiler_params=_cparams(("parallel", "parallel")),
    )(h1, h1)


def _swiglu_bwd(h1, dact, *, name):
    s, w2 = h1.shape
    w = w2 // 2
    tr = _pick(s, 256, 8)
    tc = _pick(w, 1408, 128)
    ncb = w // tc

    def body(g_ref, u_ref, d_ref, dg_ref, du_ref):
        gv = g_ref[...]
        dv = d_ref[...]
        dg_ref[...] = (dv * u_ref[...] * _dsilu(gv)).astype(dg_ref.dtype)
        du_ref[...] = (dv * _silu(gv)).astype(du_ref.dtype)

    blk = pl.BlockSpec((tr, tc), lambda i, j: (i, j))
    dg, du = pl.pallas_call(
        body, name=name, grid=(s // tr, ncb),
        in_specs=[blk, pl.BlockSpec((tr, tc), lambda i, j: (i, j + ncb)), blk],
        out_specs=[blk, blk],
        out_shape=[jax.ShapeDtypeStruct((s, w), BF16)] * 2, compiler_params=_cparams(("parallel", "parallel")),
    )(h1, h1, dact)
    return jnp.concatenate([dg, du], axis=1)


def _add(a, b, *, name):
    r, w = a.shape
    tr = _pick(r, 512, 8)

    def body(a_ref, b_ref, o_ref):
        o_ref[...] = a_ref[...] + b_ref[...].astype(F32)

    row = pl.BlockSpec((tr, w), lambda i: (i, 0))
    return pl.pallas_call(
        body, name=name, grid=(r // tr,), in_specs=[row, row], out_specs=row,
        out_shape=jax.ShapeDtypeStruct((r, w), F32), compiler_params=_cparams(("parallel",)),
    )(a, b)


def _loss(y, target, *, name):
    r, w = y.shape
    tr = _pick(r, 512, 8)

    def body(y_ref, t_ref, dy_ref, l_ref):
        err = y_ref[...] - t_ref[...]
        dy_ref[...] = err * (1.0 / w)
        part = jnp.zeros((8, LANE), F32) + 0.5 * jnp.sum(jnp.mean(err * err, axis=-1, keepdims=True))

        @pl.when(pl.program_id(0) == 0)
        def _():
            l_ref[...] = part

        @pl.when(pl.program_id(0) > 0)
        def _():
            l_ref[...] += part

    row = pl.BlockSpec((tr, w), lambda i: (i, 0))
    dy, l = pl.pallas_call(
        body, name=name, grid=(r // tr,), in_specs=[row, row],
        out_specs=[row, pl.BlockSpec((8, LANE), lambda i: (0, 0))],
        out_shape=[jax.ShapeDtypeStruct((r, w), F32), jax.ShapeDtypeStruct((8, LANE), F32)],
        compiler_params=_cparams(("arbitrary",)),
    )(y, target)
    return dy, l[0, 0]


def _adamw(w, g, m, v, *, name):
    r, c = w.shape
    tr = _pick(r, 256, 8)
    c1 = 1.0 - ADAM_B1 ** ADAM_STEP
    c2 = 1.0 - ADAM_B2 ** ADAM_STEP

    def body(w_ref, g_ref, m_ref, v_ref, d_ref, nm_ref, nv_ref):
        gv = g_ref[...]
        nm = ADAM_B1 * m_ref[...] + (1.0 - ADAM_B1) * gv
        nv = ADAM_B2 * v_ref[...] + (1.0 - ADAM_B2) * (gv * gv)
        nm_ref[...] = nm
        nv_ref[...] = nv
        d_ref[...] = -ADAM_LR * ((nm / c1) / (jnp.sqrt(nv / c2) + ADAM_EPS) + ADAM_WD * w_ref[...])

    row = pl.BlockSpec((tr, c), lambda i: (i, 0))
    return pl.pallas_call(
        body, name=name, grid=(r // tr,), in_specs=[row] * 4, out_specs=[row] * 3,
        out_shape=[jax.ShapeDtypeStruct((r, c), F32)] * 3, compiler_params=_cparams(("parallel",)),
    )(w, g, m, v)


IN_SPLIT = dict(z=(0, 1024), xbc=(1024, 3072), dt=(3072, 3200), glu=(3200, 5248), ql=(5248, 5632),
                ckv=(5632, 5888), kr=(5888, 6016), gate=(6016, 9088))


IN_WIDTH_PAD = 9216


def _w_in_pad(w):
    zeros = lambda n: jnp.zeros(w.shape[:-1] + (n,), w.dtype)
    return jnp.concatenate([w[..., :3088], zeros(112), w[..., 3088:5840], zeros(64), w[..., 5840:],
                            zeros(IN_WIDTH_PAD - 9088)], axis=-1)


def _w_in_unpad(g):
    return jnp.concatenate([g[..., :3088], g[..., 3200:5952], g[..., 6016:9088]], axis=-1)


def _qb_pad(w):
    lead = w.shape[:-1]
    w = w.reshape(lead + (MLA_HEADS, MLA_NOPE + MLA_ROPE))
    w = jnp.concatenate([w, jnp.zeros(lead + (MLA_HEADS, MLA_HP - MLA_NOPE - MLA_ROPE), w.dtype)], axis=-1)
    return w.reshape(lead + (MLA_HEADS * MLA_HP,))


def _qb_unpad(g):
    lead = g.shape[:-1]
    g = g.reshape(lead + (MLA_HEADS, MLA_HP))[..., :MLA_NOPE + MLA_ROPE]
    return g.reshape(lead + (MLA_HEADS * (MLA_NOPE + MLA_ROPE),))


def _pad_lanes(v, n):
    return jnp.concatenate([v, jnp.zeros((n - v.shape[0],), v.dtype)]).reshape(1, n)


def _layer_params(full, rep, l):
    p = {}
    w_in = full['w_in'][l]
    for k, (a, b) in IN_SPLIT.items():
        p['w_' + k] = w_in[:, a:b]
    p['w_in'] = w_in
    for k in ('mla_w_q_b', 'mla_w_kv_b', 'xattn_w_kv', 'ffn_w_in', 'ssd_w_out', 'conv_w_out', 'mla_w_o', 'w_out',
              'xattn_w_q', 'xattn_w_o', 'ffn_w_out', 'ssd_conv_w', 'conv_dw_w'):
        p[k] = full[k][l]
    p['gate_b'] = full['gate_b'][l].reshape(1, -1)
    row = lambda name: rep[name][l].reshape(1, -1)
    for k in ('mix_norm_g', 'ssd_conv_b', 'ssd_norm_g', 'conv_dw_b', 'conv_ln_g', 'conv_ln_b', 'mla_q_a_g',
              'mla_kv_a_g', 'xattn_norm_g', 'mem_norm_g', 'xattn_q_norm_g', 'xattn_k_norm_g', 'ffn_norm_g'):
        p[k] = row(k)
    for k in ('ssd_dt_bias', 'ssd_a_log', 'ssd_d'):
        p[k] = _pad_lanes(rep[k][l], LANE)
        p[k + '_t'] = p[k].reshape(LANE, 1)
    p['gq'] = _pad_lanes(rep['mla_q_norm_g'][l], MLA_HP)
    p['gk'] = _pad_lanes(rep['mla_k_norm_g'][l], MLA_HP)
    return p


def _layer_fwd(x, mem, cs, sn, p, l):
    n = lambda s: f"l{l}_{s}"
    s_len, d = x.shape
    nc = s_len // CHUNK
    sv = {'x': x}
    u = _rms_fwd(x, p['mix_norm_g'], out_dtype=BF16, name=n("mix_norm"))
    z = _matmul(u, p['w_z'], name=n("in_z"))
    xbc = _matmul(u, p['w_xbc'], name=n("in_xbc"))
    dtr = _matmul(u, p['w_dt'], name=n("in_dt"))
    glu = _matmul(u, p['w_glu'], name=n("in_glu"))
    ql = _matmul(u, p['w_ql'], name=n("in_ql"))
    ckv = _matmul(u, p['w_ckv'], name=n("in_ckv"))
    krr = _matmul(u, p['w_kr'], name=n("in_kr"))
    gl = _matmul(u, p['w_gate'], name=n("in_gate"))
    pre_s, act_s = _conv_fwd(xbc, p['ssd_conv_w'], p['ssd_conv_b'], glu=False, act=True, name=n("ssd_conv"))
    dtrt = dtr.reshape(nc, CHUNK, LANE).transpose(0, 2, 1)
    y_scan, prev = _ssd_scan_fwd(act_s, dtr, dtrt, p['ssd_dt_bias'], p['ssd_dt_bias_t'], p['ssd_a_log'],
                                 p['ssd_a_log_t'], p['ssd_d'], name=n("ssd_scan"))
    yn = _ssd_gate_fwd(y_scan, z, p['ssd_norm_g'], name=n("ssd_gate"))
    y_ssd = _matmul(yn, p['ssd_w_out'], name=n("ssd_out"))
    pre_c, = _conv_fwd(glu, p['conv_dw_w'], p['conv_dw_b'], glu=True, act=False, name=n("dw_conv"))
    vc = _ln_silu_fwd(pre_c, p['conv_ln_g'], p['conv_ln_b'], name=n("conv_ln"))
    y_conv = _matmul(vc, p['conv_w_out'], name=n("conv_out"))
    qln = _rms_fwd(ql, p['mla_q_a_g'], out_dtype=BF16, name=n("q_a_norm"))
    q = _matmul(qln, p['mla_w_q_b'], name=n("q_b"))
    ckvn = _rms_fwd(ckv, p['mla_kv_a_g'], out_dtype=BF16, name=n("kv_a_norm"))
    kv = _matmul(ckvn, p['mla_w_kv_b'], name=n("kv_b"))
    qf, kf, v = _mla_prep_fwd(q, kv, krr, cs, sn, p['gq'], p['gk'], name=n("mla_prep"))
    o, lse = _mla_attn_fwd(qf, kf, v, name=n("mla_attn"))
    y_mla = _matmul(o, p['mla_w_o'], name=n("mla_out"))
    merged = _merge_fwd(gl, p['gate_b'], y_ssd, y_conv, y_mla, name=n("merge"))
    x1 = _matmul(merged, p['w_out'], add=x, name=n("mix_out"))
    hx = _rms_fwd(x1, p['xattn_norm_g'], out_dtype=BF16, name=n("xattn_norm"))
    qx = _matmul(hx, p['xattn_w_q'], name=n("xattn_q"))
    memn = _rms_fwd(mem, p['mem_norm_g'], out_dtype=BF16, name=n("mem_norm"))
    kvx = _matmul(memn, p['xattn_w_kv'], name=n("xattn_kv"))
    m_len = mem.shape[0]
    dh = d // X_HEADS
    kraw = kvx[:, :d].reshape(m_len * X_HEADS, dh)
    kx = _rms_fwd(kraw, p['xattn_k_norm_g'], out_dtype=BF16, name=n("xattn_k_norm")).reshape(m_len, d)
    vx = kvx[:, d:].astype(BF16)
    ox = _xattn_fwd(qx, kx, vx, p['xattn_q_norm_g'], name=n("xattn_core"))
    x2 = _matmul(ox, p['xattn_w_o'], add=x1, name=n("xattn_out"))
    hf = _rms_fwd(x2, p['ffn_norm_g'], out_dtype=BF16, name=n("ffn_norm"))
    h1 = _matmul(hf, p['ffn_w_in'], name=n("ffn_in"))
    act = _swiglu_fwd(h1, name=n("swiglu"))
    x3 = _matmul(act, p['ffn_w_out'], add=x2, name=n("ffn_out"))
    sv.update(u=u, z=z, xbc=xbc, dtr=dtr, dtrt=dtrt, glu=glu, ql=ql, ckv=ckv, krr=krr, gl=gl, pre_s=pre_s,
              act_s=act_s, y_scan=y_scan, prev=prev, yn=yn, y_ssd=y_ssd, pre_c=pre_c, vc=vc, y_conv=y_conv,
              qln=qln, q=q, ckvn=ckvn, kv=kv, qf=qf, kf=kf, v=v, o=o, lse=lse, y_mla=y_mla, merged=merged,
              x1=x1, hx=hx, qx=qx, memn=memn, kraw=kraw, kx=kx, vx=vx, ox=ox, x2=x2, hf=hf, h1=h1, act=act)
    return x3, sv


DW_KEY = dict(ffn_out_dw='ffn_w_out', ffn_in_dw='ffn_w_in', xattn_out_dw='xattn_w_o', xattn_q_dw='xattn_w_q',
              xattn_kv_dw='xattn_w_kv', mix_out_dw='w_out', mla_out_dw='mla_w_o', q_b_dw='mla_w_q_b',
              kv_b_dw='mla_w_kv_b', conv_out_dw='conv_w_out', ssd_out_dw='ssd_w_out', in_dw='w_in')


def _layer_bwd(dx3, mem, cs, sn, p, sv, l, stacks, depth):
    n = lambda s: f"l{l}_b_{s}"
    dw = lambda s: dict(name=n(s), out_dtype=BF16, into=(stacks.get(DW_KEY[s]), l, depth))
    g = {}
    d = dx3.shape[1]
    dact = _matmul(dx3, p['ffn_w_out'], tb=True, name=n("ffn_out_dx"))
    g['ffn_w_out'] = _matmul(sv['act'], dx3, ta=True, **dw("ffn_out_dw"))
    dh1 = _swiglu_bwd(sv['h1'], dact, name=n("swiglu"))
    g['ffn_w_in'] = _matmul(sv['hf'], dh1, ta=True, **dw("ffn_in_dw"))
    dhf = _matmul(dh1, p['ffn_w_in'], tb=True, name=n("ffn_in_dx"))
    dx2, g['ffn_norm_g'] = _rms_bwd(sv['x2'], p['ffn_norm_g'], dhf, dx_dtype=F32, add=dx3, name=n("ffn_norm"))
    dox = _matmul(dx2, p['xattn_w_o'], tb=True, name=n("xattn_out_dx"))
    g['xattn_w_o'] = _matmul(sv['ox'], dx2, ta=True, **dw("xattn_out_dw"))
    dqx, dkx, dvx, g['xattn_q_norm_g'] = _xattn_bwd(sv['qx'], sv['kx'], sv['vx'], p['xattn_q_norm_g'], dox,
                                                    name=n("xattn_core"))
    g['xattn_w_q'] = _matmul(sv['hx'], dqx, ta=True, **dw("xattn_q_dw"))
    dhx = _matmul(dqx, p['xattn_w_q'], tb=True, name=n("xattn_q_dx"))
    dx1, g['xattn_norm_g'] = _rms_bwd(sv['x1'], p['xattn_norm_g'], dhx, dx_dtype=F32, add=dx2, name=n("xattn_norm"))
    m_len = mem.shape[0]
    dh = d // X_HEADS
    dkraw, g['xattn_k_norm_g'] = _rms_bwd(sv['kraw'], p['xattn_k_norm_g'], dkx.reshape(m_len * X_HEADS, dh),
                                          dx_dtype=BF16, name=n("xattn_k_norm"))
    dkvx = jnp.concatenate([dkraw.reshape(m_len, d), dvx.astype(BF16)], axis=1)
    g['xattn_w_kv'] = _matmul(sv['memn'], dkvx, ta=True, **dw("xattn_kv_dw"))
    dmemn = _matmul(dkvx, p['xattn_w_kv'], tb=True, name=n("xattn_kv_dx"))
    _, g['mem_norm_g'] = _rms_bwd(mem, p['mem_norm_g'], dmemn, dx_dtype=BF16, name=n("mem_norm"))
    dmerged = _matmul(dx1, p['w_out'], tb=True, name=n("mix_out_dx"))
    g['w_out'] = _matmul(sv['merged'], dx1, ta=True, **dw("mix_out_dw"))
    dgl, g['gate_b'], dys, dyc, dym = _merge_bwd(sv['gl'], p['gate_b'], sv['y_ssd'], sv['y_conv'], sv['y_mla'],
                                                 dmerged, name=n("merge"))
    do = _matmul(dym, p['mla_w_o'], tb=True, name=n("mla_out_dx"))
    g['mla_w_o'] = _matmul(sv['o'], dym, ta=True, **dw("mla_out_dw"))
    dqf, dkf, dv = _mla_attn_bwd(sv['qf'], sv['kf'], sv['v'], sv['o'], sv['lse'], do, name=n("mla_attn"))
    dq, dkv, dkrr, g['gq'], g['gk'] = _mla_prep_bwd(sv['q'], sv['kv'], sv['krr'], cs, sn, p['gq'], p['gk'],
                                                    dqf, dkf, dv, name=n("mla_prep"))
    g['mla_w_q_b'] = _matmul(sv['qln'], dq, ta=True, **dw("q_b_dw"))
    dqln = _matmul(dq, p['mla_w_q_b'], tb=True, name=n("q_b_dx"))
    dql, g['mla_q_a_g'] = _rms_bwd(sv['ql'], p['mla_q_a_g'], dqln, dx_dtype=BF16, name=n("q_a_norm"))
    g['mla_w_kv_b'] = _matmul(sv['ckvn'], dkv, ta=True, **dw("kv_b_dw"))
    dckvn = _matmul(dkv, p['mla_w_kv_b'], tb=True, name=n("kv_b_dx"))
    dckv, g['mla_kv_a_g'] = _rms_bwd(sv['ckv'], p['mla_kv_a_g'], dckvn, dx_dtype=BF16, name=n("kv_a_norm"))
    dvc = _matmul(dyc, p['conv_w_out'], tb=True, name=n("conv_out_dx"))
    g['conv_w_out'] = _matmul(sv['vc'], dyc, ta=True, **dw("conv_out_dw"))
    dpre_c, g['conv_ln_g'], g['conv_ln_b'] = _ln_silu_bwd(sv['pre_c'], p['conv_ln_g'], p['conv_ln_b'], dvc,
                                                          name=n("conv_ln"))
    da, dg, g['conv_dw_w'], g['conv_dw_b'] = _conv_bwd(sv['glu'], p['conv_dw_w'], p['conv_dw_b'], dpre_c, None,
                                                       glu=True, act=False, name=n("dw_conv"))
    dyn = _matmul(dys, p['ssd_w_out'], tb=True, name=n("ssd_out_dx"))
    g['ssd_w_out'] = _matmul(sv['yn'], dys, ta=True, **dw("ssd_out_dw"))
    dy_scan, dz, g['ssd_norm_g'] = _ssd_gate_bwd(sv['y_scan'], sv['z'], p['ssd_norm_g'], dyn, name=n("ssd_gate"))
    dxs, db, dc, ddtr, g['ssd_a_log'], g['ssd_dt_bias'], g['ssd_d'] = _ssd_scan_bwd(
        sv['act_s'], sv['dtr'], sv['dtrt'], p['ssd_dt_bias'], p['ssd_dt_bias_t'], p['ssd_a_log'], p['ssd_a_log_t'],
        p['ssd_d'], sv['prev'], dy_scan, name=n("ssd_scan"))
    dact_s = jnp.concatenate([dxs, db, dc], axis=1)
    dxbc, g['ssd_conv_w'], g['ssd_conv_b'] = _conv_bwd(sv['xbc'], p['ssd_conv_w'], p['ssd_conv_b'], dact_s,
                                                       sv['pre_s'], glu=False, act=True, name=n("ssd_conv"))
    tail = jnp.zeros((dz.shape[0], IN_WIDTH_PAD - IN_SPLIT['gate'][1]), BF16)
    dproj = jnp.concatenate([dz, dxbc, ddtr, da, dg, dql, dckv, dkrr, dgl, tail], axis=1)
    g['w_in'] = _matmul(sv['u'], dproj, ta=True, **dw("in_dw"))
    du = _matmul(dproj, p['w_in'], tb=True, name=n("in_dx"))
    dx, g['mix_norm_g'] = _rms_bwd(sv['x'], p['mix_norm_g'], du, dx_dtype=F32, add=dx1, name=n("mix_norm"))
    return dx, g


REP_NAMES = ('mix_norm_g', 'ssd_conv_b', 'ssd_dt_bias', 'ssd_a_log', 'ssd_d', 'ssd_norm_g', 'conv_dw_b', 'conv_ln_g',
             'conv_ln_b', 'mla_q_a_g', 'mla_kv_a_g', 'mla_q_norm_g', 'mla_k_norm_g', 'xattn_norm_g', 'mem_norm_g',
             'xattn_q_norm_g', 'xattn_k_norm_g', 'ffn_norm_g')
BIG = (('w_in', 1, 1024, 8912), ('mla_w_q_b', 1, 384, 1536), ('mla_w_kv_b', 1, 256, 2048),
       ('xattn_w_kv', 1, 1024, 2048), ('ffn_w_in', 1, 1024, 5632), ('ssd_w_out', 0, 1024, 1024),
       ('conv_w_out', 0, 1024, 1024), ('mla_w_o', 0, 1024, 1024), ('w_out', 0, 1024, 1024),
       ('xattn_w_q', 0, 1024, 1024), ('xattn_w_o', 0, 1024, 1024), ('ffn_w_out', 0, 2816, 1024))
SMALL = (('ssd_conv_w', 1, 4, 2048), ('conv_dw_w', 1, 31, 1024), ('gate_b', 1, 3, 1024))


def _rope_tables(positions):
    half = MLA_ROPE // 2
    inv = ROPE_THETA ** (-jnp.arange(0, MLA_ROPE, 2, dtype=F32) / MLA_ROPE)
    ang = positions.astype(F32)[:, None] * inv
    cos, sin = jnp.cos(ang), jnp.sin(ang)
    z = jnp.zeros((positions.shape[0], LANE - 2 * half), F32)
    return jnp.concatenate([cos, cos, z], axis=1), jnp.concatenate([-sin, sin, z], axis=1)


def _local_step(x, mem, positions, target, full, rep):
    depth = rep['mix_norm_g'].shape[0]
    cs, sn = _rope_tables(positions)
    params, saved = [], []
    h = x
    for l in range(depth):
        p = _layer_params(full, rep, l)
        h, sv = _layer_fwd(h, mem, cs, sn, p, l)
        params.append(p)
        saved.append(sv)
    dh, loss = _loss(h, target, name="loss")
    layer_grads = [None] * depth
    stacks = {}
    for l in reversed(range(depth)):
        dh, layer_grads[l] = _layer_bwd(dh, mem, cs, sn, params[l], saved[l], l, stacks, depth)
        stacks = {k: layer_grads[l][k] for k in DW_KEY.values()}
    stack = lambda k: jnp.stack([layer_grads[l][k] for l in range(depth)])
    gfull = {k: stack(k) for k, _, _, _ in SMALL}
    gfull.update(stacks)
    gfull['w_in'] = _w_in_unpad(gfull['w_in'])
    gfull['mla_w_q_b'] = _qb_unpad(gfull['mla_w_q_b'])
    gfull['gate_b'] = gfull['gate_b'].reshape(depth, 3, -1)
    grep = {}
    for k in REP_NAMES:
        if k == 'mla_q_norm_g':
            grep[k] = stack('gq')[:, 0, :MLA_NOPE + MLA_ROPE]
        elif k == 'mla_k_norm_g':
            grep[k] = stack('gk')[:, 0, :MLA_NOPE + MLA_ROPE]
        elif k in ('ssd_dt_bias', 'ssd_a_log', 'ssd_d'):
            grep[k] = stack(k)[:, 0, :SSD_HEADS]
        else:
            grep[k] = stack(k)[:, 0, :]
    return loss, dh, gfull, grep


N_CHIPS = 4
HBM_SPEC = pl.BlockSpec(memory_space=pl.ANY)
ROW, COL, STK, REP = "row", "col", "stk", "rep"


def _kind(axis, cs):
    if axis == 0:
        return ROW
    return COL if cs % LANE == 0 else STK


def _mesh_pos():
    return lax.axis_index("x"), lax.axis_index("y"), lax.axis_index("c")


def _chip_view(ref, kind, j, a, b, layers=None):
    lsel = slice(None) if layers is None else pl.ds(layers[0], layers[1])
    if kind == STK:
        return ref.at[j, lsel]
    if kind == ROW:
        return ref.at[lsel, pl.ds(pl.multiple_of(j * a, 8), a), :]
    if kind == COL:
        return ref.at[lsel, :, pl.ds(pl.multiple_of(j * b, LANE), b)]
    return ref.at[lsel]


def _all_gather(shards, kinds, *, name):
    n = len(shards)
    depth = shards[0].shape[0]
    lh = depth // 2

    def out_shape(w, kind):
        _, a, b = w.shape
        full = {ROW: (depth, N_CHIPS * a, b), COL: (depth, a, N_CHIPS * b), STK: (N_CHIPS, depth, a, b)}[kind]
        return jax.ShapeDtypeStruct(full, w.dtype)

    def body(*refs):
        w_refs, out_refs = refs[:n], refs[n:2 * n]
        send_sems, recv_sems = refs[2 * n:]
        x, y, cc = _mesh_pos()
        me = (x, y, cc)
        sibling = (x, y, 1 - cc)
        chips = [(1 - x, y), (x, 1 - y), (1 - x, 1 - y)]
        slot = lambda chip: 2 * chip[0] + chip[1]

        def part(i, chip, hc):
            _, a, b = w_refs[i].shape
            return _chip_view(out_refs[i], kinds[i], slot(chip), a, b, (hc * lh, lh))

        def own(i):
            _, a, b = w_refs[i].shape
            return _chip_view(out_refs[i], kinds[i], slot((x, y)), a, b)

        def copy(i, k, src, dst, to):
            return pltpu.make_async_remote_copy(src_ref=src, dst_ref=dst, send_sem=send_sems.at[i, k],
                                                recv_sem=recv_sems.at[i, k], device_id=to, device_id_type=MESH)

        sends = []
        for i in range(n):
            my_half = w_refs[i].at[pl.ds(cc * lh, lh)]
            for k, chip in enumerate(chips):
                sends.append(copy(i, k, my_half, part(i, (x, y), cc), (*chip, cc)))
            sends.append(copy(i, 6, w_refs[i], own(i), sibling))
        for cp in sends:
            cp.start()
        for k, chip in enumerate(chips):
            for i in range(n):
                copy(i, k, part(i, chip, cc), part(i, chip, cc), me).wait_recv()
                cp = copy(i, 3 + k, part(i, chip, cc), part(i, chip, cc), sibling)
                cp.start()
                sends.append(cp)
        for i in range(n):
            for k, chip in enumerate(chips):
                copy(i, 3 + k, part(i, chip, 1 - cc), part(i, chip, 1 - cc), me).wait_recv()
            copy(i, 6, own(i), own(i), me).wait_recv()
        for cp in sends:
            cp.wait_send()

    return pl.pallas_call(
        body, name=name, in_specs=[HBM_SPEC] * n, out_specs=[HBM_SPEC] * n,
        out_shape=[out_shape(w, kd) for w, kd in zip(shards, kinds)],
        scratch_shapes=[pltpu.SemaphoreType.DMA((n, 7)), pltpu.SemaphoreType.DMA((n, 7))],
    )(*shards)


def _layers_half(ref, kind, start, lh):
    return ref.at[:, pl.ds(start, lh)] if kind == STK else ref.at[pl.ds(start, lh)]


def _rs_pair(gs, kinds, *, name):
    n = len(gs)

    def half_shape(g, kind):
        s = list(g.shape)
        s[1 if kind == STK else 0] //= 2
        return jax.ShapeDtypeStruct(tuple(s), g.dtype)

    def body(*refs):
        g_refs, buf_refs = refs[:n], refs[n:2 * n]
        send_sems, recv_sems = refs[2 * n:]
        x, y, cc = _mesh_pos()
        cps = []
        for i in range(n):
            lh = buf_refs[i].shape[1 if kinds[i] == STK else 0]
            cp = pltpu.make_async_remote_copy(src_ref=_layers_half(g_refs[i], kinds[i], (1 - cc) * lh, lh),
                                              dst_ref=buf_refs[i], send_sem=send_sems.at[i], recv_sem=recv_sems.at[i],
                                              device_id=(x, y, 1 - cc), device_id_type=MESH)
            cp.start()
            cps.append(cp)
        for cp in cps:
            cp.wait()

    return pl.pallas_call(
        body, name=name, in_specs=[HBM_SPEC] * n, out_specs=[HBM_SPEC] * n,
        out_shape=[half_shape(g, kd) for g, kd in zip(gs, kinds)],
        scratch_shapes=[pltpu.SemaphoreType.DMA((n,)), pltpu.SemaphoreType.DMA((n,))],
    )(*gs)


def _row_tile(rows, cols):
    return _pick(rows, max(8, (512 * 1024 // cols) // 8 * 8), 8)


def _rs_pair_add(g, buf, kind, cc, out_dtype, *, name):
    cols = g.shape[-1]
    pre = g.shape[0] if kind == STK else 1
    rows = buf.size // (pre * cols)
    tr = _row_tile(rows, cols)

    def body(cc_ref, g_ref, b_ref, o_ref):
        o_ref[...] = (g_ref[...].astype(F32) + b_ref[...].astype(F32)).astype(o_ref.dtype)

    out = pl.pallas_call(
        body, name=name,
        grid_spec=pltpu.PrefetchScalarGridSpec(
            num_scalar_prefetch=1, grid=(pre, rows // tr),
            in_specs=[pl.BlockSpec((None, None, tr, cols), lambda s, i, cc_ref: (s, cc_ref[0], i, 0)),
                      pl.BlockSpec((None, tr, cols), lambda s, i, cc_ref: (s, i, 0))],
            out_specs=pl.BlockSpec((None, tr, cols), lambda s, i, cc_ref: (s, i, 0))),
        out_shape=jax.ShapeDtypeStruct((pre, rows, cols), out_dtype),
        compiler_params=_cparams(("parallel", "parallel")),
    )(cc.reshape(1).astype(jnp.int32), g.reshape(pre, 2, rows, cols), buf.reshape(pre, rows, cols))
    return out.reshape(buf.shape)


def _rs_cross(ps, kinds, shard_shapes, *, name):
    n = len(ps)

    def body(*refs):
        p_refs, out_refs = refs[:n], refs[n:2 * n]
        send_sems, recv_sems, local_sems = refs[2 * n:]
        x, y, cc = _mesh_pos()
        chips = [(1 - x, y), (x, 1 - y), (1 - x, 1 - y)]
        slot = lambda chip: 2 * chip[0] + chip[1]
        local, sends = [], []
        for i in range(n):
            a, b = shard_shapes[i]
            if kinds[i] == REP:
                cp = pltpu.make_async_copy(p_refs[i], out_refs[i].at[slot((x, y))], local_sems.at[i])
                cp.start()
                local.append(cp)
            for k, chip in enumerate(chips):
                cp = pltpu.make_async_remote_copy(src_ref=_chip_view(p_refs[i], kinds[i], slot(chip), a, b),
                                                  dst_ref=out_refs[i].at[slot((x, y))], send_sem=send_sems.at[i, k],
                                                  recv_sem=recv_sems.at[i, k], device_id=(*chip, cc),
                                                  device_id_type=MESH)
                cp.start()
                sends.append(cp)
        for i in range(n):
            for k, chip in enumerate(chips):
                landed = out_refs[i].at[slot(chip)]
                pltpu.make_async_remote_copy(src_ref=landed, dst_ref=landed, send_sem=send_sems.at[i, k],
                                             recv_sem=recv_sems.at[i, k], device_id=(*chip, cc),
                                             device_id_type=MESH).wait_recv()
        for cp in sends:
            cp.wait_send()
        for cp in local:
            cp.wait()

    def out_shape(p, kind, ab):
        lh = p.shape[1 if kind == STK else 0]
        return jax.ShapeDtypeStruct((N_CHIPS, lh) + tuple(ab), p.dtype)

    return pl.pallas_call(
        body, name=name, in_specs=[HBM_SPEC] * n, out_specs=[HBM_SPEC] * n,
        out_shape=[out_shape(p, kd, ab) for p, kd, ab in zip(ps, kinds, shard_shapes)],
        scratch_shapes=[pltpu.SemaphoreType.DMA((n, 3)), pltpu.SemaphoreType.DMA((n, 3)),
                        pltpu.SemaphoreType.DMA((n,))],
    )(*ps)


def _rs_sum(p, landed, kind, ab, place, cc, *, name):
    a, b = ab
    lh = landed.shape[1]
    tr = _row_tile(a, b)
    blk = lambda f: pl.BlockSpec((None, None, tr, b), f)
    if kind == ROW:
        p_spec = pl.BlockSpec((None, tr, b), lambda l, i, w, c: (l, w[0] * (a // tr) + i, 0))
    elif kind == COL:
        p_spec = pl.BlockSpec((None, tr, b), lambda l, i, w, c: (l, i, w[0]))
    elif kind == STK:
        p_spec = blk(lambda l, i, w, c: (w[0], l, i, 0))
    else:
        p_spec = blk(lambda l, i, w, c: (0, l, i, 0))
        p = landed
    if kind == REP:
        others = [blk(lambda l, i, w, c, k=k: (k, l, i, 0)) for k in (1, 2, 3)]
    else:
        others = [blk(lambda l, i, w, c: (lax.rem(w[0] + 2, 4), l, i, 0)),
                  blk(lambda l, i, w, c: (w[0] + 1 - 2 * lax.rem(w[0], 2), l, i, 0)),
                  blk(lambda l, i, w, c: (3 - w[0], l, i, 0))]

    def body(w_ref, c_ref, p_ref, b1_ref, b2_ref, b3_ref, o_ref):
        f = lambda r: r[...].astype(F32)
        o_ref[...] = ((f(p_ref) + f(b1_ref)) + f(b2_ref)) + f(b3_ref)

    return pl.pallas_call(
        body, name=name,
        grid_spec=pltpu.PrefetchScalarGridSpec(
            num_scalar_prefetch=2, grid=(lh, a // tr), in_specs=[p_spec] + others,
            out_specs=pl.BlockSpec((None, tr, b), lambda l, i, w, c: (c[0] * lh + l, i, 0))),
        out_shape=jax.ShapeDtypeStruct((2 * lh, a, b), F32),
        compiler_params=_cparams(("parallel", "parallel")),
    )(place, cc, p, landed, landed, landed)


def _rs_share(fs, *, name):
    n = len(fs)

    def body(*refs):
        out_refs = refs[n:2 * n]
        send_sems, recv_sems = refs[2 * n:]
        x, y, cc = _mesh_pos()
        cps = []
        for i in range(n):
            lh = out_refs[i].shape[0] // 2
            mine = out_refs[i].at[pl.ds(cc * lh, lh)]
            cp = pltpu.make_async_remote_copy(src_ref=mine, dst_ref=mine, send_sem=send_sems.at[i],
                                              recv_sem=recv_sems.at[i], device_id=(x, y, 1 - cc),
                                              device_id_type=MESH)
            cp.start()
            cps.append(cp)
        for i in range(n):
            lh = out_refs[i].shape[0] // 2
            theirs = out_refs[i].at[pl.ds((1 - cc) * lh, lh)]
            pltpu.make_async_remote_copy(src_ref=theirs, dst_ref=theirs, send_sem=send_sems.at[i],
                                         recv_sem=recv_sems.at[i], device_id=(x, y, 1 - cc),
                                         device_id_type=MESH).wait_recv()
        for cp in cps:
            cp.wait_send()

    return pl.pallas_call(
        body, name=name, in_specs=[HBM_SPEC] * n, out_specs=[HBM_SPEC] * n,
        out_shape=[jax.ShapeDtypeStruct(f.shape, f.dtype) for f in fs],
        input_output_aliases={i: i for i in range(n)},
        scratch_shapes=[pltpu.SemaphoreType.DMA((n,)), pltpu.SemaphoreType.DMA((n,))],
    )(*fs)


def _reduce_scatter(gs, kinds, shard_shapes, wire_dtypes):
    x, y, cc = _mesh_pos()
    place = (2 * x + y).reshape(1).astype(jnp.int32)
    cc1 = cc.reshape(1).astype(jnp.int32)
    bufs = _rs_pair(gs, kinds, name="rs_pair")
    ps = [_rs_pair_add(g, buf, kd, cc, wd, name=f"rs_pair_add_{i}")
          for i, (g, buf, kd, wd) in enumerate(zip(gs, bufs, kinds, wire_dtypes))]
    landed = _rs_cross(ps, kinds, shard_shapes, name="rs_cross")
    fs = [_rs_sum(p, b, kd, ab, place, cc1, name=f"rs_sum_{i}")
          for i, (p, b, kd, ab) in enumerate(zip(ps, landed, kinds, shard_shapes))]
    return _rs_share(fs, name="rs_share")


def _shard_shape(axis, r, c):
    return (r // N_CHIPS, c) if axis == 0 else (r, c // N_CHIPS)


def _unstack(stacked):
    ns, depth, r, cs = stacked.shape
    return stacked.transpose(1, 2, 0, 3).reshape(depth, r, ns * cs)


def _stack(fullw):
    depth, r, c = fullw.shape
    return fullw.reshape(depth, r, N_CHIPS, c // N_CHIPS).transpose(2, 0, 1, 3)


REP_SIZES = dict(mix_norm_g=1024, ssd_conv_b=2048, ssd_dt_bias=16, ssd_a_log=16, ssd_d=16, ssd_norm_g=1024,
                 conv_dw_b=1024, conv_ln_g=1024, conv_ln_b=1024, mla_q_a_g=384, mla_kv_a_g=256, mla_q_norm_g=192,
                 mla_k_norm_g=192, xattn_norm_g=1024, mem_norm_g=1024, xattn_q_norm_g=256, xattn_k_norm_g=256,
                 ffn_norm_g=1024)
REP_WIDTH = -(-sum(REP_SIZES.values()) // LANE) * LANE


def _pack_rep(d):
    flat = jnp.concatenate([d[k] for k in REP_NAMES], axis=1)
    return jnp.pad(flat, ((0, 0), (0, REP_WIDTH - flat.shape[1])))[:, None, :]


def _unpack_rep(packed):
    out, off = {}, 0
    for k in REP_NAMES:
        out[k] = packed[:, 0, off:off + REP_SIZES[k]]
        off += REP_SIZES[k]
    return out


WEIGHT_NAMES = ('mix_norm_g', 'w_in', 'ssd_conv_w', 'ssd_conv_b', 'ssd_dt_bias', 'ssd_a_log', 'ssd_d', 'ssd_norm_g',
                'ssd_w_out', 'conv_dw_w', 'conv_dw_b', 'conv_ln_g', 'conv_ln_b', 'conv_w_out', 'mla_q_a_g',
                'mla_w_q_b', 'mla_kv_a_g', 'mla_w_kv_b', 'mla_q_norm_g', 'mla_k_norm_g', 'mla_w_o', 'gate_b', 'w_out',
                'xattn_norm_g', 'mem_norm_g', 'xattn_w_q', 'xattn_w_kv', 'xattn_q_norm_g', 'xattn_k_norm_g',
                'xattn_w_o', 'ffn_norm_g', 'ffn_w_in', 'ffn_w_out')


def kernel(x, mem, positions, *rest):
    nw = len(WEIGHT_NAMES)
    weights = dict(zip(WEIGHT_NAMES, rest[:nw]))
    target = rest[nw]
    mom_m = dict(zip(WEIGHT_NAMES, rest[nw + 1:2 * nw + 1]))
    mom_v = dict(zip(WEIGHT_NAMES, rest[2 * nw + 1:3 * nw + 1]))
    depth = weights['mix_norm_g'].shape[0]

    sharded = BIG + SMALL
    kinds = [_kind(axis, _shard_shape(axis, r, c)[1]) for _, axis, r, c in sharded]
    shard_shapes = [_shard_shape(axis, r, c) for _, axis, r, c in sharded]
    shards = [weights[k].astype(BF16) for k, _, _, _ in BIG] + [weights[k] for k, _, _, _ in SMALL]
    gathered = _all_gather(shards, kinds, name="ag_weights")
    full = {k: (_unstack(w) if kd == STK else w) for (k, _, _, _), kd, w in zip(sharded, kinds, gathered)}
    full['w_in'] = _w_in_pad(full['w_in'])
    full['mla_w_q_b'] = _qb_pad(full['mla_w_q_b'])
    rep = {k: weights[k] for k in REP_NAMES}

    loss, dx, gfull, grep = _local_step(x[0], mem[0], positions[0], target[0], full, rep)
    loss = lax.psum(loss, ("x", "y", "c"))

    gs = [(_stack(gfull[k]) if kd == STK else gfull[k]) for (k, _, _, _), kd in zip(sharded, kinds)]
    wire = [BF16] * len(BIG) + [F32] * (len(SMALL) + 1)
    summed = _reduce_scatter(gs + [_pack_rep(grep)], kinds + [REP], shard_shapes + [(1, REP_WIDTH)], wire)
    grads = {k: g for (k, _, _, _), g in zip(sharded, summed[:-1])}
    rep_sum = summed[-1]
    grads.update(_unpack_rep(rep_sum))

    delta, new_m, new_v = {}, {}, {}
    for k, _, _, _ in sharded:
        w = weights[k]
        two_d = (w.shape[0] * w.shape[1], w.shape[2])
        d_, m_, v_ = _adamw(w.reshape(two_d), grads[k].reshape(two_d), mom_m[k].reshape(two_d),
                            mom_v[k].reshape(two_d), name="adamw_" + k)
        delta[k], new_m[k], new_v[k] = d_.reshape(w.shape), m_.reshape(w.shape), v_.reshape(w.shape)
    pack2 = lambda d: _pack_rep(d)[:, 0, :]
    d_, m_, v_ = _adamw(pack2(rep), rep_sum[:, 0, :], pack2({k: mom_m[k] for k in REP_NAMES}),
                        pack2({k: mom_v[k] for k in REP_NAMES}), name="adamw_rep")
    delta.update(_unpack_rep(d_[:, None, :]))
    new_m.update(_unpack_rep(m_[:, None, :]))
    new_v.update(_unpack_rep(v_[:, None, :]))

    return (loss, dx[None], *[grads[k] for k in WEIGHT_NAMES], *[delta[k] for k in WEIGHT_NAMES],
            *[new_m[k] for k in WEIGHT_NAMES], *[new_v[k] for k in WEIGHT_NAMES])
```

```python
import functools
import math

import jax
import jax.numpy as jnp
import numpy as np
from jax import lax
from jax.experimental import pallas as pl
from jax.experimental.pallas import tpu as pltpu

F32 = jnp.float32
BF16 = jnp.bfloat16
MESH = pl.DeviceIdType.MESH

EPS = 1e-6
CHUNK = 64
SSD_HEADS = 16
SSD_GROUPS = 4
SSD_P = 64
SSD_N = 128
MLA_HEADS = 8
MLA_NOPE = 128
MLA_ROPE = 64
MLA_V = 128
MLA_HP = 256
X_HEADS = 4
ROPE_THETA = 10000.0
ADAM_LR, ADAM_B1, ADAM_B2, ADAM_EPS, ADAM_WD, ADAM_STEP = 0.001, 0.9, 0.999, 1e-08, 0.01, 10
LANE = 128
NEG = -1e30
VMEM_MB = 1024 * 1024


def _pick(n, cap, mult=128):
    if n <= cap:
        return n
    d = (cap // mult) * mult
    while d >= mult:
        if n % d == 0:
            return d
        d -= mult
    return n


def _cparams(sem, mb=40):
    return pltpu.CompilerParams(dimension_semantics=sem, vmem_limit_bytes=mb * VMEM_MB)


def _sigmoid(x):
    return 1.0 / (1.0 + jnp.exp(-x))


def _silu(x):
    return x * _sigmoid(x)


def _dsilu(x):
    s = _sigmoid(x)
    return s * (1.0 + x * (1.0 - s))


def _softplus(x):
    return jnp.maximum(x, 0.0) + jnp.log(1.0 + jnp.exp(-jnp.abs(x)))


def _matmul(a, b, *, ta=False, tb=False, out_dtype=F32, add=None, into=None, name):
    if ta:
        kdim, m = a.shape
    else:
        m, kdim = a.shape
    if tb:
        n, kb = b.shape
    else:
        kb, n = b.shape
    assert kb == kdim, (a.shape, b.shape, ta, tb)
    tm = _pick(m, 512, 128) if ta else _pick(m, 1024, 8)
    tn = _pick(n, 1024 if n <= 1024 else 512, 128)
    tk = _pick(kdim, 2048, 128)
    nk = kdim // tk
    dims = (((0 if ta else 1,), (1 if tb else 0,)), ((), ()))

    has_add = add is not None
    has_stack = into is not None and into[0] is not None

    def body(a_ref, b_ref, *rest):
        add_ref = rest[0] if has_add else None
        o_ref = rest[has_add + has_stack]
        acc = rest[has_add + has_stack + 1:]
        part = lax.dot_general(a_ref[...].astype(BF16), b_ref[...].astype(BF16), dims,
                               preferred_element_type=F32)

        def finish(total):
            if has_add:
                total = total + add_ref[...]
            o_ref[...] = total.astype(o_ref.dtype)

        if nk == 1:
            finish(part)
        else:
            acc_ref, = acc
            k = pl.program_id(2)

            @pl.when(k == 0)
            def _():
                acc_ref[...] = part

            @pl.when(k > 0)
            def _():
                acc_ref[...] += part

            @pl.when(k == nk - 1)
            def _():
                finish(acc_ref[...])

    a_spec = pl.BlockSpec((tk, tm), lambda i, j, k: (k, i)) if ta else pl.BlockSpec((tm, tk), lambda i, j, k: (i, k))
    b_spec = pl.BlockSpec((tn, tk), lambda i, j, k: (j, k)) if tb else pl.BlockSpec((tk, tn), lambda i, j, k: (k, j))
    o_spec = pl.BlockSpec((tm, tn), lambda i, j, k: (i, j))
    operands = [a, b] + ([add] if has_add else [])
    in_specs = [a_spec, b_spec] + ([o_spec] if has_add else [])
    if into is None:
        out_spec, out_shape, aliases = o_spec, jax.ShapeDtypeStruct((m, n), out_dtype), {}
    else:
        stack, layer, depth = into
        out_spec = pl.BlockSpec((None, tm, tn), lambda i, j, k: (layer, i, j))
        out_shape = jax.ShapeDtypeStruct((depth, m, n), out_dtype)
        aliases = {}
        if stack is not None:
            aliases = {len(operands): 0}
            operands.append(stack)
            in_specs.append(HBM_SPEC)
    return pl.pallas_call(
        body, name=name, grid=(m // tm, n // tn, nk),
        in_specs=in_specs, out_specs=out_spec, out_shape=out_shape, input_output_aliases=aliases,
        scratch_shapes=[] if nk == 1 else [pltpu.VMEM((tm, tn), F32)],
        compiler_params=_cparams(("parallel", "parallel", "arbitrary"), 48),
    )(*operands)


def _rms_fwd(x, g, *, out_dtype, name):
    r, w = x.shape
    tr = _pick(r, 512, 8)

    def body(x_ref, g_ref, o_ref):
        xv = x_ref[...]
        rstd = lax.rsqrt(jnp.mean(xv * xv, axis=-1, keepdims=True) + EPS)
        o_ref[...] = (xv * rstd * g_ref[...]).astype(o_ref.dtype)

    return pl.pallas_call(
        body, name=name, grid=(r // tr,),
        in_specs=[pl.BlockSpec((tr, w), lambda i: (i, 0)), pl.BlockSpec((1, w), lambda i: (0, 0))],
        out_specs=pl.BlockSpec((tr, w), lambda i: (i, 0)),
        out_shape=jax.ShapeDtypeStruct((r, w), out_dtype),
        compiler_params=_cparams(("parallel",)),
    )(x, g)


def _rms_bwd(x, g, dy, *, dx_dtype, name, add=None):
    r, w = x.shape
    tr = _pick(r, 512, 8)
    has_add = add is not None

    def body(x_ref, g_ref, dy_ref, *rest):
        if has_add:
            add_ref, dx_ref, dg_ref = rest
        else:
            dx_ref, dg_ref = rest
        xv = x_ref[...]
        dyv = dy_ref[...].astype(F32)
        rstd = lax.rsqrt(jnp.mean(xv * xv, axis=-1, keepdims=True) + EPS)
        xh = xv * rstd
        dyg = dyv * g_ref[...]
        dx = rstd * (dyg - xh * jnp.mean(dyg * xh, axis=-1, keepdims=True))
        if has_add:
            dx = dx + add_ref[...]
        dx_ref[...] = dx.astype(dx_ref.dtype)
        part = jnp.sum(dyv * xh, axis=0, keepdims=True)

        @pl.when(pl.program_id(0) == 0)
        def _():
            dg_ref[...] = part

        @pl.when(pl.program_id(0) > 0)
        def _():
            dg_ref[...] += part

    row = pl.BlockSpec((tr, w), lambda i: (i, 0))
    vec = pl.BlockSpec((1, w), lambda i: (0, 0))
    ins = [x, g, dy] + ([add] if has_add else [])
    return pl.pallas_call(
        body, name=name, grid=(r // tr,),
        in_specs=[row, vec, row] + ([row] if has_add else []),
        out_specs=[row, vec],
        out_shape=[jax.ShapeDtypeStruct((r, w), dx_dtype), jax.ShapeDtypeStruct((1, w), F32)],
        compiler_params=_cparams(("arbitrary",)),
    )(*ins)


def _ln_silu_fwd(x, g, b, *, name):
    r, w = x.shape
    tr = _pick(r, 512, 8)

    def body(x_ref, g_ref, b_ref, o_ref):
        xv = x_ref[...]
        mu = jnp.mean(xv, axis=-1, keepdims=True)
        xc = xv - mu
        rstd = lax.rsqrt(jnp.mean(xc * xc, axis=-1, keepdims=True) + EPS)
        o_ref[...] = _silu(xc * rstd * g_ref[...] + b_ref[...]).astype(o_ref.dtype)

    row = pl.BlockSpec((tr, w), lambda i: (i, 0))
    vec = pl.BlockSpec((1, w), lambda i: (0, 0))
    return pl.pallas_call(
        body, name=name, grid=(r // tr,), in_specs=[row, vec, vec], out_specs=row,
        out_shape=jax.ShapeDtypeStruct((r, w), BF16), compiler_params=_cparams(("parallel",)),
    )(x, g, b)


def _ln_silu_bwd(x, g, b, dy, *, name):
    r, w = x.shape
    tr = _pick(r, 512, 8)

    def body(x_ref, g_ref, b_ref, dy_ref, dx_ref, dg_ref, db_ref):
        xv = x_ref[...]
        mu = jnp.mean(xv, axis=-1, keepdims=True)
        xc = xv - mu
        rstd = lax.rsqrt(jnp.mean(xc * xc, axis=-1, keepdims=True) + EPS)
        xh = xc * rstd
        pre = xh * g_ref[...] + b_ref[...]
        dpre = dy_ref[...].astype(F32) * _dsilu(pre)
        dxh = dpre * g_ref[...]
        dx_ref[...] = rstd * (dxh - jnp.mean(dxh, axis=-1, keepdims=True)
                              - xh * jnp.mean(dxh * xh, axis=-1, keepdims=True))
        pg = jnp.sum(dpre * xh, axis=0, keepdims=True)
        pb = jnp.sum(dpre, axis=0, keepdims=True)

        @pl.when(pl.program_id(0) == 0)
        def _():
            dg_ref[...] = pg
            db_ref[...] = pb

        @pl.when(pl.program_id(0) > 0)
        def _():
            dg_ref[...] += pg
            db_ref[...] += pb

    row = pl.BlockSpec((tr, w), lambda i: (i, 0))
    vec = pl.BlockSpec((1, w), lambda i: (0, 0))
    return pl.pallas_call(
        body, name=name, grid=(r // tr,), in_specs=[row, vec, vec, row], out_specs=[row, vec, vec],
        out_shape=[jax.ShapeDtypeStruct((r, w), F32), jax.ShapeDtypeStruct((1, w), F32),
                   jax.ShapeDtypeStruct((1, w), F32)],
        compiler_params=_cparams(("arbitrary",)),
    )(x, g, b, dy)


CONV_PAD = 32
CONV_T = 256


def _conv_fwd(src, w, b, *, glu, act, name):
    s = src.shape[0]
    k, c = w.shape
    tc = LANE
    ncb = c // tc
    tt = _pick(s, CONV_T, 8)
    assert k - 1 <= CONV_PAD

    def body(*refs):
        if glu:
            a_ref, g_ref, w_ref, b_ref = refs[:4]
            outs = refs[4:-1]
        else:
            a_ref, w_ref, b_ref = refs[:3]
            outs = refs[3:-1]
        xp = refs[-1]
        xp[0:CONV_PAD, :] = jnp.zeros((CONV_PAD, tc), F32)
        if glu:
            xp[CONV_PAD:CONV_PAD + s, :] = a_ref[...] * _sigmoid(g_ref[...])
        else:
            xp[CONV_PAD:CONV_PAD + s, :] = a_ref[...]
        wv = w_ref[...]
        bv = b_ref[...]
        for t0 in range(0, s, tt):
            acc = jnp.zeros((tt, tc), F32) + bv
            for kk in range(k):
                off = CONV_PAD + t0 - (k - 1) + kk
                acc = acc + wv[kk:kk + 1, :] * xp[off:off + tt, :]
            outs[0][t0:t0 + tt, :] = acc
            if act:
                outs[1][t0:t0 + tt, :] = _silu(acc)

    col = pl.BlockSpec((s, tc), lambda j: (0, j))
    in_specs = [col, pl.BlockSpec((s, tc), lambda j: (0, j + ncb))] if glu else [col]
    in_specs += [pl.BlockSpec((k, tc), lambda j: (0, j)), pl.BlockSpec((1, tc), lambda j: (0, j))]
    n_out = 2 if act else 1
    res = pl.pallas_call(
        body, name=name, grid=(ncb,), in_specs=in_specs,
        out_specs=[col] * n_out,
        out_shape=[jax.ShapeDtypeStruct((s, c), F32)] * n_out,
        scratch_shapes=[pltpu.VMEM((CONV_PAD + s, tc), F32)],
        compiler_params=_cparams(("parallel",), 48),
    )(*([src, src] if glu else [src]), w, b)
    return res


def _conv_bwd(src, w, b, dy, pre, *, glu, act, name):
    s = src.shape[0]
    k, c = w.shape
    tc = LANE
    ncb = c // tc
    tt = _pick(s, CONV_T, 8)

    def body(*refs):
        i = 0
        a_ref = refs[i]; i += 1
        if glu:
            g_ref = refs[i]; i += 1
        w_ref = refs[i]; i += 1
        dy_ref = refs[i]; i += 1
        if act:
            pre_ref = refs[i]; i += 1
        da_ref = refs[i]; i += 1
        if glu:
            dg_ref = refs[i]; i += 1
        dw_ref = refs[i]; db_ref = refs[i + 1]
        xp, dp = refs[-2], refs[-1]
        xp[0:CONV_PAD, :] = jnp.zeros((CONV_PAD, tc), F32)
        if glu:
            xp[CONV_PAD:CONV_PAD + s, :] = a_ref[...] * _sigmoid(g_ref[...])
        else:
            xp[CONV_PAD:CONV_PAD + s, :] = a_ref[...]
        dp[s:s + CONV_PAD, :] = jnp.zeros((CONV_PAD, tc), F32)
        if act:
            dp[0:s, :] = dy_ref[...].astype(F32) * _dsilu(pre_ref[...])
        else:
            dp[0:s, :] = dy_ref[...].astype(F32)
        wv = w_ref[...]
        dws = [jnp.zeros((1, tc), F32) for _ in range(k)]
        dbs = jnp.zeros((1, tc), F32)
        for t0 in range(0, s, tt):
            acc = jnp.zeros((tt, tc), F32)
            dcur = dp[t0:t0 + tt, :]
            dbs = dbs + jnp.sum(dcur, axis=0, keepdims=True)
            for kk in range(k):
                acc = acc + wv[kk:kk + 1, :] * dp[t0 + (k - 1) - kk:t0 + (k - 1) - kk + tt, :]
                off = CONV_PAD + t0 - (k - 1) + kk
                dws[kk] = dws[kk] + jnp.sum(dcur * xp[off:off + tt, :], axis=0, keepdims=True)
            if glu:
                av = a_ref[t0:t0 + tt, :]
                sg = _sigmoid(g_ref[t0:t0 + tt, :])
                da_ref[t0:t0 + tt, :] = (acc * sg).astype(da_ref.dtype)
                dg_ref[t0:t0 + tt, :] = (acc * av * sg * (1.0 - sg)).astype(dg_ref.dtype)
            else:
                da_ref[t0:t0 + tt, :] = acc.astype(da_ref.dtype)
        for kk in range(k):
            dw_ref[kk:kk + 1, :] = dws[kk]
        db_ref[...] = dbs

    col = pl.BlockSpec((s, tc), lambda j: (0, j))
    in_specs = [col] + ([pl.BlockSpec((s, tc), lambda j: (0, j + ncb))] if glu else [])
    in_specs += [pl.BlockSpec((k, tc), lambda j: (0, j)), col] + ([col] if act else [])
    ins = ([src, src] if glu else [src]) + [w, dy] + ([pre] if act else [])
    out_specs = [col] + ([col] if glu else []) + [pl.BlockSpec((k, tc), lambda j: (0, j)),
                                                  pl.BlockSpec((1, tc), lambda j: (0, j))]
    out_shape = [jax.ShapeDtypeStruct((s, c), BF16)] * (2 if glu else 1) + [
        jax.ShapeDtypeStruct((k, c), F32), jax.ShapeDtypeStruct((1, c), F32)]
    return pl.pallas_call(
        body, name=name, grid=(ncb,), in_specs=in_specs, out_specs=out_specs, out_shape=out_shape,
        scratch_shapes=[pltpu.VMEM((CONV_PAD + s, tc), F32), pltpu.VMEM((s + CONV_PAD, tc), F32)],
        compiler_params=_cparams(("parallel",), 56),
    )(*ins)


def _ssd_consts():
    e = np.zeros((LANE, SSD_HEADS * SSD_P), np.float32)
    for h in range(SSD_HEADS):
        e[h, h * SSD_P:(h + 1) * SSD_P] = 1.0
    ltri = np.tril(np.ones((CHUNK, CHUNK), np.float32))
    return jnp.asarray(e), jnp.asarray(e.T.copy()), jnp.asarray(ltri), jnp.asarray(ltri.T.copy())


def _split3(x):
    hi = x.astype(BF16)
    r = x - hi.astype(F32)
    mid = r.astype(BF16)
    lo = (r - mid.astype(F32)).astype(BF16)
    return hi, mid, lo


def _dot_sel(x, sel, x_left=True):
    sb = sel.astype(BF16)
    out = None
    for part in _split3(x):
        t = jnp.dot(part, sb, preferred_element_type=F32) if x_left else jnp.dot(sb, part, preferred_element_type=F32)
        out = t if out is None else out + t
    return out


def _ssd_chunk_terms(dtr_ref, dtrt_ref, bias_ref, biast_ref, alog_ref, alogt_ref, e_ref, ltri_ref, utri_ref):
    a_neg = -jnp.exp(alog_ref[...])
    dt = _softplus(dtr_ref[...] + bias_ref[...])
    a = dt * a_neg
    s = _dot_sel(a, ltri_ref[...], x_left=False)
    dtt = _softplus(dtrt_ref[...] + biast_ref[...])
    st = _dot_sel(dtt * (-jnp.exp(alogt_ref[...])), utri_ref[...])
    ev = e_ref[...]
    s_x = _dot_sel(s, ev)
    dt_x = _dot_sel(dt, ev)
    return a_neg, dt, s, st, s_x, dt_x


def _ssd_decay(s, st, h, tril):
    seg = s[:, h:h + 1] - st[h:h + 1, :]
    return jnp.exp(jnp.where(tril, seg, NEG))


def _ssd_decay_t(s, st, h, triu):
    seg = st[h:h + 1, :] - s[:, h:h + 1]
    return jnp.exp(jnp.where(triu, seg, NEG))


def _head_masks():
    lane = lax.broadcasted_iota(jnp.int32, (1, SSD_P * 4), 1)
    return [((lane >= r * SSD_P) & (lane < (r + 1) * SSD_P)).astype(F32) for r in range(4)]


def _ssd_scan_fwd(xbc, dtr, dtrt, bias, biast, alog, alogt, dskip, *, name):
    s_len = xbc.shape[0]
    nc = s_len // CHUNK
    e, et, ltri, utri = _ssd_consts()
    gw = SSD_P * 4

    def body(xs_ref, b_ref, c_ref, dtr_ref, dtrt_ref, bias_ref, biast_ref, alog_ref, alogt_ref, d_ref,
             e_ref, ltri_ref, utri_ref, y_ref, prev_ref, state):
        @pl.when(pl.program_id(0) == 0)
        def _():
            state[...] = jnp.zeros_like(state)

        a_neg, dt, s, st, s_x, dt_x = _ssd_chunk_terms(dtr_ref, dtrt_ref, bias_ref, biast_ref, alog_ref,
                                                       alogt_ref, e_ref, ltri_ref, utri_ref)
        s_last = s_x[CHUNK - 1:CHUNK, :]
        es_x = jnp.exp(s_x)
        w_x = jnp.exp(s_last - s_x)
        cd_x = jnp.exp(s_last)
        d_x = _dot_sel(jnp.broadcast_to(d_ref[...], (8, LANE)), e_ref[...])[0:1, :]
        xs = xs_ref[...]
        xv = xs * dt_x
        row = lax.broadcasted_iota(jnp.int32, (CHUNK, CHUNK), 0)
        colm = lax.broadcasted_iota(jnp.int32, (CHUNK, CHUNK), 1)
        tril = colm <= row
        masks = _head_masks()
        for g in range(SSD_GROUPS):
            gs = slice(g * gw, (g + 1) * gw)
            bg = b_ref[:, g * SSD_N:(g + 1) * SSD_N].astype(BF16)
            cg = c_ref[:, g * SSD_N:(g + 1) * SSD_N].astype(BF16)
            xg = xv[:, gs]
            hg = state[g]
            prev_ref[g] = hg
            cb = lax.dot_general(cg, bg, (((1,), (1,)), ((), ())), preferred_element_type=F32)
            yg = jnp.dot(cg, hg.astype(BF16), preferred_element_type=F32) * es_x[:, gs]
            for r in range(4):
                m = (cb * _ssd_decay(s, st, g * 4 + r, tril)).astype(BF16)
                yg = yg + jnp.dot(m, (xg * masks[r]).astype(BF16), preferred_element_type=F32)
            y_ref[:, gs] = yg + d_x[:, gs] * xs[:, gs]
            upd = lax.dot_general(bg, (xg * w_x[:, gs]).astype(BF16), (((0,), (0,)), ((), ())),
                                  preferred_element_type=F32)
            state[g] = hg * cd_x[:, gs] + upd

    nh = LANE
    chunk_row = lambda w, cb: pl.BlockSpec((CHUNK, w), lambda i, cb=cb: (i, cb))
    full = lambda a: pl.BlockSpec(a.shape, lambda i: (0,) * a.ndim)
    in_specs = [chunk_row(1024, 0), chunk_row(512, 2), chunk_row(512, 3), chunk_row(nh, 0),
                pl.BlockSpec((None, nh, CHUNK), lambda i: (i, 0, 0)),
                full(bias), full(biast), full(alog), full(alogt), full(dskip), full(e), full(ltri), full(utri)]
    return pl.pallas_call(
        body, name=name, grid=(nc,), in_specs=in_specs,
        out_specs=[pl.BlockSpec((CHUNK, 1024), lambda i: (i, 0)),
                   pl.BlockSpec((None, SSD_GROUPS, SSD_N, gw), lambda i: (i, 0, 0, 0))],
        out_shape=[jax.ShapeDtypeStruct((s_len, 1024), F32),
                   jax.ShapeDtypeStruct((nc, SSD_GROUPS, SSD_N, gw), F32)],
        scratch_shapes=[pltpu.VMEM((SSD_GROUPS, SSD_N, gw), F32)],
        compiler_params=_cparams(("arbitrary",)),
    )(xbc, xbc, xbc, dtr, dtrt, bias, biast, alog, alogt, dskip, e, ltri, utri)


def _ssd_scan_bwd(xbc, dtr, dtrt, bias, biast, alog, alogt, dskip, prev, dy, *, name):
    s_len = xbc.shape[0]
    nc = s_len // CHUNK
    e, et, ltri, utri = _ssd_consts()
    gw = SSD_P * 4

    def body(xs_ref, b_ref, c_ref, dtr_ref, dtrt_ref, bias_ref, biast_ref, alog_ref, alogt_ref, d_ref,
             e_ref, et_ref, ltri_ref, utri_ref, prev_ref, dy_ref,
             dxs_ref, db_ref, dc_ref, ddtr_ref, dalog_ref, dbias_ref, dd_ref, dstate):
        step = pl.program_id(0)

        @pl.when(step == 0)
        def _():
            dstate[...] = jnp.zeros_like(dstate)

        a_neg, dt, s, st, s_x, dt_x = _ssd_chunk_terms(dtr_ref, dtrt_ref, bias_ref, biast_ref, alog_ref,
                                                       alogt_ref, e_ref, ltri_ref, utri_ref)
        s_last = s_x[CHUNK - 1:CHUNK, :]
        es_x = jnp.exp(s_x)
        w_x = jnp.exp(s_last - s_x)
        cd_x = jnp.exp(s_last)
        d_x = _dot_sel(jnp.broadcast_to(d_ref[...], (8, LANE)), e_ref[...])[0:1, :]
        xs = xs_ref[...]
        xv = xs * dt_x
        dyv = dy_ref[...]
        row = lax.broadcasted_iota(jnp.int32, (CHUNK, CHUNK), 0)
        colm = lax.broadcasted_iota(jnp.int32, (CHUNK, CHUNK), 1)
        tril = colm <= row
        masks = _head_masks()
        is_last = lax.broadcasted_iota(jnp.int32, (CHUNK, 1), 0) == CHUNK - 1
        nt = (((1,), (1,)), ((), ()))
        tn = (((0,), (0,)), ((), ()))
        ds_parts, ddt_parts = [], []
        head_lane = lax.broadcasted_iota(jnp.int32, (CHUNK, LANE), 1)
        triu = colm >= row
        ds_diag = jnp.zeros((CHUNK, LANE), F32)
        for g in range(SSD_GROUPS):
            gs = slice(g * gw, (g + 1) * gw)
            bg = b_ref[:, g * SSD_N:(g + 1) * SSD_N].astype(BF16)
            cg = c_ref[:, g * SSD_N:(g + 1) * SSD_N].astype(BF16)
            xg = xv[:, gs]
            xgb = xg.astype(BF16)
            hg = prev_ref[g]
            hgb = hg.astype(BF16)
            dsg = dstate[g]
            dsgb = dsg.astype(BF16)
            dyg = dyv[:, gs]
            dye = (dyg * es_x[:, gs]).astype(BF16)
            xw = (xg * w_x[:, gs]).astype(BF16)
            cb = lax.dot_general(cg, bg, nt, preferred_element_type=F32)
            cbt = lax.dot_general(bg, cg, nt, preferred_element_type=F32)
            dcg = lax.dot_general(dye, hgb, nt, preferred_element_type=F32)
            dh = lax.dot_general(cg, dye, tn, preferred_element_type=F32)
            bds = jnp.dot(bg, dsgb, preferred_element_type=F32)
            yoff = es_x[:, gs] * jnp.dot(cg, hgb, preferred_element_type=F32)
            dx_state = w_x[:, gs] * bds
            dbg = lax.dot_general(xw, dsgb, nt, preferred_element_type=F32)
            dxd = jnp.zeros((CHUNK, gw), F32)
            dcb = jnp.zeros((CHUNK, CHUNK), F32)
            for r in range(4):
                dec = _ssd_decay(s, st, g * 4 + r, tril)
                mf = cb * dec
                m = mf.astype(BF16)
                dym = (dyg * masks[r]).astype(BF16)
                dm = lax.dot_general(dym, xgb, nt, preferred_element_type=F32)
                dxd = dxd + lax.dot_general(m, dym, tn, preferred_element_type=F32)
                dcb = dcb + dm * dec
                dmt = lax.dot_general(xgb, dym, nt, preferred_element_type=F32)
                rc = (jnp.sum(dm * mf, axis=1, keepdims=True)
                      - jnp.sum(dmt * cbt * _ssd_decay_t(s, st, g * 4 + r, triu), axis=1, keepdims=True))
                ds_diag = ds_diag + jnp.where(head_lane == g * 4 + r, rc, 0.0)
            dcbb = dcb.astype(BF16)
            dcg = dcg + jnp.dot(dcbb, bg, preferred_element_type=F32)
            dbg = dbg + lax.dot_general(dcbb, cg, tn, preferred_element_type=F32)
            dxg = dxd + dx_state
            extra = (jnp.sum(xg * dx_state, axis=0, keepdims=True)
                     + cd_x[:, gs] * jnp.sum(dsg * hg, axis=0, keepdims=True))
            ds_parts.append(dyg * yoff - xg * dx_state + jnp.where(is_last, extra, 0.0))
            ddt_parts.append(dxg * xs[:, gs])
            dxs_ref[:, gs] = dxg * dt_x[:, gs] + d_x[:, gs] * dyg
            db_ref[:, g * SSD_N:(g + 1) * SSD_N] = dbg
            dc_ref[:, g * SSD_N:(g + 1) * SSD_N] = dcg
            dstate[g] = cd_x[:, gs] * dsg + dh
        etv = et_ref[...]
        ds = ds_diag + _dot_sel(jnp.concatenate(ds_parts, axis=1), etv)
        da = _dot_sel(ds, utri_ref[...], x_left=False)
        ddt = da * a_neg + _dot_sel(jnp.concatenate(ddt_parts, axis=1), etv)
        ddtr = ddt * _sigmoid(dtr_ref[...] + bias_ref[...])
        ddtr_ref[...] = ddtr.astype(ddtr_ref.dtype)
        p_alog = jnp.sum(da * dt, axis=0, keepdims=True) * a_neg
        p_bias = jnp.sum(ddtr, axis=0, keepdims=True)
        p_d = _dot_sel(jnp.broadcast_to(jnp.sum(dyv * xs, axis=0, keepdims=True), (8, SSD_HEADS * SSD_P)),
                       etv)[0:1, :]

        @pl.when(step == 0)
        def _():
            dalog_ref[...] = p_alog
            dbias_ref[...] = p_bias
            dd_ref[...] = p_d

        @pl.when(step > 0)
        def _():
            dalog_ref[...] += p_alog
            dbias_ref[...] += p_bias
            dd_ref[...] += p_d

    nh = LANE
    rev = lambda i: nc - 1 - i
    chunk_row = lambda w, cb: pl.BlockSpec((CHUNK, w), lambda i, cb=cb: (rev(i), cb))
    full = lambda a: pl.BlockSpec(a.shape, lambda i: (0,) * a.ndim)
    vec = pl.BlockSpec((1, nh), lambda i: (0, 0))
    in_specs = [chunk_row(1024, 0), chunk_row(512, 2), chunk_row(512, 3), chunk_row(nh, 0),
                pl.BlockSpec((None, nh, CHUNK), lambda i: (rev(i), 0, 0)),
                full(bias), full(biast), full(alog), full(alogt), full(dskip), full(e), full(et), full(ltri),
                full(utri),
                pl.BlockSpec((None, SSD_GROUPS, SSD_N, gw), lambda i: (rev(i), 0, 0, 0)), chunk_row(1024, 0)]
    return pl.pallas_call(
        body, name=name, grid=(nc,), in_specs=in_specs,
        out_specs=[chunk_row(1024, 0), chunk_row(512, 0), chunk_row(512, 0), chunk_row(nh, 0), vec, vec, vec],
        out_shape=[jax.ShapeDtypeStruct((s_len, 1024), F32), jax.ShapeDtypeStruct((s_len, 512), F32),
                   jax.ShapeDtypeStruct((s_len, 512), F32), jax.ShapeDtypeStruct((s_len, nh), BF16),
                   jax.ShapeDtypeStruct((1, nh), F32), jax.ShapeDtypeStruct((1, nh), F32),
                   jax.ShapeDtypeStruct((1, nh), F32)],
        scratch_shapes=[pltpu.VMEM((SSD_GROUPS, SSD_N, gw), F32)],
        compiler_params=_cparams(("arbitrary",)),
    )(xbc, xbc, xbc, dtr, dtrt, bias, biast, alog, alogt, dskip, e, et, ltri, utri, prev, dy)


def _ssd_gate_fwd(y, z, g, *, name):
    r, w = y.shape
    tr = _pick(r, 512, 8)
    gw = w // SSD_GROUPS

    def body(y_ref, z_ref, g_ref, o_ref):
        for k in range(SSD_GROUPS):
            cs = slice(k * gw, (k + 1) * gw)
            t = y_ref[:, cs] * _silu(z_ref[:, cs])
            rstd = lax.rsqrt(jnp.mean(t * t, axis=-1, keepdims=True) + EPS)
            o_ref[:, cs] = (t * rstd * g_ref[:, cs]).astype(o_ref.dtype)

    row = pl.BlockSpec((tr, w), lambda i: (i, 0))
    vec = pl.BlockSpec((1, w), lambda i: (0, 0))
    return pl.pallas_call(
        body, name=name, grid=(r // tr,), in_specs=[row, row, vec], out_specs=row,
        out_shape=jax.ShapeDtypeStruct((r, w), BF16), compiler_params=_cparams(("parallel",)),
    )(y, z, g)


def _ssd_gate_bwd(y, z, g, do, *, name):
    r, w = y.shape
    tr = _pick(r, 512, 8)
    gw = w // SSD_GROUPS

    def body(y_ref, z_ref, g_ref, do_ref, dy_ref, dz_ref, dg_ref):
        parts = []
        for k in range(SSD_GROUPS):
            cs = slice(k * gw, (k + 1) * gw)
            yv = y_ref[:, cs]
            zv = z_ref[:, cs]
            sz = _silu(zv)
            t = yv * sz
            rstd = lax.rsqrt(jnp.mean(t * t, axis=-1, keepdims=True) + EPS)
            th = t * rstd
            dov = do_ref[:, cs].astype(F32)
            dog = dov * g_ref[:, cs]
            dt = rstd * (dog - th * jnp.mean(dog * th, axis=-1, keepdims=True))
            dy_ref[:, cs] = dt * sz
            dz_ref[:, cs] = (dt * yv * _dsilu(zv)).astype(dz_ref.dtype)
            parts.append(jnp.sum(dov * th, axis=0, keepdims=True))
        pg = jnp.concatenate(parts, axis=1)

        @pl.when(pl.program_id(0) == 0)
        def _():
            dg_ref[...] = pg

        @pl.when(pl.program_id(0) > 0)
        def _():
            dg_ref[...] += pg

    row = pl.BlockSpec((tr, w), lambda i: (i, 0))
    vec = pl.BlockSpec((1, w), lambda i: (0, 0))
    return pl.pallas_call(
        body, name=name, grid=(r // tr,), in_specs=[row, row, vec, row], out_specs=[row, row, vec],
        out_shape=[jax.ShapeDtypeStruct((r, w), F32), jax.ShapeDtypeStruct((r, w), BF16),
                   jax.ShapeDtypeStruct((1, w), F32)],
        compiler_params=_cparams(("arbitrary",)),
    )(y, z, g, do)


def _rope_swap(x):
    lane = lax.broadcasted_iota(jnp.int32, x.shape, 1)
    lo = pltpu.roll(x, 96, 1)
    hi = pltpu.roll(x, 32, 1)
    return jnp.where(lane < 32, lo, jnp.where(lane < 64, hi, 0.0))


def _norm_part(v, g, n):
    rstd = lax.rsqrt(jnp.sum(v * v, axis=-1, keepdims=True) * (1.0 / n) + EPS)
    xh = v * rstd
    return xh * g, xh, rstd


def _norm_part_bwd(dout, g, xh, rstd, n):
    dg = dout * g
    return rstd * (dg - xh * (jnp.sum(dg * xh, axis=-1, keepdims=True) * (1.0 / n)))


def _mla_prep_fwd(q, kv, krr, cs, sn, gq, gk, *, name):
    s = q.shape[0]
    tr = _pick(s, 256, 8)
    hp = MLA_HP

    def body(q_ref, kv_ref, krr_ref, cs_ref, sn_ref, gq_ref, gk_ref, qf_ref, kf_ref, v_ref):
        csv, snv = cs_ref[...], sn_ref[...]
        gqn, gqr = gq_ref[:, 0:128], gq_ref[:, 128:256]
        gkn, gkr = gk_ref[:, 0:128], gk_ref[:, 128:256]
        kr, _, _ = _norm_part(krr_ref[...], gkr, MLA_ROPE)
        kr = (kr * csv + _rope_swap(kr) * snv).astype(BF16)
        for h in range(MLA_HEADS):
            qn, _, _ = _norm_part(q_ref[:, h * hp:h * hp + 128], gqn, MLA_NOPE)
            qr, _, _ = _norm_part(q_ref[:, h * hp + 128:(h + 1) * hp], gqr, MLA_ROPE)
            qr = qr * csv + _rope_swap(qr) * snv
            qf_ref[h, :, 0:128] = qn.astype(BF16)
            qf_ref[h, :, 128:256] = qr.astype(BF16)
            kn, _, _ = _norm_part(kv_ref[:, h * hp:h * hp + 128], gkn, MLA_NOPE)
            kf_ref[h, :, 0:128] = kn.astype(BF16)
            kf_ref[h, :, 128:256] = kr
            v_ref[h] = kv_ref[:, h * hp + 128:(h + 1) * hp].astype(BF16)

    row = lambda w: pl.BlockSpec((tr, w), lambda i: (i, 0))
    vec = pl.BlockSpec((1, hp), lambda i: (0, 0))
    hrow = lambda w: pl.BlockSpec((MLA_HEADS, tr, w), lambda i: (0, i, 0))
    return pl.pallas_call(
        body, name=name, grid=(s // tr,),
        in_specs=[row(MLA_HEADS * hp), row(MLA_HEADS * hp), row(128), row(128), row(128), vec, vec],
        out_specs=[hrow(hp), hrow(hp), hrow(128)],
        out_shape=[jax.ShapeDtypeStruct((MLA_HEADS, s, hp), BF16), jax.ShapeDtypeStruct((MLA_HEADS, s, hp), BF16),
                   jax.ShapeDtypeStruct((MLA_HEADS, s, 128), BF16)],
        compiler_params=_cparams(("parallel",)),
    )(q, kv, krr, cs, sn, gq, gk)


def _mla_prep_bwd(q, kv, krr, cs, sn, gq, gk, dqf, dkf, dv, *, name):
    s = q.shape[0]
    tr = _pick(s, 256, 8)
    hp = MLA_HP

    def body(q_ref, kv_ref, krr_ref, cs_ref, sn_ref, gq_ref, gk_ref, dqf_ref, dkf_ref, dv_ref,
             dq_ref, dkv_ref, dkrr_ref, dgq_ref, dgk_ref):
        csv, snv = cs_ref[...], sn_ref[...]
        gqn, gqr = gq_ref[:, 0:128], gq_ref[:, 128:256]
        gkn, gkr = gk_ref[:, 0:128], gk_ref[:, 128:256]
        _, krh, krs = _norm_part(krr_ref[...], gkr, MLA_ROPE)
        dkr_sum = jnp.zeros((tr, 128), F32)
        pgqn = jnp.zeros((1, 128), F32)
        pgqr = jnp.zeros((1, 128), F32)
        pgkn = jnp.zeros((1, 128), F32)
        for h in range(MLA_HEADS):
            _, qnh, qns = _norm_part(q_ref[:, h * hp:h * hp + 128], gqn, MLA_NOPE)
            _, qrh, qrs = _norm_part(q_ref[:, h * hp + 128:(h + 1) * hp], gqr, MLA_ROPE)
            dqn = dqf_ref[h, :, 0:128]
            drr = dqf_ref[h, :, 128:256]
            dqr = drr * csv + _rope_swap(drr * snv)
            dq_ref[:, h * hp:h * hp + 128] = _norm_part_bwd(dqn, gqn, qnh, qns, MLA_NOPE).astype(dq_ref.dtype)
            dq_ref[:, h * hp + 128:(h + 1) * hp] = _norm_part_bwd(dqr, gqr, qrh, qrs, MLA_ROPE).astype(dq_ref.dtype)
            pgqn = pgqn + jnp.sum(dqn * qnh, axis=0, keepdims=True)
            pgqr = pgqr + jnp.sum(dqr * qrh, axis=0, keepdims=True)
            _, knh, kns = _norm_part(kv_ref[:, h * hp:h * hp + 128], gkn, MLA_NOPE)
            dkn = dkf_ref[h, :, 0:128]
            dkv_ref[:, h * hp:h * hp + 128] = _norm_part_bwd(dkn, gkn, knh, kns, MLA_NOPE).astype(dkv_ref.dtype)
            dkv_ref[:, h * hp + 128:(h + 1) * hp] = dv_ref[h].astype(dkv_ref.dtype)
            pgkn = pgkn + jnp.sum(dkn * knh, axis=0, keepdims=True)
            dkr_sum = dkr_sum + dkf_ref[h, :, 128:256]
        dkr = dkr_sum * csv + _rope_swap(dkr_sum * snv)
        dkrr_ref[...] = _norm_part_bwd(dkr, gkr, krh, krs, MLA_ROPE).astype(dkrr_ref.dtype)
        pgkr = jnp.sum(dkr * krh, axis=0, keepdims=True)
        pq = jnp.concatenate([pgqn, pgqr], axis=1)
        pk = jnp.concatenate([pgkn, pgkr], axis=1)

        @pl.when(pl.program_id(0) == 0)
        def _():
            dgq_ref[...] = pq
            dgk_ref[...] = pk

        @pl.when(pl.program_id(0) > 0)
        def _():
            dgq_ref[...] += pq
            dgk_ref[...] += pk

    row = lambda w: pl.BlockSpec((tr, w), lambda i: (i, 0))
    vec = pl.BlockSpec((1, hp), lambda i: (0, 0))
    hrow = lambda w: pl.BlockSpec((MLA_HEADS, tr, w), lambda i: (0, i, 0))
    return pl.pallas_call(
        body, name=name, grid=(s // tr,),
        in_specs=[row(MLA_HEADS * hp), row(MLA_HEADS * hp), row(128), row(128), row(128), vec, vec,
                  hrow(hp), hrow(hp), hrow(128)],
        out_specs=[row(MLA_HEADS * hp), row(MLA_HEADS * hp), row(128), vec, vec],
        out_shape=[jax.ShapeDtypeStruct((s, MLA_HEADS * hp), BF16), jax.ShapeDtypeStruct((s, MLA_HEADS * hp), BF16),
                   jax.ShapeDtypeStruct((s, 128), BF16), jax.ShapeDtypeStruct((1, hp), F32),
                   jax.ShapeDtypeStruct((1, hp), F32)],
        compiler_params=_cparams(("arbitrary",), 48),
    )(q, kv, krr, cs, sn, gq, gk, dqf, dkf, dv)


ATT_T = 512


def _chunk_mask(t):
    r = lax.shift_right_logical(lax.broadcasted_iota(jnp.int32, (t, t), 0), 6)
    c = lax.shift_right_logical(lax.broadcasted_iota(jnp.int32, (t, t), 1), 6)
    return c <= r


def _mla_attn_fwd(qf, kf, v, *, name):
    nh, s, hp = qf.shape
    t = _pick(s, ATT_T, CHUNK)
    scale = (MLA_NOPE + MLA_ROPE) ** -0.5
    nt = (((1,), (1,)), ((), ()))

    def body(q_ref, k_ref, v_ref, o_ref, lse_ref):
        i = pl.program_id(1)
        q = q_ref[...]

        def block(j, carry, masked):
            m, l, acc = carry
            start = pl.multiple_of(j * t, t)
            k = k_ref[pl.ds(start, t), :]
            sc = lax.dot_general(q, k, nt, preferred_element_type=F32) * scale
            if masked:
                sc = jnp.where(_chunk_mask(t), sc, NEG)
            m_new = jnp.maximum(m, jnp.max(sc, axis=-1, keepdims=True))
            alpha = jnp.exp(m - m_new)
            p = jnp.exp(sc - m_new)
            l = alpha * l + jnp.sum(p, axis=-1, keepdims=True)
            acc = alpha * acc + jnp.dot(p.astype(BF16), v_ref[pl.ds(start, t), :], preferred_element_type=F32)
            return m_new, l, acc

        init = (jnp.full((t, 1), NEG, F32), jnp.zeros((t, 1), F32), jnp.zeros((t, MLA_V), F32))
        carry = lax.fori_loop(0, i, lambda j, c: block(j, c, False), init)
        m, l, acc = block(i, carry, True)
        o_ref[...] = acc / l
        lse_ref[...] = m + jnp.log(l)

    return pl.pallas_call(
        body, name=name, grid=(nh, s // t),
        in_specs=[pl.BlockSpec((None, t, hp), lambda h, i: (h, i, 0)),
                  pl.BlockSpec((None, s, hp), lambda h, i: (h, 0, 0)),
                  pl.BlockSpec((None, s, MLA_V), lambda h, i: (h, 0, 0))],
        out_specs=[pl.BlockSpec((t, MLA_V), lambda h, i: (i, h)),
                   pl.BlockSpec((None, t, 1), lambda h, i: (h, i, 0))],
        out_shape=[jax.ShapeDtypeStruct((s, nh * MLA_V), F32), jax.ShapeDtypeStruct((nh, s, 1), F32)],
        compiler_params=_cparams(("parallel", "arbitrary"), 48),
    )(qf, kf, v)


def _mla_attn_bwd(qf, kf, v, o, lse, do, *, name):
    nh, s, hp = qf.shape
    t = _pick(s, ATT_T, CHUNK)
    nb = s // t
    scale = (MLA_NOPE + MLA_ROPE) ** -0.5
    nt = (((1,), (1,)), ((), ()))
    tn = (((0,), (0,)), ((), ()))

    def body(q_ref, k_ref, v_ref, o_ref, lse_ref, do_ref, dq_ref, dk_ref, dv_ref, delta):
        j = pl.program_id(1)

        @pl.when(j == 0)
        def _():
            dq_ref[...] = jnp.zeros_like(dq_ref)
            delta[...] = jnp.sum(do_ref[...] * o_ref[...], axis=-1, keepdims=True)

        k = k_ref[...]
        vv = v_ref[...]

        def block(i, carry, masked):
            dk, dv = carry
            start = pl.multiple_of(i * t, t)
            q = q_ref[pl.ds(start, t), :]
            dob = do_ref[pl.ds(start, t), :].astype(BF16)
            sc = lax.dot_general(q, k, nt, preferred_element_type=F32) * scale
            if masked:
                sc = jnp.where(_chunk_mask(t), sc, NEG)
            p = jnp.exp(sc - lse_ref[pl.ds(start, t), :])
            dp = lax.dot_general(dob, vv, nt, preferred_element_type=F32)
            ds = (p * (dp - delta[pl.ds(start, t), :]) * scale).astype(BF16)
            dv = dv + lax.dot_general(p.astype(BF16), dob, tn, preferred_element_type=F32)
            dk = dk + lax.dot_general(ds, q, tn, preferred_element_type=F32)
            dq_ref[pl.ds(start, t), :] += jnp.dot(ds, k, preferred_element_type=F32)
            return dk, dv

        init = (jnp.zeros((t, hp), F32), jnp.zeros((t, MLA_V), F32))
        carry = block(j, init, True)
        dk, dv = lax.fori_loop(j + 1, nb, lambda i, c: block(i, c, False), carry)
        dk_ref[...] = dk
        dv_ref[...] = dv

    whole = lambda w: pl.BlockSpec((None, s, w), lambda h, j: (h, 0, 0))
    blk = lambda w: pl.BlockSpec((None, t, w), lambda h, j: (h, j, 0))
    colh = pl.BlockSpec((s, MLA_V), lambda h, j: (0, h))
    return pl.pallas_call(
        body, name=name, grid=(nh, nb),
        in_specs=[whole(hp), blk(hp), blk(MLA_V), colh, whole(1), colh],
        out_specs=[whole(hp), blk(hp), blk(MLA_V)],
        out_shape=[jax.ShapeDtypeStruct((nh, s, hp), F32), jax.ShapeDtypeStruct((nh, s, hp), F32),
                   jax.ShapeDtypeStruct((nh, s, MLA_V), F32)],
        scratch_shapes=[pltpu.VMEM((s, 1), F32)],
        compiler_params=_cparams(("parallel", "arbitrary"), 56),
    )(qf, kf, v, o, lse, do)


def _merge_fwd(gl, gb, ys, yc, ym, *, name):
    s, d = ys.shape
    tr = _pick(s, 256, 8)

    def body(gl_ref, gb_ref, ys_ref, yc_ref, ym_ref, o_ref):
        acc = jnp.zeros((tr, d), F32)
        for k, y_ref in enumerate((ys_ref, yc_ref, ym_ref)):
            gt = _sigmoid(gl_ref[:, k * d:(k + 1) * d] + gb_ref[:, k * d:(k + 1) * d])
            acc = acc + gt * y_ref[...]
        o_ref[...] = acc.astype(o_ref.dtype)

    row = lambda w: pl.BlockSpec((tr, w), lambda i: (i, 0))
    return pl.pallas_call(
        body, name=name, grid=(s // tr,),
        in_specs=[row(3 * d), pl.BlockSpec((1, 3 * d), lambda i: (0, 0)), row(d), row(d), row(d)],
        out_specs=row(d), out_shape=jax.ShapeDtypeStruct((s, d), BF16),
        compiler_params=_cparams(("parallel",)),
    )(gl, gb, ys, yc, ym)


def _merge_bwd(gl, gb, ys, yc, ym, dm, *, name):
    s, d = ys.shape
    tr = _pick(s, 256, 8)

    def body(gl_ref, gb_ref, ys_ref, yc_ref, ym_ref, dm_ref, dgl_ref, dgb_ref, dys_ref, dyc_ref, dym_ref):
        dmv = dm_ref[...]
        parts = []
        for k, (y_ref, dy_ref) in enumerate(((ys_ref, dys_ref), (yc_ref, dyc_ref), (ym_ref, dym_ref))):
            gt = _sigmoid(gl_ref[:, k * d:(k + 1) * d] + gb_ref[:, k * d:(k + 1) * d])
            dy_ref[...] = (gt * dmv).astype(dy_ref.dtype)
            dl = dmv * y_ref[...] * gt * (1.0 - gt)
            dgl_ref[:, k * d:(k + 1) * d] = dl.astype(dgl_ref.dtype)
            parts.append(jnp.sum(dl, axis=0, keepdims=True))
        pb = jnp.concatenate(parts, axis=1)

        @pl.when(pl.program_id(0) == 0)
        def _():
            dgb_ref[...] = pb

        @pl.when(pl.program_id(0) > 0)
        def _():
            dgb_ref[...] += pb

    row = lambda w: pl.BlockSpec((tr, w), lambda i: (i, 0))
    vec = pl.BlockSpec((1, 3 * d), lambda i: (0, 0))
    return pl.pallas_call(
        body, name=name, grid=(s // tr,),
        in_specs=[row(3 * d), vec, row(d), row(d), row(d), row(d)],
        out_specs=[row(3 * d), vec, row(d), row(d), row(d)],
        out_shape=[jax.ShapeDtypeStruct((s, 3 * d), BF16), jax.ShapeDtypeStruct((1, 3 * d), F32)]
        + [jax.ShapeDtypeStruct((s, d), BF16)] * 3,
        compiler_params=_cparams(("arbitrary",)),
    )(gl, gb, ys, yc, ym, dm)


def _xattn_fwd(q, k, v, gq, *, name):
    s, d = q.shape
    dh = d // X_HEADS
    tr = _pick(s, 512, 8)
    scale = dh ** -0.5
    nt = (((1,), (1,)), ((), ()))

    def body(q_ref, k_ref, v_ref, gq_ref, o_ref):
        for h in range(X_HEADS):
            cs = slice(h * dh, (h + 1) * dh)
            qn, _, _ = _norm_part(q_ref[:, cs], gq_ref[...], dh)
            sc = lax.dot_general(qn.astype(BF16), k_ref[:, cs], nt, preferred_element_type=F32) * scale
            p = jnp.exp(sc - jnp.max(sc, axis=-1, keepdims=True))
            p = p / jnp.sum(p, axis=-1, keepdims=True)
            o_ref[:, cs] = jnp.dot(p.astype(BF16), v_ref[:, cs], preferred_element_type=F32)

    row = pl.BlockSpec((tr, d), lambda i: (i, 0))
    mem = pl.BlockSpec(k.shape, lambda i: (0, 0))
    return pl.pallas_call(
        body, name=name, grid=(s // tr,),
        in_specs=[row, mem, mem, pl.BlockSpec((1, dh), lambda i: (0, 0))], out_specs=row,
        out_shape=jax.ShapeDtypeStruct((s, d), F32), compiler_params=_cparams(("parallel",)),
    )(q, k, v, gq)


def _xattn_bwd(q, k, v, gq, do, *, name):
    s, d = q.shape
    dh = d // X_HEADS
    tr = _pick(s, 512, 8)
    scale = dh ** -0.5
    nt = (((1,), (1,)), ((), ()))
    tn = (((0,), (0,)), ((), ()))

    def body(q_ref, k_ref, v_ref, gq_ref, do_ref, dq_ref, dk_ref, dv_ref, dgq_ref):
        first = pl.program_id(0) == 0
        pg = jnp.zeros((1, dh), F32)
        for h in range(X_HEADS):
            cs = slice(h * dh, (h + 1) * dh)
            qn, qh, qs = _norm_part(q_ref[:, cs], gq_ref[...], dh)
            qnb = qn.astype(BF16)
            kh = k_ref[:, cs]
            sc = lax.dot_general(qnb, kh, nt, preferred_element_type=F32) * scale
            p = jnp.exp(sc - jnp.max(sc, axis=-1, keepdims=True))
            p = p / jnp.sum(p, axis=-1, keepdims=True)
            dob = do_ref[:, cs].astype(BF16)
            dp = lax.dot_general(dob, v_ref[:, cs], nt, preferred_element_type=F32)
            ds = (p * (dp - jnp.sum(dp * p, axis=-1, keepdims=True)) * scale).astype(BF16)
            dqn = jnp.dot(ds, kh, preferred_element_type=F32)
            dq_ref[:, cs] = _norm_part_bwd(dqn, gq_ref[...], qh, qs, dh).astype(dq_ref.dtype)
            pg = pg + jnp.sum(dqn * qh, axis=0, keepdims=True)
            pv = lax.dot_general(p.astype(BF16), dob, tn, preferred_element_type=F32)
            pk = lax.dot_general(ds, qnb, tn, preferred_element_type=F32)

            @pl.when(first)
            def _():
                dv_ref[:, cs] = pv
                dk_ref[:, cs] = pk

            @pl.when(jnp.logical_not(first))
            def _():
                dv_ref[:, cs] += pv
                dk_ref[:, cs] += pk

        @pl.when(first)
        def _():
            dgq_ref[...] = pg

        @pl.when(jnp.logical_not(first))
        def _():
            dgq_ref[...] += pg

    row = pl.BlockSpec((tr, d), lambda i: (i, 0))
    mem = pl.BlockSpec(k.shape, lambda i: (0, 0))
    vec = pl.BlockSpec((1, dh), lambda i: (0, 0))
    return pl.pallas_call(
        body, name=name, grid=(s // tr,),
        in_specs=[row, mem, mem, vec, row], out_specs=[row, mem, mem, vec],
        out_shape=[jax.ShapeDtypeStruct((s, d), BF16), jax.ShapeDtypeStruct(k.shape, F32),
                   jax.ShapeDtypeStruct(k.shape, F32), jax.ShapeDtypeStruct((1, dh), F32)],
        compiler_params=_cparams(("arbitrary",)),
    )(q, k, v, gq, do)


def _swiglu_fwd(h1, *, name):
    s, w2 = h1.shape
    w = w2 // 2
    tr = _pick(s, 256, 8)
    tc = _pick(w, 1408, 128)
    ncb = w // tc

    def body(g_ref, u_ref, o_ref):
        o_ref[...] = (_silu(g_ref[...]) * u_ref[...]).astype(o_ref.dtype)

    return pl.pallas_call(
        body, name=name, grid=(s // tr, ncb),
        in_specs=[pl.BlockSpec((tr, tc), lambda i, j: (i, j)), pl.BlockSpec((tr, tc), lambda i, j: (i, j + ncb))],
        out_specs=pl.BlockSpec((tr, tc), lambda i, j: (i, j)),
        out_shape=jax.ShapeDtypeStruct((s, w), BF16), compiler_params=_cparams(("parallel", "parallel")),
    )(h1, h1)


def _swiglu_bwd(h1, dact, *, name):
    s, w2 = h1.shape
    w = w2 // 2
    tr = _pick(s, 256, 8)
    tc = _pick(w, 1408, 128)
    ncb = w // tc

    def body(g_ref, u_ref, d_ref, dg_ref, du_ref):
        gv = g_ref[...]
        dv = d_ref[...]
        dg_ref[...] = (dv * u_ref[...] * _dsilu(gv)).astype(dg_ref.dtype)
        du_ref[...] = (dv * _silu(gv)).astype(du_ref.dtype)

    blk = pl.BlockSpec((tr, tc), lambda i, j: (i, j))
    dg, du = pl.pallas_call(
        body, name=name, grid=(s // tr, ncb),
        in_specs=[blk, pl.BlockSpec((tr, tc), lambda i, j: (i, j + ncb)), blk],
        out_specs=[blk, blk],
        out_shape=[jax.ShapeDtypeStruct((s, w), BF16)] * 2, compiler_params=_cparams(("parallel", "parallel")),
    )(h1, h1, dact)
    return jnp.concatenate([dg, du], axis=1)


def _add(a, b, *, name):
    r, w = a.shape
    tr = _pick(r, 512, 8)

    def body(a_ref, b_ref, o_ref):
        o_ref[...] = a_ref[...] + b_ref[...].astype(F32)

    row = pl.BlockSpec((tr, w), lambda i: (i, 0))
    return pl.pallas_call(
        body, name=name, grid=(r // tr,), in_specs=[row, row], out_specs=row,
        out_shape=jax.ShapeDtypeStruct((r, w), F32), compiler_params=_cparams(("parallel",)),
    )(a, b)


def _loss(y, target, *, name):
    r, w = y.shape
    tr = _pick(r, 512, 8)

    def body(y_ref, t_ref, dy_ref, l_ref):
        err = y_ref[...] - t_ref[...]
        dy_ref[...] = err * (1.0 / w)
        part = jnp.zeros((8, LANE), F32) + 0.5 * jnp.sum(jnp.mean(err * err, axis=-1, keepdims=True))

        @pl.when(pl.program_id(0) == 0)
        def _():
            l_ref[...] = part

        @pl.when(pl.program_id(0) > 0)
        def _():
            l_ref[...] += part

    row = pl.BlockSpec((tr, w), lambda i: (i, 0))
    dy, l = pl.pallas_call(
        body, name=name, grid=(r // tr,), in_specs=[row, row],
        out_specs=[row, pl.BlockSpec((8, LANE), lambda i: (0, 0))],
        out_shape=[jax.ShapeDtypeStruct((r, w), F32), jax.ShapeDtypeStruct((8, LANE), F32)],
        compiler_params=_cparams(("arbitrary",)),
    )(y, target)
    return dy, l[0, 0]


def _adamw(w, g, m, v, *, name):
    r, c = w.shape
    tr = _pick(r, 256, 8)
    c1 = 1.0 - ADAM_B1 ** ADAM_STEP
    c2 = 1.0 - ADAM_B2 ** ADAM_STEP

    def body(w_ref, g_ref, m_ref, v_ref, d_ref, nm_ref, nv_ref):
        gv = g_ref[...]
        nm = ADAM_B1 * m_ref[...] + (1.0 - ADAM_B1) * gv
        nv = ADAM_B2 * v_ref[...] + (1.0 - ADAM_B2) * (gv * gv)
        nm_ref[...] = nm
        nv_ref[...] = nv
        d_ref[...] = -ADAM_LR * ((nm / c1) / (jnp.sqrt(nv / c2) + ADAM_EPS) + ADAM_WD * w_ref[...])

    row = pl.BlockSpec((tr, c), lambda i: (i, 0))
    return pl.pallas_call(
        body, name=name, grid=(r // tr,), in_specs=[row] * 4, out_specs=[row] * 3,
        out_shape=[jax.ShapeDtypeStruct((r, c), F32)] * 3, compiler_params=_cparams(("parallel",)),
    )(w, g, m, v)


IN_SPLIT = dict(z=(0, 1024), xbc=(1024, 3072), dt=(3072, 3200), glu=(3200, 5248), ql=(5248, 5632),
                ckv=(5632, 5888), kr=(5888, 6016), gate=(6016, 9088))


IN_WIDTH_PAD = 9216


def _w_in_pad(w):
    zeros = lambda n: jnp.zeros(w.shape[:-1] + (n,), w.dtype)
    return jnp.concatenate([w[..., :3088], zeros(112), w[..., 3088:5840], zeros(64), w[..., 5840:],
                            zeros(IN_WIDTH_PAD - 9088)], axis=-1)


def _w_in_unpad(g):
    return jnp.concatenate([g[..., :3088], g[..., 3200:5952], g[..., 6016:9088]], axis=-1)


def _qb_pad(w):
    lead = w.shape[:-1]
    w = w.reshape(lead + (MLA_HEADS, MLA_NOPE + MLA_ROPE))
    w = jnp.concatenate([w, jnp.zeros(lead + (MLA_HEADS, MLA_HP - MLA_NOPE - MLA_ROPE), w.dtype)], axis=-1)
    return w.reshape(lead + (MLA_HEADS * MLA_HP,))


def _qb_unpad(g):
    lead = g.shape[:-1]
    g = g.reshape(lead + (MLA_HEADS, MLA_HP))[..., :MLA_NOPE + MLA_ROPE]
    return g.reshape(lead + (MLA_HEADS * (MLA_NOPE + MLA_ROPE),))


def _pad_lanes(v, n):
    return jnp.concatenate([v, jnp.zeros((n - v.shape[0],), v.dtype)]).reshape(1, n)


def _layer_params(full, rep, l):
    p = {}
    w_in = full['w_in']
    for k, (a, b) in IN_SPLIT.items():
        p['w_' + k] = w_in[:, a:b]
    p['w_in'] = w_in
    for k in ('mla_w_q_b', 'mla_w_kv_b', 'xattn_w_kv', 'ffn_w_in', 'ssd_w_out', 'conv_w_out', 'mla_w_o', 'w_out',
              'xattn_w_q', 'xattn_w_o', 'ffn_w_out', 'ssd_conv_w', 'conv_dw_w'):
        p[k] = full[k]
    p['gate_b'] = full['gate_b'].reshape(1, -1)
    row = lambda name: rep[name][l].reshape(1, -1)
    for k in ('mix_norm_g', 'ssd_conv_b', 'ssd_norm_g', 'conv_dw_b', 'conv_ln_g', 'conv_ln_b', 'mla_q_a_g',
              'mla_kv_a_g', 'xattn_norm_g', 'mem_norm_g', 'xattn_q_norm_g', 'xattn_k_norm_g', 'ffn_norm_g'):
        p[k] = row(k)
    for k in ('ssd_dt_bias', 'ssd_a_log', 'ssd_d'):
        p[k] = _pad_lanes(rep[k][l], LANE)
        p[k + '_t'] = p[k].reshape(LANE, 1)
    p['gq'] = _pad_lanes(rep['mla_q_norm_g'][l], MLA_HP)
    p['gk'] = _pad_lanes(rep['mla_k_norm_g'][l], MLA_HP)
    return p


def _layer_fwd(x, mem, cs, sn, p, l):
    n = lambda s: f"l{l}_{s}"
    s_len, d = x.shape
    nc = s_len // CHUNK
    sv = {'x': x}
    u = _rms_fwd(x, p['mix_norm_g'], out_dtype=BF16, name=n("mix_norm"))
    z = _matmul(u, p['w_z'], name=n("in_z"))
    xbc = _matmul(u, p['w_xbc'], name=n("in_xbc"))
    dtr = _matmul(u, p['w_dt'], name=n("in_dt"))
    glu = _matmul(u, p['w_glu'], name=n("in_glu"))
    ql = _matmul(u, p['w_ql'], name=n("in_ql"))
    ckv = _matmul(u, p['w_ckv'], name=n("in_ckv"))
    krr = _matmul(u, p['w_kr'], name=n("in_kr"))
    gl = _matmul(u, p['w_gate'], name=n("in_gate"))
    pre_s, act_s = _conv_fwd(xbc, p['ssd_conv_w'], p['ssd_conv_b'], glu=False, act=True, name=n("ssd_conv"))
    dtrt = dtr.reshape(nc, CHUNK, LANE).transpose(0, 2, 1)
    y_scan, prev = _ssd_scan_fwd(act_s, dtr, dtrt, p['ssd_dt_bias'], p['ssd_dt_bias_t'], p['ssd_a_log'],
                                 p['ssd_a_log_t'], p['ssd_d'], name=n("ssd_scan"))
    yn = _ssd_gate_fwd(y_scan, z, p['ssd_norm_g'], name=n("ssd_gate"))
    y_ssd = _matmul(yn, p['ssd_w_out'], name=n("ssd_out"))
    pre_c, = _conv_fwd(glu, p['conv_dw_w'], p['conv_dw_b'], glu=True, act=False, name=n("dw_conv"))
    vc = _ln_silu_fwd(pre_c, p['conv_ln_g'], p['conv_ln_b'], name=n("conv_ln"))
    y_conv = _matmul(vc, p['conv_w_out'], name=n("conv_out"))
    qln = _rms_fwd(ql, p['mla_q_a_g'], out_dtype=BF16, name=n("q_a_norm"))
    q = _matmul(qln, p['mla_w_q_b'], name=n("q_b"))
    ckvn = _rms_fwd(ckv, p['mla_kv_a_g'], out_dtype=BF16, name=n("kv_a_norm"))
    kv = _matmul(ckvn, p['mla_w_kv_b'], name=n("kv_b"))
    qf, kf, v = _mla_prep_fwd(q, kv, krr, cs, sn, p['gq'], p['gk'], name=n("mla_prep"))
    o, lse = _mla_attn_fwd(qf, kf, v, name=n("mla_attn"))
    y_mla = _matmul(o, p['mla_w_o'], name=n("mla_out"))
    merged = _merge_fwd(gl, p['gate_b'], y_ssd, y_conv, y_mla, name=n("merge"))
    x1 = _matmul(merged, p['w_out'], add=x, name=n("mix_out"))
    hx = _rms_fwd(x1, p['xattn_norm_g'], out_dtype=BF16, name=n("xattn_norm"))
    qx = _matmul(hx, p['xattn_w_q'], name=n("xattn_q"))
    memn = _rms_fwd(mem, p['mem_norm_g'], out_dtype=BF16, name=n("mem_norm"))
    kvx = _matmul(memn, p['xattn_w_kv'], name=n("xattn_kv"))
    m_len = mem.shape[0]
    dh = d // X_HEADS
    kraw = kvx[:, :d].reshape(m_len * X_HEADS, dh)
    kx = _rms_fwd(kraw, p['xattn_k_norm_g'], out_dtype=BF16, name=n("xattn_k_norm")).reshape(m_len, d)
    vx = kvx[:, d:].astype(BF16)
    ox = _xattn_fwd(qx, kx, vx, p['xattn_q_norm_g'], name=n("xattn_core"))
    x2 = _matmul(ox, p['xattn_w_o'], add=x1, name=n("xattn_out"))
    hf = _rms_fwd(x2, p['ffn_norm_g'], out_dtype=BF16, name=n("ffn_norm"))
    h1 = _matmul(hf, p['ffn_w_in'], name=n("ffn_in"))
    act = _swiglu_fwd(h1, name=n("swiglu"))
    x3 = _matmul(act, p['ffn_w_out'], add=x2, name=n("ffn_out"))
    sv.update(u=u, z=z, xbc=xbc, dtr=dtr, dtrt=dtrt, glu=glu, ql=ql, ckv=ckv, krr=krr, gl=gl, pre_s=pre_s,
              act_s=act_s, y_scan=y_scan, prev=prev, yn=yn, y_ssd=y_ssd, pre_c=pre_c, vc=vc, y_conv=y_conv,
              qln=qln, q=q, ckvn=ckvn, kv=kv, qf=qf, kf=kf, v=v, o=o, lse=lse, y_mla=y_mla, merged=merged,
              x1=x1, hx=hx, qx=qx, memn=memn, kraw=kraw, kx=kx, vx=vx, ox=ox, x2=x2, hf=hf, h1=h1, act=act)
    return x3, sv


DW_KEY = dict(ffn_out_dw='ffn_w_out', ffn_in_dw='ffn_w_in', xattn_out_dw='xattn_w_o', xattn_q_dw='xattn_w_q',
              xattn_kv_dw='xattn_w_kv', mix_out_dw='w_out', mla_out_dw='mla_w_o', q_b_dw='mla_w_q_b',
              kv_b_dw='mla_w_kv_b', conv_out_dw='conv_w_out', ssd_out_dw='ssd_w_out', in_dw='w_in')


def _layer_bwd(dx3, mem, cs, sn, p, sv, l, stacks, depth):
    n = lambda s: f"l{l}_b_{s}"
    dw = lambda s: dict(name=n(s), out_dtype=BF16, into=(stacks.get(DW_KEY[s]), l, depth))
    g = {}
    d = dx3.shape[1]
    dact = _matmul(dx3, p['ffn_w_out'], tb=True, name=n("ffn_out_dx"))
    g['ffn_w_out'] = _matmul(sv['act'], dx3, ta=True, **dw("ffn_out_dw"))
    dh1 = _swiglu_bwd(sv['h1'], dact, name=n("swiglu"))
    g['ffn_w_in'] = _matmul(sv['hf'], dh1, ta=True, **dw("ffn_in_dw"))
    dhf = _matmul(dh1, p['ffn_w_in'], tb=True, name=n("ffn_in_dx"))
    dx2, g['ffn_norm_g'] = _rms_bwd(sv['x2'], p['ffn_norm_g'], dhf, dx_dtype=F32, add=dx3, name=n("ffn_norm"))
    dox = _matmul(dx2, p['xattn_w_o'], tb=True, name=n("xattn_out_dx"))
    g['xattn_w_o'] = _matmul(sv['ox'], dx2, ta=True, **dw("xattn_out_dw"))
    dqx, dkx, dvx, g['xattn_q_norm_g'] = _xattn_bwd(sv['qx'], sv['kx'], sv['vx'], p['xattn_q_norm_g'], dox,
                                                    name=n("xattn_core"))
    g['xattn_w_q'] = _matmul(sv['hx'], dqx, ta=True, **dw("xattn_q_dw"))
    dhx = _matmul(dqx, p['xattn_w_q'], tb=True, name=n("xattn_q_dx"))
    dx1, g['xattn_norm_g'] = _rms_bwd(sv['x1'], p['xattn_norm_g'], dhx, dx_dtype=F32, add=dx2, name=n("xattn_norm"))
    m_len = mem.shape[0]
    dh = d // X_HEADS
    dkraw, g['xattn_k_norm_g'] = _rms_bwd(sv['kraw'], p['xattn_k_norm_g'], dkx.reshape(m_len * X_HEADS, dh),
                                          dx_dtype=BF16, name=n("xattn_k_norm"))
    dkvx = jnp.concatenate([dkraw.reshape(m_len, d), dvx.astype(BF16)], axis=1)
    g['xattn_w_kv'] = _matmul(sv['memn'], dkvx, ta=True, **dw("xattn_kv_dw"))
    dmemn = _matmul(dkvx, p['xattn_w_kv'], tb=True, name=n("xattn_kv_dx"))
    _, g['mem_norm_g'] = _rms_bwd(mem, p['mem_norm_g'], dmemn, dx_dtype=BF16, name=n("mem_norm"))
    dmerged = _matmul(dx1, p['w_out'], tb=True, name=n("mix_out_dx"))
    g['w_out'] = _matmul(sv['merged'], dx1, ta=True, **dw("mix_out_dw"))
    dgl, g['gate_b'], dys, dyc, dym = _merge_bwd(sv['gl'], p['gate_b'], sv['y_ssd'], sv['y_conv'], sv['y_mla'],
                                                 dmerged, name=n("merge"))
    do = _matmul(dym, p['mla_w_o'], tb=True, name=n("mla_out_dx"))
    g['mla_w_o'] = _matmul(sv['o'], dym, ta=True, **dw("mla_out_dw"))
    dqf, dkf, dv = _mla_attn_bwd(sv['qf'], sv['kf'], sv['v'], sv['o'], sv['lse'], do, name=n("mla_attn"))
    dq, dkv, dkrr, g['gq'], g['gk'] = _mla_prep_bwd(sv['q'], sv['kv'], sv['krr'], cs, sn, p['gq'], p['gk'],
                                                    dqf, dkf, dv, name=n("mla_prep"))
    g['mla_w_q_b'] = _matmul(sv['qln'], dq, ta=True, **dw("q_b_dw"))
    dqln = _matmul(dq, p['mla_w_q_b'], tb=True, name=n("q_b_dx"))
    dql, g['mla_q_a_g'] = _rms_bwd(sv['ql'], p['mla_q_a_g'], dqln, dx_dtype=BF16, name=n("q_a_norm"))
    g['mla_w_kv_b'] = _matmul(sv['ckvn'], dkv, ta=True, **dw("kv_b_dw"))
    dckvn = _matmul(dkv, p['mla_w_kv_b'], tb=True, name=n("kv_b_dx"))
    dckv, g['mla_kv_a_g'] = _rms_bwd(sv['ckv'], p['mla_kv_a_g'], dckvn, dx_dtype=BF16, name=n("kv_a_norm"))
    dvc = _matmul(dyc, p['conv_w_out'], tb=True, name=n("conv_out_dx"))
    g['conv_w_out'] = _matmul(sv['vc'], dyc, ta=True, **dw("conv_out_dw"))
    dpre_c, g['conv_ln_g'], g['conv_ln_b'] = _ln_silu_bwd(sv['pre_c'], p['conv_ln_g'], p['conv_ln_b'], dvc,
                                                          name=n("conv_ln"))
    da, dg, g['conv_dw_w'], g['conv_dw_b'] = _conv_bwd(sv['glu'], p['conv_dw_w'], p['conv_dw_b'], dpre_c, None,
                                                       glu=True, act=False, name=n("dw_conv"))
    dyn = _matmul(dys, p['ssd_w_out'], tb=True, name=n("ssd_out_dx"))
    g['ssd_w_out'] = _matmul(sv['yn'], dys, ta=True, **dw("ssd_out_dw"))
    dy_scan, dz, g['ssd_norm_g'] = _ssd_gate_bwd(sv['y_scan'], sv['z'], p['ssd_norm_g'], dyn, name=n("ssd_gate"))
    dxs, db, dc, ddtr, g['ssd_a_log'], g['ssd_dt_bias'], g['ssd_d'] = _ssd_scan_bwd(
        sv['act_s'], sv['dtr'], sv['dtrt'], p['ssd_dt_bias'], p['ssd_dt_bias_t'], p['ssd_a_log'], p['ssd_a_log_t'],
        p['ssd_d'], sv['prev'], dy_scan, name=n("ssd_scan"))
    dact_s = jnp.concatenate([dxs, db, dc], axis=1)
    dxbc, g['ssd_conv_w'], g['ssd_conv_b'] = _conv_bwd(sv['xbc'], p['ssd_conv_w'], p['ssd_conv_b'], dact_s,
                                                       sv['pre_s'], glu=False, act=True, name=n("ssd_conv"))
    tail = jnp.zeros((dz.shape[0], IN_WIDTH_PAD - IN_SPLIT['gate'][1]), BF16)
    dproj = jnp.concatenate([dz, dxbc, ddtr, da, dg, dql, dckv, dkrr, dgl, tail], axis=1)
    g['w_in'] = _matmul(sv['u'], dproj, ta=True, **dw("in_dw"))
    du = _matmul(dproj, p['w_in'], tb=True, name=n("in_dx"))
    dx, g['mix_norm_g'] = _rms_bwd(sv['x'], p['mix_norm_g'], du, dx_dtype=F32, add=dx1, name=n("mix_norm"))
    return dx, g


REP_NAMES = ('mix_norm_g', 'ssd_conv_b', 'ssd_dt_bias', 'ssd_a_log', 'ssd_d', 'ssd_norm_g', 'conv_dw_b', 'conv_ln_g',
             'conv_ln_b', 'mla_q_a_g', 'mla_kv_a_g', 'mla_q_norm_g', 'mla_k_norm_g', 'xattn_norm_g', 'mem_norm_g',
             'xattn_q_norm_g', 'xattn_k_norm_g', 'ffn_norm_g')
BIG = (('w_in', 1, 1024, 8912), ('mla_w_q_b', 1, 384, 1536), ('mla_w_kv_b', 1, 256, 2048),
       ('xattn_w_kv', 1, 1024, 2048), ('ffn_w_in', 1, 1024, 5632), ('ssd_w_out', 0, 1024, 1024),
       ('conv_w_out', 0, 1024, 1024), ('mla_w_o', 0, 1024, 1024), ('w_out', 0, 1024, 1024),
       ('xattn_w_q', 0, 1024, 1024), ('xattn_w_o', 0, 1024, 1024), ('ffn_w_out', 0, 2816, 1024))
SMALL = (('ssd_conv_w', 1, 4, 2048), ('conv_dw_w', 1, 31, 1024), ('gate_b', 1, 3, 1024))


def _rope_tables(positions):
    half = MLA_ROPE // 2
    inv = ROPE_THETA ** (-jnp.arange(0, MLA_ROPE, 2, dtype=F32) / MLA_ROPE)
    ang = positions.astype(F32)[:, None] * inv
    cos, sin = jnp.cos(ang), jnp.sin(ang)
    z = jnp.zeros((positions.shape[0], LANE - 2 * half), F32)
    return jnp.concatenate([cos, cos, z], axis=1), jnp.concatenate([-sin, sin, z], axis=1)


def _local_step(x, mem, positions, target, layer_weights, rep):
    depth = rep['mix_norm_g'].shape[0]
    cs, sn = _rope_tables(positions)
    params, saved = [], []
    h = x
    for l in range(depth):
        p = _layer_params(layer_weights(l, h), rep, l)
        h, sv = _layer_fwd(h, mem, cs, sn, p, l)
        params.append(p)
        saved.append(sv)
    dh, loss = _loss(h, target, name="loss")
    layer_grads = [None] * depth
    stacks = {}
    for l in reversed(range(depth)):
        dh, layer_grads[l] = _layer_bwd(dh, mem, cs, sn, params[l], saved[l], l, stacks, depth)
        stacks = {k: layer_grads[l][k] for k in DW_KEY.values()}
    stack = lambda k: jnp.stack([layer_grads[l][k] for l in range(depth)])
    gfull = {k: stack(k) for k, _, _, _ in SMALL}
    gfull.update(stacks)
    gfull['w_in'] = _w_in_unpad(gfull['w_in'])
    gfull['mla_w_q_b'] = _qb_unpad(gfull['mla_w_q_b'])
    gfull['gate_b'] = gfull['gate_b'].reshape(depth, 3, -1)
    grep = {}
    for k in REP_NAMES:
        if k == 'mla_q_norm_g':
            grep[k] = stack('gq')[:, 0, :MLA_NOPE + MLA_ROPE]
        elif k == 'mla_k_norm_g':
            grep[k] = stack('gk')[:, 0, :MLA_NOPE + MLA_ROPE]
        elif k in ('ssd_dt_bias', 'ssd_a_log', 'ssd_d'):
            grep[k] = stack(k)[:, 0, :SSD_HEADS]
        else:
            grep[k] = stack(k)[:, 0, :]
    return loss, dh, gfull, grep


N_CHIPS = 4
HBM_SPEC = pl.BlockSpec(memory_space=pl.ANY)
ROW, COL, STK, REP = "row", "col", "stk", "rep"


def _kind(axis, cs):
    if axis == 0:
        return ROW
    return COL if cs % LANE == 0 else STK


def _mesh_pos():
    return lax.axis_index("x"), lax.axis_index("y"), lax.axis_index("c")


def _chip_view(ref, kind, j, a, b, layers=None):
    lsel = slice(None) if layers is None else pl.ds(layers[0], layers[1])
    if kind == STK:
        return ref.at[j, lsel]
    if kind == ROW:
        return ref.at[lsel, pl.ds(pl.multiple_of(j * a, 8), a), :]
    if kind == COL:
        return ref.at[lsel, :, pl.ds(pl.multiple_of(j * b, LANE), b)]
    return ref.at[lsel]


def _all_gather(shards, kinds, *, name):
    n = len(shards)
    depth = shards[0].shape[0]
    lh = depth // 2

    def out_shape(w, kind):
        _, a, b = w.shape
        full = {ROW: (depth, N_CHIPS * a, b), COL: (depth, a, N_CHIPS * b), STK: (N_CHIPS, depth, a, b)}[kind]
        return jax.ShapeDtypeStruct(full, w.dtype)

    def body(*refs):
        w_refs, out_refs = refs[:n], refs[n:2 * n]
        send_sems, recv_sems = refs[2 * n:]
        x, y, cc = _mesh_pos()
        me = (x, y, cc)
        sibling = (x, y, 1 - cc)
        chips = [(1 - x, y), (x, 1 - y), (1 - x, 1 - y)]
        slot = lambda chip: 2 * chip[0] + chip[1]

        def part(i, chip, hc):
            _, a, b = w_refs[i].shape
            return _chip_view(out_refs[i], kinds[i], slot(chip), a, b, (hc * lh, lh))

        def own(i):
            _, a, b = w_refs[i].shape
            return _chip_view(out_refs[i], kinds[i], slot((x, y)), a, b)

        def copy(i, k, src, dst, to):
            return pltpu.make_async_remote_copy(src_ref=src, dst_ref=dst, send_sem=send_sems.at[i, k],
                                                recv_sem=recv_sems.at[i, k], device_id=to, device_id_type=MESH)

        sends = []
        for i in range(n):
            my_half = w_refs[i].at[pl.ds(cc * lh, lh)]
            for k, chip in enumerate(chips):
                sends.append(copy(i, k, my_half, part(i, (x, y), cc), (*chip, cc)))
            sends.append(copy(i, 6, w_refs[i], own(i), sibling))
        for cp in sends:
            cp.start()
        for k, chip in enumerate(chips):
            for i in range(n):
                copy(i, k, part(i, chip, cc), part(i, chip, cc), me).wait_recv()
                cp = copy(i, 3 + k, part(i, chip, cc), part(i, chip, cc), sibling)
                cp.start()
                sends.append(cp)
        for i in range(n):
            for k, chip in enumerate(chips):
                copy(i, 3 + k, part(i, chip, 1 - cc), part(i, chip, 1 - cc), me).wait_recv()
            copy(i, 6, own(i), own(i), me).wait_recv()
        for cp in sends:
            cp.wait_send()

    return pl.pallas_call(
        body, name=name, in_specs=[HBM_SPEC] * n, out_specs=[HBM_SPEC] * n,
        out_shape=[out_shape(w, kd) for w, kd in zip(shards, kinds)],
        scratch_shapes=[pltpu.SemaphoreType.DMA((n, 7)), pltpu.SemaphoreType.DMA((n, 7))],
    )(*shards)


SEM_SPEC = pl.BlockSpec(memory_space=pltpu.SEMAPHORE)
SPLIT_COPY = pltpu.CompilerParams(has_side_effects=pltpu.SideEffectType.DATAFLOW_SIDE_EFFECTING)


def _land_view(ref, kind, j, a, b, rows=None):
    r0, nr = (0, a) if rows is None else rows
    if kind == STK:
        return ref.at[j, pl.ds(r0, nr), :]
    if kind == ROW:
        return ref.at[pl.ds(pl.multiple_of(j * a + r0, 8), nr), :]
    return ref.at[pl.ds(r0, nr), pl.ds(pl.multiple_of(j * b, LANE), b)]


def _layer_copies(w_refs, land_refs, kinds, layer, send_sems, recv_sems):
    x, y, cc = _mesh_pos()
    place = 2 * x + y
    chips = [(1 - x, y), (x, 1 - y), (1 - x, 1 - y)]
    out, inc = [], []
    for i, (w_ref, land, kind) in enumerate(zip(w_refs, land_refs, kinds)):
        _, a, b = w_ref.shape
        half = (cc * (a // 2), a // 2)
        desc = lambda k, src, dst, to: pltpu.make_async_remote_copy(
            src_ref=src, dst_ref=dst, send_sem=send_sems.at[4 * i + k], recv_sem=recv_sems.at[4 * i + k], device_id=to,
            device_id_type=MESH)
        for k, chip in enumerate(chips):
            out.append(desc(k, w_ref.at[layer, pl.ds(half[0], half[1]), :], _land_view(land, kind, place, a, b, half),
                            (*chip, cc)))
            theirs = _land_view(land, kind, 2 * chip[0] + chip[1], a, b, half)
            inc.append(desc(k, theirs, theirs, (*chip, cc)))
        out.append(desc(3, w_ref.at[layer], _land_view(land, kind, place, a, b), (x, y, 1 - cc)))
        own = _land_view(land, kind, place, a, b)
        inc.append(desc(3, own, own, (x, y, 1 - cc)))
    return out, inc


def _land_shape(w, kind):
    _, a, b = w.shape
    return {ROW: (N_CHIPS * a, b), COL: (a, N_CHIPS * b), STK: (N_CHIPS, a, b)}[kind]


def _ag_start(shards, kinds, layer, after, *, name):
    n = len(shards)
    lands = [pltpu.with_memory_space_constraint(lax.empty(_land_shape(w, kd), w.dtype), pltpu.HBM)
             for w, kd in zip(shards, kinds)]

    def body(*refs):
        w_refs = refs[:n]
        send_sems, recv_sems = refs[2 * n + 1], refs[2 * n + 2]
        land_refs = refs[2 * n + 3:3 * n + 3]
        token = refs[3 * n + 3]
        out, _ = _layer_copies(w_refs, land_refs, kinds, layer, send_sems, recv_sems)
        for cp in out:
            cp.start()
        token[...] = jnp.zeros_like(token)

    after = jnp.zeros((8, LANE), F32) if after is None else after
    res = pl.pallas_call(
        body, name=name,
        in_specs=[HBM_SPEC] * (2 * n + 1),
        out_specs=[SEM_SPEC, SEM_SPEC] + [HBM_SPEC] * n + [pl.BlockSpec(memory_space=pltpu.VMEM)],
        out_shape=[pltpu.SemaphoreType.DMA((4 * n,)), pltpu.SemaphoreType.DMA((4 * n,))]
        + [jax.ShapeDtypeStruct(ld.shape, ld.dtype) for ld in lands] + [jax.ShapeDtypeStruct((8, LANE), F32)],
        input_output_aliases={n + i: 2 + i for i in range(n)},
        compiler_params=SPLIT_COPY,
    )(*shards, *lands, after)
    return res[0], res[1], list(res[2:2 + n]), res[2 + n]


def _ag_wait(send_sems, recv_sems, shards, lands, kinds, layer, after, *, name):
    n = len(shards)

    def body(*refs):
        w_refs, land_in = refs[:n], refs[n:2 * n]
        send_sems, recv_sems = refs[2 * n], refs[2 * n + 1]
        out, inc = _layer_copies(w_refs, land_in, kinds, layer, send_sems, recv_sems)
        for cp in out:
            cp.wait_send()
        for cp in inc:
            cp.wait_recv()

    return list(pl.pallas_call(
        body, name=name,
        in_specs=[HBM_SPEC] * (2 * n) + [SEM_SPEC, SEM_SPEC, HBM_SPEC],
        out_specs=[HBM_SPEC] * n,
        out_shape=[jax.ShapeDtypeStruct(ld.shape, ld.dtype) for ld in lands],
        input_output_aliases={n + i: i for i in range(n)},
        compiler_params=SPLIT_COPY,
    )(*shards, *lands, send_sems, recv_sems, after))


def _ag_forward(lands, kinds, shard_shapes, *, name):
    n = len(lands)

    def body(*refs):
        land_refs = refs[n:2 * n]
        send_sems, recv_sems = refs[2 * n:]
        x, y, cc = _mesh_pos()
        chips = [(1 - x, y), (x, 1 - y), (1 - x, 1 - y)]
        sends, waits = [], []
        for i, (land, kind) in enumerate(zip(land_refs, kinds)):
            a, b = shard_shapes[i]
            for k, chip in enumerate(chips):
                j = 2 * chip[0] + chip[1]
                desc = lambda view: pltpu.make_async_remote_copy(
                    src_ref=view, dst_ref=view, send_sem=send_sems.at[i, k], recv_sem=recv_sems.at[i, k],
                    device_id=(x, y, 1 - cc), device_id_type=MESH)
                sends.append(desc(_land_view(land, kind, j, a, b, (cc * (a // 2), a // 2))))
                waits.append(desc(_land_view(land, kind, j, a, b, ((1 - cc) * (a // 2), a // 2))))
        for cp in sends:
            cp.start()
        for cp in waits:
            cp.wait_recv()
        for cp in sends:
            cp.wait_send()

    return list(pl.pallas_call(
        body, name=name, in_specs=[HBM_SPEC] * n, out_specs=[HBM_SPEC] * n,
        out_shape=[jax.ShapeDtypeStruct(ld.shape, ld.dtype) for ld in lands],
        input_output_aliases={i: i for i in range(n)},
        scratch_shapes=[pltpu.SemaphoreType.DMA((n, 3)), pltpu.SemaphoreType.DMA((n, 3))],
    )(*lands))


def _layers_half(ref, kind, start, lh):
    return ref.at[:, pl.ds(start, lh)] if kind == STK else ref.at[pl.ds(start, lh)]


def _rs_pair(gs, kinds, *, name):
    n = len(gs)

    def half_shape(g, kind):
        s = list(g.shape)
        s[1 if kind == STK else 0] //= 2
        return jax.ShapeDtypeStruct(tuple(s), g.dtype)

    def body(*refs):
        g_refs, buf_refs = refs[:n], refs[n:2 * n]
        send_sems, recv_sems = refs[2 * n:]
        x, y, cc = _mesh_pos()
        cps = []
        for i in range(n):
            lh = buf_refs[i].shape[1 if kinds[i] == STK else 0]
            cp = pltpu.make_async_remote_copy(src_ref=_layers_half(g_refs[i], kinds[i], (1 - cc) * lh, lh),
                                              dst_ref=buf_refs[i], send_sem=send_sems.at[i], recv_sem=recv_sems.at[i],
                                              device_id=(x, y, 1 - cc), device_id_type=MESH)
            cp.start()
            cps.append(cp)
        for cp in cps:
            cp.wait()

    return pl.pallas_call(
        body, name=name, in_specs=[HBM_SPEC] * n, out_specs=[HBM_SPEC] * n,
        out_shape=[half_shape(g, kd) for g, kd in zip(gs, kinds)],
        scratch_shapes=[pltpu.SemaphoreType.DMA((n,)), pltpu.SemaphoreType.DMA((n,))],
    )(*gs)


def _row_tile(rows, cols):
    return _pick(rows, max(8, (512 * 1024 // cols) // 8 * 8), 8)


def _rs_pair_add(g, buf, kind, cc, out_dtype, *, name):
    cols = g.shape[-1]
    pre = g.shape[0] if kind == STK else 1
    rows = buf.size // (pre * cols)
    tr = _row_tile(rows, cols)

    def body(cc_ref, g_ref, b_ref, o_ref):
        o_ref[...] = (g_ref[...].astype(F32) + b_ref[...].astype(F32)).astype(o_ref.dtype)

    out = pl.pallas_call(
        body, name=name,
        grid_spec=pltpu.PrefetchScalarGridSpec(
            num_scalar_prefetch=1, grid=(pre, rows // tr),
            in_specs=[pl.BlockSpec((None, None, tr, cols), lambda s, i, cc_ref: (s, cc_ref[0], i, 0)),
                      pl.BlockSpec((None, tr, cols), lambda s, i, cc_ref: (s, i, 0))],
            out_specs=pl.BlockSpec((None, tr, cols), lambda s, i, cc_ref: (s, i, 0))),
        out_shape=jax.ShapeDtypeStruct((pre, rows, cols), out_dtype),
        compiler_params=_cparams(("parallel", "parallel")),
    )(cc.reshape(1).astype(jnp.int32), g.reshape(pre, 2, rows, cols), buf.reshape(pre, rows, cols))
    return out.reshape(buf.shape)


def _rs_cross(ps, kinds, shard_shapes, *, name):
    n = len(ps)

    def body(*refs):
        p_refs, out_refs = refs[:n], refs[n:2 * n]
        send_sems, recv_sems, local_sems = refs[2 * n:]
        x, y, cc = _mesh_pos()
        chips = [(1 - x, y), (x, 1 - y), (1 - x, 1 - y)]
        slot = lambda chip: 2 * chip[0] + chip[1]
        local, sends = [], []
        for i in range(n):
            a, b = shard_shapes[i]
            if kinds[i] == REP:
                cp = pltpu.make_async_copy(p_refs[i], out_refs[i].at[slot((x, y))], local_sems.at[i])
                cp.start()
                local.append(cp)
            for k, chip in enumerate(chips):
                cp = pltpu.make_async_remote_copy(src_ref=_chip_view(p_refs[i], kinds[i], slot(chip), a, b),
                                                  dst_ref=out_refs[i].at[slot((x, y))], send_sem=send_sems.at[i, k],
                                                  recv_sem=recv_sems.at[i, k], device_id=(*chip, cc),
                                                  device_id_type=MESH)
                cp.start()
                sends.append(cp)
        for i in range(n):
            for k, chip in enumerate(chips):
                landed = out_refs[i].at[slot(chip)]
                pltpu.make_async_remote_copy(src_ref=landed, dst_ref=landed, send_sem=send_sems.at[i, k],
                                             recv_sem=recv_sems.at[i, k], device_id=(*chip, cc),
                                             device_id_type=MESH).wait_recv()
        for cp in sends:
            cp.wait_send()
        for cp in local:
            cp.wait()

    def out_shape(p, kind, ab):
        lh = p.shape[1 if kind == STK else 0]
        return jax.ShapeDtypeStruct((N_CHIPS, lh) + tuple(ab), p.dtype)

    return pl.pallas_call(
        body, name=name, in_specs=[HBM_SPEC] * n, out_specs=[HBM_SPEC] * n,
        out_shape=[out_shape(p, kd, ab) for p, kd, ab in zip(ps, kinds, shard_shapes)],
        scratch_shapes=[pltpu.SemaphoreType.DMA((n, 3)), pltpu.SemaphoreType.DMA((n, 3)),
                        pltpu.SemaphoreType.DMA((n,))],
    )(*ps)


def _rs_sum(p, landed, kind, ab, place, cc, *, name):
    a, b = ab
    lh = landed.shape[1]
    tr = _row_tile(a, b)
    blk = lambda f: pl.BlockSpec((None, None, tr, b), f)
    if kind == ROW:
        p_spec = pl.BlockSpec((None, tr, b), lambda l, i, w, c: (l, w[0] * (a // tr) + i, 0))
    elif kind == COL:
        p_spec = pl.BlockSpec((None, tr, b), lambda l, i, w, c: (l, i, w[0]))
    elif kind == STK:
        p_spec = blk(lambda l, i, w, c: (w[0], l, i, 0))
    else:
        p_spec = blk(lambda l, i, w, c: (0, l, i, 0))
        p = landed
    if kind == REP:
        others = [blk(lambda l, i, w, c, k=k: (k, l, i, 0)) for k in (1, 2, 3)]
    else:
        others = [blk(lambda l, i, w, c: (lax.rem(w[0] + 2, 4), l, i, 0)),
                  blk(lambda l, i, w, c: (w[0] + 1 - 2 * lax.rem(w[0], 2), l, i, 0)),
                  blk(lambda l, i, w, c: (3 - w[0], l, i, 0))]

    def body(w_ref, c_ref, p_ref, b1_ref, b2_ref, b3_ref, o_ref):
        f = lambda r: r[...].astype(F32)
        o_ref[...] = ((f(p_ref) + f(b1_ref)) + f(b2_ref)) + f(b3_ref)

    return pl.pallas_call(
        body, name=name,
        grid_spec=pltpu.PrefetchScalarGridSpec(
            num_scalar_prefetch=2, grid=(lh, a // tr), in_specs=[p_spec] + others,
            out_specs=pl.BlockSpec((None, tr, b), lambda l, i, w, c: (c[0] * lh + l, i, 0))),
        out_shape=jax.ShapeDtypeStruct((2 * lh, a, b), F32),
        compiler_params=_cparams(("parallel", "parallel")),
    )(place, cc, p, landed, landed, landed)


def _rs_share(fs, *, name):
    n = len(fs)

    def body(*refs):
        out_refs = refs[n:2 * n]
        send_sems, recv_sems = refs[2 * n:]
        x, y, cc = _mesh_pos()
        cps = []
        for i in range(n):
            lh = out_refs[i].shape[0] // 2
            mine = out_refs[i].at[pl.ds(cc * lh, lh)]
            cp = pltpu.make_async_remote_copy(src_ref=mine, dst_ref=mine, send_sem=send_sems.at[i],
                                              recv_sem=recv_sems.at[i], device_id=(x, y, 1 - cc),
                                              device_id_type=MESH)
            cp.start()
            cps.append(cp)
        for i in range(n):
            lh = out_refs[i].shape[0] // 2
            theirs = out_refs[i].at[pl.ds((1 - cc) * lh, lh)]
            pltpu.make_async_remote_copy(src_ref=theirs, dst_ref=theirs, send_sem=send_sems.at[i],
                                         recv_sem=recv_sems.at[i], device_id=(x, y, 1 - cc),
                                         device_id_type=MESH).wait_recv()
        for cp in cps:
            cp.wait_send()

    return pl.pallas_call(
        body, name=name, in_specs=[HBM_SPEC] * n, out_specs=[HBM_SPEC] * n,
        out_shape=[jax.ShapeDtypeStruct(f.shape, f.dtype) for f in fs],
        input_output_aliases={i: i for i in range(n)},
        scratch_shapes=[pltpu.SemaphoreType.DMA((n,)), pltpu.SemaphoreType.DMA((n,))],
    )(*fs)


def _reduce_scatter(gs, kinds, shard_shapes, wire_dtypes):
    x, y, cc = _mesh_pos()
    place = (2 * x + y).reshape(1).astype(jnp.int32)
    cc1 = cc.reshape(1).astype(jnp.int32)
    bufs = _rs_pair(gs, kinds, name="rs_pair")
    ps = [_rs_pair_add(g, buf, kd, cc, wd, name=f"rs_pair_add_{i}")
          for i, (g, buf, kd, wd) in enumerate(zip(gs, bufs, kinds, wire_dtypes))]
    landed = _rs_cross(ps, kinds, shard_shapes, name="rs_cross")
    fs = [_rs_sum(p, b, kd, ab, place, cc1, name=f"rs_sum_{i}")
          for i, (p, b, kd, ab) in enumerate(zip(ps, landed, kinds, shard_shapes))]
    return _rs_share(fs, name="rs_share")


def _shard_shape(axis, r, c):
    return (r // N_CHIPS, c) if axis == 0 else (r, c // N_CHIPS)


def _unstack(stacked):
    ns, depth, r, cs = stacked.shape
    return stacked.transpose(1, 2, 0, 3).reshape(depth, r, ns * cs)


def _stack(fullw):
    depth, r, c = fullw.shape
    return fullw.reshape(depth, r, N_CHIPS, c // N_CHIPS).transpose(2, 0, 1, 3)


REP_SIZES = dict(mix_norm_g=1024, ssd_conv_b=2048, ssd_dt_bias=16, ssd_a_log=16, ssd_d=16, ssd_norm_g=1024,
                 conv_dw_b=1024, conv_ln_g=1024, conv_ln_b=1024, mla_q_a_g=384, mla_kv_a_g=256, mla_q_norm_g=192,
                 mla_k_norm_g=192, xattn_norm_g=1024, mem_norm_g=1024, xattn_q_norm_g=256, xattn_k_norm_g=256,
                 ffn_norm_g=1024)
REP_WIDTH = -(-sum(REP_SIZES.values()) // LANE) * LANE


def _pack_rep(d):
    flat = jnp.concatenate([d[k] for k in REP_NAMES], axis=1)
    return jnp.pad(flat, ((0, 0), (0, REP_WIDTH - flat.shape[1])))[:, None, :]


def _unpack_rep(packed):
    out, off = {}, 0
    for k in REP_NAMES:
        out[k] = packed[:, 0, off:off + REP_SIZES[k]]
        off += REP_SIZES[k]
    return out


WEIGHT_NAMES = ('mix_norm_g', 'w_in', 'ssd_conv_w', 'ssd_conv_b', 'ssd_dt_bias', 'ssd_a_log', 'ssd_d', 'ssd_norm_g',
                'ssd_w_out', 'conv_dw_w', 'conv_dw_b', 'conv_ln_g', 'conv_ln_b', 'conv_w_out', 'mla_q_a_g',
                'mla_w_q_b', 'mla_kv_a_g', 'mla_w_kv_b', 'mla_q_norm_g', 'mla_k_norm_g', 'mla_w_o', 'gate_b', 'w_out',
                'xattn_norm_g', 'mem_norm_g', 'xattn_w_q', 'xattn_w_kv', 'xattn_q_norm_g', 'xattn_k_norm_g',
                'xattn_w_o', 'ffn_norm_g', 'ffn_w_in', 'ffn_w_out')


def kernel(x, mem, positions, *rest):
    nw = len(WEIGHT_NAMES)
    weights = dict(zip(WEIGHT_NAMES, rest[:nw]))
    target = rest[nw]
    mom_m = dict(zip(WEIGHT_NAMES, rest[nw + 1:2 * nw + 1]))
    mom_v = dict(zip(WEIGHT_NAMES, rest[2 * nw + 1:3 * nw + 1]))
    depth = weights['mix_norm_g'].shape[0]

    sharded = BIG + SMALL
    kinds = [_kind(axis, _shard_shape(axis, r, c)[1]) for _, axis, r, c in sharded]
    shard_shapes = [_shard_shape(axis, r, c) for _, axis, r, c in sharded]
    nb = len(BIG)
    small = _all_gather([weights[k] for k, _, _, _ in SMALL], kinds[nb:], name="ag_small")
    big_shards = [weights[k].astype(BF16) for k, _, _, _ in BIG]
    send0, recv0, lands0, _ = _ag_start(big_shards, kinds[:nb], 0, None, name="ag_start_0")
    lands0 = _ag_wait(send0, recv0, big_shards, lands0, kinds[:nb], 0, big_shards[0], name="ag_wait_0")
    lands0 = _ag_forward(lands0, kinds[:nb], shard_shapes[:nb], name="ag_forward_0")
    started = {l: _ag_start(big_shards, kinds[:nb], l, lands0[0], name=f"ag_start_{l}") for l in range(1, depth)}
    tokens = sum(st[3][0, 0] for st in started.values())

    def layer_weights(l, h):
        if l == 0:
            lands = lands0
        else:
            send, recv, lands, _ = started[l]
            lands = _ag_wait(send, recv, big_shards, lands, kinds[:nb], l, h, name=f"ag_wait_{l}")
            lands = _ag_forward(lands, kinds[:nb], shard_shapes[:nb], name=f"ag_forward_{l}")
        full = {k: (ld.transpose(1, 0, 2).reshape(ld.shape[1], -1) if kd == STK else ld)
                for (k, _, _, _), kd, ld in zip(BIG, kinds[:nb], lands)}
        full.update({k: w[l] for (k, _, _, _), w in zip(SMALL, small)})
        full['w_in'] = _w_in_pad(full['w_in'])
        full['mla_w_q_b'] = _qb_pad(full['mla_w_q_b'])
        return full

    rep = {k: weights[k] for k in REP_NAMES}
    loss, dx, gfull, grep = _local_step(x[0] + tokens, mem[0], positions[0], target[0], layer_weights, rep)
    loss = lax.psum(loss, ("x", "y", "c"))

    gs = [(_stack(gfull[k]) if kd == STK else gfull[k]) for (k, _, _, _), kd in zip(sharded, kinds)]
    wire = [BF16] * len(BIG) + [F32] * (len(SMALL) + 1)
    summed = _reduce_scatter(gs + [_pack_rep(grep)], kinds + [REP], shard_shapes + [(1, REP_WIDTH)], wire)
    grads = {k: g for (k, _, _, _), g in zip(sharded, summed[:-1])}
    rep_sum = summed[-1]
    grads.update(_unpack_rep(rep_sum))

    delta, new_m, new_v = {}, {}, {}
    for k, _, _, _ in sharded:
        w = weights[k]
        two_d = (w.shape[0] * w.shape[1], w.shape[2])
        d_, m_, v_ = _adamw(w.reshape(two_d), grads[k].reshape(two_d), mom_m[k].reshape(two_d),
                            mom_v[k].reshape(two_d), name="adamw_" + k)
        delta[k], new_m[k], new_v[k] = d_.reshape(w.shape), m_.reshape(w.shape), v_.reshape(w.shape)
    pack2 = lambda d: _pack_rep(d)[:, 0, :]
    d_, m_, v_ = _adamw(pack2(rep), rep_sum[:, 0, :], pack2({k: mom_m[k] for k in REP_NAMES}),
                        pack2({k: mom_v[k] for k in REP_NAMES}), name="adamw_rep")
    delta.update(_unpack_rep(d_[:, None, :]))
    new_m.update(_unpack_rep(m_[:, None, :]))
    new_v.update(_unpack_rep(v_[:, None, :]))

    return (loss, dx[None], *[grads[k] for k in WEIGHT_NAMES], *[delta[k] for k in WEIGHT_NAMES],
            *[new_m[k] for k in WEIGHT_NAMES], *[new_v[k] for k in WEIGHT_NAMES])
```

```python
import functools
import math

import jax
import jax.numpy as jnp
import numpy as np
from jax import lax
from jax.experimental import pallas as pl
from jax.experimental.pallas import tpu as pltpu

F32 = jnp.float32
BF16 = jnp.bfloat16
MESH = pl.DeviceIdType.MESH

EPS = 1e-6
CHUNK = 64
SSD_HEADS = 16
SSD_GROUPS = 4
SSD_P = 64
SSD_N = 128
MLA_HEADS = 8
MLA_NOPE = 128
MLA_ROPE = 64
MLA_V = 128
MLA_HP = 256
X_HEADS = 4
ROPE_THETA = 10000.0
ADAM_LR, ADAM_B1, ADAM_B2, ADAM_EPS, ADAM_WD, ADAM_STEP = 0.001, 0.9, 0.999, 1e-08, 0.01, 10
LANE = 128
NEG = -1e30
VMEM_MB = 1024 * 1024


def _pick(n, cap, mult=128):
    if n <= cap:
        return n
    d = (cap // mult) * mult
    while d >= mult:
        if n % d == 0:
            return d
        d -= mult
    return n


def _cparams(sem, mb=40):
    return pltpu.CompilerParams(dimension_semantics=sem, vmem_limit_bytes=mb * VMEM_MB)


def _sigmoid(x):
    return 1.0 / (1.0 + jnp.exp(-x))


def _silu(x):
    return x * _sigmoid(x)


def _dsilu(x):
    s = _sigmoid(x)
    return s * (1.0 + x * (1.0 - s))


def _softplus(x):
    return jnp.maximum(x, 0.0) + jnp.log(1.0 + jnp.exp(-jnp.abs(x)))


def _matmul(a, b, *, ta=False, tb=False, out_dtype=F32, add=None, into=None, name):
    if ta:
        kdim, m = a.shape
    else:
        m, kdim = a.shape
    if tb:
        n, kb = b.shape
    else:
        kb, n = b.shape
    assert kb == kdim, (a.shape, b.shape, ta, tb)
    tm = _pick(m, 512, 128) if ta else _pick(m, 1024, 8)
    tn = _pick(n, 1024 if n <= 1024 else 512, 128)
    tk = _pick(kdim, 2048, 128)
    nk = kdim // tk
    dims = (((0 if ta else 1,), (1 if tb else 0,)), ((), ()))

    has_add = add is not None
    has_stack = into is not None and into[0] is not None

    def body(a_ref, b_ref, *rest):
        add_ref = rest[0] if has_add else None
        o_ref = rest[has_add + has_stack]
        acc = rest[has_add + has_stack + 1:]
        part = lax.dot_general(a_ref[...].astype(BF16), b_ref[...].astype(BF16), dims,
                               preferred_element_type=F32)

        def finish(total):
            if has_add:
                total = total + add_ref[...]
            o_ref[...] = total.astype(o_ref.dtype)

        if nk == 1:
            finish(part)
        else:
            acc_ref, = acc
            k = pl.program_id(2)

            @pl.when(k == 0)
            def _():
                acc_ref[...] = part

            @pl.when(k > 0)
            def _():
                acc_ref[...] += part

            @pl.when(k == nk - 1)
            def _():
                finish(acc_ref[...])

    a_spec = pl.BlockSpec((tk, tm), lambda i, j, k: (k, i)) if ta else pl.BlockSpec((tm, tk), lambda i, j, k: (i, k))
    b_spec = pl.BlockSpec((tn, tk), lambda i, j, k: (j, k)) if tb else pl.BlockSpec((tk, tn), lambda i, j, k: (k, j))
    o_spec = pl.BlockSpec((tm, tn), lambda i, j, k: (i, j))
    operands = [a, b] + ([add] if has_add else [])
    in_specs = [a_spec, b_spec] + ([o_spec] if has_add else [])
    if into is None:
        out_spec, out_shape, aliases = o_spec, jax.ShapeDtypeStruct((m, n), out_dtype), {}
    else:
        stack, layer, depth = into
        out_spec = pl.BlockSpec((None, tm, tn), lambda i, j, k: (layer, i, j))
        out_shape = jax.ShapeDtypeStruct((depth, m, n), out_dtype)
        aliases = {}
        if stack is not None:
            aliases = {len(operands): 0}
            operands.append(stack)
            in_specs.append(HBM_SPEC)
    return pl.pallas_call(
        body, name=name, grid=(m // tm, n // tn, nk),
        in_specs=in_specs, out_specs=out_spec, out_shape=out_shape, input_output_aliases=aliases,
        scratch_shapes=[] if nk == 1 else [pltpu.VMEM((tm, tn), F32)],
        compiler_params=_cparams(("parallel", "parallel", "arbitrary"), 48),
    )(*operands)


def _rms_fwd(x, g, *, out_dtype, name):
    r, w = x.shape
    tr = _pick(r, 512, 8)

    def body(x_ref, g_ref, o_ref):
        xv = x_ref[...]
        rstd = lax.rsqrt(jnp.mean(xv * xv, axis=-1, keepdims=True) + EPS)
        o_ref[...] = (xv * rstd * g_ref[...]).astype(o_ref.dtype)

    return pl.pallas_call(
        body, name=name, grid=(r // tr,),
        in_specs=[pl.BlockSpec((tr, w), lambda i: (i, 0)), pl.BlockSpec((1, w), lambda i: (0, 0))],
        out_specs=pl.BlockSpec((tr, w), lambda i: (i, 0)),
        out_shape=jax.ShapeDtypeStruct((r, w), out_dtype),
        compiler_params=_cparams(("parallel",)),
    )(x, g)


def _rms_bwd(x, g, dy, *, dx_dtype, name, add=None):
    r, w = x.shape
    tr = _pick(r, 512, 8)
    has_add = add is not None

    def body(x_ref, g_ref, dy_ref, *rest):
        if has_add:
            add_ref, dx_ref, dg_ref = rest
        else:
            dx_ref, dg_ref = rest
        xv = x_ref[...]
        dyv = dy_ref[...].astype(F32)
        rstd = lax.rsqrt(jnp.mean(xv * xv, axis=-1, keepdims=True) + EPS)
        xh = xv * rstd
        dyg = dyv * g_ref[...]
        dx = rstd * (dyg - xh * jnp.mean(dyg * xh, axis=-1, keepdims=True))
        if has_add:
            dx = dx + add_ref[...]
        dx_ref[...] = dx.astype(dx_ref.dtype)
        part = jnp.sum(dyv * xh, axis=0, keepdims=True)

        @pl.when(pl.program_id(0) == 0)
        def _():
            dg_ref[...] = part

        @pl.when(pl.program_id(0) > 0)
        def _():
            dg_ref[...] += part

    row = pl.BlockSpec((tr, w), lambda i: (i, 0))
    vec = pl.BlockSpec((1, w), lambda i: (0, 0))
    ins = [x, g, dy] + ([add] if has_add else [])
    return pl.pallas_call(
        body, name=name, grid=(r // tr,),
        in_specs=[row, vec, row] + ([row] if has_add else []),
        out_specs=[row, vec],
        out_shape=[jax.ShapeDtypeStruct((r, w), dx_dtype), jax.ShapeDtypeStruct((1, w), F32)],
        compiler_params=_cparams(("arbitrary",)),
    )(*ins)


def _ln_silu_fwd(x, g, b, *, name):
    r, w = x.shape
    tr = _pick(r, 512, 8)

    def body(x_ref, g_ref, b_ref, o_ref):
        xv = x_ref[...]
        mu = jnp.mean(xv, axis=-1, keepdims=True)
        xc = xv - mu
        rstd = lax.rsqrt(jnp.mean(xc * xc, axis=-1, keepdims=True) + EPS)
        o_ref[...] = _silu(xc * rstd * g_ref[...] + b_ref[...]).astype(o_ref.dtype)

    row = pl.BlockSpec((tr, w), lambda i: (i, 0))
    vec = pl.BlockSpec((1, w), lambda i: (0, 0))
    return pl.pallas_call(
        body, name=name, grid=(r // tr,), in_specs=[row, vec, vec], out_specs=row,
        out_shape=jax.ShapeDtypeStruct((r, w), BF16), compiler_params=_cparams(("parallel",)),
    )(x, g, b)


def _ln_silu_bwd(x, g, b, dy, *, name):
    r, w = x.shape
    tr = _pick(r, 512, 8)

    def body(x_ref, g_ref, b_ref, dy_ref, dx_ref, dg_ref, db_ref):
        xv = x_ref[...]
        mu = jnp.mean(xv, axis=-1, keepdims=True)
        xc = xv - mu
        rstd = lax.rsqrt(jnp.mean(xc * xc, axis=-1, keepdims=True) + EPS)
        xh = xc * rstd
        pre = xh * g_ref[...] + b_ref[...]
        dpre = dy_ref[...].astype(F32) * _dsilu(pre)
        dxh = dpre * g_ref[...]
        dx_ref[...] = rstd * (dxh - jnp.mean(dxh, axis=-1, keepdims=True)
                              - xh * jnp.mean(dxh * xh, axis=-1, keepdims=True))
        pg = jnp.sum(dpre * xh, axis=0, keepdims=True)
        pb = jnp.sum(dpre, axis=0, keepdims=True)

        @pl.when(pl.program_id(0) == 0)
        def _():
            dg_ref[...] = pg
            db_ref[...] = pb

        @pl.when(pl.program_id(0) > 0)
        def _():
            dg_ref[...] += pg
            db_ref[...] += pb

    row = pl.BlockSpec((tr, w), lambda i: (i, 0))
    vec = pl.BlockSpec((1, w), lambda i: (0, 0))
    return pl.pallas_call(
        body, name=name, grid=(r // tr,), in_specs=[row, vec, vec, row], out_specs=[row, vec, vec],
        out_shape=[jax.ShapeDtypeStruct((r, w), F32), jax.ShapeDtypeStruct((1, w), F32),
                   jax.ShapeDtypeStruct((1, w), F32)],
        compiler_params=_cparams(("arbitrary",)),
    )(x, g, b, dy)


CONV_PAD = 32
CONV_T = 256


def _conv_fwd(src, w, b, *, glu, act, name):
    s = src.shape[0]
    k, c = w.shape
    tc = LANE
    ncb = c // tc
    tt = _pick(s, CONV_T, 8)
    assert k - 1 <= CONV_PAD

    def body(*refs):
        if glu:
            a_ref, g_ref, w_ref, b_ref = refs[:4]
            outs = refs[4:-1]
        else:
            a_ref, w_ref, b_ref = refs[:3]
            outs = refs[3:-1]
        xp = refs[-1]
        xp[0:CONV_PAD, :] = jnp.zeros((CONV_PAD, tc), F32)
        if glu:
            xp[CONV_PAD:CONV_PAD + s, :] = a_ref[...] * _sigmoid(g_ref[...])
        else:
            xp[CONV_PAD:CONV_PAD + s, :] = a_ref[...]
        wv = w_ref[...]
        bv = b_ref[...]
        for t0 in range(0, s, tt):
            acc = jnp.zeros((tt, tc), F32) + bv
            for kk in range(k):
                off = CONV_PAD + t0 - (k - 1) + kk
                acc = acc + wv[kk:kk + 1, :] * xp[off:off + tt, :]
            outs[0][t0:t0 + tt, :] = acc
            if act:
                outs[1][t0:t0 + tt, :] = _silu(acc)

    col = pl.BlockSpec((s, tc), lambda j: (0, j))
    in_specs = [col, pl.BlockSpec((s, tc), lambda j: (0, j + ncb))] if glu else [col]
    in_specs += [pl.BlockSpec((k, tc), lambda j: (0, j)), pl.BlockSpec((1, tc), lambda j: (0, j))]
    n_out = 2 if act else 1
    res = pl.pallas_call(
        body, name=name, grid=(ncb,), in_specs=in_specs,
        out_specs=[col] * n_out,
        out_shape=[jax.ShapeDtypeStruct((s, c), F32)] * n_out,
        scratch_shapes=[pltpu.VMEM((CONV_PAD + s, tc), F32)],
        compiler_params=_cparams(("parallel",), 48),
    )(*([src, src] if glu else [src]), w, b)
    return res


def _conv_bwd(src, w, b, dy, pre, *, glu, act, name):
    s = src.shape[0]
    k, c = w.shape
    tc = LANE
    ncb = c // tc
    tt = _pick(s, CONV_T, 8)

    def body(*refs):
        i = 0
        a_ref = refs[i]; i += 1
        if glu:
            g_ref = refs[i]; i += 1
        w_ref = refs[i]; i += 1
        dy_ref = refs[i]; i += 1
        if act:
            pre_ref = refs[i]; i += 1
        da_ref = refs[i]; i += 1
        if glu:
            dg_ref = refs[i]; i += 1
        dw_ref = refs[i]; db_ref = refs[i + 1]
        xp, dp = refs[-2], refs[-1]
        xp[0:CONV_PAD, :] = jnp.zeros((CONV_PAD, tc), F32)
        if glu:
            xp[CONV_PAD:CONV_PAD + s, :] = a_ref[...] * _sigmoid(g_ref[...])
        else:
            xp[CONV_PAD:CONV_PAD + s, :] = a_ref[...]
        dp[s:s + CONV_PAD, :] = jnp.zeros((CONV_PAD, tc), F32)
        if act:
            dp[0:s, :] = dy_ref[...].astype(F32) * _dsilu(pre_ref[...])
        else:
            dp[0:s, :] = dy_ref[...].astype(F32)
        wv = w_ref[...]
        dws = [jnp.zeros((1, tc), F32) for _ in range(k)]
        dbs = jnp.zeros((1, tc), F32)
        for t0 in range(0, s, tt):
            acc = jnp.zeros((tt, tc), F32)
            dcur = dp[t0:t0 + tt, :]
            dbs = dbs + jnp.sum(dcur, axis=0, keepdims=True)
            for kk in range(k):
                acc = acc + wv[kk:kk + 1, :] * dp[t0 + (k - 1) - kk:t0 + (k - 1) - kk + tt, :]
                off = CONV_PAD + t0 - (k - 1) + kk
                dws[kk] = dws[kk] + jnp.sum(dcur * xp[off:off + tt, :], axis=0, keepdims=True)
            if glu:
                av = a_ref[t0:t0 + tt, :]
                sg = _sigmoid(g_ref[t0:t0 + tt, :])
                da_ref[t0:t0 + tt, :] = (acc * sg).astype(da_ref.dtype)
                dg_ref[t0:t0 + tt, :] = (acc * av * sg * (1.0 - sg)).astype(dg_ref.dtype)
            else:
                da_ref[t0:t0 + tt, :] = acc.astype(da_ref.dtype)
        for kk in range(k):
            dw_ref[kk:kk + 1, :] = dws[kk]
        db_ref[...] = dbs

    col = pl.BlockSpec((s, tc), lambda j: (0, j))
    in_specs = [col] + ([pl.BlockSpec((s, tc), lambda j: (0, j + ncb))] if glu else [])
    in_specs += [pl.BlockSpec((k, tc), lambda j: (0, j)), col] + ([col] if act else [])
    ins = ([src, src] if glu else [src]) + [w, dy] + ([pre] if act else [])
    out_specs = [col] + ([col] if glu else []) + [pl.BlockSpec((k, tc), lambda j: (0, j)),
                                                  pl.BlockSpec((1, tc), lambda j: (0, j))]
    out_shape = [jax.ShapeDtypeStruct((s, c), BF16)] * (2 if glu else 1) + [
        jax.ShapeDtypeStruct((k, c), F32), jax.ShapeDtypeStruct((1, c), F32)]
    return pl.pallas_call(
        body, name=name, grid=(ncb,), in_specs=in_specs, out_specs=out_specs, out_shape=out_shape,
        scratch_shapes=[pltpu.VMEM((CONV_PAD + s, tc), F32), pltpu.VMEM((s + CONV_PAD, tc), F32)],
        compiler_params=_cparams(("parallel",), 56),
    )(*ins)


def _ssd_consts():
    e = np.zeros((LANE, SSD_HEADS * SSD_P), np.float32)
    for h in range(SSD_HEADS):
        e[h, h * SSD_P:(h + 1) * SSD_P] = 1.0
    ltri = np.tril(np.ones((CHUNK, CHUNK), np.float32))
    return jnp.asarray(e), jnp.asarray(e.T.copy()), jnp.asarray(ltri), jnp.asarray(ltri.T.copy())


def _split3(x):
    hi = x.astype(BF16)
    r = x - hi.astype(F32)
    mid = r.astype(BF16)
    lo = (r - mid.astype(F32)).astype(BF16)
    return hi, mid, lo


def _dot_sel(x, sel, x_left=True):
    sb = sel.astype(BF16)
    out = None
    for part in _split3(x):
        t = jnp.dot(part, sb, preferred_element_type=F32) if x_left else jnp.dot(sb, part, preferred_element_type=F32)
        out = t if out is None else out + t
    return out


def _ssd_chunk_terms(dtr_ref, dtrt_ref, bias_ref, biast_ref, alog_ref, alogt_ref, e_ref, ltri_ref, utri_ref):
    a_neg = -jnp.exp(alog_ref[...])
    dt = _softplus(dtr_ref[...] + bias_ref[...])
    a = dt * a_neg
    s = _dot_sel(a, ltri_ref[...], x_left=False)
    dtt = _softplus(dtrt_ref[...] + biast_ref[...])
    st = _dot_sel(dtt * (-jnp.exp(alogt_ref[...])), utri_ref[...])
    ev = e_ref[...]
    s_x = _dot_sel(s, ev)
    dt_x = _dot_sel(dt, ev)
    return a_neg, dt, s, st, s_x, dt_x


def _ssd_decay(s, st, h, tril):
    seg = s[:, h:h + 1] - st[h:h + 1, :]
    return jnp.exp(jnp.where(tril, seg, NEG))


def _ssd_decay_t(s, st, h, triu):
    seg = st[h:h + 1, :] - s[:, h:h + 1]
    return jnp.exp(jnp.where(triu, seg, NEG))


def _head_masks():
    lane = lax.broadcasted_iota(jnp.int32, (1, SSD_P * 4), 1)
    return [((lane >= r * SSD_P) & (lane < (r + 1) * SSD_P)).astype(F32) for r in range(4)]


def _ssd_scan_fwd(xbc, dtr, dtrt, bias, biast, alog, alogt, dskip, *, name):
    s_len = xbc.shape[0]
    nc = s_len // CHUNK
    e, et, ltri, utri = _ssd_consts()
    gw = SSD_P * 4

    def body(xs_ref, b_ref, c_ref, dtr_ref, dtrt_ref, bias_ref, biast_ref, alog_ref, alogt_ref, d_ref,
             e_ref, ltri_ref, utri_ref, y_ref, prev_ref, state):
        @pl.when(pl.program_id(0) == 0)
        def _():
            state[...] = jnp.zeros_like(state)

        a_neg, dt, s, st, s_x, dt_x = _ssd_chunk_terms(dtr_ref, dtrt_ref, bias_ref, biast_ref, alog_ref,
                                                       alogt_ref, e_ref, ltri_ref, utri_ref)
        s_last = s_x[CHUNK - 1:CHUNK, :]
        es_x = jnp.exp(s_x)
        w_x = jnp.exp(s_last - s_x)
        cd_x = jnp.exp(s_last)
        d_x = _dot_sel(jnp.broadcast_to(d_ref[...], (8, LANE)), e_ref[...])[0:1, :]
        xs = xs_ref[...]
        xv = xs * dt_x
        row = lax.broadcasted_iota(jnp.int32, (CHUNK, CHUNK), 0)
        colm = lax.broadcasted_iota(jnp.int32, (CHUNK, CHUNK), 1)
        tril = colm <= row
        masks = _head_masks()
        for g in range(SSD_GROUPS):
            gs = slice(g * gw, (g + 1) * gw)
            bg = b_ref[:, g * SSD_N:(g + 1) * SSD_N].astype(BF16)
            cg = c_ref[:, g * SSD_N:(g + 1) * SSD_N].astype(BF16)
            xg = xv[:, gs]
            hg = state[g]
            prev_ref[g] = hg
            cb = lax.dot_general(cg, bg, (((1,), (1,)), ((), ())), preferred_element_type=F32)
            yg = jnp.dot(cg, hg.astype(BF16), preferred_element_type=F32) * es_x[:, gs]
            for r in range(4):
                m = (cb * _ssd_decay(s, st, g * 4 + r, tril)).astype(BF16)
                yg = yg + jnp.dot(m, (xg * masks[r]).astype(BF16), preferred_element_type=F32)
            y_ref[:, gs] = yg + d_x[:, gs] * xs[:, gs]
            upd = lax.dot_general(bg, (xg * w_x[:, gs]).astype(BF16), (((0,), (0,)), ((), ())),
                                  preferred_element_type=F32)
            state[g] = hg * cd_x[:, gs] + upd

    nh = LANE
    chunk_row = lambda w, cb: pl.BlockSpec((CHUNK, w), lambda i, cb=cb: (i, cb))
    full = lambda a: pl.BlockSpec(a.shape, lambda i: (0,) * a.ndim)
    in_specs = [chunk_row(1024, 0), chunk_row(512, 2), chunk_row(512, 3), chunk_row(nh, 0),
                pl.BlockSpec((None, nh, CHUNK), lambda i: (i, 0, 0)),
                full(bias), full(biast), full(alog), full(alogt), full(dskip), full(e), full(ltri), full(utri)]
    return pl.pallas_call(
        body, name=name, grid=(nc,), in_specs=in_specs,
        out_specs=[pl.BlockSpec((CHUNK, 1024), lambda i: (i, 0)),
                   pl.BlockSpec((None, SSD_GROUPS, SSD_N, gw), lambda i: (i, 0, 0, 0))],
        out_shape=[jax.ShapeDtypeStruct((s_len, 1024), F32),
                   jax.ShapeDtypeStruct((nc, SSD_GROUPS, SSD_N, gw), F32)],
        scratch_shapes=[pltpu.VMEM((SSD_GROUPS, SSD_N, gw), F32)],
        compiler_params=_cparams(("arbitrary",)),
    )(xbc, xbc, xbc, dtr, dtrt, bias, biast, alog, alogt, dskip, e, ltri, utri)


def _ssd_scan_bwd(xbc, dtr, dtrt, bias, biast, alog, alogt, dskip, prev, dy, *, name):
    s_len = xbc.shape[0]
    nc = s_len // CHUNK
    e, et, ltri, utri = _ssd_consts()
    gw = SSD_P * 4

    def body(xs_ref, b_ref, c_ref, dtr_ref, dtrt_ref, bias_ref, biast_ref, alog_ref, alogt_ref, d_ref,
             e_ref, et_ref, ltri_ref, utri_ref, prev_ref, dy_ref,
             dxs_ref, db_ref, dc_ref, ddtr_ref, dalog_ref, dbias_ref, dd_ref, dstate):
        step = pl.program_id(0)

        @pl.when(step == 0)
        def _():
            dstate[...] = jnp.zeros_like(dstate)

        a_neg, dt, s, st, s_x, dt_x = _ssd_chunk_terms(dtr_ref, dtrt_ref, bias_ref, biast_ref, alog_ref,
                                                       alogt_ref, e_ref, ltri_ref, utri_ref)
        s_last = s_x[CHUNK - 1:CHUNK, :]
        es_x = jnp.exp(s_x)
        w_x = jnp.exp(s_last - s_x)
        cd_x = jnp.exp(s_last)
        d_x = _dot_sel(jnp.broadcast_to(d_ref[...], (8, LANE)), e_ref[...])[0:1, :]
        xs = xs_ref[...]
        xv = xs * dt_x
        dyv = dy_ref[...]
        row = lax.broadcasted_iota(jnp.int32, (CHUNK, CHUNK), 0)
        colm = lax.broadcasted_iota(jnp.int32, (CHUNK, CHUNK), 1)
        tril = colm <= row
        masks = _head_masks()
        is_last = lax.broadcasted_iota(jnp.int32, (CHUNK, 1), 0) == CHUNK - 1
        nt = (((1,), (1,)), ((), ()))
        tn = (((0,), (0,)), ((), ()))
        ds_parts, ddt_parts = [], []
        head_lane = lax.broadcasted_iota(jnp.int32, (CHUNK, LANE), 1)
        triu = colm >= row
        ds_diag = jnp.zeros((CHUNK, LANE), F32)
        for g in range(SSD_GROUPS):
            gs = slice(g * gw, (g + 1) * gw)
            bg = b_ref[:, g * SSD_N:(g + 1) * SSD_N].astype(BF16)
            cg = c_ref[:, g * SSD_N:(g + 1) * SSD_N].astype(BF16)
            xg = xv[:, gs]
            xgb = xg.astype(BF16)
            hg = prev_ref[g]
            hgb = hg.astype(BF16)
            dsg = dstate[g]
            dsgb = dsg.astype(BF16)
            dyg = dyv[:, gs]
            dye = (dyg * es_x[:, gs]).astype(BF16)
            xw = (xg * w_x[:, gs]).astype(BF16)
            cb = lax.dot_general(cg, bg, nt, preferred_element_type=F32)
            cbt = lax.dot_general(bg, cg, nt, preferred_element_type=F32)
            dcg = lax.dot_general(dye, hgb, nt, preferred_element_type=F32)
            dh = lax.dot_general(cg, dye, tn, preferred_element_type=F32)
            bds = jnp.dot(bg, dsgb, preferred_element_type=F32)
            yoff = es_x[:, gs] * jnp.dot(cg, hgb, preferred_element_type=F32)
            dx_state = w_x[:, gs] * bds
            dbg = lax.dot_general(xw, dsgb, nt, preferred_element_type=F32)
            dxd = jnp.zeros((CHUNK, gw), F32)
            dcb = jnp.zeros((CHUNK, CHUNK), F32)
            for r in range(4):
                dec = _ssd_decay(s, st, g * 4 + r, tril)
                mf = cb * dec
                m = mf.astype(BF16)
                dym = (dyg * masks[r]).astype(BF16)
                dm = lax.dot_general(dym, xgb, nt, preferred_element_type=F32)
                dxd = dxd + lax.dot_general(m, dym, tn, preferred_element_type=F32)
                dcb = dcb + dm * dec
                dmt = lax.dot_general(xgb, dym, nt, preferred_element_type=F32)
                rc = (jnp.sum(dm * mf, axis=1, keepdims=True)
                      - jnp.sum(dmt * cbt * _ssd_decay_t(s, st, g * 4 + r, triu), axis=1, keepdims=True))
                ds_diag = ds_diag + jnp.where(head_lane == g * 4 + r, rc, 0.0)
            dcbb = dcb.astype(BF16)
            dcg = dcg + jnp.dot(dcbb, bg, preferred_element_type=F32)
            dbg = dbg + lax.dot_general(dcbb, cg, tn, preferred_element_type=F32)
            dxg = dxd + dx_state
            extra = (jnp.sum(xg * dx_state, axis=0, keepdims=True)
                     + cd_x[:, gs] * jnp.sum(dsg * hg, axis=0, keepdims=True))
            ds_parts.append(dyg * yoff - xg * dx_state + jnp.where(is_last, extra, 0.0))
            ddt_parts.append(dxg * xs[:, gs])
            dxs_ref[:, gs] = dxg * dt_x[:, gs] + d_x[:, gs] * dyg
            db_ref[:, g * SSD_N:(g + 1) * SSD_N] = dbg
            dc_ref[:, g * SSD_N:(g + 1) * SSD_N] = dcg
            dstate[g] = cd_x[:, gs] * dsg + dh
        etv = et_ref[...]
        ds = ds_diag + _dot_sel(jnp.concatenate(ds_parts, axis=1), etv)
        da = _dot_sel(ds, utri_ref[...], x_left=False)
        ddt = da * a_neg + _dot_sel(jnp.concatenate(ddt_parts, axis=1), etv)
        ddtr = ddt * _sigmoid(dtr_ref[...] + bias_ref[...])
        ddtr_ref[...] = ddtr.astype(ddtr_ref.dtype)
        p_alog = jnp.sum(da * dt, axis=0, keepdims=True) * a_neg
        p_bias = jnp.sum(ddtr, axis=0, keepdims=True)
        p_d = _dot_sel(jnp.broadcast_to(jnp.sum(dyv * xs, axis=0, keepdims=True), (8, SSD_HEADS * SSD_P)),
                       etv)[0:1, :]

        @pl.when(step == 0)
        def _():
            dalog_ref[...] = p_alog
            dbias_ref[...] = p_bias
            dd_ref[...] = p_d

        @pl.when(step > 0)
        def _():
            dalog_ref[...] += p_alog
            dbias_ref[...] += p_bias
            dd_ref[...] += p_d

    nh = LANE
    rev = lambda i: nc - 1 - i
    chunk_row = lambda w, cb: pl.BlockSpec((CHUNK, w), lambda i, cb=cb: (rev(i), cb))
    full = lambda a: pl.BlockSpec(a.shape, lambda i: (0,) * a.ndim)
    vec = pl.BlockSpec((1, nh), lambda i: (0, 0))
    in_specs = [chunk_row(1024, 0), chunk_row(512, 2), chunk_row(512, 3), chunk_row(nh, 0),
                pl.BlockSpec((None, nh, CHUNK), lambda i: (rev(i), 0, 0)),
                full(bias), full(biast), full(alog), full(alogt), full(dskip), full(e), full(et), full(ltri),
                full(utri),
                pl.BlockSpec((None, SSD_GROUPS, SSD_N, gw), lambda i: (rev(i), 0, 0, 0)), chunk_row(1024, 0)]
    return pl.pallas_call(
        body, name=name, grid=(nc,), in_specs=in_specs,
        out_specs=[chunk_row(1024, 0), chunk_row(512, 0), chunk_row(512, 0), chunk_row(nh, 0), vec, vec, vec],
        out_shape=[jax.ShapeDtypeStruct((s_len, 1024), F32), jax.ShapeDtypeStruct((s_len, 512), F32),
                   jax.ShapeDtypeStruct((s_len, 512), F32), jax.ShapeDtypeStruct((s_len, nh), BF16),
                   jax.ShapeDtypeStruct((1, nh), F32), jax.ShapeDtypeStruct((1, nh), F32),
                   jax.ShapeDtypeStruct((1, nh), F32)],
        scratch_shapes=[pltpu.VMEM((SSD_GROUPS, SSD_N, gw), F32)],
        compiler_params=_cparams(("arbitrary",)),
    )(xbc, xbc, xbc, dtr, dtrt, bias, biast, alog, alogt, dskip, e, et, ltri, utri, prev, dy)


def _ssd_gate_fwd(y, z, g, *, name):
    r, w = y.shape
    tr = _pick(r, 512, 8)
    gw = w // SSD_GROUPS

    def body(y_ref, z_ref, g_ref, o_ref):
        for k in range(SSD_GROUPS):
            cs = slice(k * gw, (k + 1) * gw)
            t = y_ref[:, cs] * _silu(z_ref[:, cs])
            rstd = lax.rsqrt(jnp.mean(t * t, axis=-1, keepdims=True) + EPS)
            o_ref[:, cs] = (t * rstd * g_ref[:, cs]).astype(o_ref.dtype)

    row = pl.BlockSpec((tr, w), lambda i: (i, 0))
    vec = pl.BlockSpec((1, w), lambda i: (0, 0))
    return pl.pallas_call(
        body, name=name, grid=(r // tr,), in_specs=[row, row, vec], out_specs=row,
        out_shape=jax.ShapeDtypeStruct((r, w), BF16), compiler_params=_cparams(("parallel",)),
    )(y, z, g)


def _ssd_gate_bwd(y, z, g, do, *, name):
    r, w = y.shape
    tr = _pick(r, 512, 8)
    gw = w // SSD_GROUPS

    def body(y_ref, z_ref, g_ref, do_ref, dy_ref, dz_ref, dg_ref):
        parts = []
        for k in range(SSD_GROUPS):
            cs = slice(k * gw, (k + 1) * gw)
            yv = y_ref[:, cs]
            zv = z_ref[:, cs]
            sz = _silu(zv)
            t = yv * sz
            rstd = lax.rsqrt(jnp.mean(t * t, axis=-1, keepdims=True) + EPS)
            th = t * rstd
            dov = do_ref[:, cs].astype(F32)
            dog = dov * g_ref[:, cs]
            dt = rstd * (dog - th * jnp.mean(dog * th, axis=-1, keepdims=True))
            dy_ref[:, cs] = dt * sz
            dz_ref[:, cs] = (dt * yv * _dsilu(zv)).astype(dz_ref.dtype)
            parts.append(jnp.sum(dov * th, axis=0, keepdims=True))
        pg = jnp.concatenate(parts, axis=1)

        @pl.when(pl.program_id(0) == 0)
        def _():
            dg_ref[...] = pg

        @pl.when(pl.program_id(0) > 0)
        def _():
            dg_ref[...] += pg

    row = pl.BlockSpec((tr, w), lambda i: (i, 0))
    vec = pl.BlockSpec((1, w), lambda i: (0, 0))
    return pl.pallas_call(
        body, name=name, grid=(r // tr,), in_specs=[row, row, vec, row], out_specs=[row, row, vec],
        out_shape=[jax.ShapeDtypeStruct((r, w), F32), jax.ShapeDtypeStruct((r, w), BF16),
                   jax.ShapeDtypeStruct((1, w), F32)],
        compiler_params=_cparams(("arbitrary",)),
    )(y, z, g, do)


def _rope_swap(x):
    lane = lax.broadcasted_iota(jnp.int32, x.shape, 1)
    lo = pltpu.roll(x, 96, 1)
    hi = pltpu.roll(x, 32, 1)
    return jnp.where(lane < 32, lo, jnp.where(lane < 64, hi, 0.0))


def _norm_part(v, g, n):
    rstd = lax.rsqrt(jnp.sum(v * v, axis=-1, keepdims=True) * (1.0 / n) + EPS)
    xh = v * rstd
    return xh * g, xh, rstd


def _norm_part_bwd(dout, g, xh, rstd, n):
    dg = dout * g
    return rstd * (dg - xh * (jnp.sum(dg * xh, axis=-1, keepdims=True) * (1.0 / n)))


def _mla_prep_fwd(q, kv, krr, cs, sn, gq, gk, *, name):
    s = q.shape[0]
    tr = _pick(s, 256, 8)
    hp = MLA_HP

    def body(q_ref, kv_ref, krr_ref, cs_ref, sn_ref, gq_ref, gk_ref, qf_ref, kf_ref, v_ref):
        csv, snv = cs_ref[...], sn_ref[...]
        gqn, gqr = gq_ref[:, 0:128], gq_ref[:, 128:256]
        gkn, gkr = gk_ref[:, 0:128], gk_ref[:, 128:256]
        kr, _, _ = _norm_part(krr_ref[...], gkr, MLA_ROPE)
        kr = (kr * csv + _rope_swap(kr) * snv).astype(BF16)
        for h in range(MLA_HEADS):
            qn, _, _ = _norm_part(q_ref[:, h * hp:h * hp + 128], gqn, MLA_NOPE)
            qr, _, _ = _norm_part(q_ref[:, h * hp + 128:(h + 1) * hp], gqr, MLA_ROPE)
            qr = qr * csv + _rope_swap(qr) * snv
            qf_ref[h, :, 0:128] = qn.astype(BF16)
            qf_ref[h, :, 128:256] = qr.astype(BF16)
            kn, _, _ = _norm_part(kv_ref[:, h * hp:h * hp + 128], gkn, MLA_NOPE)
            kf_ref[h, :, 0:128] = kn.astype(BF16)
            kf_ref[h, :, 128:256] = kr
            v_ref[h] = kv_ref[:, h * hp + 128:(h + 1) * hp].astype(BF16)

    row = lambda w: pl.BlockSpec((tr, w), lambda i: (i, 0))
    vec = pl.BlockSpec((1, hp), lambda i: (0, 0))
    hrow = lambda w: pl.BlockSpec((MLA_HEADS, tr, w), lambda i: (0, i, 0))
    return pl.pallas_call(
        body, name=name, grid=(s // tr,),
        in_specs=[row(MLA_HEADS * hp), row(MLA_HEADS * hp), row(128), row(128), row(128), vec, vec],
        out_specs=[hrow(hp), hrow(hp), hrow(128)],
        out_shape=[jax.ShapeDtypeStruct((MLA_HEADS, s, hp), BF16), jax.ShapeDtypeStruct((MLA_HEADS, s, hp), BF16),
                   jax.ShapeDtypeStruct((MLA_HEADS, s, 128), BF16)],
        compiler_params=_cparams(("parallel",)),
    )(q, kv, krr, cs, sn, gq, gk)


def _mla_prep_bwd(q, kv, krr, cs, sn, gq, gk, dqf, dkf, dv, *, name):
    s = q.shape[0]
    tr = _pick(s, 256, 8)
    hp = MLA_HP

    def body(q_ref, kv_ref, krr_ref, cs_ref, sn_ref, gq_ref, gk_ref, dqf_ref, dkf_ref, dv_ref,
             dq_ref, dkv_ref, dkrr_ref, dgq_ref, dgk_ref):
        csv, snv = cs_ref[...], sn_ref[...]
        gqn, gqr = gq_ref[:, 0:128], gq_ref[:, 128:256]
        gkn, gkr = gk_ref[:, 0:128], gk_ref[:, 128:256]
        _, krh, krs = _norm_part(krr_ref[...], gkr, MLA_ROPE)
        dkr_sum = jnp.zeros((tr, 128), F32)
        pgqn = jnp.zeros((1, 128), F32)
        pgqr = jnp.zeros((1, 128), F32)
        pgkn = jnp.zeros((1, 128), F32)
        for h in range(MLA_HEADS):
            _, qnh, qns = _norm_part(q_ref[:, h * hp:h * hp + 128], gqn, MLA_NOPE)
            _, qrh, qrs = _norm_part(q_ref[:, h * hp + 128:(h + 1) * hp], gqr, MLA_ROPE)
            dqn = dqf_ref[h, :, 0:128]
            drr = dqf_ref[h, :, 128:256]
            dqr = drr * csv + _rope_swap(drr * snv)
            dq_ref[:, h * hp:h * hp + 128] = _norm_part_bwd(dqn, gqn, qnh, qns, MLA_NOPE).astype(dq_ref.dtype)
            dq_ref[:, h * hp + 128:(h + 1) * hp] = _norm_part_bwd(dqr, gqr, qrh, qrs, MLA_ROPE).astype(dq_ref.dtype)
            pgqn = pgqn + jnp.sum(dqn * qnh, axis=0, keepdims=True)
            pgqr = pgqr + jnp.sum(dqr * qrh, axis=0, keepdims=True)
            _, knh, kns = _norm_part(kv_ref[:, h * hp:h * hp + 128], gkn, MLA_NOPE)
            dkn = dkf_ref[h, :, 0:128]
            dkv_ref[:, h * hp:h * hp + 128] = _norm_part_bwd(dkn, gkn, knh, kns, MLA_NOPE).astype(dkv_ref.dtype)
            dkv_ref[:, h * hp + 128:(h + 1) * hp] = dv_ref[h].astype(dkv_ref.dtype)
            pgkn = pgkn + jnp.sum(dkn * knh, axis=0, keepdims=True)
            dkr_sum = dkr_sum + dkf_ref[h, :, 128:256]
        dkr = dkr_sum * csv + _rope_swap(dkr_sum * snv)
        dkrr_ref[...] = _norm_part_bwd(dkr, gkr, krh, krs, MLA_ROPE).astype(dkrr_ref.dtype)
        pgkr = jnp.sum(dkr * krh, axis=0, keepdims=True)
        pq = jnp.concatenate([pgqn, pgqr], axis=1)
        pk = jnp.concatenate([pgkn, pgkr], axis=1)

        @pl.when(pl.program_id(0) == 0)
        def _():
            dgq_ref[...] = pq
            dgk_ref[...] = pk

        @pl.when(pl.program_id(0) > 0)
        def _():
            dgq_ref[...] += pq
            dgk_ref[...] += pk

    row = lambda w: pl.BlockSpec((tr, w), lambda i: (i, 0))
    vec = pl.BlockSpec((1, hp), lambda i: (0, 0))
    hrow = lambda w: pl.BlockSpec((MLA_HEADS, tr, w), lambda i: (0, i, 0))
    return pl.pallas_call(
        body, name=name, grid=(s // tr,),
        in_specs=[row(MLA_HEADS * hp), row(MLA_HEADS * hp), row(128), row(128), row(128), vec, vec,
                  hrow(hp), hrow(hp), hrow(128)],
        out_specs=[row(MLA_HEADS * hp), row(MLA_HEADS * hp), row(128), vec, vec],
        out_shape=[jax.ShapeDtypeStruct((s, MLA_HEADS * hp), BF16), jax.ShapeDtypeStruct((s, MLA_HEADS * hp), BF16),
                   jax.ShapeDtypeStruct((s, 128), BF16), jax.ShapeDtypeStruct((1, hp), F32),
                   jax.ShapeDtypeStruct((1, hp), F32)],
        compiler_params=_cparams(("arbitrary",), 48),
    )(q, kv, krr, cs, sn, gq, gk, dqf, dkf, dv)


ATT_T = 512


def _chunk_mask(t):
    r = lax.shift_right_logical(lax.broadcasted_iota(jnp.int32, (t, t), 0), 6)
    c = lax.shift_right_logical(lax.broadcasted_iota(jnp.int32, (t, t), 1), 6)
    return c <= r


def _mla_attn_fwd(qf, kf, v, *, name):
    nh, s, hp = qf.shape
    t = _pick(s, ATT_T, CHUNK)
    scale = (MLA_NOPE + MLA_ROPE) ** -0.5
    nt = (((1,), (1,)), ((), ()))

    def body(q_ref, k_ref, v_ref, o_ref, lse_ref):
        i = pl.program_id(1)
        q = q_ref[...]

        def block(j, carry, masked):
            m, l, acc = carry
            start = pl.multiple_of(j * t, t)
            k = k_ref[pl.ds(start, t), :]
            sc = lax.dot_general(q, k, nt, preferred_element_type=F32) * scale
            if masked:
                sc = jnp.where(_chunk_mask(t), sc, NEG)
            m_new = jnp.maximum(m, jnp.max(sc, axis=-1, keepdims=True))
            alpha = jnp.exp(m - m_new)
            p = jnp.exp(sc - m_new)
            l = alpha * l + jnp.sum(p, axis=-1, keepdims=True)
            acc = alpha * acc + jnp.dot(p.astype(BF16), v_ref[pl.ds(start, t), :], preferred_element_type=F32)
            return m_new, l, acc

        init = (jnp.full((t, 1), NEG, F32), jnp.zeros((t, 1), F32), jnp.zeros((t, MLA_V), F32))
        carry = lax.fori_loop(0, i, lambda j, c: block(j, c, False), init)
        m, l, acc = block(i, carry, True)
        o_ref[...] = acc / l
        lse_ref[...] = m + jnp.log(l)

    return pl.pallas_call(
        body, name=name, grid=(nh, s // t),
        in_specs=[pl.BlockSpec((None, t, hp), lambda h, i: (h, i, 0)),
                  pl.BlockSpec((None, s, hp), lambda h, i: (h, 0, 0)),
                  pl.BlockSpec((None, s, MLA_V), lambda h, i: (h, 0, 0))],
        out_specs=[pl.BlockSpec((t, MLA_V), lambda h, i: (i, h)),
                   pl.BlockSpec((None, t, 1), lambda h, i: (h, i, 0))],
        out_shape=[jax.ShapeDtypeStruct((s, nh * MLA_V), F32), jax.ShapeDtypeStruct((nh, s, 1), F32)],
        compiler_params=_cparams(("parallel", "arbitrary"), 48),
    )(qf, kf, v)


def _mla_attn_bwd(qf, kf, v, o, lse, do, *, name):
    nh, s, hp = qf.shape
    t = _pick(s, ATT_T, CHUNK)
    nb = s // t
    scale = (MLA_NOPE + MLA_ROPE) ** -0.5
    nt = (((1,), (1,)), ((), ()))
    tn = (((0,), (0,)), ((), ()))

    def body(q_ref, k_ref, v_ref, o_ref, lse_ref, do_ref, dq_ref, dk_ref, dv_ref, delta):
        j = pl.program_id(1)

        @pl.when(j == 0)
        def _():
            dq_ref[...] = jnp.zeros_like(dq_ref)
            delta[...] = jnp.sum(do_ref[...] * o_ref[...], axis=-1, keepdims=True)

        k = k_ref[...]
        vv = v_ref[...]

        def block(i, carry, masked):
            dk, dv = carry
            start = pl.multiple_of(i * t, t)
            q = q_ref[pl.ds(start, t), :]
            dob = do_ref[pl.ds(start, t), :].astype(BF16)
            sc = lax.dot_general(q, k, nt, preferred_element_type=F32) * scale
            if masked:
                sc = jnp.where(_chunk_mask(t), sc, NEG)
            p = jnp.exp(sc - lse_ref[pl.ds(start, t), :])
            dp = lax.dot_general(dob, vv, nt, preferred_element_type=F32)
            ds = (p * (dp - delta[pl.ds(start, t), :]) * scale).astype(BF16)
            dv = dv + lax.dot_general(p.astype(BF16), dob, tn, preferred_element_type=F32)
            dk = dk + lax.dot_general(ds, q, tn, preferred_element_type=F32)
            dq_ref[pl.ds(start, t), :] += jnp.dot(ds, k, preferred_element_type=F32)
            return dk, dv

        init = (jnp.zeros((t, hp), F32), jnp.zeros((t, MLA_V), F32))
        carry = block(j, init, True)
        dk, dv = lax.fori_loop(j + 1, nb, lambda i, c: block(i, c, False), carry)
        dk_ref[...] = dk
        dv_ref[...] = dv

    whole = lambda w: pl.BlockSpec((None, s, w), lambda h, j: (h, 0, 0))
    blk = lambda w: pl.BlockSpec((None, t, w), lambda h, j: (h, j, 0))
    colh = pl.BlockSpec((s, MLA_V), lambda h, j: (0, h))
    return pl.pallas_call(
        body, name=name, grid=(nh, nb),
        in_specs=[whole(hp), blk(hp), blk(MLA_V), colh, whole(1), colh],
        out_specs=[whole(hp), blk(hp), blk(MLA_V)],
        out_shape=[jax.ShapeDtypeStruct((nh, s, hp), F32), jax.ShapeDtypeStruct((nh, s, hp), F32),
                   jax.ShapeDtypeStruct((nh, s, MLA_V), F32)],
        scratch_shapes=[pltpu.VMEM((s, 1), F32)],
        compiler_params=_cparams(("parallel", "arbitrary"), 56),
    )(qf, kf, v, o, lse, do)


def _merge_fwd(gl, gb, ys, yc, ym, *, name):
    s, d = ys.shape
    tr = _pick(s, 256, 8)

    def body(gl_ref, gb_ref, ys_ref, yc_ref, ym_ref, o_ref):
        acc = jnp.zeros((tr, d), F32)
        for k, y_ref in enumerate((ys_ref, yc_ref, ym_ref)):
            gt = _sigmoid(gl_ref[:, k * d:(k + 1) * d] + gb_ref[:, k * d:(k + 1) * d])
            acc = acc + gt * y_ref[...]
        o_ref[...] = acc.astype(o_ref.dtype)

    row = lambda w: pl.BlockSpec((tr, w), lambda i: (i, 0))
    return pl.pallas_call(
        body, name=name, grid=(s // tr,),
        in_specs=[row(3 * d), pl.BlockSpec((1, 3 * d), lambda i: (0, 0)), row(d), row(d), row(d)],
        out_specs=row(d), out_shape=jax.ShapeDtypeStruct((s, d), BF16),
        compiler_params=_cparams(("parallel",)),
    )(gl, gb, ys, yc, ym)


def _merge_bwd(gl, gb, ys, yc, ym, dm, *, name):
    s, d = ys.shape
    tr = _pick(s, 256, 8)

    def body(gl_ref, gb_ref, ys_ref, yc_ref, ym_ref, dm_ref, dgl_ref, dgb_ref, dys_ref, dyc_ref, dym_ref):
        dmv = dm_ref[...]
        parts = []
        for k, (y_ref, dy_ref) in enumerate(((ys_ref, dys_ref), (yc_ref, dyc_ref), (ym_ref, dym_ref))):
            gt = _sigmoid(gl_ref[:, k * d:(k + 1) * d] + gb_ref[:, k * d:(k + 1) * d])
            dy_ref[...] = (gt * dmv).astype(dy_ref.dtype)
            dl = dmv * y_ref[...] * gt * (1.0 - gt)
            dgl_ref[:, k * d:(k + 1) * d] = dl.astype(dgl_ref.dtype)
            parts.append(jnp.sum(dl, axis=0, keepdims=True))
        pb = jnp.concatenate(parts, axis=1)

        @pl.when(pl.program_id(0) == 0)
        def _():
            dgb_ref[...] = pb

        @pl.when(pl.program_id(0) > 0)
        def _():
            dgb_ref[...] += pb

    row = lambda w: pl.BlockSpec((tr, w), lambda i: (i, 0))
    vec = pl.BlockSpec((1, 3 * d), lambda i: (0, 0))
    return pl.pallas_call(
        body, name=name, grid=(s // tr,),
        in_specs=[row(3 * d), vec, row(d), row(d), row(d), row(d)],
        out_specs=[row(3 * d), vec, row(d), row(d), row(d)],
        out_shape=[jax.ShapeDtypeStruct((s, 3 * d), BF16), jax.ShapeDtypeStruct((1, 3 * d), F32)]
        + [jax.ShapeDtypeStruct((s, d), BF16)] * 3,
        compiler_params=_cparams(("arbitrary",)),
    )(gl, gb, ys, yc, ym, dm)


def _xattn_fwd(q, k, v, gq, *, name):
    s, d = q.shape
    dh = d // X_HEADS
    tr = _pick(s, 512, 8)
    scale = dh ** -0.5
    nt = (((1,), (1,)), ((), ()))

    def body(q_ref, k_ref, v_ref, gq_ref, o_ref):
        for h in range(X_HEADS):
            cs = slice(h * dh, (h + 1) * dh)
            qn, _, _ = _norm_part(q_ref[:, cs], gq_ref[...], dh)
            sc = lax.dot_general(qn.astype(BF16), k_ref[:, cs], nt, preferred_element_type=F32) * scale
            p = jnp.exp(sc - jnp.max(sc, axis=-1, keepdims=True))
            p = p / jnp.sum(p, axis=-1, keepdims=True)
            o_ref[:, cs] = jnp.dot(p.astype(BF16), v_ref[:, cs], preferred_element_type=F32)

    row = pl.BlockSpec((tr, d), lambda i: (i, 0))
    mem = pl.BlockSpec(k.shape, lambda i: (0, 0))
    return pl.pallas_call(
        body, name=name, grid=(s // tr,),
        in_specs=[row, mem, mem, pl.BlockSpec((1, dh), lambda i: (0, 0))], out_specs=row,
        out_shape=jax.ShapeDtypeStruct((s, d), F32), compiler_params=_cparams(("parallel",)),
    )(q, k, v, gq)


def _xattn_bwd(q, k, v, gq, do, *, name):
    s, d = q.shape
    dh = d // X_HEADS
    tr = _pick(s, 512, 8)
    scale = dh ** -0.5
    nt = (((1,), (1,)), ((), ()))
    tn = (((0,), (0,)), ((), ()))

    def body(q_ref, k_ref, v_ref, gq_ref, do_ref, dq_ref, dk_ref, dv_ref, dgq_ref):
        first = pl.program_id(0) == 0
        pg = jnp.zeros((1, dh), F32)
        for h in range(X_HEADS):
            cs = slice(h * dh, (h + 1) * dh)
            qn, qh, qs = _norm_part(q_ref[:, cs], gq_ref[...], dh)
            qnb = qn.astype(BF16)
            kh = k_ref[:, cs]
            sc = lax.dot_general(qnb, kh, nt, preferred_element_type=F32) * scale
            p = jnp.exp(sc - jnp.max(sc, axis=-1, keepdims=True))
            p = p / jnp.sum(p, axis=-1, keepdims=True)
            dob = do_ref[:, cs].astype(BF16)
            dp = lax.dot_general(dob, v_ref[:, cs], nt, preferred_element_type=F32)
            ds = (p * (dp - jnp.sum(dp * p, axis=-1, keepdims=True)) * scale).astype(BF16)
            dqn = jnp.dot(ds, kh, preferred_element_type=F32)
            dq_ref[:, cs] = _norm_part_bwd(dqn, gq_ref[...], qh, qs, dh).astype(dq_ref.dtype)
            pg = pg + jnp.sum(dqn * qh, axis=0, keepdims=True)
            pv = lax.dot_general(p.astype(BF16), dob, tn, preferred_element_type=F32)
            pk = lax.dot_general(ds, qnb, tn, preferred_element_type=F32)

            @pl.when(first)
            def _():
                dv_ref[:, cs] = pv
                dk_ref[:, cs] = pk

            @pl.when(jnp.logical_not(first))
            def _():
                dv_ref[:, cs] += pv
                dk_ref[:, cs] += pk

        @pl.when(first)
        def _():
            dgq_ref[...] = pg

        @pl.when(jnp.logical_not(first))
        def _():
            dgq_ref[...] += pg

    row = pl.BlockSpec((tr, d), lambda i: (i, 0))
    mem = pl.BlockSpec(k.shape, lambda i: (0, 0))
    vec = pl.BlockSpec((1, dh), lambda i: (0, 0))
    return pl.pallas_call(
        body, name=name, grid=(s // tr,),
        in_specs=[row, mem, mem, vec, row], out_specs=[row, mem, mem, vec],
        out_shape=[jax.ShapeDtypeStruct((s, d), BF16), jax.ShapeDtypeStruct(k.shape, F32),
                   jax.ShapeDtypeStruct(k.shape, F32), jax.ShapeDtypeStruct((1, dh), F32)],
        compiler_params=_cparams(("arbitrary",)),
    )(q, k, v, gq, do)


def _swiglu_fwd(h1, *, name):
    s, w2 = h1.shape
    w = w2 // 2
    tr = _pick(s, 256, 8)
    tc = _pick(w, 1408, 128)
    ncb = w // tc

    def body(g_ref, u_ref, o_ref):
        o_ref[...] = (_silu(g_ref[...]) * u_ref[...]).astype(o_ref.dtype)

    return pl.pallas_call(
        body, name=name, grid=(s // tr, ncb),
        in_specs=[pl.BlockSpec((tr, tc), lambda i, j: (i, j)), pl.BlockSpec((tr, tc), lambda i, j: (i, j + ncb))],
        out_specs=pl.BlockSpec((tr, tc), lambda i, j: (i, j)),
        out_shape=jax.ShapeDtypeStruct((s, w), BF16), compiler_params=_cparams(("parallel", "parallel")),
    )(h1, h1)


def _swiglu_bwd(h1, dact, *, name):
    s, w2 = h1.shape
    w = w2 // 2
    tr = _pick(s, 256, 8)
    tc = _pick(w, 1408, 128)
    ncb = w // tc

    def body(g_ref, u_ref, d_ref, dg_ref, du_ref):
        gv = g_ref[...]
        dv = d_ref[...]
        dg_ref[...] = (dv * u_ref[...] * _dsilu(gv)).astype(dg_ref.dtype)
        du_ref[...] = (dv * _silu(gv)).astype(du_ref.dtype)

    blk = pl.BlockSpec((tr, tc), lambda i, j: (i, j))
    dg, du = pl.pallas_call(
        body, name=name, grid=(s // tr, ncb),
        in_specs=[blk, pl.BlockSpec((tr, tc), lambda i, j: (i, j + ncb)), blk],
        out_specs=[blk, blk],
        out_shape=[jax.ShapeDtypeStruct((s, w), BF16)] * 2, compiler_params=_cparams(("parallel", "parallel")),
    )(h1, h1, dact)
    return jnp.concatenate([dg, du], axis=1)


def _add(a, b, *, name):
    r, w = a.shape
    tr = _pick(r, 512, 8)

    def body(a_ref, b_ref, o_ref):
        o_ref[...] = a_ref[...] + b_ref[...].astype(F32)

    row = pl.BlockSpec((tr, w), lambda i: (i, 0))
    return pl.pallas_call(
        body, name=name, grid=(r // tr,), in_specs=[row, row], out_specs=row,
        out_shape=jax.ShapeDtypeStruct((r, w), F32), compiler_params=_cparams(("parallel",)),
    )(a, b)


def _loss(y, target, *, name):
    r, w = y.shape
    tr = _pick(r, 512, 8)

    def body(y_ref, t_ref, dy_ref, l_ref):
        err = y_ref[...] - t_ref[...]
        dy_ref[...] = err * (1.0 / w)
        part = jnp.zeros((8, LANE), F32) + 0.5 * jnp.sum(jnp.mean(err * err, axis=-1, keepdims=True))

        @pl.when(pl.program_id(0) == 0)
        def _():
            l_ref[...] = part

        @pl.when(pl.program_id(0) > 0)
        def _():
            l_ref[...] += part

    row = pl.BlockSpec((tr, w), lambda i: (i, 0))
    dy, l = pl.pallas_call(
        body, name=name, grid=(r // tr,), in_specs=[row, row],
        out_specs=[row, pl.BlockSpec((8, LANE), lambda i: (0, 0))],
        out_shape=[jax.ShapeDtypeStruct((r, w), F32), jax.ShapeDtypeStruct((8, LANE), F32)],
        compiler_params=_cparams(("arbitrary",)),
    )(y, target)
    return dy, l[0, 0]


def _adamw(w, g, m, v, *, name):
    r, c = w.shape
    tr = _pick(r, 256, 8)
    c1 = 1.0 - ADAM_B1 ** ADAM_STEP
    c2 = 1.0 - ADAM_B2 ** ADAM_STEP

    def body(w_ref, g_ref, m_ref, v_ref, d_ref, nm_ref, nv_ref):
        gv = g_ref[...]
        nm = ADAM_B1 * m_ref[...] + (1.0 - ADAM_B1) * gv
        nv = ADAM_B2 * v_ref[...] + (1.0 - ADAM_B2) * (gv * gv)
        nm_ref[...] = nm
        nv_ref[...] = nv
        d_ref[...] = -ADAM_LR * ((nm / c1) / (jnp.sqrt(nv / c2) + ADAM_EPS) + ADAM_WD * w_ref[...])

    row = pl.BlockSpec((tr, c), lambda i: (i, 0))
    return pl.pallas_call(
        body, name=name, grid=(r // tr,), in_specs=[row] * 4, out_specs=[row] * 3,
        out_shape=[jax.ShapeDtypeStruct((r, c), F32)] * 3, compiler_params=_cparams(("parallel",)),
    )(w, g, m, v)


IN_SPLIT = dict(z=(0, 1024), xbc=(1024, 3072), dt=(3072, 3200), glu=(3200, 5248), ql=(5248, 5632),
                ckv=(5632, 5888), kr=(5888, 6016), gate=(6016, 9088))


IN_WIDTH_PAD = 9216


def _w_in_pad(w):
    zeros = lambda n: jnp.zeros(w.shape[:-1] + (n,), w.dtype)
    return jnp.concatenate([w[..., :3088], zeros(112), w[..., 3088:5840], zeros(64), w[..., 5840:],
                            zeros(IN_WIDTH_PAD - 9088)], axis=-1)


def _w_in_unpad(g):
    return jnp.concatenate([g[..., :3088], g[..., 3200:5952], g[..., 6016:9088]], axis=-1)


def _qb_pad(w):
    lead = w.shape[:-1]
    w = w.reshape(lead + (MLA_HEADS, MLA_NOPE + MLA_ROPE))
    w = jnp.concatenate([w, jnp.zeros(lead + (MLA_HEADS, MLA_HP - MLA_NOPE - MLA_ROPE), w.dtype)], axis=-1)
    return w.reshape(lead + (MLA_HEADS * MLA_HP,))


def _qb_unpad(g):
    lead = g.shape[:-1]
    g = g.reshape(lead + (MLA_HEADS, MLA_HP))[..., :MLA_NOPE + MLA_ROPE]
    return g.reshape(lead + (MLA_HEADS * (MLA_NOPE + MLA_ROPE),))


def _pad_lanes(v, n):
    return jnp.concatenate([v, jnp.zeros((n - v.shape[0],), v.dtype)]).reshape(1, n)


def _layer_params(full, rep, l):
    p = {}
    w_in = full['w_in']
    for k, (a, b) in IN_SPLIT.items():
        p['w_' + k] = w_in[:, a:b]
    p['w_in'] = w_in
    for k in ('mla_w_q_b', 'mla_w_kv_b', 'xattn_w_kv', 'ffn_w_in', 'ssd_w_out', 'conv_w_out', 'mla_w_o', 'w_out',
              'xattn_w_q', 'xattn_w_o', 'ffn_w_out', 'ssd_conv_w', 'conv_dw_w'):
        p[k] = full[k]
    p['gate_b'] = full['gate_b'].reshape(1, -1)
    row = lambda name: rep[name][l].reshape(1, -1)
    for k in ('mix_norm_g', 'ssd_conv_b', 'ssd_norm_g', 'conv_dw_b', 'conv_ln_g', 'conv_ln_b', 'mla_q_a_g',
              'mla_kv_a_g', 'xattn_norm_g', 'mem_norm_g', 'xattn_q_norm_g', 'xattn_k_norm_g', 'ffn_norm_g'):
        p[k] = row(k)
    for k in ('ssd_dt_bias', 'ssd_a_log', 'ssd_d'):
        p[k] = _pad_lanes(rep[k][l], LANE)
        p[k + '_t'] = p[k].reshape(LANE, 1)
    p['gq'] = _pad_lanes(rep['mla_q_norm_g'][l], MLA_HP)
    p['gk'] = _pad_lanes(rep['mla_k_norm_g'][l], MLA_HP)
    return p


def _layer_fwd(x, mem, cs, sn, p, l):
    n = lambda s: f"l{l}_{s}"
    s_len, d = x.shape
    nc = s_len // CHUNK
    sv = {'x': x}
    u = _rms_fwd(x, p['mix_norm_g'], out_dtype=BF16, name=n("mix_norm"))
    z = _matmul(u, p['w_z'], name=n("in_z"))
    xbc = _matmul(u, p['w_xbc'], name=n("in_xbc"))
    dtr = _matmul(u, p['w_dt'], name=n("in_dt"))
    glu = _matmul(u, p['w_glu'], name=n("in_glu"))
    ql = _matmul(u, p['w_ql'], name=n("in_ql"))
    ckv = _matmul(u, p['w_ckv'], name=n("in_ckv"))
    krr = _matmul(u, p['w_kr'], name=n("in_kr"))
    gl = _matmul(u, p['w_gate'], name=n("in_gate"))
    pre_s, act_s = _conv_fwd(xbc, p['ssd_conv_w'], p['ssd_conv_b'], glu=False, act=True, name=n("ssd_conv"))
    dtrt = dtr.reshape(nc, CHUNK, LANE).transpose(0, 2, 1)
    y_scan, prev = _ssd_scan_fwd(act_s, dtr, dtrt, p['ssd_dt_bias'], p['ssd_dt_bias_t'], p['ssd_a_log'],
                                 p['ssd_a_log_t'], p['ssd_d'], name=n("ssd_scan"))
    yn = _ssd_gate_fwd(y_scan, z, p['ssd_norm_g'], name=n("ssd_gate"))
    y_ssd = _matmul(yn, p['ssd_w_out'], name=n("ssd_out"))
    pre_c, = _conv_fwd(glu, p['conv_dw_w'], p['conv_dw_b'], glu=True, act=False, name=n("dw_conv"))
    vc = _ln_silu_fwd(pre_c, p['conv_ln_g'], p['conv_ln_b'], name=n("conv_ln"))
    y_conv = _matmul(vc, p['conv_w_out'], name=n("conv_out"))
    qln = _rms_fwd(ql, p['mla_q_a_g'], out_dtype=BF16, name=n("q_a_norm"))
    q = _matmul(qln, p['mla_w_q_b'], name=n("q_b"))
    ckvn = _rms_fwd(ckv, p['mla_kv_a_g'], out_dtype=BF16, name=n("kv_a_norm"))
    kv = _matmul(ckvn, p['mla_w_kv_b'], name=n("kv_b"))
    qf, kf, v = _mla_prep_fwd(q, kv, krr, cs, sn, p['gq'], p['gk'], name=n("mla_prep"))
    o, lse = _mla_attn_fwd(qf, kf, v, name=n("mla_attn"))
    y_mla = _matmul(o, p['mla_w_o'], name=n("mla_out"))
    merged = _merge_fwd(gl, p['gate_b'], y_ssd, y_conv, y_mla, name=n("merge"))
    x1 = _matmul(merged, p['w_out'], add=x, name=n("mix_out"))
    hx = _rms_fwd(x1, p['xattn_norm_g'], out_dtype=BF16, name=n("xattn_norm"))
    qx = _matmul(hx, p['xattn_w_q'], name=n("xattn_q"))
    memn = _rms_fwd(mem, p['mem_norm_g'], out_dtype=BF16, name=n("mem_norm"))
    kvx = _matmul(memn, p['xattn_w_kv'], name=n("xattn_kv"))
    m_len = mem.shape[0]
    dh = d // X_HEADS
    kraw = kvx[:, :d].reshape(m_len * X_HEADS, dh)
    kx = _rms_fwd(kraw, p['xattn_k_norm_g'], out_dtype=BF16, name=n("xattn_k_norm")).reshape(m_len, d)
    vx = kvx[:, d:].astype(BF16)
    ox = _xattn_fwd(qx, kx, vx, p['xattn_q_norm_g'], name=n("xattn_core"))
    x2 = _matmul(ox, p['xattn_w_o'], add=x1, name=n("xattn_out"))
    hf = _rms_fwd(x2, p['ffn_norm_g'], out_dtype=BF16, name=n("ffn_norm"))
    h1 = _matmul(hf, p['ffn_w_in'], name=n("ffn_in"))
    act = _swiglu_fwd(h1, name=n("swiglu"))
    x3 = _matmul(act, p['ffn_w_out'], add=x2, name=n("ffn_out"))
    sv.update(u=u, z=z, xbc=xbc, dtr=dtr, dtrt=dtrt, glu=glu, ql=ql, ckv=ckv, krr=krr, gl=gl, pre_s=pre_s,
              act_s=act_s, y_scan=y_scan, prev=prev, yn=yn, y_ssd=y_ssd, pre_c=pre_c, vc=vc, y_conv=y_conv,
              qln=qln, q=q, ckvn=ckvn, kv=kv, qf=qf, kf=kf, v=v, o=o, lse=lse, y_mla=y_mla, merged=merged,
              x1=x1, hx=hx, qx=qx, memn=memn, kraw=kraw, kx=kx, vx=vx, ox=ox, x2=x2, hf=hf, h1=h1, act=act)
    return x3, sv


DW_KEY = dict(ffn_out_dw='ffn_w_out', ffn_in_dw='ffn_w_in', xattn_out_dw='xattn_w_o', xattn_q_dw='xattn_w_q',
              xattn_kv_dw='xattn_w_kv', mix_out_dw='w_out', mla_out_dw='mla_w_o', q_b_dw='mla_w_q_b',
              kv_b_dw='mla_w_kv_b', conv_out_dw='conv_w_out', ssd_out_dw='ssd_w_out', in_dw='w_in')


def _layer_bwd(dx3, mem, cs, sn, p, sv, l, stacks, depth):
    n = lambda s: f"l{l}_b_{s}"
    dw = lambda s: dict(name=n(s), out_dtype=BF16, into=(stacks.get(DW_KEY[s]), l, depth))
    g = {}
    d = dx3.shape[1]
    dact = _matmul(dx3, p['ffn_w_out'], tb=True, name=n("ffn_out_dx"))
    g['ffn_w_out'] = _matmul(sv['act'], dx3, ta=True, **dw("ffn_out_dw"))
    dh1 = _swiglu_bwd(sv['h1'], dact, name=n("swiglu"))
    g['ffn_w_in'] = _matmul(sv['hf'], dh1, ta=True, **dw("ffn_in_dw"))
    dhf = _matmul(dh1, p['ffn_w_in'], tb=True, name=n("ffn_in_dx"))
    dx2, g['ffn_norm_g'] = _rms_bwd(sv['x2'], p['ffn_norm_g'], dhf, dx_dtype=F32, add=dx3, name=n("ffn_norm"))
    dox = _matmul(dx2, p['xattn_w_o'], tb=True, name=n("xattn_out_dx"))
    g['xattn_w_o'] = _matmul(sv['ox'], dx2, ta=True, **dw("xattn_out_dw"))
    dqx, dkx, dvx, g['xattn_q_norm_g'] = _xattn_bwd(sv['qx'], sv['kx'], sv['vx'], p['xattn_q_norm_g'], dox,
                                                    name=n("xattn_core"))
    g['xattn_w_q'] = _matmul(sv['hx'], dqx, ta=True, **dw("xattn_q_dw"))
    dhx = _matmul(dqx, p['xattn_w_q'], tb=True, name=n("xattn_q_dx"))
    dx1, g['xattn_norm_g'] = _rms_bwd(sv['x1'], p['xattn_norm_g'], dhx, dx_dtype=F32, add=dx2, name=n("xattn_norm"))
    m_len = mem.shape[0]
    dh = d // X_HEADS
    dkraw, g['xattn_k_norm_g'] = _rms_bwd(sv['kraw'], p['xattn_k_norm_g'], dkx.reshape(m_len * X_HEADS, dh),
                                          dx_dtype=BF16, name=n("xattn_k_norm"))
    dkvx = jnp.concatenate([dkraw.reshape(m_len, d), dvx.astype(BF16)], axis=1)
    g['xattn_w_kv'] = _matmul(sv['memn'], dkvx, ta=True, **dw("xattn_kv_dw"))
    dmemn = _matmul(dkvx, p['xattn_w_kv'], tb=True, name=n("xattn_kv_dx"))
    _, g['mem_norm_g'] = _rms_bwd(mem, p['mem_norm_g'], dmemn, dx_dtype=BF16, name=n("mem_norm"))
    dmerged = _matmul(dx1, p['w_out'], tb=True, name=n("mix_out_dx"))
    g['w_out'] = _matmul(sv['merged'], dx1, ta=True, **dw("mix_out_dw"))
    dgl, g['gate_b'], dys, dyc, dym = _merge_bwd(sv['gl'], p['gate_b'], sv['y_ssd'], sv['y_conv'], sv['y_mla'],
                                                 dmerged, name=n("merge"))
    do = _matmul(dym, p['mla_w_o'], tb=True, name=n("mla_out_dx"))
    g['mla_w_o'] = _matmul(sv['o'], dym, ta=True, **dw("mla_out_dw"))
    dqf, dkf, dv = _mla_attn_bwd(sv['qf'], sv['kf'], sv['v'], sv['o'], sv['lse'], do, name=n("mla_attn"))
    dq, dkv, dkrr, g['gq'], g['gk'] = _mla_prep_bwd(sv['q'], sv['kv'], sv['krr'], cs, sn, p['gq'], p['gk'],
                                                    dqf, dkf, dv, name=n("mla_prep"))
    g['mla_w_q_b'] = _matmul(sv['qln'], dq, ta=True, **dw("q_b_dw"))
    dqln = _matmul(dq, p['mla_w_q_b'], tb=True, name=n("q_b_dx"))
    dql, g['mla_q_a_g'] = _rms_bwd(sv['ql'], p['mla_q_a_g'], dqln, dx_dtype=BF16, name=n("q_a_norm"))
    g['mla_w_kv_b'] = _matmul(sv['ckvn'], dkv, ta=True, **dw("kv_b_dw"))
    dckvn = _matmul(dkv, p['mla_w_kv_b'], tb=True, name=n("kv_b_dx"))
    dckv, g['mla_kv_a_g'] = _rms_bwd(sv['ckv'], p['mla_kv_a_g'], dckvn, dx_dtype=BF16, name=n("kv_a_norm"))
    dvc = _matmul(dyc, p['conv_w_out'], tb=True, name=n("conv_out_dx"))
    g['conv_w_out'] = _matmul(sv['vc'], dyc, ta=True, **dw("conv_out_dw"))
    dpre_c, g['conv_ln_g'], g['conv_ln_b'] = _ln_silu_bwd(sv['pre_c'], p['conv_ln_g'], p['conv_ln_b'], dvc,
                                                          name=n("conv_ln"))
    da, dg, g['conv_dw_w'], g['conv_dw_b'] = _conv_bwd(sv['glu'], p['conv_dw_w'], p['conv_dw_b'], dpre_c, None,
                                                       glu=True, act=False, name=n("dw_conv"))
    dyn = _matmul(dys, p['ssd_w_out'], tb=True, name=n("ssd_out_dx"))
    g['ssd_w_out'] = _matmul(sv['yn'], dys, ta=True, **dw("ssd_out_dw"))
    dy_scan, dz, g['ssd_norm_g'] = _ssd_gate_bwd(sv['y_scan'], sv['z'], p['ssd_norm_g'], dyn, name=n("ssd_gate"))
    dxs, db, dc, ddtr, g['ssd_a_log'], g['ssd_dt_bias'], g['ssd_d'] = _ssd_scan_bwd(
        sv['act_s'], sv['dtr'], sv['dtrt'], p['ssd_dt_bias'], p['ssd_dt_bias_t'], p['ssd_a_log'], p['ssd_a_log_t'],
        p['ssd_d'], sv['prev'], dy_scan, name=n("ssd_scan"))
    dact_s = jnp.concatenate([dxs, db, dc], axis=1)
    dxbc, g['ssd_conv_w'], g['ssd_conv_b'] = _conv_bwd(sv['xbc'], p['ssd_conv_w'], p['ssd_conv_b'], dact_s,
                                                       sv['pre_s'], glu=False, act=True, name=n("ssd_conv"))
    tail = jnp.zeros((dz.shape[0], IN_WIDTH_PAD - IN_SPLIT['gate'][1]), BF16)
    dproj = jnp.concatenate([dz, dxbc, ddtr, da, dg, dql, dckv, dkrr, dgl, tail], axis=1)
    g['w_in'] = _matmul(sv['u'], dproj, ta=True, **dw("in_dw"))
    du = _matmul(dproj, p['w_in'], tb=True, name=n("in_dx"))
    dx, g['mix_norm_g'] = _rms_bwd(sv['x'], p['mix_norm_g'], du, dx_dtype=F32, add=dx1, name=n("mix_norm"))
    return dx, g


REP_NAMES = ('mix_norm_g', 'ssd_conv_b', 'ssd_dt_bias', 'ssd_a_log', 'ssd_d', 'ssd_norm_g', 'conv_dw_b', 'conv_ln_g',
             'conv_ln_b', 'mla_q_a_g', 'mla_kv_a_g', 'mla_q_norm_g', 'mla_k_norm_g', 'xattn_norm_g', 'mem_norm_g',
             'xattn_q_norm_g', 'xattn_k_norm_g', 'ffn_norm_g')
BIG = (('w_in', 1, 1024, 8912), ('mla_w_q_b', 1, 384, 1536), ('mla_w_kv_b', 1, 256, 2048),
       ('xattn_w_kv', 1, 1024, 2048), ('ffn_w_in', 1, 1024, 5632), ('ssd_w_out', 0, 1024, 1024),
       ('conv_w_out', 0, 1024, 1024), ('mla_w_o', 0, 1024, 1024), ('w_out', 0, 1024, 1024),
       ('xattn_w_q', 0, 1024, 1024), ('xattn_w_o', 0, 1024, 1024), ('ffn_w_out', 0, 2816, 1024))
SMALL = (('ssd_conv_w', 1, 4, 2048), ('conv_dw_w', 1, 31, 1024), ('gate_b', 1, 3, 1024))


def _rope_tables(positions):
    half = MLA_ROPE // 2
    inv = ROPE_THETA ** (-jnp.arange(0, MLA_ROPE, 2, dtype=F32) / MLA_ROPE)
    ang = positions.astype(F32)[:, None] * inv
    cos, sin = jnp.cos(ang), jnp.sin(ang)
    z = jnp.zeros((positions.shape[0], LANE - 2 * half), F32)
    return jnp.concatenate([cos, cos, z], axis=1), jnp.concatenate([-sin, sin, z], axis=1)


def _local_step(x, mem, positions, target, layer_weights, rep):
    depth = rep['mix_norm_g'].shape[0]
    cs, sn = _rope_tables(positions)
    params, saved = [], []
    h = x
    for l in range(depth):
        p = _layer_params(layer_weights(l, h), rep, l)
        h, sv = _layer_fwd(h, mem, cs, sn, p, l)
        params.append(p)
        saved.append(sv)
    dh, loss = _loss(h, target, name="loss")
    layer_grads = [None] * depth
    stacks = {}
    for l in reversed(range(depth)):
        dh, layer_grads[l] = _layer_bwd(dh, mem, cs, sn, params[l], saved[l], l, stacks, depth)
        stacks = {k: layer_grads[l][k] for k in DW_KEY.values()}
    stack = lambda k: jnp.stack([layer_grads[l][k] for l in range(depth)])
    gfull = {k: stack(k) for k, _, _, _ in SMALL}
    gfull.update(stacks)
    gfull['w_in'] = _w_in_unpad(gfull['w_in'])
    gfull['mla_w_q_b'] = _qb_unpad(gfull['mla_w_q_b'])
    gfull['gate_b'] = gfull['gate_b'].reshape(depth, 3, -1)
    grep = {}
    for k in REP_NAMES:
        if k == 'mla_q_norm_g':
            grep[k] = stack('gq')[:, 0, :MLA_NOPE + MLA_ROPE]
        elif k == 'mla_k_norm_g':
            grep[k] = stack('gk')[:, 0, :MLA_NOPE + MLA_ROPE]
        elif k in ('ssd_dt_bias', 'ssd_a_log', 'ssd_d'):
            grep[k] = stack(k)[:, 0, :SSD_HEADS]
        else:
            grep[k] = stack(k)[:, 0, :]
    return loss, dh, gfull, grep


N_CHIPS = 4
HBM_SPEC = pl.BlockSpec(memory_space=pl.ANY)
ROW, COL, STK, REP = "row", "col", "stk", "rep"


def _kind(axis, cs):
    if axis == 0:
        return ROW
    return COL if cs % LANE == 0 else STK


def _mesh_pos():
    return lax.axis_index("x"), lax.axis_index("y"), lax.axis_index("c")


def _chip_view(ref, kind, j, a, b, layers=None):
    lsel = slice(None) if layers is None else pl.ds(layers[0], layers[1])
    if kind == STK:
        return ref.at[j, lsel]
    if kind == ROW:
        return ref.at[lsel, pl.ds(pl.multiple_of(j * a, 8), a), :]
    if kind == COL:
        return ref.at[lsel, :, pl.ds(pl.multiple_of(j * b, LANE), b)]
    return ref.at[lsel]


def _all_gather(shards, kinds, *, name):
    n = len(shards)
    depth = shards[0].shape[0]
    lh = depth // 2

    def out_shape(w, kind):
        _, a, b = w.shape
        full = {ROW: (depth, N_CHIPS * a, b), COL: (depth, a, N_CHIPS * b), STK: (N_CHIPS, depth, a, b)}[kind]
        return jax.ShapeDtypeStruct(full, w.dtype)

    def body(*refs):
        w_refs, out_refs = refs[:n], refs[n:2 * n]
        send_sems, recv_sems = refs[2 * n:]
        x, y, cc = _mesh_pos()
        me = (x, y, cc)
        sibling = (x, y, 1 - cc)
        chips = [(1 - x, y), (x, 1 - y), (1 - x, 1 - y)]
        slot = lambda chip: 2 * chip[0] + chip[1]

        def part(i, chip, hc):
            _, a, b = w_refs[i].shape
            return _chip_view(out_refs[i], kinds[i], slot(chip), a, b, (hc * lh, lh))

        def own(i):
            _, a, b = w_refs[i].shape
            return _chip_view(out_refs[i], kinds[i], slot((x, y)), a, b)

        def copy(i, k, src, dst, to):
            return pltpu.make_async_remote_copy(src_ref=src, dst_ref=dst, send_sem=send_sems.at[i, k],
                                                recv_sem=recv_sems.at[i, k], device_id=to, device_id_type=MESH)

        sends = []
        for i in range(n):
            my_half = w_refs[i].at[pl.ds(cc * lh, lh)]
            for k, chip in enumerate(chips):
                sends.append(copy(i, k, my_half, part(i, (x, y), cc), (*chip, cc)))
            sends.append(copy(i, 6, w_refs[i], own(i), sibling))
        for cp in sends:
            cp.start()
        for k, chip in enumerate(chips):
            for i in range(n):
                copy(i, k, part(i, chip, cc), part(i, chip, cc), me).wait_recv()
                cp = copy(i, 3 + k, part(i, chip, cc), part(i, chip, cc), sibling)
                cp.start()
                sends.append(cp)
        for i in range(n):
            for k, chip in enumerate(chips):
                copy(i, 3 + k, part(i, chip, 1 - cc), part(i, chip, 1 - cc), me).wait_recv()
            copy(i, 6, own(i), own(i), me).wait_recv()
        for cp in sends:
            cp.wait_send()

    return pl.pallas_call(
        body, name=name, in_specs=[HBM_SPEC] * n, out_specs=[HBM_SPEC] * n,
        out_shape=[out_shape(w, kd) for w, kd in zip(shards, kinds)],
        scratch_shapes=[pltpu.SemaphoreType.DMA((n, 7)), pltpu.SemaphoreType.DMA((n, 7))],
    )(*shards)


SEM_SPEC = pl.BlockSpec(memory_space=pltpu.SEMAPHORE)
SPLIT_COPY = pltpu.CompilerParams(has_side_effects=pltpu.SideEffectType.DATAFLOW_SIDE_EFFECTING)


def _land_view(ref, kind, j, a, b, rows=None):
    r0, nr = (0, a) if rows is None else rows
    if kind == STK:
        return ref.at[j, pl.ds(r0, nr), :]
    if kind == ROW:
        return ref.at[pl.ds(pl.multiple_of(j * a + r0, 8), nr), :]
    return ref.at[pl.ds(r0, nr), pl.ds(pl.multiple_of(j * b, LANE), b)]


def _layer_copies(w_refs, land_refs, kinds, layer, send_sems, recv_sems):
    x, y, cc = _mesh_pos()
    place = 2 * x + y
    chips = [(1 - x, y), (x, 1 - y), (1 - x, 1 - y)]
    out, inc = [], []
    for i, (w_ref, land, kind) in enumerate(zip(w_refs, land_refs, kinds)):
        _, a, b = w_ref.shape
        half = (cc * (a // 2), a // 2)
        desc = lambda k, src, dst, to: pltpu.make_async_remote_copy(
            src_ref=src, dst_ref=dst, send_sem=send_sems.at[4 * i + k], recv_sem=recv_sems.at[4 * i + k], device_id=to,
            device_id_type=MESH)
        for k, chip in enumerate(chips):
            out.append(desc(k, w_ref.at[layer, pl.ds(half[0], half[1]), :], _land_view(land, kind, place, a, b, half),
                            (*chip, cc)))
            theirs = _land_view(land, kind, 2 * chip[0] + chip[1], a, b, half)
            inc.append(desc(k, theirs, theirs, (*chip, cc)))
        out.append(desc(3, w_ref.at[layer], _land_view(land, kind, place, a, b), (x, y, 1 - cc)))
        own = _land_view(land, kind, place, a, b)
        inc.append(desc(3, own, own, (x, y, 1 - cc)))
    return out, inc


def _land_shape(w, kind):
    _, a, b = w.shape
    return {ROW: (N_CHIPS * a, b), COL: (a, N_CHIPS * b), STK: (N_CHIPS, a, b)}[kind]


def _ag_start(shards, kinds, layer, after, *, name):
    n = len(shards)
    lands = [pltpu.with_memory_space_constraint(lax.empty(_land_shape(w, kd), w.dtype), pltpu.HBM)
             for w, kd in zip(shards, kinds)]

    def body(*refs):
        w_refs = refs[:n]
        send_sems, recv_sems = refs[2 * n + 1], refs[2 * n + 2]
        land_refs = refs[2 * n + 3:3 * n + 3]
        token = refs[3 * n + 3]
        out, _ = _layer_copies(w_refs, land_refs, kinds, layer, send_sems, recv_sems)
        for cp in out:
            cp.start()
        token[...] = jnp.zeros_like(token)

    after = jnp.zeros((8, LANE), F32) if after is None else after
    res = pl.pallas_call(
        body, name=name,
        in_specs=[HBM_SPEC] * (2 * n + 1),
        out_specs=[SEM_SPEC, SEM_SPEC] + [HBM_SPEC] * n + [pl.BlockSpec(memory_space=pltpu.VMEM)],
        out_shape=[pltpu.SemaphoreType.DMA((4 * n,)), pltpu.SemaphoreType.DMA((4 * n,))]
        + [jax.ShapeDtypeStruct(ld.shape, ld.dtype) for ld in lands] + [jax.ShapeDtypeStruct((8, LANE), F32)],
        input_output_aliases={n + i: 2 + i for i in range(n)},
        compiler_params=SPLIT_COPY,
    )(*shards, *lands, after)
    return res[0], res[1], list(res[2:2 + n]), res[2 + n]


def _ag_wait(send_sems, recv_sems, shards, lands, kinds, layer, after, *, name):
    n = len(shards)

    def body(*refs):
        w_refs, land_in = refs[:n], refs[n:2 * n]
        send_sems, recv_sems = refs[2 * n], refs[2 * n + 1]
        out, inc = _layer_copies(w_refs, land_in, kinds, layer, send_sems, recv_sems)
        for cp in out:
            cp.wait_send()
        for cp in inc:
            cp.wait_recv()

    return list(pl.pallas_call(
        body, name=name,
        in_specs=[HBM_SPEC] * (2 * n) + [SEM_SPEC, SEM_SPEC, HBM_SPEC],
        out_specs=[HBM_SPEC] * n,
        out_shape=[jax.ShapeDtypeStruct(ld.shape, ld.dtype) for ld in lands],
        input_output_aliases={n + i: i for i in range(n)},
        compiler_params=SPLIT_COPY,
    )(*shards, *lands, send_sems, recv_sems, after))


def _ag_forward(lands, kinds, shard_shapes, *, name):
    n = len(lands)

    def body(*refs):
        land_refs = refs[n:2 * n]
        send_sems, recv_sems = refs[2 * n:]
        x, y, cc = _mesh_pos()
        chips = [(1 - x, y), (x, 1 - y), (1 - x, 1 - y)]
        sends, waits = [], []
        for i, (land, kind) in enumerate(zip(land_refs, kinds)):
            a, b = shard_shapes[i]
            for k, chip in enumerate(chips):
                j = 2 * chip[0] + chip[1]
                desc = lambda view: pltpu.make_async_remote_copy(
                    src_ref=view, dst_ref=view, send_sem=send_sems.at[i, k], recv_sem=recv_sems.at[i, k],
                    device_id=(x, y, 1 - cc), device_id_type=MESH)
                sends.append(desc(_land_view(land, kind, j, a, b, (cc * (a // 2), a // 2))))
                waits.append(desc(_land_view(land, kind, j, a, b, ((1 - cc) * (a // 2), a // 2))))
        for cp in sends:
            cp.start()
        for cp in waits:
            cp.wait_recv()
        for cp in sends:
            cp.wait_send()

    return list(pl.pallas_call(
        body, name=name, in_specs=[HBM_SPEC] * n, out_specs=[HBM_SPEC] * n,
        out_shape=[jax.ShapeDtypeStruct(ld.shape, ld.dtype) for ld in lands],
        input_output_aliases={i: i for i in range(n)},
        scratch_shapes=[pltpu.SemaphoreType.DMA((n, 3)), pltpu.SemaphoreType.DMA((n, 3))],
    )(*lands))


def _layers_half(ref, kind, start, lh):
    return ref.at[:, pl.ds(start, lh)] if kind == STK else ref.at[pl.ds(start, lh)]


def _rs_pair(gs, kinds, *, name):
    n = len(gs)

    def half_shape(g, kind):
        s = list(g.shape)
        s[1 if kind == STK else 0] //= 2
        return jax.ShapeDtypeStruct(tuple(s), g.dtype)

    def body(*refs):
        g_refs, buf_refs = refs[:n], refs[n:2 * n]
        send_sems, recv_sems = refs[2 * n:]
        x, y, cc = _mesh_pos()
        cps = []
        for i in range(n):
            lh = buf_refs[i].shape[1 if kinds[i] == STK else 0]
            cp = pltpu.make_async_remote_copy(src_ref=_layers_half(g_refs[i], kinds[i], (1 - cc) * lh, lh),
                                              dst_ref=buf_refs[i], send_sem=send_sems.at[i], recv_sem=recv_sems.at[i],
                                              device_id=(x, y, 1 - cc), device_id_type=MESH)
            cp.start()
            cps.append(cp)
        for cp in cps:
            cp.wait()

    return pl.pallas_call(
        body, name=name, in_specs=[HBM_SPEC] * n, out_specs=[HBM_SPEC] * n,
        out_shape=[half_shape(g, kd) for g, kd in zip(gs, kinds)],
        scratch_shapes=[pltpu.SemaphoreType.DMA((n,)), pltpu.SemaphoreType.DMA((n,))],
    )(*gs)


def _row_tile(rows, cols):
    return _pick(rows, max(8, (512 * 1024 // cols) // 8 * 8), 8)


def _rs_pair_add(g, buf, kind, cc, out_dtype, *, name):
    cols = g.shape[-1]
    pre = g.shape[0] if kind == STK else 1
    rows = buf.size // (pre * cols)
    tr = _row_tile(rows, cols)

    def body(cc_ref, g_ref, b_ref, o_ref):
        o_ref[...] = (g_ref[...].astype(F32) + b_ref[...].astype(F32)).astype(o_ref.dtype)

    out = pl.pallas_call(
        body, name=name,
        grid_spec=pltpu.PrefetchScalarGridSpec(
            num_scalar_prefetch=1, grid=(pre, rows // tr),
            in_specs=[pl.BlockSpec((None, None, tr, cols), lambda s, i, cc_ref: (s, cc_ref[0], i, 0)),
                      pl.BlockSpec((None, tr, cols), lambda s, i, cc_ref: (s, i, 0))],
            out_specs=pl.BlockSpec((None, tr, cols), lambda s, i, cc_ref: (s, i, 0))),
        out_shape=jax.ShapeDtypeStruct((pre, rows, cols), out_dtype),
        compiler_params=_cparams(("parallel", "parallel")),
    )(cc.reshape(1).astype(jnp.int32), g.reshape(pre, 2, rows, cols), buf.reshape(pre, rows, cols))
    return out.reshape(buf.shape)


def _rs_cross(ps, kinds, shard_shapes, *, name):
    n = len(ps)

    def body(*refs):
        p_refs, out_refs = refs[:n], refs[n:2 * n]
        send_sems, recv_sems, local_sems = refs[2 * n:]
        x, y, cc = _mesh_pos()
        chips = [(1 - x, y), (x, 1 - y), (1 - x, 1 - y)]
        slot = lambda chip: 2 * chip[0] + chip[1]
        local, sends = [], []
        for i in range(n):
            a, b = shard_shapes[i]
            if kinds[i] == REP:
                cp = pltpu.make_async_copy(p_refs[i], out_refs[i].at[slot((x, y))], local_sems.at[i])
                cp.start()
                local.append(cp)
            for k, chip in enumerate(chips):
                cp = pltpu.make_async_remote_copy(src_ref=_chip_view(p_refs[i], kinds[i], slot(chip), a, b),
                                                  dst_ref=out_refs[i].at[slot((x, y))], send_sem=send_sems.at[i, k],
                                                  recv_sem=recv_sems.at[i, k], device_id=(*chip, cc),
                                                  device_id_type=MESH)
                cp.start()
                sends.append(cp)
        for i in range(n):
            for k, chip in enumerate(chips):
                landed = out_refs[i].at[slot(chip)]
                pltpu.make_async_remote_copy(src_ref=landed, dst_ref=landed, send_sem=send_sems.at[i, k],
                                             recv_sem=recv_sems.at[i, k], device_id=(*chip, cc),
                                             device_id_type=MESH).wait_recv()
        for cp in sends:
            cp.wait_send()
        for cp in local:
            cp.wait()

    def out_shape(p, kind, ab):
        lh = p.shape[1 if kind == STK else 0]
        return jax.ShapeDtypeStruct((N_CHIPS, lh) + tuple(ab), p.dtype)

    return pl.pallas_call(
        body, name=name, in_specs=[HBM_SPEC] * n, out_specs=[HBM_SPEC] * n,
        out_shape=[out_shape(p, kd, ab) for p, kd, ab in zip(ps, kinds, shard_shapes)],
        scratch_shapes=[pltpu.SemaphoreType.DMA((n, 3)), pltpu.SemaphoreType.DMA((n, 3)),
                        pltpu.SemaphoreType.DMA((n,))],
    )(*ps)


def _rs_sum(p, landed, kind, ab, place, cc, *, name):
    a, b = ab
    lh = landed.shape[1]
    tr = _row_tile(a, b)
    blk = lambda f: pl.BlockSpec((None, None, tr, b), f)
    if kind == ROW:
        p_spec = pl.BlockSpec((None, tr, b), lambda l, i, w, c: (l, w[0] * (a // tr) + i, 0))
    elif kind == COL:
        p_spec = pl.BlockSpec((None, tr, b), lambda l, i, w, c: (l, i, w[0]))
    elif kind == STK:
        p_spec = blk(lambda l, i, w, c: (w[0], l, i, 0))
    else:
        p_spec = blk(lambda l, i, w, c: (0, l, i, 0))
        p = landed
    if kind == REP:
        others = [blk(lambda l, i, w, c, k=k: (k, l, i, 0)) for k in (1, 2, 3)]
    else:
        others = [blk(lambda l, i, w, c: (lax.rem(w[0] + 2, 4), l, i, 0)),
                  blk(lambda l, i, w, c: (w[0] + 1 - 2 * lax.rem(w[0], 2), l, i, 0)),
                  blk(lambda l, i, w, c: (3 - w[0], l, i, 0))]

    def body(w_ref, c_ref, p_ref, b1_ref, b2_ref, b3_ref, o_ref):
        f = lambda r: r[...].astype(F32)
        o_ref[...] = ((f(p_ref) + f(b1_ref)) + f(b2_ref)) + f(b3_ref)

    return pl.pallas_call(
        body, name=name,
        grid_spec=pltpu.PrefetchScalarGridSpec(
            num_scalar_prefetch=2, grid=(lh, a // tr), in_specs=[p_spec] + others,
            out_specs=pl.BlockSpec((None, tr, b), lambda l, i, w, c: (c[0] * lh + l, i, 0))),
        out_shape=jax.ShapeDtypeStruct((2 * lh, a, b), F32),
        compiler_params=_cparams(("parallel", "parallel")),
    )(place, cc, p, landed, landed, landed)


def _rs_share(fs, *, name):
    n = len(fs)

    def body(*refs):
        out_refs = refs[n:2 * n]
        send_sems, recv_sems = refs[2 * n:]
        x, y, cc = _mesh_pos()
        cps = []
        for i in range(n):
            lh = out_refs[i].shape[0] // 2
            mine = out_refs[i].at[pl.ds(cc * lh, lh)]
            cp = pltpu.make_async_remote_copy(src_ref=mine, dst_ref=mine, send_sem=send_sems.at[i],
                                              recv_sem=recv_sems.at[i], device_id=(x, y, 1 - cc),
                                              device_id_type=MESH)
            cp.start()
            cps.append(cp)
        for i in range(n):
            lh = out_refs[i].shape[0] // 2
            theirs = out_refs[i].at[pl.ds((1 - cc) * lh, lh)]
            pltpu.make_async_remote_copy(src_ref=theirs, dst_ref=theirs, send_sem=send_sems.at[i],
                                         recv_sem=recv_sems.at[i], device_id=(x, y, 1 - cc),
                                         device_id_type=MESH).wait_recv()
        for cp in cps:
            cp.wait_send()

    return pl.pallas_call(
        body, name=name, in_specs=[HBM_SPEC] * n, out_specs=[HBM_SPEC] * n,
        out_shape=[jax.ShapeDtypeStruct(f.shape, f.dtype) for f in fs],
        input_output_aliases={i: i for i in range(n)},
        scratch_shapes=[pltpu.SemaphoreType.DMA((n,)), pltpu.SemaphoreType.DMA((n,))],
    )(*fs)


def _reduce_scatter(gs, kinds, shard_shapes, wire_dtypes):
    x, y, cc = _mesh_pos()
    place = (2 * x + y).reshape(1).astype(jnp.int32)
    cc1 = cc.reshape(1).astype(jnp.int32)
    bufs = _rs_pair(gs, kinds, name="rs_pair")
    ps = [_rs_pair_add(g, buf, kd, cc, wd, name=f"rs_pair_add_{i}")
          for i, (g, buf, kd, wd) in enumerate(zip(gs, bufs, kinds, wire_dtypes))]
    landed = _rs_cross(ps, kinds, shard_shapes, name="rs_cross")
    fs = [_rs_sum(p, b, kd, ab, place, cc1, name=f"rs_sum_{i}")
          for i, (p, b, kd, ab) in enumerate(zip(ps, landed, kinds, shard_shapes))]
    return _rs_share(fs, name="rs_share")


def _shard_shape(axis, r, c):
    return (r // N_CHIPS, c) if axis == 0 else (r, c // N_CHIPS)


def _unstack(stacked):
    ns, depth, r, cs = stacked.shape
    return stacked.transpose(1, 2, 0, 3).reshape(depth, r, ns * cs)


def _stack(fullw):
    depth, r, c = fullw.shape
    return fullw.reshape(depth, r, N_CHIPS, c // N_CHIPS).transpose(2, 0, 1, 3)


REP_SIZES = dict(mix_norm_g=1024, ssd_conv_b=2048, ssd_dt_bias=16, ssd_a_log=16, ssd_d=16, ssd_norm_g=1024,
                 conv_dw_b=1024, conv_ln_g=1024, conv_ln_b=1024, mla_q_a_g=384, mla_kv_a_g=256, mla_q_norm_g=192,
                 mla_k_norm_g=192, xattn_norm_g=1024, mem_norm_g=1024, xattn_q_norm_g=256, xattn_k_norm_g=256,
                 ffn_norm_g=1024)
REP_ROWS = 8
REP_COLS = -(-sum(REP_SIZES.values()) // (REP_ROWS * LANE)) * LANE
SMALL_COLS = 256
SMALL_ROWS = 48


def _pack_rep(d):
    flat = jnp.concatenate([d[k] for k in REP_NAMES], axis=1)
    flat = jnp.pad(flat, ((0, 0), (0, REP_ROWS * REP_COLS - flat.shape[1])))
    return flat.reshape(flat.shape[0], REP_ROWS, REP_COLS)


def _unpack_rep(packed):
    out, off = {}, 0
    flat = packed.reshape(packed.shape[0], -1)
    for k in REP_NAMES:
        out[k] = flat[:, off:off + REP_SIZES[k]]
        off += REP_SIZES[k]
    return out


def _pack_small(d):
    depth = d['gate_b'].shape[0]
    rows = jnp.concatenate([d[k].reshape(depth, -1, SMALL_COLS) for k, _, _, _ in SMALL], axis=1)
    return jnp.pad(rows, ((0, 0), (0, SMALL_ROWS - rows.shape[1]), (0, 0)))


def _unpack_small(packed):
    out, off = {}, 0
    for k, axis, r, c in SMALL:
        rs, cs = _shard_shape(axis, r, c)
        n = rs * cs // SMALL_COLS
        out[k] = packed[:, off:off + n].reshape(packed.shape[0], rs, cs)
        off += n
    return out


WEIGHT_NAMES = ('mix_norm_g', 'w_in', 'ssd_conv_w', 'ssd_conv_b', 'ssd_dt_bias', 'ssd_a_log', 'ssd_d', 'ssd_norm_g',
                'ssd_w_out', 'conv_dw_w', 'conv_dw_b', 'conv_ln_g', 'conv_ln_b', 'conv_w_out', 'mla_q_a_g',
                'mla_w_q_b', 'mla_kv_a_g', 'mla_w_kv_b', 'mla_q_norm_g', 'mla_k_norm_g', 'mla_w_o', 'gate_b', 'w_out',
                'xattn_norm_g', 'mem_norm_g', 'xattn_w_q', 'xattn_w_kv', 'xattn_q_norm_g', 'xattn_k_norm_g',
                'xattn_w_o', 'ffn_norm_g', 'ffn_w_in', 'ffn_w_out')


def kernel(x, mem, positions, *rest):
    nw = len(WEIGHT_NAMES)
    weights = dict(zip(WEIGHT_NAMES, rest[:nw]))
    target = rest[nw]
    mom_m = dict(zip(WEIGHT_NAMES, rest[nw + 1:2 * nw + 1]))
    mom_v = dict(zip(WEIGHT_NAMES, rest[2 * nw + 1:3 * nw + 1]))
    depth = weights['mix_norm_g'].shape[0]

    kinds = [_kind(axis, _shard_shape(axis, r, c)[1]) for _, axis, r, c in BIG]
    shard_shapes = [_shard_shape(axis, r, c) for _, axis, r, c in BIG]
    nb = len(BIG)
    small_all, = _all_gather([_pack_small(weights)], [STK], name="ag_small")
    small_chips = [_unpack_small(small_all[j]) for j in range(N_CHIPS)]
    small = {k: jnp.concatenate([sc[k] for sc in small_chips], axis=2) for k, _, _, _ in SMALL}
    big_shards = [weights[k].astype(BF16) for k, _, _, _ in BIG]
    send0, recv0, lands0, _ = _ag_start(big_shards, kinds[:nb], 0, None, name="ag_start_0")
    lands0 = _ag_wait(send0, recv0, big_shards, lands0, kinds[:nb], 0, big_shards[0], name="ag_wait_0")
    lands0 = _ag_forward(lands0, kinds[:nb], shard_shapes[:nb], name="ag_forward_0")
    started = {l: _ag_start(big_shards, kinds[:nb], l, lands0[0], name=f"ag_start_{l}") for l in range(1, depth)}
    tokens = sum(st[3][0, 0] for st in started.values())

    def layer_weights(l, h):
        if l == 0:
            lands = lands0
        else:
            send, recv, lands, _ = started[l]
            lands = _ag_wait(send, recv, big_shards, lands, kinds[:nb], l, h, name=f"ag_wait_{l}")
            lands = _ag_forward(lands, kinds[:nb], shard_shapes[:nb], name=f"ag_forward_{l}")
        full = {k: (ld.transpose(1, 0, 2).reshape(ld.shape[1], -1) if kd == STK else ld)
                for (k, _, _, _), kd, ld in zip(BIG, kinds[:nb], lands)}
        full.update({k: w[l] for k, w in small.items()})
        full['w_in'] = _w_in_pad(full['w_in'])
        full['mla_w_q_b'] = _qb_pad(full['mla_w_q_b'])
        return full

    rep = {k: weights[k] for k in REP_NAMES}
    loss, dx, gfull, grep = _local_step(x[0] + tokens, mem[0], positions[0], target[0], layer_weights, rep)
    loss = lax.psum(loss, ("x", "y", "c"))

    gs = [(_stack(gfull[k]) if kd == STK else gfull[k]) for (k, _, _, _), kd in zip(BIG, kinds)]
    small_split = {k: _stack(gfull[k]) for k, _, _, _ in SMALL}
    gs.append(jnp.stack([_pack_small({k: v[j] for k, v in small_split.items()}) for j in range(N_CHIPS)]))
    gs.append(_pack_rep(grep))
    summed = _reduce_scatter(gs, kinds + [STK, REP], shard_shapes + [(SMALL_ROWS, SMALL_COLS), (REP_ROWS, REP_COLS)],
                             [BF16] * nb + [F32, F32])
    grads = {k: g for (k, _, _, _), g in zip(BIG, summed[:nb])}
    grads.update(_unpack_small(summed[nb]))
    rep_sum = summed[nb + 1]
    grads.update(_unpack_rep(rep_sum))

    delta, new_m, new_v = {}, {}, {}
    for k, _, _, _ in BIG + SMALL:
        w = weights[k]
        two_d = (w.shape[0] * w.shape[1], w.shape[2])
        d_, m_, v_ = _adamw(w.reshape(two_d), grads[k].reshape(two_d), mom_m[k].reshape(two_d),
                            mom_v[k].reshape(two_d), name="adamw_" + k)
        delta[k], new_m[k], new_v[k] = d_.reshape(w.shape), m_.reshape(w.shape), v_.reshape(w.shape)
    pack2 = lambda d: _pack_rep(d).reshape(depth, -1)
    d_, m_, v_ = _adamw(pack2(rep), rep_sum.reshape(depth, -1), pack2({k: mom_m[k] for k in REP_NAMES}),
                        pack2({k: mom_v[k] for k in REP_NAMES}), name="adamw_rep")
    delta.update(_unpack_rep(d_))
    new_m.update(_unpack_rep(m_))
    new_v.update(_unpack_rep(v_))

    return (loss, dx[None], *[grads[k] for k in WEIGHT_NAMES], *[delta[k] for k in WEIGHT_NAMES],
            *[new_m[k] for k in WEIGHT_NAMES], *[new_v[k] for k in WEIGHT_NAMES])
```

```python
import functools
import math

import jax
import jax.numpy as jnp
import numpy as np
from jax import lax
from jax.experimental import pallas as pl
from jax.experimental.pallas import tpu as pltpu

F32 = jnp.float32
BF16 = jnp.bfloat16
MESH = pl.DeviceIdType.MESH

EPS = 1e-6
CHUNK = 64
SSD_HEADS = 16
SSD_GROUPS = 4
SSD_P = 64
SSD_N = 128
MLA_HEADS = 8
MLA_NOPE = 128
MLA_ROPE = 64
MLA_V = 128
MLA_HP = 256
X_HEADS = 4
ROPE_THETA = 10000.0
ADAM_LR, ADAM_B1, ADAM_B2, ADAM_EPS, ADAM_WD, ADAM_STEP = 0.001, 0.9, 0.999, 1e-08, 0.01, 10
LANE = 128
NEG = -1e30
VMEM_MB = 1024 * 1024


def _pick(n, cap, mult=128):
    if n <= cap:
        return n
    d = (cap // mult) * mult
    while d >= mult:
        if n % d == 0:
            return d
        d -= mult
    return n


def _cparams(sem, mb=40):
    return pltpu.CompilerParams(dimension_semantics=sem, vmem_limit_bytes=mb * VMEM_MB)


def _sigmoid(x):
    return 1.0 / (1.0 + jnp.exp(-x))


def _silu(x):
    return x * _sigmoid(x)


def _dsilu(x):
    s = _sigmoid(x)
    return s * (1.0 + x * (1.0 - s))


def _softplus(x):
    return jnp.maximum(x, 0.0) + jnp.log(1.0 + jnp.exp(-jnp.abs(x)))


def _matmul(a, b, *, ta=False, tb=False, out_dtype=F32, add=None, into=None, name):
    if ta:
        kdim, m = a.shape
    else:
        m, kdim = a.shape
    if tb:
        n, kb = b.shape
    else:
        kb, n = b.shape
    assert kb == kdim, (a.shape, b.shape, ta, tb)
    tm = _pick(m, 512, 128) if ta else _pick(m, 1024, 8)
    tn = _pick(n, 1024 if n <= 1024 else 512, 128)
    tk = _pick(kdim, 2048, 128)
    nk = kdim // tk
    dims = (((0 if ta else 1,), (1 if tb else 0,)), ((), ()))

    has_add = add is not None
    has_stack = into is not None and into[0] is not None

    def body(a_ref, b_ref, *rest):
        add_ref = rest[0] if has_add else None
        o_ref = rest[has_add + has_stack]
        acc = rest[has_add + has_stack + 1:]
        part = lax.dot_general(a_ref[...].astype(BF16), b_ref[...].astype(BF16), dims,
                               preferred_element_type=F32)

        def finish(total):
            if has_add:
                total = total + add_ref[...]
            o_ref[...] = total.astype(o_ref.dtype)

        if nk == 1:
            finish(part)
        else:
            acc_ref, = acc
            k = pl.program_id(2)

            @pl.when(k == 0)
            def _():
                acc_ref[...] = part

            @pl.when(k > 0)
            def _():
                acc_ref[...] += part

            @pl.when(k == nk - 1)
            def _():
                finish(acc_ref[...])

    a_spec = pl.BlockSpec((tk, tm), lambda i, j, k: (k, i)) if ta else pl.BlockSpec((tm, tk), lambda i, j, k: (i, k))
    b_spec = pl.BlockSpec((tn, tk), lambda i, j, k: (j, k)) if tb else pl.BlockSpec((tk, tn), lambda i, j, k: (k, j))
    o_spec = pl.BlockSpec((tm, tn), lambda i, j, k: (i, j))
    operands = [a, b] + ([add] if has_add else [])
    in_specs = [a_spec, b_spec] + ([o_spec] if has_add else [])
    if into is None:
        out_spec, out_shape, aliases = o_spec, jax.ShapeDtypeStruct((m, n), out_dtype), {}
    else:
        stack, layer, depth = into
        out_spec = pl.BlockSpec((None, tm, tn), lambda i, j, k: (layer, i, j))
        out_shape = jax.ShapeDtypeStruct((depth, m, n), out_dtype)
        aliases = {}
        if stack is not None:
            aliases = {len(operands): 0}
            operands.append(stack)
            in_specs.append(HBM_SPEC)
    return pl.pallas_call(
        body, name=name, grid=(m // tm, n // tn, nk),
        in_specs=in_specs, out_specs=out_spec, out_shape=out_shape, input_output_aliases=aliases,
        scratch_shapes=[] if nk == 1 else [pltpu.VMEM((tm, tn), F32)],
        compiler_params=_cparams(("parallel", "parallel", "arbitrary"), 48),
    )(*operands)


def _rms_fwd(x, g, *, out_dtype, name):
    r, w = x.shape
    tr = _pick(r, 512, 8)

    def body(x_ref, g_ref, o_ref):
        xv = x_ref[...]
        rstd = lax.rsqrt(jnp.mean(xv * xv, axis=-1, keepdims=True) + EPS)
        o_ref[...] = (xv * rstd * g_ref[...]).astype(o_ref.dtype)

    return pl.pallas_call(
        body, name=name, grid=(r // tr,),
        in_specs=[pl.BlockSpec((tr, w), lambda i: (i, 0)), pl.BlockSpec((1, w), lambda i: (0, 0))],
        out_specs=pl.BlockSpec((tr, w), lambda i: (i, 0)),
        out_shape=jax.ShapeDtypeStruct((r, w), out_dtype),
        compiler_params=_cparams(("parallel",)),
    )(x, g)


def _rms_bwd(x, g, dy, *, dx_dtype, name, add=None):
    r, w = x.shape
    tr = _pick(r, 512, 8)
    has_add = add is not None

    def body(x_ref, g_ref, dy_ref, *rest):
        if has_add:
            add_ref, dx_ref, dg_ref = rest
        else:
            dx_ref, dg_ref = rest
        xv = x_ref[...]
        dyv = dy_ref[...].astype(F32)
        rstd = lax.rsqrt(jnp.mean(xv * xv, axis=-1, keepdims=True) + EPS)
        xh = xv * rstd
        dyg = dyv * g_ref[...]
        dx = rstd * (dyg - xh * jnp.mean(dyg * xh, axis=-1, keepdims=True))
        if has_add:
            dx = dx + add_ref[...]
        dx_ref[...] = dx.astype(dx_ref.dtype)
        part = jnp.sum(dyv * xh, axis=0, keepdims=True)

        @pl.when(pl.program_id(0) == 0)
        def _():
            dg_ref[...] = part

        @pl.when(pl.program_id(0) > 0)
        def _():
            dg_ref[...] += part

    row = pl.BlockSpec((tr, w), lambda i: (i, 0))
    vec = pl.BlockSpec((1, w), lambda i: (0, 0))
    ins = [x, g, dy] + ([add] if has_add else [])
    return pl.pallas_call(
        body, name=name, grid=(r // tr,),
        in_specs=[row, vec, row] + ([row] if has_add else []),
        out_specs=[row, vec],
        out_shape=[jax.ShapeDtypeStruct((r, w), dx_dtype), jax.ShapeDtypeStruct((1, w), F32)],
        compiler_params=_cparams(("arbitrary",)),
    )(*ins)


def _ln_silu_fwd(x, g, b, *, name):
    r, w = x.shape
    tr = _pick(r, 512, 8)

    def body(x_ref, g_ref, b_ref, o_ref):
        xv = x_ref[...]
        mu = jnp.mean(xv, axis=-1, keepdims=True)
        xc = xv - mu
        rstd = lax.rsqrt(jnp.mean(xc * xc, axis=-1, keepdims=True) + EPS)
        o_ref[...] = _silu(xc * rstd * g_ref[...] + b_ref[...]).astype(o_ref.dtype)

    row = pl.BlockSpec((tr, w), lambda i: (i, 0))
    vec = pl.BlockSpec((1, w), lambda i: (0, 0))
    return pl.pallas_call(
        body, name=name, grid=(r // tr,), in_specs=[row, vec, vec], out_specs=row,
        out_shape=jax.ShapeDtypeStruct((r, w), BF16), compiler_params=_cparams(("parallel",)),
    )(x, g, b)


def _ln_silu_bwd(x, g, b, dy, *, name):
    r, w = x.shape
    tr = _pick(r, 512, 8)

    def body(x_ref, g_ref, b_ref, dy_ref, dx_ref, dg_ref, db_ref):
        xv = x_ref[...]
        mu = jnp.mean(xv, axis=-1, keepdims=True)
        xc = xv - mu
        rstd = lax.rsqrt(jnp.mean(xc * xc, axis=-1, keepdims=True) + EPS)
        xh = xc * rstd
        pre = xh * g_ref[...] + b_ref[...]
        dpre = dy_ref[...].astype(F32) * _dsilu(pre)
        dxh = dpre * g_ref[...]
        dx_ref[...] = rstd * (dxh - jnp.mean(dxh, axis=-1, keepdims=True)
                              - xh * jnp.mean(dxh * xh, axis=-1, keepdims=True))
        pg = jnp.sum(dpre * xh, axis=0, keepdims=True)
        pb = jnp.sum(dpre, axis=0, keepdims=True)

        @pl.when(pl.program_id(0) == 0)
        def _():
            dg_ref[...] = pg
            db_ref[...] = pb

        @pl.when(pl.program_id(0) > 0)
        def _():
            dg_ref[...] += pg
            db_ref[...] += pb

    row = pl.BlockSpec((tr, w), lambda i: (i, 0))
    vec = pl.BlockSpec((1, w), lambda i: (0, 0))
    return pl.pallas_call(
        body, name=name, grid=(r // tr,), in_specs=[row, vec, vec, row], out_specs=[row, vec, vec],
        out_shape=[jax.ShapeDtypeStruct((r, w), F32), jax.ShapeDtypeStruct((1, w), F32),
                   jax.ShapeDtypeStruct((1, w), F32)],
        compiler_params=_cparams(("arbitrary",)),
    )(x, g, b, dy)


CONV_PAD = 32
CONV_T = 256


def _conv_fwd(src, w, b, *, glu, act, name):
    s = src.shape[0]
    k, c = w.shape
    tc = LANE
    ncb = c // tc
    tt = _pick(s, CONV_T, 8)
    assert k - 1 <= CONV_PAD

    def body(*refs):
        if glu:
            a_ref, g_ref, w_ref, b_ref = refs[:4]
            outs = refs[4:-1]
        else:
            a_ref, w_ref, b_ref = refs[:3]
            outs = refs[3:-1]
        xp = refs[-1]
        xp[0:CONV_PAD, :] = jnp.zeros((CONV_PAD, tc), F32)
        if glu:
            xp[CONV_PAD:CONV_PAD + s, :] = a_ref[...] * _sigmoid(g_ref[...])
        else:
            xp[CONV_PAD:CONV_PAD + s, :] = a_ref[...]
        wv = w_ref[...]
        bv = b_ref[...]
        for t0 in range(0, s, tt):
            acc = jnp.zeros((tt, tc), F32) + bv
            for kk in range(k):
                off = CONV_PAD + t0 - (k - 1) + kk
                acc = acc + wv[kk:kk + 1, :] * xp[off:off + tt, :]
            outs[0][t0:t0 + tt, :] = acc
            if act:
                outs[1][t0:t0 + tt, :] = _silu(acc)

    col = pl.BlockSpec((s, tc), lambda j: (0, j))
    in_specs = [col, pl.BlockSpec((s, tc), lambda j: (0, j + ncb))] if glu else [col]
    in_specs += [pl.BlockSpec((k, tc), lambda j: (0, j)), pl.BlockSpec((1, tc), lambda j: (0, j))]
    n_out = 2 if act else 1
    res = pl.pallas_call(
        body, name=name, grid=(ncb,), in_specs=in_specs,
        out_specs=[col] * n_out,
        out_shape=[jax.ShapeDtypeStruct((s, c), F32)] * n_out,
        scratch_shapes=[pltpu.VMEM((CONV_PAD + s, tc), F32)],
        compiler_params=_cparams(("parallel",), 48),
    )(*([src, src] if glu else [src]), w, b)
    return res


def _conv_bwd(src, w, b, dy, pre, *, glu, act, name):
    s = src.shape[0]
    k, c = w.shape
    tc = LANE
    ncb = c // tc
    tt = _pick(s, CONV_T, 8)

    def body(*refs):
        i = 0
        a_ref = refs[i]; i += 1
        if glu:
            g_ref = refs[i]; i += 1
        w_ref = refs[i]; i += 1
        dy_ref = refs[i]; i += 1
        if act:
            pre_ref = refs[i]; i += 1
        da_ref = refs[i]; i += 1
        if glu:
            dg_ref = refs[i]; i += 1
        dw_ref = refs[i]; db_ref = refs[i + 1]
        xp, dp = refs[-2], refs[-1]
        xp[0:CONV_PAD, :] = jnp.zeros((CONV_PAD, tc), F32)
        if glu:
            xp[CONV_PAD:CONV_PAD + s, :] = a_ref[...] * _sigmoid(g_ref[...])
        else:
            xp[CONV_PAD:CONV_PAD + s, :] = a_ref[...]
        dp[s:s + CONV_PAD, :] = jnp.zeros((CONV_PAD, tc), F32)
        if act:
            dp[0:s, :] = dy_ref[...].astype(F32) * _dsilu(pre_ref[...])
        else:
            dp[0:s, :] = dy_ref[...].astype(F32)
        wv = w_ref[...]
        dws = [jnp.zeros((1, tc), F32) for _ in range(k)]
        dbs = jnp.zeros((1, tc), F32)
        for t0 in range(0, s, tt):
            acc = jnp.zeros((tt, tc), F32)
            dcur = dp[t0:t0 + tt, :]
            dbs = dbs + jnp.sum(dcur, axis=0, keepdims=True)
            for kk in range(k):
                acc = acc + wv[kk:kk + 1, :] * dp[t0 + (k - 1) - kk:t0 + (k - 1) - kk + tt, :]
                off = CONV_PAD + t0 - (k - 1) + kk
                dws[kk] = dws[kk] + jnp.sum(dcur * xp[off:off + tt, :], axis=0, keepdims=True)
            if glu:
                av = a_ref[t0:t0 + tt, :]
                sg = _sigmoid(g_ref[t0:t0 + tt, :])
                da_ref[t0:t0 + tt, :] = (acc * sg).astype(da_ref.dtype)
                dg_ref[t0:t0 + tt, :] = (acc * av * sg * (1.0 - sg)).astype(dg_ref.dtype)
            else:
                da_ref[t0:t0 + tt, :] = acc.astype(da_ref.dtype)
        for kk in range(k):
            dw_ref[kk:kk + 1, :] = dws[kk]
        db_ref[...] = dbs

    col = pl.BlockSpec((s, tc), lambda j: (0, j))
    in_specs = [col] + ([pl.BlockSpec((s, tc), lambda j: (0, j + ncb))] if glu else [])
    in_specs += [pl.BlockSpec((k, tc), lambda j: (0, j)), col] + ([col] if act else [])
    ins = ([src, src] if glu else [src]) + [w, dy] + ([pre] if act else [])
    out_specs = [col] + ([col] if glu else []) + [pl.BlockSpec((k, tc), lambda j: (0, j)),
                                                  pl.BlockSpec((1, tc), lambda j: (0, j))]
    out_shape = [jax.ShapeDtypeStruct((s, c), BF16)] * (2 if glu else 1) + [
        jax.ShapeDtypeStruct((k, c), F32), jax.ShapeDtypeStruct((1, c), F32)]
    return pl.pallas_call(
        body, name=name, grid=(ncb,), in_specs=in_specs, out_specs=out_specs, out_shape=out_shape,
        scratch_shapes=[pltpu.VMEM((CONV_PAD + s, tc), F32), pltpu.VMEM((s + CONV_PAD, tc), F32)],
        compiler_params=_cparams(("parallel",), 56),
    )(*ins)


def _ssd_consts():
    e = np.zeros((LANE, SSD_HEADS * SSD_P), np.float32)
    for h in range(SSD_HEADS):
        e[h, h * SSD_P:(h + 1) * SSD_P] = 1.0
    ltri = np.tril(np.ones((CHUNK, CHUNK), np.float32))
    return jnp.asarray(e), jnp.asarray(e.T.copy()), jnp.asarray(ltri), jnp.asarray(ltri.T.copy())


def _split3(x):
    hi = x.astype(BF16)
    r = x - hi.astype(F32)
    mid = r.astype(BF16)
    lo = (r - mid.astype(F32)).astype(BF16)
    return hi, mid, lo


def _dot_sel(x, sel, x_left=True):
    sb = sel.astype(BF16)
    out = None
    for part in _split3(x):
        t = jnp.dot(part, sb, preferred_element_type=F32) if x_left else jnp.dot(sb, part, preferred_element_type=F32)
        out = t if out is None else out + t
    return out


def _ssd_chunk_terms(dtr_ref, dtrt_ref, bias_ref, biast_ref, alog_ref, alogt_ref, e_ref, ltri_ref, utri_ref):
    a_neg = -jnp.exp(alog_ref[...])
    dt = _softplus(dtr_ref[...] + bias_ref[...])
    a = dt * a_neg
    s = _dot_sel(a, ltri_ref[...], x_left=False)
    dtt = _softplus(dtrt_ref[...] + biast_ref[...])
    st = _dot_sel(dtt * (-jnp.exp(alogt_ref[...])), utri_ref[...])
    ev = e_ref[...]
    s_x = _dot_sel(s, ev)
    dt_x = _dot_sel(dt, ev)
    return a_neg, dt, s, st, s_x, dt_x


def _ssd_decay(s, st, h, tril):
    seg = s[:, h:h + 1] - st[h:h + 1, :]
    return jnp.exp(jnp.where(tril, seg, NEG))


def _ssd_decay_t(s, st, h, triu):
    seg = st[h:h + 1, :] - s[:, h:h + 1]
    return jnp.exp(jnp.where(triu, seg, NEG))


def _head_masks():
    lane = lax.broadcasted_iota(jnp.int32, (1, SSD_P * 4), 1)
    return [((lane >= r * SSD_P) & (lane < (r + 1) * SSD_P)).astype(F32) for r in range(4)]


def _ssd_scan_fwd(xbc, dtr, dtrt, bias, biast, alog, alogt, dskip, *, name):
    s_len = xbc.shape[0]
    nc = s_len // CHUNK
    e, et, ltri, utri = _ssd_consts()
    gw = SSD_P * 4

    def body(xs_ref, b_ref, c_ref, dtr_ref, dtrt_ref, bias_ref, biast_ref, alog_ref, alogt_ref, d_ref,
             e_ref, ltri_ref, utri_ref, y_ref, prev_ref, state):
        @pl.when(pl.program_id(0) == 0)
        def _():
            state[...] = jnp.zeros_like(state)

        a_neg, dt, s, st, s_x, dt_x = _ssd_chunk_terms(dtr_ref, dtrt_ref, bias_ref, biast_ref, alog_ref,
                                                       alogt_ref, e_ref, ltri_ref, utri_ref)
        s_last = s_x[CHUNK - 1:CHUNK, :]
        es_x = jnp.exp(s_x)
        w_x = jnp.exp(s_last - s_x)
        cd_x = jnp.exp(s_last)
        d_x = _dot_sel(jnp.broadcast_to(d_ref[...], (8, LANE)), e_ref[...])[0:1, :]
        xs = xs_ref[...]
        xv = xs * dt_x
        row = lax.broadcasted_iota(jnp.int32, (CHUNK, CHUNK), 0)
        colm = lax.broadcasted_iota(jnp.int32, (CHUNK, CHUNK), 1)
        tril = colm <= row
        masks = _head_masks()
        for g in range(SSD_GROUPS):
            gs = slice(g * gw, (g + 1) * gw)
            bg = b_ref[:, g * SSD_N:(g + 1) * SSD_N].astype(BF16)
            cg = c_ref[:, g * SSD_N:(g + 1) * SSD_N].astype(BF16)
            xg = xv[:, gs]
            hg = state[g]
            prev_ref[g] = hg
            cb = lax.dot_general(cg, bg, (((1,), (1,)), ((), ())), preferred_element_type=F32)
            yg = jnp.dot(cg, hg.astype(BF16), preferred_element_type=F32) * es_x[:, gs]
            for r in range(4):
                m = (cb * _ssd_decay(s, st, g * 4 + r, tril)).astype(BF16)
                yg = yg + jnp.dot(m, (xg * masks[r]).astype(BF16), preferred_element_type=F32)
            y_ref[:, gs] = yg + d_x[:, gs] * xs[:, gs]
            upd = lax.dot_general(bg, (xg * w_x[:, gs]).astype(BF16), (((0,), (0,)), ((), ())),
                                  preferred_element_type=F32)
            state[g] = hg * cd_x[:, gs] + upd

    nh = LANE
    chunk_row = lambda w, cb: pl.BlockSpec((CHUNK, w), lambda i, cb=cb: (i, cb))
    full = lambda a: pl.BlockSpec(a.shape, lambda i: (0,) * a.ndim)
    in_specs = [chunk_row(1024, 0), chunk_row(512, 2), chunk_row(512, 3), chunk_row(nh, 0),
                pl.BlockSpec((None, nh, CHUNK), lambda i: (i, 0, 0)),
                full(bias), full(biast), full(alog), full(alogt), full(dskip), full(e), full(ltri), full(utri)]
    return pl.pallas_call(
        body, name=name, grid=(nc,), in_specs=in_specs,
        out_specs=[pl.BlockSpec((CHUNK, 1024), lambda i: (i, 0)),
                   pl.BlockSpec((None, SSD_GROUPS, SSD_N, gw), lambda i: (i, 0, 0, 0))],
        out_shape=[jax.ShapeDtypeStruct((s_len, 1024), F32),
                   jax.ShapeDtypeStruct((nc, SSD_GROUPS, SSD_N, gw), F32)],
        scratch_shapes=[pltpu.VMEM((SSD_GROUPS, SSD_N, gw), F32)],
        compiler_params=_cparams(("arbitrary",)),
    )(xbc, xbc, xbc, dtr, dtrt, bias, biast, alog, alogt, dskip, e, ltri, utri)


def _ssd_scan_bwd(xbc, dtr, dtrt, bias, biast, alog, alogt, dskip, prev, dy, *, name):
    s_len = xbc.shape[0]
    nc = s_len // CHUNK
    e, et, ltri, utri = _ssd_consts()
    gw = SSD_P * 4

    def body(xs_ref, b_ref, c_ref, dtr_ref, dtrt_ref, bias_ref, biast_ref, alog_ref, alogt_ref, d_ref,
             e_ref, et_ref, ltri_ref, utri_ref, prev_ref, dy_ref,
             dxs_ref, db_ref, dc_ref, ddtr_ref, dalog_ref, dbias_ref, dd_ref, dstate):
        step = pl.program_id(0)

        @pl.when(step == 0)
        def _():
            dstate[...] = jnp.zeros_like(dstate)

        a_neg, dt, s, st, s_x, dt_x = _ssd_chunk_terms(dtr_ref, dtrt_ref, bias_ref, biast_ref, alog_ref,
                                                       alogt_ref, e_ref, ltri_ref, utri_ref)
        s_last = s_x[CHUNK - 1:CHUNK, :]
        es_x = jnp.exp(s_x)
        w_x = jnp.exp(s_last - s_x)
        cd_x = jnp.exp(s_last)
        d_x = _dot_sel(jnp.broadcast_to(d_ref[...], (8, LANE)), e_ref[...])[0:1, :]
        xs = xs_ref[...]
        xv = xs * dt_x
        dyv = dy_ref[...]
        row = lax.broadcasted_iota(jnp.int32, (CHUNK, CHUNK), 0)
        colm = lax.broadcasted_iota(jnp.int32, (CHUNK, CHUNK), 1)
        tril = colm <= row
        masks = _head_masks()
        is_last = lax.broadcasted_iota(jnp.int32, (CHUNK, 1), 0) == CHUNK - 1
        nt = (((1,), (1,)), ((), ()))
        tn = (((0,), (0,)), ((), ()))
        ds_parts, ddt_parts = [], []
        head_lane = lax.broadcasted_iota(jnp.int32, (CHUNK, LANE), 1)
        triu = colm >= row
        ds_diag = jnp.zeros((CHUNK, LANE), F32)
        for g in range(SSD_GROUPS):
            gs = slice(g * gw, (g + 1) * gw)
            bg = b_ref[:, g * SSD_N:(g + 1) * SSD_N].astype(BF16)
            cg = c_ref[:, g * SSD_N:(g + 1) * SSD_N].astype(BF16)
            xg = xv[:, gs]
            xgb = xg.astype(BF16)
            hg = prev_ref[g]
            hgb = hg.astype(BF16)
            dsg = dstate[g]
            dsgb = dsg.astype(BF16)
            dyg = dyv[:, gs]
            dye = (dyg * es_x[:, gs]).astype(BF16)
            xw = (xg * w_x[:, gs]).astype(BF16)
            cb = lax.dot_general(cg, bg, nt, preferred_element_type=F32)
            cbt = lax.dot_general(bg, cg, nt, preferred_element_type=F32)
            dcg = lax.dot_general(dye, hgb, nt, preferred_element_type=F32)
            dh = lax.dot_general(cg, dye, tn, preferred_element_type=F32)
            bds = jnp.dot(bg, dsgb, preferred_element_type=F32)
            yoff = es_x[:, gs] * jnp.dot(cg, hgb, preferred_element_type=F32)
            dx_state = w_x[:, gs] * bds
            dbg = lax.dot_general(xw, dsgb, nt, preferred_element_type=F32)
            dxd = jnp.zeros((CHUNK, gw), F32)
            dcb = jnp.zeros((CHUNK, CHUNK), F32)
            for r in range(4):
                dec = _ssd_decay(s, st, g * 4 + r, tril)
                mf = cb * dec
                m = mf.astype(BF16)
                dym = (dyg * masks[r]).astype(BF16)
                dm = lax.dot_general(dym, xgb, nt, preferred_element_type=F32)
                dxd = dxd + lax.dot_general(m, dym, tn, preferred_element_type=F32)
                dcb = dcb + dm * dec
                dmt = lax.dot_general(xgb, dym, nt, preferred_element_type=F32)
                rc = (jnp.sum(dm * mf, axis=1, keepdims=True)
                      - jnp.sum(dmt * cbt * _ssd_decay_t(s, st, g * 4 + r, triu), axis=1, keepdims=True))
                ds_diag = ds_diag + jnp.where(head_lane == g * 4 + r, rc, 0.0)
            dcbb = dcb.astype(BF16)
            dcg = dcg + jnp.dot(dcbb, bg, preferred_element_type=F32)
            dbg = dbg + lax.dot_general(dcbb, cg, tn, preferred_element_type=F32)
            dxg = dxd + dx_state
            extra = (jnp.sum(xg * dx_state, axis=0, keepdims=True)
                     + cd_x[:, gs] * jnp.sum(dsg * hg, axis=0, keepdims=True))
            ds_parts.append(dyg * yoff - xg * dx_state + jnp.where(is_last, extra, 0.0))
            ddt_parts.append(dxg * xs[:, gs])
            dxs_ref[:, gs] = dxg * dt_x[:, gs] + d_x[:, gs] * dyg
            db_ref[:, g * SSD_N:(g + 1) * SSD_N] = dbg
            dc_ref[:, g * SSD_N:(g + 1) * SSD_N] = dcg
            dstate[g] = cd_x[:, gs] * dsg + dh
        etv = et_ref[...]
        ds = ds_diag + _dot_sel(jnp.concatenate(ds_parts, axis=1), etv)
        da = _dot_sel(ds, utri_ref[...], x_left=False)
        ddt = da * a_neg + _dot_sel(jnp.concatenate(ddt_parts, axis=1), etv)
        ddtr = ddt * _sigmoid(dtr_ref[...] + bias_ref[...])
        ddtr_ref[...] = ddtr.astype(ddtr_ref.dtype)
        p_alog = jnp.sum(da * dt, axis=0, keepdims=True) * a_neg
        p_bias = jnp.sum(ddtr, axis=0, keepdims=True)
        p_d = _dot_sel(jnp.broadcast_to(jnp.sum(dyv * xs, axis=0, keepdims=True), (8, SSD_HEADS * SSD_P)),
                       etv)[0:1, :]

        @pl.when(step == 0)
        def _():
            dalog_ref[...] = p_alog
            dbias_ref[...] = p_bias
            dd_ref[...] = p_d

        @pl.when(step > 0)
        def _():
            dalog_ref[...] += p_alog
            dbias_ref[...] += p_bias
            dd_ref[...] += p_d

    nh = LANE
    rev = lambda i: nc - 1 - i
    chunk_row = lambda w, cb: pl.BlockSpec((CHUNK, w), lambda i, cb=cb: (rev(i), cb))
    full = lambda a: pl.BlockSpec(a.shape, lambda i: (0,) * a.ndim)
    vec = pl.BlockSpec((1, nh), lambda i: (0, 0))
    in_specs = [chunk_row(1024, 0), chunk_row(512, 2), chunk_row(512, 3), chunk_row(nh, 0),
                pl.BlockSpec((None, nh, CHUNK), lambda i: (rev(i), 0, 0)),
                full(bias), full(biast), full(alog), full(alogt), full(dskip), full(e), full(et), full(ltri),
                full(utri),
                pl.BlockSpec((None, SSD_GROUPS, SSD_N, gw), lambda i: (rev(i), 0, 0, 0)), chunk_row(1024, 0)]
    return pl.pallas_call(
        body, name=name, grid=(nc,), in_specs=in_specs,
        out_specs=[chunk_row(1024, 0), chunk_row(512, 0), chunk_row(512, 0), chunk_row(nh, 0), vec, vec, vec],
        out_shape=[jax.ShapeDtypeStruct((s_len, 1024), F32), jax.ShapeDtypeStruct((s_len, 512), F32),
                   jax.ShapeDtypeStruct((s_len, 512), F32), jax.ShapeDtypeStruct((s_len, nh), BF16),
                   jax.ShapeDtypeStruct((1, nh), F32), jax.ShapeDtypeStruct((1, nh), F32),
                   jax.ShapeDtypeStruct((1, nh), F32)],
        scratch_shapes=[pltpu.VMEM((SSD_GROUPS, SSD_N, gw), F32)],
        compiler_params=_cparams(("arbitrary",)),
    )(xbc, xbc, xbc, dtr, dtrt, bias, biast, alog, alogt, dskip, e, et, ltri, utri, prev, dy)


def _ssd_gate_fwd(y, z, g, *, name):
    r, w = y.shape
    tr = _pick(r, 512, 8)
    gw = w // SSD_GROUPS

    def body(y_ref, z_ref, g_ref, o_ref):
        for k in range(SSD_GROUPS):
            cs = slice(k * gw, (k + 1) * gw)
            t = y_ref[:, cs] * _silu(z_ref[:, cs])
            rstd = lax.rsqrt(jnp.mean(t * t, axis=-1, keepdims=True) + EPS)
            o_ref[:, cs] = (t * rstd * g_ref[:, cs]).astype(o_ref.dtype)

    row = pl.BlockSpec((tr, w), lambda i: (i, 0))
    vec = pl.BlockSpec((1, w), lambda i: (0, 0))
    return pl.pallas_call(
        body, name=name, grid=(r // tr,), in_specs=[row, row, vec], out_specs=row,
        out_shape=jax.ShapeDtypeStruct((r, w), BF16), compiler_params=_cparams(("parallel",)),
    )(y, z, g)


def _ssd_gate_bwd(y, z, g, do, *, name):
    r, w = y.shape
    tr = _pick(r, 512, 8)
    gw = w // SSD_GROUPS

    def body(y_ref, z_ref, g_ref, do_ref, dy_ref, dz_ref, dg_ref):
        parts = []
        for k in range(SSD_GROUPS):
            cs = slice(k * gw, (k + 1) * gw)
            yv = y_ref[:, cs]
            zv = z_ref[:, cs]
            sz = _silu(zv)
            t = yv * sz
            rstd = lax.rsqrt(jnp.mean(t * t, axis=-1, keepdims=True) + EPS)
            th = t * rstd
            dov = do_ref[:, cs].astype(F32)
            dog = dov * g_ref[:, cs]
            dt = rstd * (dog - th * jnp.mean(dog * th, axis=-1, keepdims=True))
            dy_ref[:, cs] = dt * sz
            dz_ref[:, cs] = (dt * yv * _dsilu(zv)).astype(dz_ref.dtype)
            parts.append(jnp.sum(dov * th, axis=0, keepdims=True))
        pg = jnp.concatenate(parts, axis=1)

        @pl.when(pl.program_id(0) == 0)
        def _():
            dg_ref[...] = pg

        @pl.when(pl.program_id(0) > 0)
        def _():
            dg_ref[...] += pg

    row = pl.BlockSpec((tr, w), lambda i: (i, 0))
    vec = pl.BlockSpec((1, w), lambda i: (0, 0))
    return pl.pallas_call(
        body, name=name, grid=(r // tr,), in_specs=[row, row, vec, row], out_specs=[row, row, vec],
        out_shape=[jax.ShapeDtypeStruct((r, w), F32), jax.ShapeDtypeStruct((r, w), BF16),
                   jax.ShapeDtypeStruct((1, w), F32)],
        compiler_params=_cparams(("arbitrary",)),
    )(y, z, g, do)


def _rope_swap(x):
    lane = lax.broadcasted_iota(jnp.int32, x.shape, 1)
    lo = pltpu.roll(x, 96, 1)
    hi = pltpu.roll(x, 32, 1)
    return jnp.where(lane < 32, lo, jnp.where(lane < 64, hi, 0.0))


def _norm_part(v, g, n):
    rstd = lax.rsqrt(jnp.sum(v * v, axis=-1, keepdims=True) * (1.0 / n) + EPS)
    xh = v * rstd
    return xh * g, xh, rstd


def _norm_part_bwd(dout, g, xh, rstd, n):
    dg = dout * g
    return rstd * (dg - xh * (jnp.sum(dg * xh, axis=-1, keepdims=True) * (1.0 / n)))


def _mla_prep_fwd(q, kv, krr, cs, sn, gq, gk, *, name):
    s = q.shape[0]
    tr = _pick(s, 256, 8)
    hp = MLA_HP

    def body(q_ref, kv_ref, krr_ref, cs_ref, sn_ref, gq_ref, gk_ref, qf_ref, kf_ref, v_ref):
        csv, snv = cs_ref[...], sn_ref[...]
        gqn, gqr = gq_ref[:, 0:128], gq_ref[:, 128:256]
        gkn, gkr = gk_ref[:, 0:128], gk_ref[:, 128:256]
        kr, _, _ = _norm_part(krr_ref[...], gkr, MLA_ROPE)
        kr = (kr * csv + _rope_swap(kr) * snv).astype(BF16)
        for h in range(MLA_HEADS):
            qn, _, _ = _norm_part(q_ref[:, h * hp:h * hp + 128], gqn, MLA_NOPE)
            qr, _, _ = _norm_part(q_ref[:, h * hp + 128:(h + 1) * hp], gqr, MLA_ROPE)
            qr = qr * csv + _rope_swap(qr) * snv
            qf_ref[h, :, 0:128] = qn.astype(BF16)
            qf_ref[h, :, 128:256] = qr.astype(BF16)
            kn, _, _ = _norm_part(kv_ref[:, h * hp:h * hp + 128], gkn, MLA_NOPE)
            kf_ref[h, :, 0:128] = kn.astype(BF16)
            kf_ref[h, :, 128:256] = kr
            v_ref[h] = kv_ref[:, h * hp + 128:(h + 1) * hp].astype(BF16)

    row = lambda w: pl.BlockSpec((tr, w), lambda i: (i, 0))
    vec = pl.BlockSpec((1, hp), lambda i: (0, 0))
    hrow = lambda w: pl.BlockSpec((MLA_HEADS, tr, w), lambda i: (0, i, 0))
    return pl.pallas_call(
        body, name=name, grid=(s // tr,),
        in_specs=[row(MLA_HEADS * hp), row(MLA_HEADS * hp), row(128), row(128), row(128), vec, vec],
        out_specs=[hrow(hp), hrow(hp), hrow(128)],
        out_shape=[jax.ShapeDtypeStruct((MLA_HEADS, s, hp), BF16), jax.ShapeDtypeStruct((MLA_HEADS, s, hp), BF16),
                   jax.ShapeDtypeStruct((MLA_HEADS, s, 128), BF16)],
        compiler_params=_cparams(("parallel",)),
    )(q, kv, krr, cs, sn, gq, gk)


def _mla_prep_bwd(q, kv, krr, cs, sn, gq, gk, dqf, dkf, dv, *, name):
    s = q.shape[0]
    tr = _pick(s, 256, 8)
    hp = MLA_HP

    def body(q_ref, kv_ref, krr_ref, cs_ref, sn_ref, gq_ref, gk_ref, dqf_ref, dkf_ref, dv_ref,
             dq_ref, dkv_ref, dkrr_ref, dgq_ref, dgk_ref):
        csv, snv = cs_ref[...], sn_ref[...]
        gqn, gqr = gq_ref[:, 0:128], gq_ref[:, 128:256]
        gkn, gkr = gk_ref[:, 0:128], gk_ref[:, 128:256]
        _, krh, krs = _norm_part(krr_ref[...], gkr, MLA_ROPE)
        dkr_sum = jnp.zeros((tr, 128), F32)
        pgqn = jnp.zeros((1, 128), F32)
        pgqr = jnp.zeros((1, 128), F32)
        pgkn = jnp.zeros((1, 128), F32)
        for h in range(MLA_HEADS):
            _, qnh, qns = _norm_part(q_ref[:, h * hp:h * hp + 128], gqn, MLA_NOPE)
            _, qrh, qrs = _norm_part(q_ref[:, h * hp + 128:(h + 1) * hp], gqr, MLA_ROPE)
            dqn = dqf_ref[h, :, 0:128]
            drr = dqf_ref[h, :, 128:256]
            dqr = drr * csv + _rope_swap(drr * snv)
            dq_ref[:, h * hp:h * hp + 128] = _norm_part_bwd(dqn, gqn, qnh, qns, MLA_NOPE).astype(dq_ref.dtype)
            dq_ref[:, h * hp + 128:(h + 1) * hp] = _norm_part_bwd(dqr, gqr, qrh, qrs, MLA_ROPE).astype(dq_ref.dtype)
            pgqn = pgqn + jnp.sum(dqn * qnh, axis=0, keepdims=True)
            pgqr = pgqr + jnp.sum(dqr * qrh, axis=0, keepdims=True)
            _, knh, kns = _norm_part(kv_ref[:, h * hp:h * hp + 128], gkn, MLA_NOPE)
            dkn = dkf_ref[h, :, 0:128]
            dkv_ref[:, h * hp:h * hp + 128] = _norm_part_bwd(dkn, gkn, knh, kns, MLA_NOPE).astype(dkv_ref.dtype)
            dkv_ref[:, h * hp + 128:(h + 1) * hp] = dv_ref[h].astype(dkv_ref.dtype)
            pgkn = pgkn + jnp.sum(dkn * knh, axis=0, keepdims=True)
            dkr_sum = dkr_sum + dkf_ref[h, :, 128:256]
        dkr = dkr_sum * csv + _rope_swap(dkr_sum * snv)
        dkrr_ref[...] = _norm_part_bwd(dkr, gkr, krh, krs, MLA_ROPE).astype(dkrr_ref.dtype)
        pgkr = jnp.sum(dkr * krh, axis=0, keepdims=True)
        pq = jnp.concatenate([pgqn, pgqr], axis=1)
        pk = jnp.concatenate([pgkn, pgkr], axis=1)

        @pl.when(pl.program_id(0) == 0)
        def _():
            dgq_ref[...] = pq
            dgk_ref[...] = pk

        @pl.when(pl.program_id(0) > 0)
        def _():
            dgq_ref[...] += pq
            dgk_ref[...] += pk

    row = lambda w: pl.BlockSpec((tr, w), lambda i: (i, 0))
    vec = pl.BlockSpec((1, hp), lambda i: (0, 0))
    hrow = lambda w: pl.BlockSpec((MLA_HEADS, tr, w), lambda i: (0, i, 0))
    return pl.pallas_call(
        body, name=name, grid=(s // tr,),
        in_specs=[row(MLA_HEADS * hp), row(MLA_HEADS * hp), row(128), row(128), row(128), vec, vec,
                  hrow(hp), hrow(hp), hrow(128)],
        out_specs=[row(MLA_HEADS * hp), row(MLA_HEADS * hp), row(128), vec, vec],
        out_shape=[jax.ShapeDtypeStruct((s, MLA_HEADS * hp), BF16), jax.ShapeDtypeStruct((s, MLA_HEADS * hp), BF16),
                   jax.ShapeDtypeStruct((s, 128), BF16), jax.ShapeDtypeStruct((1, hp), F32),
                   jax.ShapeDtypeStruct((1, hp), F32)],
        compiler_params=_cparams(("arbitrary",), 48),
    )(q, kv, krr, cs, sn, gq, gk, dqf, dkf, dv)


ATT_T = 512


def _chunk_mask(t):
    r = lax.shift_right_logical(lax.broadcasted_iota(jnp.int32, (t, t), 0), 6)
    c = lax.shift_right_logical(lax.broadcasted_iota(jnp.int32, (t, t), 1), 6)
    return c <= r


def _mla_attn_fwd(qf, kf, v, *, name):
    nh, s, hp = qf.shape
    t = _pick(s, ATT_T, CHUNK)
    scale = (MLA_NOPE + MLA_ROPE) ** -0.5
    nt = (((1,), (1,)), ((), ()))

    def body(q_ref, k_ref, v_ref, o_ref, lse_ref):
        i = pl.program_id(1)
        q = q_ref[...]

        def block(j, carry, masked):
            m, l, acc = carry
            start = pl.multiple_of(j * t, t)
            k = k_ref[pl.ds(start, t), :]
            sc = lax.dot_general(q, k, nt, preferred_element_type=F32) * scale
            if masked:
                sc = jnp.where(_chunk_mask(t), sc, NEG)
            m_new = jnp.maximum(m, jnp.max(sc, axis=-1, keepdims=True))
            alpha = jnp.exp(m - m_new)
            p = jnp.exp(sc - m_new)
            l = alpha * l + jnp.sum(p, axis=-1, keepdims=True)
            acc = alpha * acc + jnp.dot(p.astype(BF16), v_ref[pl.ds(start, t), :], preferred_element_type=F32)
            return m_new, l, acc

        init = (jnp.full((t, 1), NEG, F32), jnp.zeros((t, 1), F32), jnp.zeros((t, MLA_V), F32))
        carry = lax.fori_loop(0, i, lambda j, c: block(j, c, False), init)
        m, l, acc = block(i, carry, True)
        o_ref[...] = acc / l
        lse_ref[...] = m + jnp.log(l)

    return pl.pallas_call(
        body, name=name, grid=(nh, s // t),
        in_specs=[pl.BlockSpec((None, t, hp), lambda h, i: (h, i, 0)),
                  pl.BlockSpec((None, s, hp), lambda h, i: (h, 0, 0)),
                  pl.BlockSpec((None, s, MLA_V), lambda h, i: (h, 0, 0))],
        out_specs=[pl.BlockSpec((t, MLA_V), lambda h, i: (i, h)),
                   pl.BlockSpec((None, t, 1), lambda h, i: (h, i, 0))],
        out_shape=[jax.ShapeDtypeStruct((s, nh * MLA_V), F32), jax.ShapeDtypeStruct((nh, s, 1), F32)],
        compiler_params=_cparams(("parallel", "arbitrary"), 48),
    )(qf, kf, v)


def _mla_attn_bwd(qf, kf, v, o, lse, do, *, name):
    nh, s, hp = qf.shape
    t = _pick(s, ATT_T, CHUNK)
    nb = s // t
    scale = (MLA_NOPE + MLA_ROPE) ** -0.5
    nt = (((1,), (1,)), ((), ()))
    tn = (((0,), (0,)), ((), ()))

    def body(q_ref, k_ref, v_ref, o_ref, lse_ref, do_ref, dq_ref, dk_ref, dv_ref, delta):
        j = pl.program_id(1)

        @pl.when(j == 0)
        def _():
            dq_ref[...] = jnp.zeros_like(dq_ref)
            delta[...] = jnp.sum(do_ref[...] * o_ref[...], axis=-1, keepdims=True)

        k = k_ref[...]
        vv = v_ref[...]

        def block(i, carry, masked):
            dk, dv = carry
            start = pl.multiple_of(i * t, t)
            q = q_ref[pl.ds(start, t), :]
            dob = do_ref[pl.ds(start, t), :].astype(BF16)
            sc = lax.dot_general(q, k, nt, preferred_element_type=F32) * scale
            if masked:
                sc = jnp.where(_chunk_mask(t), sc, NEG)
            p = jnp.exp(sc - lse_ref[pl.ds(start, t), :])
            dp = lax.dot_general(dob, vv, nt, preferred_element_type=F32)
            ds = (p * (dp - delta[pl.ds(start, t), :]) * scale).astype(BF16)
            dv = dv + lax.dot_general(p.astype(BF16), dob, tn, preferred_element_type=F32)
            dk = dk + lax.dot_general(ds, q, tn, preferred_element_type=F32)
            dq_ref[pl.ds(start, t), :] += jnp.dot(ds, k, preferred_element_type=F32)
            return dk, dv

        init = (jnp.zeros((t, hp), F32), jnp.zeros((t, MLA_V), F32))
        carry = block(j, init, True)
        dk, dv = lax.fori_loop(j + 1, nb, lambda i, c: block(i, c, False), carry)
        dk_ref[...] = dk
        dv_ref[...] = dv

    whole = lambda w: pl.BlockSpec((None, s, w), lambda h, j: (h, 0, 0))
    blk = lambda w: pl.BlockSpec((None, t, w), lambda h, j: (h, j, 0))
    colh = pl.BlockSpec((s, MLA_V), lambda h, j: (0, h))
    return pl.pallas_call(
        body, name=name, grid=(nh, nb),
        in_specs=[whole(hp), blk(hp), blk(MLA_V), colh, whole(1), colh],
        out_specs=[whole(hp), blk(hp), blk(MLA_V)],
        out_shape=[jax.ShapeDtypeStruct((nh, s, hp), F32), jax.ShapeDtypeStruct((nh, s, hp), F32),
                   jax.ShapeDtypeStruct((nh, s, MLA_V), F32)],
        scratch_shapes=[pltpu.VMEM((s, 1), F32)],
        compiler_params=_cparams(("parallel", "arbitrary"), 56),
    )(qf, kf, v, o, lse, do)


def _merge_fwd(gl, gb, ys, yc, ym, *, name):
    s, d = ys.shape
    tr = _pick(s, 256, 8)

    def body(gl_ref, gb_ref, ys_ref, yc_ref, ym_ref, o_ref):
        acc = jnp.zeros((tr, d), F32)
        for k, y_ref in enumerate((ys_ref, yc_ref, ym_ref)):
            gt = _sigmoid(gl_ref[:, k * d:(k + 1) * d] + gb_ref[:, k * d:(k + 1) * d])
            acc = acc + gt * y_ref[...]
        o_ref[...] = acc.astype(o_ref.dtype)

    row = lambda w: pl.BlockSpec((tr, w), lambda i: (i, 0))
    return pl.pallas_call(
        body, name=name, grid=(s // tr,),
        in_specs=[row(3 * d), pl.BlockSpec((1, 3 * d), lambda i: (0, 0)), row(d), row(d), row(d)],
        out_specs=row(d), out_shape=jax.ShapeDtypeStruct((s, d), BF16),
        compiler_params=_cparams(("parallel",)),
    )(gl, gb, ys, yc, ym)


def _merge_bwd(gl, gb, ys, yc, ym, dm, *, name):
    s, d = ys.shape
    tr = _pick(s, 256, 8)

    def body(gl_ref, gb_ref, ys_ref, yc_ref, ym_ref, dm_ref, dgl_ref, dgb_ref, dys_ref, dyc_ref, dym_ref):
        dmv = dm_ref[...]
        parts = []
        for k, (y_ref, dy_ref) in enumerate(((ys_ref, dys_ref), (yc_ref, dyc_ref), (ym_ref, dym_ref))):
            gt = _sigmoid(gl_ref[:, k * d:(k + 1) * d] + gb_ref[:, k * d:(k + 1) * d])
            dy_ref[...] = (gt * dmv).astype(dy_ref.dtype)
            dl = dmv * y_ref[...] * gt * (1.0 - gt)
            dgl_ref[:, k * d:(k + 1) * d] = dl.astype(dgl_ref.dtype)
            parts.append(jnp.sum(dl, axis=0, keepdims=True))
        pb = jnp.concatenate(parts, axis=1)

        @pl.when(pl.program_id(0) == 0)
        def _():
            dgb_ref[...] = pb

        @pl.when(pl.program_id(0) > 0)
        def _():
            dgb_ref[...] += pb

    row = lambda w: pl.BlockSpec((tr, w), lambda i: (i, 0))
    vec = pl.BlockSpec((1, 3 * d), lambda i: (0, 0))
    return pl.pallas_call(
        body, name=name, grid=(s // tr,),
        in_specs=[row(3 * d), vec, row(d), row(d), row(d), row(d)],
        out_specs=[row(3 * d), vec, row(d), row(d), row(d)],
        out_shape=[jax.ShapeDtypeStruct((s, 3 * d), BF16), jax.ShapeDtypeStruct((1, 3 * d), F32)]
        + [jax.ShapeDtypeStruct((s, d), BF16)] * 3,
        compiler_params=_cparams(("arbitrary",)),
    )(gl, gb, ys, yc, ym, dm)


def _xattn_fwd(q, k, v, gq, *, name):
    s, d = q.shape
    dh = d // X_HEADS
    tr = _pick(s, 512, 8)
    scale = dh ** -0.5
    nt = (((1,), (1,)), ((), ()))

    def body(q_ref, k_ref, v_ref, gq_ref, o_ref):
        for h in range(X_HEADS):
            cs = slice(h * dh, (h + 1) * dh)
            qn, _, _ = _norm_part(q_ref[:, cs], gq_ref[...], dh)
            sc = lax.dot_general(qn.astype(BF16), k_ref[:, cs], nt, preferred_element_type=F32) * scale
            p = jnp.exp(sc - jnp.max(sc, axis=-1, keepdims=True))
            p = p / jnp.sum(p, axis=-1, keepdims=True)
            o_ref[:, cs] = jnp.dot(p.astype(BF16), v_ref[:, cs], preferred_element_type=F32)

    row = pl.BlockSpec((tr, d), lambda i: (i, 0))
    mem = pl.BlockSpec(k.shape, lambda i: (0, 0))
    return pl.pallas_call(
        body, name=name, grid=(s // tr,),
        in_specs=[row, mem, mem, pl.BlockSpec((1, dh), lambda i: (0, 0))], out_specs=row,
        out_shape=jax.ShapeDtypeStruct((s, d), F32), compiler_params=_cparams(("parallel",)),
    )(q, k, v, gq)


def _xattn_bwd(q, k, v, gq, do, *, name):
    s, d = q.shape
    dh = d // X_HEADS
    tr = _pick(s, 512, 8)
    scale = dh ** -0.5
    nt = (((1,), (1,)), ((), ()))
    tn = (((0,), (0,)), ((), ()))

    def body(q_ref, k_ref, v_ref, gq_ref, do_ref, dq_ref, dk_ref, dv_ref, dgq_ref):
        first = pl.program_id(0) == 0
        pg = jnp.zeros((1, dh), F32)
        for h in range(X_HEADS):
            cs = slice(h * dh, (h + 1) * dh)
            qn, qh, qs = _norm_part(q_ref[:, cs], gq_ref[...], dh)
            qnb = qn.astype(BF16)
            kh = k_ref[:, cs]
            sc = lax.dot_general(qnb, kh, nt, preferred_element_type=F32) * scale
            p = jnp.exp(sc - jnp.max(sc, axis=-1, keepdims=True))
            p = p / jnp.sum(p, axis=-1, keepdims=True)
            dob = do_ref[:, cs].astype(BF16)
            dp = lax.dot_general(dob, v_ref[:, cs], nt, preferred_element_type=F32)
            ds = (p * (dp - jnp.sum(dp * p, axis=-1, keepdims=True)) * scale).astype(BF16)
            dqn = jnp.dot(ds, kh, preferred_element_type=F32)
            dq_ref[:, cs] = _norm_part_bwd(dqn, gq_ref[...], qh, qs, dh).astype(dq_ref.dtype)
            pg = pg + jnp.sum(dqn * qh, axis=0, keepdims=True)
            pv = lax.dot_general(p.astype(BF16), dob, tn, preferred_element_type=F32)
            pk = lax.dot_general(ds, qnb, tn, preferred_element_type=F32)

            @pl.when(first)
            def _():
                dv_ref[:, cs] = pv
                dk_ref[:, cs] = pk

            @pl.when(jnp.logical_not(first))
            def _():
                dv_ref[:, cs] += pv
                dk_ref[:, cs] += pk

        @pl.when(first)
        def _():
            dgq_ref[...] = pg

        @pl.when(jnp.logical_not(first))
        def _():
            dgq_ref[...] += pg

    row = pl.BlockSpec((tr, d), lambda i: (i, 0))
    mem = pl.BlockSpec(k.shape, lambda i: (0, 0))
    vec = pl.BlockSpec((1, dh), lambda i: (0, 0))
    return pl.pallas_call(
        body, name=name, grid=(s // tr,),
        in_specs=[row, mem, mem, vec, row], out_specs=[row, mem, mem, vec],
        out_shape=[jax.ShapeDtypeStruct((s, d), BF16), jax.ShapeDtypeStruct(k.shape, F32),
                   jax.ShapeDtypeStruct(k.shape, F32), jax.ShapeDtypeStruct((1, dh), F32)],
        compiler_params=_cparams(("arbitrary",)),
    )(q, k, v, gq, do)


def _swiglu_fwd(h1, *, name):
    s, w2 = h1.shape
    w = w2 // 2
    tr = _pick(s, 256, 8)
    tc = _pick(w, 1408, 128)
    ncb = w // tc

    def body(g_ref, u_ref, o_ref):
        o_ref[...] = (_silu(g_ref[...]) * u_ref[...]).astype(o_ref.dtype)

    return pl.pallas_call(
        body, name=name, grid=(s // tr, ncb),
        in_specs=[pl.BlockSpec((tr, tc), lambda i, j: (i, j)), pl.BlockSpec((tr, tc), lambda i, j: (i, j + ncb))],
        out_specs=pl.BlockSpec((tr, tc), lambda i, j: (i, j)),
        out_shape=jax.ShapeDtypeStruct((s, w), BF16), compiler_params=_cparams(("parallel", "parallel")),
    )(h1, h1)


def _swiglu_bwd(h1, dact, *, name):
    s, w2 = h1.shape
    w = w2 // 2
    tr = _pick(s, 256, 8)
    tc = _pick(w, 1408, 128)
    ncb = w // tc

    def body(g_ref, u_ref, d_ref, dg_ref, du_ref):
        gv = g_ref[...]
        dv = d_ref[...]
        dg_ref[...] = (dv * u_ref[...] * _dsilu(gv)).astype(dg_ref.dtype)
        du_ref[...] = (dv * _silu(gv)).astype(du_ref.dtype)

    blk = pl.BlockSpec((tr, tc), lambda i, j: (i, j))
    dg, du = pl.pallas_call(
        body, name=name, grid=(s // tr, ncb),
        in_specs=[blk, pl.BlockSpec((tr, tc), lambda i, j: (i, j + ncb)), blk],
        out_specs=[blk, blk],
        out_shape=[jax.ShapeDtypeStruct((s, w), BF16)] * 2, compiler_params=_cparams(("parallel", "parallel")),
    )(h1, h1, dact)
    return jnp.concatenate([dg, du], axis=1)


def _add(a, b, *, name):
    r, w = a.shape
    tr = _pick(r, 512, 8)

    def body(a_ref, b_ref, o_ref):
        o_ref[...] = a_ref[...] + b_ref[...].astype(F32)

    row = pl.BlockSpec((tr, w), lambda i: (i, 0))
    return pl.pallas_call(
        body, name=name, grid=(r // tr,), in_specs=[row, row], out_specs=row,
        out_shape=jax.ShapeDtypeStruct((r, w), F32), compiler_params=_cparams(("parallel",)),
    )(a, b)


def _loss(y, target, *, name):
    r, w = y.shape
    tr = _pick(r, 512, 8)

    def body(y_ref, t_ref, dy_ref, l_ref):
        err = y_ref[...] - t_ref[...]
        dy_ref[...] = err * (1.0 / w)
        part = jnp.zeros((8, LANE), F32) + 0.5 * jnp.sum(jnp.mean(err * err, axis=-1, keepdims=True))

        @pl.when(pl.program_id(0) == 0)
        def _():
            l_ref[...] = part

        @pl.when(pl.program_id(0) > 0)
        def _():
            l_ref[...] += part

    row = pl.BlockSpec((tr, w), lambda i: (i, 0))
    dy, l = pl.pallas_call(
        body, name=name, grid=(r // tr,), in_specs=[row, row],
        out_specs=[row, pl.BlockSpec((8, LANE), lambda i: (0, 0))],
        out_shape=[jax.ShapeDtypeStruct((r, w), F32), jax.ShapeDtypeStruct((8, LANE), F32)],
        compiler_params=_cparams(("arbitrary",)),
    )(y, target)
    return dy, l[0, 0]


def _adamw(w, g, m, v, *, name):
    r, c = w.shape
    tr = _pick(r, 256, 8)
    c1 = 1.0 - ADAM_B1 ** ADAM_STEP
    c2 = 1.0 - ADAM_B2 ** ADAM_STEP

    def body(w_ref, g_ref, m_ref, v_ref, d_ref, nm_ref, nv_ref):
        gv = g_ref[...]
        nm = ADAM_B1 * m_ref[...] + (1.0 - ADAM_B1) * gv
        nv = ADAM_B2 * v_ref[...] + (1.0 - ADAM_B2) * (gv * gv)
        nm_ref[...] = nm
        nv_ref[...] = nv
        d_ref[...] = -ADAM_LR * ((nm / c1) / (jnp.sqrt(nv / c2) + ADAM_EPS) + ADAM_WD * w_ref[...])

    row = pl.BlockSpec((tr, c), lambda i: (i, 0))
    return pl.pallas_call(
        body, name=name, grid=(r // tr,), in_specs=[row] * 4, out_specs=[row] * 3,
        out_shape=[jax.ShapeDtypeStruct((r, c), F32)] * 3, compiler_params=_cparams(("parallel",)),
    )(w, g, m, v)


IN_SPLIT = dict(z=(0, 1024), xbc=(1024, 3072), dt=(3072, 3200), glu=(3200, 5248), ql=(5248, 5632),
                ckv=(5632, 5888), kr=(5888, 6016), gate=(6016, 9088))


IN_WIDTH_PAD = 9216


def _w_in_pad(w):
    zeros = lambda n: jnp.zeros(w.shape[:-1] + (n,), w.dtype)
    return jnp.concatenate([w[..., :3088], zeros(112), w[..., 3088:5840], zeros(64), w[..., 5840:],
                            zeros(IN_WIDTH_PAD - 9088)], axis=-1)


def _w_in_unpad(g):
    return jnp.concatenate([g[..., :3088], g[..., 3200:5952], g[..., 6016:9088]], axis=-1)


def _qb_pad(w):
    lead = w.shape[:-1]
    w = w.reshape(lead + (MLA_HEADS, MLA_NOPE + MLA_ROPE))
    w = jnp.concatenate([w, jnp.zeros(lead + (MLA_HEADS, MLA_HP - MLA_NOPE - MLA_ROPE), w.dtype)], axis=-1)
    return w.reshape(lead + (MLA_HEADS * MLA_HP,))


def _qb_unpad(g):
    lead = g.shape[:-1]
    g = g.reshape(lead + (MLA_HEADS, MLA_HP))[..., :MLA_NOPE + MLA_ROPE]
    return g.reshape(lead + (MLA_HEADS * (MLA_NOPE + MLA_ROPE),))


def _pad_lanes(v, n):
    return jnp.concatenate([v, jnp.zeros((n - v.shape[0],), v.dtype)]).reshape(1, n)


def _layer_params(full, rep, l):
    p = {}
    w_in = full['w_in']
    for k, (a, b) in IN_SPLIT.items():
        p['w_' + k] = w_in[:, a:b]
    p['w_in'] = w_in
    for k in ('mla_w_q_b', 'mla_w_kv_b', 'xattn_w_kv', 'ffn_w_in', 'ssd_w_out', 'conv_w_out', 'mla_w_o', 'w_out',
              'xattn_w_q', 'xattn_w_o', 'ffn_w_out', 'ssd_conv_w', 'conv_dw_w'):
        p[k] = full[k]
    p['gate_b'] = full['gate_b'].reshape(1, -1)
    row = lambda name: rep[name][l].reshape(1, -1)
    for k in ('mix_norm_g', 'ssd_conv_b', 'ssd_norm_g', 'conv_dw_b', 'conv_ln_g', 'conv_ln_b', 'mla_q_a_g',
              'mla_kv_a_g', 'xattn_norm_g', 'mem_norm_g', 'xattn_q_norm_g', 'xattn_k_norm_g', 'ffn_norm_g'):
        p[k] = row(k)
    for k in ('ssd_dt_bias', 'ssd_a_log', 'ssd_d'):
        p[k] = _pad_lanes(rep[k][l], LANE)
        p[k + '_t'] = p[k].reshape(LANE, 1)
    p['gq'] = _pad_lanes(rep['mla_q_norm_g'][l], MLA_HP)
    p['gk'] = _pad_lanes(rep['mla_k_norm_g'][l], MLA_HP)
    return p


def _layer_fwd(x, mem, cs, sn, p, l):
    n = lambda s: f"l{l}_{s}"
    s_len, d = x.shape
    nc = s_len // CHUNK
    sv = {'x': x}
    u = _rms_fwd(x, p['mix_norm_g'], out_dtype=BF16, name=n("mix_norm"))
    z = _matmul(u, p['w_z'], name=n("in_z"))
    xbc = _matmul(u, p['w_xbc'], name=n("in_xbc"))
    dtr = _matmul(u, p['w_dt'], name=n("in_dt"))
    glu = _matmul(u, p['w_glu'], name=n("in_glu"))
    ql = _matmul(u, p['w_ql'], name=n("in_ql"))
    ckv = _matmul(u, p['w_ckv'], name=n("in_ckv"))
    krr = _matmul(u, p['w_kr'], name=n("in_kr"))
    gl = _matmul(u, p['w_gate'], name=n("in_gate"))
    pre_s, act_s = _conv_fwd(xbc, p['ssd_conv_w'], p['ssd_conv_b'], glu=False, act=True, name=n("ssd_conv"))
    dtrt = dtr.reshape(nc, CHUNK, LANE).transpose(0, 2, 1)
    y_scan, prev = _ssd_scan_fwd(act_s, dtr, dtrt, p['ssd_dt_bias'], p['ssd_dt_bias_t'], p['ssd_a_log'],
                                 p['ssd_a_log_t'], p['ssd_d'], name=n("ssd_scan"))
    yn = _ssd_gate_fwd(y_scan, z, p['ssd_norm_g'], name=n("ssd_gate"))
    y_ssd = _matmul(yn, p['ssd_w_out'], name=n("ssd_out"))
    pre_c, = _conv_fwd(glu, p['conv_dw_w'], p['conv_dw_b'], glu=True, act=False, name=n("dw_conv"))
    vc = _ln_silu_fwd(pre_c, p['conv_ln_g'], p['conv_ln_b'], name=n("conv_ln"))
    y_conv = _matmul(vc, p['conv_w_out'], name=n("conv_out"))
    qln = _rms_fwd(ql, p['mla_q_a_g'], out_dtype=BF16, name=n("q_a_norm"))
    q = _matmul(qln, p['mla_w_q_b'], name=n("q_b"))
    ckvn = _rms_fwd(ckv, p['mla_kv_a_g'], out_dtype=BF16, name=n("kv_a_norm"))
    kv = _matmul(ckvn, p['mla_w_kv_b'], name=n("kv_b"))
    qf, kf, v = _mla_prep_fwd(q, kv, krr, cs, sn, p['gq'], p['gk'], name=n("mla_prep"))
    o, lse = _mla_attn_fwd(qf, kf, v, name=n("mla_attn"))
    y_mla = _matmul(o, p['mla_w_o'], name=n("mla_out"))
    merged = _merge_fwd(gl, p['gate_b'], y_ssd, y_conv, y_mla, name=n("merge"))
    x1 = _matmul(merged, p['w_out'], add=x, name=n("mix_out"))
    hx = _rms_fwd(x1, p['xattn_norm_g'], out_dtype=BF16, name=n("xattn_norm"))
    qx = _matmul(hx, p['xattn_w_q'], name=n("xattn_q"))
    memn = _rms_fwd(mem, p['mem_norm_g'], out_dtype=BF16, name=n("mem_norm"))
    kvx = _matmul(memn, p['xattn_w_kv'], name=n("xattn_kv"))
    m_len = mem.shape[0]
    dh = d // X_HEADS
    kraw = kvx[:, :d].reshape(m_len * X_HEADS, dh)
    kx = _rms_fwd(kraw, p['xattn_k_norm_g'], out_dtype=BF16, name=n("xattn_k_norm")).reshape(m_len, d)
    vx = kvx[:, d:].astype(BF16)
    ox = _xattn_fwd(qx, kx, vx, p['xattn_q_norm_g'], name=n("xattn_core"))
    x2 = _matmul(ox, p['xattn_w_o'], add=x1, name=n("xattn_out"))
    hf = _rms_fwd(x2, p['ffn_norm_g'], out_dtype=BF16, name=n("ffn_norm"))
    h1 = _matmul(hf, p['ffn_w_in'], name=n("ffn_in"))
    act = _swiglu_fwd(h1, name=n("swiglu"))
    x3 = _matmul(act, p['ffn_w_out'], add=x2, name=n("ffn_out"))
    sv.update(u=u, z=z, xbc=xbc, dtr=dtr, dtrt=dtrt, glu=glu, ql=ql, ckv=ckv, krr=krr, gl=gl, pre_s=pre_s,
              act_s=act_s, y_scan=y_scan, prev=prev, yn=yn, y_ssd=y_ssd, pre_c=pre_c, vc=vc, y_conv=y_conv,
              qln=qln, q=q, ckvn=ckvn, kv=kv, qf=qf, kf=kf, v=v, o=o, lse=lse, y_mla=y_mla, merged=merged,
              x1=x1, hx=hx, qx=qx, memn=memn, kraw=kraw, kx=kx, vx=vx, ox=ox, x2=x2, hf=hf, h1=h1, act=act)
    return x3, sv


DW_KEY = dict(ffn_out_dw='ffn_w_out', ffn_in_dw='ffn_w_in', xattn_out_dw='xattn_w_o', xattn_q_dw='xattn_w_q',
              xattn_kv_dw='xattn_w_kv', mix_out_dw='w_out', mla_out_dw='mla_w_o', q_b_dw='mla_w_q_b',
              kv_b_dw='mla_w_kv_b', conv_out_dw='conv_w_out', ssd_out_dw='ssd_w_out', in_dw='w_in')


def _layer_bwd(dx3, mem, cs, sn, p, sv, l, stacks, depth):
    n = lambda s: f"l{l}_b_{s}"
    dw = lambda s: dict(name=n(s), out_dtype=BF16, into=(stacks.get(DW_KEY[s]), l, depth))
    g = {}
    d = dx3.shape[1]
    dact = _matmul(dx3, p['ffn_w_out'], tb=True, name=n("ffn_out_dx"))
    g['ffn_w_out'] = _matmul(sv['act'], dx3, ta=True, **dw("ffn_out_dw"))
    dh1 = _swiglu_bwd(sv['h1'], dact, name=n("swiglu"))
    g['ffn_w_in'] = _matmul(sv['hf'], dh1, ta=True, **dw("ffn_in_dw"))
    dhf = _matmul(dh1, p['ffn_w_in'], tb=True, name=n("ffn_in_dx"))
    dx2, g['ffn_norm_g'] = _rms_bwd(sv['x2'], p['ffn_norm_g'], dhf, dx_dtype=F32, add=dx3, name=n("ffn_norm"))
    dox = _matmul(dx2, p['xattn_w_o'], tb=True, name=n("xattn_out_dx"))
    g['xattn_w_o'] = _matmul(sv['ox'], dx2, ta=True, **dw("xattn_out_dw"))
    dqx, dkx, dvx, g['xattn_q_norm_g'] = _xattn_bwd(sv['qx'], sv['kx'], sv['vx'], p['xattn_q_norm_g'], dox,
                                                    name=n("xattn_core"))
    g['xattn_w_q'] = _matmul(sv['hx'], dqx, ta=True, **dw("xattn_q_dw"))
    dhx = _matmul(dqx, p['xattn_w_q'], tb=True, name=n("xattn_q_dx"))
    dx1, g['xattn_norm_g'] = _rms_bwd(sv['x1'], p['xattn_norm_g'], dhx, dx_dtype=F32, add=dx2, name=n("xattn_norm"))
    m_len = mem.shape[0]
    dh = d // X_HEADS
    dkraw, g['xattn_k_norm_g'] = _rms_bwd(sv['kraw'], p['xattn_k_norm_g'], dkx.reshape(m_len * X_HEADS, dh),
                                          dx_dtype=BF16, name=n("xattn_k_norm"))
    dkvx = jnp.concatenate([dkraw.reshape(m_len, d), dvx.astype(BF16)], axis=1)
    g['xattn_w_kv'] = _matmul(sv['memn'], dkvx, ta=True, **dw("xattn_kv_dw"))
    dmemn = _matmul(dkvx, p['xattn_w_kv'], tb=True, name=n("xattn_kv_dx"))
    _, g['mem_norm_g'] = _rms_bwd(mem, p['mem_norm_g'], dmemn, dx_dtype=BF16, name=n("mem_norm"))
    dmerged = _matmul(dx1, p['w_out'], tb=True, name=n("mix_out_dx"))
    g['w_out'] = _matmul(sv['merged'], dx1, ta=True, **dw("mix_out_dw"))
    dgl, g['gate_b'], dys, dyc, dym = _merge_bwd(sv['gl'], p['gate_b'], sv['y_ssd'], sv['y_conv'], sv['y_mla'],
                                                 dmerged, name=n("merge"))
    do = _matmul(dym, p['mla_w_o'], tb=True, name=n("mla_out_dx"))
    g['mla_w_o'] = _matmul(sv['o'], dym, ta=True, **dw("mla_out_dw"))
    dqf, dkf, dv = _mla_attn_bwd(sv['qf'], sv['kf'], sv['v'], sv['o'], sv['lse'], do, name=n("mla_attn"))
    dq, dkv, dkrr, g['gq'], g['gk'] = _mla_prep_bwd(sv['q'], sv['kv'], sv['krr'], cs, sn, p['gq'], p['gk'],
                                                    dqf, dkf, dv, name=n("mla_prep"))
    g['mla_w_q_b'] = _matmul(sv['qln'], dq, ta=True, **dw("q_b_dw"))
    dqln = _matmul(dq, p['mla_w_q_b'], tb=True, name=n("q_b_dx"))
    dql, g['mla_q_a_g'] = _rms_bwd(sv['ql'], p['mla_q_a_g'], dqln, dx_dtype=BF16, name=n("q_a_norm"))
    g['mla_w_kv_b'] = _matmul(sv['ckvn'], dkv, ta=True, **dw("kv_b_dw"))
    dckvn = _matmul(dkv, p['mla_w_kv_b'], tb=True, name=n("kv_b_dx"))
    dckv, g['mla_kv_a_g'] = _rms_bwd(sv['ckv'], p['mla_kv_a_g'], dckvn, dx_dtype=BF16, name=n("kv_a_norm"))
    dvc = _matmul(dyc, p['conv_w_out'], tb=True, name=n("conv_out_dx"))
    g['conv_w_out'] = _matmul(sv['vc'], dyc, ta=True, **dw("conv_out_dw"))
    dpre_c, g['conv_ln_g'], g['conv_ln_b'] = _ln_silu_bwd(sv['pre_c'], p['conv_ln_g'], p['conv_ln_b'], dvc,
                                                          name=n("conv_ln"))
    da, dg, g['conv_dw_w'], g['conv_dw_b'] = _conv_bwd(sv['glu'], p['conv_dw_w'], p['conv_dw_b'], dpre_c, None,
                                                       glu=True, act=False, name=n("dw_conv"))
    dyn = _matmul(dys, p['ssd_w_out'], tb=True, name=n("ssd_out_dx"))
    g['ssd_w_out'] = _matmul(sv['yn'], dys, ta=True, **dw("ssd_out_dw"))
    dy_scan, dz, g['ssd_norm_g'] = _ssd_gate_bwd(sv['y_scan'], sv['z'], p['ssd_norm_g'], dyn, name=n("ssd_gate"))
    dxs, db, dc, ddtr, g['ssd_a_log'], g['ssd_dt_bias'], g['ssd_d'] = _ssd_scan_bwd(
        sv['act_s'], sv['dtr'], sv['dtrt'], p['ssd_dt_bias'], p['ssd_dt_bias_t'], p['ssd_a_log'], p['ssd_a_log_t'],
        p['ssd_d'], sv['prev'], dy_scan, name=n("ssd_scan"))
    dact_s = jnp.concatenate([dxs, db, dc], axis=1)
    dxbc, g['ssd_conv_w'], g['ssd_conv_b'] = _conv_bwd(sv['xbc'], p['ssd_conv_w'], p['ssd_conv_b'], dact_s,
                                                       sv['pre_s'], glu=False, act=True, name=n("ssd_conv"))
    tail = jnp.zeros((dz.shape[0], IN_WIDTH_PAD - IN_SPLIT['gate'][1]), BF16)
    dproj = jnp.concatenate([dz, dxbc, ddtr, da, dg, dql, dckv, dkrr, dgl, tail], axis=1)
    g['w_in'] = _matmul(sv['u'], dproj, ta=True, **dw("in_dw"))
    du = _matmul(dproj, p['w_in'], tb=True, name=n("in_dx"))
    dx, g['mix_norm_g'] = _rms_bwd(sv['x'], p['mix_norm_g'], du, dx_dtype=F32, add=dx1, name=n("mix_norm"))
    return dx, g


REP_NAMES = ('mix_norm_g', 'ssd_conv_b', 'ssd_dt_bias', 'ssd_a_log', 'ssd_d', 'ssd_norm_g', 'conv_dw_b', 'conv_ln_g',
             'conv_ln_b', 'mla_q_a_g', 'mla_kv_a_g', 'mla_q_norm_g', 'mla_k_norm_g', 'xattn_norm_g', 'mem_norm_g',
             'xattn_q_norm_g', 'xattn_k_norm_g', 'ffn_norm_g')
BIG = (('w_in', 1, 1024, 8912), ('mla_w_q_b', 1, 384, 1536), ('mla_w_kv_b', 1, 256, 2048),
       ('xattn_w_kv', 1, 1024, 2048), ('ffn_w_in', 1, 1024, 5632), ('ssd_w_out', 0, 1024, 1024),
       ('conv_w_out', 0, 1024, 1024), ('mla_w_o', 0, 1024, 1024), ('w_out', 0, 1024, 1024),
       ('xattn_w_q', 0, 1024, 1024), ('xattn_w_o', 0, 1024, 1024), ('ffn_w_out', 0, 2816, 1024))
SMALL = (('ssd_conv_w', 1, 4, 2048), ('conv_dw_w', 1, 31, 1024), ('gate_b', 1, 3, 1024))


def _rope_tables(positions):
    half = MLA_ROPE // 2
    inv = ROPE_THETA ** (-jnp.arange(0, MLA_ROPE, 2, dtype=F32) / MLA_ROPE)
    ang = positions.astype(F32)[:, None] * inv
    cos, sin = jnp.cos(ang), jnp.sin(ang)
    z = jnp.zeros((positions.shape[0], LANE - 2 * half), F32)
    return jnp.concatenate([cos, cos, z], axis=1), jnp.concatenate([-sin, sin, z], axis=1)


def _local_step(x, mem, positions, target, layer_weights, rep, on_layer=None):
    depth = rep['mix_norm_g'].shape[0]
    cs, sn = _rope_tables(positions)
    params, saved = [], []
    h = x
    for l in range(depth):
        p = _layer_params(layer_weights(l, h), rep, l)
        h, sv = _layer_fwd(h, mem, cs, sn, p, l)
        params.append(p)
        saved.append(sv)
    dh, loss = _loss(h, target, name="loss")
    layer_grads = [None] * depth
    stacks = {}
    for l in reversed(range(depth)):
        dh, layer_grads[l] = _layer_bwd(dh, mem, cs, sn, params[l], saved[l], l, stacks, depth)
        stacks = {k: layer_grads[l][k] for k in DW_KEY.values()}
        if on_layer is not None:
            on_layer(l, stacks)
    stack = lambda k: jnp.stack([layer_grads[l][k] for l in range(depth)])
    gfull = {k: stack(k) for k, _, _, _ in SMALL}
    gfull.update(stacks)
    gfull['w_in'] = _w_in_unpad(gfull['w_in'])
    gfull['mla_w_q_b'] = _qb_unpad(gfull['mla_w_q_b'])
    gfull['gate_b'] = gfull['gate_b'].reshape(depth, 3, -1)
    grep = {}
    for k in REP_NAMES:
        if k == 'mla_q_norm_g':
            grep[k] = stack('gq')[:, 0, :MLA_NOPE + MLA_ROPE]
        elif k == 'mla_k_norm_g':
            grep[k] = stack('gk')[:, 0, :MLA_NOPE + MLA_ROPE]
        elif k in ('ssd_dt_bias', 'ssd_a_log', 'ssd_d'):
            grep[k] = stack(k)[:, 0, :SSD_HEADS]
        else:
            grep[k] = stack(k)[:, 0, :]
    return loss, dh, gfull, grep


N_CHIPS = 4
HBM_SPEC = pl.BlockSpec(memory_space=pl.ANY)
ROW, COL, STK, REP = "row", "col", "stk", "rep"


def _kind(axis, cs):
    if axis == 0:
        return ROW
    return COL if cs % LANE == 0 else STK


def _mesh_pos():
    return lax.axis_index("x"), lax.axis_index("y"), lax.axis_index("c")


def _chip_view(ref, kind, j, a, b, layers=None):
    lsel = slice(None) if layers is None else pl.ds(layers[0], layers[1])
    if kind == STK:
        return ref.at[j, lsel]
    if kind == ROW:
        return ref.at[lsel, pl.ds(pl.multiple_of(j * a, 8), a), :]
    if kind == COL:
        return ref.at[lsel, :, pl.ds(pl.multiple_of(j * b, LANE), b)]
    return ref.at[lsel]


def _all_gather(shards, kinds, *, name):
    n = len(shards)
    depth = shards[0].shape[0]
    lh = depth // 2

    def out_shape(w, kind):
        _, a, b = w.shape
        full = {ROW: (depth, N_CHIPS * a, b), COL: (depth, a, N_CHIPS * b), STK: (N_CHIPS, depth, a, b)}[kind]
        return jax.ShapeDtypeStruct(full, w.dtype)

    def body(*refs):
        w_refs, out_refs = refs[:n], refs[n:2 * n]
        send_sems, recv_sems = refs[2 * n:]
        x, y, cc = _mesh_pos()
        me = (x, y, cc)
        sibling = (x, y, 1 - cc)
        chips = [(1 - x, y), (x, 1 - y), (1 - x, 1 - y)]
        slot = lambda chip: 2 * chip[0] + chip[1]

        def part(i, chip, hc):
            _, a, b = w_refs[i].shape
            return _chip_view(out_refs[i], kinds[i], slot(chip), a, b, (hc * lh, lh))

        def own(i):
            _, a, b = w_refs[i].shape
            return _chip_view(out_refs[i], kinds[i], slot((x, y)), a, b)

        def copy(i, k, src, dst, to):
            return pltpu.make_async_remote_copy(src_ref=src, dst_ref=dst, send_sem=send_sems.at[i, k],
                                                recv_sem=recv_sems.at[i, k], device_id=to, device_id_type=MESH)

        sends = []
        for i in range(n):
            my_half = w_refs[i].at[pl.ds(cc * lh, lh)]
            for k, chip in enumerate(chips):
                sends.append(copy(i, k, my_half, part(i, (x, y), cc), (*chip, cc)))
            sends.append(copy(i, 6, w_refs[i], own(i), sibling))
        for cp in sends:
            cp.start()
        for k, chip in enumerate(chips):
            for i in range(n):
                copy(i, k, part(i, chip, cc), part(i, chip, cc), me).wait_recv()
                cp = copy(i, 3 + k, part(i, chip, cc), part(i, chip, cc), sibling)
                cp.start()
                sends.append(cp)
        for i in range(n):
            for k, chip in enumerate(chips):
                copy(i, 3 + k, part(i, chip, 1 - cc), part(i, chip, 1 - cc), me).wait_recv()
            copy(i, 6, own(i), own(i), me).wait_recv()
        for cp in sends:
            cp.wait_send()

    return pl.pallas_call(
        body, name=name, in_specs=[HBM_SPEC] * n, out_specs=[HBM_SPEC] * n,
        out_shape=[out_shape(w, kd) for w, kd in zip(shards, kinds)],
        scratch_shapes=[pltpu.SemaphoreType.DMA((n, 7)), pltpu.SemaphoreType.DMA((n, 7))],
    )(*shards)


SEM_SPEC = pl.BlockSpec(memory_space=pltpu.SEMAPHORE)
SPLIT_COPY = pltpu.CompilerParams(has_side_effects=pltpu.SideEffectType.DATAFLOW_SIDE_EFFECTING)


def _land_view(ref, kind, j, a, b, rows=None):
    r0, nr = (0, a) if rows is None else rows
    if kind == STK:
        return ref.at[j, pl.ds(r0, nr), :]
    if kind == ROW:
        return ref.at[pl.ds(pl.multiple_of(j * a + r0, 8), nr), :]
    return ref.at[pl.ds(r0, nr), pl.ds(pl.multiple_of(j * b, LANE), b)]


def _layer_copies(w_refs, land_refs, kinds, layer, send_sems, recv_sems):
    x, y, cc = _mesh_pos()
    place = 2 * x + y
    chips = [(1 - x, y), (x, 1 - y), (1 - x, 1 - y)]
    out, inc = [], []
    for i, (w_ref, land, kind) in enumerate(zip(w_refs, land_refs, kinds)):
        _, a, b = w_ref.shape
        half = (cc * (a // 2), a // 2)
        desc = lambda k, src, dst, to: pltpu.make_async_remote_copy(
            src_ref=src, dst_ref=dst, send_sem=send_sems.at[4 * i + k], recv_sem=recv_sems.at[4 * i + k], device_id=to,
            device_id_type=MESH)
        for k, chip in enumerate(chips):
            out.append(desc(k, w_ref.at[layer, pl.ds(half[0], half[1]), :], _land_view(land, kind, place, a, b, half),
                            (*chip, cc)))
            theirs = _land_view(land, kind, 2 * chip[0] + chip[1], a, b, half)
            inc.append(desc(k, theirs, theirs, (*chip, cc)))
        out.append(desc(3, w_ref.at[layer], _land_view(land, kind, place, a, b), (x, y, 1 - cc)))
        own = _land_view(land, kind, place, a, b)
        inc.append(desc(3, own, own, (x, y, 1 - cc)))
    return out, inc


def _land_shape(w, kind):
    _, a, b = w.shape
    return {ROW: (N_CHIPS * a, b), COL: (a, N_CHIPS * b), STK: (N_CHIPS, a, b)}[kind]


def _ag_start(shards, kinds, layer, after, *, name):
    n = len(shards)
    lands = [pltpu.with_memory_space_constraint(lax.empty(_land_shape(w, kd), w.dtype), pltpu.HBM)
             for w, kd in zip(shards, kinds)]

    def body(*refs):
        w_refs = refs[:n]
        send_sems, recv_sems = refs[2 * n + 1], refs[2 * n + 2]
        land_refs = refs[2 * n + 3:3 * n + 3]
        token = refs[3 * n + 3]
        out, _ = _layer_copies(w_refs, land_refs, kinds, layer, send_sems, recv_sems)
        for cp in out:
            cp.start()
        token[...] = jnp.zeros_like(token)

    after = jnp.zeros((8, LANE), F32) if after is None else after
    res = pl.pallas_call(
        body, name=name,
        in_specs=[HBM_SPEC] * (2 * n + 1),
        out_specs=[SEM_SPEC, SEM_SPEC] + [HBM_SPEC] * n + [pl.BlockSpec(memory_space=pltpu.VMEM)],
        out_shape=[pltpu.SemaphoreType.DMA((4 * n,)), pltpu.SemaphoreType.DMA((4 * n,))]
        + [jax.ShapeDtypeStruct(ld.shape, ld.dtype) for ld in lands] + [jax.ShapeDtypeStruct((8, LANE), F32)],
        input_output_aliases={n + i: 2 + i for i in range(n)},
        compiler_params=SPLIT_COPY,
    )(*shards, *lands, after)
    return res[0], res[1], list(res[2:2 + n]), res[2 + n]


def _ag_wait(send_sems, recv_sems, shards, lands, kinds, layer, after, *, name):
    n = len(shards)

    def body(*refs):
        w_refs, land_in = refs[:n], refs[n:2 * n]
        send_sems, recv_sems = refs[2 * n], refs[2 * n + 1]
        out, inc = _layer_copies(w_refs, land_in, kinds, layer, send_sems, recv_sems)
        for cp in out:
            cp.wait_send()
        for cp in inc:
            cp.wait_recv()

    return list(pl.pallas_call(
        body, name=name,
        in_specs=[HBM_SPEC] * (2 * n) + [SEM_SPEC, SEM_SPEC, HBM_SPEC],
        out_specs=[HBM_SPEC] * n,
        out_shape=[jax.ShapeDtypeStruct(ld.shape, ld.dtype) for ld in lands],
        input_output_aliases={n + i: i for i in range(n)},
        compiler_params=SPLIT_COPY,
    )(*shards, *lands, send_sems, recv_sems, after))


def _ag_forward(lands, kinds, shard_shapes, *, name):
    n = len(lands)

    def body(*refs):
        land_refs = refs[n:2 * n]
        send_sems, recv_sems = refs[2 * n:]
        x, y, cc = _mesh_pos()
        chips = [(1 - x, y), (x, 1 - y), (1 - x, 1 - y)]
        sends, waits = [], []
        for i, (land, kind) in enumerate(zip(land_refs, kinds)):
            a, b = shard_shapes[i]
            for k, chip in enumerate(chips):
                j = 2 * chip[0] + chip[1]
                desc = lambda view: pltpu.make_async_remote_copy(
                    src_ref=view, dst_ref=view, send_sem=send_sems.at[i, k], recv_sem=recv_sems.at[i, k],
                    device_id=(x, y, 1 - cc), device_id_type=MESH)
                sends.append(desc(_land_view(land, kind, j, a, b, (cc * (a // 2), a // 2))))
                waits.append(desc(_land_view(land, kind, j, a, b, ((1 - cc) * (a // 2), a // 2))))
        for cp in sends:
            cp.start()
        for cp in waits:
            cp.wait_recv()
        for cp in sends:
            cp.wait_send()

    return list(pl.pallas_call(
        body, name=name, in_specs=[HBM_SPEC] * n, out_specs=[HBM_SPEC] * n,
        out_shape=[jax.ShapeDtypeStruct(ld.shape, ld.dtype) for ld in lands],
        input_output_aliases={i: i for i in range(n)},
        scratch_shapes=[pltpu.SemaphoreType.DMA((n, 3)), pltpu.SemaphoreType.DMA((n, 3))],
    )(*lands))


def _layers_half(ref, kind, start, lh):
    return ref.at[:, pl.ds(start, lh)] if kind == STK else ref.at[pl.ds(start, lh)]


def _rs_pair(gs, kinds, *, name):
    n = len(gs)

    def half_shape(g, kind):
        s = list(g.shape)
        s[1 if kind == STK else 0] //= 2
        return jax.ShapeDtypeStruct(tuple(s), g.dtype)

    def body(*refs):
        g_refs, buf_refs = refs[:n], refs[n:2 * n]
        send_sems, recv_sems = refs[2 * n:]
        x, y, cc = _mesh_pos()
        cps = []
        for i in range(n):
            lh = buf_refs[i].shape[1 if kinds[i] == STK else 0]
            cp = pltpu.make_async_remote_copy(src_ref=_layers_half(g_refs[i], kinds[i], (1 - cc) * lh, lh),
                                              dst_ref=buf_refs[i], send_sem=send_sems.at[i], recv_sem=recv_sems.at[i],
                                              device_id=(x, y, 1 - cc), device_id_type=MESH)
            cp.start()
            cps.append(cp)
        for cp in cps:
            cp.wait()

    return pl.pallas_call(
        body, name=name, in_specs=[HBM_SPEC] * n, out_specs=[HBM_SPEC] * n,
        out_shape=[half_shape(g, kd) for g, kd in zip(gs, kinds)],
        scratch_shapes=[pltpu.SemaphoreType.DMA((n,)), pltpu.SemaphoreType.DMA((n,))],
    )(*gs)


def _row_tile(rows, cols):
    return _pick(rows, max(8, (512 * 1024 // cols) // 8 * 8), 8)


def _rs_pair_add(g, buf, kind, cc, out_dtype, *, name):
    cols = g.shape[-1]
    pre = g.shape[0] if kind == STK else 1
    rows = buf.size // (pre * cols)
    tr = _row_tile(rows, cols)

    def body(cc_ref, g_ref, b_ref, o_ref):
        o_ref[...] = (g_ref[...].astype(F32) + b_ref[...].astype(F32)).astype(o_ref.dtype)

    out = pl.pallas_call(
        body, name=name,
        grid_spec=pltpu.PrefetchScalarGridSpec(
            num_scalar_prefetch=1, grid=(pre, rows // tr),
            in_specs=[pl.BlockSpec((None, None, tr, cols), lambda s, i, cc_ref: (s, cc_ref[0], i, 0)),
                      pl.BlockSpec((None, tr, cols), lambda s, i, cc_ref: (s, i, 0))],
            out_specs=pl.BlockSpec((None, tr, cols), lambda s, i, cc_ref: (s, i, 0))),
        out_shape=jax.ShapeDtypeStruct((pre, rows, cols), out_dtype),
        compiler_params=_cparams(("parallel", "parallel")),
    )(cc.reshape(1).astype(jnp.int32), g.reshape(pre, 2, rows, cols), buf.reshape(pre, rows, cols))
    return out.reshape(buf.shape)


def _rs_cross(ps, kinds, shard_shapes, *, name):
    n = len(ps)

    def body(*refs):
        p_refs, out_refs = refs[:n], refs[n:2 * n]
        send_sems, recv_sems, local_sems = refs[2 * n:]
        x, y, cc = _mesh_pos()
        chips = [(1 - x, y), (x, 1 - y), (1 - x, 1 - y)]
        slot = lambda chip: 2 * chip[0] + chip[1]
        local, sends = [], []
        for i in range(n):
            a, b = shard_shapes[i]
            if kinds[i] == REP:
                cp = pltpu.make_async_copy(p_refs[i], out_refs[i].at[slot((x, y))], local_sems.at[i])
                cp.start()
                local.append(cp)
            for k, chip in enumerate(chips):
                cp = pltpu.make_async_remote_copy(src_ref=_chip_view(p_refs[i], kinds[i], slot(chip), a, b),
                                                  dst_ref=out_refs[i].at[slot((x, y))], send_sem=send_sems.at[i, k],
                                                  recv_sem=recv_sems.at[i, k], device_id=(*chip, cc),
                                                  device_id_type=MESH)
                cp.start()
                sends.append(cp)
        for i in range(n):
            for k, chip in enumerate(chips):
                landed = out_refs[i].at[slot(chip)]
                pltpu.make_async_remote_copy(src_ref=landed, dst_ref=landed, send_sem=send_sems.at[i, k],
                                             recv_sem=recv_sems.at[i, k], device_id=(*chip, cc),
                                             device_id_type=MESH).wait_recv()
        for cp in sends:
            cp.wait_send()
        for cp in local:
            cp.wait()

    def out_shape(p, kind, ab):
        lh = p.shape[1 if kind == STK else 0]
        return jax.ShapeDtypeStruct((N_CHIPS, lh) + tuple(ab), p.dtype)

    return pl.pallas_call(
        body, name=name, in_specs=[HBM_SPEC] * n, out_specs=[HBM_SPEC] * n,
        out_shape=[out_shape(p, kd, ab) for p, kd, ab in zip(ps, kinds, shard_shapes)],
        scratch_shapes=[pltpu.SemaphoreType.DMA((n, 3)), pltpu.SemaphoreType.DMA((n, 3)),
                        pltpu.SemaphoreType.DMA((n,))],
    )(*ps)


def _cross_copies(p_refs, land_refs, kinds, shard_shapes, send_sems, recv_sems):
    x, y, cc = _mesh_pos()
    chips = [(1 - x, y), (x, 1 - y), (1 - x, 1 - y)]
    mine = 2 * x + y
    out, inc = [], []
    for i, (p_ref, land, kind) in enumerate(zip(p_refs, land_refs, kinds)):
        a, b = shard_shapes[i]
        for k, chip in enumerate(chips):
            j = 2 * chip[0] + chip[1]
            desc = lambda src, dst: pltpu.make_async_remote_copy(
                src_ref=src, dst_ref=dst, send_sem=send_sems.at[3 * i + k], recv_sem=recv_sems.at[3 * i + k],
                device_id=(*chip, cc), device_id_type=MESH)
            out.append(desc(_chip_view(p_ref, kind, j, a, b), land.at[mine]))
            inc.append(desc(land.at[j], land.at[j]))
    return out, inc


def _rs_cross_start(ps, kinds, shard_shapes, *, name):
    n = len(ps)
    lands = []
    for p, kd, ab in zip(ps, kinds, shard_shapes):
        lh = p.shape[1 if kd == STK else 0]
        lands.append(pltpu.with_memory_space_constraint(lax.empty((N_CHIPS, lh) + tuple(ab), p.dtype), pltpu.HBM))

    def body(*refs):
        p_refs = refs[:n]
        send_sems, recv_sems = refs[2 * n], refs[2 * n + 1]
        land_refs = refs[2 * n + 2:3 * n + 2]
        out, _ = _cross_copies(p_refs, land_refs, kinds, shard_shapes, send_sems, recv_sems)
        for cp in out:
            cp.start()

    res = pl.pallas_call(
        body, name=name, in_specs=[HBM_SPEC] * (2 * n), out_specs=[SEM_SPEC, SEM_SPEC] + [HBM_SPEC] * n,
        out_shape=[pltpu.SemaphoreType.DMA((3 * n,)), pltpu.SemaphoreType.DMA((3 * n,))]
        + [jax.ShapeDtypeStruct(ld.shape, ld.dtype) for ld in lands],
        input_output_aliases={n + i: 2 + i for i in range(n)}, compiler_params=SPLIT_COPY,
    )(*ps, *lands)
    return res[0], res[1], list(res[2:])


def _rs_cross_wait(send_sems, recv_sems, ps, lands, kinds, shard_shapes, after, *, name):
    n = len(ps)

    def body(*refs):
        p_refs, land_in = refs[:n], refs[n:2 * n]
        out, inc = _cross_copies(p_refs, land_in, kinds, shard_shapes, refs[2 * n], refs[2 * n + 1])
        for cp in out:
            cp.wait_send()
        for cp in inc:
            cp.wait_recv()

    return list(pl.pallas_call(
        body, name=name, in_specs=[HBM_SPEC] * (2 * n) + [SEM_SPEC, SEM_SPEC, HBM_SPEC], out_specs=[HBM_SPEC] * n,
        out_shape=[jax.ShapeDtypeStruct(ld.shape, ld.dtype) for ld in lands],
        input_output_aliases={n + i: i for i in range(n)}, compiler_params=SPLIT_COPY,
    )(*ps, *lands, send_sems, recv_sems, after))


def _rs_sum(p, landed, kind, ab, place, cc, *, name):
    a, b = ab
    lh = landed.shape[1]
    tr = _row_tile(a, b)
    blk = lambda f: pl.BlockSpec((None, None, tr, b), f)
    if kind == ROW:
        p_spec = pl.BlockSpec((None, tr, b), lambda l, i, w, c: (l, w[0] * (a // tr) + i, 0))
    elif kind == COL:
        p_spec = pl.BlockSpec((None, tr, b), lambda l, i, w, c: (l, i, w[0]))
    elif kind == STK:
        p_spec = blk(lambda l, i, w, c: (w[0], l, i, 0))
    else:
        p_spec = blk(lambda l, i, w, c: (0, l, i, 0))
        p = landed
    if kind == REP:
        others = [blk(lambda l, i, w, c, k=k: (k, l, i, 0)) for k in (1, 2, 3)]
    else:
        others = [blk(lambda l, i, w, c: (lax.rem(w[0] + 2, 4), l, i, 0)),
                  blk(lambda l, i, w, c: (w[0] + 1 - 2 * lax.rem(w[0], 2), l, i, 0)),
                  blk(lambda l, i, w, c: (3 - w[0], l, i, 0))]

    def body(w_ref, c_ref, p_ref, b1_ref, b2_ref, b3_ref, o_ref):
        f = lambda r: r[...].astype(F32)
        o_ref[...] = ((f(p_ref) + f(b1_ref)) + f(b2_ref)) + f(b3_ref)

    return pl.pallas_call(
        body, name=name,
        grid_spec=pltpu.PrefetchScalarGridSpec(
            num_scalar_prefetch=2, grid=(lh, a // tr), in_specs=[p_spec] + others,
            out_specs=pl.BlockSpec((None, tr, b), lambda l, i, w, c: (c[0] * lh + l, i, 0))),
        out_shape=jax.ShapeDtypeStruct((2 * lh, a, b), F32),
        compiler_params=_cparams(("parallel", "parallel")),
    )(place, cc, p, landed, landed, landed)


def _rs_share(fs, *, name):
    n = len(fs)

    def body(*refs):
        out_refs = refs[n:2 * n]
        send_sems, recv_sems = refs[2 * n:]
        x, y, cc = _mesh_pos()
        cps = []
        for i in range(n):
            lh = out_refs[i].shape[0] // 2
            mine = out_refs[i].at[pl.ds(cc * lh, lh)]
            cp = pltpu.make_async_remote_copy(src_ref=mine, dst_ref=mine, send_sem=send_sems.at[i],
                                              recv_sem=recv_sems.at[i], device_id=(x, y, 1 - cc),
                                              device_id_type=MESH)
            cp.start()
            cps.append(cp)
        for i in range(n):
            lh = out_refs[i].shape[0] // 2
            theirs = out_refs[i].at[pl.ds((1 - cc) * lh, lh)]
            pltpu.make_async_remote_copy(src_ref=theirs, dst_ref=theirs, send_sem=send_sems.at[i],
                                         recv_sem=recv_sems.at[i], device_id=(x, y, 1 - cc),
                                         device_id_type=MESH).wait_recv()
        for cp in cps:
            cp.wait_send()

    return pl.pallas_call(
        body, name=name, in_specs=[HBM_SPEC] * n, out_specs=[HBM_SPEC] * n,
        out_shape=[jax.ShapeDtypeStruct(f.shape, f.dtype) for f in fs],
        input_output_aliases={i: i for i in range(n)},
        scratch_shapes=[pltpu.SemaphoreType.DMA((n,)), pltpu.SemaphoreType.DMA((n,))],
    )(*fs)


def _reduce_scatter(gs, kinds, shard_shapes, wire_dtypes):
    ps = _rs_pair_sums(gs, kinds, wire_dtypes, "")
    landed = _rs_cross(ps, kinds, shard_shapes, name="rs_cross")
    return _rs_finish(ps, landed, kinds, shard_shapes, "")


def _rs_pair_sums(gs, kinds, wire_dtypes, tag):
    cc = lax.axis_index("c")
    bufs = _rs_pair(gs, kinds, name="rs_pair" + tag)
    return [_rs_pair_add(g, buf, kd, cc, wd, name=f"rs_pair_add{tag}_{i}")
            for i, (g, buf, kd, wd) in enumerate(zip(gs, bufs, kinds, wire_dtypes))]


def _rs_finish(ps, landed, kinds, shard_shapes, tag):
    x, y, cc = _mesh_pos()
    place = (2 * x + y).reshape(1).astype(jnp.int32)
    cc1 = cc.reshape(1).astype(jnp.int32)
    fs = [_rs_sum(p, b, kd, ab, place, cc1, name=f"rs_sum{tag}_{i}")
          for i, (p, b, kd, ab) in enumerate(zip(ps, landed, kinds, shard_shapes))]
    return _rs_share(fs, name="rs_share" + tag)


def _reduce_scatter_begin(gs, kinds, shard_shapes, wire_dtypes):
    ps = _rs_pair_sums(gs, kinds, wire_dtypes, "_early")
    send, recv, lands = _rs_cross_start(ps, kinds, shard_shapes, name="rs_cross_start")
    return ps, send, recv, lands


def _reduce_scatter_end(state, kinds, shard_shapes, after):
    ps, send, recv, lands = state
    landed = _rs_cross_wait(send, recv, ps, lands, kinds, shard_shapes, after, name="rs_cross_wait")
    return _rs_finish(ps, landed, kinds, shard_shapes, "_early")


def _shard_shape(axis, r, c):
    return (r // N_CHIPS, c) if axis == 0 else (r, c // N_CHIPS)


def _unstack(stacked):
    ns, depth, r, cs = stacked.shape
    return stacked.transpose(1, 2, 0, 3).reshape(depth, r, ns * cs)


def _stack(fullw):
    depth, r, c = fullw.shape
    return fullw.reshape(depth, r, N_CHIPS, c // N_CHIPS).transpose(2, 0, 1, 3)


REP_SIZES = dict(mix_norm_g=1024, ssd_conv_b=2048, ssd_dt_bias=16, ssd_a_log=16, ssd_d=16, ssd_norm_g=1024,
                 conv_dw_b=1024, conv_ln_g=1024, conv_ln_b=1024, mla_q_a_g=384, mla_kv_a_g=256, mla_q_norm_g=192,
                 mla_k_norm_g=192, xattn_norm_g=1024, mem_norm_g=1024, xattn_q_norm_g=256, xattn_k_norm_g=256,
                 ffn_norm_g=1024)
REP_ROWS = 8
REP_COLS = -(-sum(REP_SIZES.values()) // (REP_ROWS * LANE)) * LANE
SMALL_COLS = 256
SMALL_ROWS = 48


def _pack_rep(d):
    flat = jnp.concatenate([d[k] for k in REP_NAMES], axis=1)
    flat = jnp.pad(flat, ((0, 0), (0, REP_ROWS * REP_COLS - flat.shape[1])))
    return flat.reshape(flat.shape[0], REP_ROWS, REP_COLS)


def _unpack_rep(packed):
    out, off = {}, 0
    flat = packed.reshape(packed.shape[0], -1)
    for k in REP_NAMES:
        out[k] = flat[:, off:off + REP_SIZES[k]]
        off += REP_SIZES[k]
    return out


def _pack_small(d):
    depth = d['gate_b'].shape[0]
    rows = jnp.concatenate([d[k].reshape(depth, -1, SMALL_COLS) for k, _, _, _ in SMALL], axis=1)
    return jnp.pad(rows, ((0, 0), (0, SMALL_ROWS - rows.shape[1]), (0, 0)))


def _unpack_small(packed):
    out, off = {}, 0
    for k, axis, r, c in SMALL:
        rs, cs = _shard_shape(axis, r, c)
        n = rs * cs // SMALL_COLS
        out[k] = packed[:, off:off + n].reshape(packed.shape[0], rs, cs)
        off += n
    return out


WEIGHT_NAMES = ('mix_norm_g', 'w_in', 'ssd_conv_w', 'ssd_conv_b', 'ssd_dt_bias', 'ssd_a_log', 'ssd_d', 'ssd_norm_g',
                'ssd_w_out', 'conv_dw_w', 'conv_dw_b', 'conv_ln_g', 'conv_ln_b', 'conv_w_out', 'mla_q_a_g',
                'mla_w_q_b', 'mla_kv_a_g', 'mla_w_kv_b', 'mla_q_norm_g', 'mla_k_norm_g', 'mla_w_o', 'gate_b', 'w_out',
                'xattn_norm_g', 'mem_norm_g', 'xattn_w_q', 'xattn_w_kv', 'xattn_q_norm_g', 'xattn_k_norm_g',
                'xattn_w_o', 'ffn_norm_g', 'ffn_w_in', 'ffn_w_out')


def kernel(x, mem, positions, *rest):
    nw = len(WEIGHT_NAMES)
    weights = dict(zip(WEIGHT_NAMES, rest[:nw]))
    target = rest[nw]
    mom_m = dict(zip(WEIGHT_NAMES, rest[nw + 1:2 * nw + 1]))
    mom_v = dict(zip(WEIGHT_NAMES, rest[2 * nw + 1:3 * nw + 1]))
    depth = weights['mix_norm_g'].shape[0]

    kinds = [_kind(axis, _shard_shape(axis, r, c)[1]) for _, axis, r, c in BIG]
    shard_shapes = [_shard_shape(axis, r, c) for _, axis, r, c in BIG]
    nb = len(BIG)
    small_all, = _all_gather([_pack_small(weights)], [STK], name="ag_small")
    small_chips = [_unpack_small(small_all[j]) for j in range(N_CHIPS)]
    small = {k: jnp.concatenate([sc[k] for sc in small_chips], axis=2) for k, _, _, _ in SMALL}
    big_shards = [weights[k].astype(BF16) for k, _, _, _ in BIG]
    send0, recv0, lands0, _ = _ag_start(big_shards, kinds[:nb], 0, None, name="ag_start_0")
    lands0 = _ag_wait(send0, recv0, big_shards, lands0, kinds[:nb], 0, big_shards[0], name="ag_wait_0")
    lands0 = _ag_forward(lands0, kinds[:nb], shard_shapes[:nb], name="ag_forward_0")
    started = {l: _ag_start(big_shards, kinds[:nb], l, lands0[0], name=f"ag_start_{l}") for l in range(1, depth)}
    tokens = sum(st[3][0, 0] for st in started.values())

    def layer_weights(l, h):
        if l == 0:
            lands = lands0
        else:
            send, recv, lands, _ = started[l]
            lands = _ag_wait(send, recv, big_shards, lands, kinds[:nb], l, h, name=f"ag_wait_{l}")
            lands = _ag_forward(lands, kinds[:nb], shard_shapes[:nb], name=f"ag_forward_{l}")
        full = {k: (ld.transpose(1, 0, 2).reshape(ld.shape[1], -1) if kd == STK else ld)
                for (k, _, _, _), kd, ld in zip(BIG, kinds[:nb], lands)}
        full.update({k: w[l] for k, w in small.items()})
        full['w_in'] = _w_in_pad(full['w_in'])
        full['mla_w_q_b'] = _qb_pad(full['mla_w_q_b'])
        return full

    rep = {k: weights[k] for k in REP_NAMES}
    top = depth // 2
    by_chip = lambda g: [(_stack(g[k]) if kd == STK else g[k]) for (k, _, _, _), kd in zip(BIG, kinds)]
    early = []

    def on_layer(l, stacks):
        if l == top:
            g = {k: stacks[k][top:] for k, _, _, _ in BIG}
            g['w_in'] = _w_in_unpad(g['w_in'])
            g['mla_w_q_b'] = _qb_unpad(g['mla_w_q_b'])
            early.append(_reduce_scatter_begin(by_chip(g), kinds, shard_shapes, [BF16] * nb))

    loss, dx, gfull, grep = _local_step(x[0] + tokens, mem[0], positions[0], target[0], layer_weights, rep, on_layer)
    loss = lax.psum(loss, ("x", "y", "c"))
    upper = _reduce_scatter_end(early[0], kinds, shard_shapes, dx)

    gs = by_chip({k: gfull[k][:top] for k, _, _, _ in BIG})
    small_split = {k: _stack(gfull[k]) for k, _, _, _ in SMALL}
    gs.append(jnp.stack([_pack_small({k: v[j] for k, v in small_split.items()}) for j in range(N_CHIPS)]))
    gs.append(_pack_rep(grep))
    summed = _reduce_scatter(gs, kinds + [STK, REP], shard_shapes + [(SMALL_ROWS, SMALL_COLS), (REP_ROWS, REP_COLS)],
                             [BF16] * nb + [F32, F32])
    grads = {k: jnp.concatenate([lo, hi]) for (k, _, _, _), lo, hi in zip(BIG, summed[:nb], upper)}
    grads.update(_unpack_small(summed[nb]))
    rep_sum = summed[nb + 1]
    grads.update(_unpack_rep(rep_sum))

    delta, new_m, new_v = {}, {}, {}
    for k, _, _, _ in BIG + SMALL:
        w = weights[k]
        two_d = (w.shape[0] * w.shape[1], w.shape[2])
        d_, m_, v_ = _adamw(w.reshape(two_d), grads[k].reshape(two_d), mom_m[k].reshape(two_d),
                            mom_v[k].reshape(two_d), name="adamw_" + k)
        delta[k], new_m[k], new_v[k] = d_.reshape(w.shape), m_.reshape(w.shape), v_.reshape(w.shape)
    pack2 = lambda d: _pack_rep(d).reshape(depth, -1)
    d_, m_, v_ = _adamw(pack2(rep), rep_sum.reshape(depth, -1), pack2({k: mom_m[k] for k in REP_NAMES}),
                        pack2({k: mom_v[k] for k in REP_NAMES}), name="adamw_rep")
    delta.update(_unpack_rep(d_))
    new_m.update(_unpack_rep(m_))
    new_v.update(_unpack_rep(v_))

    return (loss, dx[None], *[grads[k] for k in WEIGHT_NAMES], *[delta[k] for k in WEIGHT_NAMES],
            *[new_m[k] for k in WEIGHT_NAMES], *[new_v[k] for k in WEIGHT_NAMES])
```

```python
import functools
import math

import jax
import jax.numpy as jnp
import numpy as np
from jax import lax
from jax.experimental import pallas as pl
from jax.experimental.pallas import tpu as pltpu

F32 = jnp.float32
BF16 = jnp.bfloat16
MESH = pl.DeviceIdType.MESH

EPS = 1e-6
CHUNK = 64
SSD_HEADS = 16
SSD_GROUPS = 4
SSD_P = 64
SSD_N = 128
MLA_HEADS = 8
MLA_NOPE = 128
MLA_ROPE = 64
MLA_V = 128
MLA_HP = 256
X_HEADS = 4
ROPE_THETA = 10000.0
ADAM_LR, ADAM_B1, ADAM_B2, ADAM_EPS, ADAM_WD, ADAM_STEP = 0.001, 0.9, 0.999, 1e-08, 0.01, 10
LANE = 128
NEG = -1e30
VMEM_MB = 1024 * 1024


def _pick(n, cap, mult=128):
    if n <= cap:
        return n
    d = (cap // mult) * mult
    while d >= mult:
        if n % d == 0:
            return d
        d -= mult
    return n


def _cparams(sem, mb=40):
    return pltpu.CompilerParams(dimension_semantics=sem, vmem_limit_bytes=mb * VMEM_MB)


def _sigmoid(x):
    return 1.0 / (1.0 + jnp.exp(-x))


def _silu(x):
    return x * _sigmoid(x)


def _dsilu(x):
    s = _sigmoid(x)
    return s * (1.0 + x * (1.0 - s))


def _softplus(x):
    return jnp.maximum(x, 0.0) + jnp.log(1.0 + jnp.exp(-jnp.abs(x)))


def _matmul(a, b, *, ta=False, tb=False, out_dtype=F32, add=None, into=None, name):
    if ta:
        kdim, m = a.shape
    else:
        m, kdim = a.shape
    if tb:
        n, kb = b.shape
    else:
        kb, n = b.shape
    assert kb == kdim, (a.shape, b.shape, ta, tb)
    tm = _pick(m, 512, 128) if ta else _pick(m, 1024, 8)
    tn = _pick(n, 1024 if n <= 1024 else 512, 128)
    tk = _pick(kdim, 2048, 128)
    nk = kdim // tk
    dims = (((0 if ta else 1,), (1 if tb else 0,)), ((), ()))

    has_add = add is not None
    has_stack = into is not None and into[0] is not None

    def body(a_ref, b_ref, *rest):
        add_ref = rest[0] if has_add else None
        o_ref = rest[has_add + has_stack]
        acc = rest[has_add + has_stack + 1:]
        part = lax.dot_general(a_ref[...].astype(BF16), b_ref[...].astype(BF16), dims,
                               preferred_element_type=F32)

        def finish(total):
            if has_add:
                total = total + add_ref[...]
            o_ref[...] = total.astype(o_ref.dtype)

        if nk == 1:
            finish(part)
        else:
            acc_ref, = acc
            k = pl.program_id(2)

            @pl.when(k == 0)
            def _():
                acc_ref[...] = part

            @pl.when(k > 0)
            def _():
                acc_ref[...] += part

            @pl.when(k == nk - 1)
            def _():
                finish(acc_ref[...])

    a_spec = pl.BlockSpec((tk, tm), lambda i, j, k: (k, i)) if ta else pl.BlockSpec((tm, tk), lambda i, j, k: (i, k))
    b_spec = pl.BlockSpec((tn, tk), lambda i, j, k: (j, k)) if tb else pl.BlockSpec((tk, tn), lambda i, j, k: (k, j))
    o_spec = pl.BlockSpec((tm, tn), lambda i, j, k: (i, j))
    operands = [a, b] + ([add] if has_add else [])
    in_specs = [a_spec, b_spec] + ([o_spec] if has_add else [])
    if into is None:
        out_spec, out_shape, aliases = o_spec, jax.ShapeDtypeStruct((m, n), out_dtype), {}
    else:
        stack, layer, depth = into
        out_spec = pl.BlockSpec((None, tm, tn), lambda i, j, k: (layer, i, j))
        out_shape = jax.ShapeDtypeStruct((depth, m, n), out_dtype)
        aliases = {}
        if stack is not None:
            aliases = {len(operands): 0}
            operands.append(stack)
            in_specs.append(HBM_SPEC)
    return pl.pallas_call(
        body, name=name, grid=(m // tm, n // tn, nk),
        in_specs=in_specs, out_specs=out_spec, out_shape=out_shape, input_output_aliases=aliases,
        scratch_shapes=[] if nk == 1 else [pltpu.VMEM((tm, tn), F32)],
        compiler_params=_cparams(("parallel", "parallel", "arbitrary"), 48),
    )(*operands)


def _rms_fwd(x, g, *, out_dtype, name):
    r, w = x.shape
    tr = _pick(r, 512, 8)

    def body(x_ref, g_ref, o_ref):
        xv = x_ref[...]
        rstd = lax.rsqrt(jnp.mean(xv * xv, axis=-1, keepdims=True) + EPS)
        o_ref[...] = (xv * rstd * g_ref[...]).astype(o_ref.dtype)

    return pl.pallas_call(
        body, name=name, grid=(r // tr,),
        in_specs=[pl.BlockSpec((tr, w), lambda i: (i, 0)), pl.BlockSpec((1, w), lambda i: (0, 0))],
        out_specs=pl.BlockSpec((tr, w), lambda i: (i, 0)),
        out_shape=jax.ShapeDtypeStruct((r, w), out_dtype),
        compiler_params=_cparams(("parallel",)),
    )(x, g)


def _rms_bwd(x, g, dy, *, dx_dtype, name, add=None):
    r, w = x.shape
    tr = _pick(r, 512, 8)
    has_add = add is not None

    def body(x_ref, g_ref, dy_ref, *rest):
        if has_add:
            add_ref, dx_ref, dg_ref = rest
        else:
            dx_ref, dg_ref = rest
        xv = x_ref[...]
        dyv = dy_ref[...].astype(F32)
        rstd = lax.rsqrt(jnp.mean(xv * xv, axis=-1, keepdims=True) + EPS)
        xh = xv * rstd
        dyg = dyv * g_ref[...]
        dx = rstd * (dyg - xh * jnp.mean(dyg * xh, axis=-1, keepdims=True))
        if has_add:
            dx = dx + add_ref[...]
        dx_ref[...] = dx.astype(dx_ref.dtype)
        part = jnp.sum(dyv * xh, axis=0, keepdims=True)

        @pl.when(pl.program_id(0) == 0)
        def _():
            dg_ref[...] = part

        @pl.when(pl.program_id(0) > 0)
        def _():
            dg_ref[...] += part

    row = pl.BlockSpec((tr, w), lambda i: (i, 0))
    vec = pl.BlockSpec((1, w), lambda i: (0, 0))
    ins = [x, g, dy] + ([add] if has_add else [])
    return pl.pallas_call(
        body, name=name, grid=(r // tr,),
        in_specs=[row, vec, row] + ([row] if has_add else []),
        out_specs=[row, vec],
        out_shape=[jax.ShapeDtypeStruct((r, w), dx_dtype), jax.ShapeDtypeStruct((1, w), F32)],
        compiler_params=_cparams(("arbitrary",)),
    )(*ins)


def _ln_silu_fwd(x, g, b, *, name):
    r, w = x.shape
    tr = _pick(r, 512, 8)

    def body(x_ref, g_ref, b_ref, o_ref):
        xv = x_ref[...]
        mu = jnp.mean(xv, axis=-1, keepdims=True)
        xc = xv - mu
        rstd = lax.rsqrt(jnp.mean(xc * xc, axis=-1, keepdims=True) + EPS)
        o_ref[...] = _silu(xc * rstd * g_ref[...] + b_ref[...]).astype(o_ref.dtype)

    row = pl.BlockSpec((tr, w), lambda i: (i, 0))
    vec = pl.BlockSpec((1, w), lambda i: (0, 0))
    return pl.pallas_call(
        body, name=name, grid=(r // tr,), in_specs=[row, vec, vec], out_specs=row,
        out_shape=jax.ShapeDtypeStruct((r, w), BF16), compiler_params=_cparams(("parallel",)),
    )(x, g, b)


def _ln_silu_bwd(x, g, b, dy, *, name):
    r, w = x.shape
    tr = _pick(r, 512, 8)

    def body(x_ref, g_ref, b_ref, dy_ref, dx_ref, dg_ref, db_ref):
        xv = x_ref[...]
        mu = jnp.mean(xv, axis=-1, keepdims=True)
        xc = xv - mu
        rstd = lax.rsqrt(jnp.mean(xc * xc, axis=-1, keepdims=True) + EPS)
        xh = xc * rstd
        pre = xh * g_ref[...] + b_ref[...]
        dpre = dy_ref[...].astype(F32) * _dsilu(pre)
        dxh = dpre * g_ref[...]
        dx_ref[...] = rstd * (dxh - jnp.mean(dxh, axis=-1, keepdims=True)
                              - xh * jnp.mean(dxh * xh, axis=-1, keepdims=True))
        pg = jnp.sum(dpre * xh, axis=0, keepdims=True)
        pb = jnp.sum(dpre, axis=0, keepdims=True)

        @pl.when(pl.program_id(0) == 0)
        def _():
            dg_ref[...] = pg
            db_ref[...] = pb

        @pl.when(pl.program_id(0) > 0)
        def _():
            dg_ref[...] += pg
            db_ref[...] += pb

    row = pl.BlockSpec((tr, w), lambda i: (i, 0))
    vec = pl.BlockSpec((1, w), lambda i: (0, 0))
    return pl.pallas_call(
        body, name=name, grid=(r // tr,), in_specs=[row, vec, vec, row], out_specs=[row, vec, vec],
        out_shape=[jax.ShapeDtypeStruct((r, w), F32), jax.ShapeDtypeStruct((1, w), F32),
                   jax.ShapeDtypeStruct((1, w), F32)],
        compiler_params=_cparams(("arbitrary",)),
    )(x, g, b, dy)


CONV_PAD = 32
CONV_T = 256


def _conv_fwd(src, w, b, *, glu, act, name):
    s = src.shape[0]
    k, c = w.shape
    tc = LANE
    ncb = c // tc
    tt = _pick(s, CONV_T, 8)
    assert k - 1 <= CONV_PAD

    def body(*refs):
        if glu:
            a_ref, g_ref, w_ref, b_ref = refs[:4]
            outs = refs[4:-1]
        else:
            a_ref, w_ref, b_ref = refs[:3]
            outs = refs[3:-1]
        xp = refs[-1]
        xp[0:CONV_PAD, :] = jnp.zeros((CONV_PAD, tc), F32)
        if glu:
            xp[CONV_PAD:CONV_PAD + s, :] = a_ref[...] * _sigmoid(g_ref[...])
        else:
            xp[CONV_PAD:CONV_PAD + s, :] = a_ref[...]
        wv = w_ref[...]
        bv = b_ref[...]
        for t0 in range(0, s, tt):
            acc = jnp.zeros((tt, tc), F32) + bv
            for kk in range(k):
                off = CONV_PAD + t0 - (k - 1) + kk
                acc = acc + wv[kk:kk + 1, :] * xp[off:off + tt, :]
            outs[0][t0:t0 + tt, :] = acc
            if act:
                outs[1][t0:t0 + tt, :] = _silu(acc)

    col = pl.BlockSpec((s, tc), lambda j: (0, j))
    in_specs = [col, pl.BlockSpec((s, tc), lambda j: (0, j + ncb))] if glu else [col]
    in_specs += [pl.BlockSpec((k, tc), lambda j: (0, j)), pl.BlockSpec((1, tc), lambda j: (0, j))]
    n_out = 2 if act else 1
    res = pl.pallas_call(
        body, name=name, grid=(ncb,), in_specs=in_specs,
        out_specs=[col] * n_out,
        out_shape=[jax.ShapeDtypeStruct((s, c), F32)] * n_out,
        scratch_shapes=[pltpu.VMEM((CONV_PAD + s, tc), F32)],
        compiler_params=_cparams(("parallel",), 48),
    )(*([src, src] if glu else [src]), w, b)
    return res


def _conv_bwd(src, w, b, dy, pre, *, glu, act, name):
    s = src.shape[0]
    k, c = w.shape
    tc = LANE
    ncb = c // tc
    tt = _pick(s, CONV_T, 8)

    def body(*refs):
        i = 0
        a_ref = refs[i]; i += 1
        if glu:
            g_ref = refs[i]; i += 1
        w_ref = refs[i]; i += 1
        dy_ref = refs[i]; i += 1
        if act:
            pre_ref = refs[i]; i += 1
        da_ref = refs[i]; i += 1
        if glu:
            dg_ref = refs[i]; i += 1
        dw_ref = refs[i]; db_ref = refs[i + 1]
        xp, dp = refs[-2], refs[-1]
        xp[0:CONV_PAD, :] = jnp.zeros((CONV_PAD, tc), F32)
        if glu:
            xp[CONV_PAD:CONV_PAD + s, :] = a_ref[...] * _sigmoid(g_ref[...])
        else:
            xp[CONV_PAD:CONV_PAD + s, :] = a_ref[...]
        dp[s:s + CONV_PAD, :] = jnp.zeros((CONV_PAD, tc), F32)
        if act:
            dp[0:s, :] = dy_ref[...].astype(F32) * _dsilu(pre_ref[...])
        else:
            dp[0:s, :] = dy_ref[...].astype(F32)
        wv = w_ref[...]
        dws = [jnp.zeros((1, tc), F32) for _ in range(k)]
        dbs = jnp.zeros((1, tc), F32)
        for t0 in range(0, s, tt):
            acc = jnp.zeros((tt, tc), F32)
            dcur = dp[t0:t0 + tt, :]
            dbs = dbs + jnp.sum(dcur, axis=0, keepdims=True)
            for kk in range(k):
                acc = acc + wv[kk:kk + 1, :] * dp[t0 + (k - 1) - kk:t0 + (k - 1) - kk + tt, :]
                off = CONV_PAD + t0 - (k - 1) + kk
                dws[kk] = dws[kk] + jnp.sum(dcur * xp[off:off + tt, :], axis=0, keepdims=True)
            if glu:
                av = a_ref[t0:t0 + tt, :]
                sg = _sigmoid(g_ref[t0:t0 + tt, :])
                da_ref[t0:t0 + tt, :] = (acc * sg).astype(da_ref.dtype)
                dg_ref[t0:t0 + tt, :] = (acc * av * sg * (1.0 - sg)).astype(dg_ref.dtype)
            else:
                da_ref[t0:t0 + tt, :] = acc.astype(da_ref.dtype)
        for kk in range(k):
            dw_ref[kk:kk + 1, :] = dws[kk]
        db_ref[...] = dbs

    col = pl.BlockSpec((s, tc), lambda j: (0, j))
    in_specs = [col] + ([pl.BlockSpec((s, tc), lambda j: (0, j + ncb))] if glu else [])
    in_specs += [pl.BlockSpec((k, tc), lambda j: (0, j)), col] + ([col] if act else [])
    ins = ([src, src] if glu else [src]) + [w, dy] + ([pre] if act else [])
    out_specs = [col] + ([col] if glu else []) + [pl.BlockSpec((k, tc), lambda j: (0, j)),
                                                  pl.BlockSpec((1, tc), lambda j: (0, j))]
    out_shape = [jax.ShapeDtypeStruct((s, c), BF16)] * (2 if glu else 1) + [
        jax.ShapeDtypeStruct((k, c), F32), jax.ShapeDtypeStruct((1, c), F32)]
    return pl.pallas_call(
        body, name=name, grid=(ncb,), in_specs=in_specs, out_specs=out_specs, out_shape=out_shape,
        scratch_shapes=[pltpu.VMEM((CONV_PAD + s, tc), F32), pltpu.VMEM((s + CONV_PAD, tc), F32)],
        compiler_params=_cparams(("parallel",), 56),
    )(*ins)


def _ssd_consts():
    e = np.zeros((LANE, SSD_HEADS * SSD_P), np.float32)
    for h in range(SSD_HEADS):
        e[h, h * SSD_P:(h + 1) * SSD_P] = 1.0
    ltri = np.tril(np.ones((CHUNK, CHUNK), np.float32))
    return jnp.asarray(e), jnp.asarray(e.T.copy()), jnp.asarray(ltri), jnp.asarray(ltri.T.copy())


def _split3(x):
    hi = x.astype(BF16)
    r = x - hi.astype(F32)
    mid = r.astype(BF16)
    lo = (r - mid.astype(F32)).astype(BF16)
    return hi, mid, lo


def _dot_sel(x, sel, x_left=True):
    sb = sel.astype(BF16)
    out = None
    for part in _split3(x):
        t = jnp.dot(part, sb, preferred_element_type=F32) if x_left else jnp.dot(sb, part, preferred_element_type=F32)
        out = t if out is None else out + t
    return out


def _ssd_chunk_terms(dtr_ref, dtrt_ref, bias_ref, biast_ref, alog_ref, alogt_ref, e_ref, ltri_ref, utri_ref):
    a_neg = -jnp.exp(alog_ref[...])
    dt = _softplus(dtr_ref[...] + bias_ref[...])
    a = dt * a_neg
    s = _dot_sel(a, ltri_ref[...], x_left=False)
    dtt = _softplus(dtrt_ref[...] + biast_ref[...])
    st = _dot_sel(dtt * (-jnp.exp(alogt_ref[...])), utri_ref[...])
    ev = e_ref[...]
    s_x = _dot_sel(s, ev)
    dt_x = _dot_sel(dt, ev)
    return a_neg, dt, s, st, s_x, dt_x


def _ssd_decay(s, st, h, tril):
    seg = s[:, h:h + 1] - st[h:h + 1, :]
    return jnp.exp(jnp.where(tril, seg, NEG))


def _ssd_decay_t(s, st, h, triu):
    seg = st[h:h + 1, :] - s[:, h:h + 1]
    return jnp.exp(jnp.where(triu, seg, NEG))


def _head_masks():
    lane = lax.broadcasted_iota(jnp.int32, (1, SSD_P * 4), 1)
    return [((lane >= r * SSD_P) & (lane < (r + 1) * SSD_P)).astype(F32) for r in range(4)]


def _ssd_scan_fwd(xbc, dtr, dtrt, bias, biast, alog, alogt, dskip, *, name):
    s_len = xbc.shape[0]
    nc = s_len // CHUNK
    e, et, ltri, utri = _ssd_consts()
    gw = SSD_P * 4

    def body(xs_ref, b_ref, c_ref, dtr_ref, dtrt_ref, bias_ref, biast_ref, alog_ref, alogt_ref, d_ref,
             e_ref, ltri_ref, utri_ref, y_ref, prev_ref, state):
        @pl.when(pl.program_id(0) == 0)
        def _():
            state[...] = jnp.zeros_like(state)

        a_neg, dt, s, st, s_x, dt_x = _ssd_chunk_terms(dtr_ref, dtrt_ref, bias_ref, biast_ref, alog_ref,
                                                       alogt_ref, e_ref, ltri_ref, utri_ref)
        s_last = s_x[CHUNK - 1:CHUNK, :]
        es_x = jnp.exp(s_x)
        w_x = jnp.exp(s_last - s_x)
        cd_x = jnp.exp(s_last)
        d_x = _dot_sel(jnp.broadcast_to(d_ref[...], (8, LANE)), e_ref[...])[0:1, :]
        xs = xs_ref[...]
        xv = xs * dt_x
        row = lax.broadcasted_iota(jnp.int32, (CHUNK, CHUNK), 0)
        colm = lax.broadcasted_iota(jnp.int32, (CHUNK, CHUNK), 1)
        tril = colm <= row
        masks = _head_masks()
        for g in range(SSD_GROUPS):
            gs = slice(g * gw, (g + 1) * gw)
            bg = b_ref[:, g * SSD_N:(g + 1) * SSD_N].astype(BF16)
            cg = c_ref[:, g * SSD_N:(g + 1) * SSD_N].astype(BF16)
            xg = xv[:, gs]
            hg = state[g]
            prev_ref[g] = hg
            cb = lax.dot_general(cg, bg, (((1,), (1,)), ((), ())), preferred_element_type=F32)
            yg = jnp.dot(cg, hg.astype(BF16), preferred_element_type=F32) * es_x[:, gs]
            for r in range(4):
                m = (cb * _ssd_decay(s, st, g * 4 + r, tril)).astype(BF16)
                yg = yg + jnp.dot(m, (xg * masks[r]).astype(BF16), preferred_element_type=F32)
            y_ref[:, gs] = yg + d_x[:, gs] * xs[:, gs]
            upd = lax.dot_general(bg, (xg * w_x[:, gs]).astype(BF16), (((0,), (0,)), ((), ())),
                                  preferred_element_type=F32)
            state[g] = hg * cd_x[:, gs] + upd

    nh = LANE
    chunk_row = lambda w, cb: pl.BlockSpec((CHUNK, w), lambda i, cb=cb: (i, cb))
    full = lambda a: pl.BlockSpec(a.shape, lambda i: (0,) * a.ndim)
    in_specs = [chunk_row(1024, 0), chunk_row(512, 2), chunk_row(512, 3), chunk_row(nh, 0),
                pl.BlockSpec((None, nh, CHUNK), lambda i: (i, 0, 0)),
                full(bias), full(biast), full(alog), full(alogt), full(dskip), full(e), full(ltri), full(utri)]
    return pl.pallas_call(
        body, name=name, grid=(nc,), in_specs=in_specs,
        out_specs=[pl.BlockSpec((CHUNK, 1024), lambda i: (i, 0)),
                   pl.BlockSpec((None, SSD_GROUPS, SSD_N, gw), lambda i: (i, 0, 0, 0))],
        out_shape=[jax.ShapeDtypeStruct((s_len, 1024), F32),
                   jax.ShapeDtypeStruct((nc, SSD_GROUPS, SSD_N, gw), F32)],
        scratch_shapes=[pltpu.VMEM((SSD_GROUPS, SSD_N, gw), F32)],
        compiler_params=_cparams(("arbitrary",)),
    )(xbc, xbc, xbc, dtr, dtrt, bias, biast, alog, alogt, dskip, e, ltri, utri)


def _ssd_scan_bwd(xbc, dtr, dtrt, bias, biast, alog, alogt, dskip, prev, dy, *, name):
    s_len = xbc.shape[0]
    nc = s_len // CHUNK
    e, et, ltri, utri = _ssd_consts()
    gw = SSD_P * 4

    def body(xs_ref, b_ref, c_ref, dtr_ref, dtrt_ref, bias_ref, biast_ref, alog_ref, alogt_ref, d_ref,
             e_ref, et_ref, ltri_ref, utri_ref, prev_ref, dy_ref,
             dxs_ref, db_ref, dc_ref, ddtr_ref, dalog_ref, dbias_ref, dd_ref, dstate):
        step = pl.program_id(0)

        @pl.when(step == 0)
        def _():
            dstate[...] = jnp.zeros_like(dstate)

        a_neg, dt, s, st, s_x, dt_x = _ssd_chunk_terms(dtr_ref, dtrt_ref, bias_ref, biast_ref, alog_ref,
                                                       alogt_ref, e_ref, ltri_ref, utri_ref)
        s_last = s_x[CHUNK - 1:CHUNK, :]
        es_x = jnp.exp(s_x)
        w_x = jnp.exp(s_last - s_x)
        cd_x = jnp.exp(s_last)
        d_x = _dot_sel(jnp.broadcast_to(d_ref[...], (8, LANE)), e_ref[...])[0:1, :]
        xs = xs_ref[...]
        xv = xs * dt_x
        dyv = dy_ref[...]
        row = lax.broadcasted_iota(jnp.int32, (CHUNK, CHUNK), 0)
        colm = lax.broadcasted_iota(jnp.int32, (CHUNK, CHUNK), 1)
        tril = colm <= row
        masks = _head_masks()
        is_last = lax.broadcasted_iota(jnp.int32, (CHUNK, 1), 0) == CHUNK - 1
        nt = (((1,), (1,)), ((), ()))
        tn = (((0,), (0,)), ((), ()))
        ds_parts, ddt_parts = [], []
        head_lane = lax.broadcasted_iota(jnp.int32, (CHUNK, LANE), 1)
        triu = colm >= row
        ds_diag = jnp.zeros((CHUNK, LANE), F32)
        for g in range(SSD_GROUPS):
            gs = slice(g * gw, (g + 1) * gw)
            bg = b_ref[:, g * SSD_N:(g + 1) * SSD_N].astype(BF16)
            cg = c_ref[:, g * SSD_N:(g + 1) * SSD_N].astype(BF16)
            xg = xv[:, gs]
            xgb = xg.astype(BF16)
            hg = prev_ref[g]
            hgb = hg.astype(BF16)
            dsg = dstate[g]
            dsgb = dsg.astype(BF16)
            dyg = dyv[:, gs]
            dye = (dyg * es_x[:, gs]).astype(BF16)
            xw = (xg * w_x[:, gs]).astype(BF16)
            cb = lax.dot_general(cg, bg, nt, preferred_element_type=F32)
            cbt = lax.dot_general(bg, cg, nt, preferred_element_type=F32)
            dcg = lax.dot_general(dye, hgb, nt, preferred_element_type=F32)
            dh = lax.dot_general(cg, dye, tn, preferred_element_type=F32)
            bds = jnp.dot(bg, dsgb, preferred_element_type=F32)
            yoff = es_x[:, gs] * jnp.dot(cg, hgb, preferred_element_type=F32)
            dx_state = w_x[:, gs] * bds
            dbg = lax.dot_general(xw, dsgb, nt, preferred_element_type=F32)
            dxd = jnp.zeros((CHUNK, gw), F32)
            dcb = jnp.zeros((CHUNK, CHUNK), F32)
            for r in range(4):
                dec = _ssd_decay(s, st, g * 4 + r, tril)
                mf = cb * dec
                m = mf.astype(BF16)
                dym = (dyg * masks[r]).astype(BF16)
                dm = lax.dot_general(dym, xgb, nt, preferred_element_type=F32)
                dxd = dxd + lax.dot_general(m, dym, tn, preferred_element_type=F32)
                dcb = dcb + dm * dec
                dmt = lax.dot_general(xgb, dym, nt, preferred_element_type=F32)
                rc = (jnp.sum(dm * mf, axis=1, keepdims=True)
                      - jnp.sum(dmt * cbt * _ssd_decay_t(s, st, g * 4 + r, triu), axis=1, keepdims=True))
                ds_diag = ds_diag + jnp.where(head_lane == g * 4 + r, rc, 0.0)
            dcbb = dcb.astype(BF16)
            dcg = dcg + jnp.dot(dcbb, bg, preferred_element_type=F32)
            dbg = dbg + lax.dot_general(dcbb, cg, tn, preferred_element_type=F32)
            dxg = dxd + dx_state
            extra = (jnp.sum(xg * dx_state, axis=0, keepdims=True)
                     + cd_x[:, gs] * jnp.sum(dsg * hg, axis=0, keepdims=True))
            ds_parts.append(dyg * yoff - xg * dx_state + jnp.where(is_last, extra, 0.0))
            ddt_parts.append(dxg * xs[:, gs])
            dxs_ref[:, gs] = dxg * dt_x[:, gs] + d_x[:, gs] * dyg
            db_ref[:, g * SSD_N:(g + 1) * SSD_N] = dbg
            dc_ref[:, g * SSD_N:(g + 1) * SSD_N] = dcg
            dstate[g] = cd_x[:, gs] * dsg + dh
        etv = et_ref[...]
        ds = ds_diag + _dot_sel(jnp.concatenate(ds_parts, axis=1), etv)
        da = _dot_sel(ds, utri_ref[...], x_left=False)
        ddt = da * a_neg + _dot_sel(jnp.concatenate(ddt_parts, axis=1), etv)
        ddtr = ddt * _sigmoid(dtr_ref[...] + bias_ref[...])
        ddtr_ref[...] = ddtr.astype(ddtr_ref.dtype)
        p_alog = jnp.sum(da * dt, axis=0, keepdims=True) * a_neg
        p_bias = jnp.sum(ddtr, axis=0, keepdims=True)
        p_d = _dot_sel(jnp.broadcast_to(jnp.sum(dyv * xs, axis=0, keepdims=True), (8, SSD_HEADS * SSD_P)),
                       etv)[0:1, :]

        @pl.when(step == 0)
        def _():
            dalog_ref[...] = p_alog
            dbias_ref[...] = p_bias
            dd_ref[...] = p_d

        @pl.when(step > 0)
        def _():
            dalog_ref[...] += p_alog
            dbias_ref[...] += p_bias
            dd_ref[...] += p_d

    nh = LANE
    rev = lambda i: nc - 1 - i
    chunk_row = lambda w, cb: pl.BlockSpec((CHUNK, w), lambda i, cb=cb: (rev(i), cb))
    full = lambda a: pl.BlockSpec(a.shape, lambda i: (0,) * a.ndim)
    vec = pl.BlockSpec((1, nh), lambda i: (0, 0))
    in_specs = [chunk_row(1024, 0), chunk_row(512, 2), chunk_row(512, 3), chunk_row(nh, 0),
                pl.BlockSpec((None, nh, CHUNK), lambda i: (rev(i), 0, 0)),
                full(bias), full(biast), full(alog), full(alogt), full(dskip), full(e), full(et), full(ltri),
                full(utri),
                pl.BlockSpec((None, SSD_GROUPS, SSD_N, gw), lambda i: (rev(i), 0, 0, 0)), chunk_row(1024, 0)]
    return pl.pallas_call(
        body, name=name, grid=(nc,), in_specs=in_specs,
        out_specs=[chunk_row(1024, 0), chunk_row(512, 0), chunk_row(512, 0), chunk_row(nh, 0), vec, vec, vec],
        out_shape=[jax.ShapeDtypeStruct((s_len, 1024), F32), jax.ShapeDtypeStruct((s_len, 512), F32),
                   jax.ShapeDtypeStruct((s_len, 512), F32), jax.ShapeDtypeStruct((s_len, nh), BF16),
                   jax.ShapeDtypeStruct((1, nh), F32), jax.ShapeDtypeStruct((1, nh), F32),
                   jax.ShapeDtypeStruct((1, nh), F32)],
        scratch_shapes=[pltpu.VMEM((SSD_GROUPS, SSD_N, gw), F32)],
        compiler_params=_cparams(("arbitrary",)),
    )(xbc, xbc, xbc, dtr, dtrt, bias, biast, alog, alogt, dskip, e, et, ltri, utri, prev, dy)


def _ssd_gate_fwd(y, z, g, *, name):
    r, w = y.shape
    tr = _pick(r, 512, 8)
    gw = w // SSD_GROUPS

    def body(y_ref, z_ref, g_ref, o_ref):
        for k in range(SSD_GROUPS):
            cs = slice(k * gw, (k + 1) * gw)
            t = y_ref[:, cs] * _silu(z_ref[:, cs])
            rstd = lax.rsqrt(jnp.mean(t * t, axis=-1, keepdims=True) + EPS)
            o_ref[:, cs] = (t * rstd * g_ref[:, cs]).astype(o_ref.dtype)

    row = pl.BlockSpec((tr, w), lambda i: (i, 0))
    vec = pl.BlockSpec((1, w), lambda i: (0, 0))
    return pl.pallas_call(
        body, name=name, grid=(r // tr,), in_specs=[row, row, vec], out_specs=row,
        out_shape=jax.ShapeDtypeStruct((r, w), BF16), compiler_params=_cparams(("parallel",)),
    )(y, z, g)


def _ssd_gate_bwd(y, z, g, do, *, name):
    r, w = y.shape
    tr = _pick(r, 512, 8)
    gw = w // SSD_GROUPS

    def body(y_ref, z_ref, g_ref, do_ref, dy_ref, dz_ref, dg_ref):
        parts = []
        for k in range(SSD_GROUPS):
            cs = slice(k * gw, (k + 1) * gw)
            yv = y_ref[:, cs]
            zv = z_ref[:, cs]
            sz = _silu(zv)
            t = yv * sz
            rstd = lax.rsqrt(jnp.mean(t * t, axis=-1, keepdims=True) + EPS)
            th = t * rstd
            dov = do_ref[:, cs].astype(F32)
            dog = dov * g_ref[:, cs]
            dt = rstd * (dog - th * jnp.mean(dog * th, axis=-1, keepdims=True))
            dy_ref[:, cs] = dt * sz
            dz_ref[:, cs] = (dt * yv * _dsilu(zv)).astype(dz_ref.dtype)
            parts.append(jnp.sum(dov * th, axis=0, keepdims=True))
        pg = jnp.concatenate(parts, axis=1)

        @pl.when(pl.program_id(0) == 0)
        def _():
            dg_ref[...] = pg

        @pl.when(pl.program_id(0) > 0)
        def _():
            dg_ref[...] += pg

    row = pl.BlockSpec((tr, w), lambda i: (i, 0))
    vec = pl.BlockSpec((1, w), lambda i: (0, 0))
    return pl.pallas_call(
        body, name=name, grid=(r // tr,), in_specs=[row, row, vec, row], out_specs=[row, row, vec],
        out_shape=[jax.ShapeDtypeStruct((r, w), F32), jax.ShapeDtypeStruct((r, w), BF16),
                   jax.ShapeDtypeStruct((1, w), F32)],
        compiler_params=_cparams(("arbitrary",)),
    )(y, z, g, do)


def _rope_swap(x):
    lane = lax.broadcasted_iota(jnp.int32, x.shape, 1)
    lo = pltpu.roll(x, 96, 1)
    hi = pltpu.roll(x, 32, 1)
    return jnp.where(lane < 32, lo, jnp.where(lane < 64, hi, 0.0))


def _norm_part(v, g, n):
    rstd = lax.rsqrt(jnp.sum(v * v, axis=-1, keepdims=True) * (1.0 / n) + EPS)
    xh = v * rstd
    return xh * g, xh, rstd


def _norm_part_bwd(dout, g, xh, rstd, n):
    dg = dout * g
    return rstd * (dg - xh * (jnp.sum(dg * xh, axis=-1, keepdims=True) * (1.0 / n)))


def _mla_prep_fwd(q, kv, krr, cs, sn, gq, gk, *, name):
    s = q.shape[0]
    tr = _pick(s, 256, 8)
    hp = MLA_HP

    def body(q_ref, kv_ref, krr_ref, cs_ref, sn_ref, gq_ref, gk_ref, qf_ref, kf_ref, v_ref):
        csv, snv = cs_ref[...], sn_ref[...]
        gqn, gqr = gq_ref[:, 0:128], gq_ref[:, 128:256]
        gkn, gkr = gk_ref[:, 0:128], gk_ref[:, 128:256]
        kr, _, _ = _norm_part(krr_ref[...], gkr, MLA_ROPE)
        kr = (kr * csv + _rope_swap(kr) * snv).astype(BF16)
        for h in range(MLA_HEADS):
            qn, _, _ = _norm_part(q_ref[:, h * hp:h * hp + 128], gqn, MLA_NOPE)
            qr, _, _ = _norm_part(q_ref[:, h * hp + 128:(h + 1) * hp], gqr, MLA_ROPE)
            qr = qr * csv + _rope_swap(qr) * snv
            qf_ref[h, :, 0:128] = qn.astype(BF16)
            qf_ref[h, :, 128:256] = qr.astype(BF16)
            kn, _, _ = _norm_part(kv_ref[:, h * hp:h * hp + 128], gkn, MLA_NOPE)
            kf_ref[h, :, 0:128] = kn.astype(BF16)
            kf_ref[h, :, 128:256] = kr
            v_ref[h] = kv_ref[:, h * hp + 128:(h + 1) * hp].astype(BF16)

    row = lambda w: pl.BlockSpec((tr, w), lambda i: (i, 0))
    vec = pl.BlockSpec((1, hp), lambda i: (0, 0))
    hrow = lambda w: pl.BlockSpec((MLA_HEADS, tr, w), lambda i: (0, i, 0))
    return pl.pallas_call(
        body, name=name, grid=(s // tr,),
        in_specs=[row(MLA_HEADS * hp), row(MLA_HEADS * hp), row(128), row(128), row(128), vec, vec],
        out_specs=[hrow(hp), hrow(hp), hrow(128)],
        out_shape=[jax.ShapeDtypeStruct((MLA_HEADS, s, hp), BF16), jax.ShapeDtypeStruct((MLA_HEADS, s, hp), BF16),
                   jax.ShapeDtypeStruct((MLA_HEADS, s, 128), BF16)],
        compiler_params=_cparams(("parallel",)),
    )(q, kv, krr, cs, sn, gq, gk)


def _mla_prep_bwd(q, kv, krr, cs, sn, gq, gk, dqf, dkf, dv, *, name):
    s = q.shape[0]
    tr = _pick(s, 256, 8)
    hp = MLA_HP

    def body(q_ref, kv_ref, krr_ref, cs_ref, sn_ref, gq_ref, gk_ref, dqf_ref, dkf_ref, dv_ref,
             dq_ref, dkv_ref, dkrr_ref, dgq_ref, dgk_ref):
        csv, snv = cs_ref[...], sn_ref[...]
        gqn, gqr = gq_ref[:, 0:128], gq_ref[:, 128:256]
        gkn, gkr = gk_ref[:, 0:128], gk_ref[:, 128:256]
        _, krh, krs = _norm_part(krr_ref[...], gkr, MLA_ROPE)
        dkr_sum = jnp.zeros((tr, 128), F32)
        pgqn = jnp.zeros((1, 128), F32)
        pgqr = jnp.zeros((1, 128), F32)
        pgkn = jnp.zeros((1, 128), F32)
        for h in range(MLA_HEADS):
            _, qnh, qns = _norm_part(q_ref[:, h * hp:h * hp + 128], gqn, MLA_NOPE)
            _, qrh, qrs = _norm_part(q_ref[:, h * hp + 128:(h + 1) * hp], gqr, MLA_ROPE)
            dqn = dqf_ref[h, :, 0:128]
            drr = dqf_ref[h, :, 128:256]
            dqr = drr * csv + _rope_swap(drr * snv)
            dq_ref[:, h * hp:h * hp + 128] = _norm_part_bwd(dqn, gqn, qnh, qns, MLA_NOPE).astype(dq_ref.dtype)
            dq_ref[:, h * hp + 128:(h + 1) * hp] = _norm_part_bwd(dqr, gqr, qrh, qrs, MLA_ROPE).astype(dq_ref.dtype)
            pgqn = pgqn + jnp.sum(dqn * qnh, axis=0, keepdims=True)
            pgqr = pgqr + jnp.sum(dqr * qrh, axis=0, keepdims=True)
            _, knh, kns = _norm_part(kv_ref[:, h * hp:h * hp + 128], gkn, MLA_NOPE)
            dkn = dkf_ref[h, :, 0:128]
            dkv_ref[:, h * hp:h * hp + 128] = _norm_part_bwd(dkn, gkn, knh, kns, MLA_NOPE).astype(dkv_ref.dtype)
            dkv_ref[:, h * hp + 128:(h + 1) * hp] = dv_ref[h].astype(dkv_ref.dtype)
            pgkn = pgkn + jnp.sum(dkn * knh, axis=0, keepdims=True)
            dkr_sum = dkr_sum + dkf_ref[h, :, 128:256]
        dkr = dkr_sum * csv + _rope_swap(dkr_sum * snv)
        dkrr_ref[...] = _norm_part_bwd(dkr, gkr, krh, krs, MLA_ROPE).astype(dkrr_ref.dtype)
        pgkr = jnp.sum(dkr * krh, axis=0, keepdims=True)
        pq = jnp.concatenate([pgqn, pgqr], axis=1)
        pk = jnp.concatenate([pgkn, pgkr], axis=1)

        @pl.when(pl.program_id(0) == 0)
        def _():
            dgq_ref[...] = pq
            dgk_ref[...] = pk

        @pl.when(pl.program_id(0) > 0)
        def _():
            dgq_ref[...] += pq
            dgk_ref[...] += pk

    row = lambda w: pl.BlockSpec((tr, w), lambda i: (i, 0))
    vec = pl.BlockSpec((1, hp), lambda i: (0, 0))
    hrow = lambda w: pl.BlockSpec((MLA_HEADS, tr, w), lambda i: (0, i, 0))
    return pl.pallas_call(
        body, name=name, grid=(s // tr,),
        in_specs=[row(MLA_HEADS * hp), row(MLA_HEADS * hp), row(128), row(128), row(128), vec, vec,
                  hrow(hp), hrow(hp), hrow(128)],
        out_specs=[row(MLA_HEADS * hp), row(MLA_HEADS * hp), row(128), vec, vec],
        out_shape=[jax.ShapeDtypeStruct((s, MLA_HEADS * hp), BF16), jax.ShapeDtypeStruct((s, MLA_HEADS * hp), BF16),
                   jax.ShapeDtypeStruct((s, 128), BF16), jax.ShapeDtypeStruct((1, hp), F32),
                   jax.ShapeDtypeStruct((1, hp), F32)],
        compiler_params=_cparams(("arbitrary",), 48),
    )(q, kv, krr, cs, sn, gq, gk, dqf, dkf, dv)


ATT_T = 512


def _chunk_mask(t):
    r = lax.shift_right_logical(lax.broadcasted_iota(jnp.int32, (t, t), 0), 6)
    c = lax.shift_right_logical(lax.broadcasted_iota(jnp.int32, (t, t), 1), 6)
    return c <= r


def _mla_attn_fwd(qf, kf, v, *, name):
    nh, s, hp = qf.shape
    t = _pick(s, ATT_T, CHUNK)
    scale = (MLA_NOPE + MLA_ROPE) ** -0.5
    nt = (((1,), (1,)), ((), ()))

    def body(q_ref, k_ref, v_ref, o_ref, lse_ref):
        i = pl.program_id(1)
        q = q_ref[...]

        def block(j, carry, masked):
            m, l, acc = carry
            start = pl.multiple_of(j * t, t)
            k = k_ref[pl.ds(start, t), :]
            sc = lax.dot_general(q, k, nt, preferred_element_type=F32) * scale
            if masked:
                sc = jnp.where(_chunk_mask(t), sc, NEG)
            m_new = jnp.maximum(m, jnp.max(sc, axis=-1, keepdims=True))
            alpha = jnp.exp(m - m_new)
            p = jnp.exp(sc - m_new)
            l = alpha * l + jnp.sum(p, axis=-1, keepdims=True)
            acc = alpha * acc + jnp.dot(p.astype(BF16), v_ref[pl.ds(start, t), :], preferred_element_type=F32)
            return m_new, l, acc

        init = (jnp.full((t, 1), NEG, F32), jnp.zeros((t, 1), F32), jnp.zeros((t, MLA_V), F32))
        carry = lax.fori_loop(0, i, lambda j, c: block(j, c, False), init)
        m, l, acc = block(i, carry, True)
        o_ref[...] = acc / l
        lse_ref[...] = m + jnp.log(l)

    return pl.pallas_call(
        body, name=name, grid=(nh, s // t),
        in_specs=[pl.BlockSpec((None, t, hp), lambda h, i: (h, i, 0)),
                  pl.BlockSpec((None, s, hp), lambda h, i: (h, 0, 0)),
                  pl.BlockSpec((None, s, MLA_V), lambda h, i: (h, 0, 0))],
        out_specs=[pl.BlockSpec((t, MLA_V), lambda h, i: (i, h)),
                   pl.BlockSpec((None, t, 1), lambda h, i: (h, i, 0))],
        out_shape=[jax.ShapeDtypeStruct((s, nh * MLA_V), F32), jax.ShapeDtypeStruct((nh, s, 1), F32)],
        compiler_params=_cparams(("parallel", "arbitrary"), 48),
    )(qf, kf, v)


def _mla_attn_bwd(qf, kf, v, o, lse, do, *, name):
    nh, s, hp = qf.shape
    t = _pick(s, ATT_T, CHUNK)
    nb = s // t
    scale = (MLA_NOPE + MLA_ROPE) ** -0.5
    nt = (((1,), (1,)), ((), ()))
    tn = (((0,), (0,)), ((), ()))

    def body(q_ref, k_ref, v_ref, o_ref, lse_ref, do_ref, dq_ref, dk_ref, dv_ref, delta):
        j = pl.program_id(1)

        @pl.when(j == 0)
        def _():
            dq_ref[...] = jnp.zeros_like(dq_ref)
            delta[...] = jnp.sum(do_ref[...] * o_ref[...], axis=-1, keepdims=True)

        k = k_ref[...]
        vv = v_ref[...]

        def block(i, carry, masked):
            dk, dv = carry
            start = pl.multiple_of(i * t, t)
            q = q_ref[pl.ds(start, t), :]
            dob = do_ref[pl.ds(start, t), :].astype(BF16)
            sc = lax.dot_general(q, k, nt, preferred_element_type=F32) * scale
            if masked:
                sc = jnp.where(_chunk_mask(t), sc, NEG)
            p = jnp.exp(sc - lse_ref[pl.ds(start, t), :])
            dp = lax.dot_general(dob, vv, nt, preferred_element_type=F32)
            ds = (p * (dp - delta[pl.ds(start, t), :]) * scale).astype(BF16)
            dv = dv + lax.dot_general(p.astype(BF16), dob, tn, preferred_element_type=F32)
            dk = dk + lax.dot_general(ds, q, tn, preferred_element_type=F32)
            dq_ref[pl.ds(start, t), :] += jnp.dot(ds, k, preferred_element_type=F32)
            return dk, dv

        init = (jnp.zeros((t, hp), F32), jnp.zeros((t, MLA_V), F32))
        carry = block(j, init, True)
        dk, dv = lax.fori_loop(j + 1, nb, lambda i, c: block(i, c, False), carry)
        dk_ref[...] = dk
        dv_ref[...] = dv

    whole = lambda w: pl.BlockSpec((None, s, w), lambda h, j: (h, 0, 0))
    blk = lambda w: pl.BlockSpec((None, t, w), lambda h, j: (h, j, 0))
    colh = pl.BlockSpec((s, MLA_V), lambda h, j: (0, h))
    return pl.pallas_call(
        body, name=name, grid=(nh, nb),
        in_specs=[whole(hp), blk(hp), blk(MLA_V), colh, whole(1), colh],
        out_specs=[whole(hp), blk(hp), blk(MLA_V)],
        out_shape=[jax.ShapeDtypeStruct((nh, s, hp), F32), jax.ShapeDtypeStruct((nh, s, hp), F32),
                   jax.ShapeDtypeStruct((nh, s, MLA_V), F32)],
        scratch_shapes=[pltpu.VMEM((s, 1), F32)],
        compiler_params=_cparams(("parallel", "arbitrary"), 56),
    )(qf, kf, v, o, lse, do)


def _merge_fwd(gl, gb, ys, yc, ym, *, name):
    s, d = ys.shape
    tr = _pick(s, 256, 8)

    def body(gl_ref, gb_ref, ys_ref, yc_ref, ym_ref, o_ref):
        acc = jnp.zeros((tr, d), F32)
        for k, y_ref in enumerate((ys_ref, yc_ref, ym_ref)):
            gt = _sigmoid(gl_ref[:, k * d:(k + 1) * d] + gb_ref[:, k * d:(k + 1) * d])
            acc = acc + gt * y_ref[...]
        o_ref[...] = acc.astype(o_ref.dtype)

    row = lambda w: pl.BlockSpec((tr, w), lambda i: (i, 0))
    return pl.pallas_call(
        body, name=name, grid=(s // tr,),
        in_specs=[row(3 * d), pl.BlockSpec((1, 3 * d), lambda i: (0, 0)), row(d), row(d), row(d)],
        out_specs=row(d), out_shape=jax.ShapeDtypeStruct((s, d), BF16),
        compiler_params=_cparams(("parallel",)),
    )(gl, gb, ys, yc, ym)


def _merge_bwd(gl, gb, ys, yc, ym, dm, *, name):
    s, d = ys.shape
    tr = _pick(s, 256, 8)

    def body(gl_ref, gb_ref, ys_ref, yc_ref, ym_ref, dm_ref, dgl_ref, dgb_ref, dys_ref, dyc_ref, dym_ref):
        dmv = dm_ref[...]
        parts = []
        for k, (y_ref, dy_ref) in enumerate(((ys_ref, dys_ref), (yc_ref, dyc_ref), (ym_ref, dym_ref))):
            gt = _sigmoid(gl_ref[:, k * d:(k + 1) * d] + gb_ref[:, k * d:(k + 1) * d])
            dy_ref[...] = (gt * dmv).astype(dy_ref.dtype)
            dl = dmv * y_ref[...] * gt * (1.0 - gt)
            dgl_ref[:, k * d:(k + 1) * d] = dl.astype(dgl_ref.dtype)
            parts.append(jnp.sum(dl, axis=0, keepdims=True))
        pb = jnp.concatenate(parts, axis=1)

        @pl.when(pl.program_id(0) == 0)
        def _():
            dgb_ref[...] = pb

        @pl.when(pl.program_id(0) > 0)
        def _():
            dgb_ref[...] += pb

    row = lambda w: pl.BlockSpec((tr, w), lambda i: (i, 0))
    vec = pl.BlockSpec((1, 3 * d), lambda i: (0, 0))
    return pl.pallas_call(
        body, name=name, grid=(s // tr,),
        in_specs=[row(3 * d), vec, row(d), row(d), row(d), row(d)],
        out_specs=[row(3 * d), vec, row(d), row(d), row(d)],
        out_shape=[jax.ShapeDtypeStruct((s, 3 * d), BF16), jax.ShapeDtypeStruct((1, 3 * d), F32)]
        + [jax.ShapeDtypeStruct((s, d), BF16)] * 3,
        compiler_params=_cparams(("arbitrary",)),
    )(gl, gb, ys, yc, ym, dm)


def _xattn_fwd(q, k, v, gq, *, name):
    s, d = q.shape
    dh = d // X_HEADS
    tr = _pick(s, 512, 8)
    scale = dh ** -0.5
    nt = (((1,), (1,)), ((), ()))

    def body(q_ref, k_ref, v_ref, gq_ref, o_ref):
        for h in range(X_HEADS):
            cs = slice(h * dh, (h + 1) * dh)
            qn, _, _ = _norm_part(q_ref[:, cs], gq_ref[...], dh)
            sc = lax.dot_general(qn.astype(BF16), k_ref[:, cs], nt, preferred_element_type=F32) * scale
            p = jnp.exp(sc - jnp.max(sc, axis=-1, keepdims=True))
            p = p / jnp.sum(p, axis=-1, keepdims=True)
            o_ref[:, cs] = jnp.dot(p.astype(BF16), v_ref[:, cs], preferred_element_type=F32)

    row = pl.BlockSpec((tr, d), lambda i: (i, 0))
    mem = pl.BlockSpec(k.shape, lambda i: (0, 0))
    return pl.pallas_call(
        body, name=name, grid=(s // tr,),
        in_specs=[row, mem, mem, pl.BlockSpec((1, dh), lambda i: (0, 0))], out_specs=row,
        out_shape=jax.ShapeDtypeStruct((s, d), F32), compiler_params=_cparams(("parallel",)),
    )(q, k, v, gq)


def _xattn_bwd(q, k, v, gq, do, *, name):
    s, d = q.shape
    dh = d // X_HEADS
    tr = _pick(s, 512, 8)
    scale = dh ** -0.5
    nt = (((1,), (1,)), ((), ()))
    tn = (((0,), (0,)), ((), ()))

    def body(q_ref, k_ref, v_ref, gq_ref, do_ref, dq_ref, dk_ref, dv_ref, dgq_ref):
        first = pl.program_id(0) == 0
        pg = jnp.zeros((1, dh), F32)
        for h in range(X_HEADS):
            cs = slice(h * dh, (h + 1) * dh)
            qn, qh, qs = _norm_part(q_ref[:, cs], gq_ref[...], dh)
            qnb = qn.astype(BF16)
            kh = k_ref[:, cs]
            sc = lax.dot_general(qnb, kh, nt, preferred_element_type=F32) * scale
            p = jnp.exp(sc - jnp.max(sc, axis=-1, keepdims=True))
            p = p / jnp.sum(p, axis=-1, keepdims=True)
            dob = do_ref[:, cs].astype(BF16)
            dp = lax.dot_general(dob, v_ref[:, cs], nt, preferred_element_type=F32)
            ds = (p * (dp - jnp.sum(dp * p, axis=-1, keepdims=True)) * scale).astype(BF16)
            dqn = jnp.dot(ds, kh, preferred_element_type=F32)
            dq_ref[:, cs] = _norm_part_bwd(dqn, gq_ref[...], qh, qs, dh).astype(dq_ref.dtype)
            pg = pg + jnp.sum(dqn * qh, axis=0, keepdims=True)
            pv = lax.dot_general(p.astype(BF16), dob, tn, preferred_element_type=F32)
            pk = lax.dot_general(ds, qnb, tn, preferred_element_type=F32)

            @pl.when(first)
            def _():
                dv_ref[:, cs] = pv
                dk_ref[:, cs] = pk

            @pl.when(jnp.logical_not(first))
            def _():
                dv_ref[:, cs] += pv
                dk_ref[:, cs] += pk

        @pl.when(first)
        def _():
            dgq_ref[...] = pg

        @pl.when(jnp.logical_not(first))
        def _():
            dgq_ref[...] += pg

    row = pl.BlockSpec((tr, d), lambda i: (i, 0))
    mem = pl.BlockSpec(k.shape, lambda i: (0, 0))
    vec = pl.BlockSpec((1, dh), lambda i: (0, 0))
    return pl.pallas_call(
        body, name=name, grid=(s // tr,),
        in_specs=[row, mem, mem, vec, row], out_specs=[row, mem, mem, vec],
        out_shape=[jax.ShapeDtypeStruct((s, d), BF16), jax.ShapeDtypeStruct(k.shape, F32),
                   jax.ShapeDtypeStruct(k.shape, F32), jax.ShapeDtypeStruct((1, dh), F32)],
        compiler_params=_cparams(("arbitrary",)),
    )(q, k, v, gq, do)


def _swiglu_fwd(h1, *, name):
    s, w2 = h1.shape
    w = w2 // 2
    tr = _pick(s, 256, 8)
    tc = _pick(w, 1408, 128)
    ncb = w // tc

    def body(g_ref, u_ref, o_ref):
        o_ref[...] = (_silu(g_ref[...]) * u_ref[...]).astype(o_ref.dtype)

    return pl.pallas_call(
        body, name=name, grid=(s // tr, ncb),
        in_specs=[pl.BlockSpec((tr, tc), lambda i, j: (i, j)), pl.BlockSpec((tr, tc), lambda i, j: (i, j + ncb))],
        out_specs=pl.BlockSpec((tr, tc), lambda i, j: (i, j)),
        out_shape=jax.ShapeDtypeStruct((s, w), BF16), compiler_params=_cparams(("parallel", "parallel")),
    )(h1, h1)


def _swiglu_bwd(h1, dact, *, name):
    s, w2 = h1.shape
    w = w2 // 2
    tr = _pick(s, 256, 8)
    tc = _pick(w, 1408, 128)
    ncb = w // tc

    def body(g_ref, u_ref, d_ref, dg_ref, du_ref):
        gv = g_ref[...]
        dv = d_ref[...]
        dg_ref[...] = (dv * u_ref[...] * _dsilu(gv)).astype(dg_ref.dtype)
        du_ref[...] = (dv * _silu(gv)).astype(du_ref.dtype)

    blk = pl.BlockSpec((tr, tc), lambda i, j: (i, j))
    dg, du = pl.pallas_call(
        body, name=name, grid=(s // tr, ncb),
        in_specs=[blk, pl.BlockSpec((tr, tc), lambda i, j: (i, j + ncb)), blk],
        out_specs=[blk, blk],
        out_shape=[jax.ShapeDtypeStruct((s, w), BF16)] * 2, compiler_params=_cparams(("parallel", "parallel")),
    )(h1, h1, dact)
    return jnp.concatenate([dg, du], axis=1)


def _add(a, b, *, name):
    r, w = a.shape
    tr = _pick(r, 512, 8)

    def body(a_ref, b_ref, o_ref):
        o_ref[...] = a_ref[...] + b_ref[...].astype(F32)

    row = pl.BlockSpec((tr, w), lambda i: (i, 0))
    return pl.pallas_call(
        body, name=name, grid=(r // tr,), in_specs=[row, row], out_specs=row,
        out_shape=jax.ShapeDtypeStruct((r, w), F32), compiler_params=_cparams(("parallel",)),
    )(a, b)


def _loss(y, target, *, name):
    r, w = y.shape
    tr = _pick(r, 512, 8)

    def body(y_ref, t_ref, dy_ref, l_ref):
        err = y_ref[...] - t_ref[...]
        dy_ref[...] = err * (1.0 / w)
        part = jnp.zeros((8, LANE), F32) + 0.5 * jnp.sum(jnp.mean(err * err, axis=-1, keepdims=True))

        @pl.when(pl.program_id(0) == 0)
        def _():
            l_ref[...] = part

        @pl.when(pl.program_id(0) > 0)
        def _():
            l_ref[...] += part

    row = pl.BlockSpec((tr, w), lambda i: (i, 0))
    dy, l = pl.pallas_call(
        body, name=name, grid=(r // tr,), in_specs=[row, row],
        out_specs=[row, pl.BlockSpec((8, LANE), lambda i: (0, 0))],
        out_shape=[jax.ShapeDtypeStruct((r, w), F32), jax.ShapeDtypeStruct((8, LANE), F32)],
        compiler_params=_cparams(("arbitrary",)),
    )(y, target)
    return dy, l[0, 0]


def _adamw(w, g, m, v, *, name):
    r, c = w.shape
    tr = _pick(r, 256, 8)
    c1 = 1.0 - ADAM_B1 ** ADAM_STEP
    c2 = 1.0 - ADAM_B2 ** ADAM_STEP

    def body(w_ref, g_ref, m_ref, v_ref, d_ref, nm_ref, nv_ref):
        gv = g_ref[...]
        nm = ADAM_B1 * m_ref[...] + (1.0 - ADAM_B1) * gv
        nv = ADAM_B2 * v_ref[...] + (1.0 - ADAM_B2) * (gv * gv)
        nm_ref[...] = nm
        nv_ref[...] = nv
        d_ref[...] = -ADAM_LR * ((nm / c1) / (jnp.sqrt(nv / c2) + ADAM_EPS) + ADAM_WD * w_ref[...])

    row = pl.BlockSpec((tr, c), lambda i: (i, 0))
    return pl.pallas_call(
        body, name=name, grid=(r // tr,), in_specs=[row] * 4, out_specs=[row] * 3,
        out_shape=[jax.ShapeDtypeStruct((r, c), F32)] * 3, compiler_params=_cparams(("parallel",)),
    )(w, g, m, v)


IN_SPLIT = dict(z=(0, 1024), xbc=(1024, 3072), dt=(3072, 3200), glu=(3200, 5248), ql=(5248, 5632),
                ckv=(5632, 5888), kr=(5888, 6016), gate=(6016, 9088))


IN_WIDTH_PAD = 9216


def _w_in_pad(w):
    zeros = lambda n: jnp.zeros(w.shape[:-1] + (n,), w.dtype)
    return jnp.concatenate([w[..., :3088], zeros(112), w[..., 3088:5840], zeros(64), w[..., 5840:],
                            zeros(IN_WIDTH_PAD - 9088)], axis=-1)


def _w_in_unpad(g):
    return jnp.concatenate([g[..., :3088], g[..., 3200:5952], g[..., 6016:9088]], axis=-1)


def _qb_pad(w):
    lead = w.shape[:-1]
    w = w.reshape(lead + (MLA_HEADS, MLA_NOPE + MLA_ROPE))
    w = jnp.concatenate([w, jnp.zeros(lead + (MLA_HEADS, MLA_HP - MLA_NOPE - MLA_ROPE), w.dtype)], axis=-1)
    return w.reshape(lead + (MLA_HEADS * MLA_HP,))


def _qb_unpad(g):
    lead = g.shape[:-1]
    g = g.reshape(lead + (MLA_HEADS, MLA_HP))[..., :MLA_NOPE + MLA_ROPE]
    return g.reshape(lead + (MLA_HEADS * (MLA_NOPE + MLA_ROPE),))


def _pad_lanes(v, n):
    return jnp.concatenate([v, jnp.zeros((n - v.shape[0],), v.dtype)]).reshape(1, n)


def _layer_params(full, rep, l):
    p = {}
    w_in = full['w_in']
    for k, (a, b) in IN_SPLIT.items():
        p['w_' + k] = w_in[:, a:b]
    p['w_in'] = w_in
    for k in ('mla_w_q_b', 'mla_w_kv_b', 'xattn_w_kv', 'ffn_w_in', 'ssd_w_out', 'conv_w_out', 'mla_w_o', 'w_out',
              'xattn_w_q', 'xattn_w_o', 'ffn_w_out', 'ssd_conv_w', 'conv_dw_w'):
        p[k] = full[k]
    p['gate_b'] = full['gate_b'].reshape(1, -1)
    row = lambda name: rep[name][l].reshape(1, -1)
    for k in ('mix_norm_g', 'ssd_conv_b', 'ssd_norm_g', 'conv_dw_b', 'conv_ln_g', 'conv_ln_b', 'mla_q_a_g',
              'mla_kv_a_g', 'xattn_norm_g', 'mem_norm_g', 'xattn_q_norm_g', 'xattn_k_norm_g', 'ffn_norm_g'):
        p[k] = row(k)
    for k in ('ssd_dt_bias', 'ssd_a_log', 'ssd_d'):
        p[k] = _pad_lanes(rep[k][l], LANE)
        p[k + '_t'] = p[k].reshape(LANE, 1)
    p['gq'] = _pad_lanes(rep['mla_q_norm_g'][l], MLA_HP)
    p['gk'] = _pad_lanes(rep['mla_k_norm_g'][l], MLA_HP)
    return p


def _layer_fwd(x, mem, cs, sn, p, l):
    n = lambda s: f"l{l}_{s}"
    s_len, d = x.shape
    nc = s_len // CHUNK
    sv = {'x': x}
    u = _rms_fwd(x, p['mix_norm_g'], out_dtype=BF16, name=n("mix_norm"))
    z = _matmul(u, p['w_z'], name=n("in_z"))
    xbc = _matmul(u, p['w_xbc'], name=n("in_xbc"))
    dtr = _matmul(u, p['w_dt'], name=n("in_dt"))
    glu = _matmul(u, p['w_glu'], name=n("in_glu"))
    ql = _matmul(u, p['w_ql'], name=n("in_ql"))
    ckv = _matmul(u, p['w_ckv'], name=n("in_ckv"))
    krr = _matmul(u, p['w_kr'], name=n("in_kr"))
    gl = _matmul(u, p['w_gate'], name=n("in_gate"))
    pre_s, act_s = _conv_fwd(xbc, p['ssd_conv_w'], p['ssd_conv_b'], glu=False, act=True, name=n("ssd_conv"))
    dtrt = dtr.reshape(nc, CHUNK, LANE).transpose(0, 2, 1)
    y_scan, prev = _ssd_scan_fwd(act_s, dtr, dtrt, p['ssd_dt_bias'], p['ssd_dt_bias_t'], p['ssd_a_log'],
                                 p['ssd_a_log_t'], p['ssd_d'], name=n("ssd_scan"))
    yn = _ssd_gate_fwd(y_scan, z, p['ssd_norm_g'], name=n("ssd_gate"))
    y_ssd = _matmul(yn, p['ssd_w_out'], name=n("ssd_out"))
    pre_c, = _conv_fwd(glu, p['conv_dw_w'], p['conv_dw_b'], glu=True, act=False, name=n("dw_conv"))
    vc = _ln_silu_fwd(pre_c, p['conv_ln_g'], p['conv_ln_b'], name=n("conv_ln"))
    y_conv = _matmul(vc, p['conv_w_out'], name=n("conv_out"))
    qln = _rms_fwd(ql, p['mla_q_a_g'], out_dtype=BF16, name=n("q_a_norm"))
    q = _matmul(qln, p['mla_w_q_b'], name=n("q_b"))
    ckvn = _rms_fwd(ckv, p['mla_kv_a_g'], out_dtype=BF16, name=n("kv_a_norm"))
    kv = _matmul(ckvn, p['mla_w_kv_b'], name=n("kv_b"))
    qf, kf, v = _mla_prep_fwd(q, kv, krr, cs, sn, p['gq'], p['gk'], name=n("mla_prep"))
    o, lse = _mla_attn_fwd(qf, kf, v, name=n("mla_attn"))
    y_mla = _matmul(o, p['mla_w_o'], name=n("mla_out"))
    merged = _merge_fwd(gl, p['gate_b'], y_ssd, y_conv, y_mla, name=n("merge"))
    x1 = _matmul(merged, p['w_out'], add=x, name=n("mix_out"))
    hx = _rms_fwd(x1, p['xattn_norm_g'], out_dtype=BF16, name=n("xattn_norm"))
    qx = _matmul(hx, p['xattn_w_q'], name=n("xattn_q"))
    memn = _rms_fwd(mem, p['mem_norm_g'], out_dtype=BF16, name=n("mem_norm"))
    kvx = _matmul(memn, p['xattn_w_kv'], name=n("xattn_kv"))
    m_len = mem.shape[0]
    dh = d // X_HEADS
    kraw = kvx[:, :d].reshape(m_len * X_HEADS, dh)
    kx = _rms_fwd(kraw, p['xattn_k_norm_g'], out_dtype=BF16, name=n("xattn_k_norm")).reshape(m_len, d)
    vx = kvx[:, d:].astype(BF16)
    ox = _xattn_fwd(qx, kx, vx, p['xattn_q_norm_g'], name=n("xattn_core"))
    x2 = _matmul(ox, p['xattn_w_o'], add=x1, name=n("xattn_out"))
    hf = _rms_fwd(x2, p['ffn_norm_g'], out_dtype=BF16, name=n("ffn_norm"))
    h1 = _matmul(hf, p['ffn_w_in'], name=n("ffn_in"))
    act = _swiglu_fwd(h1, name=n("swiglu"))
    x3 = _matmul(act, p['ffn_w_out'], add=x2, name=n("ffn_out"))
    sv.update(u=u, z=z, xbc=xbc, dtr=dtr, dtrt=dtrt, glu=glu, ql=ql, ckv=ckv, krr=krr, gl=gl, pre_s=pre_s,
              act_s=act_s, y_scan=y_scan, prev=prev, yn=yn, y_ssd=y_ssd, pre_c=pre_c, vc=vc, y_conv=y_conv,
              qln=qln, q=q, ckvn=ckvn, kv=kv, qf=qf, kf=kf, v=v, o=o, lse=lse, y_mla=y_mla, merged=merged,
              x1=x1, hx=hx, qx=qx, memn=memn, kraw=kraw, kx=kx, vx=vx, ox=ox, x2=x2, hf=hf, h1=h1, act=act)
    return x3, sv


DW_KEY = dict(ffn_out_dw='ffn_w_out', ffn_in_dw='ffn_w_in', xattn_out_dw='xattn_w_o', xattn_q_dw='xattn_w_q',
              xattn_kv_dw='xattn_w_kv', mix_out_dw='w_out', mla_out_dw='mla_w_o', q_b_dw='mla_w_q_b',
              kv_b_dw='mla_w_kv_b', conv_out_dw='conv_w_out', ssd_out_dw='ssd_w_out', in_dw='w_in')


def _layer_bwd(dx3, mem, cs, sn, p, sv, l, stacks, depth):
    n = lambda s: f"l{l}_b_{s}"
    dw = lambda s: dict(name=n(s), out_dtype=BF16, into=(stacks.get(DW_KEY[s]), l, depth))
    g = {}
    d = dx3.shape[1]
    dact = _matmul(dx3, p['ffn_w_out'], tb=True, name=n("ffn_out_dx"))
    g['ffn_w_out'] = _matmul(sv['act'], dx3, ta=True, **dw("ffn_out_dw"))
    dh1 = _swiglu_bwd(sv['h1'], dact, name=n("swiglu"))
    g['ffn_w_in'] = _matmul(sv['hf'], dh1, ta=True, **dw("ffn_in_dw"))
    dhf = _matmul(dh1, p['ffn_w_in'], tb=True, name=n("ffn_in_dx"))
    dx2, g['ffn_norm_g'] = _rms_bwd(sv['x2'], p['ffn_norm_g'], dhf, dx_dtype=F32, add=dx3, name=n("ffn_norm"))
    dox = _matmul(dx2, p['xattn_w_o'], tb=True, name=n("xattn_out_dx"))
    g['xattn_w_o'] = _matmul(sv['ox'], dx2, ta=True, **dw("xattn_out_dw"))
    dqx, dkx, dvx, g['xattn_q_norm_g'] = _xattn_bwd(sv['qx'], sv['kx'], sv['vx'], p['xattn_q_norm_g'], dox,
                                                    name=n("xattn_core"))
    g['xattn_w_q'] = _matmul(sv['hx'], dqx, ta=True, **dw("xattn_q_dw"))
    dhx = _matmul(dqx, p['xattn_w_q'], tb=True, name=n("xattn_q_dx"))
    dx1, g['xattn_norm_g'] = _rms_bwd(sv['x1'], p['xattn_norm_g'], dhx, dx_dtype=F32, add=dx2, name=n("xattn_norm"))
    m_len = mem.shape[0]
    dh = d // X_HEADS
    dkraw, g['xattn_k_norm_g'] = _rms_bwd(sv['kraw'], p['xattn_k_norm_g'], dkx.reshape(m_len * X_HEADS, dh),
                                          dx_dtype=BF16, name=n("xattn_k_norm"))
    dkvx = jnp.concatenate([dkraw.reshape(m_len, d), dvx.astype(BF16)], axis=1)
    g['xattn_w_kv'] = _matmul(sv['memn'], dkvx, ta=True, **dw("xattn_kv_dw"))
    dmemn = _matmul(dkvx, p['xattn_w_kv'], tb=True, name=n("xattn_kv_dx"))
    _, g['mem_norm_g'] = _rms_bwd(mem, p['mem_norm_g'], dmemn, dx_dtype=BF16, name=n("mem_norm"))
    dmerged = _matmul(dx1, p['w_out'], tb=True, name=n("mix_out_dx"))
    g['w_out'] = _matmul(sv['merged'], dx1, ta=True, **dw("mix_out_dw"))
    dgl, g['gate_b'], dys, dyc, dym = _merge_bwd(sv['gl'], p['gate_b'], sv['y_ssd'], sv['y_conv'], sv['y_mla'],
                                                 dmerged, name=n("merge"))
    do = _matmul(dym, p['mla_w_o'], tb=True, name=n("mla_out_dx"))
    g['mla_w_o'] = _matmul(sv['o'], dym, ta=True, **dw("mla_out_dw"))
    dqf, dkf, dv = _mla_attn_bwd(sv['qf'], sv['kf'], sv['v'], sv['o'], sv['lse'], do, name=n("mla_attn"))
    dq, dkv, dkrr, g['gq'], g['gk'] = _mla_prep_bwd(sv['q'], sv['kv'], sv['krr'], cs, sn, p['gq'], p['gk'],
                                                    dqf, dkf, dv, name=n("mla_prep"))
    g['mla_w_q_b'] = _matmul(sv['qln'], dq, ta=True, **dw("q_b_dw"))
    dqln = _matmul(dq, p['mla_w_q_b'], tb=True, name=n("q_b_dx"))
    dql, g['mla_q_a_g'] = _rms_bwd(sv['ql'], p['mla_q_a_g'], dqln, dx_dtype=BF16, name=n("q_a_norm"))
    g['mla_w_kv_b'] = _matmul(sv['ckvn'], dkv, ta=True, **dw("kv_b_dw"))
    dckvn = _matmul(dkv, p['mla_w_kv_b'], tb=True, name=n("kv_b_dx"))
    dckv, g['mla_kv_a_g'] = _rms_bwd(sv['ckv'], p['mla_kv_a_g'], dckvn, dx_dtype=BF16, name=n("kv_a_norm"))
    dvc = _matmul(dyc, p['conv_w_out'], tb=True, name=n("conv_out_dx"))
    g['conv_w_out'] = _matmul(sv['vc'], dyc, ta=True, **dw("conv_out_dw"))
    dpre_c, g['conv_ln_g'], g['conv_ln_b'] = _ln_silu_bwd(sv['pre_c'], p['conv_ln_g'], p['conv_ln_b'], dvc,
                                                          name=n("conv_ln"))
    da, dg, g['conv_dw_w'], g['conv_dw_b'] = _conv_bwd(sv['glu'], p['conv_dw_w'], p['conv_dw_b'], dpre_c, None,
                                                       glu=True, act=False, name=n("dw_conv"))
    dyn = _matmul(dys, p['ssd_w_out'], tb=True, name=n("ssd_out_dx"))
    g['ssd_w_out'] = _matmul(sv['yn'], dys, ta=True, **dw("ssd_out_dw"))
    dy_scan, dz, g['ssd_norm_g'] = _ssd_gate_bwd(sv['y_scan'], sv['z'], p['ssd_norm_g'], dyn, name=n("ssd_gate"))
    dxs, db, dc, ddtr, g['ssd_a_log'], g['ssd_dt_bias'], g['ssd_d'] = _ssd_scan_bwd(
        sv['act_s'], sv['dtr'], sv['dtrt'], p['ssd_dt_bias'], p['ssd_dt_bias_t'], p['ssd_a_log'], p['ssd_a_log_t'],
        p['ssd_d'], sv['prev'], dy_scan, name=n("ssd_scan"))
    dact_s = jnp.concatenate([dxs, db, dc], axis=1)
    dxbc, g['ssd_conv_w'], g['ssd_conv_b'] = _conv_bwd(sv['xbc'], p['ssd_conv_w'], p['ssd_conv_b'], dact_s,
                                                       sv['pre_s'], glu=False, act=True, name=n("ssd_conv"))
    tail = jnp.zeros((dz.shape[0], IN_WIDTH_PAD - IN_SPLIT['gate'][1]), BF16)
    dproj = jnp.concatenate([dz, dxbc, ddtr, da, dg, dql, dckv, dkrr, dgl, tail], axis=1)
    g['w_in'] = _matmul(sv['u'], dproj, ta=True, **dw("in_dw"))
    du = _matmul(dproj, p['w_in'], tb=True, name=n("in_dx"))
    dx, g['mix_norm_g'] = _rms_bwd(sv['x'], p['mix_norm_g'], du, dx_dtype=F32, add=dx1, name=n("mix_norm"))
    return dx, g


REP_NAMES = ('mix_norm_g', 'ssd_conv_b', 'ssd_dt_bias', 'ssd_a_log', 'ssd_d', 'ssd_norm_g', 'conv_dw_b', 'conv_ln_g',
             'conv_ln_b', 'mla_q_a_g', 'mla_kv_a_g', 'mla_q_norm_g', 'mla_k_norm_g', 'xattn_norm_g', 'mem_norm_g',
             'xattn_q_norm_g', 'xattn_k_norm_g', 'ffn_norm_g')
BIG = (('w_in', 1, 1024, 8912), ('mla_w_q_b', 1, 384, 1536), ('mla_w_kv_b', 1, 256, 2048),
       ('xattn_w_kv', 1, 1024, 2048), ('ffn_w_in', 1, 1024, 5632), ('ssd_w_out', 0, 1024, 1024),
       ('conv_w_out', 0, 1024, 1024), ('mla_w_o', 0, 1024, 1024), ('w_out', 0, 1024, 1024),
       ('xattn_w_q', 0, 1024, 1024), ('xattn_w_o', 0, 1024, 1024), ('ffn_w_out', 0, 2816, 1024))
SMALL = (('ssd_conv_w', 1, 4, 2048), ('conv_dw_w', 1, 31, 1024), ('gate_b', 1, 3, 1024))


def _rope_tables(positions):
    half = MLA_ROPE // 2
    inv = ROPE_THETA ** (-jnp.arange(0, MLA_ROPE, 2, dtype=F32) / MLA_ROPE)
    ang = positions.astype(F32)[:, None] * inv
    cos, sin = jnp.cos(ang), jnp.sin(ang)
    z = jnp.zeros((positions.shape[0], LANE - 2 * half), F32)
    return jnp.concatenate([cos, cos, z], axis=1), jnp.concatenate([-sin, sin, z], axis=1)


def _local_step(x, mem, positions, target, layer_weights, rep, on_layer=None):
    depth = rep['mix_norm_g'].shape[0]
    cs, sn = _rope_tables(positions)
    params, saved = [], []
    h = x
    for l in range(depth):
        p = _layer_params(layer_weights(l, h), rep, l)
        h, sv = _layer_fwd(h, mem, cs, sn, p, l)
        params.append(p)
        saved.append(sv)
    dh, loss = _loss(h, target, name="loss")
    layer_grads = [None] * depth
    stacks = {}
    for l in reversed(range(depth)):
        dh, layer_grads[l] = _layer_bwd(dh, mem, cs, sn, params[l], saved[l], l, stacks, depth)
        stacks = {k: layer_grads[l][k] for k in DW_KEY.values()}
        if on_layer is not None:
            order = on_layer(l, stacks)
            if order is not None:
                dh = dh + order
    stack = lambda k: jnp.stack([layer_grads[l][k] for l in range(depth)])
    gfull = {k: stack(k) for k, _, _, _ in SMALL}
    gfull.update(stacks)
    gfull['w_in'] = _w_in_unpad(gfull['w_in'])
    gfull['mla_w_q_b'] = _qb_unpad(gfull['mla_w_q_b'])
    gfull['gate_b'] = gfull['gate_b'].reshape(depth, 3, -1)
    grep = {}
    for k in REP_NAMES:
        if k == 'mla_q_norm_g':
            grep[k] = stack('gq')[:, 0, :MLA_NOPE + MLA_ROPE]
        elif k == 'mla_k_norm_g':
            grep[k] = stack('gk')[:, 0, :MLA_NOPE + MLA_ROPE]
        elif k in ('ssd_dt_bias', 'ssd_a_log', 'ssd_d'):
            grep[k] = stack(k)[:, 0, :SSD_HEADS]
        else:
            grep[k] = stack(k)[:, 0, :]
    return loss, dh, gfull, grep


N_CHIPS = 4
HBM_SPEC = pl.BlockSpec(memory_space=pl.ANY)
ROW, COL, STK, REP = "row", "col", "stk", "rep"


def _kind(axis, cs):
    if axis == 0:
        return ROW
    return COL if cs % LANE == 0 else STK


def _mesh_pos():
    return lax.axis_index("x"), lax.axis_index("y"), lax.axis_index("c")


def _chip_view(ref, kind, j, a, b, layers=None):
    lsel = slice(None) if layers is None else pl.ds(layers[0], layers[1])
    if kind == STK:
        return ref.at[j, lsel]
    if kind == ROW:
        return ref.at[lsel, pl.ds(pl.multiple_of(j * a, 8), a), :]
    if kind == COL:
        return ref.at[lsel, :, pl.ds(pl.multiple_of(j * b, LANE), b)]
    return ref.at[lsel]


def _all_gather(shards, kinds, *, name):
    n = len(shards)
    depth = shards[0].shape[0]
    lh = depth // 2

    def out_shape(w, kind):
        _, a, b = w.shape
        full = {ROW: (depth, N_CHIPS * a, b), COL: (depth, a, N_CHIPS * b), STK: (N_CHIPS, depth, a, b)}[kind]
        return jax.ShapeDtypeStruct(full, w.dtype)

    def body(*refs):
        w_refs, out_refs = refs[:n], refs[n:2 * n]
        send_sems, recv_sems = refs[2 * n:]
        x, y, cc = _mesh_pos()
        me = (x, y, cc)
        sibling = (x, y, 1 - cc)
        chips = [(1 - x, y), (x, 1 - y), (1 - x, 1 - y)]
        slot = lambda chip: 2 * chip[0] + chip[1]

        def part(i, chip, hc):
            _, a, b = w_refs[i].shape
            return _chip_view(out_refs[i], kinds[i], slot(chip), a, b, (hc * lh, lh))

        def own(i):
            _, a, b = w_refs[i].shape
            return _chip_view(out_refs[i], kinds[i], slot((x, y)), a, b)

        def copy(i, k, src, dst, to):
            return pltpu.make_async_remote_copy(src_ref=src, dst_ref=dst, send_sem=send_sems.at[i, k],
                                                recv_sem=recv_sems.at[i, k], device_id=to, device_id_type=MESH)

        sends = []
        for i in range(n):
            my_half = w_refs[i].at[pl.ds(cc * lh, lh)]
            for k, chip in enumerate(chips):
                sends.append(copy(i, k, my_half, part(i, (x, y), cc), (*chip, cc)))
            sends.append(copy(i, 6, w_refs[i], own(i), sibling))
        for cp in sends:
            cp.start()
        for k, chip in enumerate(chips):
            for i in range(n):
                copy(i, k, part(i, chip, cc), part(i, chip, cc), me).wait_recv()
                cp = copy(i, 3 + k, part(i, chip, cc), part(i, chip, cc), sibling)
                cp.start()
                sends.append(cp)
        for i in range(n):
            for k, chip in enumerate(chips):
                copy(i, 3 + k, part(i, chip, 1 - cc), part(i, chip, 1 - cc), me).wait_recv()
            copy(i, 6, own(i), own(i), me).wait_recv()
        for cp in sends:
            cp.wait_send()

    return pl.pallas_call(
        body, name=name, in_specs=[HBM_SPEC] * n, out_specs=[HBM_SPEC] * n,
        out_shape=[out_shape(w, kd) for w, kd in zip(shards, kinds)],
        scratch_shapes=[pltpu.SemaphoreType.DMA((n, 7)), pltpu.SemaphoreType.DMA((n, 7))],
    )(*shards)


SEM_SPEC = pl.BlockSpec(memory_space=pltpu.SEMAPHORE)
SPLIT_COPY = pltpu.CompilerParams(has_side_effects=pltpu.SideEffectType.DATAFLOW_SIDE_EFFECTING)


def _land_view(ref, kind, j, a, b, rows=None):
    r0, nr = (0, a) if rows is None else rows
    if kind == STK:
        return ref.at[j, pl.ds(r0, nr), :]
    if kind == ROW:
        return ref.at[pl.ds(pl.multiple_of(j * a + r0, 8), nr), :]
    return ref.at[pl.ds(r0, nr), pl.ds(pl.multiple_of(j * b, LANE), b)]


def _layer_copies(w_refs, land_refs, kinds, layer, send_sems, recv_sems):
    x, y, cc = _mesh_pos()
    place = 2 * x + y
    chips = [(1 - x, y), (x, 1 - y), (1 - x, 1 - y)]
    out, inc = [], []
    for i, (w_ref, land, kind) in enumerate(zip(w_refs, land_refs, kinds)):
        _, a, b = w_ref.shape
        half = (cc * (a // 2), a // 2)
        desc = lambda k, src, dst, to: pltpu.make_async_remote_copy(
            src_ref=src, dst_ref=dst, send_sem=send_sems.at[4 * i + k], recv_sem=recv_sems.at[4 * i + k], device_id=to,
            device_id_type=MESH)
        for k, chip in enumerate(chips):
            out.append(desc(k, w_ref.at[layer, pl.ds(half[0], half[1]), :], _land_view(land, kind, place, a, b, half),
                            (*chip, cc)))
            theirs = _land_view(land, kind, 2 * chip[0] + chip[1], a, b, half)
            inc.append(desc(k, theirs, theirs, (*chip, cc)))
        out.append(desc(3, w_ref.at[layer], _land_view(land, kind, place, a, b), (x, y, 1 - cc)))
        own = _land_view(land, kind, place, a, b)
        inc.append(desc(3, own, own, (x, y, 1 - cc)))
    return out, inc


def _land_shape(w, kind):
    _, a, b = w.shape
    return {ROW: (N_CHIPS * a, b), COL: (a, N_CHIPS * b), STK: (N_CHIPS, a, b)}[kind]


def _ag_start(shards, kinds, layer, after, *, name):
    n = len(shards)
    lands = [pltpu.with_memory_space_constraint(lax.empty(_land_shape(w, kd), w.dtype), pltpu.HBM)
             for w, kd in zip(shards, kinds)]

    def body(*refs):
        w_refs = refs[:n]
        send_sems, recv_sems = refs[2 * n + 1], refs[2 * n + 2]
        land_refs = refs[2 * n + 3:3 * n + 3]
        token = refs[3 * n + 3]
        out, _ = _layer_copies(w_refs, land_refs, kinds, layer, send_sems, recv_sems)
        for cp in out:
            cp.start()
        token[...] = jnp.zeros_like(token)

    after = jnp.zeros((8, LANE), F32) if after is None else after
    res = pl.pallas_call(
        body, name=name,
        in_specs=[HBM_SPEC] * (2 * n + 1),
        out_specs=[SEM_SPEC, SEM_SPEC] + [HBM_SPEC] * n + [pl.BlockSpec(memory_space=pltpu.VMEM)],
        out_shape=[pltpu.SemaphoreType.DMA((4 * n,)), pltpu.SemaphoreType.DMA((4 * n,))]
        + [jax.ShapeDtypeStruct(ld.shape, ld.dtype) for ld in lands] + [jax.ShapeDtypeStruct((8, LANE), F32)],
        input_output_aliases={n + i: 2 + i for i in range(n)},
        compiler_params=SPLIT_COPY,
    )(*shards, *lands, after)
    return res[0], res[1], list(res[2:2 + n]), res[2 + n]


def _ag_wait(send_sems, recv_sems, shards, lands, kinds, layer, after, *, name):
    n = len(shards)

    def body(*refs):
        w_refs, land_in = refs[:n], refs[n:2 * n]
        send_sems, recv_sems = refs[2 * n], refs[2 * n + 1]
        out, inc = _layer_copies(w_refs, land_in, kinds, layer, send_sems, recv_sems)
        for cp in out:
            cp.wait_send()
        for cp in inc:
            cp.wait_recv()

    return list(pl.pallas_call(
        body, name=name,
        in_specs=[HBM_SPEC] * (2 * n) + [SEM_SPEC, SEM_SPEC, HBM_SPEC],
        out_specs=[HBM_SPEC] * n,
        out_shape=[jax.ShapeDtypeStruct(ld.shape, ld.dtype) for ld in lands],
        input_output_aliases={n + i: i for i in range(n)},
        compiler_params=SPLIT_COPY,
    )(*shards, *lands, send_sems, recv_sems, after))


def _ag_forward(lands, kinds, shard_shapes, *, name):
    n = len(lands)

    def body(*refs):
        land_refs = refs[n:2 * n]
        send_sems, recv_sems = refs[2 * n:]
        x, y, cc = _mesh_pos()
        chips = [(1 - x, y), (x, 1 - y), (1 - x, 1 - y)]
        sends, waits = [], []
        for i, (land, kind) in enumerate(zip(land_refs, kinds)):
            a, b = shard_shapes[i]
            for k, chip in enumerate(chips):
                j = 2 * chip[0] + chip[1]
                desc = lambda view: pltpu.make_async_remote_copy(
                    src_ref=view, dst_ref=view, send_sem=send_sems.at[i, k], recv_sem=recv_sems.at[i, k],
                    device_id=(x, y, 1 - cc), device_id_type=MESH)
                sends.append(desc(_land_view(land, kind, j, a, b, (cc * (a // 2), a // 2))))
                waits.append(desc(_land_view(land, kind, j, a, b, ((1 - cc) * (a // 2), a // 2))))
        for cp in sends:
            cp.start()
        for cp in waits:
            cp.wait_recv()
        for cp in sends:
            cp.wait_send()

    return list(pl.pallas_call(
        body, name=name, in_specs=[HBM_SPEC] * n, out_specs=[HBM_SPEC] * n,
        out_shape=[jax.ShapeDtypeStruct(ld.shape, ld.dtype) for ld in lands],
        input_output_aliases={i: i for i in range(n)},
        scratch_shapes=[pltpu.SemaphoreType.DMA((n, 3)), pltpu.SemaphoreType.DMA((n, 3))],
    )(*lands))


def _layers_half(ref, kind, start, lh):
    return ref.at[:, pl.ds(start, lh)] if kind == STK else ref.at[pl.ds(start, lh)]


def _rs_pair(gs, kinds, *, name):
    n = len(gs)

    def half_shape(g, kind):
        s = list(g.shape)
        s[1 if kind == STK else 0] //= 2
        return jax.ShapeDtypeStruct(tuple(s), g.dtype)

    def body(*refs):
        g_refs, buf_refs = refs[:n], refs[n:2 * n]
        send_sems, recv_sems = refs[2 * n:]
        x, y, cc = _mesh_pos()
        cps = []
        for i in range(n):
            lh = buf_refs[i].shape[1 if kinds[i] == STK else 0]
            cp = pltpu.make_async_remote_copy(src_ref=_layers_half(g_refs[i], kinds[i], (1 - cc) * lh, lh),
                                              dst_ref=buf_refs[i], send_sem=send_sems.at[i], recv_sem=recv_sems.at[i],
                                              device_id=(x, y, 1 - cc), device_id_type=MESH)
            cp.start()
            cps.append(cp)
        for cp in cps:
            cp.wait()

    return pl.pallas_call(
        body, name=name, in_specs=[HBM_SPEC] * n, out_specs=[HBM_SPEC] * n,
        out_shape=[half_shape(g, kd) for g, kd in zip(gs, kinds)],
        scratch_shapes=[pltpu.SemaphoreType.DMA((n,)), pltpu.SemaphoreType.DMA((n,))],
    )(*gs)


def _row_tile(rows, cols):
    return _pick(rows, max(8, (512 * 1024 // cols) // 8 * 8), 8)


def _rs_pair_add(g, buf, kind, cc, out_dtype, *, name):
    cols = g.shape[-1]
    pre = g.shape[0] if kind == STK else 1
    rows = buf.size // (pre * cols)
    tr = _row_tile(rows, cols)

    def body(cc_ref, g_ref, b_ref, o_ref):
        o_ref[...] = (g_ref[...].astype(F32) + b_ref[...].astype(F32)).astype(o_ref.dtype)

    out = pl.pallas_call(
        body, name=name,
        grid_spec=pltpu.PrefetchScalarGridSpec(
            num_scalar_prefetch=1, grid=(pre, rows // tr),
            in_specs=[pl.BlockSpec((None, None, tr, cols), lambda s, i, cc_ref: (s, cc_ref[0], i, 0)),
                      pl.BlockSpec((None, tr, cols), lambda s, i, cc_ref: (s, i, 0))],
            out_specs=pl.BlockSpec((None, tr, cols), lambda s, i, cc_ref: (s, i, 0))),
        out_shape=jax.ShapeDtypeStruct((pre, rows, cols), out_dtype),
        compiler_params=_cparams(("parallel", "parallel")),
    )(cc.reshape(1).astype(jnp.int32), g.reshape(pre, 2, rows, cols), buf.reshape(pre, rows, cols))
    return out.reshape(buf.shape)


def _rs_cross(ps, kinds, shard_shapes, *, name):
    n = len(ps)

    def body(*refs):
        p_refs, out_refs = refs[:n], refs[n:2 * n]
        send_sems, recv_sems, local_sems = refs[2 * n:]
        x, y, cc = _mesh_pos()
        chips = [(1 - x, y), (x, 1 - y), (1 - x, 1 - y)]
        slot = lambda chip: 2 * chip[0] + chip[1]
        local, sends = [], []
        for i in range(n):
            a, b = shard_shapes[i]
            if kinds[i] == REP:
                cp = pltpu.make_async_copy(p_refs[i], out_refs[i].at[slot((x, y))], local_sems.at[i])
                cp.start()
                local.append(cp)
            for k, chip in enumerate(chips):
                cp = pltpu.make_async_remote_copy(src_ref=_chip_view(p_refs[i], kinds[i], slot(chip), a, b),
                                                  dst_ref=out_refs[i].at[slot((x, y))], send_sem=send_sems.at[i, k],
                                                  recv_sem=recv_sems.at[i, k], device_id=(*chip, cc),
                                                  device_id_type=MESH)
                cp.start()
                sends.append(cp)
        for i in range(n):
            for k, chip in enumerate(chips):
                landed = out_refs[i].at[slot(chip)]
                pltpu.make_async_remote_copy(src_ref=landed, dst_ref=landed, send_sem=send_sems.at[i, k],
                                             recv_sem=recv_sems.at[i, k], device_id=(*chip, cc),
                                             device_id_type=MESH).wait_recv()
        for cp in sends:
            cp.wait_send()
        for cp in local:
            cp.wait()

    def out_shape(p, kind, ab):
        lh = p.shape[1 if kind == STK else 0]
        return jax.ShapeDtypeStruct((N_CHIPS, lh) + tuple(ab), p.dtype)

    return pl.pallas_call(
        body, name=name, in_specs=[HBM_SPEC] * n, out_specs=[HBM_SPEC] * n,
        out_shape=[out_shape(p, kd, ab) for p, kd, ab in zip(ps, kinds, shard_shapes)],
        scratch_shapes=[pltpu.SemaphoreType.DMA((n, 3)), pltpu.SemaphoreType.DMA((n, 3)),
                        pltpu.SemaphoreType.DMA((n,))],
    )(*ps)


def _cross_copies(p_refs, land_refs, kinds, shard_shapes, send_sems, recv_sems):
    x, y, cc = _mesh_pos()
    chips = [(1 - x, y), (x, 1 - y), (1 - x, 1 - y)]
    mine = 2 * x + y
    out, inc = [], []
    for i, (p_ref, land, kind) in enumerate(zip(p_refs, land_refs, kinds)):
        a, b = shard_shapes[i]
        for k, chip in enumerate(chips):
            j = 2 * chip[0] + chip[1]
            desc = lambda src, dst: pltpu.make_async_remote_copy(
                src_ref=src, dst_ref=dst, send_sem=send_sems.at[3 * i + k], recv_sem=recv_sems.at[3 * i + k],
                device_id=(*chip, cc), device_id_type=MESH)
            out.append(desc(_chip_view(p_ref, kind, j, a, b), land.at[mine]))
            inc.append(desc(land.at[j], land.at[j]))
    return out, inc


def _rs_cross_start(ps, kinds, shard_shapes, *, name):
    n = len(ps)
    lands = []
    for p, kd, ab in zip(ps, kinds, shard_shapes):
        lh = p.shape[1 if kd == STK else 0]
        lands.append(pltpu.with_memory_space_constraint(lax.empty((N_CHIPS, lh) + tuple(ab), p.dtype), pltpu.HBM))

    def body(*refs):
        p_refs = refs[:n]
        send_sems, recv_sems = refs[2 * n], refs[2 * n + 1]
        land_refs = refs[2 * n + 2:3 * n + 2]
        token = refs[3 * n + 2]
        out, _ = _cross_copies(p_refs, land_refs, kinds, shard_shapes, send_sems, recv_sems)
        for cp in out:
            cp.start()
        token[...] = jnp.zeros_like(token)

    res = pl.pallas_call(
        body, name=name, in_specs=[HBM_SPEC] * (2 * n),
        out_specs=[SEM_SPEC, SEM_SPEC] + [HBM_SPEC] * n + [pl.BlockSpec(memory_space=pltpu.VMEM)],
        out_shape=[pltpu.SemaphoreType.DMA((3 * n,)), pltpu.SemaphoreType.DMA((3 * n,))]
        + [jax.ShapeDtypeStruct(ld.shape, ld.dtype) for ld in lands] + [jax.ShapeDtypeStruct((8, LANE), F32)],
        input_output_aliases={n + i: 2 + i for i in range(n)}, compiler_params=SPLIT_COPY,
    )(*ps, *lands)
    return res[0], res[1], list(res[2:2 + n]), res[2 + n]


def _rs_cross_wait(send_sems, recv_sems, ps, lands, kinds, shard_shapes, after, *, name):
    n = len(ps)

    def body(*refs):
        p_refs, land_in = refs[:n], refs[n:2 * n]
        out, inc = _cross_copies(p_refs, land_in, kinds, shard_shapes, refs[2 * n], refs[2 * n + 1])
        for cp in out:
            cp.wait_send()
        for cp in inc:
            cp.wait_recv()

    return list(pl.pallas_call(
        body, name=name, in_specs=[HBM_SPEC] * (2 * n) + [SEM_SPEC, SEM_SPEC, HBM_SPEC], out_specs=[HBM_SPEC] * n,
        out_shape=[jax.ShapeDtypeStruct(ld.shape, ld.dtype) for ld in lands],
        input_output_aliases={n + i: i for i in range(n)}, compiler_params=SPLIT_COPY,
    )(*ps, *lands, send_sems, recv_sems, after))


def _rs_sum(p, landed, kind, ab, place, cc, *, name):
    a, b = ab
    lh = landed.shape[1]
    tr = _row_tile(a, b)
    blk = lambda f: pl.BlockSpec((None, None, tr, b), f)
    if kind == ROW:
        p_spec = pl.BlockSpec((None, tr, b), lambda l, i, w, c: (l, w[0] * (a // tr) + i, 0))
    elif kind == COL:
        p_spec = pl.BlockSpec((None, tr, b), lambda l, i, w, c: (l, i, w[0]))
    elif kind == STK:
        p_spec = blk(lambda l, i, w, c: (w[0], l, i, 0))
    else:
        p_spec = blk(lambda l, i, w, c: (0, l, i, 0))
        p = landed
    if kind == REP:
        others = [blk(lambda l, i, w, c, k=k: (k, l, i, 0)) for k in (1, 2, 3)]
    else:
        others = [blk(lambda l, i, w, c: (lax.rem(w[0] + 2, 4), l, i, 0)),
                  blk(lambda l, i, w, c: (w[0] + 1 - 2 * lax.rem(w[0], 2), l, i, 0)),
                  blk(lambda l, i, w, c: (3 - w[0], l, i, 0))]

    def body(w_ref, c_ref, p_ref, b1_ref, b2_ref, b3_ref, o_ref):
        f = lambda r: r[...].astype(F32)
        o_ref[...] = ((f(p_ref) + f(b1_ref)) + f(b2_ref)) + f(b3_ref)

    return pl.pallas_call(
        body, name=name,
        grid_spec=pltpu.PrefetchScalarGridSpec(
            num_scalar_prefetch=2, grid=(lh, a // tr), in_specs=[p_spec] + others,
            out_specs=pl.BlockSpec((None, tr, b), lambda l, i, w, c: (c[0] * lh + l, i, 0))),
        out_shape=jax.ShapeDtypeStruct((2 * lh, a, b), F32),
        compiler_params=_cparams(("parallel", "parallel")),
    )(place, cc, p, landed, landed, landed)


def _rs_share(fs, *, name):
    n = len(fs)

    def body(*refs):
        out_refs = refs[n:2 * n]
        send_sems, recv_sems = refs[2 * n:]
        x, y, cc = _mesh_pos()
        cps = []
        for i in range(n):
            lh = out_refs[i].shape[0] // 2
            mine = out_refs[i].at[pl.ds(cc * lh, lh)]
            cp = pltpu.make_async_remote_copy(src_ref=mine, dst_ref=mine, send_sem=send_sems.at[i],
                                              recv_sem=recv_sems.at[i], device_id=(x, y, 1 - cc),
                                              device_id_type=MESH)
            cp.start()
            cps.append(cp)
        for i in range(n):
            lh = out_refs[i].shape[0] // 2
            theirs = out_refs[i].at[pl.ds((1 - cc) * lh, lh)]
            pltpu.make_async_remote_copy(src_ref=theirs, dst_ref=theirs, send_sem=send_sems.at[i],
                                         recv_sem=recv_sems.at[i], device_id=(x, y, 1 - cc),
                                         device_id_type=MESH).wait_recv()
        for cp in cps:
            cp.wait_send()

    return pl.pallas_call(
        body, name=name, in_specs=[HBM_SPEC] * n, out_specs=[HBM_SPEC] * n,
        out_shape=[jax.ShapeDtypeStruct(f.shape, f.dtype) for f in fs],
        input_output_aliases={i: i for i in range(n)},
        scratch_shapes=[pltpu.SemaphoreType.DMA((n,)), pltpu.SemaphoreType.DMA((n,))],
    )(*fs)


def _reduce_scatter(gs, kinds, shard_shapes, wire_dtypes):
    ps = _rs_pair_sums(gs, kinds, wire_dtypes, "")
    landed = _rs_cross(ps, kinds, shard_shapes, name="rs_cross")
    return _rs_finish(ps, landed, kinds, shard_shapes, "")


def _rs_pair_sums(gs, kinds, wire_dtypes, tag):
    cc = lax.axis_index("c")
    bufs = _rs_pair(gs, kinds, name="rs_pair" + tag)
    return [_rs_pair_add(g, buf, kd, cc, wd, name=f"rs_pair_add{tag}_{i}")
            for i, (g, buf, kd, wd) in enumerate(zip(gs, bufs, kinds, wire_dtypes))]


def _rs_finish(ps, landed, kinds, shard_shapes, tag):
    x, y, cc = _mesh_pos()
    place = (2 * x + y).reshape(1).astype(jnp.int32)
    cc1 = cc.reshape(1).astype(jnp.int32)
    fs = [_rs_sum(p, b, kd, ab, place, cc1, name=f"rs_sum{tag}_{i}")
          for i, (p, b, kd, ab) in enumerate(zip(ps, landed, kinds, shard_shapes))]
    return _rs_share(fs, name="rs_share" + tag)


def _reduce_scatter_begin(gs, kinds, shard_shapes, wire_dtypes):
    ps = _rs_pair_sums(gs, kinds, wire_dtypes, "_early")
    send, recv, lands, token = _rs_cross_start(ps, kinds, shard_shapes, name="rs_cross_start")
    return ps, send, recv, lands, token


def _reduce_scatter_end(state, kinds, shard_shapes, after):
    ps, send, recv, lands, _ = state
    landed = _rs_cross_wait(send, recv, ps, lands, kinds, shard_shapes, after, name="rs_cross_wait")
    return _rs_finish(ps, landed, kinds, shard_shapes, "_early")


def _shard_shape(axis, r, c):
    return (r // N_CHIPS, c) if axis == 0 else (r, c // N_CHIPS)


def _unstack(stacked):
    ns, depth, r, cs = stacked.shape
    return stacked.transpose(1, 2, 0, 3).reshape(depth, r, ns * cs)


def _stack(fullw):
    depth, r, c = fullw.shape
    return fullw.reshape(depth, r, N_CHIPS, c // N_CHIPS).transpose(2, 0, 1, 3)


REP_SIZES = dict(mix_norm_g=1024, ssd_conv_b=2048, ssd_dt_bias=16, ssd_a_log=16, ssd_d=16, ssd_norm_g=1024,
                 conv_dw_b=1024, conv_ln_g=1024, conv_ln_b=1024, mla_q_a_g=384, mla_kv_a_g=256, mla_q_norm_g=192,
                 mla_k_norm_g=192, xattn_norm_g=1024, mem_norm_g=1024, xattn_q_norm_g=256, xattn_k_norm_g=256,
                 ffn_norm_g=1024)
REP_ROWS = 8
REP_COLS = -(-sum(REP_SIZES.values()) // (REP_ROWS * LANE)) * LANE
SMALL_COLS = 256
SMALL_ROWS = 48


def _pack_rep(d):
    flat = jnp.concatenate([d[k] for k in REP_NAMES], axis=1)
    flat = jnp.pad(flat, ((0, 0), (0, REP_ROWS * REP_COLS - flat.shape[1])))
    return flat.reshape(flat.shape[0], REP_ROWS, REP_COLS)


def _unpack_rep(packed):
    out, off = {}, 0
    flat = packed.reshape(packed.shape[0], -1)
    for k in REP_NAMES:
        out[k] = flat[:, off:off + REP_SIZES[k]]
        off += REP_SIZES[k]
    return out


def _pack_small(d):
    depth = d['gate_b'].shape[0]
    rows = jnp.concatenate([d[k].reshape(depth, -1, SMALL_COLS) for k, _, _, _ in SMALL], axis=1)
    return jnp.pad(rows, ((0, 0), (0, SMALL_ROWS - rows.shape[1]), (0, 0)))


def _unpack_small(packed):
    out, off = {}, 0
    for k, axis, r, c in SMALL:
        rs, cs = _shard_shape(axis, r, c)
        n = rs * cs // SMALL_COLS
        out[k] = packed[:, off:off + n].reshape(packed.shape[0], rs, cs)
        off += n
    return out


WEIGHT_NAMES = ('mix_norm_g', 'w_in', 'ssd_conv_w', 'ssd_conv_b', 'ssd_dt_bias', 'ssd_a_log', 'ssd_d', 'ssd_norm_g',
                'ssd_w_out', 'conv_dw_w', 'conv_dw_b', 'conv_ln_g', 'conv_ln_b', 'conv_w_out', 'mla_q_a_g',
                'mla_w_q_b', 'mla_kv_a_g', 'mla_w_kv_b', 'mla_q_norm_g', 'mla_k_norm_g', 'mla_w_o', 'gate_b', 'w_out',
                'xattn_norm_g', 'mem_norm_g', 'xattn_w_q', 'xattn_w_kv', 'xattn_q_norm_g', 'xattn_k_norm_g',
                'xattn_w_o', 'ffn_norm_g', 'ffn_w_in', 'ffn_w_out')


def kernel(x, mem, positions, *rest):
    nw = len(WEIGHT_NAMES)
    weights = dict(zip(WEIGHT_NAMES, rest[:nw]))
    target = rest[nw]
    mom_m = dict(zip(WEIGHT_NAMES, rest[nw + 1:2 * nw + 1]))
    mom_v = dict(zip(WEIGHT_NAMES, rest[2 * nw + 1:3 * nw + 1]))
    depth = weights['mix_norm_g'].shape[0]

    kinds = [_kind(axis, _shard_shape(axis, r, c)[1]) for _, axis, r, c in BIG]
    shard_shapes = [_shard_shape(axis, r, c) for _, axis, r, c in BIG]
    nb = len(BIG)
    small_all, = _all_gather([_pack_small(weights)], [STK], name="ag_small")
    small_chips = [_unpack_small(small_all[j]) for j in range(N_CHIPS)]
    small = {k: jnp.concatenate([sc[k] for sc in small_chips], axis=2) for k, _, _, _ in SMALL}
    big_shards = [weights[k].astype(BF16) for k, _, _, _ in BIG]
    send0, recv0, lands0, _ = _ag_start(big_shards, kinds[:nb], 0, None, name="ag_start_0")
    lands0 = _ag_wait(send0, recv0, big_shards, lands0, kinds[:nb], 0, big_shards[0], name="ag_wait_0")
    lands0 = _ag_forward(lands0, kinds[:nb], shard_shapes[:nb], name="ag_forward_0")
    started = {l: _ag_start(big_shards, kinds[:nb], l, lands0[0], name=f"ag_start_{l}") for l in range(1, depth)}
    tokens = sum(st[3][0, 0] for st in started.values())

    def layer_weights(l, h):
        if l == 0:
            lands = lands0
        else:
            send, recv, lands, _ = started[l]
            lands = _ag_wait(send, recv, big_shards, lands, kinds[:nb], l, h, name=f"ag_wait_{l}")
            lands = _ag_forward(lands, kinds[:nb], shard_shapes[:nb], name=f"ag_forward_{l}")
        full = {k: (ld.transpose(1, 0, 2).reshape(ld.shape[1], -1) if kd == STK else ld)
                for (k, _, _, _), kd, ld in zip(BIG, kinds[:nb], lands)}
        full.update({k: w[l] for k, w in small.items()})
        full['w_in'] = _w_in_pad(full['w_in'])
        full['mla_w_q_b'] = _qb_pad(full['mla_w_q_b'])
        return full

    rep = {k: weights[k] for k in REP_NAMES}
    top = depth // 2
    by_chip = lambda g: [(_stack(g[k]) if kd == STK else g[k]) for (k, _, _, _), kd in zip(BIG, kinds)]
    early = []

    def on_layer(l, stacks):
        if l == top:
            g = {k: stacks[k][top:] for k, _, _, _ in BIG}
            g['w_in'] = _w_in_unpad(g['w_in'])
            g['mla_w_q_b'] = _qb_unpad(g['mla_w_q_b'])
            early.append(_reduce_scatter_begin(by_chip(g), kinds, shard_shapes, [BF16] * nb))
            return early[0][4][0, 0]

    loss, dx, gfull, grep = _local_step(x[0] + tokens, mem[0], positions[0], target[0], layer_weights, rep, on_layer)
    loss = lax.psum(loss, ("x", "y", "c"))
    upper = _reduce_scatter_end(early[0], kinds, shard_shapes, dx)

    gs = by_chip({k: gfull[k][:top] for k, _, _, _ in BIG})
    small_split = {k: _stack(gfull[k]) for k, _, _, _ in SMALL}
    gs.append(jnp.stack([_pack_small({k: v[j] for k, v in small_split.items()}) for j in range(N_CHIPS)]))
    gs.append(_pack_rep(grep))
    summed = _reduce_scatter(gs, kinds + [STK, REP], shard_shapes + [(SMALL_ROWS, SMALL_COLS), (REP_ROWS, REP_COLS)],
                             [BF16] * nb + [F32, F32])
    grads = {k: jnp.concatenate([lo, hi]) for (k, _, _, _), lo, hi in zip(BIG, summed[:nb], upper)}
    grads.update(_unpack_small(summed[nb]))
    rep_sum = summed[nb + 1]
    grads.update(_unpack_rep(rep_sum))

    delta, new_m, new_v = {}, {}, {}
    for k, _, _, _ in BIG + SMALL:
        w = weights[k]
        two_d = (w.shape[0] * w.shape[1], w.shape[2])
        d_, m_, v_ = _adamw(w.reshape(two_d), grads[k].reshape(two_d), mom_m[k].reshape(two_d),
                            mom_v[k].reshape(two_d), name="adamw_" + k)
        delta[k], new_m[k], new_v[k] = d_.reshape(w.shape), m_.reshape(w.shape), v_.reshape(w.shape)
    pack2 = lambda d: _pack_rep(d).reshape(depth, -1)
    d_, m_, v_ = _adamw(pack2(rep), rep_sum.reshape(depth, -1), pack2({k: mom_m[k] for k in REP_NAMES}),
                        pack2({k: mom_v[k] for k in REP_NAMES}), name="adamw_rep")
    delta.update(_unpack_rep(d_))
    new_m.update(_unpack_rep(m_))
    new_v.update(_unpack_rep(v_))

    return (loss, dx[None], *[grads[k] for k in WEIGHT_NAMES], *[delta[k] for k in WEIGHT_NAMES],
            *[new_m[k] for k in WEIGHT_NAMES], *[new_v[k] for k in WEIGHT_NAMES])
```

```python
import functools
import math

import jax
import jax.numpy as jnp
import numpy as np
from jax import lax
from jax.experimental import pallas as pl
from jax.experimental.pallas import tpu as pltpu

F32 = jnp.float32
BF16 = jnp.bfloat16
MESH = pl.DeviceIdType.MESH

EPS = 1e-6
CHUNK = 64
SSD_HEADS = 16
SSD_GROUPS = 4
SSD_P = 64
SSD_N = 128
MLA_HEADS = 8
MLA_NOPE = 128
MLA_ROPE = 64
MLA_V = 128
MLA_HP = 256
X_HEADS = 4
ROPE_THETA = 10000.0
ADAM_LR, ADAM_B1, ADAM_B2, ADAM_EPS, ADAM_WD, ADAM_STEP = 0.001, 0.9, 0.999, 1e-08, 0.01, 10
LANE = 128
NEG = -1e30
VMEM_MB = 1024 * 1024


def _pick(n, cap, mult=128):
    if n <= cap:
        return n
    d = (cap // mult) * mult
    while d >= mult:
        if n % d == 0:
            return d
        d -= mult
    return n


def _cparams(sem, mb=40):
    return pltpu.CompilerParams(dimension_semantics=sem, vmem_limit_bytes=mb * VMEM_MB)


def _sigmoid(x):
    return 1.0 / (1.0 + jnp.exp(-x))


def _silu(x):
    return x * _sigmoid(x)


def _dsilu(x):
    s = _sigmoid(x)
    return s * (1.0 + x * (1.0 - s))


def _softplus(x):
    return jnp.maximum(x, 0.0) + jnp.log(1.0 + jnp.exp(-jnp.abs(x)))


def _matmul(a, b, *, ta=False, tb=False, out_dtype=F32, add=None, into=None, name):
    if ta:
        kdim, m = a.shape
    else:
        m, kdim = a.shape
    if tb:
        n, kb = b.shape
    else:
        kb, n = b.shape
    assert kb == kdim, (a.shape, b.shape, ta, tb)
    tm = _pick(m, 512, 128) if ta else _pick(m, 1024, 8)
    tn = _pick(n, 1024 if n <= 1024 else 512, 128)
    tk = _pick(kdim, 2048, 128)
    nk = kdim // tk
    dims = (((0 if ta else 1,), (1 if tb else 0,)), ((), ()))

    has_add = add is not None
    has_stack = into is not None and into[0] is not None

    def body(a_ref, b_ref, *rest):
        add_ref = rest[0] if has_add else None
        o_ref = rest[has_add + has_stack]
        acc = rest[has_add + has_stack + 1:]
        part = lax.dot_general(a_ref[...].astype(BF16), b_ref[...].astype(BF16), dims,
                               preferred_element_type=F32)

        def finish(total):
            if has_add:
                total = total + add_ref[...]
            o_ref[...] = total.astype(o_ref.dtype)

        if nk == 1:
            finish(part)
        else:
            acc_ref, = acc
            k = pl.program_id(2)

            @pl.when(k == 0)
            def _():
                acc_ref[...] = part

            @pl.when(k > 0)
            def _():
                acc_ref[...] += part

            @pl.when(k == nk - 1)
            def _():
                finish(acc_ref[...])

    a_spec = pl.BlockSpec((tk, tm), lambda i, j, k: (k, i)) if ta else pl.BlockSpec((tm, tk), lambda i, j, k: (i, k))
    b_spec = pl.BlockSpec((tn, tk), lambda i, j, k: (j, k)) if tb else pl.BlockSpec((tk, tn), lambda i, j, k: (k, j))
    o_spec = pl.BlockSpec((tm, tn), lambda i, j, k: (i, j))
    operands = [a, b] + ([add] if has_add else [])
    in_specs = [a_spec, b_spec] + ([o_spec] if has_add else [])
    if into is None:
        out_spec, out_shape, aliases = o_spec, jax.ShapeDtypeStruct((m, n), out_dtype), {}
    else:
        stack, layer, depth = into
        out_spec = pl.BlockSpec((None, tm, tn), lambda i, j, k: (layer, i, j))
        out_shape = jax.ShapeDtypeStruct((depth, m, n), out_dtype)
        aliases = {}
        if stack is not None:
            aliases = {len(operands): 0}
            operands.append(stack)
            in_specs.append(HBM_SPEC)
    return pl.pallas_call(
        body, name=name, grid=(m // tm, n // tn, nk),
        in_specs=in_specs, out_specs=out_spec, out_shape=out_shape, input_output_aliases=aliases,
        scratch_shapes=[] if nk == 1 else [pltpu.VMEM((tm, tn), F32)],
        compiler_params=_cparams(("parallel", "parallel", "arbitrary"), 48),
    )(*operands)


def _rms_fwd(x, g, *, out_dtype, name):
    r, w = x.shape
    tr = _pick(r, 512, 8)

    def body(x_ref, g_ref, o_ref):
        xv = x_ref[...]
        rstd = lax.rsqrt(jnp.mean(xv * xv, axis=-1, keepdims=True) + EPS)
        o_ref[...] = (xv * rstd * g_ref[...]).astype(o_ref.dtype)

    return pl.pallas_call(
        body, name=name, grid=(r // tr,),
        in_specs=[pl.BlockSpec((tr, w), lambda i: (i, 0)), pl.BlockSpec((1, w), lambda i: (0, 0))],
        out_specs=pl.BlockSpec((tr, w), lambda i: (i, 0)),
        out_shape=jax.ShapeDtypeStruct((r, w), out_dtype),
        compiler_params=_cparams(("parallel",)),
    )(x, g)


def _rms_bwd(x, g, dy, *, dx_dtype, name, add=None):
    r, w = x.shape
    tr = _pick(r, 512, 8)
    has_add = add is not None

    def body(x_ref, g_ref, dy_ref, *rest):
        if has_add:
            add_ref, dx_ref, dg_ref = rest
        else:
            dx_ref, dg_ref = rest
        xv = x_ref[...]
        dyv = dy_ref[...].astype(F32)
        rstd = lax.rsqrt(jnp.mean(xv * xv, axis=-1, keepdims=True) + EPS)
        xh = xv * rstd
        dyg = dyv * g_ref[...]
        dx = rstd * (dyg - xh * jnp.mean(dyg * xh, axis=-1, keepdims=True))
        if has_add:
            dx = dx + add_ref[...]
        dx_ref[...] = dx.astype(dx_ref.dtype)
        part = jnp.sum(dyv * xh, axis=0, keepdims=True)

        @pl.when(pl.program_id(0) == 0)
        def _():
            dg_ref[...] = part

        @pl.when(pl.program_id(0) > 0)
        def _():
            dg_ref[...] += part

    row = pl.BlockSpec((tr, w), lambda i: (i, 0))
    vec = pl.BlockSpec((1, w), lambda i: (0, 0))
    ins = [x, g, dy] + ([add] if has_add else [])
    return pl.pallas_call(
        body, name=name, grid=(r // tr,),
        in_specs=[row, vec, row] + ([row] if has_add else []),
        out_specs=[row, vec],
        out_shape=[jax.ShapeDtypeStruct((r, w), dx_dtype), jax.ShapeDtypeStruct((1, w), F32)],
        compiler_params=_cparams(("arbitrary",)),
    )(*ins)


def _ln_silu_fwd(x, g, b, *, name):
    r, w = x.shape
    tr = _pick(r, 512, 8)

    def body(x_ref, g_ref, b_ref, o_ref):
        xv = x_ref[...]
        mu = jnp.mean(xv, axis=-1, keepdims=True)
        xc = xv - mu
        rstd = lax.rsqrt(jnp.mean(xc * xc, axis=-1, keepdims=True) + EPS)
        o_ref[...] = _silu(xc * rstd * g_ref[...] + b_ref[...]).astype(o_ref.dtype)

    row = pl.BlockSpec((tr, w), lambda i: (i, 0))
    vec = pl.BlockSpec((1, w), lambda i: (0, 0))
    return pl.pallas_call(
        body, name=name, grid=(r // tr,), in_specs=[row, vec, vec], out_specs=row,
        out_shape=jax.ShapeDtypeStruct((r, w), BF16), compiler_params=_cparams(("parallel",)),
    )(x, g, b)


def _ln_silu_bwd(x, g, b, dy, *, name):
    r, w = x.shape
    tr = _pick(r, 512, 8)

    def body(x_ref, g_ref, b_ref, dy_ref, dx_ref, dg_ref, db_ref):
        xv = x_ref[...]
        mu = jnp.mean(xv, axis=-1, keepdims=True)
        xc = xv - mu
        rstd = lax.rsqrt(jnp.mean(xc * xc, axis=-1, keepdims=True) + EPS)
        xh = xc * rstd
        pre = xh * g_ref[...] + b_ref[...]
        dpre = dy_ref[...].astype(F32) * _dsilu(pre)
        dxh = dpre * g_ref[...]
        dx_ref[...] = rstd * (dxh - jnp.mean(dxh, axis=-1, keepdims=True)
                              - xh * jnp.mean(dxh * xh, axis=-1, keepdims=True))
        pg = jnp.sum(dpre * xh, axis=0, keepdims=True)
        pb = jnp.sum(dpre, axis=0, keepdims=True)

        @pl.when(pl.program_id(0) == 0)
        def _():
            dg_ref[...] = pg
            db_ref[...] = pb

        @pl.when(pl.program_id(0) > 0)
        def _():
            dg_ref[...] += pg
            db_ref[...] += pb

    row = pl.BlockSpec((tr, w), lambda i: (i, 0))
    vec = pl.BlockSpec((1, w), lambda i: (0, 0))
    return pl.pallas_call(
        body, name=name, grid=(r // tr,), in_specs=[row, vec, vec, row], out_specs=[row, vec, vec],
        out_shape=[jax.ShapeDtypeStruct((r, w), F32), jax.ShapeDtypeStruct((1, w), F32),
                   jax.ShapeDtypeStruct((1, w), F32)],
        compiler_params=_cparams(("arbitrary",)),
    )(x, g, b, dy)


CONV_PAD = 32
CONV_T = 256


def _conv_fwd(src, w, b, *, glu, act, name):
    s = src.shape[0]
    k, c = w.shape
    tc = LANE
    ncb = c // tc
    tt = _pick(s, CONV_T, 8)
    assert k - 1 <= CONV_PAD

    def body(*refs):
        if glu:
            a_ref, g_ref, w_ref, b_ref = refs[:4]
            outs = refs[4:-1]
        else:
            a_ref, w_ref, b_ref = refs[:3]
            outs = refs[3:-1]
        xp = refs[-1]
        xp[0:CONV_PAD, :] = jnp.zeros((CONV_PAD, tc), F32)
        if glu:
            xp[CONV_PAD:CONV_PAD + s, :] = a_ref[...] * _sigmoid(g_ref[...])
        else:
            xp[CONV_PAD:CONV_PAD + s, :] = a_ref[...]
        wv = w_ref[...]
        bv = b_ref[...]
        for t0 in range(0, s, tt):
            acc = jnp.zeros((tt, tc), F32) + bv
            for kk in range(k):
                off = CONV_PAD + t0 - (k - 1) + kk
                acc = acc + wv[kk:kk + 1, :] * xp[off:off + tt, :]
            outs[0][t0:t0 + tt, :] = acc
            if act:
                outs[1][t0:t0 + tt, :] = _silu(acc)

    col = pl.BlockSpec((s, tc), lambda j: (0, j))
    in_specs = [col, pl.BlockSpec((s, tc), lambda j: (0, j + ncb))] if glu else [col]
    in_specs += [pl.BlockSpec((k, tc), lambda j: (0, j)), pl.BlockSpec((1, tc), lambda j: (0, j))]
    n_out = 2 if act else 1
    res = pl.pallas_call(
        body, name=name, grid=(ncb,), in_specs=in_specs,
        out_specs=[col] * n_out,
        out_shape=[jax.ShapeDtypeStruct((s, c), F32)] * n_out,
        scratch_shapes=[pltpu.VMEM((CONV_PAD + s, tc), F32)],
        compiler_params=_cparams(("parallel",), 48),
    )(*([src, src] if glu else [src]), w, b)
    return res


def _conv_bwd(src, w, b, dy, pre, *, glu, act, name):
    s = src.shape[0]
    k, c = w.shape
    tc = LANE
    ncb = c // tc
    tt = _pick(s, CONV_T, 8)

    def body(*refs):
        i = 0
        a_ref = refs[i]; i += 1
        if glu:
            g_ref = refs[i]; i += 1
        w_ref = refs[i]; i += 1
        dy_ref = refs[i]; i += 1
        if act:
            pre_ref = refs[i]; i += 1
        da_ref = refs[i]; i += 1
        if glu:
            dg_ref = refs[i]; i += 1
        dw_ref = refs[i]; db_ref = refs[i + 1]
        xp, dp = refs[-2], refs[-1]
        xp[0:CONV_PAD, :] = jnp.zeros((CONV_PAD, tc), F32)
        if glu:
            xp[CONV_PAD:CONV_PAD + s, :] = a_ref[...] * _sigmoid(g_ref[...])
        else:
            xp[CONV_PAD:CONV_PAD + s, :] = a_ref[...]
        dp[s:s + CONV_PAD, :] = jnp.zeros((CONV_PAD, tc), F32)
        if act:
            dp[0:s, :] = dy_ref[...].astype(F32) * _dsilu(pre_ref[...])
        else:
            dp[0:s, :] = dy_ref[...].astype(F32)
        wv = w_ref[...]
        dws = [jnp.zeros((1, tc), F32) for _ in range(k)]
        dbs = jnp.zeros((1, tc), F32)
        for t0 in range(0, s, tt):
            acc = jnp.zeros((tt, tc), F32)
            dcur = dp[t0:t0 + tt, :]
            dbs = dbs + jnp.sum(dcur, axis=0, keepdims=True)
            for kk in range(k):
                acc = acc + wv[kk:kk + 1, :] * dp[t0 + (k - 1) - kk:t0 + (k - 1) - kk + tt, :]
                off = CONV_PAD + t0 - (k - 1) + kk
                dws[kk] = dws[kk] + jnp.sum(dcur * xp[off:off + tt, :], axis=0, keepdims=True)
            if glu:
                av = a_ref[t0:t0 + tt, :]
                sg = _sigmoid(g_ref[t0:t0 + tt, :])
                da_ref[t0:t0 + tt, :] = (acc * sg).astype(da_ref.dtype)
                dg_ref[t0:t0 + tt, :] = (acc * av * sg * (1.0 - sg)).astype(dg_ref.dtype)
            else:
                da_ref[t0:t0 + tt, :] = acc.astype(da_ref.dtype)
        for kk in range(k):
            dw_ref[kk:kk + 1, :] = dws[kk]
        db_ref[...] = dbs

    col = pl.BlockSpec((s, tc), lambda j: (0, j))
    in_specs = [col] + ([pl.BlockSpec((s, tc), lambda j: (0, j + ncb))] if glu else [])
    in_specs += [pl.BlockSpec((k, tc), lambda j: (0, j)), col] + ([col] if act else [])
    ins = ([src, src] if glu else [src]) + [w, dy] + ([pre] if act else [])
    out_specs = [col] + ([col] if glu else []) + [pl.BlockSpec((k, tc), lambda j: (0, j)),
                                                  pl.BlockSpec((1, tc), lambda j: (0, j))]
    out_shape = [jax.ShapeDtypeStruct((s, c), BF16)] * (2 if glu else 1) + [
        jax.ShapeDtypeStruct((k, c), F32), jax.ShapeDtypeStruct((1, c), F32)]
    return pl.pallas_call(
        body, name=name, grid=(ncb,), in_specs=in_specs, out_specs=out_specs, out_shape=out_shape,
        scratch_shapes=[pltpu.VMEM((CONV_PAD + s, tc), F32), pltpu.VMEM((s + CONV_PAD, tc), F32)],
        compiler_params=_cparams(("parallel",), 56),
    )(*ins)


def _ssd_consts():
    e = np.zeros((LANE, SSD_HEADS * SSD_P), np.float32)
    for h in range(SSD_HEADS):
        e[h, h * SSD_P:(h + 1) * SSD_P] = 1.0
    ltri = np.tril(np.ones((CHUNK, CHUNK), np.float32))
    return jnp.asarray(e), jnp.asarray(e.T.copy()), jnp.asarray(ltri), jnp.asarray(ltri.T.copy())


def _split3(x):
    hi = x.astype(BF16)
    r = x - hi.astype(F32)
    mid = r.astype(BF16)
    lo = (r - mid.astype(F32)).astype(BF16)
    return hi, mid, lo


def _dot_sel(x, sel, x_left=True):
    sb = sel.astype(BF16)
    out = None
    for part in _split3(x):
        t = jnp.dot(part, sb, preferred_element_type=F32) if x_left else jnp.dot(sb, part, preferred_element_type=F32)
        out = t if out is None else out + t
    return out


def _ssd_chunk_terms(dtr_ref, dtrt_ref, bias_ref, biast_ref, alog_ref, alogt_ref, e_ref, ltri_ref, utri_ref):
    a_neg = -jnp.exp(alog_ref[...])
    dt = _softplus(dtr_ref[...] + bias_ref[...])
    a = dt * a_neg
    s = _dot_sel(a, ltri_ref[...], x_left=False)
    dtt = _softplus(dtrt_ref[...] + biast_ref[...])
    st = _dot_sel(dtt * (-jnp.exp(alogt_ref[...])), utri_ref[...])
    ev = e_ref[...]
    s_x = _dot_sel(s, ev)
    dt_x = _dot_sel(dt, ev)
    return a_neg, dt, s, st, s_x, dt_x


def _ssd_decay(s, st, h, tril):
    seg = s[:, h:h + 1] - st[h:h + 1, :]
    return jnp.exp(jnp.where(tril, seg, NEG))


def _ssd_decay_t(s, st, h, triu):
    seg = st[h:h + 1, :] - s[:, h:h + 1]
    return jnp.exp(jnp.where(triu, seg, NEG))


def _head_masks():
    lane = lax.broadcasted_iota(jnp.int32, (1, SSD_P * 4), 1)
    return [((lane >= r * SSD_P) & (lane < (r + 1) * SSD_P)).astype(F32) for r in range(4)]


def _ssd_scan_fwd(xbc, dtr, dtrt, bias, biast, alog, alogt, dskip, *, name):
    s_len = xbc.shape[0]
    nc = s_len // CHUNK
    e, et, ltri, utri = _ssd_consts()
    gw = SSD_P * 4

    def body(xs_ref, b_ref, c_ref, dtr_ref, dtrt_ref, bias_ref, biast_ref, alog_ref, alogt_ref, d_ref,
             e_ref, ltri_ref, utri_ref, y_ref, prev_ref, state):
        @pl.when(pl.program_id(0) == 0)
        def _():
            state[...] = jnp.zeros_like(state)

        a_neg, dt, s, st, s_x, dt_x = _ssd_chunk_terms(dtr_ref, dtrt_ref, bias_ref, biast_ref, alog_ref,
                                                       alogt_ref, e_ref, ltri_ref, utri_ref)
        s_last = s_x[CHUNK - 1:CHUNK, :]
        es_x = jnp.exp(s_x)
        w_x = jnp.exp(s_last - s_x)
        cd_x = jnp.exp(s_last)
        d_x = _dot_sel(jnp.broadcast_to(d_ref[...], (8, LANE)), e_ref[...])[0:1, :]
        xs = xs_ref[...]
        xv = xs * dt_x
        row = lax.broadcasted_iota(jnp.int32, (CHUNK, CHUNK), 0)
        colm = lax.broadcasted_iota(jnp.int32, (CHUNK, CHUNK), 1)
        tril = colm <= row
        masks = _head_masks()
        for g in range(SSD_GROUPS):
            gs = slice(g * gw, (g + 1) * gw)
            bg = b_ref[:, g * SSD_N:(g + 1) * SSD_N].astype(BF16)
            cg = c_ref[:, g * SSD_N:(g + 1) * SSD_N].astype(BF16)
            xg = xv[:, gs]
            hg = state[g]
            prev_ref[g] = hg
            cb = lax.dot_general(cg, bg, (((1,), (1,)), ((), ())), preferred_element_type=F32)
            yg = jnp.dot(cg, hg.astype(BF16), preferred_element_type=F32) * es_x[:, gs]
            for r in range(4):
                m = (cb * _ssd_decay(s, st, g * 4 + r, tril)).astype(BF16)
                yg = yg + jnp.dot(m, (xg * masks[r]).astype(BF16), preferred_element_type=F32)
            y_ref[:, gs] = yg + d_x[:, gs] * xs[:, gs]
            upd = lax.dot_general(bg, (xg * w_x[:, gs]).astype(BF16), (((0,), (0,)), ((), ())),
                                  preferred_element_type=F32)
            state[g] = hg * cd_x[:, gs] + upd

    nh = LANE
    chunk_row = lambda w, cb: pl.BlockSpec((CHUNK, w), lambda i, cb=cb: (i, cb))
    full = lambda a: pl.BlockSpec(a.shape, lambda i: (0,) * a.ndim)
    in_specs = [chunk_row(1024, 0), chunk_row(512, 2), chunk_row(512, 3), chunk_row(nh, 0),
                pl.BlockSpec((None, nh, CHUNK), lambda i: (i, 0, 0)),
                full(bias), full(biast), full(alog), full(alogt), full(dskip), full(e), full(ltri), full(utri)]
    return pl.pallas_call(
        body, name=name, grid=(nc,), in_specs=in_specs,
        out_specs=[pl.BlockSpec((CHUNK, 1024), lambda i: (i, 0)),
                   pl.BlockSpec((None, SSD_GROUPS, SSD_N, gw), lambda i: (i, 0, 0, 0))],
        out_shape=[jax.ShapeDtypeStruct((s_len, 1024), F32),
                   jax.ShapeDtypeStruct((nc, SSD_GROUPS, SSD_N, gw), F32)],
        scratch_shapes=[pltpu.VMEM((SSD_GROUPS, SSD_N, gw), F32)],
        compiler_params=_cparams(("arbitrary",)),
    )(xbc, xbc, xbc, dtr, dtrt, bias, biast, alog, alogt, dskip, e, ltri, utri)


def _ssd_scan_bwd(xbc, dtr, dtrt, bias, biast, alog, alogt, dskip, prev, dy, *, name):
    s_len = xbc.shape[0]
    nc = s_len // CHUNK
    e, et, ltri, utri = _ssd_consts()
    gw = SSD_P * 4

    def body(xs_ref, b_ref, c_ref, dtr_ref, dtrt_ref, bias_ref, biast_ref, alog_ref, alogt_ref, d_ref,
             e_ref, et_ref, ltri_ref, utri_ref, prev_ref, dy_ref,
             dxs_ref, db_ref, dc_ref, ddtr_ref, dalog_ref, dbias_ref, dd_ref, dstate):
        step = pl.program_id(0)

        @pl.when(step == 0)
        def _():
            dstate[...] = jnp.zeros_like(dstate)

        a_neg, dt, s, st, s_x, dt_x = _ssd_chunk_terms(dtr_ref, dtrt_ref, bias_ref, biast_ref, alog_ref,
                                                       alogt_ref, e_ref, ltri_ref, utri_ref)
        s_last = s_x[CHUNK - 1:CHUNK, :]
        es_x = jnp.exp(s_x)
        w_x = jnp.exp(s_last - s_x)
        cd_x = jnp.exp(s_last)
        d_x = _dot_sel(jnp.broadcast_to(d_ref[...], (8, LANE)), e_ref[...])[0:1, :]
        xs = xs_ref[...]
        xv = xs * dt_x
        dyv = dy_ref[...]
        row = lax.broadcasted_iota(jnp.int32, (CHUNK, CHUNK), 0)
        colm = lax.broadcasted_iota(jnp.int32, (CHUNK, CHUNK), 1)
        tril = colm <= row
        masks = _head_masks()
        is_last = lax.broadcasted_iota(jnp.int32, (CHUNK, 1), 0) == CHUNK - 1
        nt = (((1,), (1,)), ((), ()))
        tn = (((0,), (0,)), ((), ()))
        ds_parts, ddt_parts = [], []
        head_lane = lax.broadcasted_iota(jnp.int32, (CHUNK, LANE), 1)
        triu = colm >= row
        ds_diag = jnp.zeros((CHUNK, LANE), F32)
        for g in range(SSD_GROUPS):
            gs = slice(g * gw, (g + 1) * gw)
            bg = b_ref[:, g * SSD_N:(g + 1) * SSD_N].astype(BF16)
            cg = c_ref[:, g * SSD_N:(g + 1) * SSD_N].astype(BF16)
            xg = xv[:, gs]
            xgb = xg.astype(BF16)
            hg = prev_ref[g]
            hgb = hg.astype(BF16)
            dsg = dstate[g]
            dsgb = dsg.astype(BF16)
            dyg = dyv[:, gs]
            dye = (dyg * es_x[:, gs]).astype(BF16)
            xw = (xg * w_x[:, gs]).astype(BF16)
            cb = lax.dot_general(cg, bg, nt, preferred_element_type=F32)
            cbt = lax.dot_general(bg, cg, nt, preferred_element_type=F32)
            dcg = lax.dot_general(dye, hgb, nt, preferred_element_type=F32)
            dh = lax.dot_general(cg, dye, tn, preferred_element_type=F32)
            bds = jnp.dot(bg, dsgb, preferred_element_type=F32)
            yoff = es_x[:, gs] * jnp.dot(cg, hgb, preferred_element_type=F32)
            dx_state = w_x[:, gs] * bds
            dbg = lax.dot_general(xw, dsgb, nt, preferred_element_type=F32)
            dxd = jnp.zeros((CHUNK, gw), F32)
            dcb = jnp.zeros((CHUNK, CHUNK), F32)
            for r in range(4):
                dec = _ssd_decay(s, st, g * 4 + r, tril)
                mf = cb * dec
                m = mf.astype(BF16)
                dym = (dyg * masks[r]).astype(BF16)
                dm = lax.dot_general(dym, xgb, nt, preferred_element_type=F32)
                dxd = dxd + lax.dot_general(m, dym, tn, preferred_element_type=F32)
                dcb = dcb + dm * dec
                dmt = lax.dot_general(xgb, dym, nt, preferred_element_type=F32)
                rc = (jnp.sum(dm * mf, axis=1, keepdims=True)
                      - jnp.sum(dmt * cbt * _ssd_decay_t(s, st, g * 4 + r, triu), axis=1, keepdims=True))
                ds_diag = ds_diag + jnp.where(head_lane == g * 4 + r, rc, 0.0)
            dcbb = dcb.astype(BF16)
            dcg = dcg + jnp.dot(dcbb, bg, preferred_element_type=F32)
            dbg = dbg + lax.dot_general(dcbb, cg, tn, preferred_element_type=F32)
            dxg = dxd + dx_state
            extra = (jnp.sum(xg * dx_state, axis=0, keepdims=True)
                     + cd_x[:, gs] * jnp.sum(dsg * hg, axis=0, keepdims=True))
            ds_parts.append(dyg * yoff - xg * dx_state + jnp.where(is_last, extra, 0.0))
            ddt_parts.append(dxg * xs[:, gs])
            dxs_ref[:, gs] = dxg * dt_x[:, gs] + d_x[:, gs] * dyg
            db_ref[:, g * SSD_N:(g + 1) * SSD_N] = dbg
            dc_ref[:, g * SSD_N:(g + 1) * SSD_N] = dcg
            dstate[g] = cd_x[:, gs] * dsg + dh
        etv = et_ref[...]
        ds = ds_diag + _dot_sel(jnp.concatenate(ds_parts, axis=1), etv)
        da = _dot_sel(ds, utri_ref[...], x_left=False)
        ddt = da * a_neg + _dot_sel(jnp.concatenate(ddt_parts, axis=1), etv)
        ddtr = ddt * _sigmoid(dtr_ref[...] + bias_ref[...])
        ddtr_ref[...] = ddtr.astype(ddtr_ref.dtype)
        p_alog = jnp.sum(da * dt, axis=0, keepdims=True) * a_neg
        p_bias = jnp.sum(ddtr, axis=0, keepdims=True)
        p_d = _dot_sel(jnp.broadcast_to(jnp.sum(dyv * xs, axis=0, keepdims=True), (8, SSD_HEADS * SSD_P)),
                       etv)[0:1, :]

        @pl.when(step == 0)
        def _():
            dalog_ref[...] = p_alog
            dbias_ref[...] = p_bias
            dd_ref[...] = p_d

        @pl.when(step > 0)
        def _():
            dalog_ref[...] += p_alog
            dbias_ref[...] += p_bias
            dd_ref[...] += p_d

    nh = LANE
    rev = lambda i: nc - 1 - i
    chunk_row = lambda w, cb: pl.BlockSpec((CHUNK, w), lambda i, cb=cb: (rev(i), cb))
    full = lambda a: pl.BlockSpec(a.shape, lambda i: (0,) * a.ndim)
    vec = pl.BlockSpec((1, nh), lambda i: (0, 0))
    in_specs = [chunk_row(1024, 0), chunk_row(512, 2), chunk_row(512, 3), chunk_row(nh, 0),
                pl.BlockSpec((None, nh, CHUNK), lambda i: (rev(i), 0, 0)),
                full(bias), full(biast), full(alog), full(alogt), full(dskip), full(e), full(et), full(ltri),
                full(utri),
                pl.BlockSpec((None, SSD_GROUPS, SSD_N, gw), lambda i: (rev(i), 0, 0, 0)), chunk_row(1024, 0)]
    return pl.pallas_call(
        body, name=name, grid=(nc,), in_specs=in_specs,
        out_specs=[chunk_row(1024, 0), chunk_row(512, 0), chunk_row(512, 0), chunk_row(nh, 0), vec, vec, vec],
        out_shape=[jax.ShapeDtypeStruct((s_len, 1024), F32), jax.ShapeDtypeStruct((s_len, 512), F32),
                   jax.ShapeDtypeStruct((s_len, 512), F32), jax.ShapeDtypeStruct((s_len, nh), BF16),
                   jax.ShapeDtypeStruct((1, nh), F32), jax.ShapeDtypeStruct((1, nh), F32),
                   jax.ShapeDtypeStruct((1, nh), F32)],
        scratch_shapes=[pltpu.VMEM((SSD_GROUPS, SSD_N, gw), F32)],
        compiler_params=_cparams(("arbitrary",)),
    )(xbc, xbc, xbc, dtr, dtrt, bias, biast, alog, alogt, dskip, e, et, ltri, utri, prev, dy)


def _ssd_gate_fwd(y, z, g, *, name):
    r, w = y.shape
    tr = _pick(r, 512, 8)
    gw = w // SSD_GROUPS

    def body(y_ref, z_ref, g_ref, o_ref):
        for k in range(SSD_GROUPS):
            cs = slice(k * gw, (k + 1) * gw)
            t = y_ref[:, cs] * _silu(z_ref[:, cs])
            rstd = lax.rsqrt(jnp.mean(t * t, axis=-1, keepdims=True) + EPS)
            o_ref[:, cs] = (t * rstd * g_ref[:, cs]).astype(o_ref.dtype)

    row = pl.BlockSpec((tr, w), lambda i: (i, 0))
    vec = pl.BlockSpec((1, w), lambda i: (0, 0))
    return pl.pallas_call(
        body, name=name, grid=(r // tr,), in_specs=[row, row, vec], out_specs=row,
        out_shape=jax.ShapeDtypeStruct((r, w), BF16), compiler_params=_cparams(("parallel",)),
    )(y, z, g)


def _ssd_gate_bwd(y, z, g, do, *, name):
    r, w = y.shape
    tr = _pick(r, 512, 8)
    gw = w // SSD_GROUPS

    def body(y_ref, z_ref, g_ref, do_ref, dy_ref, dz_ref, dg_ref):
        parts = []
        for k in range(SSD_GROUPS):
            cs = slice(k * gw, (k + 1) * gw)
            yv = y_ref[:, cs]
            zv = z_ref[:, cs]
            sz = _silu(zv)
            t = yv * sz
            rstd = lax.rsqrt(jnp.mean(t * t, axis=-1, keepdims=True) + EPS)
            th = t * rstd
            dov = do_ref[:, cs].astype(F32)
            dog = dov * g_ref[:, cs]
            dt = rstd * (dog - th * jnp.mean(dog * th, axis=-1, keepdims=True))
            dy_ref[:, cs] = dt * sz
            dz_ref[:, cs] = (dt * yv * _dsilu(zv)).astype(dz_ref.dtype)
            parts.append(jnp.sum(dov * th, axis=0, keepdims=True))
        pg = jnp.concatenate(parts, axis=1)

        @pl.when(pl.program_id(0) == 0)
        def _():
            dg_ref[...] = pg

        @pl.when(pl.program_id(0) > 0)
        def _():
            dg_ref[...] += pg

    row = pl.BlockSpec((tr, w), lambda i: (i, 0))
    vec = pl.BlockSpec((1, w), lambda i: (0, 0))
    return pl.pallas_call(
        body, name=name, grid=(r // tr,), in_specs=[row, row, vec, row], out_specs=[row, row, vec],
        out_shape=[jax.ShapeDtypeStruct((r, w), F32), jax.ShapeDtypeStruct((r, w), BF16),
                   jax.ShapeDtypeStruct((1, w), F32)],
        compiler_params=_cparams(("arbitrary",)),
    )(y, z, g, do)


def _rope_swap(x):
    lane = lax.broadcasted_iota(jnp.int32, x.shape, 1)
    lo = pltpu.roll(x, 96, 1)
    hi = pltpu.roll(x, 32, 1)
    return jnp.where(lane < 32, lo, jnp.where(lane < 64, hi, 0.0))


def _norm_part(v, g, n):
    rstd = lax.rsqrt(jnp.sum(v * v, axis=-1, keepdims=True) * (1.0 / n) + EPS)
    xh = v * rstd
    return xh * g, xh, rstd


def _norm_part_bwd(dout, g, xh, rstd, n):
    dg = dout * g
    return rstd * (dg - xh * (jnp.sum(dg * xh, axis=-1, keepdims=True) * (1.0 / n)))


def _mla_prep_fwd(q, kv, krr, cs, sn, gq, gk, *, name):
    s = q.shape[0]
    tr = _pick(s, 256, 8)
    hp = MLA_HP

    def body(q_ref, kv_ref, krr_ref, cs_ref, sn_ref, gq_ref, gk_ref, qf_ref, kf_ref, v_ref):
        csv, snv = cs_ref[...], sn_ref[...]
        gqn, gqr = gq_ref[:, 0:128], gq_ref[:, 128:256]
        gkn, gkr = gk_ref[:, 0:128], gk_ref[:, 128:256]
        kr, _, _ = _norm_part(krr_ref[...], gkr, MLA_ROPE)
        kr = (kr * csv + _rope_swap(kr) * snv).astype(BF16)
        for h in range(MLA_HEADS):
            qn, _, _ = _norm_part(q_ref[:, h * hp:h * hp + 128], gqn, MLA_NOPE)
            qr, _, _ = _norm_part(q_ref[:, h * hp + 128:(h + 1) * hp], gqr, MLA_ROPE)
            qr = qr * csv + _rope_swap(qr) * snv
            qf_ref[h, :, 0:128] = qn.astype(BF16)
            qf_ref[h, :, 128:256] = qr.astype(BF16)
            kn, _, _ = _norm_part(kv_ref[:, h * hp:h * hp + 128], gkn, MLA_NOPE)
            kf_ref[h, :, 0:128] = kn.astype(BF16)
            kf_ref[h, :, 128:256] = kr
            v_ref[h] = kv_ref[:, h * hp + 128:(h + 1) * hp].astype(BF16)

    row = lambda w: pl.BlockSpec((tr, w), lambda i: (i, 0))
    vec = pl.BlockSpec((1, hp), lambda i: (0, 0))
    hrow = lambda w: pl.BlockSpec((MLA_HEADS, tr, w), lambda i: (0, i, 0))
    return pl.pallas_call(
        body, name=name, grid=(s // tr,),
        in_specs=[row(MLA_HEADS * hp), row(MLA_HEADS * hp), row(128), row(128), row(128), vec, vec],
        out_specs=[hrow(hp), hrow(hp), hrow(128)],
        out_shape=[jax.ShapeDtypeStruct((MLA_HEADS, s, hp), BF16), jax.ShapeDtypeStruct((MLA_HEADS, s, hp), BF16),
                   jax.ShapeDtypeStruct((MLA_HEADS, s, 128), BF16)],
        compiler_params=_cparams(("parallel",)),
    )(q, kv, krr, cs, sn, gq, gk)


def _mla_prep_bwd(q, kv, krr, cs, sn, gq, gk, dqf, dkf, dv, *, name):
    s = q.shape[0]
    tr = _pick(s, 256, 8)
    hp = MLA_HP

    def body(q_ref, kv_ref, krr_ref, cs_ref, sn_ref, gq_ref, gk_ref, dqf_ref, dkf_ref, dv_ref,
             dq_ref, dkv_ref, dkrr_ref, dgq_ref, dgk_ref):
        csv, snv = cs_ref[...], sn_ref[...]
        gqn, gqr = gq_ref[:, 0:128], gq_ref[:, 128:256]
        gkn, gkr = gk_ref[:, 0:128], gk_ref[:, 128:256]
        _, krh, krs = _norm_part(krr_ref[...], gkr, MLA_ROPE)
        dkr_sum = jnp.zeros((tr, 128), F32)
        pgqn = jnp.zeros((1, 128), F32)
        pgqr = jnp.zeros((1, 128), F32)
        pgkn = jnp.zeros((1, 128), F32)
        for h in range(MLA_HEADS):
            _, qnh, qns = _norm_part(q_ref[:, h * hp:h * hp + 128], gqn, MLA_NOPE)
            _, qrh, qrs = _norm_part(q_ref[:, h * hp + 128:(h + 1) * hp], gqr, MLA_ROPE)
            dqn = dqf_ref[h, :, 0:128]
            drr = dqf_ref[h, :, 128:256]
            dqr = drr * csv + _rope_swap(drr * snv)
            dq_ref[:, h * hp:h * hp + 128] = _norm_part_bwd(dqn, gqn, qnh, qns, MLA_NOPE).astype(dq_ref.dtype)
            dq_ref[:, h * hp + 128:(h + 1) * hp] = _norm_part_bwd(dqr, gqr, qrh, qrs, MLA_ROPE).astype(dq_ref.dtype)
            pgqn = pgqn + jnp.sum(dqn * qnh, axis=0, keepdims=True)
            pgqr = pgqr + jnp.sum(dqr * qrh, axis=0, keepdims=True)
            _, knh, kns = _norm_part(kv_ref[:, h * hp:h * hp + 128], gkn, MLA_NOPE)
            dkn = dkf_ref[h, :, 0:128]
            dkv_ref[:, h * hp:h * hp + 128] = _norm_part_bwd(dkn, gkn, knh, kns, MLA_NOPE).astype(dkv_ref.dtype)
            dkv_ref[:, h * hp + 128:(h + 1) * hp] = dv_ref[h].astype(dkv_ref.dtype)
            pgkn = pgkn + jnp.sum(dkn * knh, axis=0, keepdims=True)
            dkr_sum = dkr_sum + dkf_ref[h, :, 128:256]
        dkr = dkr_sum * csv + _rope_swap(dkr_sum * snv)
        dkrr_ref[...] = _norm_part_bwd(dkr, gkr, krh, krs, MLA_ROPE).astype(dkrr_ref.dtype)
        pgkr = jnp.sum(dkr * krh, axis=0, keepdims=True)
        pq = jnp.concatenate([pgqn, pgqr], axis=1)
        pk = jnp.concatenate([pgkn, pgkr], axis=1)

        @pl.when(pl.program_id(0) == 0)
        def _():
            dgq_ref[...] = pq
            dgk_ref[...] = pk

        @pl.when(pl.program_id(0) > 0)
        def _():
            dgq_ref[...] += pq
            dgk_ref[...] += pk

    row = lambda w: pl.BlockSpec((tr, w), lambda i: (i, 0))
    vec = pl.BlockSpec((1, hp), lambda i: (0, 0))
    hrow = lambda w: pl.BlockSpec((MLA_HEADS, tr, w), lambda i: (0, i, 0))
    return pl.pallas_call(
        body, name=name, grid=(s // tr,),
        in_specs=[row(MLA_HEADS * hp), row(MLA_HEADS * hp), row(128), row(128), row(128), vec, vec,
                  hrow(hp), hrow(hp), hrow(128)],
        out_specs=[row(MLA_HEADS * hp), row(MLA_HEADS * hp), row(128), vec, vec],
        out_shape=[jax.ShapeDtypeStruct((s, MLA_HEADS * hp), BF16), jax.ShapeDtypeStruct((s, MLA_HEADS * hp), BF16),
                   jax.ShapeDtypeStruct((s, 128), BF16), jax.ShapeDtypeStruct((1, hp), F32),
                   jax.ShapeDtypeStruct((1, hp), F32)],
        compiler_params=_cparams(("arbitrary",), 48),
    )(q, kv, krr, cs, sn, gq, gk, dqf, dkf, dv)


ATT_T = 512


def _chunk_mask(t):
    r = lax.shift_right_logical(lax.broadcasted_iota(jnp.int32, (t, t), 0), 6)
    c = lax.shift_right_logical(lax.broadcasted_iota(jnp.int32, (t, t), 1), 6)
    return c <= r


def _mla_attn_fwd(qf, kf, v, *, name):
    nh, s, hp = qf.shape
    t = _pick(s, ATT_T, CHUNK)
    scale = (MLA_NOPE + MLA_ROPE) ** -0.5
    nt = (((1,), (1,)), ((), ()))

    def body(q_ref, k_ref, v_ref, o_ref, lse_ref):
        i = pl.program_id(1)
        q = q_ref[...]

        def block(j, carry, masked):
            m, l, acc = carry
            start = pl.multiple_of(j * t, t)
            k = k_ref[pl.ds(start, t), :]
            sc = lax.dot_general(q, k, nt, preferred_element_type=F32) * scale
            if masked:
                sc = jnp.where(_chunk_mask(t), sc, NEG)
            m_new = jnp.maximum(m, jnp.max(sc, axis=-1, keepdims=True))
            alpha = jnp.exp(m - m_new)
            p = jnp.exp(sc - m_new)
            l = alpha * l + jnp.sum(p, axis=-1, keepdims=True)
            acc = alpha * acc + jnp.dot(p.astype(BF16), v_ref[pl.ds(start, t), :], preferred_element_type=F32)
            return m_new, l, acc

        init = (jnp.full((t, 1), NEG, F32), jnp.zeros((t, 1), F32), jnp.zeros((t, MLA_V), F32))
        carry = lax.fori_loop(0, i, lambda j, c: block(j, c, False), init)
        m, l, acc = block(i, carry, True)
        o_ref[...] = acc / l
        lse_ref[...] = m + jnp.log(l)

    return pl.pallas_call(
        body, name=name, grid=(nh, s // t),
        in_specs=[pl.BlockSpec((None, t, hp), lambda h, i: (h, i, 0)),
                  pl.BlockSpec((None, s, hp), lambda h, i: (h, 0, 0)),
                  pl.BlockSpec((None, s, MLA_V), lambda h, i: (h, 0, 0))],
        out_specs=[pl.BlockSpec((t, MLA_V), lambda h, i: (i, h)),
                   pl.BlockSpec((None, t, 1), lambda h, i: (h, i, 0))],
        out_shape=[jax.ShapeDtypeStruct((s, nh * MLA_V), F32), jax.ShapeDtypeStruct((nh, s, 1), F32)],
        compiler_params=_cparams(("parallel", "arbitrary"), 48),
    )(qf, kf, v)


def _mla_attn_bwd(qf, kf, v, o, lse, do, *, name):
    nh, s, hp = qf.shape
    t = _pick(s, ATT_T, CHUNK)
    nb = s // t
    scale = (MLA_NOPE + MLA_ROPE) ** -0.5
    nt = (((1,), (1,)), ((), ()))
    tn = (((0,), (0,)), ((), ()))

    def body(q_ref, k_ref, v_ref, o_ref, lse_ref, do_ref, dq_ref, dk_ref, dv_ref, delta):
        j = pl.program_id(1)

        @pl.when(j == 0)
        def _():
            dq_ref[...] = jnp.zeros_like(dq_ref)
            delta[...] = jnp.sum(do_ref[...] * o_ref[...], axis=-1, keepdims=True)

        k = k_ref[...]
        vv = v_ref[...]

        def block(i, carry, masked):
            dk, dv = carry
            start = pl.multiple_of(i * t, t)
            q = q_ref[pl.ds(start, t), :]
            dob = do_ref[pl.ds(start, t), :].astype(BF16)
            sc = lax.dot_general(q, k, nt, preferred_element_type=F32) * scale
            if masked:
                sc = jnp.where(_chunk_mask(t), sc, NEG)
            p = jnp.exp(sc - lse_ref[pl.ds(start, t), :])
            dp = lax.dot_general(dob, vv, nt, preferred_element_type=F32)
            ds = (p * (dp - delta[pl.ds(start, t), :]) * scale).astype(BF16)
            dv = dv + lax.dot_general(p.astype(BF16), dob, tn, preferred_element_type=F32)
            dk = dk + lax.dot_general(ds, q, tn, preferred_element_type=F32)
            dq_ref[pl.ds(start, t), :] += jnp.dot(ds, k, preferred_element_type=F32)
            return dk, dv

        init = (jnp.zeros((t, hp), F32), jnp.zeros((t, MLA_V), F32))
        carry = block(j, init, True)
        dk, dv = lax.fori_loop(j + 1, nb, lambda i, c: block(i, c, False), carry)
        dk_ref[...] = dk
        dv_ref[...] = dv

    whole = lambda w: pl.BlockSpec((None, s, w), lambda h, j: (h, 0, 0))
    blk = lambda w: pl.BlockSpec((None, t, w), lambda h, j: (h, j, 0))
    colh = pl.BlockSpec((s, MLA_V), lambda h, j: (0, h))
    return pl.pallas_call(
        body, name=name, grid=(nh, nb),
        in_specs=[whole(hp), blk(hp), blk(MLA_V), colh, whole(1), colh],
        out_specs=[whole(hp), blk(hp), blk(MLA_V)],
        out_shape=[jax.ShapeDtypeStruct((nh, s, hp), F32), jax.ShapeDtypeStruct((nh, s, hp), F32),
                   jax.ShapeDtypeStruct((nh, s, MLA_V), F32)],
        scratch_shapes=[pltpu.VMEM((s, 1), F32)],
        compiler_params=_cparams(("parallel", "arbitrary"), 56),
    )(qf, kf, v, o, lse, do)


def _merge_fwd(gl, gb, ys, yc, ym, *, name):
    s, d = ys.shape
    tr = _pick(s, 256, 8)

    def body(gl_ref, gb_ref, ys_ref, yc_ref, ym_ref, o_ref):
        acc = jnp.zeros((tr, d), F32)
        for k, y_ref in enumerate((ys_ref, yc_ref, ym_ref)):
            gt = _sigmoid(gl_ref[:, k * d:(k + 1) * d] + gb_ref[:, k * d:(k + 1) * d])
            acc = acc + gt * y_ref[...]
        o_ref[...] = acc.astype(o_ref.dtype)

    row = lambda w: pl.BlockSpec((tr, w), lambda i: (i, 0))
    return pl.pallas_call(
        body, name=name, grid=(s // tr,),
        in_specs=[row(3 * d), pl.BlockSpec((1, 3 * d), lambda i: (0, 0)), row(d), row(d), row(d)],
        out_specs=row(d), out_shape=jax.ShapeDtypeStruct((s, d), BF16),
        compiler_params=_cparams(("parallel",)),
    )(gl, gb, ys, yc, ym)


def _merge_bwd(gl, gb, ys, yc, ym, dm, *, name):
    s, d = ys.shape
    tr = _pick(s, 256, 8)

    def body(gl_ref, gb_ref, ys_ref, yc_ref, ym_ref, dm_ref, dgl_ref, dgb_ref, dys_ref, dyc_ref, dym_ref):
        dmv = dm_ref[...]
        parts = []
        for k, (y_ref, dy_ref) in enumerate(((ys_ref, dys_ref), (yc_ref, dyc_ref), (ym_ref, dym_ref))):
            gt = _sigmoid(gl_ref[:, k * d:(k + 1) * d] + gb_ref[:, k * d:(k + 1) * d])
            dy_ref[...] = (gt * dmv).astype(dy_ref.dtype)
            dl = dmv * y_ref[...] * gt * (1.0 - gt)
            dgl_ref[:, k * d:(k + 1) * d] = dl.astype(dgl_ref.dtype)
            parts.append(jnp.sum(dl, axis=0, keepdims=True))
        pb = jnp.concatenate(parts, axis=1)

        @pl.when(pl.program_id(0) == 0)
        def _():
            dgb_ref[...] = pb

        @pl.when(pl.program_id(0) > 0)
        def _():
            dgb_ref[...] += pb

    row = lambda w: pl.BlockSpec((tr, w), lambda i: (i, 0))
    vec = pl.BlockSpec((1, 3 * d), lambda i: (0, 0))
    return pl.pallas_call(
        body, name=name, grid=(s // tr,),
        in_specs=[row(3 * d), vec, row(d), row(d), row(d), row(d)],
        out_specs=[row(3 * d), vec, row(d), row(d), row(d)],
        out_shape=[jax.ShapeDtypeStruct((s, 3 * d), BF16), jax.ShapeDtypeStruct((1, 3 * d), F32)]
        + [jax.ShapeDtypeStruct((s, d), BF16)] * 3,
        compiler_params=_cparams(("arbitrary",)),
    )(gl, gb, ys, yc, ym, dm)


def _xattn_fwd(q, k, v, gq, *, name):
    s, d = q.shape
    dh = d // X_HEADS
    tr = _pick(s, 512, 8)
    scale = dh ** -0.5
    nt = (((1,), (1,)), ((), ()))

    def body(q_ref, k_ref, v_ref, gq_ref, o_ref):
        for h in range(X_HEADS):
            cs = slice(h * dh, (h + 1) * dh)
            qn, _, _ = _norm_part(q_ref[:, cs], gq_ref[...], dh)
            sc = lax.dot_general(qn.astype(BF16), k_ref[:, cs], nt, preferred_element_type=F32) * scale
            p = jnp.exp(sc - jnp.max(sc, axis=-1, keepdims=True))
            p = p / jnp.sum(p, axis=-1, keepdims=True)
            o_ref[:, cs] = jnp.dot(p.astype(BF16), v_ref[:, cs], preferred_element_type=F32)

    row = pl.BlockSpec((tr, d), lambda i: (i, 0))
    mem = pl.BlockSpec(k.shape, lambda i: (0, 0))
    return pl.pallas_call(
        body, name=name, grid=(s // tr,),
        in_specs=[row, mem, mem, pl.BlockSpec((1, dh), lambda i: (0, 0))], out_specs=row,
        out_shape=jax.ShapeDtypeStruct((s, d), F32), compiler_params=_cparams(("parallel",)),
    )(q, k, v, gq)


def _xattn_bwd(q, k, v, gq, do, *, name):
    s, d = q.shape
    dh = d // X_HEADS
    tr = _pick(s, 512, 8)
    scale = dh ** -0.5
    nt = (((1,), (1,)), ((), ()))
    tn = (((0,), (0,)), ((), ()))

    def body(q_ref, k_ref, v_ref, gq_ref, do_ref, dq_ref, dk_ref, dv_ref, dgq_ref):
        first = pl.program_id(0) == 0
        pg = jnp.zeros((1, dh), F32)
        for h in range(X_HEADS):
            cs = slice(h * dh, (h + 1) * dh)
            qn, qh, qs = _norm_part(q_ref[:, cs], gq_ref[...], dh)
            qnb = qn.astype(BF16)
            kh = k_ref[:, cs]
            sc = lax.dot_general(qnb, kh, nt, preferred_element_type=F32) * scale
            p = jnp.exp(sc - jnp.max(sc, axis=-1, keepdims=True))
            p = p / jnp.sum(p, axis=-1, keepdims=True)
            dob = do_ref[:, cs].astype(BF16)
            dp = lax.dot_general(dob, v_ref[:, cs], nt, preferred_element_type=F32)
            ds = (p * (dp - jnp.sum(dp * p, axis=-1, keepdims=True)) * scale).astype(BF16)
            dqn = jnp.dot(ds, kh, preferred_element_type=F32)
            dq_ref[:, cs] = _norm_part_bwd(dqn, gq_ref[...], qh, qs, dh).astype(dq_ref.dtype)
            pg = pg + jnp.sum(dqn * qh, axis=0, keepdims=True)
            pv = lax.dot_general(p.astype(BF16), dob, tn, preferred_element_type=F32)
            pk = lax.dot_general(ds, qnb, tn, preferred_element_type=F32)

            @pl.when(first)
            def _():
                dv_ref[:, cs] = pv
                dk_ref[:, cs] = pk

            @pl.when(jnp.logical_not(first))
            def _():
                dv_ref[:, cs] += pv
                dk_ref[:, cs] += pk

        @pl.when(first)
        def _():
            dgq_ref[...] = pg

        @pl.when(jnp.logical_not(first))
        def _():
            dgq_ref[...] += pg

    row = pl.BlockSpec((tr, d), lambda i: (i, 0))
    mem = pl.BlockSpec(k.shape, lambda i: (0, 0))
    vec = pl.BlockSpec((1, dh), lambda i: (0, 0))
    return pl.pallas_call(
        body, name=name, grid=(s // tr,),
        in_specs=[row, mem, mem, vec, row], out_specs=[row, mem, mem, vec],
        out_shape=[jax.ShapeDtypeStruct((s, d), BF16), jax.ShapeDtypeStruct(k.shape, F32),
                   jax.ShapeDtypeStruct(k.shape, F32), jax.ShapeDtypeStruct((1, dh), F32)],
        compiler_params=_cparams(("arbitrary",)),
    )(q, k, v, gq, do)


def _swiglu_fwd(h1, *, name):
    s, w2 = h1.shape
    w = w2 // 2
    tr = _pick(s, 256, 8)
    tc = _pick(w, 1408, 128)
    ncb = w // tc

    def body(g_ref, u_ref, o_ref):
        o_ref[...] = (_silu(g_ref[...]) * u_ref[...]).astype(o_ref.dtype)

    return pl.pallas_call(
        body, name=name, grid=(s // tr, ncb),
        in_specs=[pl.BlockSpec((tr, tc), lambda i, j: (i, j)), pl.BlockSpec((tr, tc), lambda i, j: (i, j + ncb))],
        out_specs=pl.BlockSpec((tr, tc), lambda i, j: (i, j)),
        out_shape=jax.ShapeDtypeStruct((s, w), BF16), compiler_params=_cparams(("parallel", "parallel")),
    )(h1, h1)


def _swiglu_bwd(h1, dact, *, name):
    s, w2 = h1.shape
    w = w2 // 2
    tr = _pick(s, 256, 8)
    tc = _pick(w, 1408, 128)
    ncb = w // tc

    def body(g_ref, u_ref, d_ref, dg_ref, du_ref):
        gv = g_ref[...]
        dv = d_ref[...]
        dg_ref[...] = (dv * u_ref[...] * _dsilu(gv)).astype(dg_ref.dtype)
        du_ref[...] = (dv * _silu(gv)).astype(du_ref.dtype)

    blk = pl.BlockSpec((tr, tc), lambda i, j: (i, j))
    dg, du = pl.pallas_call(
        body, name=name, grid=(s // tr, ncb),
        in_specs=[blk, pl.BlockSpec((tr, tc), lambda i, j: (i, j + ncb)), blk],
        out_specs=[blk, blk],
        out_shape=[jax.ShapeDtypeStruct((s, w), BF16)] * 2, compiler_params=_cparams(("parallel", "parallel")),
    )(h1, h1, dact)
    return jnp.concatenate([dg, du], axis=1)


def _add(a, b, *, name):
    r, w = a.shape
    tr = _pick(r, 512, 8)

    def body(a_ref, b_ref, o_ref):
        o_ref[...] = a_ref[...] + b_ref[...].astype(F32)

    row = pl.BlockSpec((tr, w), lambda i: (i, 0))
    return pl.pallas_call(
        body, name=name, grid=(r // tr,), in_specs=[row, row], out_specs=row,
        out_shape=jax.ShapeDtypeStruct((r, w), F32), compiler_params=_cparams(("parallel",)),
    )(a, b)


def _loss(y, target, *, name):
    r, w = y.shape
    tr = _pick(r, 512, 8)

    def body(y_ref, t_ref, dy_ref, l_ref):
        err = y_ref[...] - t_ref[...]
        dy_ref[...] = err * (1.0 / w)
        part = jnp.zeros((8, LANE), F32) + 0.5 * jnp.sum(jnp.mean(err * err, axis=-1, keepdims=True))

        @pl.when(pl.program_id(0) == 0)
        def _():
            l_ref[...] = part

        @pl.when(pl.program_id(0) > 0)
        def _():
            l_ref[...] += part

    row = pl.BlockSpec((tr, w), lambda i: (i, 0))
    dy, l = pl.pallas_call(
        body, name=name, grid=(r // tr,), in_specs=[row, row],
        out_specs=[row, pl.BlockSpec((8, LANE), lambda i: (0, 0))],
        out_shape=[jax.ShapeDtypeStruct((r, w), F32), jax.ShapeDtypeStruct((8, LANE), F32)],
        compiler_params=_cparams(("arbitrary",)),
    )(y, target)
    return dy, l[0, 0]


def _adamw(w, g, m, v, *, name):
    r, c = w.shape
    tr = _pick(r, 256, 8)
    c1 = 1.0 - ADAM_B1 ** ADAM_STEP
    c2 = 1.0 - ADAM_B2 ** ADAM_STEP

    def body(w_ref, g_ref, m_ref, v_ref, d_ref, nm_ref, nv_ref):
        gv = g_ref[...]
        nm = ADAM_B1 * m_ref[...] + (1.0 - ADAM_B1) * gv
        nv = ADAM_B2 * v_ref[...] + (1.0 - ADAM_B2) * (gv * gv)
        nm_ref[...] = nm
        nv_ref[...] = nv
        d_ref[...] = -ADAM_LR * ((nm / c1) / (jnp.sqrt(nv / c2) + ADAM_EPS) + ADAM_WD * w_ref[...])

    row = pl.BlockSpec((tr, c), lambda i: (i, 0))
    return pl.pallas_call(
        body, name=name, grid=(r // tr,), in_specs=[row] * 4, out_specs=[row] * 3,
        out_shape=[jax.ShapeDtypeStruct((r, c), F32)] * 3, compiler_params=_cparams(("parallel",)),
    )(w, g, m, v)


IN_SPLIT = dict(z=(0, 1024), xbc=(1024, 3072), dt=(3072, 3200), glu=(3200, 5248), ql=(5248, 5632),
                ckv=(5632, 5888), kr=(5888, 6016), gate=(6016, 9088))


IN_WIDTH_PAD = 9216


def _w_in_pad(w):
    zeros = lambda n: jnp.zeros(w.shape[:-1] + (n,), w.dtype)
    return jnp.concatenate([w[..., :3088], zeros(112), w[..., 3088:5840], zeros(64), w[..., 5840:],
                            zeros(IN_WIDTH_PAD - 9088)], axis=-1)


def _w_in_unpad(g):
    return jnp.concatenate([g[..., :3088], g[..., 3200:5952], g[..., 6016:9088]], axis=-1)


def _qb_pad(w):
    lead = w.shape[:-1]
    w = w.reshape(lead + (MLA_HEADS, MLA_NOPE + MLA_ROPE))
    w = jnp.concatenate([w, jnp.zeros(lead + (MLA_HEADS, MLA_HP - MLA_NOPE - MLA_ROPE), w.dtype)], axis=-1)
    return w.reshape(lead + (MLA_HEADS * MLA_HP,))


def _qb_unpad(g):
    lead = g.shape[:-1]
    g = g.reshape(lead + (MLA_HEADS, MLA_HP))[..., :MLA_NOPE + MLA_ROPE]
    return g.reshape(lead + (MLA_HEADS * (MLA_NOPE + MLA_ROPE),))


def _pad_lanes(v, n):
    return jnp.concatenate([v, jnp.zeros((n - v.shape[0],), v.dtype)]).reshape(1, n)


def _layer_params(full, rep, l):
    p = {}
    w_in = full['w_in']
    for k, (a, b) in IN_SPLIT.items():
        p['w_' + k] = w_in[:, a:b]
    p['w_in'] = w_in
    for k in ('mla_w_q_b', 'mla_w_kv_b', 'xattn_w_kv', 'ffn_w_in', 'ssd_w_out', 'conv_w_out', 'mla_w_o', 'w_out',
              'xattn_w_q', 'xattn_w_o', 'ffn_w_out', 'ssd_conv_w', 'conv_dw_w'):
        p[k] = full[k]
    p['gate_b'] = full['gate_b'].reshape(1, -1)
    row = lambda name: rep[name][l].reshape(1, -1)
    for k in ('mix_norm_g', 'ssd_conv_b', 'ssd_norm_g', 'conv_dw_b', 'conv_ln_g', 'conv_ln_b', 'mla_q_a_g',
              'mla_kv_a_g', 'xattn_norm_g', 'mem_norm_g', 'xattn_q_norm_g', 'xattn_k_norm_g', 'ffn_norm_g'):
        p[k] = row(k)
    for k in ('ssd_dt_bias', 'ssd_a_log', 'ssd_d'):
        p[k] = _pad_lanes(rep[k][l], LANE)
        p[k + '_t'] = p[k].reshape(LANE, 1)
    p['gq'] = _pad_lanes(rep['mla_q_norm_g'][l], MLA_HP)
    p['gk'] = _pad_lanes(rep['mla_k_norm_g'][l], MLA_HP)
    return p


def _layer_fwd(x, mem, cs, sn, p, l):
    n = lambda s: f"l{l}_{s}"
    s_len, d = x.shape
    nc = s_len // CHUNK
    sv = {'x': x}
    u = _rms_fwd(x, p['mix_norm_g'], out_dtype=BF16, name=n("mix_norm"))
    z = _matmul(u, p['w_z'], name=n("in_z"))
    xbc = _matmul(u, p['w_xbc'], name=n("in_xbc"))
    dtr = _matmul(u, p['w_dt'], name=n("in_dt"))
    glu = _matmul(u, p['w_glu'], name=n("in_glu"))
    ql = _matmul(u, p['w_ql'], name=n("in_ql"))
    ckv = _matmul(u, p['w_ckv'], name=n("in_ckv"))
    krr = _matmul(u, p['w_kr'], name=n("in_kr"))
    gl = _matmul(u, p['w_gate'], name=n("in_gate"))
    pre_s, act_s = _conv_fwd(xbc, p['ssd_conv_w'], p['ssd_conv_b'], glu=False, act=True, name=n("ssd_conv"))
    dtrt = dtr.reshape(nc, CHUNK, LANE).transpose(0, 2, 1)
    y_scan, prev = _ssd_scan_fwd(act_s, dtr, dtrt, p['ssd_dt_bias'], p['ssd_dt_bias_t'], p['ssd_a_log'],
                                 p['ssd_a_log_t'], p['ssd_d'], name=n("ssd_scan"))
    yn = _ssd_gate_fwd(y_scan, z, p['ssd_norm_g'], name=n("ssd_gate"))
    y_ssd = _matmul(yn, p['ssd_w_out'], name=n("ssd_out"))
    pre_c, = _conv_fwd(glu, p['conv_dw_w'], p['conv_dw_b'], glu=True, act=False, name=n("dw_conv"))
    vc = _ln_silu_fwd(pre_c, p['conv_ln_g'], p['conv_ln_b'], name=n("conv_ln"))
    y_conv = _matmul(vc, p['conv_w_out'], name=n("conv_out"))
    qln = _rms_fwd(ql, p['mla_q_a_g'], out_dtype=BF16, name=n("q_a_norm"))
    q = _matmul(qln, p['mla_w_q_b'], name=n("q_b"))
    ckvn = _rms_fwd(ckv, p['mla_kv_a_g'], out_dtype=BF16, name=n("kv_a_norm"))
    kv = _matmul(ckvn, p['mla_w_kv_b'], name=n("kv_b"))
    qf, kf, v = _mla_prep_fwd(q, kv, krr, cs, sn, p['gq'], p['gk'], name=n("mla_prep"))
    o, lse = _mla_attn_fwd(qf, kf, v, name=n("mla_attn"))
    y_mla = _matmul(o, p['mla_w_o'], name=n("mla_out"))
    merged = _merge_fwd(gl, p['gate_b'], y_ssd, y_conv, y_mla, name=n("merge"))
    x1 = _matmul(merged, p['w_out'], add=x, name=n("mix_out"))
    hx = _rms_fwd(x1, p['xattn_norm_g'], out_dtype=BF16, name=n("xattn_norm"))
    qx = _matmul(hx, p['xattn_w_q'], name=n("xattn_q"))
    memn = _rms_fwd(mem, p['mem_norm_g'], out_dtype=BF16, name=n("mem_norm"))
    kvx = _matmul(memn, p['xattn_w_kv'], name=n("xattn_kv"))
    m_len = mem.shape[0]
    dh = d // X_HEADS
    kraw = kvx[:, :d].reshape(m_len * X_HEADS, dh)
    kx = _rms_fwd(kraw, p['xattn_k_norm_g'], out_dtype=BF16, name=n("xattn_k_norm")).reshape(m_len, d)
    vx = kvx[:, d:].astype(BF16)
    ox = _xattn_fwd(qx, kx, vx, p['xattn_q_norm_g'], name=n("xattn_core"))
    x2 = _matmul(ox, p['xattn_w_o'], add=x1, name=n("xattn_out"))
    hf = _rms_fwd(x2, p['ffn_norm_g'], out_dtype=BF16, name=n("ffn_norm"))
    h1 = _matmul(hf, p['ffn_w_in'], name=n("ffn_in"))
    act = _swiglu_fwd(h1, name=n("swiglu"))
    x3 = _matmul(act, p['ffn_w_out'], add=x2, name=n("ffn_out"))
    sv.update(u=u, z=z, xbc=xbc, dtr=dtr, dtrt=dtrt, glu=glu, ql=ql, ckv=ckv, krr=krr, gl=gl, pre_s=pre_s,
              act_s=act_s, y_scan=y_scan, prev=prev, yn=yn, y_ssd=y_ssd, pre_c=pre_c, vc=vc, y_conv=y_conv,
              qln=qln, q=q, ckvn=ckvn, kv=kv, qf=qf, kf=kf, v=v, o=o, lse=lse, y_mla=y_mla, merged=merged,
              x1=x1, hx=hx, qx=qx, memn=memn, kraw=kraw, kx=kx, vx=vx, ox=ox, x2=x2, hf=hf, h1=h1, act=act)
    return x3, sv


DW_KEY = dict(ffn_out_dw='ffn_w_out', ffn_in_dw='ffn_w_in', xattn_out_dw='xattn_w_o', xattn_q_dw='xattn_w_q',
              xattn_kv_dw='xattn_w_kv', mix_out_dw='w_out', mla_out_dw='mla_w_o', q_b_dw='mla_w_q_b',
              kv_b_dw='mla_w_kv_b', conv_out_dw='conv_w_out', ssd_out_dw='ssd_w_out', in_dw='w_in')


def _layer_bwd(dx3, mem, cs, sn, p, sv, l, stacks, depth):
    n = lambda s: f"l{l}_b_{s}"
    dw = lambda s: dict(name=n(s), out_dtype=BF16, into=(stacks.get(DW_KEY[s]), l, depth))
    g = {}
    d = dx3.shape[1]
    dact = _matmul(dx3, p['ffn_w_out'], tb=True, name=n("ffn_out_dx"))
    g['ffn_w_out'] = _matmul(sv['act'], dx3, ta=True, **dw("ffn_out_dw"))
    dh1 = _swiglu_bwd(sv['h1'], dact, name=n("swiglu"))
    g['ffn_w_in'] = _matmul(sv['hf'], dh1, ta=True, **dw("ffn_in_dw"))
    dhf = _matmul(dh1, p['ffn_w_in'], tb=True, name=n("ffn_in_dx"))
    dx2, g['ffn_norm_g'] = _rms_bwd(sv['x2'], p['ffn_norm_g'], dhf, dx_dtype=F32, add=dx3, name=n("ffn_norm"))
    dox = _matmul(dx2, p['xattn_w_o'], tb=True, name=n("xattn_out_dx"))
    g['xattn_w_o'] = _matmul(sv['ox'], dx2, ta=True, **dw("xattn_out_dw"))
    dqx, dkx, dvx, g['xattn_q_norm_g'] = _xattn_bwd(sv['qx'], sv['kx'], sv['vx'], p['xattn_q_norm_g'], dox,
                                                    name=n("xattn_core"))
    g['xattn_w_q'] = _matmul(sv['hx'], dqx, ta=True, **dw("xattn_q_dw"))
    dhx = _matmul(dqx, p['xattn_w_q'], tb=True, name=n("xattn_q_dx"))
    dx1, g['xattn_norm_g'] = _rms_bwd(sv['x1'], p['xattn_norm_g'], dhx, dx_dtype=F32, add=dx2, name=n("xattn_norm"))
    m_len = mem.shape[0]
    dh = d // X_HEADS
    dkraw, g['xattn_k_norm_g'] = _rms_bwd(sv['kraw'], p['xattn_k_norm_g'], dkx.reshape(m_len * X_HEADS, dh),
                                          dx_dtype=BF16, name=n("xattn_k_norm"))
    dkvx = jnp.concatenate([dkraw.reshape(m_len, d), dvx.astype(BF16)], axis=1)
    g['xattn_w_kv'] = _matmul(sv['memn'], dkvx, ta=True, **dw("xattn_kv_dw"))
    dmemn = _matmul(dkvx, p['xattn_w_kv'], tb=True, name=n("xattn_kv_dx"))
    _, g['mem_norm_g'] = _rms_bwd(mem, p['mem_norm_g'], dmemn, dx_dtype=BF16, name=n("mem_norm"))
    dmerged = _matmul(dx1, p['w_out'], tb=True, name=n("mix_out_dx"))
    g['w_out'] = _matmul(sv['merged'], dx1, ta=True, **dw("mix_out_dw"))
    dgl, g['gate_b'], dys, dyc, dym = _merge_bwd(sv['gl'], p['gate_b'], sv['y_ssd'], sv['y_conv'], sv['y_mla'],
                                                 dmerged, name=n("merge"))
    do = _matmul(dym, p['mla_w_o'], tb=True, name=n("mla_out_dx"))
    g['mla_w_o'] = _matmul(sv['o'], dym, ta=True, **dw("mla_out_dw"))
    dqf, dkf, dv = _mla_attn_bwd(sv['qf'], sv['kf'], sv['v'], sv['o'], sv['lse'], do, name=n("mla_attn"))
    dq, dkv, dkrr, g['gq'], g['gk'] = _mla_prep_bwd(sv['q'], sv['kv'], sv['krr'], cs, sn, p['gq'], p['gk'],
                                                    dqf, dkf, dv, name=n("mla_prep"))
    g['mla_w_q_b'] = _matmul(sv['qln'], dq, ta=True, **dw("q_b_dw"))
    dqln = _matmul(dq, p['mla_w_q_b'], tb=True, name=n("q_b_dx"))
    dql, g['mla_q_a_g'] = _rms_bwd(sv['ql'], p['mla_q_a_g'], dqln, dx_dtype=BF16, name=n("q_a_norm"))
    g['mla_w_kv_b'] = _matmul(sv['ckvn'], dkv, ta=True, **dw("kv_b_dw"))
    dckvn = _matmul(dkv, p['mla_w_kv_b'], tb=True, name=n("kv_b_dx"))
    dckv, g['mla_kv_a_g'] = _rms_bwd(sv['ckv'], p['mla_kv_a_g'], dckvn, dx_dtype=BF16, name=n("kv_a_norm"))
    dvc = _matmul(dyc, p['conv_w_out'], tb=True, name=n("conv_out_dx"))
    g['conv_w_out'] = _matmul(sv['vc'], dyc, ta=True, **dw("conv_out_dw"))
    dpre_c, g['conv_ln_g'], g['conv_ln_b'] = _ln_silu_bwd(sv['pre_c'], p['conv_ln_g'], p['conv_ln_b'], dvc,
                                                          name=n("conv_ln"))
    da, dg, g['conv_dw_w'], g['conv_dw_b'] = _conv_bwd(sv['glu'], p['conv_dw_w'], p['conv_dw_b'], dpre_c, None,
                                                       glu=True, act=False, name=n("dw_conv"))
    dyn = _matmul(dys, p['ssd_w_out'], tb=True, name=n("ssd_out_dx"))
    g['ssd_w_out'] = _matmul(sv['yn'], dys, ta=True, **dw("ssd_out_dw"))
    dy_scan, dz, g['ssd_norm_g'] = _ssd_gate_bwd(sv['y_scan'], sv['z'], p['ssd_norm_g'], dyn, name=n("ssd_gate"))
    dxs, db, dc, ddtr, g['ssd_a_log'], g['ssd_dt_bias'], g['ssd_d'] = _ssd_scan_bwd(
        sv['act_s'], sv['dtr'], sv['dtrt'], p['ssd_dt_bias'], p['ssd_dt_bias_t'], p['ssd_a_log'], p['ssd_a_log_t'],
        p['ssd_d'], sv['prev'], dy_scan, name=n("ssd_scan"))
    dact_s = jnp.concatenate([dxs, db, dc], axis=1)
    dxbc, g['ssd_conv_w'], g['ssd_conv_b'] = _conv_bwd(sv['xbc'], p['ssd_conv_w'], p['ssd_conv_b'], dact_s,
                                                       sv['pre_s'], glu=False, act=True, name=n("ssd_conv"))
    tail = jnp.zeros((dz.shape[0], IN_WIDTH_PAD - IN_SPLIT['gate'][1]), BF16)
    dproj = jnp.concatenate([dz, dxbc, ddtr, da, dg, dql, dckv, dkrr, dgl, tail], axis=1)
    g['w_in'] = _matmul(sv['u'], dproj, ta=True, **dw("in_dw"))
    du = _matmul(dproj, p['w_in'], tb=True, name=n("in_dx"))
    dx, g['mix_norm_g'] = _rms_bwd(sv['x'], p['mix_norm_g'], du, dx_dtype=F32, add=dx1, name=n("mix_norm"))
    return dx, g


REP_NAMES = ('mix_norm_g', 'ssd_conv_b', 'ssd_dt_bias', 'ssd_a_log', 'ssd_d', 'ssd_norm_g', 'conv_dw_b', 'conv_ln_g',
             'conv_ln_b', 'mla_q_a_g', 'mla_kv_a_g', 'mla_q_norm_g', 'mla_k_norm_g', 'xattn_norm_g', 'mem_norm_g',
             'xattn_q_norm_g', 'xattn_k_norm_g', 'ffn_norm_g')
BIG = (('w_in', 1, 1024, 8912), ('mla_w_q_b', 1, 384, 1536), ('mla_w_kv_b', 1, 256, 2048),
       ('xattn_w_kv', 1, 1024, 2048), ('ffn_w_in', 1, 1024, 5632), ('ssd_w_out', 0, 1024, 1024),
       ('conv_w_out', 0, 1024, 1024), ('mla_w_o', 0, 1024, 1024), ('w_out', 0, 1024, 1024),
       ('xattn_w_q', 0, 1024, 1024), ('xattn_w_o', 0, 1024, 1024), ('ffn_w_out', 0, 2816, 1024))
SMALL = (('ssd_conv_w', 1, 4, 2048), ('conv_dw_w', 1, 31, 1024), ('gate_b', 1, 3, 1024))


def _rope_tables(positions):
    half = MLA_ROPE // 2
    inv = ROPE_THETA ** (-jnp.arange(0, MLA_ROPE, 2, dtype=F32) / MLA_ROPE)
    ang = positions.astype(F32)[:, None] * inv
    cos, sin = jnp.cos(ang), jnp.sin(ang)
    z = jnp.zeros((positions.shape[0], LANE - 2 * half), F32)
    return jnp.concatenate([cos, cos, z], axis=1), jnp.concatenate([-sin, sin, z], axis=1)


def _local_step(x, mem, positions, target, layer_weights, rep, on_layer=None):
    depth = rep['mix_norm_g'].shape[0]
    cs, sn = _rope_tables(positions)
    params, saved = [], []
    h = x
    for l in range(depth):
        p = _layer_params(layer_weights(l, h), rep, l)
        h, sv = _layer_fwd(h, mem, cs, sn, p, l)
        params.append(p)
        saved.append(sv)
    dh, loss = _loss(h, target, name="loss")
    layer_grads = [None] * depth
    stacks = {}
    for l in reversed(range(depth)):
        dh, layer_grads[l] = _layer_bwd(dh, mem, cs, sn, params[l], saved[l], l, stacks, depth)
        stacks = {k: layer_grads[l][k] for k in DW_KEY.values()}
        if on_layer is not None:
            order = on_layer(l, stacks)
            if order is not None:
                dh = dh + order
    stack = lambda k: jnp.stack([layer_grads[l][k] for l in range(depth)])
    gfull = {k: stack(k) for k, _, _, _ in SMALL}
    gfull.update(stacks)
    gfull['w_in'] = _w_in_unpad(gfull['w_in'])
    gfull['mla_w_q_b'] = _qb_unpad(gfull['mla_w_q_b'])
    gfull['gate_b'] = gfull['gate_b'].reshape(depth, 3, -1)
    grep = {}
    for k in REP_NAMES:
        if k == 'mla_q_norm_g':
            grep[k] = stack('gq')[:, 0, :MLA_NOPE + MLA_ROPE]
        elif k == 'mla_k_norm_g':
            grep[k] = stack('gk')[:, 0, :MLA_NOPE + MLA_ROPE]
        elif k in ('ssd_dt_bias', 'ssd_a_log', 'ssd_d'):
            grep[k] = stack(k)[:, 0, :SSD_HEADS]
        else:
            grep[k] = stack(k)[:, 0, :]
    return loss, dh, gfull, grep


N_CHIPS = 4
HBM_SPEC = pl.BlockSpec(memory_space=pl.ANY)
ROW, COL, STK, REP = "row", "col", "stk", "rep"


def _kind(axis, cs):
    if axis == 0:
        return ROW
    return COL if cs % LANE == 0 else STK


def _mesh_pos():
    return lax.axis_index("x"), lax.axis_index("y"), lax.axis_index("c")


def _chip_view(ref, kind, j, a, b, layers=None):
    lsel = slice(None) if layers is None else pl.ds(layers[0], layers[1])
    if kind == STK:
        return ref.at[j, lsel]
    if kind == ROW:
        return ref.at[lsel, pl.ds(pl.multiple_of(j * a, 8), a), :]
    if kind == COL:
        return ref.at[lsel, :, pl.ds(pl.multiple_of(j * b, LANE), b)]
    return ref.at[lsel]


def _all_gather(shards, kinds, *, name):
    n = len(shards)
    depth = shards[0].shape[0]
    lh = depth // 2

    def out_shape(w, kind):
        _, a, b = w.shape
        full = {ROW: (depth, N_CHIPS * a, b), COL: (depth, a, N_CHIPS * b), STK: (N_CHIPS, depth, a, b)}[kind]
        return jax.ShapeDtypeStruct(full, w.dtype)

    def body(*refs):
        w_refs, out_refs = refs[:n], refs[n:2 * n]
        send_sems, recv_sems = refs[2 * n:]
        x, y, cc = _mesh_pos()
        me = (x, y, cc)
        sibling = (x, y, 1 - cc)
        chips = [(1 - x, y), (x, 1 - y), (1 - x, 1 - y)]
        slot = lambda chip: 2 * chip[0] + chip[1]

        def part(i, chip, hc):
            _, a, b = w_refs[i].shape
            return _chip_view(out_refs[i], kinds[i], slot(chip), a, b, (hc * lh, lh))

        def own(i):
            _, a, b = w_refs[i].shape
            return _chip_view(out_refs[i], kinds[i], slot((x, y)), a, b)

        def copy(i, k, src, dst, to):
            return pltpu.make_async_remote_copy(src_ref=src, dst_ref=dst, send_sem=send_sems.at[i, k],
                                                recv_sem=recv_sems.at[i, k], device_id=to, device_id_type=MESH)

        sends = []
        for i in range(n):
            my_half = w_refs[i].at[pl.ds(cc * lh, lh)]
            for k, chip in enumerate(chips):
                sends.append(copy(i, k, my_half, part(i, (x, y), cc), (*chip, cc)))
            sends.append(copy(i, 6, w_refs[i], own(i), sibling))
        for cp in sends:
            cp.start()
        for k, chip in enumerate(chips):
            for i in range(n):
                copy(i, k, part(i, chip, cc), part(i, chip, cc), me).wait_recv()
                cp = copy(i, 3 + k, part(i, chip, cc), part(i, chip, cc), sibling)
                cp.start()
                sends.append(cp)
        for i in range(n):
            for k, chip in enumerate(chips):
                copy(i, 3 + k, part(i, chip, 1 - cc), part(i, chip, 1 - cc), me).wait_recv()
            copy(i, 6, own(i), own(i), me).wait_recv()
        for cp in sends:
            cp.wait_send()

    return pl.pallas_call(
        body, name=name, in_specs=[HBM_SPEC] * n, out_specs=[HBM_SPEC] * n,
        out_shape=[out_shape(w, kd) for w, kd in zip(shards, kinds)],
        scratch_shapes=[pltpu.SemaphoreType.DMA((n, 7)), pltpu.SemaphoreType.DMA((n, 7))],
    )(*shards)


SEM_SPEC = pl.BlockSpec(memory_space=pltpu.SEMAPHORE)
SPLIT_COPY = pltpu.CompilerParams(has_side_effects=pltpu.SideEffectType.DATAFLOW_SIDE_EFFECTING)


def _land_view(ref, kind, j, a, b, rows=None):
    r0, nr = (0, a) if rows is None else rows
    if kind == STK:
        return ref.at[j, pl.ds(r0, nr), :]
    if kind == ROW:
        return ref.at[pl.ds(pl.multiple_of(j * a + r0, 8), nr), :]
    return ref.at[pl.ds(r0, nr), pl.ds(pl.multiple_of(j * b, LANE), b)]


def _layer_copies(w_refs, land_refs, kinds, layer, send_sems, recv_sems):
    x, y, cc = _mesh_pos()
    place = 2 * x + y
    chips = [(1 - x, y), (x, 1 - y), (1 - x, 1 - y)]
    out, inc = [], []
    for i, (w_ref, land, kind) in enumerate(zip(w_refs, land_refs, kinds)):
        _, a, b = w_ref.shape
        half = (cc * (a // 2), a // 2)
        desc = lambda k, src, dst, to: pltpu.make_async_remote_copy(
            src_ref=src, dst_ref=dst, send_sem=send_sems.at[4 * i + k], recv_sem=recv_sems.at[4 * i + k], device_id=to,
            device_id_type=MESH)
        for k, chip in enumerate(chips):
            out.append(desc(k, w_ref.at[layer, pl.ds(half[0], half[1]), :], _land_view(land, kind, place, a, b, half),
                            (*chip, cc)))
            theirs = _land_view(land, kind, 2 * chip[0] + chip[1], a, b, half)
            inc.append(desc(k, theirs, theirs, (*chip, cc)))
        out.append(desc(3, w_ref.at[layer], _land_view(land, kind, place, a, b), (x, y, 1 - cc)))
        own = _land_view(land, kind, place, a, b)
        inc.append(desc(3, own, own, (x, y, 1 - cc)))
    return out, inc


def _land_shape(w, kind):
    _, a, b = w.shape
    return {ROW: (N_CHIPS * a, b), COL: (a, N_CHIPS * b), STK: (N_CHIPS, a, b)}[kind]


def _ag_start(shards, kinds, layer, after, *, name):
    n = len(shards)
    lands = [pltpu.with_memory_space_constraint(lax.empty(_land_shape(w, kd), w.dtype), pltpu.HBM)
             for w, kd in zip(shards, kinds)]

    def body(*refs):
        w_refs = refs[:n]
        send_sems, recv_sems = refs[2 * n + 1], refs[2 * n + 2]
        land_refs = refs[2 * n + 3:3 * n + 3]
        token = refs[3 * n + 3]
        out, _ = _layer_copies(w_refs, land_refs, kinds, layer, send_sems, recv_sems)
        for cp in out:
            cp.start()
        token[...] = jnp.zeros_like(token)

    after = jnp.zeros((8, LANE), F32) if after is None else after
    res = pl.pallas_call(
        body, name=name,
        in_specs=[HBM_SPEC] * (2 * n + 1),
        out_specs=[SEM_SPEC, SEM_SPEC] + [HBM_SPEC] * n + [pl.BlockSpec(memory_space=pltpu.VMEM)],
        out_shape=[pltpu.SemaphoreType.DMA((4 * n,)), pltpu.SemaphoreType.DMA((4 * n,))]
        + [jax.ShapeDtypeStruct(ld.shape, ld.dtype) for ld in lands] + [jax.ShapeDtypeStruct((8, LANE), F32)],
        input_output_aliases={n + i: 2 + i for i in range(n)},
        compiler_params=SPLIT_COPY,
    )(*shards, *lands, after)
    return res[0], res[1], list(res[2:2 + n]), res[2 + n]


def _ag_wait(send_sems, recv_sems, shards, lands, kinds, layer, after, *, name):
    n = len(shards)

    def body(*refs):
        w_refs, land_in = refs[:n], refs[n:2 * n]
        send_sems, recv_sems = refs[2 * n], refs[2 * n + 1]
        out, inc = _layer_copies(w_refs, land_in, kinds, layer, send_sems, recv_sems)
        for cp in out:
            cp.wait_send()
        for cp in inc:
            cp.wait_recv()

    return list(pl.pallas_call(
        body, name=name,
        in_specs=[HBM_SPEC] * (2 * n) + [SEM_SPEC, SEM_SPEC, HBM_SPEC],
        out_specs=[HBM_SPEC] * n,
        out_shape=[jax.ShapeDtypeStruct(ld.shape, ld.dtype) for ld in lands],
        input_output_aliases={n + i: i for i in range(n)},
        compiler_params=SPLIT_COPY,
    )(*shards, *lands, send_sems, recv_sems, after))


def _ag_forward(lands, kinds, shard_shapes, *, name):
    n = len(lands)

    def body(*refs):
        land_refs = refs[n:2 * n]
        send_sems, recv_sems = refs[2 * n:]
        x, y, cc = _mesh_pos()
        chips = [(1 - x, y), (x, 1 - y), (1 - x, 1 - y)]
        sends, waits = [], []
        for i, (land, kind) in enumerate(zip(land_refs, kinds)):
            a, b = shard_shapes[i]
            for k, chip in enumerate(chips):
                j = 2 * chip[0] + chip[1]
                desc = lambda view: pltpu.make_async_remote_copy(
                    src_ref=view, dst_ref=view, send_sem=send_sems.at[i, k], recv_sem=recv_sems.at[i, k],
                    device_id=(x, y, 1 - cc), device_id_type=MESH)
                sends.append(desc(_land_view(land, kind, j, a, b, (cc * (a // 2), a // 2))))
                waits.append(desc(_land_view(land, kind, j, a, b, ((1 - cc) * (a // 2), a // 2))))
        for cp in sends:
            cp.start()
        for cp in waits:
            cp.wait_recv()
        for cp in sends:
            cp.wait_send()

    return list(pl.pallas_call(
        body, name=name, in_specs=[HBM_SPEC] * n, out_specs=[HBM_SPEC] * n,
        out_shape=[jax.ShapeDtypeStruct(ld.shape, ld.dtype) for ld in lands],
        input_output_aliases={i: i for i in range(n)},
        scratch_shapes=[pltpu.SemaphoreType.DMA((n, 3)), pltpu.SemaphoreType.DMA((n, 3))],
    )(*lands))


def _layers_half(ref, kind, start, lh):
    return ref.at[:, pl.ds(start, lh)] if kind == STK else ref.at[pl.ds(start, lh)]


def _rs_pair(gs, kinds, *, name):
    n = len(gs)

    def half_shape(g, kind):
        s = list(g.shape)
        s[1 if kind == STK else 0] //= 2
        return jax.ShapeDtypeStruct(tuple(s), g.dtype)

    def body(*refs):
        g_refs, buf_refs = refs[:n], refs[n:2 * n]
        send_sems, recv_sems = refs[2 * n:]
        x, y, cc = _mesh_pos()
        cps = []
        for i in range(n):
            lh = buf_refs[i].shape[1 if kinds[i] == STK else 0]
            cp = pltpu.make_async_remote_copy(src_ref=_layers_half(g_refs[i], kinds[i], (1 - cc) * lh, lh),
                                              dst_ref=buf_refs[i], send_sem=send_sems.at[i], recv_sem=recv_sems.at[i],
                                              device_id=(x, y, 1 - cc), device_id_type=MESH)
            cp.start()
            cps.append(cp)
        for cp in cps:
            cp.wait()

    return pl.pallas_call(
        body, name=name, in_specs=[HBM_SPEC] * n, out_specs=[HBM_SPEC] * n,
        out_shape=[half_shape(g, kd) for g, kd in zip(gs, kinds)],
        scratch_shapes=[pltpu.SemaphoreType.DMA((n,)), pltpu.SemaphoreType.DMA((n,))],
    )(*gs)


def _row_tile(rows, cols):
    return _pick(rows, max(8, (512 * 1024 // cols) // 8 * 8), 8)


def _rs_pair_add(g, buf, kind, cc, out_dtype, *, name):
    cols = g.shape[-1]
    pre = g.shape[0] if kind == STK else 1
    rows = buf.size // (pre * cols)
    tr = _row_tile(rows, cols)

    def body(cc_ref, g_ref, b_ref, o_ref):
        o_ref[...] = (g_ref[...].astype(F32) + b_ref[...].astype(F32)).astype(o_ref.dtype)

    out = pl.pallas_call(
        body, name=name,
        grid_spec=pltpu.PrefetchScalarGridSpec(
            num_scalar_prefetch=1, grid=(pre, rows // tr),
            in_specs=[pl.BlockSpec((None, None, tr, cols), lambda s, i, cc_ref: (s, cc_ref[0], i, 0)),
                      pl.BlockSpec((None, tr, cols), lambda s, i, cc_ref: (s, i, 0))],
            out_specs=pl.BlockSpec((None, tr, cols), lambda s, i, cc_ref: (s, i, 0))),
        out_shape=jax.ShapeDtypeStruct((pre, rows, cols), out_dtype),
        compiler_params=_cparams(("parallel", "parallel")),
    )(cc.reshape(1).astype(jnp.int32), g.reshape(pre, 2, rows, cols), buf.reshape(pre, rows, cols))
    return out.reshape(buf.shape)


def _rs_cross(ps, kinds, shard_shapes, *, name):
    n = len(ps)

    def body(*refs):
        p_refs, out_refs = refs[:n], refs[n:2 * n]
        send_sems, recv_sems, local_sems = refs[2 * n:]
        x, y, cc = _mesh_pos()
        chips = [(1 - x, y), (x, 1 - y), (1 - x, 1 - y)]
        slot = lambda chip: 2 * chip[0] + chip[1]
        local, sends = [], []
        for i in range(n):
            a, b = shard_shapes[i]
            if kinds[i] == REP:
                cp = pltpu.make_async_copy(p_refs[i], out_refs[i].at[slot((x, y))], local_sems.at[i])
                cp.start()
                local.append(cp)
            for k, chip in enumerate(chips):
                cp = pltpu.make_async_remote_copy(src_ref=_chip_view(p_refs[i], kinds[i], slot(chip), a, b),
                                                  dst_ref=out_refs[i].at[slot((x, y))], send_sem=send_sems.at[i, k],
                                                  recv_sem=recv_sems.at[i, k], device_id=(*chip, cc),
                                                  device_id_type=MESH)
                cp.start()
                sends.append(cp)
        for i in range(n):
            for k, chip in enumerate(chips):
                landed = out_refs[i].at[slot(chip)]
                pltpu.make_async_remote_copy(src_ref=landed, dst_ref=landed, send_sem=send_sems.at[i, k],
                                             recv_sem=recv_sems.at[i, k], device_id=(*chip, cc),
                                             device_id_type=MESH).wait_recv()
        for cp in sends:
            cp.wait_send()
        for cp in local:
            cp.wait()

    def out_shape(p, kind, ab):
        lh = p.shape[1 if kind == STK else 0]
        return jax.ShapeDtypeStruct((N_CHIPS, lh) + tuple(ab), p.dtype)

    return pl.pallas_call(
        body, name=name, in_specs=[HBM_SPEC] * n, out_specs=[HBM_SPEC] * n,
        out_shape=[out_shape(p, kd, ab) for p, kd, ab in zip(ps, kinds, shard_shapes)],
        scratch_shapes=[pltpu.SemaphoreType.DMA((n, 3)), pltpu.SemaphoreType.DMA((n, 3)),
                        pltpu.SemaphoreType.DMA((n,))],
    )(*ps)


def _cross_copies(p_refs, land_refs, kinds, shard_shapes, send_sems, recv_sems):
    x, y, cc = _mesh_pos()
    chips = [(1 - x, y), (x, 1 - y), (1 - x, 1 - y)]
    mine = 2 * x + y
    out, inc = [], []
    for i, (p_ref, land, kind) in enumerate(zip(p_refs, land_refs, kinds)):
        a, b = shard_shapes[i]
        for k, chip in enumerate(chips):
            j = 2 * chip[0] + chip[1]
            desc = lambda src, dst: pltpu.make_async_remote_copy(
                src_ref=src, dst_ref=dst, send_sem=send_sems.at[3 * i + k], recv_sem=recv_sems.at[3 * i + k],
                device_id=(*chip, cc), device_id_type=MESH)
            out.append(desc(_chip_view(p_ref, kind, j, a, b), land.at[mine]))
            inc.append(desc(land.at[j], land.at[j]))
    return out, inc


def _rs_cross_start(ps, kinds, shard_shapes, *, name):
    n = len(ps)
    lands = []
    for p, kd, ab in zip(ps, kinds, shard_shapes):
        lh = p.shape[1 if kd == STK else 0]
        lands.append(pltpu.with_memory_space_constraint(lax.empty((N_CHIPS, lh) + tuple(ab), p.dtype), pltpu.HBM))

    def body(*refs):
        p_refs = refs[:n]
        send_sems, recv_sems = refs[2 * n], refs[2 * n + 1]
        land_refs = refs[2 * n + 2:3 * n + 2]
        token = refs[3 * n + 2]
        out, _ = _cross_copies(p_refs, land_refs, kinds, shard_shapes, send_sems, recv_sems)
        for cp in out:
            cp.start()
        token[...] = jnp.zeros_like(token)

    res = pl.pallas_call(
        body, name=name, in_specs=[HBM_SPEC] * (2 * n),
        out_specs=[SEM_SPEC, SEM_SPEC] + [HBM_SPEC] * n + [pl.BlockSpec(memory_space=pltpu.VMEM)],
        out_shape=[pltpu.SemaphoreType.DMA((3 * n,)), pltpu.SemaphoreType.DMA((3 * n,))]
        + [jax.ShapeDtypeStruct(ld.shape, ld.dtype) for ld in lands] + [jax.ShapeDtypeStruct((8, LANE), F32)],
        input_output_aliases={n + i: 2 + i for i in range(n)}, compiler_params=SPLIT_COPY,
    )(*ps, *lands)
    return res[0], res[1], list(res[2:2 + n]), res[2 + n]


def _rs_cross_wait(send_sems, recv_sems, ps, lands, kinds, shard_shapes, after, *, name):
    n = len(ps)

    def body(*refs):
        p_refs, land_in = refs[:n], refs[n:2 * n]
        out, inc = _cross_copies(p_refs, land_in, kinds, shard_shapes, refs[2 * n], refs[2 * n + 1])
        for cp in out:
            cp.wait_send()
        for cp in inc:
            cp.wait_recv()

    return list(pl.pallas_call(
        body, name=name, in_specs=[HBM_SPEC] * (2 * n) + [SEM_SPEC, SEM_SPEC, HBM_SPEC], out_specs=[HBM_SPEC] * n,
        out_shape=[jax.ShapeDtypeStruct(ld.shape, ld.dtype) for ld in lands],
        input_output_aliases={n + i: i for i in range(n)}, compiler_params=SPLIT_COPY,
    )(*ps, *lands, send_sems, recv_sems, after))


def _rs_sum(p, landed, kind, ab, place, cc, into, *, name):
    a, b = ab
    lh = landed.shape[1]
    tr = _row_tile(a, b)
    blk = lambda f: pl.BlockSpec((None, None, tr, b), f)
    if kind == ROW:
        p_spec = pl.BlockSpec((None, tr, b), lambda l, i, w, c: (l, w[0] * (a // tr) + i, 0))
    elif kind == COL:
        p_spec = pl.BlockSpec((None, tr, b), lambda l, i, w, c: (l, i, w[0]))
    elif kind == STK:
        p_spec = blk(lambda l, i, w, c: (w[0], l, i, 0))
    else:
        p_spec = blk(lambda l, i, w, c: (0, l, i, 0))
        p = landed
    if kind == REP:
        others = [blk(lambda l, i, w, c, k=k: (k, l, i, 0)) for k in (1, 2, 3)]
    else:
        others = [blk(lambda l, i, w, c: (lax.rem(w[0] + 2, 4), l, i, 0)),
                  blk(lambda l, i, w, c: (w[0] + 1 - 2 * lax.rem(w[0], 2), l, i, 0)),
                  blk(lambda l, i, w, c: (3 - w[0], l, i, 0))]

    def body(w_ref, c_ref, p_ref, b1_ref, b2_ref, b3_ref, *rest):
        o_ref = rest[-1]
        f = lambda r: r[...].astype(F32)
        o_ref[...] = ((f(p_ref) + f(b1_ref)) + f(b2_ref)) + f(b3_ref)

    base, total, prev = into
    operands = [place, cc, p, landed, landed, landed] + ([] if prev is None else [prev])
    return pl.pallas_call(
        body, name=name,
        grid_spec=pltpu.PrefetchScalarGridSpec(
            num_scalar_prefetch=2, grid=(lh, a // tr),
            in_specs=[p_spec] + others + ([] if prev is None else [HBM_SPEC]),
            out_specs=pl.BlockSpec((None, tr, b), lambda l, i, w, c: (base + c[0] * lh + l, i, 0))),
        out_shape=jax.ShapeDtypeStruct((total, a, b), F32),
        input_output_aliases={} if prev is None else {6: 0},
        compiler_params=_cparams(("parallel", "parallel")),
    )(*operands)


def _rs_share(fs, spans, *, name):
    n = len(fs)

    def body(*refs):
        out_refs = refs[n:2 * n]
        send_sems, recv_sems = refs[2 * n:]
        x, y, cc = _mesh_pos()
        cps = []
        for i in range(n):
            base, lh = spans[i]
            mine = out_refs[i].at[pl.ds(base + cc * lh, lh)]
            cp = pltpu.make_async_remote_copy(src_ref=mine, dst_ref=mine, send_sem=send_sems.at[i],
                                              recv_sem=recv_sems.at[i], device_id=(x, y, 1 - cc),
                                              device_id_type=MESH)
            cp.start()
            cps.append(cp)
        for i in range(n):
            base, lh = spans[i]
            theirs = out_refs[i].at[pl.ds(base + (1 - cc) * lh, lh)]
            pltpu.make_async_remote_copy(src_ref=theirs, dst_ref=theirs, send_sem=send_sems.at[i],
                                         recv_sem=recv_sems.at[i], device_id=(x, y, 1 - cc),
                                         device_id_type=MESH).wait_recv()
        for cp in cps:
            cp.wait_send()

    return pl.pallas_call(
        body, name=name, in_specs=[HBM_SPEC] * n, out_specs=[HBM_SPEC] * n,
        out_shape=[jax.ShapeDtypeStruct(f.shape, f.dtype) for f in fs],
        input_output_aliases={i: i for i in range(n)},
        scratch_shapes=[pltpu.SemaphoreType.DMA((n,)), pltpu.SemaphoreType.DMA((n,))],
    )(*fs)


def _reduce_scatter(gs, kinds, shard_shapes, wire_dtypes, intos=None):
    ps = _rs_pair_sums(gs, kinds, wire_dtypes, "")
    landed = _rs_cross(ps, kinds, shard_shapes, name="rs_cross")
    return _rs_finish(ps, landed, kinds, shard_shapes, "", intos)


def _rs_pair_sums(gs, kinds, wire_dtypes, tag):
    cc = lax.axis_index("c")
    bufs = _rs_pair(gs, kinds, name="rs_pair" + tag)
    return [_rs_pair_add(g, buf, kd, cc, wd, name=f"rs_pair_add{tag}_{i}")
            for i, (g, buf, kd, wd) in enumerate(zip(gs, bufs, kinds, wire_dtypes))]


def _rs_finish(ps, landed, kinds, shard_shapes, tag, intos=None):
    x, y, cc = _mesh_pos()
    place = (2 * x + y).reshape(1).astype(jnp.int32)
    cc1 = cc.reshape(1).astype(jnp.int32)
    if intos is None:
        intos = [(0, 2 * b.shape[1], None) for b in landed]
    fs = [_rs_sum(p, b, kd, ab, place, cc1, into, name=f"rs_sum{tag}_{i}")
          for i, (p, b, kd, ab, into) in enumerate(zip(ps, landed, kinds, shard_shapes, intos))]
    return _rs_share(fs, [(into[0], b.shape[1]) for into, b in zip(intos, landed)], name="rs_share" + tag)


def _reduce_scatter_begin(gs, kinds, shard_shapes, wire_dtypes):
    ps = _rs_pair_sums(gs, kinds, wire_dtypes, "_early")
    send, recv, lands, token = _rs_cross_start(ps, kinds, shard_shapes, name="rs_cross_start")
    return ps, send, recv, lands, token


def _reduce_scatter_end(state, kinds, shard_shapes, after, intos):
    ps, send, recv, lands, _ = state
    landed = _rs_cross_wait(send, recv, ps, lands, kinds, shard_shapes, after, name="rs_cross_wait")
    return _rs_finish(ps, landed, kinds, shard_shapes, "_early", intos)


def _shard_shape(axis, r, c):
    return (r // N_CHIPS, c) if axis == 0 else (r, c // N_CHIPS)


def _unstack(stacked):
    ns, depth, r, cs = stacked.shape
    return stacked.transpose(1, 2, 0, 3).reshape(depth, r, ns * cs)


def _stack(fullw):
    depth, r, c = fullw.shape
    return fullw.reshape(depth, r, N_CHIPS, c // N_CHIPS).transpose(2, 0, 1, 3)


REP_SIZES = dict(mix_norm_g=1024, ssd_conv_b=2048, ssd_dt_bias=16, ssd_a_log=16, ssd_d=16, ssd_norm_g=1024,
                 conv_dw_b=1024, conv_ln_g=1024, conv_ln_b=1024, mla_q_a_g=384, mla_kv_a_g=256, mla_q_norm_g=192,
                 mla_k_norm_g=192, xattn_norm_g=1024, mem_norm_g=1024, xattn_q_norm_g=256, xattn_k_norm_g=256,
                 ffn_norm_g=1024)
REP_ROWS = 8
REP_COLS = -(-sum(REP_SIZES.values()) // (REP_ROWS * LANE)) * LANE
SMALL_COLS = 256
SMALL_ROWS = 48


def _pack_rep(d):
    flat = jnp.concatenate([d[k] for k in REP_NAMES], axis=1)
    flat = jnp.pad(flat, ((0, 0), (0, REP_ROWS * REP_COLS - flat.shape[1])))
    return flat.reshape(flat.shape[0], REP_ROWS, REP_COLS)


def _unpack_rep(packed):
    out, off = {}, 0
    flat = packed.reshape(packed.shape[0], -1)
    for k in REP_NAMES:
        out[k] = flat[:, off:off + REP_SIZES[k]]
        off += REP_SIZES[k]
    return out


def _pack_small(d):
    depth = d['gate_b'].shape[0]
    rows = jnp.concatenate([d[k].reshape(depth, -1, SMALL_COLS) for k, _, _, _ in SMALL], axis=1)
    return jnp.pad(rows, ((0, 0), (0, SMALL_ROWS - rows.shape[1]), (0, 0)))


def _unpack_small(packed):
    out, off = {}, 0
    for k, axis, r, c in SMALL:
        rs, cs = _shard_shape(axis, r, c)
        n = rs * cs // SMALL_COLS
        out[k] = packed[:, off:off + n].reshape(packed.shape[0], rs, cs)
        off += n
    return out


WEIGHT_NAMES = ('mix_norm_g', 'w_in', 'ssd_conv_w', 'ssd_conv_b', 'ssd_dt_bias', 'ssd_a_log', 'ssd_d', 'ssd_norm_g',
                'ssd_w_out', 'conv_dw_w', 'conv_dw_b', 'conv_ln_g', 'conv_ln_b', 'conv_w_out', 'mla_q_a_g',
                'mla_w_q_b', 'mla_kv_a_g', 'mla_w_kv_b', 'mla_q_norm_g', 'mla_k_norm_g', 'mla_w_o', 'gate_b', 'w_out',
                'xattn_norm_g', 'mem_norm_g', 'xattn_w_q', 'xattn_w_kv', 'xattn_q_norm_g', 'xattn_k_norm_g',
                'xattn_w_o', 'ffn_norm_g', 'ffn_w_in', 'ffn_w_out')


def kernel(x, mem, positions, *rest):
    nw = len(WEIGHT_NAMES)
    weights = dict(zip(WEIGHT_NAMES, rest[:nw]))
    target = rest[nw]
    mom_m = dict(zip(WEIGHT_NAMES, rest[nw + 1:2 * nw + 1]))
    mom_v = dict(zip(WEIGHT_NAMES, rest[2 * nw + 1:3 * nw + 1]))
    depth = weights['mix_norm_g'].shape[0]

    kinds = [_kind(axis, _shard_shape(axis, r, c)[1]) for _, axis, r, c in BIG]
    shard_shapes = [_shard_shape(axis, r, c) for _, axis, r, c in BIG]
    nb = len(BIG)
    small_all, = _all_gather([_pack_small(weights)], [STK], name="ag_small")
    small_chips = [_unpack_small(small_all[j]) for j in range(N_CHIPS)]
    small = {k: jnp.concatenate([sc[k] for sc in small_chips], axis=2) for k, _, _, _ in SMALL}
    big_shards = [weights[k].astype(BF16) for k, _, _, _ in BIG]
    send0, recv0, lands0, _ = _ag_start(big_shards, kinds[:nb], 0, None, name="ag_start_0")
    lands0 = _ag_wait(send0, recv0, big_shards, lands0, kinds[:nb], 0, big_shards[0], name="ag_wait_0")
    lands0 = _ag_forward(lands0, kinds[:nb], shard_shapes[:nb], name="ag_forward_0")
    started = {l: _ag_start(big_shards, kinds[:nb], l, lands0[0], name=f"ag_start_{l}") for l in range(1, depth)}
    tokens = sum(st[3][0, 0] for st in started.values())

    def layer_weights(l, h):
        if l == 0:
            lands = lands0
        else:
            send, recv, lands, _ = started[l]
            lands = _ag_wait(send, recv, big_shards, lands, kinds[:nb], l, h, name=f"ag_wait_{l}")
            lands = _ag_forward(lands, kinds[:nb], shard_shapes[:nb], name=f"ag_forward_{l}")
        full = {k: (ld.transpose(1, 0, 2).reshape(ld.shape[1], -1) if kd == STK else ld)
                for (k, _, _, _), kd, ld in zip(BIG, kinds[:nb], lands)}
        full.update({k: w[l] for k, w in small.items()})
        full['w_in'] = _w_in_pad(full['w_in'])
        full['mla_w_q_b'] = _qb_pad(full['mla_w_q_b'])
        return full

    rep = {k: weights[k] for k in REP_NAMES}
    top = depth // 2
    by_chip = lambda g: [(_stack(g[k]) if kd == STK else g[k]) for (k, _, _, _), kd in zip(BIG, kinds)]
    early = []

    def on_layer(l, stacks):
        if l == top:
            g = {k: stacks[k][top:] for k, _, _, _ in BIG}
            g['w_in'] = _w_in_unpad(g['w_in'])
            g['mla_w_q_b'] = _qb_unpad(g['mla_w_q_b'])
            early.append(_reduce_scatter_begin(by_chip(g), kinds, shard_shapes, [BF16] * nb))
            return early[0][4][0, 0]

    loss, dx, gfull, grep = _local_step(x[0] + tokens, mem[0], positions[0], target[0], layer_weights, rep, on_layer)
    loss = lax.psum(loss, ("x", "y", "c"))
    upper = _reduce_scatter_end(early[0], kinds, shard_shapes, dx, [(top, depth, None)] * nb)

    gs = by_chip({k: gfull[k][:top] for k, _, _, _ in BIG})
    small_split = {k: _stack(gfull[k]) for k, _, _, _ in SMALL}
    gs.append(jnp.stack([_pack_small({k: v[j] for k, v in small_split.items()}) for j in range(N_CHIPS)]))
    gs.append(_pack_rep(grep))
    summed = _reduce_scatter(gs, kinds + [STK, REP], shard_shapes + [(SMALL_ROWS, SMALL_COLS), (REP_ROWS, REP_COLS)],
                             [BF16] * nb + [F32, F32],
                             [(0, depth, up) for up in upper] + [(0, depth, None)] * 2)
    grads = {k: g for (k, _, _, _), g in zip(BIG, summed[:nb])}
    grads.update(_unpack_small(summed[nb]))
    rep_sum = summed[nb + 1]
    grads.update(_unpack_rep(rep_sum))

    delta, new_m, new_v = {}, {}, {}
    for k, _, _, _ in BIG + SMALL:
        w = weights[k]
        two_d = (w.shape[0] * w.shape[1], w.shape[2])
        d_, m_, v_ = _adamw(w.reshape(two_d), grads[k].reshape(two_d), mom_m[k].reshape(two_d),
                            mom_v[k].reshape(two_d), name="adamw_" + k)
        delta[k], new_m[k], new_v[k] = d_.reshape(w.shape), m_.reshape(w.shape), v_.reshape(w.shape)
    pack2 = lambda d: _pack_rep(d).reshape(depth, -1)
    d_, m_, v_ = _adamw(pack2(rep), rep_sum.reshape(depth, -1), pack2({k: mom_m[k] for k in REP_NAMES}),
                        pack2({k: mom_v[k] for k in REP_NAMES}), name="adamw_rep")
    delta.update(_unpack_rep(d_))
    new_m.update(_unpack_rep(m_))
    new_v.update(_unpack_rep(v_))

    return (loss, dx[None], *[grads[k] for k in WEIGHT_NAMES], *[delta[k] for k in WEIGHT_NAMES],
            *[new_m[k] for k in WEIGHT_NAMES], *[new_v[k] for k in WEIGHT_NAMES])
```

```python
import functools
import math

import jax
import jax.numpy as jnp
import numpy as np
from jax import lax
from jax.experimental import pallas as pl
from jax.experimental.pallas import tpu as pltpu

F32 = jnp.float32
BF16 = jnp.bfloat16
MESH = pl.DeviceIdType.MESH

EPS = 1e-6
CHUNK = 64
SSD_HEADS = 16
SSD_GROUPS = 4
SSD_P = 64
SSD_N = 128
MLA_HEADS = 8
MLA_NOPE = 128
MLA_ROPE = 64
MLA_V = 128
MLA_HP = 256
X_HEADS = 4
ROPE_THETA = 10000.0
ADAM_LR, ADAM_B1, ADAM_B2, ADAM_EPS, ADAM_WD, ADAM_STEP = 0.001, 0.9, 0.999, 1e-08, 0.01, 10
LANE = 128
NEG = -1e30
VMEM_MB = 1024 * 1024


def _pick(n, cap, mult=128):
    if n <= cap:
        return n
    d = (cap // mult) * mult
    while d >= mult:
        if n % d == 0:
            return d
        d -= mult
    return n


def _cparams(sem, mb=40):
    return pltpu.CompilerParams(dimension_semantics=sem, vmem_limit_bytes=mb * VMEM_MB)


def _sigmoid(x):
    return 1.0 / (1.0 + jnp.exp(-x))


def _silu(x):
    return x * _sigmoid(x)


def _dsilu(x):
    s = _sigmoid(x)
    return s * (1.0 + x * (1.0 - s))


def _softplus(x):
    return jnp.maximum(x, 0.0) + jnp.log(1.0 + jnp.exp(-jnp.abs(x)))


def _matmul(a, b, *, ta=False, tb=False, out_dtype=F32, add=None, into=None, name):
    if ta:
        kdim, m = a.shape
    else:
        m, kdim = a.shape
    if tb:
        n, kb = b.shape
    else:
        kb, n = b.shape
    assert kb == kdim, (a.shape, b.shape, ta, tb)
    tm = _pick(m, 512, 128) if ta else _pick(m, 1024, 8)
    tn = _pick(n, 1024, 128)
    tk = _pick(kdim, 2048, 128)
    nk = kdim // tk
    dims = (((0 if ta else 1,), (1 if tb else 0,)), ((), ()))

    has_add = add is not None
    has_stack = into is not None and into[0] is not None

    def body(a_ref, b_ref, *rest):
        add_ref = rest[0] if has_add else None
        o_ref = rest[has_add + has_stack]
        acc = rest[has_add + has_stack + 1:]
        part = lax.dot_general(a_ref[...].astype(BF16), b_ref[...].astype(BF16), dims,
                               preferred_element_type=F32)

        def finish(total):
            if has_add:
                total = total + add_ref[...]
            o_ref[...] = total.astype(o_ref.dtype)

        if nk == 1:
            finish(part)
        else:
            acc_ref, = acc
            k = pl.program_id(2)

            @pl.when(k == 0)
            def _():
                acc_ref[...] = part

            @pl.when(k > 0)
            def _():
                acc_ref[...] += part

            @pl.when(k == nk - 1)
            def _():
                finish(acc_ref[...])

    a_spec = pl.BlockSpec((tk, tm), lambda i, j, k: (k, i)) if ta else pl.BlockSpec((tm, tk), lambda i, j, k: (i, k))
    b_spec = pl.BlockSpec((tn, tk), lambda i, j, k: (j, k)) if tb else pl.BlockSpec((tk, tn), lambda i, j, k: (k, j))
    o_spec = pl.BlockSpec((tm, tn), lambda i, j, k: (i, j))
    operands = [a, b] + ([add] if has_add else [])
    in_specs = [a_spec, b_spec] + ([o_spec] if has_add else [])
    if into is None:
        out_spec, out_shape, aliases = o_spec, jax.ShapeDtypeStruct((m, n), out_dtype), {}
    else:
        stack, layer, depth = into
        out_spec = pl.BlockSpec((None, tm, tn), lambda i, j, k: (layer, i, j))
        out_shape = jax.ShapeDtypeStruct((depth, m, n), out_dtype)
        aliases = {}
        if stack is not None:
            aliases = {len(operands): 0}
            operands.append(stack)
            in_specs.append(HBM_SPEC)
    return pl.pallas_call(
        body, name=name, grid=(m // tm, n // tn, nk),
        in_specs=in_specs, out_specs=out_spec, out_shape=out_shape, input_output_aliases=aliases,
        scratch_shapes=[] if nk == 1 else [pltpu.VMEM((tm, tn), F32)],
        compiler_params=_cparams(("parallel", "parallel", "arbitrary"), 48),
    )(*operands)


def _rms_fwd(x, g, *, out_dtype, name):
    r, w = x.shape
    tr = _pick(r, 512, 8)

    def body(x_ref, g_ref, o_ref):
        xv = x_ref[...]
        rstd = lax.rsqrt(jnp.mean(xv * xv, axis=-1, keepdims=True) + EPS)
        o_ref[...] = (xv * rstd * g_ref[...]).astype(o_ref.dtype)

    return pl.pallas_call(
        body, name=name, grid=(r // tr,),
        in_specs=[pl.BlockSpec((tr, w), lambda i: (i, 0)), pl.BlockSpec((1, w), lambda i: (0, 0))],
        out_specs=pl.BlockSpec((tr, w), lambda i: (i, 0)),
        out_shape=jax.ShapeDtypeStruct((r, w), out_dtype),
        compiler_params=_cparams(("parallel",)),
    )(x, g)


def _rms_bwd(x, g, dy, *, dx_dtype, name, add=None):
    r, w = x.shape
    tr = _pick(r, 512, 8)
    has_add = add is not None

    def body(x_ref, g_ref, dy_ref, *rest):
        if has_add:
            add_ref, dx_ref, dg_ref = rest
        else:
            dx_ref, dg_ref = rest
        xv = x_ref[...]
        dyv = dy_ref[...].astype(F32)
        rstd = lax.rsqrt(jnp.mean(xv * xv, axis=-1, keepdims=True) + EPS)
        xh = xv * rstd
        dyg = dyv * g_ref[...]
        dx = rstd * (dyg - xh * jnp.mean(dyg * xh, axis=-1, keepdims=True))
        if has_add:
            dx = dx + add_ref[...]
        dx_ref[...] = dx.astype(dx_ref.dtype)
        part = jnp.sum(dyv * xh, axis=0, keepdims=True)

        @pl.when(pl.program_id(0) == 0)
        def _():
            dg_ref[...] = part

        @pl.when(pl.program_id(0) > 0)
        def _():
            dg_ref[...] += part

    row = pl.BlockSpec((tr, w), lambda i: (i, 0))
    vec = pl.BlockSpec((1, w), lambda i: (0, 0))
    ins = [x, g, dy] + ([add] if has_add else [])
    return pl.pallas_call(
        body, name=name, grid=(r // tr,),
        in_specs=[row, vec, row] + ([row] if has_add else []),
        out_specs=[row, vec],
        out_shape=[jax.ShapeDtypeStruct((r, w), dx_dtype), jax.ShapeDtypeStruct((1, w), F32)],
        compiler_params=_cparams(("arbitrary",)),
    )(*ins)


def _ln_silu_fwd(x, g, b, *, name):
    r, w = x.shape
    tr = _pick(r, 512, 8)

    def body(x_ref, g_ref, b_ref, o_ref):
        xv = x_ref[...]
        mu = jnp.mean(xv, axis=-1, keepdims=True)
        xc = xv - mu
        rstd = lax.rsqrt(jnp.mean(xc * xc, axis=-1, keepdims=True) + EPS)
        o_ref[...] = _silu(xc * rstd * g_ref[...] + b_ref[...]).astype(o_ref.dtype)

    row = pl.BlockSpec((tr, w), lambda i: (i, 0))
    vec = pl.BlockSpec((1, w), lambda i: (0, 0))
    return pl.pallas_call(
        body, name=name, grid=(r // tr,), in_specs=[row, vec, vec], out_specs=row,
        out_shape=jax.ShapeDtypeStruct((r, w), BF16), compiler_params=_cparams(("parallel",)),
    )(x, g, b)


def _ln_silu_bwd(x, g, b, dy, *, name):
    r, w = x.shape
    tr = _pick(r, 512, 8)

    def body(x_ref, g_ref, b_ref, dy_ref, dx_ref, dg_ref, db_ref):
        xv = x_ref[...]
        mu = jnp.mean(xv, axis=-1, keepdims=True)
        xc = xv - mu
        rstd = lax.rsqrt(jnp.mean(xc * xc, axis=-1, keepdims=True) + EPS)
        xh = xc * rstd
        pre = xh * g_ref[...] + b_ref[...]
        dpre = dy_ref[...].astype(F32) * _dsilu(pre)
        dxh = dpre * g_ref[...]
        dx_ref[...] = rstd * (dxh - jnp.mean(dxh, axis=-1, keepdims=True)
                              - xh * jnp.mean(dxh * xh, axis=-1, keepdims=True))
        pg = jnp.sum(dpre * xh, axis=0, keepdims=True)
        pb = jnp.sum(dpre, axis=0, keepdims=True)

        @pl.when(pl.program_id(0) == 0)
        def _():
            dg_ref[...] = pg
            db_ref[...] = pb

        @pl.when(pl.program_id(0) > 0)
        def _():
            dg_ref[...] += pg
            db_ref[...] += pb

    row = pl.BlockSpec((tr, w), lambda i: (i, 0))
    vec = pl.BlockSpec((1, w), lambda i: (0, 0))
    return pl.pallas_call(
        body, name=name, grid=(r // tr,), in_specs=[row, vec, vec, row], out_specs=[row, vec, vec],
        out_shape=[jax.ShapeDtypeStruct((r, w), F32), jax.ShapeDtypeStruct((1, w), F32),
                   jax.ShapeDtypeStruct((1, w), F32)],
        compiler_params=_cparams(("arbitrary",)),
    )(x, g, b, dy)


CONV_PAD = 32
CONV_T = 256


def _conv_fwd(src, w, b, *, glu, act, name):
    s = src.shape[0]
    k, c = w.shape
    tc = LANE
    ncb = c // tc
    tt = _pick(s, CONV_T, 8)
    assert k - 1 <= CONV_PAD

    def body(*refs):
        if glu:
            a_ref, g_ref, w_ref, b_ref = refs[:4]
            outs = refs[4:-1]
        else:
            a_ref, w_ref, b_ref = refs[:3]
            outs = refs[3:-1]
        xp = refs[-1]
        xp[0:CONV_PAD, :] = jnp.zeros((CONV_PAD, tc), F32)
        if glu:
            xp[CONV_PAD:CONV_PAD + s, :] = a_ref[...] * _sigmoid(g_ref[...])
        else:
            xp[CONV_PAD:CONV_PAD + s, :] = a_ref[...]
        wv = w_ref[...]
        bv = b_ref[...]
        for t0 in range(0, s, tt):
            acc = jnp.zeros((tt, tc), F32) + bv
            for kk in range(k):
                off = CONV_PAD + t0 - (k - 1) + kk
                acc = acc + wv[kk:kk + 1, :] * xp[off:off + tt, :]
            outs[0][t0:t0 + tt, :] = acc
            if act:
                outs[1][t0:t0 + tt, :] = _silu(acc)

    col = pl.BlockSpec((s, tc), lambda j: (0, j))
    in_specs = [col, pl.BlockSpec((s, tc), lambda j: (0, j + ncb))] if glu else [col]
    in_specs += [pl.BlockSpec((k, tc), lambda j: (0, j)), pl.BlockSpec((1, tc), lambda j: (0, j))]
    n_out = 2 if act else 1
    res = pl.pallas_call(
        body, name=name, grid=(ncb,), in_specs=in_specs,
        out_specs=[col] * n_out,
        out_shape=[jax.ShapeDtypeStruct((s, c), F32)] * n_out,
        scratch_shapes=[pltpu.VMEM((CONV_PAD + s, tc), F32)],
        compiler_params=_cparams(("parallel",), 48),
    )(*([src, src] if glu else [src]), w, b)
    return res


def _conv_bwd(src, w, b, dy, pre, *, glu, act, name):
    s = src.shape[0]
    k, c = w.shape
    tc = LANE
    ncb = c // tc
    tt = _pick(s, CONV_T, 8)

    def body(*refs):
        i = 0
        a_ref = refs[i]; i += 1
        if glu:
            g_ref = refs[i]; i += 1
        w_ref = refs[i]; i += 1
        dy_ref = refs[i]; i += 1
        if act:
            pre_ref = refs[i]; i += 1
        da_ref = refs[i]; i += 1
        if glu:
            dg_ref = refs[i]; i += 1
        dw_ref = refs[i]; db_ref = refs[i + 1]
        xp, dp = refs[-2], refs[-1]
        xp[0:CONV_PAD, :] = jnp.zeros((CONV_PAD, tc), F32)
        if glu:
            xp[CONV_PAD:CONV_PAD + s, :] = a_ref[...] * _sigmoid(g_ref[...])
        else:
            xp[CONV_PAD:CONV_PAD + s, :] = a_ref[...]
        dp[s:s + CONV_PAD, :] = jnp.zeros((CONV_PAD, tc), F32)
        if act:
            dp[0:s, :] = dy_ref[...].astype(F32) * _dsilu(pre_ref[...])
        else:
            dp[0:s, :] = dy_ref[...].astype(F32)
        wv = w_ref[...]
        dws = [jnp.zeros((1, tc), F32) for _ in range(k)]
        dbs = jnp.zeros((1, tc), F32)
        for t0 in range(0, s, tt):
            acc = jnp.zeros((tt, tc), F32)
            dcur = dp[t0:t0 + tt, :]
            dbs = dbs + jnp.sum(dcur, axis=0, keepdims=True)
            for kk in range(k):
                acc = acc + wv[kk:kk + 1, :] * dp[t0 + (k - 1) - kk:t0 + (k - 1) - kk + tt, :]
                off = CONV_PAD + t0 - (k - 1) + kk
                dws[kk] = dws[kk] + jnp.sum(dcur * xp[off:off + tt, :], axis=0, keepdims=True)
            if glu:
                av = a_ref[t0:t0 + tt, :]
                sg = _sigmoid(g_ref[t0:t0 + tt, :])
                da_ref[t0:t0 + tt, :] = (acc * sg).astype(da_ref.dtype)
                dg_ref[t0:t0 + tt, :] = (acc * av * sg * (1.0 - sg)).astype(dg_ref.dtype)
            else:
                da_ref[t0:t0 + tt, :] = acc.astype(da_ref.dtype)
        for kk in range(k):
            dw_ref[kk:kk + 1, :] = dws[kk]
        db_ref[...] = dbs

    col = pl.BlockSpec((s, tc), lambda j: (0, j))
    in_specs = [col] + ([pl.BlockSpec((s, tc), lambda j: (0, j + ncb))] if glu else [])
    in_specs += [pl.BlockSpec((k, tc), lambda j: (0, j)), col] + ([col] if act else [])
    ins = ([src, src] if glu else [src]) + [w, dy] + ([pre] if act else [])
    out_specs = [col] + ([col] if glu else []) + [pl.BlockSpec((k, tc), lambda j: (0, j)),
                                                  pl.BlockSpec((1, tc), lambda j: (0, j))]
    out_shape = [jax.ShapeDtypeStruct((s, c), BF16)] * (2 if glu else 1) + [
        jax.ShapeDtypeStruct((k, c), F32), jax.ShapeDtypeStruct((1, c), F32)]
    return pl.pallas_call(
        body, name=name, grid=(ncb,), in_specs=in_specs, out_specs=out_specs, out_shape=out_shape,
        scratch_shapes=[pltpu.VMEM((CONV_PAD + s, tc), F32), pltpu.VMEM((s + CONV_PAD, tc), F32)],
        compiler_params=_cparams(("parallel",), 56),
    )(*ins)


def _ssd_consts():
    e = np.zeros((LANE, SSD_HEADS * SSD_P), np.float32)
    for h in range(SSD_HEADS):
        e[h, h * SSD_P:(h + 1) * SSD_P] = 1.0
    ltri = np.tril(np.ones((CHUNK, CHUNK), np.float32))
    return jnp.asarray(e), jnp.asarray(e.T.copy()), jnp.asarray(ltri), jnp.asarray(ltri.T.copy())


def _split3(x):
    hi = x.astype(BF16)
    r = x - hi.astype(F32)
    mid = r.astype(BF16)
    lo = (r - mid.astype(F32)).astype(BF16)
    return hi, mid, lo


def _dot_sel(x, sel, x_left=True):
    sb = sel.astype(BF16)
    out = None
    for part in _split3(x):
        t = jnp.dot(part, sb, preferred_element_type=F32) if x_left else jnp.dot(sb, part, preferred_element_type=F32)
        out = t if out is None else out + t
    return out


def _ssd_chunk_terms(dtr_ref, dtrt_ref, bias_ref, biast_ref, alog_ref, alogt_ref, e_ref, ltri_ref, utri_ref):
    a_neg = -jnp.exp(alog_ref[...])
    dt = _softplus(dtr_ref[...] + bias_ref[...])
    a = dt * a_neg
    s = _dot_sel(a, ltri_ref[...], x_left=False)
    dtt = _softplus(dtrt_ref[...] + biast_ref[...])
    st = _dot_sel(dtt * (-jnp.exp(alogt_ref[...])), utri_ref[...])
    ev = e_ref[...]
    s_x = _dot_sel(s, ev)
    dt_x = _dot_sel(dt, ev)
    return a_neg, dt, s, st, s_x, dt_x


def _ssd_decay(s, st, h, tril):
    seg = s[:, h:h + 1] - st[h:h + 1, :]
    return jnp.exp(jnp.where(tril, seg, NEG))


def _ssd_decay_t(s, st, h, triu):
    seg = st[h:h + 1, :] - s[:, h:h + 1]
    return jnp.exp(jnp.where(triu, seg, NEG))


def _head_masks():
    lane = lax.broadcasted_iota(jnp.int32, (1, SSD_P * 4), 1)
    return [((lane >= r * SSD_P) & (lane < (r + 1) * SSD_P)).astype(F32) for r in range(4)]


def _ssd_scan_fwd(xbc, dtr, dtrt, bias, biast, alog, alogt, dskip, *, name):
    s_len = xbc.shape[0]
    nc = s_len // CHUNK
    e, et, ltri, utri = _ssd_consts()
    gw = SSD_P * 4

    def body(xs_ref, b_ref, c_ref, dtr_ref, dtrt_ref, bias_ref, biast_ref, alog_ref, alogt_ref, d_ref,
             e_ref, ltri_ref, utri_ref, y_ref, prev_ref, state):
        @pl.when(pl.program_id(0) == 0)
        def _():
            state[...] = jnp.zeros_like(state)

        a_neg, dt, s, st, s_x, dt_x = _ssd_chunk_terms(dtr_ref, dtrt_ref, bias_ref, biast_ref, alog_ref,
                                                       alogt_ref, e_ref, ltri_ref, utri_ref)
        s_last = s_x[CHUNK - 1:CHUNK, :]
        es_x = jnp.exp(s_x)
        w_x = jnp.exp(s_last - s_x)
        cd_x = jnp.exp(s_last)
        d_x = _dot_sel(jnp.broadcast_to(d_ref[...], (8, LANE)), e_ref[...])[0:1, :]
        xs = xs_ref[...]
        xv = xs * dt_x
        row = lax.broadcasted_iota(jnp.int32, (CHUNK, CHUNK), 0)
        colm = lax.broadcasted_iota(jnp.int32, (CHUNK, CHUNK), 1)
        tril = colm <= row
        masks = _head_masks()
        for g in range(SSD_GROUPS):
            gs = slice(g * gw, (g + 1) * gw)
            bg = b_ref[:, g * SSD_N:(g + 1) * SSD_N].astype(BF16)
            cg = c_ref[:, g * SSD_N:(g + 1) * SSD_N].astype(BF16)
            xg = xv[:, gs]
            hg = state[g]
            prev_ref[g] = hg
            cb = lax.dot_general(cg, bg, (((1,), (1,)), ((), ())), preferred_element_type=F32)
            yg = jnp.dot(cg, hg.astype(BF16), preferred_element_type=F32) * es_x[:, gs]
            for r in range(4):
                m = (cb * _ssd_decay(s, st, g * 4 + r, tril)).astype(BF16)
                yg = yg + jnp.dot(m, (xg * masks[r]).astype(BF16), preferred_element_type=F32)
            y_ref[:, gs] = yg + d_x[:, gs] * xs[:, gs]
            upd = lax.dot_general(bg, (xg * w_x[:, gs]).astype(BF16), (((0,), (0,)), ((), ())),
                                  preferred_element_type=F32)
            state[g] = hg * cd_x[:, gs] + upd

    nh = LANE
    chunk_row = lambda w, cb: pl.BlockSpec((CHUNK, w), lambda i, cb=cb: (i, cb))
    full = lambda a: pl.BlockSpec(a.shape, lambda i: (0,) * a.ndim)
    in_specs = [chunk_row(1024, 0), chunk_row(512, 2), chunk_row(512, 3), chunk_row(nh, 0),
                pl.BlockSpec((None, nh, CHUNK), lambda i: (i, 0, 0)),
                full(bias), full(biast), full(alog), full(alogt), full(dskip), full(e), full(ltri), full(utri)]
    return pl.pallas_call(
        body, name=name, grid=(nc,), in_specs=in_specs,
        out_specs=[pl.BlockSpec((CHUNK, 1024), lambda i: (i, 0)),
                   pl.BlockSpec((None, SSD_GROUPS, SSD_N, gw), lambda i: (i, 0, 0, 0))],
        out_shape=[jax.ShapeDtypeStruct((s_len, 1024), F32),
                   jax.ShapeDtypeStruct((nc, SSD_GROUPS, SSD_N, gw), F32)],
        scratch_shapes=[pltpu.VMEM((SSD_GROUPS, SSD_N, gw), F32)],
        compiler_params=_cparams(("arbitrary",)),
    )(xbc, xbc, xbc, dtr, dtrt, bias, biast, alog, alogt, dskip, e, ltri, utri)


def _ssd_scan_bwd(xbc, dtr, dtrt, bias, biast, alog, alogt, dskip, prev, dy, *, name):
    s_len = xbc.shape[0]
    nc = s_len // CHUNK
    e, et, ltri, utri = _ssd_consts()
    gw = SSD_P * 4

    def body(xs_ref, b_ref, c_ref, dtr_ref, dtrt_ref, bias_ref, biast_ref, alog_ref, alogt_ref, d_ref,
             e_ref, et_ref, ltri_ref, utri_ref, prev_ref, dy_ref,
             dxs_ref, db_ref, dc_ref, ddtr_ref, dalog_ref, dbias_ref, dd_ref, dstate):
        step = pl.program_id(0)

        @pl.when(step == 0)
        def _():
            dstate[...] = jnp.zeros_like(dstate)

        a_neg, dt, s, st, s_x, dt_x = _ssd_chunk_terms(dtr_ref, dtrt_ref, bias_ref, biast_ref, alog_ref,
                                                       alogt_ref, e_ref, ltri_ref, utri_ref)
        s_last = s_x[CHUNK - 1:CHUNK, :]
        es_x = jnp.exp(s_x)
        w_x = jnp.exp(s_last - s_x)
        cd_x = jnp.exp(s_last)
        d_x = _dot_sel(jnp.broadcast_to(d_ref[...], (8, LANE)), e_ref[...])[0:1, :]
        xs = xs_ref[...]
        xv = xs * dt_x
        dyv = dy_ref[...]
        row = lax.broadcasted_iota(jnp.int32, (CHUNK, CHUNK), 0)
        colm = lax.broadcasted_iota(jnp.int32, (CHUNK, CHUNK), 1)
        tril = colm <= row
        masks = _head_masks()
        is_last = lax.broadcasted_iota(jnp.int32, (CHUNK, 1), 0) == CHUNK - 1
        nt = (((1,), (1,)), ((), ()))
        tn = (((0,), (0,)), ((), ()))
        ds_parts, ddt_parts = [], []
        head_lane = lax.broadcasted_iota(jnp.int32, (CHUNK, LANE), 1)
        triu = colm >= row
        ds_diag = jnp.zeros((CHUNK, LANE), F32)
        for g in range(SSD_GROUPS):
            gs = slice(g * gw, (g + 1) * gw)
            bg = b_ref[:, g * SSD_N:(g + 1) * SSD_N].astype(BF16)
            cg = c_ref[:, g * SSD_N:(g + 1) * SSD_N].astype(BF16)
            xg = xv[:, gs]
            xgb = xg.astype(BF16)
            hg = prev_ref[g]
            hgb = hg.astype(BF16)
            dsg = dstate[g]
            dsgb = dsg.astype(BF16)
            dyg = dyv[:, gs]
            dye = (dyg * es_x[:, gs]).astype(BF16)
            xw = (xg * w_x[:, gs]).astype(BF16)
            cb = lax.dot_general(cg, bg, nt, preferred_element_type=F32)
            cbt = lax.dot_general(bg, cg, nt, preferred_element_type=F32)
            dcg = lax.dot_general(dye, hgb, nt, preferred_element_type=F32)
            dh = lax.dot_general(cg, dye, tn, preferred_element_type=F32)
            bds = jnp.dot(bg, dsgb, preferred_element_type=F32)
            yoff = es_x[:, gs] * jnp.dot(cg, hgb, preferred_element_type=F32)
            dx_state = w_x[:, gs] * bds
            dbg = lax.dot_general(xw, dsgb, nt, preferred_element_type=F32)
            dxd = jnp.zeros((CHUNK, gw), F32)
            dcb = jnp.zeros((CHUNK, CHUNK), F32)
            for r in range(4):
                dec = _ssd_decay(s, st, g * 4 + r, tril)
                mf = cb * dec
                m = mf.astype(BF16)
                dym = (dyg * masks[r]).astype(BF16)
                dm = lax.dot_general(dym, xgb, nt, preferred_element_type=F32)
                dxd = dxd + lax.dot_general(m, dym, tn, preferred_element_type=F32)
                dcb = dcb + dm * dec
                dmt = lax.dot_general(xgb, dym, nt, preferred_element_type=F32)
                rc = (jnp.sum(dm * mf, axis=1, keepdims=True)
                      - jnp.sum(dmt * cbt * _ssd_decay_t(s, st, g * 4 + r, triu), axis=1, keepdims=True))
                ds_diag = ds_diag + jnp.where(head_lane == g * 4 + r, rc, 0.0)
            dcbb = dcb.astype(BF16)
            dcg = dcg + jnp.dot(dcbb, bg, preferred_element_type=F32)
            dbg = dbg + lax.dot_general(dcbb, cg, tn, preferred_element_type=F32)
            dxg = dxd + dx_state
            extra = (jnp.sum(xg * dx_state, axis=0, keepdims=True)
                     + cd_x[:, gs] * jnp.sum(dsg * hg, axis=0, keepdims=True))
            ds_parts.append(dyg * yoff - xg * dx_state + jnp.where(is_last, extra, 0.0))
            ddt_parts.append(dxg * xs[:, gs])
            dxs_ref[:, gs] = dxg * dt_x[:, gs] + d_x[:, gs] * dyg
            db_ref[:, g * SSD_N:(g + 1) * SSD_N] = dbg
            dc_ref[:, g * SSD_N:(g + 1) * SSD_N] = dcg
            dstate[g] = cd_x[:, gs] * dsg + dh
        etv = et_ref[...]
        ds = ds_diag + _dot_sel(jnp.concatenate(ds_parts, axis=1), etv)
        da = _dot_sel(ds, utri_ref[...], x_left=False)
        ddt = da * a_neg + _dot_sel(jnp.concatenate(ddt_parts, axis=1), etv)
        ddtr = ddt * _sigmoid(dtr_ref[...] + bias_ref[...])
        ddtr_ref[...] = ddtr.astype(ddtr_ref.dtype)
        p_alog = jnp.sum(da * dt, axis=0, keepdims=True) * a_neg
        p_bias = jnp.sum(ddtr, axis=0, keepdims=True)
        p_d = _dot_sel(jnp.broadcast_to(jnp.sum(dyv * xs, axis=0, keepdims=True), (8, SSD_HEADS * SSD_P)),
                       etv)[0:1, :]

        @pl.when(step == 0)
        def _():
            dalog_ref[...] = p_alog
            dbias_ref[...] = p_bias
            dd_ref[...] = p_d

        @pl.when(step > 0)
        def _():
            dalog_ref[...] += p_alog
            dbias_ref[...] += p_bias
            dd_ref[...] += p_d

    nh = LANE
    rev = lambda i: nc - 1 - i
    chunk_row = lambda w, cb: pl.BlockSpec((CHUNK, w), lambda i, cb=cb: (rev(i), cb))
    full = lambda a: pl.BlockSpec(a.shape, lambda i: (0,) * a.ndim)
    vec = pl.BlockSpec((1, nh), lambda i: (0, 0))
    in_specs = [chunk_row(1024, 0), chunk_row(512, 2), chunk_row(512, 3), chunk_row(nh, 0),
                pl.BlockSpec((None, nh, CHUNK), lambda i: (rev(i), 0, 0)),
                full(bias), full(biast), full(alog), full(alogt), full(dskip), full(e), full(et), full(ltri),
                full(utri),
                pl.BlockSpec((None, SSD_GROUPS, SSD_N, gw), lambda i: (rev(i), 0, 0, 0)), chunk_row(1024, 0)]
    return pl.pallas_call(
        body, name=name, grid=(nc,), in_specs=in_specs,
        out_specs=[chunk_row(1024, 0), chunk_row(512, 0), chunk_row(512, 0), chunk_row(nh, 0), vec, vec, vec],
        out_shape=[jax.ShapeDtypeStruct((s_len, 1024), F32), jax.ShapeDtypeStruct((s_len, 512), F32),
                   jax.ShapeDtypeStruct((s_len, 512), F32), jax.ShapeDtypeStruct((s_len, nh), BF16),
                   jax.ShapeDtypeStruct((1, nh), F32), jax.ShapeDtypeStruct((1, nh), F32),
                   jax.ShapeDtypeStruct((1, nh), F32)],
        scratch_shapes=[pltpu.VMEM((SSD_GROUPS, SSD_N, gw), F32)],
        compiler_params=_cparams(("arbitrary",)),
    )(xbc, xbc, xbc, dtr, dtrt, bias, biast, alog, alogt, dskip, e, et, ltri, utri, prev, dy)


def _ssd_gate_fwd(y, z, g, *, name):
    r, w = y.shape
    tr = _pick(r, 512, 8)
    gw = w // SSD_GROUPS

    def body(y_ref, z_ref, g_ref, o_ref):
        for k in range(SSD_GROUPS):
            cs = slice(k * gw, (k + 1) * gw)
            t = y_ref[:, cs] * _silu(z_ref[:, cs])
            rstd = lax.rsqrt(jnp.mean(t * t, axis=-1, keepdims=True) + EPS)
            o_ref[:, cs] = (t * rstd * g_ref[:, cs]).astype(o_ref.dtype)

    row = pl.BlockSpec((tr, w), lambda i: (i, 0))
    vec = pl.BlockSpec((1, w), lambda i: (0, 0))
    return pl.pallas_call(
        body, name=name, grid=(r // tr,), in_specs=[row, row, vec], out_specs=row,
        out_shape=jax.ShapeDtypeStruct((r, w), BF16), compiler_params=_cparams(("parallel",)),
    )(y, z, g)


def _ssd_gate_bwd(y, z, g, do, *, name):
    r, w = y.shape
    tr = _pick(r, 512, 8)
    gw = w // SSD_GROUPS

    def body(y_ref, z_ref, g_ref, do_ref, dy_ref, dz_ref, dg_ref):
        parts = []
        for k in range(SSD_GROUPS):
            cs = slice(k * gw, (k + 1) * gw)
            yv = y_ref[:, cs]
            zv = z_ref[:, cs]
            sz = _silu(zv)
            t = yv * sz
            rstd = lax.rsqrt(jnp.mean(t * t, axis=-1, keepdims=True) + EPS)
            th = t * rstd
            dov = do_ref[:, cs].astype(F32)
            dog = dov * g_ref[:, cs]
            dt = rstd * (dog - th * jnp.mean(dog * th, axis=-1, keepdims=True))
            dy_ref[:, cs] = dt * sz
            dz_ref[:, cs] = (dt * yv * _dsilu(zv)).astype(dz_ref.dtype)
            parts.append(jnp.sum(dov * th, axis=0, keepdims=True))
        pg = jnp.concatenate(parts, axis=1)

        @pl.when(pl.program_id(0) == 0)
        def _():
            dg_ref[...] = pg

        @pl.when(pl.program_id(0) > 0)
        def _():
            dg_ref[...] += pg

    row = pl.BlockSpec((tr, w), lambda i: (i, 0))
    vec = pl.BlockSpec((1, w), lambda i: (0, 0))
    return pl.pallas_call(
        body, name=name, grid=(r // tr,), in_specs=[row, row, vec, row], out_specs=[row, row, vec],
        out_shape=[jax.ShapeDtypeStruct((r, w), F32), jax.ShapeDtypeStruct((r, w), BF16),
                   jax.ShapeDtypeStruct((1, w), F32)],
        compiler_params=_cparams(("arbitrary",)),
    )(y, z, g, do)


def _rope_swap(x):
    lane = lax.broadcasted_iota(jnp.int32, x.shape, 1)
    lo = pltpu.roll(x, 96, 1)
    hi = pltpu.roll(x, 32, 1)
    return jnp.where(lane < 32, lo, jnp.where(lane < 64, hi, 0.0))


def _norm_part(v, g, n):
    rstd = lax.rsqrt(jnp.sum(v * v, axis=-1, keepdims=True) * (1.0 / n) + EPS)
    xh = v * rstd
    return xh * g, xh, rstd


def _norm_part_bwd(dout, g, xh, rstd, n):
    dg = dout * g
    return rstd * (dg - xh * (jnp.sum(dg * xh, axis=-1, keepdims=True) * (1.0 / n)))


def _mla_prep_fwd(q, kv, krr, cs, sn, gq, gk, *, name):
    s = q.shape[0]
    tr = _pick(s, 256, 8)
    hp = MLA_HP

    def body(q_ref, kv_ref, krr_ref, cs_ref, sn_ref, gq_ref, gk_ref, qf_ref, kf_ref, v_ref):
        csv, snv = cs_ref[...], sn_ref[...]
        gqn, gqr = gq_ref[:, 0:128], gq_ref[:, 128:256]
        gkn, gkr = gk_ref[:, 0:128], gk_ref[:, 128:256]
        kr, _, _ = _norm_part(krr_ref[...], gkr, MLA_ROPE)
        kr = (kr * csv + _rope_swap(kr) * snv).astype(BF16)
        for h in range(MLA_HEADS):
            qn, _, _ = _norm_part(q_ref[:, h * hp:h * hp + 128], gqn, MLA_NOPE)
            qr, _, _ = _norm_part(q_ref[:, h * hp + 128:(h + 1) * hp], gqr, MLA_ROPE)
            qr = qr * csv + _rope_swap(qr) * snv
            qf_ref[h, :, 0:128] = qn.astype(BF16)
            qf_ref[h, :, 128:256] = qr.astype(BF16)
            kn, _, _ = _norm_part(kv_ref[:, h * hp:h * hp + 128], gkn, MLA_NOPE)
            kf_ref[h, :, 0:128] = kn.astype(BF16)
            kf_ref[h, :, 128:256] = kr
            v_ref[h] = kv_ref[:, h * hp + 128:(h + 1) * hp].astype(BF16)

    row = lambda w: pl.BlockSpec((tr, w), lambda i: (i, 0))
    vec = pl.BlockSpec((1, hp), lambda i: (0, 0))
    hrow = lambda w: pl.BlockSpec((MLA_HEADS, tr, w), lambda i: (0, i, 0))
    return pl.pallas_call(
        body, name=name, grid=(s // tr,),
        in_specs=[row(MLA_HEADS * hp), row(MLA_HEADS * hp), row(128), row(128), row(128), vec, vec],
        out_specs=[hrow(hp), hrow(hp), hrow(128)],
        out_shape=[jax.ShapeDtypeStruct((MLA_HEADS, s, hp), BF16), jax.ShapeDtypeStruct((MLA_HEADS, s, hp), BF16),
                   jax.ShapeDtypeStruct((MLA_HEADS, s, 128), BF16)],
        compiler_params=_cparams(("parallel",)),
    )(q, kv, krr, cs, sn, gq, gk)


def _mla_prep_bwd(q, kv, krr, cs, sn, gq, gk, dqf, dkf, dv, *, name):
    s = q.shape[0]
    tr = _pick(s, 256, 8)
    hp = MLA_HP

    def body(q_ref, kv_ref, krr_ref, cs_ref, sn_ref, gq_ref, gk_ref, dqf_ref, dkf_ref, dv_ref,
             dq_ref, dkv_ref, dkrr_ref, dgq_ref, dgk_ref):
        csv, snv = cs_ref[...], sn_ref[...]
        gqn, gqr = gq_ref[:, 0:128], gq_ref[:, 128:256]
        gkn, gkr = gk_ref[:, 0:128], gk_ref[:, 128:256]
        _, krh, krs = _norm_part(krr_ref[...], gkr, MLA_ROPE)
        dkr_sum = jnp.zeros((tr, 128), F32)
        pgqn = jnp.zeros((1, 128), F32)
        pgqr = jnp.zeros((1, 128), F32)
        pgkn = jnp.zeros((1, 128), F32)
        for h in range(MLA_HEADS):
            _, qnh, qns = _norm_part(q_ref[:, h * hp:h * hp + 128], gqn, MLA_NOPE)
            _, qrh, qrs = _norm_part(q_ref[:, h * hp + 128:(h + 1) * hp], gqr, MLA_ROPE)
            dqn = dqf_ref[h, :, 0:128]
            drr = dqf_ref[h, :, 128:256]
            dqr = drr * csv + _rope_swap(drr * snv)
            dq_ref[:, h * hp:h * hp + 128] = _norm_part_bwd(dqn, gqn, qnh, qns, MLA_NOPE).astype(dq_ref.dtype)
            dq_ref[:, h * hp + 128:(h + 1) * hp] = _norm_part_bwd(dqr, gqr, qrh, qrs, MLA_ROPE).astype(dq_ref.dtype)
            pgqn = pgqn + jnp.sum(dqn * qnh, axis=0, keepdims=True)
            pgqr = pgqr + jnp.sum(dqr * qrh, axis=0, keepdims=True)
            _, knh, kns = _norm_part(kv_ref[:, h * hp:h * hp + 128], gkn, MLA_NOPE)
            dkn = dkf_ref[h, :, 0:128]
            dkv_ref[:, h * hp:h * hp + 128] = _norm_part_bwd(dkn, gkn, knh, kns, MLA_NOPE).astype(dkv_ref.dtype)
            dkv_ref[:, h * hp + 128:(h + 1) * hp] = dv_ref[h].astype(dkv_ref.dtype)
            pgkn = pgkn + jnp.sum(dkn * knh, axis=0, keepdims=True)
            dkr_sum = dkr_sum + dkf_ref[h, :, 128:256]
        dkr = dkr_sum * csv + _rope_swap(dkr_sum * snv)
        dkrr_ref[...] = _norm_part_bwd(dkr, gkr, krh, krs, MLA_ROPE).astype(dkrr_ref.dtype)
        pgkr = jnp.sum(dkr * krh, axis=0, keepdims=True)
        pq = jnp.concatenate([pgqn, pgqr], axis=1)
        pk = jnp.concatenate([pgkn, pgkr], axis=1)

        @pl.when(pl.program_id(0) == 0)
        def _():
            dgq_ref[...] = pq
            dgk_ref[...] = pk

        @pl.when(pl.program_id(0) > 0)
        def _():
            dgq_ref[...] += pq
            dgk_ref[...] += pk

    row = lambda w: pl.BlockSpec((tr, w), lambda i: (i, 0))
    vec = pl.BlockSpec((1, hp), lambda i: (0, 0))
    hrow = lambda w: pl.BlockSpec((MLA_HEADS, tr, w), lambda i: (0, i, 0))
    return pl.pallas_call(
        body, name=name, grid=(s // tr,),
        in_specs=[row(MLA_HEADS * hp), row(MLA_HEADS * hp), row(128), row(128), row(128), vec, vec,
                  hrow(hp), hrow(hp), hrow(128)],
        out_specs=[row(MLA_HEADS * hp), row(MLA_HEADS * hp), row(128), vec, vec],
        out_shape=[jax.ShapeDtypeStruct((s, MLA_HEADS * hp), BF16), jax.ShapeDtypeStruct((s, MLA_HEADS * hp), BF16),
                   jax.ShapeDtypeStruct((s, 128), BF16), jax.ShapeDtypeStruct((1, hp), F32),
                   jax.ShapeDtypeStruct((1, hp), F32)],
        compiler_params=_cparams(("arbitrary",), 48),
    )(q, kv, krr, cs, sn, gq, gk, dqf, dkf, dv)


ATT_T = 512


def _chunk_mask(t):
    r = lax.shift_right_logical(lax.broadcasted_iota(jnp.int32, (t, t), 0), 6)
    c = lax.shift_right_logical(lax.broadcasted_iota(jnp.int32, (t, t), 1), 6)
    return c <= r


def _mla_attn_fwd(qf, kf, v, *, name):
    nh, s, hp = qf.shape
    t = _pick(s, ATT_T, CHUNK)
    scale = (MLA_NOPE + MLA_ROPE) ** -0.5
    nt = (((1,), (1,)), ((), ()))

    def body(q_ref, k_ref, v_ref, o_ref, lse_ref):
        i = pl.program_id(1)
        q = q_ref[...]

        def block(j, carry, masked):
            m, l, acc = carry
            start = pl.multiple_of(j * t, t)
            k = k_ref[pl.ds(start, t), :]
            sc = lax.dot_general(q, k, nt, preferred_element_type=F32) * scale
            if masked:
                sc = jnp.where(_chunk_mask(t), sc, NEG)
            m_new = jnp.maximum(m, jnp.max(sc, axis=-1, keepdims=True))
            alpha = jnp.exp(m - m_new)
            p = jnp.exp(sc - m_new)
            l = alpha * l + jnp.sum(p, axis=-1, keepdims=True)
            acc = alpha * acc + jnp.dot(p.astype(BF16), v_ref[pl.ds(start, t), :], preferred_element_type=F32)
            return m_new, l, acc

        init = (jnp.full((t, 1), NEG, F32), jnp.zeros((t, 1), F32), jnp.zeros((t, MLA_V), F32))
        carry = lax.fori_loop(0, i, lambda j, c: block(j, c, False), init)
        m, l, acc = block(i, carry, True)
        o_ref[...] = acc / l
        lse_ref[...] = m + jnp.log(l)

    return pl.pallas_call(
        body, name=name, grid=(nh, s // t),
        in_specs=[pl.BlockSpec((None, t, hp), lambda h, i: (h, i, 0)),
                  pl.BlockSpec((None, s, hp), lambda h, i: (h, 0, 0)),
                  pl.BlockSpec((None, s, MLA_V), lambda h, i: (h, 0, 0))],
        out_specs=[pl.BlockSpec((t, MLA_V), lambda h, i: (i, h)),
                   pl.BlockSpec((None, t, 1), lambda h, i: (h, i, 0))],
        out_shape=[jax.ShapeDtypeStruct((s, nh * MLA_V), F32), jax.ShapeDtypeStruct((nh, s, 1), F32)],
        compiler_params=_cparams(("parallel", "arbitrary"), 48),
    )(qf, kf, v)


def _mla_attn_bwd(qf, kf, v, o, lse, do, *, name):
    nh, s, hp = qf.shape
    t = _pick(s, ATT_T, CHUNK)
    nb = s // t
    scale = (MLA_NOPE + MLA_ROPE) ** -0.5
    nt = (((1,), (1,)), ((), ()))
    tn = (((0,), (0,)), ((), ()))

    def body(q_ref, k_ref, v_ref, o_ref, lse_ref, do_ref, dq_ref, dk_ref, dv_ref, delta):
        j = pl.program_id(1)

        @pl.when(j == 0)
        def _():
            dq_ref[...] = jnp.zeros_like(dq_ref)
            delta[...] = jnp.sum(do_ref[...] * o_ref[...], axis=-1, keepdims=True)

        k = k_ref[...]
        vv = v_ref[...]

        def block(i, carry, masked):
            dk, dv = carry
            start = pl.multiple_of(i * t, t)
            q = q_ref[pl.ds(start, t), :]
            dob = do_ref[pl.ds(start, t), :].astype(BF16)
            sc = lax.dot_general(q, k, nt, preferred_element_type=F32) * scale
            if masked:
                sc = jnp.where(_chunk_mask(t), sc, NEG)
            p = jnp.exp(sc - lse_ref[pl.ds(start, t), :])
            dp = lax.dot_general(dob, vv, nt, preferred_element_type=F32)
            ds = (p * (dp - delta[pl.ds(start, t), :]) * scale).astype(BF16)
            dv = dv + lax.dot_general(p.astype(BF16), dob, tn, preferred_element_type=F32)
            dk = dk + lax.dot_general(ds, q, tn, preferred_element_type=F32)
            dq_ref[pl.ds(start, t), :] += jnp.dot(ds, k, preferred_element_type=F32)
            return dk, dv

        init = (jnp.zeros((t, hp), F32), jnp.zeros((t, MLA_V), F32))
        carry = block(j, init, True)
        dk, dv = lax.fori_loop(j + 1, nb, lambda i, c: block(i, c, False), carry)
        dk_ref[...] = dk
        dv_ref[...] = dv

    whole = lambda w: pl.BlockSpec((None, s, w), lambda h, j: (h, 0, 0))
    blk = lambda w: pl.BlockSpec((None, t, w), lambda h, j: (h, j, 0))
    colh = pl.BlockSpec((s, MLA_V), lambda h, j: (0, h))
    return pl.pallas_call(
        body, name=name, grid=(nh, nb),
        in_specs=[whole(hp), blk(hp), blk(MLA_V), colh, whole(1), colh],
        out_specs=[whole(hp), blk(hp), blk(MLA_V)],
        out_shape=[jax.ShapeDtypeStruct((nh, s, hp), F32), jax.ShapeDtypeStruct((nh, s, hp), F32),
                   jax.ShapeDtypeStruct((nh, s, MLA_V), F32)],
        scratch_shapes=[pltpu.VMEM((s, 1), F32)],
        compiler_params=_cparams(("parallel", "arbitrary"), 56),
    )(qf, kf, v, o, lse, do)


def _merge_fwd(gl, gb, ys, yc, ym, *, name):
    s, d = ys.shape
    tr = _pick(s, 256, 8)

    def body(gl_ref, gb_ref, ys_ref, yc_ref, ym_ref, o_ref):
        acc = jnp.zeros((tr, d), F32)
        for k, y_ref in enumerate((ys_ref, yc_ref, ym_ref)):
            gt = _sigmoid(gl_ref[:, k * d:(k + 1) * d] + gb_ref[:, k * d:(k + 1) * d])
            acc = acc + gt * y_ref[...]
        o_ref[...] = acc.astype(o_ref.dtype)

    row = lambda w: pl.BlockSpec((tr, w), lambda i: (i, 0))
    return pl.pallas_call(
        body, name=name, grid=(s // tr,),
        in_specs=[row(3 * d), pl.BlockSpec((1, 3 * d), lambda i: (0, 0)), row(d), row(d), row(d)],
        out_specs=row(d), out_shape=jax.ShapeDtypeStruct((s, d), BF16),
        compiler_params=_cparams(("parallel",)),
    )(gl, gb, ys, yc, ym)


def _merge_bwd(gl, gb, ys, yc, ym, dm, *, name):
    s, d = ys.shape
    tr = _pick(s, 256, 8)

    def body(gl_ref, gb_ref, ys_ref, yc_ref, ym_ref, dm_ref, dgl_ref, dgb_ref, dys_ref, dyc_ref, dym_ref):
        dmv = dm_ref[...]
        parts = []
        for k, (y_ref, dy_ref) in enumerate(((ys_ref, dys_ref), (yc_ref, dyc_ref), (ym_ref, dym_ref))):
            gt = _sigmoid(gl_ref[:, k * d:(k + 1) * d] + gb_ref[:, k * d:(k + 1) * d])
            dy_ref[...] = (gt * dmv).astype(dy_ref.dtype)
            dl = dmv * y_ref[...] * gt * (1.0 - gt)
            dgl_ref[:, k * d:(k + 1) * d] = dl.astype(dgl_ref.dtype)
            parts.append(jnp.sum(dl, axis=0, keepdims=True))
        pb = jnp.concatenate(parts, axis=1)

        @pl.when(pl.program_id(0) == 0)
        def _():
            dgb_ref[...] = pb

        @pl.when(pl.program_id(0) > 0)
        def _():
            dgb_ref[...] += pb

    row = lambda w: pl.BlockSpec((tr, w), lambda i: (i, 0))
    vec = pl.BlockSpec((1, 3 * d), lambda i: (0, 0))
    return pl.pallas_call(
        body, name=name, grid=(s // tr,),
        in_specs=[row(3 * d), vec, row(d), row(d), row(d), row(d)],
        out_specs=[row(3 * d), vec, row(d), row(d), row(d)],
        out_shape=[jax.ShapeDtypeStruct((s, 3 * d), BF16), jax.ShapeDtypeStruct((1, 3 * d), F32)]
        + [jax.ShapeDtypeStruct((s, d), BF16)] * 3,
        compiler_params=_cparams(("arbitrary",)),
    )(gl, gb, ys, yc, ym, dm)


def _xattn_fwd(q, k, v, gq, *, name):
    s, d = q.shape
    dh = d // X_HEADS
    tr = _pick(s, 512, 8)
    scale = dh ** -0.5
    nt = (((1,), (1,)), ((), ()))

    def body(q_ref, k_ref, v_ref, gq_ref, o_ref):
        for h in range(X_HEADS):
            cs = slice(h * dh, (h + 1) * dh)
            qn, _, _ = _norm_part(q_ref[:, cs], gq_ref[...], dh)
            sc = lax.dot_general(qn.astype(BF16), k_ref[:, cs], nt, preferred_element_type=F32) * scale
            p = jnp.exp(sc - jnp.max(sc, axis=-1, keepdims=True))
            p = p / jnp.sum(p, axis=-1, keepdims=True)
            o_ref[:, cs] = jnp.dot(p.astype(BF16), v_ref[:, cs], preferred_element_type=F32)

    row = pl.BlockSpec((tr, d), lambda i: (i, 0))
    mem = pl.BlockSpec(k.shape, lambda i: (0, 0))
    return pl.pallas_call(
        body, name=name, grid=(s // tr,),
        in_specs=[row, mem, mem, pl.BlockSpec((1, dh), lambda i: (0, 0))], out_specs=row,
        out_shape=jax.ShapeDtypeStruct((s, d), F32), compiler_params=_cparams(("parallel",)),
    )(q, k, v, gq)


def _xattn_bwd(q, k, v, gq, do, *, name):
    s, d = q.shape
    dh = d // X_HEADS
    tr = _pick(s, 512, 8)
    scale = dh ** -0.5
    nt = (((1,), (1,)), ((), ()))
    tn = (((0,), (0,)), ((), ()))

    def body(q_ref, k_ref, v_ref, gq_ref, do_ref, dq_ref, dk_ref, dv_ref, dgq_ref):
        first = pl.program_id(0) == 0
        pg = jnp.zeros((1, dh), F32)
        for h in range(X_HEADS):
            cs = slice(h * dh, (h + 1) * dh)
            qn, qh, qs = _norm_part(q_ref[:, cs], gq_ref[...], dh)
            qnb = qn.astype(BF16)
            kh = k_ref[:, cs]
            sc = lax.dot_general(qnb, kh, nt, preferred_element_type=F32) * scale
            p = jnp.exp(sc - jnp.max(sc, axis=-1, keepdims=True))
            p = p / jnp.sum(p, axis=-1, keepdims=True)
            dob = do_ref[:, cs].astype(BF16)
            dp = lax.dot_general(dob, v_ref[:, cs], nt, preferred_element_type=F32)
            ds = (p * (dp - jnp.sum(dp * p, axis=-1, keepdims=True)) * scale).astype(BF16)
            dqn = jnp.dot(ds, kh, preferred_element_type=F32)
            dq_ref[:, cs] = _norm_part_bwd(dqn, gq_ref[...], qh, qs, dh).astype(dq_ref.dtype)
            pg = pg + jnp.sum(dqn * qh, axis=0, keepdims=True)
            pv = lax.dot_general(p.astype(BF16), dob, tn, preferred_element_type=F32)
            pk = lax.dot_general(ds, qnb, tn, preferred_element_type=F32)

            @pl.when(first)
            def _():
                dv_ref[:, cs] = pv
                dk_ref[:, cs] = pk

            @pl.when(jnp.logical_not(first))
            def _():
                dv_ref[:, cs] += pv
                dk_ref[:, cs] += pk

        @pl.when(first)
        def _():
            dgq_ref[...] = pg

        @pl.when(jnp.logical_not(first))
        def _():
            dgq_ref[...] += pg

    row = pl.BlockSpec((tr, d), lambda i: (i, 0))
    mem = pl.BlockSpec(k.shape, lambda i: (0, 0))
    vec = pl.BlockSpec((1, dh), lambda i: (0, 0))
    return pl.pallas_call(
        body, name=name, grid=(s // tr,),
        in_specs=[row, mem, mem, vec, row], out_specs=[row, mem, mem, vec],
        out_shape=[jax.ShapeDtypeStruct((s, d), BF16), jax.ShapeDtypeStruct(k.shape, F32),
                   jax.ShapeDtypeStruct(k.shape, F32), jax.ShapeDtypeStruct((1, dh), F32)],
        compiler_params=_cparams(("arbitrary",)),
    )(q, k, v, gq, do)


def _swiglu_fwd(h1, *, name):
    s, w2 = h1.shape
    w = w2 // 2
    tr = _pick(s, 256, 8)
    tc = _pick(w, 1408, 128)
    ncb = w // tc

    def body(g_ref, u_ref, o_ref):
        o_ref[...] = (_silu(g_ref[...]) * u_ref[...]).astype(o_ref.dtype)

    return pl.pallas_call(
        body, name=name, grid=(s // tr, ncb),
        in_specs=[pl.BlockSpec((tr, tc), lambda i, j: (i, j)), pl.BlockSpec((tr, tc), lambda i, j: (i, j + ncb))],
        out_specs=pl.BlockSpec((tr, tc), lambda i, j: (i, j)),
        out_shape=jax.ShapeDtypeStruct((s, w), BF16), compiler_params=_cparams(("parallel", "parallel")),
    )(h1, h1)


def _swiglu_bwd(h1, dact, *, name):
    s, w2 = h1.shape
    w = w2 // 2
    tr = _pick(s, 256, 8)
    tc = _pick(w, 1408, 128)
    ncb = w // tc

    def body(g_ref, u_ref, d_ref, dg_ref, du_ref):
        gv = g_ref[...]
        dv = d_ref[...]
        dg_ref[...] = (dv * u_ref[...] * _dsilu(gv)).astype(dg_ref.dtype)
        du_ref[...] = (dv * _silu(gv)).astype(du_ref.dtype)

    blk = pl.BlockSpec((tr, tc), lambda i, j: (i, j))
    dg, du = pl.pallas_call(
        body, name=name, grid=(s // tr, ncb),
        in_specs=[blk, pl.BlockSpec((tr, tc), lambda i, j: (i, j + ncb)), blk],
        out_specs=[blk, blk],
        out_shape=[jax.ShapeDtypeStruct((s, w), BF16)] * 2, compiler_params=_cparams(("parallel", "parallel")),
    )(h1, h1, dact)
    return jnp.concatenate([dg, du], axis=1)


def _add(a, b, *, name):
    r, w = a.shape
    tr = _pick(r, 512, 8)

    def body(a_ref, b_ref, o_ref):
        o_ref[...] = a_ref[...] + b_ref[...].astype(F32)

    row = pl.BlockSpec((tr, w), lambda i: (i, 0))
    return pl.pallas_call(
        body, name=name, grid=(r // tr,), in_specs=[row, row], out_specs=row,
        out_shape=jax.ShapeDtypeStruct((r, w), F32), compiler_params=_cparams(("parallel",)),
    )(a, b)


def _loss(y, target, *, name):
    r, w = y.shape
    tr = _pick(r, 512, 8)

    def body(y_ref, t_ref, dy_ref, l_ref):
        err = y_ref[...] - t_ref[...]
        dy_ref[...] = err * (1.0 / w)
        part = jnp.zeros((8, LANE), F32) + 0.5 * jnp.sum(jnp.mean(err * err, axis=-1, keepdims=True))

        @pl.when(pl.program_id(0) == 0)
        def _():
            l_ref[...] = part

        @pl.when(pl.program_id(0) > 0)
        def _():
            l_ref[...] += part

    row = pl.BlockSpec((tr, w), lambda i: (i, 0))
    dy, l = pl.pallas_call(
        body, name=name, grid=(r // tr,), in_specs=[row, row],
        out_specs=[row, pl.BlockSpec((8, LANE), lambda i: (0, 0))],
        out_shape=[jax.ShapeDtypeStruct((r, w), F32), jax.ShapeDtypeStruct((8, LANE), F32)],
        compiler_params=_cparams(("arbitrary",)),
    )(y, target)
    return dy, l[0, 0]


def _adamw(w, g, m, v, *, name):
    r, c = w.shape
    tr = _pick(r, 256, 8)
    c1 = 1.0 - ADAM_B1 ** ADAM_STEP
    c2 = 1.0 - ADAM_B2 ** ADAM_STEP

    def body(w_ref, g_ref, m_ref, v_ref, d_ref, nm_ref, nv_ref):
        gv = g_ref[...]
        nm = ADAM_B1 * m_ref[...] + (1.0 - ADAM_B1) * gv
        nv = ADAM_B2 * v_ref[...] + (1.0 - ADAM_B2) * (gv * gv)
        nm_ref[...] = nm
        nv_ref[...] = nv
        d_ref[...] = -ADAM_LR * ((nm / c1) / (jnp.sqrt(nv / c2) + ADAM_EPS) + ADAM_WD * w_ref[...])

    row = pl.BlockSpec((tr, c), lambda i: (i, 0))
    return pl.pallas_call(
        body, name=name, grid=(r // tr,), in_specs=[row] * 4, out_specs=[row] * 3,
        out_shape=[jax.ShapeDtypeStruct((r, c), F32)] * 3, compiler_params=_cparams(("parallel",)),
    )(w, g, m, v)


IN_SPLIT = dict(z=(0, 1024), xbc=(1024, 3072), dt=(3072, 3200), glu=(3200, 5248), ql=(5248, 5632),
                ckv=(5632, 5888), kr=(5888, 6016), gate=(6016, 9088))


IN_WIDTH_PAD = 9216


def _w_in_pad(w):
    zeros = lambda n: jnp.zeros(w.shape[:-1] + (n,), w.dtype)
    return jnp.concatenate([w[..., :3088], zeros(112), w[..., 3088:5840], zeros(64), w[..., 5840:],
                            zeros(IN_WIDTH_PAD - 9088)], axis=-1)


def _w_in_unpad(g):
    return jnp.concatenate([g[..., :3088], g[..., 3200:5952], g[..., 6016:9088]], axis=-1)


def _qb_pad(w):
    lead = w.shape[:-1]
    w = w.reshape(lead + (MLA_HEADS, MLA_NOPE + MLA_ROPE))
    w = jnp.concatenate([w, jnp.zeros(lead + (MLA_HEADS, MLA_HP - MLA_NOPE - MLA_ROPE), w.dtype)], axis=-1)
    return w.reshape(lead + (MLA_HEADS * MLA_HP,))


def _qb_unpad(g):
    lead = g.shape[:-1]
    g = g.reshape(lead + (MLA_HEADS, MLA_HP))[..., :MLA_NOPE + MLA_ROPE]
    return g.reshape(lead + (MLA_HEADS * (MLA_NOPE + MLA_ROPE),))


def _pad_lanes(v, n):
    return jnp.concatenate([v, jnp.zeros((n - v.shape[0],), v.dtype)]).reshape(1, n)


def _layer_params(full, rep, l):
    p = {}
    w_in = full['w_in']
    for k, (a, b) in IN_SPLIT.items():
        p['w_' + k] = w_in[:, a:b]
    p['w_in'] = w_in
    for k in ('mla_w_q_b', 'mla_w_kv_b', 'xattn_w_kv', 'ffn_w_in', 'ssd_w_out', 'conv_w_out', 'mla_w_o', 'w_out',
              'xattn_w_q', 'xattn_w_o', 'ffn_w_out', 'ssd_conv_w', 'conv_dw_w'):
        p[k] = full[k]
    p['gate_b'] = full['gate_b'].reshape(1, -1)
    row = lambda name: rep[name][l].reshape(1, -1)
    for k in ('mix_norm_g', 'ssd_conv_b', 'ssd_norm_g', 'conv_dw_b', 'conv_ln_g', 'conv_ln_b', 'mla_q_a_g',
              'mla_kv_a_g', 'xattn_norm_g', 'mem_norm_g', 'xattn_q_norm_g', 'xattn_k_norm_g', 'ffn_norm_g'):
        p[k] = row(k)
    for k in ('ssd_dt_bias', 'ssd_a_log', 'ssd_d'):
        p[k] = _pad_lanes(rep[k][l], LANE)
        p[k + '_t'] = p[k].reshape(LANE, 1)
    p['gq'] = _pad_lanes(rep['mla_q_norm_g'][l], MLA_HP)
    p['gk'] = _pad_lanes(rep['mla_k_norm_g'][l], MLA_HP)
    return p


def _layer_fwd(x, mem, cs, sn, p, l):
    n = lambda s: f"l{l}_{s}"
    s_len, d = x.shape
    nc = s_len // CHUNK
    sv = {'x': x}
    u = _rms_fwd(x, p['mix_norm_g'], out_dtype=BF16, name=n("mix_norm"))
    z = _matmul(u, p['w_z'], name=n("in_z"))
    xbc = _matmul(u, p['w_xbc'], name=n("in_xbc"))
    dtr = _matmul(u, p['w_dt'], name=n("in_dt"))
    glu = _matmul(u, p['w_glu'], name=n("in_glu"))
    ql = _matmul(u, p['w_ql'], name=n("in_ql"))
    ckv = _matmul(u, p['w_ckv'], name=n("in_ckv"))
    krr = _matmul(u, p['w_kr'], name=n("in_kr"))
    gl = _matmul(u, p['w_gate'], name=n("in_gate"))
    pre_s, act_s = _conv_fwd(xbc, p['ssd_conv_w'], p['ssd_conv_b'], glu=False, act=True, name=n("ssd_conv"))
    dtrt = dtr.reshape(nc, CHUNK, LANE).transpose(0, 2, 1)
    y_scan, prev = _ssd_scan_fwd(act_s, dtr, dtrt, p['ssd_dt_bias'], p['ssd_dt_bias_t'], p['ssd_a_log'],
                                 p['ssd_a_log_t'], p['ssd_d'], name=n("ssd_scan"))
    yn = _ssd_gate_fwd(y_scan, z, p['ssd_norm_g'], name=n("ssd_gate"))
    y_ssd = _matmul(yn, p['ssd_w_out'], name=n("ssd_out"))
    pre_c, = _conv_fwd(glu, p['conv_dw_w'], p['conv_dw_b'], glu=True, act=False, name=n("dw_conv"))
    vc = _ln_silu_fwd(pre_c, p['conv_ln_g'], p['conv_ln_b'], name=n("conv_ln"))
    y_conv = _matmul(vc, p['conv_w_out'], name=n("conv_out"))
    qln = _rms_fwd(ql, p['mla_q_a_g'], out_dtype=BF16, name=n("q_a_norm"))
    q = _matmul(qln, p['mla_w_q_b'], name=n("q_b"))
    ckvn = _rms_fwd(ckv, p['mla_kv_a_g'], out_dtype=BF16, name=n("kv_a_norm"))
    kv = _matmul(ckvn, p['mla_w_kv_b'], name=n("kv_b"))
    qf, kf, v = _mla_prep_fwd(q, kv, krr, cs, sn, p['gq'], p['gk'], name=n("mla_prep"))
    o, lse = _mla_attn_fwd(qf, kf, v, name=n("mla_attn"))
    y_mla = _matmul(o, p['mla_w_o'], name=n("mla_out"))
    merged = _merge_fwd(gl, p['gate_b'], y_ssd, y_conv, y_mla, name=n("merge"))
    x1 = _matmul(merged, p['w_out'], add=x, name=n("mix_out"))
    hx = _rms_fwd(x1, p['xattn_norm_g'], out_dtype=BF16, name=n("xattn_norm"))
    qx = _matmul(hx, p['xattn_w_q'], name=n("xattn_q"))
    memn = _rms_fwd(mem, p['mem_norm_g'], out_dtype=BF16, name=n("mem_norm"))
    kvx = _matmul(memn, p['xattn_w_kv'], name=n("xattn_kv"))
    m_len = mem.shape[0]
    dh = d // X_HEADS
    kraw = kvx[:, :d].reshape(m_len * X_HEADS, dh)
    kx = _rms_fwd(kraw, p['xattn_k_norm_g'], out_dtype=BF16, name=n("xattn_k_norm")).reshape(m_len, d)
    vx = kvx[:, d:].astype(BF16)
    ox = _xattn_fwd(qx, kx, vx, p['xattn_q_norm_g'], name=n("xattn_core"))
    x2 = _matmul(ox, p['xattn_w_o'], add=x1, name=n("xattn_out"))
    hf = _rms_fwd(x2, p['ffn_norm_g'], out_dtype=BF16, name=n("ffn_norm"))
    h1 = _matmul(hf, p['ffn_w_in'], name=n("ffn_in"))
    act = _swiglu_fwd(h1, name=n("swiglu"))
    x3 = _matmul(act, p['ffn_w_out'], add=x2, name=n("ffn_out"))
    sv.update(u=u, z=z, xbc=xbc, dtr=dtr, dtrt=dtrt, glu=glu, ql=ql, ckv=ckv, krr=krr, gl=gl, pre_s=pre_s,
              act_s=act_s, y_scan=y_scan, prev=prev, yn=yn, y_ssd=y_ssd, pre_c=pre_c, vc=vc, y_conv=y_conv,
              qln=qln, q=q, ckvn=ckvn, kv=kv, qf=qf, kf=kf, v=v, o=o, lse=lse, y_mla=y_mla, merged=merged,
              x1=x1, hx=hx, qx=qx, memn=memn, kraw=kraw, kx=kx, vx=vx, ox=ox, x2=x2, hf=hf, h1=h1, act=act)
    return x3, sv


DW_KEY = dict(ffn_out_dw='ffn_w_out', ffn_in_dw='ffn_w_in', xattn_out_dw='xattn_w_o', xattn_q_dw='xattn_w_q',
              xattn_kv_dw='xattn_w_kv', mix_out_dw='w_out', mla_out_dw='mla_w_o', q_b_dw='mla_w_q_b',
              kv_b_dw='mla_w_kv_b', conv_out_dw='conv_w_out', ssd_out_dw='ssd_w_out', in_dw='w_in')


def _layer_bwd(dx3, mem, cs, sn, p, sv, l, stacks, depth):
    n = lambda s: f"l{l}_b_{s}"
    dw = lambda s: dict(name=n(s), out_dtype=BF16, into=(stacks.get(DW_KEY[s]), l, depth))
    g = {}
    d = dx3.shape[1]
    dact = _matmul(dx3, p['ffn_w_out'], tb=True, name=n("ffn_out_dx"))
    g['ffn_w_out'] = _matmul(sv['act'], dx3, ta=True, **dw("ffn_out_dw"))
    dh1 = _swiglu_bwd(sv['h1'], dact, name=n("swiglu"))
    g['ffn_w_in'] = _matmul(sv['hf'], dh1, ta=True, **dw("ffn_in_dw"))
    dhf = _matmul(dh1, p['ffn_w_in'], tb=True, name=n("ffn_in_dx"))
    dx2, g['ffn_norm_g'] = _rms_bwd(sv['x2'], p['ffn_norm_g'], dhf, dx_dtype=F32, add=dx3, name=n("ffn_norm"))
    dox = _matmul(dx2, p['xattn_w_o'], tb=True, name=n("xattn_out_dx"))
    g['xattn_w_o'] = _matmul(sv['ox'], dx2, ta=True, **dw("xattn_out_dw"))
    dqx, dkx, dvx, g['xattn_q_norm_g'] = _xattn_bwd(sv['qx'], sv['kx'], sv['vx'], p['xattn_q_norm_g'], dox,
                                                    name=n("xattn_core"))
    g['xattn_w_q'] = _matmul(sv['hx'], dqx, ta=True, **dw("xattn_q_dw"))
    dhx = _matmul(dqx, p['xattn_w_q'], tb=True, name=n("xattn_q_dx"))
    dx1, g['xattn_norm_g'] = _rms_bwd(sv['x1'], p['xattn_norm_g'], dhx, dx_dtype=F32, add=dx2, name=n("xattn_norm"))
    m_len = mem.shape[0]
    dh = d // X_HEADS
    dkraw, g['xattn_k_norm_g'] = _rms_bwd(sv['kraw'], p['xattn_k_norm_g'], dkx.reshape(m_len * X_HEADS, dh),
                                          dx_dtype=BF16, name=n("xattn_k_norm"))
    dkvx = jnp.concatenate([dkraw.reshape(m_len, d), dvx.astype(BF16)], axis=1)
    g['xattn_w_kv'] = _matmul(sv['memn'], dkvx, ta=True, **dw("xattn_kv_dw"))
    dmemn = _matmul(dkvx, p['xattn_w_kv'], tb=True, name=n("xattn_kv_dx"))
    _, g['mem_norm_g'] = _rms_bwd(mem, p['mem_norm_g'], dmemn, dx_dtype=BF16, name=n("mem_norm"))
    dmerged = _matmul(dx1, p['w_out'], tb=True, name=n("mix_out_dx"))
    g['w_out'] = _matmul(sv['merged'], dx1, ta=True, **dw("mix_out_dw"))
    dgl, g['gate_b'], dys, dyc, dym = _merge_bwd(sv['gl'], p['gate_b'], sv['y_ssd'], sv['y_conv'], sv['y_mla'],
                                                 dmerged, name=n("merge"))
    do = _matmul(dym, p['mla_w_o'], tb=True, name=n("mla_out_dx"))
    g['mla_w_o'] = _matmul(sv['o'], dym, ta=True, **dw("mla_out_dw"))
    dqf, dkf, dv = _mla_attn_bwd(sv['qf'], sv['kf'], sv['v'], sv['o'], sv['lse'], do, name=n("mla_attn"))
    dq, dkv, dkrr, g['gq'], g['gk'] = _mla_prep_bwd(sv['q'], sv['kv'], sv['krr'], cs, sn, p['gq'], p['gk'],
                                                    dqf, dkf, dv, name=n("mla_prep"))
    g['mla_w_q_b'] = _matmul(sv['qln'], dq, ta=True, **dw("q_b_dw"))
    dqln = _matmul(dq, p['mla_w_q_b'], tb=True, name=n("q_b_dx"))
    dql, g['mla_q_a_g'] = _rms_bwd(sv['ql'], p['mla_q_a_g'], dqln, dx_dtype=BF16, name=n("q_a_norm"))
    g['mla_w_kv_b'] = _matmul(sv['ckvn'], dkv, ta=True, **dw("kv_b_dw"))
    dckvn = _matmul(dkv, p['mla_w_kv_b'], tb=True, name=n("kv_b_dx"))
    dckv, g['mla_kv_a_g'] = _rms_bwd(sv['ckv'], p['mla_kv_a_g'], dckvn, dx_dtype=BF16, name=n("kv_a_norm"))
    dvc = _matmul(dyc, p['conv_w_out'], tb=True, name=n("conv_out_dx"))
    g['conv_w_out'] = _matmul(sv['vc'], dyc, ta=True, **dw("conv_out_dw"))
    dpre_c, g['conv_ln_g'], g['conv_ln_b'] = _ln_silu_bwd(sv['pre_c'], p['conv_ln_g'], p['conv_ln_b'], dvc,
                                                          name=n("conv_ln"))
    da, dg, g['conv_dw_w'], g['conv_dw_b'] = _conv_bwd(sv['glu'], p['conv_dw_w'], p['conv_dw_b'], dpre_c, None,
                                                       glu=True, act=False, name=n("dw_conv"))
    dyn = _matmul(dys, p['ssd_w_out'], tb=True, name=n("ssd_out_dx"))
    g['ssd_w_out'] = _matmul(sv['yn'], dys, ta=True, **dw("ssd_out_dw"))
    dy_scan, dz, g['ssd_norm_g'] = _ssd_gate_bwd(sv['y_scan'], sv['z'], p['ssd_norm_g'], dyn, name=n("ssd_gate"))
    dxs, db, dc, ddtr, g['ssd_a_log'], g['ssd_dt_bias'], g['ssd_d'] = _ssd_scan_bwd(
        sv['act_s'], sv['dtr'], sv['dtrt'], p['ssd_dt_bias'], p['ssd_dt_bias_t'], p['ssd_a_log'], p['ssd_a_log_t'],
        p['ssd_d'], sv['prev'], dy_scan, name=n("ssd_scan"))
    dact_s = jnp.concatenate([dxs, db, dc], axis=1)
    dxbc, g['ssd_conv_w'], g['ssd_conv_b'] = _conv_bwd(sv['xbc'], p['ssd_conv_w'], p['ssd_conv_b'], dact_s,
                                                       sv['pre_s'], glu=False, act=True, name=n("ssd_conv"))
    tail = jnp.zeros((dz.shape[0], IN_WIDTH_PAD - IN_SPLIT['gate'][1]), BF16)
    dproj = jnp.concatenate([dz, dxbc, ddtr, da, dg, dql, dckv, dkrr, dgl, tail], axis=1)
    g['w_in'] = _matmul(sv['u'], dproj, ta=True, **dw("in_dw"))
    du = _matmul(dproj, p['w_in'], tb=True, name=n("in_dx"))
    dx, g['mix_norm_g'] = _rms_bwd(sv['x'], p['mix_norm_g'], du, dx_dtype=F32, add=dx1, name=n("mix_norm"))
    return dx, g


REP_NAMES = ('mix_norm_g', 'ssd_conv_b', 'ssd_dt_bias', 'ssd_a_log', 'ssd_d', 'ssd_norm_g', 'conv_dw_b', 'conv_ln_g',
             'conv_ln_b', 'mla_q_a_g', 'mla_kv_a_g', 'mla_q_norm_g', 'mla_k_norm_g', 'xattn_norm_g', 'mem_norm_g',
             'xattn_q_norm_g', 'xattn_k_norm_g', 'ffn_norm_g')
BIG = (('w_in', 1, 1024, 8912), ('mla_w_q_b', 1, 384, 1536), ('mla_w_kv_b', 1, 256, 2048),
       ('xattn_w_kv', 1, 1024, 2048), ('ffn_w_in', 1, 1024, 5632), ('ssd_w_out', 0, 1024, 1024),
       ('conv_w_out', 0, 1024, 1024), ('mla_w_o', 0, 1024, 1024), ('w_out', 0, 1024, 1024),
       ('xattn_w_q', 0, 1024, 1024), ('xattn_w_o', 0, 1024, 1024), ('ffn_w_out', 0, 2816, 1024))
SMALL = (('ssd_conv_w', 1, 4, 2048), ('conv_dw_w', 1, 31, 1024), ('gate_b', 1, 3, 1024))


def _rope_tables(positions):
    half = MLA_ROPE // 2
    inv = ROPE_THETA ** (-jnp.arange(0, MLA_ROPE, 2, dtype=F32) / MLA_ROPE)
    ang = positions.astype(F32)[:, None] * inv
    cos, sin = jnp.cos(ang), jnp.sin(ang)
    z = jnp.zeros((positions.shape[0], LANE - 2 * half), F32)
    return jnp.concatenate([cos, cos, z], axis=1), jnp.concatenate([-sin, sin, z], axis=1)


def _local_step(x, mem, positions, target, layer_weights, rep, on_layer=None):
    depth = rep['mix_norm_g'].shape[0]
    cs, sn = _rope_tables(positions)
    params, saved = [], []
    h = x
    for l in range(depth):
        p = _layer_params(layer_weights(l, h), rep, l)
        h, sv = _layer_fwd(h, mem, cs, sn, p, l)
        params.append(p)
        saved.append(sv)
    dh, loss = _loss(h, target, name="loss")
    layer_grads = [None] * depth
    stacks = {}
    for l in reversed(range(depth)):
        dh, layer_grads[l] = _layer_bwd(dh, mem, cs, sn, params[l], saved[l], l, stacks, depth)
        stacks = {k: layer_grads[l][k] for k in DW_KEY.values()}
        if on_layer is not None:
            order = on_layer(l, stacks)
            if order is not None:
                dh = dh + order
    stack = lambda k: jnp.stack([layer_grads[l][k] for l in range(depth)])
    gfull = {k: stack(k) for k, _, _, _ in SMALL}
    gfull.update(stacks)
    gfull['w_in'] = _w_in_unpad(gfull['w_in'])
    gfull['mla_w_q_b'] = _qb_unpad(gfull['mla_w_q_b'])
    gfull['gate_b'] = gfull['gate_b'].reshape(depth, 3, -1)
    grep = {}
    for k in REP_NAMES:
        if k == 'mla_q_norm_g':
            grep[k] = stack('gq')[:, 0, :MLA_NOPE + MLA_ROPE]
        elif k == 'mla_k_norm_g':
            grep[k] = stack('gk')[:, 0, :MLA_NOPE + MLA_ROPE]
        elif k in ('ssd_dt_bias', 'ssd_a_log', 'ssd_d'):
            grep[k] = stack(k)[:, 0, :SSD_HEADS]
        else:
            grep[k] = stack(k)[:, 0, :]
    return loss, dh, gfull, grep


N_CHIPS = 4
HBM_SPEC = pl.BlockSpec(memory_space=pl.ANY)
ROW, COL, STK, REP = "row", "col", "stk", "rep"


def _kind(axis, cs):
    if axis == 0:
        return ROW
    return COL if cs % LANE == 0 else STK


def _mesh_pos():
    return lax.axis_index("x"), lax.axis_index("y"), lax.axis_index("c")


def _chip_view(ref, kind, j, a, b, layers=None):
    lsel = slice(None) if layers is None else pl.ds(layers[0], layers[1])
    if kind == STK:
        return ref.at[j, lsel]
    if kind == ROW:
        return ref.at[lsel, pl.ds(pl.multiple_of(j * a, 8), a), :]
    if kind == COL:
        return ref.at[lsel, :, pl.ds(pl.multiple_of(j * b, LANE), b)]
    return ref.at[lsel]


def _all_gather(shards, kinds, *, name):
    n = len(shards)
    depth = shards[0].shape[0]
    lh = depth // 2

    def out_shape(w, kind):
        _, a, b = w.shape
        full = {ROW: (depth, N_CHIPS * a, b), COL: (depth, a, N_CHIPS * b), STK: (N_CHIPS, depth, a, b)}[kind]
        return jax.ShapeDtypeStruct(full, w.dtype)

    def body(*refs):
        w_refs, out_refs = refs[:n], refs[n:2 * n]
        send_sems, recv_sems = refs[2 * n:]
        x, y, cc = _mesh_pos()
        me = (x, y, cc)
        sibling = (x, y, 1 - cc)
        chips = [(1 - x, y), (x, 1 - y), (1 - x, 1 - y)]
        slot = lambda chip: 2 * chip[0] + chip[1]

        def part(i, chip, hc):
            _, a, b = w_refs[i].shape
            return _chip_view(out_refs[i], kinds[i], slot(chip), a, b, (hc * lh, lh))

        def own(i):
            _, a, b = w_refs[i].shape
            return _chip_view(out_refs[i], kinds[i], slot((x, y)), a, b)

        def copy(i, k, src, dst, to):
            return pltpu.make_async_remote_copy(src_ref=src, dst_ref=dst, send_sem=send_sems.at[i, k],
                                                recv_sem=recv_sems.at[i, k], device_id=to, device_id_type=MESH)

        sends = []
        for i in range(n):
            my_half = w_refs[i].at[pl.ds(cc * lh, lh)]
            for k, chip in enumerate(chips):
                sends.append(copy(i, k, my_half, part(i, (x, y), cc), (*chip, cc)))
            sends.append(copy(i, 6, w_refs[i], own(i), sibling))
        for cp in sends:
            cp.start()
        for k, chip in enumerate(chips):
            for i in range(n):
                copy(i, k, part(i, chip, cc), part(i, chip, cc), me).wait_recv()
                cp = copy(i, 3 + k, part(i, chip, cc), part(i, chip, cc), sibling)
                cp.start()
                sends.append(cp)
        for i in range(n):
            for k, chip in enumerate(chips):
                copy(i, 3 + k, part(i, chip, 1 - cc), part(i, chip, 1 - cc), me).wait_recv()
            copy(i, 6, own(i), own(i), me).wait_recv()
        for cp in sends:
            cp.wait_send()

    return pl.pallas_call(
        body, name=name, in_specs=[HBM_SPEC] * n, out_specs=[HBM_SPEC] * n,
        out_shape=[out_shape(w, kd) for w, kd in zip(shards, kinds)],
        scratch_shapes=[pltpu.SemaphoreType.DMA((n, 7)), pltpu.SemaphoreType.DMA((n, 7))],
    )(*shards)


SEM_SPEC = pl.BlockSpec(memory_space=pltpu.SEMAPHORE)
SPLIT_COPY = pltpu.CompilerParams(has_side_effects=pltpu.SideEffectType.DATAFLOW_SIDE_EFFECTING)


def _land_view(ref, kind, j, a, b, rows=None):
    r0, nr = (0, a) if rows is None else rows
    if kind == STK:
        return ref.at[j, pl.ds(r0, nr), :]
    if kind == ROW:
        return ref.at[pl.ds(pl.multiple_of(j * a + r0, 8), nr), :]
    return ref.at[pl.ds(r0, nr), pl.ds(pl.multiple_of(j * b, LANE), b)]


def _layer_copies(w_refs, land_refs, kinds, layer, send_sems, recv_sems):
    x, y, cc = _mesh_pos()
    place = 2 * x + y
    chips = [(1 - x, y), (x, 1 - y), (1 - x, 1 - y)]
    out, inc = [], []
    for i, (w_ref, land, kind) in enumerate(zip(w_refs, land_refs, kinds)):
        _, a, b = w_ref.shape
        half = (cc * (a // 2), a // 2)
        desc = lambda k, src, dst, to: pltpu.make_async_remote_copy(
            src_ref=src, dst_ref=dst, send_sem=send_sems.at[4 * i + k], recv_sem=recv_sems.at[4 * i + k], device_id=to,
            device_id_type=MESH)
        for k, chip in enumerate(chips):
            out.append(desc(k, w_ref.at[layer, pl.ds(half[0], half[1]), :], _land_view(land, kind, place, a, b, half),
                            (*chip, cc)))
            theirs = _land_view(land, kind, 2 * chip[0] + chip[1], a, b, half)
            inc.append(desc(k, theirs, theirs, (*chip, cc)))
        out.append(desc(3, w_ref.at[layer], _land_view(land, kind, place, a, b), (x, y, 1 - cc)))
        own = _land_view(land, kind, place, a, b)
        inc.append(desc(3, own, own, (x, y, 1 - cc)))
    return out, inc


def _land_shape(w, kind):
    _, a, b = w.shape
    return {ROW: (N_CHIPS * a, b), COL: (a, N_CHIPS * b), STK: (N_CHIPS, a, b)}[kind]


def _ag_start(shards, kinds, layer, after, *, name):
    n = len(shards)
    lands = [pltpu.with_memory_space_constraint(lax.empty(_land_shape(w, kd), w.dtype), pltpu.HBM)
             for w, kd in zip(shards, kinds)]

    def body(*refs):
        w_refs = refs[:n]
        send_sems, recv_sems = refs[2 * n + 1], refs[2 * n + 2]
        land_refs = refs[2 * n + 3:3 * n + 3]
        token = refs[3 * n + 3]
        out, _ = _layer_copies(w_refs, land_refs, kinds, layer, send_sems, recv_sems)
        for cp in out:
            cp.start()
        token[...] = jnp.zeros_like(token)

    after = jnp.zeros((8, LANE), F32) if after is None else after
    res = pl.pallas_call(
        body, name=name,
        in_specs=[HBM_SPEC] * (2 * n + 1),
        out_specs=[SEM_SPEC, SEM_SPEC] + [HBM_SPEC] * n + [pl.BlockSpec(memory_space=pltpu.VMEM)],
        out_shape=[pltpu.SemaphoreType.DMA((4 * n,)), pltpu.SemaphoreType.DMA((4 * n,))]
        + [jax.ShapeDtypeStruct(ld.shape, ld.dtype) for ld in lands] + [jax.ShapeDtypeStruct((8, LANE), F32)],
        input_output_aliases={n + i: 2 + i for i in range(n)},
        compiler_params=SPLIT_COPY,
    )(*shards, *lands, after)
    return res[0], res[1], list(res[2:2 + n]), res[2 + n]


def _ag_wait(send_sems, recv_sems, shards, lands, kinds, layer, after, *, name):
    n = len(shards)

    def body(*refs):
        w_refs, land_in = refs[:n], refs[n:2 * n]
        send_sems, recv_sems = refs[2 * n], refs[2 * n + 1]
        out, inc = _layer_copies(w_refs, land_in, kinds, layer, send_sems, recv_sems)
        for cp in out:
            cp.wait_send()
        for cp in inc:
            cp.wait_recv()

    return list(pl.pallas_call(
        body, name=name,
        in_specs=[HBM_SPEC] * (2 * n) + [SEM_SPEC, SEM_SPEC, HBM_SPEC],
        out_specs=[HBM_SPEC] * n,
        out_shape=[jax.ShapeDtypeStruct(ld.shape, ld.dtype) for ld in lands],
        input_output_aliases={n + i: i for i in range(n)},
        compiler_params=SPLIT_COPY,
    )(*shards, *lands, send_sems, recv_sems, after))


def _ag_forward(lands, kinds, shard_shapes, *, name):
    n = len(lands)

    def body(*refs):
        land_refs = refs[n:2 * n]
        send_sems, recv_sems = refs[2 * n:]
        x, y, cc = _mesh_pos()
        chips = [(1 - x, y), (x, 1 - y), (1 - x, 1 - y)]
        sends, waits = [], []
        for i, (land, kind) in enumerate(zip(land_refs, kinds)):
            a, b = shard_shapes[i]
            for k, chip in enumerate(chips):
                j = 2 * chip[0] + chip[1]
                desc = lambda view: pltpu.make_async_remote_copy(
                    src_ref=view, dst_ref=view, send_sem=send_sems.at[i, k], recv_sem=recv_sems.at[i, k],
                    device_id=(x, y, 1 - cc), device_id_type=MESH)
                sends.append(desc(_land_view(land, kind, j, a, b, (cc * (a // 2), a // 2))))
                waits.append(desc(_land_view(land, kind, j, a, b, ((1 - cc) * (a // 2), a // 2))))
        for cp in sends:
            cp.start()
        for cp in waits:
            cp.wait_recv()
        for cp in sends:
            cp.wait_send()

    return list(pl.pallas_call(
        body, name=name, in_specs=[HBM_SPEC] * n, out_specs=[HBM_SPEC] * n,
        out_shape=[jax.ShapeDtypeStruct(ld.shape, ld.dtype) for ld in lands],
        input_output_aliases={i: i for i in range(n)},
        scratch_shapes=[pltpu.SemaphoreType.DMA((n, 3)), pltpu.SemaphoreType.DMA((n, 3))],
    )(*lands))


def _layers_half(ref, kind, start, lh):
    return ref.at[:, pl.ds(start, lh)] if kind == STK else ref.at[pl.ds(start, lh)]


def _rs_pair(gs, kinds, *, name):
    n = len(gs)

    def half_shape(g, kind):
        s = list(g.shape)
        s[1 if kind == STK else 0] //= 2
        return jax.ShapeDtypeStruct(tuple(s), g.dtype)

    def body(*refs):
        g_refs, buf_refs = refs[:n], refs[n:2 * n]
        send_sems, recv_sems = refs[2 * n:]
        x, y, cc = _mesh_pos()
        cps = []
        for i in range(n):
            lh = buf_refs[i].shape[1 if kinds[i] == STK else 0]
            cp = pltpu.make_async_remote_copy(src_ref=_layers_half(g_refs[i], kinds[i], (1 - cc) * lh, lh),
                                              dst_ref=buf_refs[i], send_sem=send_sems.at[i], recv_sem=recv_sems.at[i],
                                              device_id=(x, y, 1 - cc), device_id_type=MESH)
            cp.start()
            cps.append(cp)
        for cp in cps:
            cp.wait()

    return pl.pallas_call(
        body, name=name, in_specs=[HBM_SPEC] * n, out_specs=[HBM_SPEC] * n,
        out_shape=[half_shape(g, kd) for g, kd in zip(gs, kinds)],
        scratch_shapes=[pltpu.SemaphoreType.DMA((n,)), pltpu.SemaphoreType.DMA((n,))],
    )(*gs)


def _row_tile(rows, cols):
    return _pick(rows, max(8, (512 * 1024 // cols) // 8 * 8), 8)


def _rs_pair_add(g, buf, kind, cc, out_dtype, *, name):
    cols = g.shape[-1]
    pre = g.shape[0] if kind == STK else 1
    rows = buf.size // (pre * cols)
    tr = _row_tile(rows, cols)

    def body(cc_ref, g_ref, b_ref, o_ref):
        o_ref[...] = (g_ref[...].astype(F32) + b_ref[...].astype(F32)).astype(o_ref.dtype)

    out = pl.pallas_call(
        body, name=name,
        grid_spec=pltpu.PrefetchScalarGridSpec(
            num_scalar_prefetch=1, grid=(pre, rows // tr),
            in_specs=[pl.BlockSpec((None, None, tr, cols), lambda s, i, cc_ref: (s, cc_ref[0], i, 0)),
                      pl.BlockSpec((None, tr, cols), lambda s, i, cc_ref: (s, i, 0))],
            out_specs=pl.BlockSpec((None, tr, cols), lambda s, i, cc_ref: (s, i, 0))),
        out_shape=jax.ShapeDtypeStruct((pre, rows, cols), out_dtype),
        compiler_params=_cparams(("parallel", "parallel")),
    )(cc.reshape(1).astype(jnp.int32), g.reshape(pre, 2, rows, cols), buf.reshape(pre, rows, cols))
    return out.reshape(buf.shape)


def _rs_cross(ps, kinds, shard_shapes, *, name):
    n = len(ps)

    def body(*refs):
        p_refs, out_refs = refs[:n], refs[n:2 * n]
        send_sems, recv_sems, local_sems = refs[2 * n:]
        x, y, cc = _mesh_pos()
        chips = [(1 - x, y), (x, 1 - y), (1 - x, 1 - y)]
        slot = lambda chip: 2 * chip[0] + chip[1]
        local, sends = [], []
        for i in range(n):
            a, b = shard_shapes[i]
            if kinds[i] == REP:
                cp = pltpu.make_async_copy(p_refs[i], out_refs[i].at[slot((x, y))], local_sems.at[i])
                cp.start()
                local.append(cp)
            for k, chip in enumerate(chips):
                cp = pltpu.make_async_remote_copy(src_ref=_chip_view(p_refs[i], kinds[i], slot(chip), a, b),
                                                  dst_ref=out_refs[i].at[slot((x, y))], send_sem=send_sems.at[i, k],
                                                  recv_sem=recv_sems.at[i, k], device_id=(*chip, cc),
                                                  device_id_type=MESH)
                cp.start()
                sends.append(cp)
        for i in range(n):
            for k, chip in enumerate(chips):
                landed = out_refs[i].at[slot(chip)]
                pltpu.make_async_remote_copy(src_ref=landed, dst_ref=landed, send_sem=send_sems.at[i, k],
                                             recv_sem=recv_sems.at[i, k], device_id=(*chip, cc),
                                             device_id_type=MESH).wait_recv()
        for cp in sends:
            cp.wait_send()
        for cp in local:
            cp.wait()

    def out_shape(p, kind, ab):
        lh = p.shape[1 if kind == STK else 0]
        return jax.ShapeDtypeStruct((N_CHIPS, lh) + tuple(ab), p.dtype)

    return pl.pallas_call(
        body, name=name, in_specs=[HBM_SPEC] * n, out_specs=[HBM_SPEC] * n,
        out_shape=[out_shape(p, kd, ab) for p, kd, ab in zip(ps, kinds, shard_shapes)],
        scratch_shapes=[pltpu.SemaphoreType.DMA((n, 3)), pltpu.SemaphoreType.DMA((n, 3)),
                        pltpu.SemaphoreType.DMA((n,))],
    )(*ps)


def _cross_copies(p_refs, land_refs, kinds, shard_shapes, send_sems, recv_sems):
    x, y, cc = _mesh_pos()
    chips = [(1 - x, y), (x, 1 - y), (1 - x, 1 - y)]
    mine = 2 * x + y
    out, inc = [], []
    for i, (p_ref, land, kind) in enumerate(zip(p_refs, land_refs, kinds)):
        a, b = shard_shapes[i]
        for k, chip in enumerate(chips):
            j = 2 * chip[0] + chip[1]
            desc = lambda src, dst: pltpu.make_async_remote_copy(
                src_ref=src, dst_ref=dst, send_sem=send_sems.at[3 * i + k], recv_sem=recv_sems.at[3 * i + k],
                device_id=(*chip, cc), device_id_type=MESH)
            out.append(desc(_chip_view(p_ref, kind, j, a, b), land.at[mine]))
            inc.append(desc(land.at[j], land.at[j]))
    return out, inc


def _rs_cross_start(ps, kinds, shard_shapes, *, name):
    n = len(ps)
    lands = []
    for p, kd, ab in zip(ps, kinds, shard_shapes):
        lh = p.shape[1 if kd == STK else 0]
        lands.append(pltpu.with_memory_space_constraint(lax.empty((N_CHIPS, lh) + tuple(ab), p.dtype), pltpu.HBM))

    def body(*refs):
        p_refs = refs[:n]
        send_sems, recv_sems = refs[2 * n], refs[2 * n + 1]
        land_refs = refs[2 * n + 2:3 * n + 2]
        token = refs[3 * n + 2]
        out, _ = _cross_copies(p_refs, land_refs, kinds, shard_shapes, send_sems, recv_sems)
        for cp in out:
            cp.start()
        token[...] = jnp.zeros_like(token)

    res = pl.pallas_call(
        body, name=name, in_specs=[HBM_SPEC] * (2 * n),
        out_specs=[SEM_SPEC, SEM_SPEC] + [HBM_SPEC] * n + [pl.BlockSpec(memory_space=pltpu.VMEM)],
        out_shape=[pltpu.SemaphoreType.DMA((3 * n,)), pltpu.SemaphoreType.DMA((3 * n,))]
        + [jax.ShapeDtypeStruct(ld.shape, ld.dtype) for ld in lands] + [jax.ShapeDtypeStruct((8, LANE), F32)],
        input_output_aliases={n + i: 2 + i for i in range(n)}, compiler_params=SPLIT_COPY,
    )(*ps, *lands)
    return res[0], res[1], list(res[2:2 + n]), res[2 + n]


def _rs_cross_wait(send_sems, recv_sems, ps, lands, kinds, shard_shapes, after, *, name):
    n = len(ps)

    def body(*refs):
        p_refs, land_in = refs[:n], refs[n:2 * n]
        out, inc = _cross_copies(p_refs, land_in, kinds, shard_shapes, refs[2 * n], refs[2 * n + 1])
        for cp in out:
            cp.wait_send()
        for cp in inc:
            cp.wait_recv()

    return list(pl.pallas_call(
        body, name=name, in_specs=[HBM_SPEC] * (2 * n) + [SEM_SPEC, SEM_SPEC, HBM_SPEC], out_specs=[HBM_SPEC] * n,
        out_shape=[jax.ShapeDtypeStruct(ld.shape, ld.dtype) for ld in lands],
        input_output_aliases={n + i: i for i in range(n)}, compiler_params=SPLIT_COPY,
    )(*ps, *lands, send_sems, recv_sems, after))


def _rs_sum(p, landed, kind, ab, place, cc, into, *, name):
    a, b = ab
    lh = landed.shape[1]
    tr = _row_tile(a, b)
    blk = lambda f: pl.BlockSpec((None, None, tr, b), f)
    if kind == ROW:
        p_spec = pl.BlockSpec((None, tr, b), lambda l, i, w, c: (l, w[0] * (a // tr) + i, 0))
    elif kind == COL:
        p_spec = pl.BlockSpec((None, tr, b), lambda l, i, w, c: (l, i, w[0]))
    elif kind == STK:
        p_spec = blk(lambda l, i, w, c: (w[0], l, i, 0))
    else:
        p_spec = blk(lambda l, i, w, c: (0, l, i, 0))
        p = landed
    if kind == REP:
        others = [blk(lambda l, i, w, c, k=k: (k, l, i, 0)) for k in (1, 2, 3)]
    else:
        others = [blk(lambda l, i, w, c: (lax.rem(w[0] + 2, 4), l, i, 0)),
                  blk(lambda l, i, w, c: (w[0] + 1 - 2 * lax.rem(w[0], 2), l, i, 0)),
                  blk(lambda l, i, w, c: (3 - w[0], l, i, 0))]

    def body(w_ref, c_ref, p_ref, b1_ref, b2_ref, b3_ref, *rest):
        o_ref = rest[-1]
        f = lambda r: r[...].astype(F32)
        o_ref[...] = ((f(p_ref) + f(b1_ref)) + f(b2_ref)) + f(b3_ref)

    base, total, prev = into
    operands = [place, cc, p, landed, landed, landed] + ([] if prev is None else [prev])
    return pl.pallas_call(
        body, name=name,
        grid_spec=pltpu.PrefetchScalarGridSpec(
            num_scalar_prefetch=2, grid=(lh, a // tr),
            in_specs=[p_spec] + others + ([] if prev is None else [HBM_SPEC]),
            out_specs=pl.BlockSpec((None, tr, b), lambda l, i, w, c: (base + c[0] * lh + l, i, 0))),
        out_shape=jax.ShapeDtypeStruct((total, a, b), F32),
        input_output_aliases={} if prev is None else {6: 0},
        compiler_params=_cparams(("parallel", "parallel")),
    )(*operands)


def _rs_share(fs, spans, *, name):
    n = len(fs)

    def body(*refs):
        out_refs = refs[n:2 * n]
        send_sems, recv_sems = refs[2 * n:]
        x, y, cc = _mesh_pos()
        cps = []
        for i in range(n):
            base, lh = spans[i]
            mine = out_refs[i].at[pl.ds(base + cc * lh, lh)]
            cp = pltpu.make_async_remote_copy(src_ref=mine, dst_ref=mine, send_sem=send_sems.at[i],
                                              recv_sem=recv_sems.at[i], device_id=(x, y, 1 - cc),
                                              device_id_type=MESH)
            cp.start()
            cps.append(cp)
        for i in range(n):
            base, lh = spans[i]
            theirs = out_refs[i].at[pl.ds(base + (1 - cc) * lh, lh)]
            pltpu.make_async_remote_copy(src_ref=theirs, dst_ref=theirs, send_sem=send_sems.at[i],
                                         recv_sem=recv_sems.at[i], device_id=(x, y, 1 - cc),
                                         device_id_type=MESH).wait_recv()
        for cp in cps:
            cp.wait_send()

    return pl.pallas_call(
        body, name=name, in_specs=[HBM_SPEC] * n, out_specs=[HBM_SPEC] * n,
        out_shape=[jax.ShapeDtypeStruct(f.shape, f.dtype) for f in fs],
        input_output_aliases={i: i for i in range(n)},
        scratch_shapes=[pltpu.SemaphoreType.DMA((n,)), pltpu.SemaphoreType.DMA((n,))],
    )(*fs)


def _reduce_scatter(gs, kinds, shard_shapes, wire_dtypes, intos=None):
    ps = _rs_pair_sums(gs, kinds, wire_dtypes, "")
    landed = _rs_cross(ps, kinds, shard_shapes, name="rs_cross")
    return _rs_finish(ps, landed, kinds, shard_shapes, "", intos)


def _rs_pair_sums(gs, kinds, wire_dtypes, tag):
    cc = lax.axis_index("c")
    bufs = _rs_pair(gs, kinds, name="rs_pair" + tag)
    return [_rs_pair_add(g, buf, kd, cc, wd, name=f"rs_pair_add{tag}_{i}")
            for i, (g, buf, kd, wd) in enumerate(zip(gs, bufs, kinds, wire_dtypes))]


def _rs_finish(ps, landed, kinds, shard_shapes, tag, intos=None):
    x, y, cc = _mesh_pos()
    place = (2 * x + y).reshape(1).astype(jnp.int32)
    cc1 = cc.reshape(1).astype(jnp.int32)
    if intos is None:
        intos = [(0, 2 * b.shape[1], None) for b in landed]
    fs = [_rs_sum(p, b, kd, ab, place, cc1, into, name=f"rs_sum{tag}_{i}")
          for i, (p, b, kd, ab, into) in enumerate(zip(ps, landed, kinds, shard_shapes, intos))]
    return _rs_share(fs, [(into[0], b.shape[1]) for into, b in zip(intos, landed)], name="rs_share" + tag)


def _reduce_scatter_begin(gs, kinds, shard_shapes, wire_dtypes):
    ps = _rs_pair_sums(gs, kinds, wire_dtypes, "_early")
    send, recv, lands, token = _rs_cross_start(ps, kinds, shard_shapes, name="rs_cross_start")
    return ps, send, recv, lands, token


def _reduce_scatter_end(state, kinds, shard_shapes, after, intos):
    ps, send, recv, lands, _ = state
    landed = _rs_cross_wait(send, recv, ps, lands, kinds, shard_shapes, after, name="rs_cross_wait")
    return _rs_finish(ps, landed, kinds, shard_shapes, "_early", intos)


def _shard_shape(axis, r, c):
    return (r // N_CHIPS, c) if axis == 0 else (r, c // N_CHIPS)


def _unstack(stacked):
    ns, depth, r, cs = stacked.shape
    return stacked.transpose(1, 2, 0, 3).reshape(depth, r, ns * cs)


def _stack(fullw):
    depth, r, c = fullw.shape
    return fullw.reshape(depth, r, N_CHIPS, c // N_CHIPS).transpose(2, 0, 1, 3)


REP_SIZES = dict(mix_norm_g=1024, ssd_conv_b=2048, ssd_dt_bias=16, ssd_a_log=16, ssd_d=16, ssd_norm_g=1024,
                 conv_dw_b=1024, conv_ln_g=1024, conv_ln_b=1024, mla_q_a_g=384, mla_kv_a_g=256, mla_q_norm_g=192,
                 mla_k_norm_g=192, xattn_norm_g=1024, mem_norm_g=1024, xattn_q_norm_g=256, xattn_k_norm_g=256,
                 ffn_norm_g=1024)
REP_ROWS = 8
REP_COLS = -(-sum(REP_SIZES.values()) // (REP_ROWS * LANE)) * LANE
SMALL_COLS = 256
SMALL_ROWS = 48


def _pack_rep(d):
    flat = jnp.concatenate([d[k] for k in REP_NAMES], axis=1)
    flat = jnp.pad(flat, ((0, 0), (0, REP_ROWS * REP_COLS - flat.shape[1])))
    return flat.reshape(flat.shape[0], REP_ROWS, REP_COLS)


def _unpack_rep(packed):
    out, off = {}, 0
    flat = packed.reshape(packed.shape[0], -1)
    for k in REP_NAMES:
        out[k] = flat[:, off:off + REP_SIZES[k]]
        off += REP_SIZES[k]
    return out


def _pack_small(d):
    depth = d['gate_b'].shape[0]
    rows = jnp.concatenate([d[k].reshape(depth, -1, SMALL_COLS) for k, _, _, _ in SMALL], axis=1)
    return jnp.pad(rows, ((0, 0), (0, SMALL_ROWS - rows.shape[1]), (0, 0)))


def _unpack_small(packed):
    out, off = {}, 0
    for k, axis, r, c in SMALL:
        rs, cs = _shard_shape(axis, r, c)
        n = rs * cs // SMALL_COLS
        out[k] = packed[:, off:off + n].reshape(packed.shape[0], rs, cs)
        off += n
    return out


WEIGHT_NAMES = ('mix_norm_g', 'w_in', 'ssd_conv_w', 'ssd_conv_b', 'ssd_dt_bias', 'ssd_a_log', 'ssd_d', 'ssd_norm_g',
                'ssd_w_out', 'conv_dw_w', 'conv_dw_b', 'conv_ln_g', 'conv_ln_b', 'conv_w_out', 'mla_q_a_g',
                'mla_w_q_b', 'mla_kv_a_g', 'mla_w_kv_b', 'mla_q_norm_g', 'mla_k_norm_g', 'mla_w_o', 'gate_b', 'w_out',
                'xattn_norm_g', 'mem_norm_g', 'xattn_w_q', 'xattn_w_kv', 'xattn_q_norm_g', 'xattn_k_norm_g',
                'xattn_w_o', 'ffn_norm_g', 'ffn_w_in', 'ffn_w_out')


def kernel(x, mem, positions, *rest):
    nw = len(WEIGHT_NAMES)
    weights = dict(zip(WEIGHT_NAMES, rest[:nw]))
    target = rest[nw]
    mom_m = dict(zip(WEIGHT_NAMES, rest[nw + 1:2 * nw + 1]))
    mom_v = dict(zip(WEIGHT_NAMES, rest[2 * nw + 1:3 * nw + 1]))
    depth = weights['mix_norm_g'].shape[0]

    kinds = [_kind(axis, _shard_shape(axis, r, c)[1]) for _, axis, r, c in BIG]
    shard_shapes = [_shard_shape(axis, r, c) for _, axis, r, c in BIG]
    nb = len(BIG)
    small_all, = _all_gather([_pack_small(weights)], [STK], name="ag_small")
    small_chips = [_unpack_small(small_all[j]) for j in range(N_CHIPS)]
    small = {k: jnp.concatenate([sc[k] for sc in small_chips], axis=2) for k, _, _, _ in SMALL}
    big_shards = [weights[k].astype(BF16) for k, _, _, _ in BIG]
    send0, recv0, lands0, _ = _ag_start(big_shards, kinds[:nb], 0, None, name="ag_start_0")
    lands0 = _ag_wait(send0, recv0, big_shards, lands0, kinds[:nb], 0, big_shards[0], name="ag_wait_0")
    lands0 = _ag_forward(lands0, kinds[:nb], shard_shapes[:nb], name="ag_forward_0")
    started = {l: _ag_start(big_shards, kinds[:nb], l, lands0[0], name=f"ag_start_{l}") for l in range(1, depth)}
    tokens = sum(st[3][0, 0] for st in started.values())

    def layer_weights(l, h):
        if l == 0:
            lands = lands0
        else:
            send, recv, lands, _ = started[l]
            lands = _ag_wait(send, recv, big_shards, lands, kinds[:nb], l, h, name=f"ag_wait_{l}")
            lands = _ag_forward(lands, kinds[:nb], shard_shapes[:nb], name=f"ag_forward_{l}")
        full = {k: (ld.transpose(1, 0, 2).reshape(ld.shape[1], -1) if kd == STK else ld)
                for (k, _, _, _), kd, ld in zip(BIG, kinds[:nb], lands)}
        full.update({k: w[l] for k, w in small.items()})
        full['w_in'] = _w_in_pad(full['w_in'])
        full['mla_w_q_b'] = _qb_pad(full['mla_w_q_b'])
        return full

    rep = {k: weights[k] for k in REP_NAMES}
    top = depth // 2
    by_chip = lambda g: [(_stack(g[k]) if kd == STK else g[k]) for (k, _, _, _), kd in zip(BIG, kinds)]
    early = []

    def on_layer(l, stacks):
        if l == top:
            g = {k: stacks[k][top:] for k, _, _, _ in BIG}
            g['w_in'] = _w_in_unpad(g['w_in'])
            g['mla_w_q_b'] = _qb_unpad(g['mla_w_q_b'])
            early.append(_reduce_scatter_begin(by_chip(g), kinds, shard_shapes, [BF16] * nb))
            return early[0][4][0, 0]

    loss, dx, gfull, grep = _local_step(x[0] + tokens, mem[0], positions[0], target[0], layer_weights, rep, on_layer)
    loss = lax.psum(loss, ("x", "y", "c"))
    upper = _reduce_scatter_end(early[0], kinds, shard_shapes, dx, [(top, depth, None)] * nb)

    gs = by_chip({k: gfull[k][:top] for k, _, _, _ in BIG})
    small_split = {k: _stack(gfull[k]) for k, _, _, _ in SMALL}
    gs.append(jnp.stack([_pack_small({k: v[j] for k, v in small_split.items()}) for j in range(N_CHIPS)]))
    gs.append(_pack_rep(grep))
    summed = _reduce_scatter(gs, kinds + [STK, REP], shard_shapes + [(SMALL_ROWS, SMALL_COLS), (REP_ROWS, REP_COLS)],
                             [BF16] * nb + [F32, F32],
                             [(0, depth, up) for up in upper] + [(0, depth, None)] * 2)
    grads = {k: g for (k, _, _, _), g in zip(BIG, summed[:nb])}
    grads.update(_unpack_small(summed[nb]))
    rep_sum = summed[nb + 1]
    grads.update(_unpack_rep(rep_sum))

    delta, new_m, new_v = {}, {}, {}
    for k, _, _, _ in BIG + SMALL:
        w = weights[k]
        two_d = (w.shape[0] * w.shape[1], w.shape[2])
        d_, m_, v_ = _adamw(w.reshape(two_d), grads[k].reshape(two_d), mom_m[k].reshape(two_d),
                            mom_v[k].reshape(two_d), name="adamw_" + k)
        delta[k], new_m[k], new_v[k] = d_.reshape(w.shape), m_.reshape(w.shape), v_.reshape(w.shape)
    pack2 = lambda d: _pack_rep(d).reshape(depth, -1)
    d_, m_, v_ = _adamw(pack2(rep), rep_sum.reshape(depth, -1), pack2({k: mom_m[k] for k in REP_NAMES}),
                        pack2({k: mom_v[k] for k in REP_NAMES}), name="adamw_rep")
    delta.update(_unpack_rep(d_))
    new_m.update(_unpack_rep(m_))
    new_v.update(_unpack_rep(v_))

    return (loss, dx[None], *[grads[k] for k in WEIGHT_NAMES], *[delta[k] for k in WEIGHT_NAMES],
            *[new_m[k] for k in WEIGHT_NAMES], *[new_v[k] for k in WEIGHT_NAMES])
```
